```python
import jax, jax.numpy as jnp
from jax import lax
import numpy as np

D_MODEL = 2048
BATCH = 8
SEQ = 8192
DEPTH = 2

N_MIXERS = 2
CONV_WIDTH = 3
CHUNK = 128
SG_GROUPS = 8
SG_WIDTH = D_MODEL
D_FF = 5632
RMS_EPS = 1e-5
N_A = (DEPTH + 1) // 2
N_B = DEPTH // 2

kernel_name = "hybrid_shortconv_spatialgate_convffn"


def rmsnorm(x, g):
    xf = x.astype(jnp.float32)
    inv = lax.rsqrt(jnp.mean(xf * xf, axis=-1, keepdims=True) + RMS_EPS)
    return (xf * inv).astype(x.dtype) * g


def causal_dwconv3(x, w):
    s = x.shape[1]
    xp = jnp.pad(x, ((0, 0), (CONV_WIDTH - 1, 0), (0, 0)))
    return xp[:, :s] * w[0] + xp[:, 1:s + 1] * w[1] + xp[:, 2:s + 2] * w[2]


def short_conv_mixer(h, w_in, w_conv, w_out):
    bcx = jnp.einsum('bsd,de->bse', h, w_in)
    gb, gc, xs = jnp.split(bcx, 3, axis=-1)
    y = gb * causal_dwconv3(gc * xs, w_conv)
    return jnp.einsum('bsd,de->bse', y, w_out)


def spatial_gating_mixer(h, w_in, v_norm, w_s, b_s, w_out):
    bsz, s, _ = h.shape
    z = jax.nn.gelu(jnp.einsum('bsd,de->bse', h, w_in))
    u, v = jnp.split(z, 2, axis=-1)
    v = rmsnorm(v, v_norm)
    n_chunks = s // CHUNK
    vr = v.reshape(bsz, n_chunks, CHUNK, SG_GROUPS, SG_WIDTH // SG_GROUPS)
    mask = jnp.tril(jnp.ones((CHUNK, CHUNK), dtype=w_s.dtype))
    ws = w_s * mask
    mixed = jnp.einsum('hts,bnshc->bnthc', ws, vr) + b_s.T[None, None, :, :, None]
    gate = mixed.reshape(bsz, s, SG_WIDTH)
    return jnp.einsum('bsd,de->bse', u * gate, w_out)


def conv_ffn(h, w_up, conv_w, conv_b, w_down):
    up = jnp.einsum('bsd,df->bsf', h, w_up)
    up = causal_dwconv3(up, conv_w) + conv_b
    g, a = jnp.split(up, 2, axis=-1)
    return jnp.einsum('bsf,fd->bsd', jax.nn.silu(g) * a, w_down)


def _fwd_setup_inputs(seed: int = 0) -> dict:
    key = jax.random.key(seed)
    ks = jax.random.split(key, 20)
    f32 = jnp.float32
    D = D_MODEL
    def nrm(k, shape, scale):
        return jax.random.normal(k, shape, f32) * scale
    def gain(k, shape):
        return 1.0 + 0.02 * jax.random.normal(k, shape, f32)
    return {
        "x": nrm(ks[0], (BATCH, SEQ, D), 1.0),
        "a_norm": gain(ks[1], (N_A, D)),
        "a_in": nrm(ks[2], (N_A, D, 3 * D), D ** -0.5),
        "a_conv": nrm(ks[3], (N_A, CONV_WIDTH, D), CONV_WIDTH ** -0.5),
        "a_out": nrm(ks[4], (N_A, D, D), D ** -0.5),
        "b_norm": gain(ks[5], (N_B, D)),
        "b_in": nrm(ks[6], (N_B, D, 2 * SG_WIDTH), D ** -0.5),
        "b_vnorm": gain(ks[7], (N_B, SG_WIDTH)),
        "b_ws": nrm(ks[8], (N_B, SG_GROUPS, CHUNK, CHUNK), CHUNK ** -0.5),
        "b_bs": gain(ks[9], (N_B, SG_GROUPS, CHUNK)),
        "b_out": nrm(ks[10], (N_B, SG_WIDTH, D), SG_WIDTH ** -0.5),
        "f_norm": gain(ks[11], (DEPTH, D)),
        "f_up": nrm(ks[12], (DEPTH, D, 2 * D_FF), D ** -0.5),
        "f_conv_w": nrm(ks[13], (DEPTH, CONV_WIDTH, 2 * D_FF), CONV_WIDTH ** -0.5),
        "f_conv_b": nrm(ks[14], (DEPTH, 2 * D_FF), 0.01),
        "f_down": nrm(ks[15], (DEPTH, D_FF, D), D_FF ** -0.5),
        "final_norm": gain(ks[16], (D,)),
    }


def _fwd_reference(x, a_norm, a_in, a_conv, a_out, b_norm, b_in, b_vnorm, b_ws, b_bs, b_out,
              f_norm, f_up, f_conv_w, f_conv_b, f_down, final_norm):
    for i in range(DEPTH):
        j = i // N_MIXERS
        if i % N_MIXERS == 0:
            h = rmsnorm(x, a_norm[j])
            x = x + short_conv_mixer(h, a_in[j], a_conv[j], a_out[j])
        else:
            h = rmsnorm(x, b_norm[j])
            x = x + spatial_gating_mixer(h, b_in[j], b_vnorm[j], b_ws[j], b_bs[j], b_out[j])
        h = rmsnorm(x, f_norm[i])
        x = x + conv_ffn(h, f_up[i], f_conv_w[i], f_conv_b[i], f_down[i])
    return rmsnorm(x, final_norm)


import jax as _jax
import jax.numpy as _jnp

TWIN_FORMAT = 'train_step'
FWD_PARAMS = ['x', 'a_norm', 'a_in', 'a_conv', 'a_out', 'b_norm', 'b_in', 'b_vnorm', 'b_ws', 'b_bs', 'b_out', 'f_norm', 'f_up', 'f_conv_w', 'f_conv_b', 'f_down', 'final_norm']
TWIN_WEIGHTS = ['a_norm', 'a_in', 'a_conv', 'a_out', 'b_norm', 'b_in', 'b_vnorm', 'b_ws', 'b_bs', 'b_out', 'f_norm', 'f_up', 'f_conv_w', 'f_conv_b', 'f_down', 'final_norm']
TWIN_DIFF_INPUT = 'x'
TWIN_INPUTS = ['x', 'a_norm', 'a_in', 'a_conv', 'a_out', 'b_norm', 'b_in', 'b_vnorm', 'b_ws', 'b_bs', 'b_out', 'f_norm', 'f_up', 'f_conv_w', 'f_conv_b', 'f_down', 'final_norm', 'loss_target', 'm_a_norm', 'm_a_in', 'm_a_conv', 'm_a_out', 'm_b_norm', 'm_b_in', 'm_b_vnorm', 'm_b_ws', 'm_b_bs', 'm_b_out', 'm_f_norm', 'm_f_up', 'm_f_conv_w', 'm_f_conv_b', 'm_f_down', 'm_final_norm', 'v_a_norm', 'v_a_in', 'v_a_conv', 'v_a_out', 'v_b_norm', 'v_b_in', 'v_b_vnorm', 'v_b_ws', 'v_b_bs', 'v_b_out', 'v_f_norm', 'v_f_up', 'v_f_conv_w', 'v_f_conv_b', 'v_f_down', 'v_final_norm']
TWIN_OUTPUTS = ['loss', 'grad_x', 'grad_a_norm', 'grad_a_in', 'grad_a_conv', 'grad_a_out', 'grad_b_norm', 'grad_b_in', 'grad_b_vnorm', 'grad_b_ws', 'grad_b_bs', 'grad_b_out', 'grad_f_norm', 'grad_f_up', 'grad_f_conv_w', 'grad_f_conv_b', 'grad_f_down', 'grad_final_norm', 'delta_a_norm', 'delta_a_in', 'delta_a_conv', 'delta_a_out', 'delta_b_norm', 'delta_b_in', 'delta_b_vnorm', 'delta_b_ws', 'delta_b_bs', 'delta_b_out', 'delta_f_norm', 'delta_f_up', 'delta_f_conv_w', 'delta_f_conv_b', 'delta_f_down', 'delta_final_norm', 'new_m_a_norm', 'new_m_a_in', 'new_m_a_conv', 'new_m_a_out', 'new_m_b_norm', 'new_m_b_in', 'new_m_b_vnorm', 'new_m_b_ws', 'new_m_b_bs', 'new_m_b_out', 'new_m_f_norm', 'new_m_f_up', 'new_m_f_conv_w', 'new_m_f_conv_b', 'new_m_f_down', 'new_m_final_norm', 'new_v_a_norm', 'new_v_a_in', 'new_v_a_conv', 'new_v_a_out', 'new_v_b_norm', 'new_v_b_in', 'new_v_b_vnorm', 'new_v_b_ws', 'new_v_b_bs', 'new_v_b_out', 'new_v_f_norm', 'new_v_f_up', 'new_v_f_conv_w', 'new_v_f_conv_b', 'new_v_f_down', 'new_v_final_norm']
TWIN_LEAF_KINDS = {'loss': 'loss', 'grad_x': 'grad_x', 'grad_a_norm': 'grad_w', 'grad_a_in': 'grad_w', 'grad_a_conv': 'grad_w', 'grad_a_out': 'grad_w', 'grad_b_norm': 'grad_w', 'grad_b_in': 'grad_w', 'grad_b_vnorm': 'grad_w', 'grad_b_ws': 'grad_w', 'grad_b_bs': 'grad_w', 'grad_b_out': 'grad_w', 'grad_f_norm': 'grad_w', 'grad_f_up': 'grad_w', 'grad_f_conv_w': 'grad_w', 'grad_f_conv_b': 'grad_w', 'grad_f_down': 'grad_w', 'grad_final_norm': 'grad_w', 'delta_a_norm': 'delta_w', 'delta_a_in': 'delta_w', 'delta_a_conv': 'delta_w', 'delta_a_out': 'delta_w', 'delta_b_norm': 'delta_w', 'delta_b_in': 'delta_w', 'delta_b_vnorm': 'delta_w', 'delta_b_ws': 'delta_w', 'delta_b_bs': 'delta_w', 'delta_b_out': 'delta_w', 'delta_f_norm': 'delta_w', 'delta_f_up': 'delta_w', 'delta_f_conv_w': 'delta_w', 'delta_f_conv_b': 'delta_w', 'delta_f_down': 'delta_w', 'delta_final_norm': 'delta_w', 'new_m_a_norm': 'new_m', 'new_m_a_in': 'new_m', 'new_m_a_conv': 'new_m', 'new_m_a_out': 'new_m', 'new_m_b_norm': 'new_m', 'new_m_b_in': 'new_m', 'new_m_b_vnorm': 'new_m', 'new_m_b_ws': 'new_m', 'new_m_b_bs': 'new_m', 'new_m_b_out': 'new_m', 'new_m_f_norm': 'new_m', 'new_m_f_up': 'new_m', 'new_m_f_conv_w': 'new_m', 'new_m_f_conv_b': 'new_m', 'new_m_f_down': 'new_m', 'new_m_final_norm': 'new_m', 'new_v_a_norm': 'new_v', 'new_v_a_in': 'new_v', 'new_v_a_conv': 'new_v', 'new_v_a_out': 'new_v', 'new_v_b_norm': 'new_v', 'new_v_b_in': 'new_v', 'new_v_b_vnorm': 'new_v', 'new_v_b_ws': 'new_v', 'new_v_b_bs': 'new_v', 'new_v_b_out': 'new_v', 'new_v_f_norm': 'new_v', 'new_v_f_up': 'new_v', 'new_v_f_conv_w': 'new_v', 'new_v_f_conv_b': 'new_v', 'new_v_f_down': 'new_v', 'new_v_final_norm': 'new_v'}


def _forward(args):
    return _fwd_reference(*[args[k] for k in FWD_PARAMS])


def _output_shape():
    def fwd():
        inp = _fwd_setup_inputs(0)
        return _fwd_reference(*[inp[k] for k in FWD_PARAMS])
    out = _jax.eval_shape(fwd)
    return out.shape, out.dtype

N_MICROBATCH = 1
ADAM_LR = 0.001
ADAM_B1 = 0.9
ADAM_B2 = 0.999
ADAM_EPS = 1e-08
ADAM_WD = 0.01
ADAM_STEP = 10
PER_EXAMPLE_BATCH_AXIS = {'x': 0, 'loss_target': 0}
SHARED_INPUTS = []
_WEIGHT_DTYPES = {'a_norm': _jnp.float32, 'a_in': _jnp.float32, 'a_conv': _jnp.float32, 'a_out': _jnp.float32, 'b_norm': _jnp.float32, 'b_in': _jnp.float32, 'b_vnorm': _jnp.float32, 'b_ws': _jnp.float32, 'b_bs': _jnp.float32, 'b_out': _jnp.float32, 'f_norm': _jnp.float32, 'f_up': _jnp.float32, 'f_conv_w': _jnp.float32, 'f_conv_b': _jnp.float32, 'f_down': _jnp.float32, 'final_norm': _jnp.float32}
MOMENT_SCALE = {'a_norm': 1.869250e-01, 'a_in': 1.059812e-01, 'a_conv': 1.065389e-01, 'a_out': 1.060739e-01, 'b_norm': 7.457994e-02, 'b_in': 5.112744e-02, 'b_vnorm': 3.659293e-02, 'b_ws': 4.930303e-02, 'b_bs': 7.668568e-02, 'b_out': 6.299882e-02, 'f_norm': 7.224879e-02, 'f_up': 3.021750e-02, 'f_conv_w': 3.017411e-02, 'f_conv_b': 2.954447e-02, 'f_down': 4.939856e-02, 'final_norm': 3.195564e+01}


def _to_microbatches(a, axis):
    t = _jnp.moveaxis(a, axis, 0)
    t = t.reshape((N_MICROBATCH, t.shape[0] // N_MICROBATCH) + t.shape[1:])
    return _jnp.moveaxis(t, 1, axis + 1)


def setup_inputs(seed: int = 0) -> dict:
    inp = _fwd_setup_inputs(seed)
    key = _jax.random.fold_in(_jax.random.key(seed), 7919)
    shape, _ = _output_shape()
    out = dict(inp)
    out["loss_target"] = _jax.random.normal(_jax.random.fold_in(key, 0), shape, _jnp.float32)
    for i, name in enumerate(TWIN_WEIGHTS):
        w = inp[name].astype(_jnp.float32)
        if MOMENT_SCALE is None:
            s = _jnp.sqrt(_jnp.mean(_jnp.square(w)) + 1e-30)
        else:
            s = MOMENT_SCALE[name]
        km, kv = _jax.random.split(_jax.random.fold_in(key, i + 1))
        out[name] = w
        out["m_" + name] = s * _jax.random.normal(km, w.shape, _jnp.float32)
        out["v_" + name] = (s * s) * _jax.random.uniform(kv, w.shape, _jnp.float32, 0.5, 1.5)
    if N_MICROBATCH > 1:
        for name, axis in PER_EXAMPLE_BATCH_AXIS.items():
            out[name] = _to_microbatches(out[name], axis)
    return {'x': out['x'], 'a_norm': out['a_norm'], 'a_in': out['a_in'], 'a_conv': out['a_conv'], 'a_out': out['a_out'], 'b_norm': out['b_norm'], 'b_in': out['b_in'], 'b_vnorm': out['b_vnorm'], 'b_ws': out['b_ws'], 'b_bs': out['b_bs'], 'b_out': out['b_out'], 'f_norm': out['f_norm'], 'f_up': out['f_up'], 'f_conv_w': out['f_conv_w'], 'f_conv_b': out['f_conv_b'], 'f_down': out['f_down'], 'final_norm': out['final_norm'], 'loss_target': out['loss_target'], 'm_a_norm': out['m_a_norm'], 'm_a_in': out['m_a_in'], 'm_a_conv': out['m_a_conv'], 'm_a_out': out['m_a_out'], 'm_b_norm': out['m_b_norm'], 'm_b_in': out['m_b_in'], 'm_b_vnorm': out['m_b_vnorm'], 'm_b_ws': out['m_b_ws'], 'm_b_bs': out['m_b_bs'], 'm_b_out': out['m_b_out'], 'm_f_norm': out['m_f_norm'], 'm_f_up': out['m_f_up'], 'm_f_conv_w': out['m_f_conv_w'], 'm_f_conv_b': out['m_f_conv_b'], 'm_f_down': out['m_f_down'], 'm_final_norm': out['m_final_norm'], 'v_a_norm': out['v_a_norm'], 'v_a_in': out['v_a_in'], 'v_a_conv': out['v_a_conv'], 'v_a_out': out['v_a_out'], 'v_b_norm': out['v_b_norm'], 'v_b_in': out['v_b_in'], 'v_b_vnorm': out['v_b_vnorm'], 'v_b_ws': out['v_b_ws'], 'v_b_bs': out['v_b_bs'], 'v_b_out': out['v_b_out'], 'v_f_norm': out['v_f_norm'], 'v_f_up': out['v_f_up'], 'v_f_conv_w': out['v_f_conv_w'], 'v_f_conv_b': out['v_f_conv_b'], 'v_f_down': out['v_f_down'], 'v_final_norm': out['v_final_norm']}


def _loss(weights, diff, rest, loss_target):
    with _jax.named_scope("forward"):
        args = {**rest, TWIN_DIFF_INPUT: diff, **{k: w.astype(_WEIGHT_DTYPES[k]) for k, w in weights.items()}}
        y = _forward(args)
    with _jax.named_scope("loss_head"):
        err = _jnp.square(y.astype(_jnp.float32) - loss_target)
        return 0.5 * _jnp.sum(_jnp.mean(err, axis=-1)) if err.ndim else 0.5 * err


def _adamw(w, g, m, v):
    m = ADAM_B1 * m + (1.0 - ADAM_B1) * g
    v = ADAM_B2 * v + (1.0 - ADAM_B2) * _jnp.square(g)
    m_hat = m / (1.0 - ADAM_B1 ** ADAM_STEP)
    v_hat = v / (1.0 - ADAM_B2 ** ADAM_STEP)
    delta = -ADAM_LR * (m_hat / (_jnp.sqrt(v_hat) + ADAM_EPS) + ADAM_WD * w)
    return delta, m, v


def reference(x, a_norm, a_in, a_conv, a_out, b_norm, b_in, b_vnorm, b_ws, b_bs, b_out, f_norm, f_up, f_conv_w, f_conv_b, f_down, final_norm, loss_target, m_a_norm, m_a_in, m_a_conv, m_a_out, m_b_norm, m_b_in, m_b_vnorm, m_b_ws, m_b_bs, m_b_out, m_f_norm, m_f_up, m_f_conv_w, m_f_conv_b, m_f_down, m_final_norm, v_a_norm, v_a_in, v_a_conv, v_a_out, v_b_norm, v_b_in, v_b_vnorm, v_b_ws, v_b_bs, v_b_out, v_f_norm, v_f_up, v_f_conv_w, v_f_conv_b, v_f_down, v_final_norm):
    given = dict(x=x, a_norm=a_norm, a_in=a_in, a_conv=a_conv, a_out=a_out, b_norm=b_norm, b_in=b_in, b_vnorm=b_vnorm, b_ws=b_ws, b_bs=b_bs, b_out=b_out, f_norm=f_norm, f_up=f_up, f_conv_w=f_conv_w, f_conv_b=f_conv_b, f_down=f_down, final_norm=final_norm, loss_target=loss_target, m_a_norm=m_a_norm, m_a_in=m_a_in, m_a_conv=m_a_conv, m_a_out=m_a_out, m_b_norm=m_b_norm, m_b_in=m_b_in, m_b_vnorm=m_b_vnorm, m_b_ws=m_b_ws, m_b_bs=m_b_bs, m_b_out=m_b_out, m_f_norm=m_f_norm, m_f_up=m_f_up, m_f_conv_w=m_f_conv_w, m_f_conv_b=m_f_conv_b, m_f_down=m_f_down, m_final_norm=m_final_norm, v_a_norm=v_a_norm, v_a_in=v_a_in, v_a_conv=v_a_conv, v_a_out=v_a_out, v_b_norm=v_b_norm, v_b_in=v_b_in, v_b_vnorm=v_b_vnorm, v_b_ws=v_b_ws, v_b_bs=v_b_bs, v_b_out=v_b_out, v_f_norm=v_f_norm, v_f_up=v_f_up, v_f_conv_w=v_f_conv_w, v_f_conv_b=v_f_conv_b, v_f_down=v_f_down, v_final_norm=v_final_norm)
    weights = {n: given[n] for n in TWIN_WEIGHTS}
    shared = {n: given[n] for n in SHARED_INPUTS}
    per_example = {n: given[n] for n in ['x']}
    grad_fn = _jax.value_and_grad(_loss, argnums=(0, 1))

    def one_microbatch(ex, loss_target):
        ex = dict(ex)
        diff = ex.pop(TWIN_DIFF_INPUT)
        return grad_fn(weights, diff, {**shared, **ex}, loss_target)

    if N_MICROBATCH == 1:
        loss, (grad_w, grad_x) = one_microbatch(per_example, given["loss_target"])
    else:
        def body(carry, xs):
            loss_sum, grad_sum = carry
            l_k, (gw_k, gx_k) = one_microbatch(xs[0], xs[1])
            with _jax.named_scope("update"):
                return (loss_sum + l_k, _jax.tree.map(_jnp.add, grad_sum, gw_k)), gx_k

        init = (_jnp.zeros((), _jnp.float32), _jax.tree.map(_jnp.zeros_like, weights))
        (loss, grad_w), grad_x = _jax.lax.scan(body, init, (per_example, given["loss_target"]))
    with _jax.named_scope("update"):
        delta_w, new_m, new_v = {}, {}, {}
        for n in TWIN_WEIGHTS:
            delta_w[n], new_m[n], new_v[n] = _adamw(weights[n], grad_w[n], given["m_" + n], given["v_" + n])
    return (loss, grad_x, *[grad_w[n] for n in TWIN_WEIGHTS], *[delta_w[n] for n in TWIN_WEIGHTS],
            *[new_m[n] for n in TWIN_WEIGHTS], *[new_v[n] for n in TWIN_WEIGHTS])
```

```python
import functools

import jax
import jax.numpy as jnp
from jax import lax
from jax.experimental import pallas as pl
from jax.experimental.pallas import tpu as pltpu

F32 = jnp.float32
BF16 = jnp.bfloat16
MESH = pl.DeviceIdType.MESH
ANY = pl.BlockSpec(memory_space=pl.ANY)

RMS_EPS = 1e-5
CHUNK = 128
GROUPS = 8
ADAM_LR, ADAM_B1, ADAM_B2, ADAM_EPS, ADAM_WD, ADAM_STEP = 0.001, 0.9, 0.999, 1e-08, 0.01, 10

N_CHIPS = 4
HALO = 8
VMEM_LIMIT = 56 * 1024 * 1024
GELU_C = 0.7978845608028654
GELU_A = 0.044715


def _params(sem=None):
    return pltpu.CompilerParams(dimension_semantics=sem, vmem_limit_bytes=VMEM_LIMIT)


def _tile(dim, pref, quantum=128):
    if dim <= pref:
        return dim
    t = (pref // quantum) * quantum
    while t >= quantum:
        if dim % t == 0:
            return t
        t -= quantum
    return dim


def _mm(a, b, *, name, ta=False, tb=False, res=None, out_dtype=F32, tm=1024, tn=1024, tk=2048):
    (K, M) = a.shape if ta else a.shape[::-1]
    N = b.shape[0] if tb else b.shape[1]
    assert (b.shape[1] if tb else b.shape[0]) == K
    tm, tn, tk = _tile(M, tm), _tile(N, tn), _tile(K, tk)
    nk = K // tk
    a_spec = pl.BlockSpec((tk, tm), lambda i, j, k: (k, i)) if ta else pl.BlockSpec((tm, tk), lambda i, j, k: (i, k))
    b_spec = pl.BlockSpec((tn, tk), lambda i, j, k: (j, k)) if tb else pl.BlockSpec((tk, tn), lambda i, j, k: (k, j))
    o_spec = pl.BlockSpec((tm, tn), lambda i, j, k: (i, j))
    dims = (((0 if ta else 1,), (1 if tb else 0,)), ((), ()))
    direct = out_dtype == F32

    def body(*refs):
        a_ref, b_ref = refs[0], refs[1]
        r_ref = refs[2] if res is not None else None
        o_ref = refs[3] if res is not None else refs[2]
        acc_ref = o_ref if direct else refs[-1]
        part = lax.dot_general(a_ref[...], b_ref[...], dims, preferred_element_type=F32)
        if nk == 1:
            if r_ref is not None:
                part = part + r_ref[...]
            o_ref[...] = part.astype(o_ref.dtype)
            return
        k = pl.program_id(2)

        @pl.when(k == 0)
        def _():
            acc_ref[...] = part

        @pl.when(jnp.logical_and(k > 0, k < nk - 1))
        def _():
            acc_ref[...] += part

        @pl.when(k == nk - 1)
        def _():
            tot = acc_ref[...] + part
            if r_ref is not None:
                tot = tot + r_ref[...]
            o_ref[...] = tot.astype(o_ref.dtype)

    in_specs = [a_spec, b_spec] + ([o_spec] if res is not None else [])
    args = (a, b) + ((res,) if res is not None else ())
    scratch = [] if (direct or nk == 1) else [pltpu.VMEM((tm, tn), F32)]
    return pl.pallas_call(
        body, grid=(M // tm, N // tn, nk), in_specs=in_specs, out_specs=o_spec,
        out_shape=jax.ShapeDtypeStruct((M, N), out_dtype), scratch_shapes=scratch,
        compiler_params=_params(("parallel", "parallel", "arbitrary")), name=name,
    )(*args)


def _row_spec(rb, w):
    return pl.BlockSpec((rb, w), lambda i: (i, 0))


def _prev_spec(rb, w):
    return pl.BlockSpec((HALO, w), lambda i: (jnp.maximum(i * (rb // HALO) - 1, 0), 0))


def _next_spec(rb, w, t):
    return pl.BlockSpec((HALO, w), lambda i: (jnp.minimum((i + 1) * (rb // HALO), t // HALO - 1), 0))


def _full_spec(shape):
    return pl.BlockSpec(shape, lambda i: tuple(0 for _ in shape))


def _shift(e, s):
    return pltpu.roll(e, s % e.shape[0], 0)


def _gelu(x):
    return 0.5 * x * (1.0 + jnp.tanh(GELU_C * (x + GELU_A * x * x * x)))


def _gelu_grad(x):
    th = jnp.tanh(GELU_C * (x + GELU_A * x * x * x))
    return 0.5 * (1.0 + th) + 0.5 * x * (1.0 - th * th) * (GELU_C * (1.0 + 3.0 * GELU_A * x * x))


def _sigmoid(x):
    return 1.0 / (1.0 + jnp.exp(-x))


def _rms_fwd(x, g, *, name, rb=256):
    t, d = x.shape
    rb = _tile(t, rb, 8)

    def body(x_ref, g_ref, h_ref):
        xv = x_ref[...]
        r = lax.rsqrt(jnp.mean(xv * xv, axis=-1, keepdims=True) + RMS_EPS)
        h_ref[...] = ((xv * r) * g_ref[...]).astype(BF16)

    return pl.pallas_call(
        body, grid=(t // rb,), in_specs=[_row_spec(rb, d), _full_spec((1, d))], out_specs=_row_spec(rb, d),
        out_shape=jax.ShapeDtypeStruct((t, d), BF16), compiler_params=_params(("parallel",)), name=name,
    )(x, g)


def _rms_bwd(dh, x, g, dres, *, name, rb=256):
    t, d = x.shape
    rb = _tile(t, rb, 8)

    def body(dh_ref, x_ref, g_ref, dres_ref, dx_ref, dxb_ref, dg_ref):
        xv = x_ref[...]
        r = lax.rsqrt(jnp.mean(xv * xv, axis=-1, keepdims=True) + RMS_EPS)
        xhat = xv * r
        dh_v = dh_ref[...]
        dxhat = dh_v * g_ref[...]
        m = jnp.mean(dxhat * xhat, axis=-1, keepdims=True)
        dx = dres_ref[...] + r * (dxhat - xhat * m)
        dx_ref[...] = dx
        dxb_ref[...] = dx.astype(BF16)

        @pl.when(pl.program_id(0) == 0)
        def _():
            dg_ref[...] = jnp.zeros_like(dg_ref)

        dg_ref[0:1, :] += jnp.sum(dh_v * xhat, axis=0, keepdims=True)

    return pl.pallas_call(
        body, grid=(t // rb,),
        in_specs=[_row_spec(rb, d), _row_spec(rb, d), _full_spec((1, d)), _row_spec(rb, d)],
        out_specs=[_row_spec(rb, d), _row_spec(rb, d), _full_spec((8, d))],
        out_shape=[jax.ShapeDtypeStruct((t, d), F32), jax.ShapeDtypeStruct((t, d), BF16), jax.ShapeDtypeStruct((8, d), F32)],
        compiler_params=_params(("arbitrary",)), name=name,
    )(dh, x, g, dres)


def _final(x, tgt, g, *, name, rb=256):
    t, d = x.shape
    rb = _tile(t, rb, 8)
    inv_d = 1.0 / d

    def body(x_ref, t_ref, g_ref, l_ref, dx_ref, dxb_ref, dg_ref):
        xv = x_ref[...]
        gv = g_ref[...]
        r = lax.rsqrt(jnp.mean(xv * xv, axis=-1, keepdims=True) + RMS_EPS)
        xhat = xv * r
        e = xhat * gv - t_ref[...]
        dy = e * inv_d
        dxhat = dy * gv
        m = jnp.mean(dxhat * xhat, axis=-1, keepdims=True)
        dx = r * (dxhat - xhat * m)
        dx_ref[...] = dx
        dxb_ref[...] = dx.astype(BF16)

        @pl.when(pl.program_id(0) == 0)
        def _():
            l_ref[...] = jnp.zeros_like(l_ref)
            dg_ref[...] = jnp.zeros_like(dg_ref)

        l_ref[0:1, :] += jnp.sum(e * e, axis=0, keepdims=True) * (0.5 * inv_d)
        dg_ref[0:1, :] += jnp.sum(dy * xhat, axis=0, keepdims=True)

    return pl.pallas_call(
        body, grid=(t // rb,),
        in_specs=[_row_spec(rb, d), _row_spec(rb, d), _full_spec((1, d))],
        out_specs=[_full_spec((8, d)), _row_spec(rb, d), _row_spec(rb, d), _full_spec((8, d))],
        out_shape=[jax.ShapeDtypeStruct((8, d), F32), jax.ShapeDtypeStruct((t, d), F32),
                   jax.ShapeDtypeStruct((t, d), BF16), jax.ShapeDtypeStruct((8, d), F32)],
        compiler_params=_params(("arbitrary",)), name=name,
    )(x, tgt, g)


def _a_mid_fwd(bcx, wconv, *, name, rb=256, cw=512):
    t, d3 = bcx.shape
    d = d3 // 3
    rb, cw = _tile(t, rb, 8), _tile(d, cw)

    def body(cur_ref, prev_ref, w_ref, y_ref):
        first = pl.program_id(0) == 0
        for c0 in range(0, d, cw):
            cs = slice(c0, c0 + cw)
            gc, xs = slice(d + c0, d + c0 + cw), slice(2 * d + c0, 2 * d + c0 + cw)
            p_prev = jnp.where(first, 0.0, prev_ref[:, gc] * prev_ref[:, xs])
            e = jnp.concatenate([p_prev, cur_ref[:, gc] * cur_ref[:, xs]], axis=0)
            w = w_ref[:, cs]
            q = w[0:1] * _shift(e, 2) + w[1:2] * _shift(e, 1) + w[2:3] * e
            y_ref[:, cs] = (cur_ref[:, cs] * q[HALO:]).astype(BF16)

    return pl.pallas_call(
        body, grid=(t // rb,),
        in_specs=[_row_spec(rb, d3), _prev_spec(rb, d3), _full_spec((3, d))], out_specs=_row_spec(rb, d),
        out_shape=jax.ShapeDtypeStruct((t, d), BF16), compiler_params=_params(("parallel",)), name=name,
    )(bcx, bcx, wconv)


def _a_mid_bwd(bcx, dy, wconv, *, name, rb=128, cw=512):
    t, d3 = bcx.shape
    d = d3 // 3
    rb, cw = _tile(t, rb, 8), _tile(d, cw)

    def body(cur_ref, prev_ref, next_ref, dy_ref, dyn_ref, w_ref, o_ref, dw_ref):
        i = pl.program_id(0)
        first, last = i == 0, i == pl.num_programs(0) - 1

        @pl.when(first)
        def _():
            dw_ref[...] = jnp.zeros_like(dw_ref)

        for c0 in range(0, d, cw):
            cs = slice(c0, c0 + cw)
            gc, xs = slice(d + c0, d + c0 + cw), slice(2 * d + c0, 2 * d + c0 + cw)
            zeros = jnp.zeros((HALO, cw), F32)
            gb_c, gc_c, xs_c = cur_ref[:, cs], cur_ref[:, gc], cur_ref[:, xs]
            p_prev = jnp.where(first, 0.0, prev_ref[:, gc] * prev_ref[:, xs])
            e = jnp.concatenate([p_prev, gc_c * xs_c, zeros], axis=0)
            dq_next = jnp.where(last, 0.0, dyn_ref[:, cs] * next_ref[:, cs])
            dy_c = dy_ref[:, cs]
            dq = jnp.concatenate([zeros, dy_c * gb_c, dq_next], axis=0)
            w = w_ref[:, cs]
            e1, e2 = _shift(e, 1), _shift(e, 2)
            q = w[0:1] * e2 + w[1:2] * e1 + w[2:3] * e
            dp = (w[2:3] * dq + w[1:2] * _shift(dq, -1) + w[0:1] * _shift(dq, -2))[HALO:HALO + rb]
            o_ref[:, cs] = (dy_c * q[HALO:HALO + rb]).astype(BF16)
            o_ref[:, gc] = (dp * xs_c).astype(BF16)
            o_ref[:, xs] = (dp * gc_c).astype(BF16)
            dq_c = dq[HALO:HALO + rb]
            dw_ref[0:1, cs] += jnp.sum(dq_c * e2[HALO:HALO + rb], axis=0, keepdims=True)
            dw_ref[1:2, cs] += jnp.sum(dq_c * e1[HALO:HALO + rb], axis=0, keepdims=True)
            dw_ref[2:3, cs] += jnp.sum(dq_c * e[HALO:HALO + rb], axis=0, keepdims=True)

    return pl.pallas_call(
        body, grid=(t // rb,),
        in_specs=[_row_spec(rb, d3), _prev_spec(rb, d3), _next_spec(rb, d3, t), _row_spec(rb, d), _next_spec(rb, d, t),
                  _full_spec((3, d))],
        out_specs=[_row_spec(rb, d3), _full_spec((8, d))],
        out_shape=[jax.ShapeDtypeStruct((t, d3), BF16), jax.ShapeDtypeStruct((8, d), F32)],
        compiler_params=_params(("arbitrary",)), name=name,
    )(bcx, bcx, bcx, dy, dy, wconv)


def _ffn_mid_fwd(up, wconv, bconv, *, name, rb=256, cw=512):
    t, f2 = up.shape
    f = f2 // 2
    rb, cw = _tile(t, rb, 8), _tile(f, cw)

    def body(cur_ref, prev_ref, w_ref, b_ref, act_ref):
        first = pl.program_id(0) == 0

        def conv(cols):
            e = jnp.concatenate([jnp.where(first, 0.0, prev_ref[:, cols]), cur_ref[:, cols]], axis=0)
            w = w_ref[:, cols]
            return (w[0:1] * _shift(e, 2) + w[1:2] * _shift(e, 1) + w[2:3] * e + b_ref[:, cols])[HALO:]

        for c0 in range(0, f, cw):
            g = conv(slice(c0, c0 + cw))
            a = conv(slice(f + c0, f + c0 + cw))
            act_ref[:, c0:c0 + cw] = (g * _sigmoid(g) * a).astype(BF16)

    return pl.pallas_call(
        body, grid=(t // rb,),
        in_specs=[_row_spec(rb, f2), _prev_spec(rb, f2), _full_spec((3, f2)), _full_spec((1, f2))],
        out_specs=_row_spec(rb, f), out_shape=jax.ShapeDtypeStruct((t, f), BF16),
        compiler_params=_params(("parallel",)), name=name,
    )(up, up, wconv, bconv)


def _ffn_mid_bwd(up, dact, wconv, bconv, *, name, rb=128, cw=512):
    t, f2 = up.shape
    f = f2 // 2
    rb, cw = _tile(t, rb, 8), _tile(f, cw)

    def body(cur_ref, prev_ref, next_ref, da_ref, dan_ref, w_ref, b_ref, o_ref, dwb_ref):
        i = pl.program_id(0)
        first, last = i == 0, i == pl.num_programs(0) - 1

        @pl.when(first)
        def _():
            dwb_ref[...] = jnp.zeros_like(dwb_ref)

        def ext(cols):
            e = jnp.concatenate([jnp.where(first, 0.0, prev_ref[:, cols]), cur_ref[:, cols], next_ref[:, cols]], axis=0)
            w = w_ref[:, cols]
            e1, e2 = _shift(e, 1), _shift(e, 2)
            return e, e1, e2, w, w[0:1] * e2 + w[1:2] * e1 + w[2:3] * e + b_ref[:, cols]

        def back(dc, e, e1, e2, w, cols):
            o_ref[:, cols] = (w[2:3] * dc + w[1:2] * _shift(dc, -1) + w[0:1] * _shift(dc, -2))[HALO:HALO + rb].astype(BF16)
            dc_c = dc[HALO:HALO + rb]
            dwb_ref[0:1, cols] += jnp.sum(dc_c * e2[HALO:HALO + rb], axis=0, keepdims=True)
            dwb_ref[1:2, cols] += jnp.sum(dc_c * e1[HALO:HALO + rb], axis=0, keepdims=True)
            dwb_ref[2:3, cols] += jnp.sum(dc_c * e[HALO:HALO + rb], axis=0, keepdims=True)
            dwb_ref[3:4, cols] += jnp.sum(dc_c, axis=0, keepdims=True)

        for c0 in range(0, f, cw):
            gcols, acols = slice(c0, c0 + cw), slice(f + c0, f + c0 + cw)
            eg, eg1, eg2, wg, g = ext(gcols)
            ea, ea1, ea2, wa, a = ext(acols)
            da_next = jnp.where(last, 0.0, dan_ref[:, gcols])
            da = jnp.concatenate([jnp.zeros((HALO, cw), F32), da_ref[:, gcols], da_next], axis=0)
            sg = _sigmoid(g)
            dg = da * a * (sg * (1.0 + g * (1.0 - sg)))
            dav = da * (g * sg)
            back(dg, eg, eg1, eg2, wg, gcols)
            back(dav, ea, ea1, ea2, wa, acols)

    return pl.pallas_call(
        body, grid=(t // rb,),
        in_specs=[_row_spec(rb, f2), _prev_spec(rb, f2), _next_spec(rb, f2, t), _row_spec(rb, f), _next_spec(rb, f, t),
                  _full_spec((3, f2)), _full_spec((1, f2))],
        out_specs=[_row_spec(rb, f2), _full_spec((8, f2))],
        out_shape=[jax.ShapeDtypeStruct((t, f2), BF16), jax.ShapeDtypeStruct((8, f2), F32)],
        compiler_params=_params(("arbitrary",)), name=name,
    )(up, up, up, dact, dact, wconv, bconv)


def _causal_mask():
    return lax.broadcasted_iota(jnp.int32, (CHUNK, CHUNK), 0) >= lax.broadcasted_iota(jnp.int32, (CHUNK, CHUNK), 1)


def _b_mid_fwd(zp, vnorm, ws, bs, *, name, rb=256):
    t, d2 = zp.shape
    d = d2 // 2
    c = d // GROUPS
    rb = _tile(t, rb, CHUNK)

    def body(zp_ref, gv_ref, ws_ref, bs_ref, ug_ref, vn_ref, gate_ref):
        v = _gelu(zp_ref[:, d:])
        rv = lax.rsqrt(jnp.mean(v * v, axis=-1, keepdims=True) + RMS_EPS)
        vn_ref[...] = ((v * rv) * gv_ref[...]).astype(BF16)
        mask = _causal_mask()
        for h in range(GROUPS):
            hc = slice(h * c, (h + 1) * c)
            wm = jnp.where(mask, ws_ref[h], 0.0).astype(BF16)
            bcol = jnp.broadcast_to(bs_ref[h:h + 1, :], (CHUNK, CHUNK)).T[:, 0:1]
            for n in range(rb // CHUNK):
                rows = slice(n * CHUNK, (n + 1) * CHUNK)
                gate_ref[rows, hc] = jnp.dot(wm, vn_ref[rows, hc], preferred_element_type=F32) + bcol
        ug_ref[...] = (_gelu(zp_ref[:, :d]) * gate_ref[...]).astype(BF16)

    return pl.pallas_call(
        body, grid=(t // rb,),
        in_specs=[_row_spec(rb, d2), _full_spec((1, d)), _full_spec((GROUPS, CHUNK, CHUNK)), _full_spec((GROUPS, CHUNK))],
        out_specs=_row_spec(rb, d), out_shape=jax.ShapeDtypeStruct((t, d), BF16),
        scratch_shapes=[pltpu.VMEM((rb, d), BF16), pltpu.VMEM((rb, d), F32)],
        compiler_params=_params(("parallel",)), name=name,
    )(zp, vnorm, ws, bs)


def _b_mid_bwd(zp, dug, vnorm, ws, bs, *, name, rb=256):
    t, d2 = zp.shape
    d = d2 // 2
    c = d // GROUPS
    rb = _tile(t, rb, CHUNK)

    def body(zp_ref, dug_ref, gv_ref, ws_ref, bs_ref, dzp_ref, dws_ref, dbs_ref, dgv_ref,
             vn_ref, gate_ref, dm_ref, dvn_ref, dbacc_ref):
        i = pl.program_id(0)

        @pl.when(i == 0)
        def _():
            dws_ref[...] = jnp.zeros_like(dws_ref)
            dgv_ref[...] = jnp.zeros_like(dgv_ref)
            dbacc_ref[...] = jnp.zeros_like(dbacc_ref)

        zu, zv = zp_ref[:, :d], zp_ref[:, d:]
        u, v = _gelu(zu), _gelu(zv)
        rv = lax.rsqrt(jnp.mean(v * v, axis=-1, keepdims=True) + RMS_EPS)
        vhat = v * rv
        gv = gv_ref[...]
        vn_ref[...] = (vhat * gv).astype(BF16)
        dug_v = dug_ref[...]
        dm = dug_v * u
        dm_ref[...] = dm.astype(BF16)
        mask = _causal_mask()
        for h in range(GROUPS):
            hc = slice(h * c, (h + 1) * c)
            wm = jnp.where(mask, ws_ref[h], 0.0)
            wm_b, wmt_b = wm.astype(BF16), wm.T.astype(BF16)
            bcol = jnp.broadcast_to(bs_ref[h:h + 1, :], (CHUNK, CHUNK)).T[:, 0:1]
            dws_h = jnp.zeros((CHUNK, CHUNK), F32)
            dbs_h = jnp.zeros((CHUNK, c), F32)
            for n in range(rb // CHUNK):
                rows = slice(n * CHUNK, (n + 1) * CHUNK)
                vn_c, dm_c = vn_ref[rows, hc], dm_ref[rows, hc]
                gate_ref[rows, hc] = jnp.dot(wm_b, vn_c, preferred_element_type=F32) + bcol
                dws_h += lax.dot_general(dm_c, vn_c, (((1,), (1,)), ((), ())), preferred_element_type=F32)
                dvn_ref[rows, hc] = jnp.dot(wmt_b, dm_c, preferred_element_type=F32)
                dbs_h += dm[rows, hc]
            dws_ref[h] += dws_h
            dbacc_ref[h] += dbs_h
        du = dug_v * gate_ref[...]
        dvn = dvn_ref[...]
        dvhat = dvn * gv
        m = jnp.mean(dvhat * vhat, axis=-1, keepdims=True)
        dv = rv * (dvhat - vhat * m)
        dgv_ref[0:1, :] += jnp.sum(dvn * vhat, axis=0, keepdims=True)
        dzp_ref[:, :d] = (du * _gelu_grad(zu)).astype(BF16)
        dzp_ref[:, d:] = (dv * _gelu_grad(zv)).astype(BF16)

        @pl.when(i == pl.num_programs(0) - 1)
        def _():
            ones = jnp.ones((8, c), F32)
            for h in range(GROUPS):
                dws_ref[h] = jnp.where(mask, dws_ref[h], 0.0)
                row = lax.dot_general(ones, dbacc_ref[h], (((1,), (1,)), ((), ())),
                                      precision=lax.Precision.HIGHEST, preferred_element_type=F32)
                dbs_ref[h:h + 1, :] = row[0:1]

    return pl.pallas_call(
        body, grid=(t // rb,),
        in_specs=[_row_spec(rb, d2), _row_spec(rb, d), _full_spec((1, d)), _full_spec((GROUPS, CHUNK, CHUNK)),
                  _full_spec((GROUPS, CHUNK))],
        out_specs=[_row_spec(rb, d2), _full_spec((GROUPS, CHUNK, CHUNK)), _full_spec((GROUPS, CHUNK)), _full_spec((8, d))],
        out_shape=[jax.ShapeDtypeStruct((t, d2), BF16), jax.ShapeDtypeStruct((GROUPS, CHUNK, CHUNK), F32),
                   jax.ShapeDtypeStruct((GROUPS, CHUNK), F32), jax.ShapeDtypeStruct((8, d), F32)],
        scratch_shapes=[pltpu.VMEM((rb, d), BF16), pltpu.VMEM((rb, d), F32), pltpu.VMEM((rb, d), BF16),
                        pltpu.VMEM((rb, d), F32), pltpu.VMEM((GROUPS, CHUNK, c), F32)],
        compiler_params=_params(("arbitrary",)), name=name,
    )(zp, dug, vnorm, ws, bs)


def _cast_bf16(w, *, name, rb=512):
    shape = w.shape
    w2 = w.reshape(-1, shape[-1])
    r, c = w2.shape
    rb = _tile(r, rb, 16)

    def body(w_ref, o_ref):
        o_ref[...] = w_ref[...].astype(BF16)

    out = pl.pallas_call(
        body, grid=(r // rb,), in_specs=[_row_spec(rb, c)], out_specs=_row_spec(rb, c),
        out_shape=jax.ShapeDtypeStruct((r, c), BF16), compiler_params=_params(("parallel",)), name=name,
    )(w2)
    return out.reshape(shape)


def _adamw(w, g, m, v, *, name, rb=256):
    r, c = w.shape
    rb = _tile(r, rb, 8)
    c1 = 1.0 - ADAM_B1 ** ADAM_STEP
    c2 = 1.0 - ADAM_B2 ** ADAM_STEP

    def body(w_ref, g_ref, m_ref, v_ref, d_ref, nm_ref, nv_ref):
        gv = g_ref[...]
        nm = ADAM_B1 * m_ref[...] + (1.0 - ADAM_B1) * gv
        nv = ADAM_B2 * v_ref[...] + (1.0 - ADAM_B2) * (gv * gv)
        nm_ref[...] = nm
        nv_ref[...] = nv
        d_ref[...] = -ADAM_LR * ((nm / c1) / (jnp.sqrt(nv / c2) + ADAM_EPS) + ADAM_WD * w_ref[...])

    spec = _row_spec(rb, c)
    return pl.pallas_call(
        body, grid=(r // rb,), in_specs=[spec] * 4, out_specs=[spec] * 3,
        out_shape=[jax.ShapeDtypeStruct((r, c), F32)] * 3, compiler_params=_params(("parallel",)), name=name,
    )(w, g, m, v)


def _place():
    x, y, c = lax.axis_index("x"), lax.axis_index("y"), lax.axis_index("c")
    chips = [(1 - x, y), (x, 1 - y), (1 - x, 1 - y)]
    return x, y, c, 2 * x + y, chips


def _half(ref, kind, c):
    r, w = ref.shape
    if kind == "col":
        return ref.at[pl.ds(pl.multiple_of(c * (r // 2), 8), r // 2), :]
    return ref.at[:, pl.ds(pl.multiple_of(c * (w // 2), 128), w // 2)]


def _shard(ref, kind, s):
    r, w = ref.shape
    if kind == "col":
        return ref.at[:, pl.ds(pl.multiple_of(s * (w // N_CHIPS), 128), w // N_CHIPS)]
    return ref.at[pl.ds(pl.multiple_of(s * (r // N_CHIPS), 8), r // N_CHIPS), :]


def _remote(src, dst, send_sem, recv_sem, dev):
    return pltpu.make_async_remote_copy(src_ref=src, dst_ref=dst, send_sem=send_sem, recv_sem=recv_sem,
                                        device_id=dev, device_id_type=MESH)


def _gather_weights(big, kinds, small):
    nb, ns = len(big), len(small)

    def full_shape(a, kind):
        return (a.shape[0], a.shape[1] * N_CHIPS) if kind == "col" else (a.shape[0] * N_CHIPS, a.shape[1])

    out_shape = [jax.ShapeDtypeStruct(full_shape(a, k), a.dtype) for a, k in zip(big, kinds)]
    out_shape += [jax.ShapeDtypeStruct((N_CHIPS,) + a.shape, a.dtype) for a in small]

    def body(*refs):
        src_b, src_s = refs[:nb], refs[nb:nb + ns]
        dst_b, dst_s = refs[nb + ns:2 * nb + ns], refs[2 * nb + ns:2 * (nb + ns)]
        send, recv, fsend, frecv, ssend, srecv, lsem = refs[2 * (nb + ns):]
        x, y, c, s, chips = _place()
        sibling = (x, y, 1 - c)

        local = [pltpu.make_async_copy(src_b[a], _shard(dst_b[a], kinds[a], s), lsem.at[a]) for a in range(nb)]
        local += [pltpu.make_async_copy(src_s[a], dst_s[a].at[s], lsem.at[nb + a]) for a in range(ns)]
        for cp in local:
            cp.start()

        def ici(k, a):
            px, py = chips[k]
            mine = _half(_shard(dst_b[a], kinds[a], s), kinds[a], c)
            return _remote(_half(src_b[a], kinds[a], c), mine, send.at[k, a], recv.at[k, a], (px, py, c))

        def landed(k, a, core):
            px, py = chips[k]
            return _half(_shard(dst_b[a], kinds[a], 2 * px + py), kinds[a], core)

        def fwd(k, a):
            return _remote(landed(k, a, c), landed(k, a, c), fsend.at[k, a], frecv.at[k, a], sibling)

        def small_cp(k, a):
            px, py = chips[k]
            return _remote(src_s[a], dst_s[a].at[s], ssend.at[k, a], srecv.at[k, a], (px, py, c))

        started = []
        for k in range(3):
            for a in range(nb):
                cp = ici(k, a)
                cp.start()
                started.append(cp)
            for a in range(ns):
                cp = small_cp(k, a)
                cp.start()
                started.append(cp)
        for k in range(3):
            for a in range(nb):
                _remote(landed(k, a, c), landed(k, a, c), send.at[k, a], recv.at[k, a], sibling).wait_recv()
                cp = fwd(k, a)
                cp.start()
                started.append(cp)
        for k in range(3):
            px, py = chips[k]
            for a in range(ns):
                _remote(src_s[a], dst_s[a].at[2 * px + py], ssend.at[k, a], srecv.at[k, a], sibling).wait_recv()
            for a in range(nb):
                _remote(landed(k, a, 1 - c), landed(k, a, 1 - c), fsend.at[k, a], frecv.at[k, a], sibling).wait_recv()
        for cp in started:
            cp.wait_send()
        for cp in local:
            cp.wait()

    dma = pltpu.SemaphoreType.DMA
    outs = pl.pallas_call(
        body, in_specs=[ANY] * (nb + ns), out_specs=[ANY] * (nb + ns), out_shape=out_shape,
        scratch_shapes=[dma((3, nb)), dma((3, nb)), dma((3, nb)), dma((3, nb)), dma((3, ns)), dma((3, ns)), dma((nb + ns,))],
        compiler_params=pltpu.CompilerParams(has_side_effects=True), name="gather_weights",
    )(*big, *small)
    return outs[:nb], outs[nb:]


def _swap_core_halves(grads, kinds):
    n = len(grads)

    def half_shape(a, kind):
        return (a.shape[0] // 2, a.shape[1]) if kind == "col" else (a.shape[0], a.shape[1] // 2)

    def body(*refs):
        src, dst = refs[:n], refs[n:2 * n]
        send, recv = refs[2 * n:]
        x, y, c, _, _ = _place()
        cps = [_remote(_half(src[a], kinds[a], 1 - c), dst[a], send.at[a], recv.at[a], (x, y, 1 - c)) for a in range(n)]
        for cp in cps:
            cp.start()
        for cp in cps:
            cp.wait()

    dma = pltpu.SemaphoreType.DMA
    return pl.pallas_call(
        body, in_specs=[ANY] * n, out_specs=[ANY] * n,
        out_shape=[jax.ShapeDtypeStruct(half_shape(a, k), a.dtype) for a, k in zip(grads, kinds)],
        scratch_shapes=[dma((n,)), dma((n,))],
        compiler_params=pltpu.CompilerParams(has_side_effects=True), name="swap_core_halves",
    )(*grads)


def _half_block(kind, rb, cb, place_ref_index):
    def index(i, j, place):
        c = place[1]
        return (i + c * place_ref_index[0], j) if kind == "col" else (i, j + c * place_ref_index[1])
    return pl.BlockSpec((rb, cb), index)


def _chip_partial(g, other, kind, place, *, name):
    hr, hc = other.shape
    rb, cb = _tile(hr, 512, 16), _tile(hc, 1024)
    nblk = (hr // rb, hc // cb)

    def body(place_ref, g_ref, o_ref, p_ref):
        p_ref[...] = (g_ref[...] + o_ref[...]).astype(BF16)

    plain = pl.BlockSpec((rb, cb), lambda i, j, place: (i, j))
    return pl.pallas_call(
        body,
        grid_spec=pltpu.PrefetchScalarGridSpec(
            num_scalar_prefetch=1, grid=nblk, in_specs=[_half_block(kind, rb, cb, nblk), plain], out_specs=plain),
        out_shape=jax.ShapeDtypeStruct((hr, hc), BF16), compiler_params=_params(("parallel", "parallel")), name=name,
    )(place, g, other)


def _exchange_partials(parts, kinds):
    n = len(parts)

    def piece_shape(a, kind):
        return (a.shape[0], a.shape[1] // N_CHIPS) if kind == "col" else (a.shape[0] // N_CHIPS, a.shape[1])

    def body(*refs):
        src, dst = refs[:n], refs[n:2 * n]
        send, recv = refs[2 * n:]
        x, y, c, _, chips = _place()
        cps = []
        for k, (px, py) in enumerate(chips):
            for a in range(n):
                cps.append(_remote(_shard(src[a], kinds[a], 2 * px + py), dst[a].at[k], send.at[k, a], recv.at[k, a], (px, py, c)))
        for cp in cps:
            cp.start()
        for cp in cps:
            cp.wait()

    dma = pltpu.SemaphoreType.DMA
    return pl.pallas_call(
        body, in_specs=[ANY] * n, out_specs=[ANY] * n,
        out_shape=[jax.ShapeDtypeStruct((3,) + piece_shape(a, k), a.dtype) for a, k in zip(parts, kinds)],
        scratch_shapes=[dma((3, n)), dma((3, n))],
        compiler_params=pltpu.CompilerParams(has_side_effects=True), name="exchange_partials",
    )(*parts)


def _reduce_half(g, other, recv, kind, place, *, name):
    _, pr, pc = recv.shape
    rb, cb = _tile(pr, 512, 16), _tile(pc, 1024)
    nblk = (pr // rb, pc // cb)
    full = (pr * 2, pc) if kind == "col" else (pr, pc * 2)

    def g_index(i, j, place):
        s, c = place[0], place[1]
        if kind == "col":
            return (i + c * nblk[0], j + s * nblk[1])
        return (i + s * nblk[0], j + c * nblk[1])

    def o_index(i, j, place):
        s = place[0]
        return (i, j + s * nblk[1]) if kind == "col" else (i + s * nblk[0], j)

    def body(place_ref, g_ref, o_ref, r_ref, out_ref):
        acc = g_ref[...] + o_ref[...]
        for k in range(3):
            acc = acc + r_ref[k].astype(F32)
        out_ref[...] = acc

    return pl.pallas_call(
        body,
        grid_spec=pltpu.PrefetchScalarGridSpec(
            num_scalar_prefetch=1, grid=nblk,
            in_specs=[pl.BlockSpec((rb, cb), g_index), pl.BlockSpec((rb, cb), o_index),
                      pl.BlockSpec((3, rb, cb), lambda i, j, place: (0, i, j))],
            out_specs=_half_block(kind, rb, cb, nblk)),
        out_shape=jax.ShapeDtypeStruct(full, F32), compiler_params=_params(("parallel", "parallel")), name=name,
    )(place, g, other, recv)


def _share_reduced(shards, kinds):
    n = len(shards)

    def body(*refs):
        src, dst = refs[:n], refs[n:2 * n]
        send, recv = refs[2 * n:]
        x, y, c, _, _ = _place()
        cps = [_remote(_half(src[a], kinds[a], c), _half(dst[a], kinds[a], c), send.at[a], recv.at[a], (x, y, 1 - c))
               for a in range(n)]
        for cp in cps:
            cp.start()
        for a, cp in enumerate(cps):
            cp.wait_send()
            _remote(_half(src[a], kinds[a], 1 - c), _half(dst[a], kinds[a], 1 - c), send.at[a], recv.at[a], (x, y, 1 - c)).wait_recv()

    dma = pltpu.SemaphoreType.DMA
    return pl.pallas_call(
        body, in_specs=[ANY] * n, out_specs=[ANY] * n,
        out_shape=[jax.ShapeDtypeStruct(a.shape, a.dtype) for a in shards],
        input_output_aliases={a: a for a in range(n)}, scratch_shapes=[dma((n,)), dma((n,))],
        compiler_params=pltpu.CompilerParams(has_side_effects=True), name="share_reduced",
    )(*shards)


def _all_reduce_small(packed):
    r, w = packed.shape
    n_dev = 8

    def body(in_ref, out_ref, slots, send, recv):
        x, y, c, _, _ = _place()
        me = 4 * x + 2 * y + c
        slots[me] = in_ref[...]
        cps = []
        for k in range(1, n_dev):
            px, py, pc = x ^ (k >> 2), y ^ ((k >> 1) & 1), c ^ (k & 1)
            cps.append(_remote(in_ref, slots.at[me], send.at[k - 1], recv.at[k - 1], (px, py, pc)))
        for cp in cps:
            cp.start()
        for k in range(1, n_dev):
            px, py, pc = x ^ (k >> 2), y ^ ((k >> 1) & 1), c ^ (k & 1)
            _remote(in_ref, slots.at[4 * px + 2 * py + pc], send.at[k - 1], recv.at[k - 1], (px, py, pc)).wait_recv()
        acc = slots[0]
        for j in range(1, n_dev):
            acc = acc + slots[j]
        out_ref[...] = acc
        for cp in cps:
            cp.wait_send()

    dma = pltpu.SemaphoreType.DMA
    vmem = pl.BlockSpec(memory_space=pltpu.VMEM)
    return pl.pallas_call(
        body, in_specs=[vmem], out_specs=vmem, out_shape=jax.ShapeDtypeStruct((r, w), F32),
        scratch_shapes=[pltpu.VMEM((n_dev, r, w), F32), dma((n_dev - 1,)), dma((n_dev - 1,))],
        compiler_params=pltpu.CompilerParams(has_side_effects=True, vmem_limit_bytes=VMEM_LIMIT), name="all_reduce_small",
    )(packed)


def _ffn_fwd(x, norm, w_up, conv_w, conv_b, w_down, tag):
    h = _rms_fwd(x, norm, name=f"ffn{tag}_norm")
    up = _mm(h, w_up, name=f"ffn{tag}_up")
    act = _ffn_mid_fwd(up, conv_w, conv_b, name=f"ffn{tag}_mid")
    x_out = _mm(act, w_down, res=x, tk=2816, name=f"ffn{tag}_down")
    return x_out, (h, up, act)


def _ffn_bwd(dx, dxb, x, norm, w_up, conv_w, conv_b, w_down, saved, tag):
    h, up, act = saved
    dact = _mm(dxb, w_down, tb=True, tn=512, tm=2048, name=f"ffn{tag}_dact")
    dup, dwb = _ffn_mid_bwd(up, dact, conv_w, conv_b, name=f"ffn{tag}_mid_bwd")
    dh = _mm(dup, w_up, tb=True, tk=2816, name=f"ffn{tag}_dh")
    dx_in, dxb_in, dnorm = _rms_bwd(dh, x, norm, dx, name=f"ffn{tag}_norm_bwd")
    d_down = _mm(act, dxb, ta=True, tm=1408, name=f"ffn{tag}_ddown")
    d_up = _mm(h, dup, ta=True, name=f"ffn{tag}_dup")
    return dx_in, dxb_in, dnorm, dwb, d_up, d_down


def kernel(x, a_norm, a_in, a_conv, a_out, b_norm, b_in, b_vnorm, b_ws, b_bs, b_out, f_norm, f_up, f_conv_w, f_conv_b, f_down, final_norm, loss_target, m_a_norm, m_a_in, m_a_conv, m_a_out, m_b_norm, m_b_in, m_b_vnorm, m_b_ws, m_b_bs, m_b_out, m_f_norm, m_f_up, m_f_conv_w, m_f_conv_b, m_f_down, m_final_norm, v_a_norm, v_a_in, v_a_conv, v_a_out, v_b_norm, v_b_in, v_b_vnorm, v_b_ws, v_b_bs, v_b_out, v_f_norm, v_f_up, v_f_conv_w, v_f_conv_b, v_f_down, v_final_norm):
    t, d = x.shape[1], x.shape[2]
    f2 = f_up.shape[2] * N_CHIPS
    x0, tgt = x[0], loss_target[0]
    s = 2 * lax.axis_index("x") + lax.axis_index("y")
    place = jnp.stack([s, lax.axis_index("c")]).astype(jnp.int32)

    kinds = ["col", "row", "col", "row", "col", "col", "row", "row"]
    shards = [_cast_bf16(a_in[0], name="cast_a_in"), _cast_bf16(a_out[0], name="cast_a_out"),
              _cast_bf16(b_in[0], name="cast_b_in"), _cast_bf16(b_out[0], name="cast_b_out")]
    f_up_b, f_down_b = _cast_bf16(f_up, name="cast_f_up"), _cast_bf16(f_down, name="cast_f_down")
    shards += [f_up_b[0], f_up_b[1], f_down_b[0], f_down_b[1]]
    small = [a_conv[0], b_norm, b_vnorm, f_conv_w.reshape(2 * 3, -1)]
    (w_ai, w_ao, w_bi, w_bo, w_up0, w_up1, w_dn0, w_dn1), (g_aconv, g_bnorm, g_bvnorm, g_fconv) = _gather_weights(shards, kinds, small)

    def unshard(a):
        return jnp.transpose(a, (1, 0, 2)).reshape(a.shape[1], -1)

    a_conv_f, b_norm_f, b_vnorm_f = unshard(g_aconv), unshard(g_bnorm), unshard(g_bvnorm)
    f_conv_f = unshard(g_fconv).reshape(2, 3, f2)
    ws, bs = b_ws[0], b_bs[0]

    h0 = _rms_fwd(x0, a_norm, name="a_norm")
    bcx = _mm(h0, w_ai, name="a_in")
    y = _a_mid_fwd(bcx, a_conv_f, name="a_mid")
    x1 = _mm(y, w_ao, res=x0, name="a_out")
    x2, ffn0 = _ffn_fwd(x1, f_norm[0:1], w_up0, f_conv_f[0], f_conv_b[0:1], w_dn0, 0)
    h2 = _rms_fwd(x2, b_norm_f, name="b_norm")
    zp = _mm(h2, w_bi, name="b_in")
    ug = _b_mid_fwd(zp, b_vnorm_f, ws, bs, name="b_mid")
    x3 = _mm(ug, w_bo, res=x2, name="b_out")
    x4, ffn1 = _ffn_fwd(x3, f_norm[1:2], w_up1, f_conv_f[1], f_conv_b[1:2], w_dn1, 1)
    loss_rows, dx4, dx4b, d_final = _final(x4, tgt, final_norm.reshape(1, d), name="final")

    dx3, dx3b, d_fnorm1, d_fwb1, d_up1, d_dn1 = _ffn_bwd(dx4, dx4b, x3, f_norm[1:2], w_up1, f_conv_f[1], f_conv_b[1:2], w_dn1, ffn1, 1)
    dug = _mm(dx3b, w_bo, tb=True, name="b_dug")
    dzp, d_ws, d_bs, d_bvnorm = _b_mid_bwd(zp, dug, b_vnorm_f, ws, bs, name="b_mid_bwd")
    dh2 = _mm(dzp, w_bi, tb=True, name="b_dh")
    dx2, dx2b, d_bnorm = _rms_bwd(dh2, x2, b_norm_f, dx3, name="b_norm_bwd")
    d_bo = _mm(ug, dx3b, ta=True, name="b_dout")
    d_bi = _mm(h2, dzp, ta=True, name="b_din")
    dx1, dx1b, d_fnorm0, d_fwb0, d_up0, d_dn0 = _ffn_bwd(dx2, dx2b, x1, f_norm[0:1], w_up0, f_conv_f[0], f_conv_b[0:1], w_dn0, ffn0, 0)
    dyy = _mm(dx1b, w_ao, tb=True, name="a_dy")
    dbcx, d_aconv = _a_mid_bwd(bcx, dyy, a_conv_f, name="a_mid_bwd")
    dh0 = _mm(dbcx, w_ai, tb=True, name="a_dh")
    grad_x, _, d_anorm = _rms_bwd(dh0, x0, a_norm, dx1, name="a_norm_bwd")
    d_ao = _mm(y, dx1b, ta=True, name="a_dout")
    d_ai = _mm(h0, dbcx, ta=True, name="a_din")

    grads = [d_ai, d_ao, d_bi, d_bo, d_up0, d_up1, d_dn0, d_dn1]
    tags = ["a_in", "a_out", "b_in", "b_out", "f_up0", "f_up1", "f_down0", "f_down1"]
    others = _swap_core_halves(grads, kinds)
    parts = [_chip_partial(g, o, k, place, name=f"partial_{n}") for g, o, k, n in zip(grads, others, kinds, tags)]
    recvs = _exchange_partials(parts, kinds)
    halves = [_reduce_half(g, o, r, k, place, name=f"reduce_{n}") for g, o, r, k, n in zip(grads, others, recvs, kinds, tags)]
    g_ai, g_ao, g_bi, g_bo, g_up0, g_up1, g_dn0, g_dn1 = _share_reduced(halves, kinds)
    g_up, g_dn = jnp.concatenate([g_up0, g_up1], axis=0), jnp.concatenate([g_dn0, g_dn1], axis=0)

    g_a = jnp.concatenate([d_anorm, d_aconv, d_bnorm, d_bvnorm, d_fnorm0, d_fnorm1, d_final, loss_rows], axis=0)
    g_b = jnp.concatenate([d_fwb0, d_fwb1], axis=0)
    g_c = jnp.concatenate([d_ws.reshape(-1, CHUNK), d_bs], axis=0)
    na, nb_ = g_a.size // 128, g_b.size // 128
    packed = jnp.concatenate([g_a.reshape(-1, 128), g_b.reshape(-1, 128), g_c], axis=0)
    red = _all_reduce_small(packed)
    r_a, r_b, r_c = red[:na].reshape(g_a.shape), red[na:na + nb_].reshape(g_b.shape), red[na + nb_:]
    loss = jnp.sum(r_a[56])
    cs, fs = d // N_CHIPS, f2 // N_CHIPS

    def mine(a, width):
        return lax.dynamic_slice_in_dim(a, s * width, width, axis=1)

    small_grads = {
        "a_norm": r_a[0:1], "a_conv": mine(r_a[8:11], cs).reshape(1, 3, cs), "b_norm": mine(r_a[16:17], cs),
        "b_vnorm": mine(r_a[24:25], cs), "f_norm": jnp.concatenate([r_a[32:33], r_a[40:41]], axis=0),
        "final_norm": r_a[48], "b_ws": r_c[:GROUPS * CHUNK].reshape(b_ws.shape), "b_bs": r_c[GROUPS * CHUNK:].reshape(b_bs.shape),
        "f_conv_w": jnp.stack([mine(r_b[0:3], fs), mine(r_b[8:11], fs)]), "f_conv_b": jnp.concatenate([r_b[3:4], r_b[11:12]], axis=0),
    }
    grad = dict(small_grads, a_in=g_ai[None], a_out=g_ao[None], b_in=g_bi[None], b_out=g_bo[None],
                f_up=g_up.reshape(f_up.shape), f_down=g_dn.reshape(f_down.shape))

    names = ["a_norm", "a_in", "a_conv", "a_out", "b_norm", "b_in", "b_vnorm", "b_ws", "b_bs", "b_out", "f_norm", "f_up",
             "f_conv_w", "f_conv_b", "f_down", "final_norm"]
    weights = dict(zip(names, [a_norm, a_in, a_conv, a_out, b_norm, b_in, b_vnorm, b_ws, b_bs, b_out, f_norm, f_up, f_conv_w,
                               f_conv_b, f_down, final_norm]))
    ms = dict(zip(names, [m_a_norm, m_a_in, m_a_conv, m_a_out, m_b_norm, m_b_in, m_b_vnorm, m_b_ws, m_b_bs, m_b_out, m_f_norm,
                          m_f_up, m_f_conv_w, m_f_conv_b, m_f_down, m_final_norm]))
    vs = dict(zip(names, [v_a_norm, v_a_in, v_a_conv, v_a_out, v_b_norm, v_b_in, v_b_vnorm, v_b_ws, v_b_bs, v_b_out, v_f_norm,
                          v_f_up, v_f_conv_w, v_f_conv_b, v_f_down, v_final_norm]))
    delta, new_m, new_v = {}, {}, {}
    for n in names:
        w = weights[n]
        as2d = (lambda a: a.reshape(-1, w.shape[-1]))
        dl, nm, nv = _adamw(as2d(w), as2d(grad[n]), as2d(ms[n]), as2d(vs[n]), name=f"adamw_{n}")
        delta[n], new_m[n], new_v[n] = dl.reshape(w.shape), nm.reshape(w.shape), nv.reshape(w.shape)
        grad[n] = grad[n].reshape(w.shape)

    return (loss, grad_x[None], *[grad[n] for n in names], *[delta[n] for n in names],
            *[new_m[n] for n in names], *[new_v[n] for n in names])
```

```python
import functools

import jax
import jax.numpy as jnp
from jax import lax
from jax.experimental import pallas as pl
from jax.experimental.pallas import tpu as pltpu

F32 = jnp.float32
BF16 = jnp.bfloat16
MESH = pl.DeviceIdType.MESH
ANY = pl.BlockSpec(memory_space=pl.ANY)

RMS_EPS = 1e-5
CHUNK = 128
GROUPS = 8
ADAM_LR, ADAM_B1, ADAM_B2, ADAM_EPS, ADAM_WD, ADAM_STEP = 0.001, 0.9, 0.999, 1e-08, 0.01, 10

N_CHIPS = 4
HALO = 8
VMEM_LIMIT = 56 * 1024 * 1024
GELU_C = 0.7978845608028654
GELU_A = 0.044715


def _params(sem=None):
    return pltpu.CompilerParams(dimension_semantics=sem, vmem_limit_bytes=VMEM_LIMIT)


def _tile(dim, pref, quantum=128):
    if dim <= pref:
        return dim
    t = (pref // quantum) * quantum
    while t >= quantum:
        if dim % t == 0:
            return t
        t -= quantum
    return dim


def _mm(a, b, *, name, ta=False, tb=False, res=None, dep=None, out_dtype=F32, tm=1024, tn=1024, tk=2048):
    (K, M) = a.shape if ta else a.shape[::-1]
    N = b.shape[0] if tb else b.shape[1]
    assert (b.shape[1] if tb else b.shape[0]) == K
    tm, tn, tk = _tile(M, tm), _tile(N, tn), _tile(K, tk)
    nk = K // tk
    a_spec = pl.BlockSpec((tk, tm), lambda i, j, k: (k, i)) if ta else pl.BlockSpec((tm, tk), lambda i, j, k: (i, k))
    b_spec = pl.BlockSpec((tn, tk), lambda i, j, k: (j, k)) if tb else pl.BlockSpec((tk, tn), lambda i, j, k: (k, j))
    o_spec = pl.BlockSpec((tm, tn), lambda i, j, k: (i, j))
    dims = (((0 if ta else 1,), (1 if tb else 0,)), ((), ()))
    direct = out_dtype == F32

    def body(*refs):
        a_ref, b_ref = refs[0], refs[1]
        r_ref = refs[2] if res is not None else None
        o_ref = refs[2 + (res is not None) + (dep is not None)]
        acc_ref = o_ref if direct else refs[-1]
        part = lax.dot_general(a_ref[...], b_ref[...], dims, preferred_element_type=F32)
        if nk == 1:
            if r_ref is not None:
                part = part + r_ref[...]
            o_ref[...] = part.astype(o_ref.dtype)
            return
        k = pl.program_id(2)

        @pl.when(k == 0)
        def _():
            acc_ref[...] = part

        @pl.when(jnp.logical_and(k > 0, k < nk - 1))
        def _():
            acc_ref[...] += part

        @pl.when(k == nk - 1)
        def _():
            tot = acc_ref[...] + part
            if r_ref is not None:
                tot = tot + r_ref[...]
            o_ref[...] = tot.astype(o_ref.dtype)

    in_specs = [a_spec, b_spec] + ([o_spec] if res is not None else []) + ([ANY] if dep is not None else [])
    args = (a, b) + ((res,) if res is not None else ()) + ((dep,) if dep is not None else ())
    scratch = [] if (direct or nk == 1) else [pltpu.VMEM((tm, tn), F32)]
    return pl.pallas_call(
        body, grid=(M // tm, N // tn, nk), in_specs=in_specs, out_specs=o_spec,
        out_shape=jax.ShapeDtypeStruct((M, N), out_dtype), scratch_shapes=scratch,
        compiler_params=_params(("parallel", "parallel", "arbitrary")), name=name,
    )(*args)


def _row_spec(rb, w):
    return pl.BlockSpec((rb, w), lambda i: (i, 0))


def _prev_spec(rb, w):
    return pl.BlockSpec((HALO, w), lambda i: (jnp.maximum(i * (rb // HALO) - 1, 0), 0))


def _next_spec(rb, w, t):
    return pl.BlockSpec((HALO, w), lambda i: (jnp.minimum((i + 1) * (rb // HALO), t // HALO - 1), 0))


def _full_spec(shape):
    return pl.BlockSpec(shape, lambda i: tuple(0 for _ in shape))


def _shift(e, s):
    return pltpu.roll(e, s % e.shape[0], 0)


def _gelu(x):
    return 0.5 * x * (1.0 + jnp.tanh(GELU_C * (x + GELU_A * x * x * x)))


def _gelu_grad(x):
    th = jnp.tanh(GELU_C * (x + GELU_A * x * x * x))
    return 0.5 * (1.0 + th) + 0.5 * x * (1.0 - th * th) * (GELU_C * (1.0 + 3.0 * GELU_A * x * x))


def _sigmoid(x):
    return 1.0 / (1.0 + jnp.exp(-x))


def _rms_fwd(x, g, *, name, dep=None, rb=256):
    t, d = x.shape
    rb = _tile(t, rb, 8)

    def body(x_ref, g_ref, *rest):
        h_ref = rest[-1]
        xv = x_ref[...]
        r = lax.rsqrt(jnp.mean(xv * xv, axis=-1, keepdims=True) + RMS_EPS)
        h_ref[...] = ((xv * r) * g_ref[...]).astype(BF16)

    return pl.pallas_call(
        body, grid=(t // rb,), in_specs=[_row_spec(rb, d), _full_spec((1, d))] + ([ANY] if dep is not None else []),
        out_specs=_row_spec(rb, d), out_shape=jax.ShapeDtypeStruct((t, d), BF16), compiler_params=_params(("parallel",)), name=name,
    )(x, g, *(() if dep is None else (dep,)))


def _rms_bwd(dh, x, g, dres, *, name, rb=256):
    t, d = x.shape
    rb = _tile(t, rb, 8)

    def body(dh_ref, x_ref, g_ref, dres_ref, dx_ref, dxb_ref, dg_ref):
        xv = x_ref[...]
        r = lax.rsqrt(jnp.mean(xv * xv, axis=-1, keepdims=True) + RMS_EPS)
        xhat = xv * r
        dh_v = dh_ref[...]
        dxhat = dh_v * g_ref[...]
        m = jnp.mean(dxhat * xhat, axis=-1, keepdims=True)
        dx = dres_ref[...] + r * (dxhat - xhat * m)
        dx_ref[...] = dx
        dxb_ref[...] = dx.astype(BF16)

        @pl.when(pl.program_id(0) == 0)
        def _():
            dg_ref[...] = jnp.zeros_like(dg_ref)

        dg_ref[0:1, :] += jnp.sum(dh_v * xhat, axis=0, keepdims=True)

    return pl.pallas_call(
        body, grid=(t // rb,),
        in_specs=[_row_spec(rb, d), _row_spec(rb, d), _full_spec((1, d)), _row_spec(rb, d)],
        out_specs=[_row_spec(rb, d), _row_spec(rb, d), _full_spec((8, d))],
        out_shape=[jax.ShapeDtypeStruct((t, d), F32), jax.ShapeDtypeStruct((t, d), BF16), jax.ShapeDtypeStruct((8, d), F32)],
        compiler_params=_params(("arbitrary",)), name=name,
    )(dh, x, g, dres)


def _final(x, tgt, g, *, name, rb=256):
    t, d = x.shape
    rb = _tile(t, rb, 8)
    inv_d = 1.0 / d

    def body(x_ref, t_ref, g_ref, l_ref, dx_ref, dxb_ref, dg_ref):
        xv = x_ref[...]
        gv = g_ref[...]
        r = lax.rsqrt(jnp.mean(xv * xv, axis=-1, keepdims=True) + RMS_EPS)
        xhat = xv * r
        e = xhat * gv - t_ref[...]
        dy = e * inv_d
        dxhat = dy * gv
        m = jnp.mean(dxhat * xhat, axis=-1, keepdims=True)
        dx = r * (dxhat - xhat * m)
        dx_ref[...] = dx
        dxb_ref[...] = dx.astype(BF16)

        @pl.when(pl.program_id(0) == 0)
        def _():
            l_ref[...] = jnp.zeros_like(l_ref)
            dg_ref[...] = jnp.zeros_like(dg_ref)

        l_ref[0:1, :] += jnp.sum(e * e, axis=0, keepdims=True) * (0.5 * inv_d)
        dg_ref[0:1, :] += jnp.sum(dy * xhat, axis=0, keepdims=True)

    return pl.pallas_call(
        body, grid=(t // rb,),
        in_specs=[_row_spec(rb, d), _row_spec(rb, d), _full_spec((1, d))],
        out_specs=[_full_spec((8, d)), _row_spec(rb, d), _row_spec(rb, d), _full_spec((8, d))],
        out_shape=[jax.ShapeDtypeStruct((8, d), F32), jax.ShapeDtypeStruct((t, d), F32),
                   jax.ShapeDtypeStruct((t, d), BF16), jax.ShapeDtypeStruct((8, d), F32)],
        compiler_params=_params(("arbitrary",)), name=name,
    )(x, tgt, g)


def _a_mid_fwd(bcx, wconv, *, name, rb=256, cw=512):
    t, d3 = bcx.shape
    d = d3 // 3
    rb, cw = _tile(t, rb, 8), _tile(d, cw)

    def body(cur_ref, prev_ref, w_ref, y_ref):
        first = pl.program_id(0) == 0
        for c0 in range(0, d, cw):
            cs = slice(c0, c0 + cw)
            gc, xs = slice(d + c0, d + c0 + cw), slice(2 * d + c0, 2 * d + c0 + cw)
            p_prev = jnp.where(first, 0.0, prev_ref[:, gc] * prev_ref[:, xs])
            e = jnp.concatenate([p_prev, cur_ref[:, gc] * cur_ref[:, xs]], axis=0)
            w = w_ref[:, cs]
            q = w[0:1] * _shift(e, 2) + w[1:2] * _shift(e, 1) + w[2:3] * e
            y_ref[:, cs] = (cur_ref[:, cs] * q[HALO:]).astype(BF16)

    return pl.pallas_call(
        body, grid=(t // rb,),
        in_specs=[_row_spec(rb, d3), _prev_spec(rb, d3), _full_spec((3, d))], out_specs=_row_spec(rb, d),
        out_shape=jax.ShapeDtypeStruct((t, d), BF16), compiler_params=_params(("parallel",)), name=name,
    )(bcx, bcx, wconv)


def _a_mid_bwd(bcx, dy, wconv, *, name, rb=128, cw=512):
    t, d3 = bcx.shape
    d = d3 // 3
    rb, cw = _tile(t, rb, 8), _tile(d, cw)

    def body(cur_ref, prev_ref, next_ref, dy_ref, dyn_ref, w_ref, o_ref, dw_ref):
        i = pl.program_id(0)
        first, last = i == 0, i == pl.num_programs(0) - 1

        @pl.when(first)
        def _():
            dw_ref[...] = jnp.zeros_like(dw_ref)

        for c0 in range(0, d, cw):
            cs = slice(c0, c0 + cw)
            gc, xs = slice(d + c0, d + c0 + cw), slice(2 * d + c0, 2 * d + c0 + cw)
            zeros = jnp.zeros((HALO, cw), F32)
            gb_c, gc_c, xs_c = cur_ref[:, cs], cur_ref[:, gc], cur_ref[:, xs]
            p_prev = jnp.where(first, 0.0, prev_ref[:, gc] * prev_ref[:, xs])
            e = jnp.concatenate([p_prev, gc_c * xs_c, zeros], axis=0)
            dq_next = jnp.where(last, 0.0, dyn_ref[:, cs] * next_ref[:, cs])
            dy_c = dy_ref[:, cs]
            dq = jnp.concatenate([zeros, dy_c * gb_c, dq_next], axis=0)
            w = w_ref[:, cs]
            e1, e2 = _shift(e, 1), _shift(e, 2)
            q = w[0:1] * e2 + w[1:2] * e1 + w[2:3] * e
            dp = (w[2:3] * dq + w[1:2] * _shift(dq, -1) + w[0:1] * _shift(dq, -2))[HALO:HALO + rb]
            o_ref[:, cs] = (dy_c * q[HALO:HALO + rb]).astype(BF16)
            o_ref[:, gc] = (dp * xs_c).astype(BF16)
            o_ref[:, xs] = (dp * gc_c).astype(BF16)
            dq_c = dq[HALO:HALO + rb]
            dw_ref[0:1, cs] += jnp.sum(dq_c * e2[HALO:HALO + rb], axis=0, keepdims=True)
            dw_ref[1:2, cs] += jnp.sum(dq_c * e1[HALO:HALO + rb], axis=0, keepdims=True)
            dw_ref[2:3, cs] += jnp.sum(dq_c * e[HALO:HALO + rb], axis=0, keepdims=True)

    return pl.pallas_call(
        body, grid=(t // rb,),
        in_specs=[_row_spec(rb, d3), _prev_spec(rb, d3), _next_spec(rb, d3, t), _row_spec(rb, d), _next_spec(rb, d, t),
                  _full_spec((3, d))],
        out_specs=[_row_spec(rb, d3), _full_spec((8, d))],
        out_shape=[jax.ShapeDtypeStruct((t, d3), BF16), jax.ShapeDtypeStruct((8, d), F32)],
        compiler_params=_params(("arbitrary",)), name=name,
    )(bcx, bcx, bcx, dy, dy, wconv)


def _ffn_mid_fwd(up, wconv, bconv, *, name, rb=256, cw=512):
    t, f2 = up.shape
    f = f2 // 2
    rb, cw = _tile(t, rb, 8), _tile(f, cw)

    def body(cur_ref, prev_ref, w_ref, b_ref, act_ref):
        first = pl.program_id(0) == 0

        def conv(cols):
            e = jnp.concatenate([jnp.where(first, 0.0, prev_ref[:, cols]), cur_ref[:, cols]], axis=0)
            w = w_ref[:, cols]
            return (w[0:1] * _shift(e, 2) + w[1:2] * _shift(e, 1) + w[2:3] * e + b_ref[:, cols])[HALO:]

        for c0 in range(0, f, cw):
            g = conv(slice(c0, c0 + cw))
            a = conv(slice(f + c0, f + c0 + cw))
            act_ref[:, c0:c0 + cw] = (g * _sigmoid(g) * a).astype(BF16)

    return pl.pallas_call(
        body, grid=(t // rb,),
        in_specs=[_row_spec(rb, f2), _prev_spec(rb, f2), _full_spec((3, f2)), _full_spec((1, f2))],
        out_specs=_row_spec(rb, f), out_shape=jax.ShapeDtypeStruct((t, f), BF16),
        compiler_params=_params(("parallel",)), name=name,
    )(up, up, wconv, bconv)


def _ffn_mid_bwd(up, dact, wconv, bconv, *, name, rb=128, cw=512):
    t, f2 = up.shape
    f = f2 // 2
    rb, cw = _tile(t, rb, 8), _tile(f, cw)

    def body(cur_ref, prev_ref, next_ref, da_ref, dan_ref, w_ref, b_ref, o_ref, dwb_ref):
        i = pl.program_id(0)
        first, last = i == 0, i == pl.num_programs(0) - 1

        @pl.when(first)
        def _():
            dwb_ref[...] = jnp.zeros_like(dwb_ref)

        def ext(cols):
            e = jnp.concatenate([jnp.where(first, 0.0, prev_ref[:, cols]), cur_ref[:, cols], next_ref[:, cols]], axis=0)
            w = w_ref[:, cols]
            e1, e2 = _shift(e, 1), _shift(e, 2)
            return e, e1, e2, w, w[0:1] * e2 + w[1:2] * e1 + w[2:3] * e + b_ref[:, cols]

        def back(dc, e, e1, e2, w, cols):
            o_ref[:, cols] = (w[2:3] * dc + w[1:2] * _shift(dc, -1) + w[0:1] * _shift(dc, -2))[HALO:HALO + rb].astype(BF16)
            dc_c = dc[HALO:HALO + rb]
            dwb_ref[0:1, cols] += jnp.sum(dc_c * e2[HALO:HALO + rb], axis=0, keepdims=True)
            dwb_ref[1:2, cols] += jnp.sum(dc_c * e1[HALO:HALO + rb], axis=0, keepdims=True)
            dwb_ref[2:3, cols] += jnp.sum(dc_c * e[HALO:HALO + rb], axis=0, keepdims=True)
            dwb_ref[3:4, cols] += jnp.sum(dc_c, axis=0, keepdims=True)

        for c0 in range(0, f, cw):
            gcols, acols = slice(c0, c0 + cw), slice(f + c0, f + c0 + cw)
            eg, eg1, eg2, wg, g = ext(gcols)
            ea, ea1, ea2, wa, a = ext(acols)
            da_next = jnp.where(last, 0.0, dan_ref[:, gcols])
            da = jnp.concatenate([jnp.zeros((HALO, cw), F32), da_ref[:, gcols], da_next], axis=0)
            sg = _sigmoid(g)
            dg = da * a * (sg * (1.0 + g * (1.0 - sg)))
            dav = da * (g * sg)
            back(dg, eg, eg1, eg2, wg, gcols)
            back(dav, ea, ea1, ea2, wa, acols)

    return pl.pallas_call(
        body, grid=(t // rb,),
        in_specs=[_row_spec(rb, f2), _prev_spec(rb, f2), _next_spec(rb, f2, t), _row_spec(rb, f), _next_spec(rb, f, t),
                  _full_spec((3, f2)), _full_spec((1, f2))],
        out_specs=[_row_spec(rb, f2), _full_spec((8, f2))],
        out_shape=[jax.ShapeDtypeStruct((t, f2), BF16), jax.ShapeDtypeStruct((8, f2), F32)],
        compiler_params=_params(("arbitrary",)), name=name,
    )(up, up, up, dact, dact, wconv, bconv)


def _causal_mask():
    return lax.broadcasted_iota(jnp.int32, (CHUNK, CHUNK), 0) >= lax.broadcasted_iota(jnp.int32, (CHUNK, CHUNK), 1)


def _b_mid_fwd(zp, vnorm, ws, bs, *, name, rb=256):
    t, d2 = zp.shape
    d = d2 // 2
    c = d // GROUPS
    rb = _tile(t, rb, CHUNK)

    def body(zp_ref, gv_ref, ws_ref, bs_ref, ug_ref, vn_ref, gate_ref):
        v = _gelu(zp_ref[:, d:])
        rv = lax.rsqrt(jnp.mean(v * v, axis=-1, keepdims=True) + RMS_EPS)
        vn_ref[...] = ((v * rv) * gv_ref[...]).astype(BF16)
        mask = _causal_mask()
        for h in range(GROUPS):
            hc = slice(h * c, (h + 1) * c)
            wm = jnp.where(mask, ws_ref[h], 0.0).astype(BF16)
            bcol = jnp.broadcast_to(bs_ref[h:h + 1, :], (CHUNK, CHUNK)).T[:, 0:1]
            for n in range(rb // CHUNK):
                rows = slice(n * CHUNK, (n + 1) * CHUNK)
                gate_ref[rows, hc] = jnp.dot(wm, vn_ref[rows, hc], preferred_element_type=F32) + bcol
        ug_ref[...] = (_gelu(zp_ref[:, :d]) * gate_ref[...]).astype(BF16)

    return pl.pallas_call(
        body, grid=(t // rb,),
        in_specs=[_row_spec(rb, d2), _full_spec((1, d)), _full_spec((GROUPS, CHUNK, CHUNK)), _full_spec((GROUPS, CHUNK))],
        out_specs=_row_spec(rb, d), out_shape=jax.ShapeDtypeStruct((t, d), BF16),
        scratch_shapes=[pltpu.VMEM((rb, d), BF16), pltpu.VMEM((rb, d), F32)],
        compiler_params=_params(("parallel",)), name=name,
    )(zp, vnorm, ws, bs)


def _b_mid_bwd(zp, dug, vnorm, ws, bs, *, name, rb=256):
    t, d2 = zp.shape
    d = d2 // 2
    c = d // GROUPS
    rb = _tile(t, rb, CHUNK)

    def body(zp_ref, dug_ref, gv_ref, ws_ref, bs_ref, dzp_ref, dws_ref, dbs_ref, dgv_ref,
             vn_ref, gate_ref, dm_ref, dvn_ref, dbacc_ref):
        i = pl.program_id(0)

        @pl.when(i == 0)
        def _():
            dws_ref[...] = jnp.zeros_like(dws_ref)
            dgv_ref[...] = jnp.zeros_like(dgv_ref)
            dbacc_ref[...] = jnp.zeros_like(dbacc_ref)

        zu, zv = zp_ref[:, :d], zp_ref[:, d:]
        u, v = _gelu(zu), _gelu(zv)
        rv = lax.rsqrt(jnp.mean(v * v, axis=-1, keepdims=True) + RMS_EPS)
        vhat = v * rv
        gv = gv_ref[...]
        vn_ref[...] = (vhat * gv).astype(BF16)
        dug_v = dug_ref[...]
        dm = dug_v * u
        dm_ref[...] = dm.astype(BF16)
        mask = _causal_mask()
        for h in range(GROUPS):
            hc = slice(h * c, (h + 1) * c)
            wm = jnp.where(mask, ws_ref[h], 0.0)
            wm_b, wmt_b = wm.astype(BF16), wm.T.astype(BF16)
            bcol = jnp.broadcast_to(bs_ref[h:h + 1, :], (CHUNK, CHUNK)).T[:, 0:1]
            dws_h = jnp.zeros((CHUNK, CHUNK), F32)
            dbs_h = jnp.zeros((CHUNK, c), F32)
            for n in range(rb // CHUNK):
                rows = slice(n * CHUNK, (n + 1) * CHUNK)
                vn_c, dm_c = vn_ref[rows, hc], dm_ref[rows, hc]
                gate_ref[rows, hc] = jnp.dot(wm_b, vn_c, preferred_element_type=F32) + bcol
                dws_h += lax.dot_general(dm_c, vn_c, (((1,), (1,)), ((), ())), preferred_element_type=F32)
                dvn_ref[rows, hc] = jnp.dot(wmt_b, dm_c, preferred_element_type=F32)
                dbs_h += dm[rows, hc]
            dws_ref[h] += dws_h
            dbacc_ref[h] += dbs_h
        du = dug_v * gate_ref[...]
        dvn = dvn_ref[...]
        dvhat = dvn * gv
        m = jnp.mean(dvhat * vhat, axis=-1, keepdims=True)
        dv = rv * (dvhat - vhat * m)
        dgv_ref[0:1, :] += jnp.sum(dvn * vhat, axis=0, keepdims=True)
        dzp_ref[:, :d] = (du * _gelu_grad(zu)).astype(BF16)
        dzp_ref[:, d:] = (dv * _gelu_grad(zv)).astype(BF16)

        @pl.when(i == pl.num_programs(0) - 1)
        def _():
            ones = jnp.ones((8, c), F32)
            for h in range(GROUPS):
                dws_ref[h] = jnp.where(mask, dws_ref[h], 0.0)
                row = lax.dot_general(ones, dbacc_ref[h], (((1,), (1,)), ((), ())),
                                      precision=lax.Precision.HIGHEST, preferred_element_type=F32)
                dbs_ref[h:h + 1, :] = row[0:1]

    return pl.pallas_call(
        body, grid=(t // rb,),
        in_specs=[_row_spec(rb, d2), _row_spec(rb, d), _full_spec((1, d)), _full_spec((GROUPS, CHUNK, CHUNK)),
                  _full_spec((GROUPS, CHUNK))],
        out_specs=[_row_spec(rb, d2), _full_spec((GROUPS, CHUNK, CHUNK)), _full_spec((GROUPS, CHUNK)), _full_spec((8, d))],
        out_shape=[jax.ShapeDtypeStruct((t, d2), BF16), jax.ShapeDtypeStruct((GROUPS, CHUNK, CHUNK), F32),
                   jax.ShapeDtypeStruct((GROUPS, CHUNK), F32), jax.ShapeDtypeStruct((8, d), F32)],
        scratch_shapes=[pltpu.VMEM((rb, d), BF16), pltpu.VMEM((rb, d), F32), pltpu.VMEM((rb, d), BF16),
                        pltpu.VMEM((rb, d), F32), pltpu.VMEM((GROUPS, CHUNK, c), F32)],
        compiler_params=_params(("arbitrary",)), name=name,
    )(zp, dug, vnorm, ws, bs)


def _cast_into_full(w, layer, kind, place, *, name, dep=None, rb=256):
    _, r, c = w.shape
    rb = _tile(r, rb, 16)
    nrb = r // rb
    full = (r, c * N_CHIPS) if kind == "col" else (r * N_CHIPS, c)

    def body(place_ref, w_ref, *rest):
        rest[-1][...] = w_ref[...].astype(BF16)

    def o_index(i, place):
        return (i, place[0]) if kind == "col" else (i + place[0] * nrb, 0)

    in_specs = [pl.BlockSpec((None, rb, c), lambda i, place: (layer, i, 0))] + ([ANY] if dep is not None else [])
    return pl.pallas_call(
        body,
        grid_spec=pltpu.PrefetchScalarGridSpec(num_scalar_prefetch=1, grid=(nrb,), in_specs=in_specs,
                                               out_specs=pl.BlockSpec((rb, c), o_index)),
        out_shape=jax.ShapeDtypeStruct(full, BF16), compiler_params=_params(("parallel",)), name=name,
    )(place, w, *(() if dep is None else (dep,)))


def _adamw_layer(w, g, m, v, layer, prev, *, name, rb=128):
    _, r, c = w.shape
    rb = _tile(r, rb, 8)
    c1 = 1.0 - ADAM_B1 ** ADAM_STEP
    c2 = 1.0 - ADAM_B2 ** ADAM_STEP

    def body(w_ref, g_ref, m_ref, v_ref, *rest):
        go_ref, d_ref, nm_ref, nv_ref = rest[-4:]
        gv = g_ref[...]
        nm = ADAM_B1 * m_ref[...] + (1.0 - ADAM_B1) * gv
        nv = ADAM_B2 * v_ref[...] + (1.0 - ADAM_B2) * (gv * gv)
        go_ref[...] = gv
        nm_ref[...] = nm
        nv_ref[...] = nv
        d_ref[...] = -ADAM_LR * ((nm / c1) / (jnp.sqrt(nv / c2) + ADAM_EPS) + ADAM_WD * w_ref[...])

    lay = pl.BlockSpec((None, rb, c), lambda i: (layer, i, 0))
    return pl.pallas_call(
        body, grid=(r // rb,), in_specs=[lay, _row_spec(rb, c), lay, lay] + ([ANY] * 4 if prev else []), out_specs=[lay] * 4,
        out_shape=[jax.ShapeDtypeStruct(w.shape, F32)] * 4, input_output_aliases={4 + k: k for k in range(4)} if prev else {},
        compiler_params=_params(("parallel",)), name=name,
    )(w, g, m, v, *(prev or ()))


HBM = pl.BlockSpec(memory_space=pltpu.HBM)
SEM = pl.BlockSpec(memory_space=pltpu.SEMAPHORE)
SIDE_EFFECT = pltpu.SideEffectType.DATAFLOW_SIDE_EFFECTING


def _place():
    x, y, c = lax.axis_index("x"), lax.axis_index("y"), lax.axis_index("c")
    chips = [(1 - x, y), (x, 1 - y), (1 - x, 1 - y)]
    return x, y, c, 2 * x + y, chips


def _half(ref, kind, c):
    r, w = ref.shape
    if kind == "col":
        return ref.at[pl.ds(pl.multiple_of(c * (r // 2), 8), r // 2), :]
    return ref.at[:, pl.ds(pl.multiple_of(c * (w // 2), 128), w // 2)]


def _shard(ref, kind, s):
    r, w = ref.shape
    if kind == "col":
        return ref.at[:, pl.ds(pl.multiple_of(s * (w // N_CHIPS), 128), w // N_CHIPS)]
    return ref.at[pl.ds(pl.multiple_of(s * (r // N_CHIPS), 8), r // N_CHIPS), :]


def _remote(src, dst, send_sem, recv_sem, dev):
    return pltpu.make_async_remote_copy(src_ref=src, dst_ref=dst, send_sem=send_sem, recv_sem=recv_sem,
                                        device_id=dev, device_id_type=MESH)


def _start(name, bufs, plan, sem_shape, dep=None):
    n = len(bufs)
    n_in = n + (dep is not None)

    def body(*refs):
        sends, _ = plan(refs[:n], refs[n_in], refs[n_in + 1])
        for cp in sends:
            cp.start()
        refs[n_in + 2 + n][...] = jnp.zeros((8, 128), F32)

    dma = pltpu.SemaphoreType.DMA
    outs = pl.pallas_call(
        body, name=name,
        out_shape=(dma(sem_shape), dma(sem_shape), *[pltpu.HBM(b.shape, b.dtype) for b in bufs], jax.ShapeDtypeStruct((8, 128), F32)),
        in_specs=(HBM,) * n + ((ANY,) if dep is not None else ()),
        out_specs=(SEM, SEM) + (HBM,) * n + (pl.BlockSpec(memory_space=pltpu.VMEM),),
        input_output_aliases={i: i + 2 for i in range(n)},
        compiler_params=pltpu.CompilerParams(has_side_effects=SIDE_EFFECT),
    )(*[pltpu.with_memory_space_constraint(b, pltpu.HBM) for b in bufs], *(() if dep is None else (dep,)))
    return outs[0], outs[1], list(outs[2:2 + n]), outs[2 + n]


def _wait(name, started, plan, after):
    send, recv, bufs, _ = started
    n = len(bufs)

    def body(*refs):
        sends, recvs = plan(refs[:n], refs[n], refs[n + 1])
        for cp in sends:
            cp.wait_send()
        for cp in recvs:
            cp.wait_recv()

    return list(pl.pallas_call(
        body, name=name, out_shape=tuple(pltpu.HBM(b.shape, b.dtype) for b in bufs),
        in_specs=(HBM,) * n + (SEM, SEM, ANY), out_specs=(HBM,) * n, input_output_aliases={i: i for i in range(n)},
        compiler_params=pltpu.CompilerParams(has_side_effects=SIDE_EFFECT),
    )(*bufs, send, recv, after))


KINDS = ("col", "row")


def _gather_ici_plan(n_small):
    def plan(refs, send, recv):
        x, y, c, s, chips = _place()
        n = len(KINDS) + n_small
        sends, recvs = [], []
        for k, (px, py) in enumerate(chips):
            sp = 2 * px + py
            for a, kind in enumerate(KINDS):
                mine, theirs = _half(_shard(refs[a], kind, s), kind, c), _half(_shard(refs[a], kind, sp), kind, c)
                sends.append(_remote(mine, mine, send.at[k * n + a], recv.at[k * n + a], (px, py, c)))
                recvs.append(_remote(theirs, theirs, send.at[k * n + a], recv.at[k * n + a], (px, py, c)))
            for b in range(n_small):
                ref, sem = refs[len(KINDS) + b], k * n + len(KINDS) + b
                sends.append(_remote(ref.at[s], ref.at[s], send.at[sem], recv.at[sem], (px, py, c)))
                recvs.append(_remote(ref.at[sp], ref.at[sp], send.at[sem], recv.at[sem], (px, py, c)))
        return sends, recvs
    return plan


def _gather_d2d_plan(refs, send, recv):
    x, y, c, _, chips = _place()
    n = len(KINDS)
    sends, recvs = [], []
    for k, (px, py) in enumerate(chips):
        for a, kind in enumerate(KINDS):
            region, sem = _shard(refs[a], kind, 2 * px + py), k * n + a
            sends.append(_remote(_half(region, kind, c), _half(region, kind, c), send.at[sem], recv.at[sem], (x, y, 1 - c)))
            recvs.append(_remote(_half(region, kind, 1 - c), _half(region, kind, 1 - c), send.at[sem], recv.at[sem], (x, y, 1 - c)))
    return sends, recvs


def _swap_plan(refs, send, recv):
    x, y, c, _, _ = _place()
    n = len(KINDS)
    cps = [_remote(_half(refs[a], KINDS[a], 1 - c), refs[n + a], send.at[a], recv.at[a], (x, y, 1 - c)) for a in range(n)]
    return cps, cps


def _exchange_plan(refs, send, recv):
    x, y, c, _, chips = _place()
    n = len(KINDS)
    cps = []
    for k, (px, py) in enumerate(chips):
        for a in range(n):
            cps.append(_remote(_shard(refs[a], KINDS[a], 2 * px + py), refs[n + a].at[k], send.at[k * n + a], recv.at[k * n + a],
                               (px, py, c)))
    return cps, cps


def _share_plan(refs, send, recv):
    x, y, c, _, _ = _place()
    sends = [_remote(_half(refs[a], KINDS[a], c), _half(refs[a], KINDS[a], c), send.at[a], recv.at[a], (x, y, 1 - c))
             for a in range(len(KINDS))]
    recvs = [_remote(_half(refs[a], KINDS[a], 1 - c), _half(refs[a], KINDS[a], 1 - c), send.at[a], recv.at[a], (x, y, 1 - c))
             for a in range(len(KINDS))]
    return sends, recvs


def _spread_plan(refs, send, recv):
    packed, slots = refs
    x, y, c, _, _ = _place()
    sends, recvs = [], []
    for k in range(1, 8):
        px, py, pc = x ^ (k >> 2), y ^ ((k >> 1) & 1), c ^ (k & 1)
        sends.append(_remote(packed, slots.at[4 * x + 2 * y + c], send.at[k - 1], recv.at[k - 1], (px, py, pc)))
        recvs.append(_remote(packed, slots.at[4 * px + 2 * py + pc], send.at[k - 1], recv.at[k - 1], (px, py, pc)))
    return sends, recvs


def _half_index(kind, nblk):
    def index(i, j, place):
        return (i + place[1] * nblk[0], j) if kind == "col" else (i, j + place[1] * nblk[1])
    return index


def _chip_partial(g, other, kind, place, *, name):
    hr, hc = other.shape
    rb, cb = _tile(hr, 512, 16), _tile(hc, 1024)
    nblk = (hr // rb, hc // cb)

    def body(place_ref, g_ref, o_ref, p_ref):
        p_ref[...] = (g_ref[...].astype(F32) + o_ref[...].astype(F32)).astype(BF16)

    plain = pl.BlockSpec((rb, cb), lambda i, j, place: (i, j))
    return pl.pallas_call(
        body,
        grid_spec=pltpu.PrefetchScalarGridSpec(
            num_scalar_prefetch=1, grid=nblk, in_specs=[pl.BlockSpec((rb, cb), _half_index(kind, nblk)), plain], out_specs=plain),
        out_shape=jax.ShapeDtypeStruct((hr, hc), BF16), compiler_params=_params(("parallel", "parallel")), name=name,
    )(place, g, other)


def _reduce_half(g, other, recv, kind, place, *, name):
    _, pr, pc = recv.shape
    rb, cb = _tile(pr, 512, 16), _tile(pc, 1024)
    nblk = (pr // rb, pc // cb)
    full = (pr * 2, pc) if kind == "col" else (pr, pc * 2)

    def g_index(i, j, place):
        s, c = place[0], place[1]
        return (i + c * nblk[0], j + s * nblk[1]) if kind == "col" else (i + s * nblk[0], j + c * nblk[1])

    def o_index(i, j, place):
        return (i, j + place[0] * nblk[1]) if kind == "col" else (i + place[0] * nblk[0], j)

    def body(place_ref, g_ref, o_ref, r_ref, out_ref):
        acc = g_ref[...].astype(F32) + o_ref[...].astype(F32)
        for k in range(3):
            acc = acc + r_ref[k].astype(F32)
        out_ref[...] = acc

    return pl.pallas_call(
        body,
        grid_spec=pltpu.PrefetchScalarGridSpec(
            num_scalar_prefetch=1, grid=nblk,
            in_specs=[pl.BlockSpec((rb, cb), g_index), pl.BlockSpec((rb, cb), o_index),
                      pl.BlockSpec((3, rb, cb), lambda i, j, place: (0, i, j))],
            out_specs=pl.BlockSpec((rb, cb), _half_index(kind, nblk))),
        out_shape=jax.ShapeDtypeStruct(full, F32), compiler_params=_params(("parallel", "parallel")), name=name,
    )(place, g, other, recv)


def _sum_slots(packed, slots, me, *, name, rb=512):
    r, w = packed.shape
    rb = _tile(r, rb, 8)

    def body(me_ref, p_ref, s_ref, o_ref):
        acc = None
        for j in range(8):
            term = jnp.where(me_ref[0] == j, p_ref[...], s_ref[j])
            acc = term if acc is None else acc + term
        o_ref[...] = acc

    return pl.pallas_call(
        body,
        grid_spec=pltpu.PrefetchScalarGridSpec(
            num_scalar_prefetch=1, grid=(r // rb,),
            in_specs=[pl.BlockSpec((rb, w), lambda i, me: (i, 0)), pl.BlockSpec((8, rb, w), lambda i, me: (0, i, 0))],
            out_specs=pl.BlockSpec((rb, w), lambda i, me: (i, 0))),
        out_shape=jax.ShapeDtypeStruct((r, w), F32), compiler_params=_params(("parallel",)), name=name,
    )(me, packed, slots)


def _half_shape(a, kind):
    return (a.shape[0] // 2, a.shape[1]) if kind == "col" else (a.shape[0], a.shape[1] // 2)


def _rs_swap(tag, grads):
    others = [lax.empty(_half_shape(g, k), g.dtype) for g, k in zip(grads, KINDS)]
    return _start(f"rs_{tag}_swap", list(grads) + others, _swap_plan, (len(KINDS),))


def _rs_exchange(tag, swapped, place, after):
    bufs = _wait(f"rs_{tag}_swap_wait", swapped, _swap_plan, after)
    n = len(KINDS)
    grads, others = bufs[:n], bufs[n:]
    parts = [_chip_partial(g, o, k, place, name=f"rs_{tag}_partial_{k}") for g, o, k in zip(grads, others, KINDS)]
    lands = []
    for p, k in zip(parts, KINDS):
        piece = (p.shape[0], p.shape[1] // N_CHIPS) if k == "col" else (p.shape[0] // N_CHIPS, p.shape[1])
        lands.append(lax.empty((3,) + piece, p.dtype))
    return _start(f"rs_{tag}_exchange", parts + lands, _exchange_plan, (3 * n,)), grads, others


def _rs_share(tag, exchanged, place, after):
    started, grads, others = exchanged
    n = len(KINDS)
    recvs = _wait(f"rs_{tag}_exchange_wait", started, _exchange_plan, after)[n:]
    halves = [_reduce_half(g, o, r, k, place, name=f"rs_{tag}_reduce_{k}") for g, o, r, k in zip(grads, others, recvs, KINDS)]
    return _start(f"rs_{tag}_share", halves, _share_plan, (n,))


def _rs_finish(tag, shared, after):
    return _wait(f"rs_{tag}_share_wait", shared, _share_plan, after)


def _spread(tag, parts, dep):
    rows = [p.reshape(-1, 128) for p in parts]
    n = sum(r.shape[0] for r in rows)
    rows.append(jnp.zeros(((-n) % 512, 128), F32))
    packed = jnp.concatenate(rows, axis=0)
    return _start(f"small_{tag}_spread", [packed, lax.empty((8,) + packed.shape, F32)], _spread_plan, (7,), dep=dep)


def _spread_sum(tag, started, parts, me, after):
    packed, slots = _wait(f"small_{tag}_spread_wait", started, _spread_plan, after)
    total = _sum_slots(packed, slots, me, name=f"small_{tag}_sum")
    out, row = [], 0
    for p in parts:
        n = p.size // 128
        out.append(total[row:row + n].reshape(p.shape))
        row += n
    return out


def _ffn_fwd(x, norm, w_up, conv_w, conv_b, w_down, tag):
    h = _rms_fwd(x, norm, name=f"ffn{tag}_norm")
    up = _mm(h, w_up, name=f"ffn{tag}_up")
    act = _ffn_mid_fwd(up, conv_w, conv_b, name=f"ffn{tag}_mid")
    x_out = _mm(act, w_down, res=x, tk=2816, name=f"ffn{tag}_down")
    return x_out, (h, up, act)


def kernel(x, a_norm, a_in, a_conv, a_out, b_norm, b_in, b_vnorm, b_ws, b_bs, b_out, f_norm, f_up, f_conv_w, f_conv_b, f_down, final_norm, loss_target, m_a_norm, m_a_in, m_a_conv, m_a_out, m_b_norm, m_b_in, m_b_vnorm, m_b_ws, m_b_bs, m_b_out, m_f_norm, m_f_up, m_f_conv_w, m_f_conv_b, m_f_down, m_final_norm, v_a_norm, v_a_in, v_a_conv, v_a_out, v_b_norm, v_b_in, v_b_vnorm, v_b_ws, v_b_bs, v_b_out, v_f_norm, v_f_up, v_f_conv_w, v_f_conv_b, v_f_down, v_final_norm):
    t, d = x.shape[1], x.shape[2]
    f2 = f_up.shape[2] * N_CHIPS
    x0, tgt = x.reshape(t, d), loss_target.reshape(t, d)
    ax, ay, ac = lax.axis_index("x"), lax.axis_index("y"), lax.axis_index("c")
    s = 2 * ax + ay
    place = jnp.stack([s, ac]).astype(jnp.int32)
    me = (4 * ax + 2 * ay + ac).astype(jnp.int32).reshape(1)

    def stacked(a):
        return lax.dynamic_update_index_in_dim(jnp.zeros((N_CHIPS,) + a.shape, F32), a, s, 0)

    def gather_start(tag, w_in, w_out, layer, small, dep):
        fulls = [_cast_into_full(w_in, layer, "col", place, name=f"cast_{tag}_in", dep=dep),
                 _cast_into_full(w_out, layer, "row", place, name=f"cast_{tag}_out", dep=dep)]
        return _start(f"ag_{tag}_ici", fulls + small, _gather_ici_plan(len(small)), (3 * (2 + len(small)),), dep=dep)

    def gather_forward(tag, started, n_small, after):
        bufs = _wait(f"ag_{tag}_ici_wait", started, _gather_ici_plan(n_small), after)
        return _start(f"ag_{tag}_d2d", bufs[:2], _gather_d2d_plan, (3 * 2,)), bufs[2:]

    def gather_finish(tag, forwarded, after):
        return _wait(f"ag_{tag}_d2d_wait", forwarded, _gather_d2d_plan, after)

    small = [stacked(a_conv[0]), stacked(b_norm), stacked(b_vnorm), stacked(f_conv_w.reshape(2 * 3, -1))]
    ag_a = gather_start("a", a_in, a_out, 0, small, None)
    ag_f0 = gather_start("f0", f_up, f_down, 0, [], ag_a[3])
    ag_b = gather_start("b", b_in, b_out, 0, [], ag_f0[3])
    ag_f1 = gather_start("f1", f_up, f_down, 1, [], ag_b[3])

    def unshard(a):
        return jnp.transpose(a, (1, 0, 2)).reshape(a.shape[1], -1)

    ws, bs = b_ws[0], b_bs[0]

    h0 = _rms_fwd(x0, a_norm, dep=ag_f1[3], name="a_norm")
    fw_a, (g_aconv, g_bnorm, g_bvnorm, g_fconv) = gather_forward("a", ag_a, 4, h0)
    w_ai, w_ao = gather_finish("a", fw_a, fw_a[3])
    a_conv_f, b_norm_f, b_vnorm_f = unshard(g_aconv), unshard(g_bnorm), unshard(g_bvnorm)
    f_conv_f = unshard(g_fconv).reshape(2, 3, f2)
    bcx = _mm(h0, w_ai, name="a_in")
    y = _a_mid_fwd(bcx, a_conv_f, name="a_mid")
    x1 = _mm(y, w_ao, res=x0, name="a_out")
    fw_f0, _ = gather_forward("f0", ag_f0, 0, x1)
    w_up0, w_dn0 = gather_finish("f0", fw_f0, fw_f0[3])
    x2, (h1, up0, act0) = _ffn_fwd(x1, f_norm[0:1], w_up0, f_conv_f[0], f_conv_b[0:1], w_dn0, 0)
    fw_b, _ = gather_forward("b", ag_b, 0, up0)
    w_bi, w_bo = gather_finish("b", fw_b, act0)
    h2 = _rms_fwd(x2, b_norm_f, name="b_norm")
    zp = _mm(h2, w_bi, name="b_in")
    fw_f1, _ = gather_forward("f1", ag_f1, 0, zp)
    ug = _b_mid_fwd(zp, b_vnorm_f, ws, bs, name="b_mid")
    x3 = _mm(ug, w_bo, res=x2, name="b_out")
    w_up1, w_dn1 = gather_finish("f1", fw_f1, x3)
    x4, (h3, up1, act1) = _ffn_fwd(x3, f_norm[1:2], w_up1, f_conv_f[1], f_conv_b[1:2], w_dn1, 1)
    loss_rows, dx4, dx4b, d_final = _final(x4, tgt, final_norm.reshape(1, d), name="final")

    d_dn1 = _mm(act1, dx4b, ta=True, tm=1408, out_dtype=BF16, name="ffn1_ddown")
    dact1 = _mm(dx4b, w_dn1, tb=True, tn=512, tm=2048, name="ffn1_dact")
    dup1, d_fwb1 = _ffn_mid_bwd(up1, dact1, f_conv_f[1], f_conv_b[1:2], name="ffn1_mid_bwd")
    d_up1 = _mm(h3, dup1, ta=True, out_dtype=BF16, name="ffn1_dup")
    sw_f1 = _rs_swap("f1", [d_up1, d_dn1])
    dh3 = _mm(dup1, w_up1, tb=True, tk=2816, dep=sw_f1[3], name="ffn1_dh")
    dx3, dx3b, d_fnorm1 = _rms_bwd(dh3, x3, f_norm[1:2], dx4, name="ffn1_norm_bwd")
    ex_f1 = _rs_exchange("f1", sw_f1, place, dx3)

    d_bo = _mm(ug, dx3b, ta=True, out_dtype=BF16, dep=ex_f1[0][3], name="b_dout")
    dug = _mm(dx3b, w_bo, tb=True, name="b_dug")
    dzp, d_ws, d_bs, d_bvnorm = _b_mid_bwd(zp, dug, b_vnorm_f, ws, bs, name="b_mid_bwd")
    d_bi = _mm(h2, dzp, ta=True, out_dtype=BF16, name="b_din")
    sw_b = _rs_swap("b", [d_bi, d_bo])
    dh2 = _mm(dzp, w_bi, tb=True, dep=sw_b[3], name="b_dh")
    dx2, dx2b, d_bnorm = _rms_bwd(dh2, x2, b_norm_f, dx3, name="b_norm_bwd")
    ex_b = _rs_exchange("b", sw_b, place, dx2)

    d_dn0 = _mm(act0, dx2b, ta=True, tm=1408, out_dtype=BF16, dep=ex_b[0][3], name="ffn0_ddown")
    dact0 = _mm(dx2b, w_dn0, tb=True, tn=512, tm=2048, name="ffn0_dact")
    dup0, d_fwb0 = _ffn_mid_bwd(up0, dact0, f_conv_f[0], f_conv_b[0:1], name="ffn0_mid_bwd")
    sh_f1 = _rs_share("f1", ex_f1, place, dup0)
    d_up0 = _mm(h1, dup0, ta=True, out_dtype=BF16, dep=sh_f1[3], name="ffn0_dup")
    sw_f0 = _rs_swap("f0", [d_up0, d_dn0])
    g_up1, g_dn1 = _rs_finish("f1", sh_f1, sw_f0[3])
    dh1 = _mm(dup0, w_up0, tb=True, tk=2816, dep=sw_f0[3], name="ffn0_dh")
    dx1, dx1b, d_fnorm0 = _rms_bwd(dh1, x1, f_norm[0:1], dx2, name="ffn0_norm_bwd")
    ex_f0 = _rs_exchange("f0", sw_f0, place, dx1)
    early = [jnp.concatenate([d_bnorm, d_bvnorm, d_fnorm0, d_fnorm1, d_final, loss_rows], axis=0),
             jnp.concatenate([d_fwb0, d_fwb1], axis=0), jnp.concatenate([d_ws.reshape(-1, CHUNK), d_bs], axis=0)]
    sp_early = _spread("early", early, ex_f0[0][3])
    sh_b = _rs_share("b", ex_b, place, sp_early[3])

    d_ao = _mm(y, dx1b, ta=True, out_dtype=BF16, dep=sh_b[3], name="a_dout")
    dyy = _mm(dx1b, w_ao, tb=True, name="a_dy")
    dbcx, d_aconv = _a_mid_bwd(bcx, dyy, a_conv_f, name="a_mid_bwd")
    d_ai = _mm(h0, dbcx, ta=True, out_dtype=BF16, name="a_din")
    sw_a = _rs_swap("a", [d_ai, d_ao])
    g_bi, g_bo = _rs_finish("b", sh_b, sw_a[3])
    dh0 = _mm(dbcx, w_ai, tb=True, dep=sw_a[3], name="a_dh")
    ex_a = _rs_exchange("a", sw_a, place, dh0)
    grad_x, _, d_anorm = _rms_bwd(dh0, x0, a_norm, dx1, name="a_norm_bwd")
    late = [jnp.concatenate([d_anorm, d_aconv], axis=0)]
    sp_late = _spread("late", late, ex_a[0][3])
    sh_f0 = _rs_share("f0", ex_f0, place, sp_late[3])
    sh_a = _rs_share("a", ex_a, place, sh_f0[3])
    g_up0, g_dn0 = _rs_finish("f0", sh_f0, sh_a[3])
    g_ai, g_ao = _rs_finish("a", sh_a, g_up0)
    r_a, r_b, r_c = _spread_sum("early", sp_early, early, me, g_ai)
    (r_l,) = _spread_sum("late", sp_late, late, me, r_a)

    loss = jnp.sum(r_a[40])
    cs, fs = d // N_CHIPS, f2 // N_CHIPS

    def mine(a, width):
        return lax.dynamic_slice_in_dim(a, s * width, width, axis=1)

    grads = {
        "a_norm": r_l[0:1], "a_conv": mine(r_l[8:11], cs), "b_norm": mine(r_a[0:1], cs), "b_vnorm": mine(r_a[8:9], cs),
        "f_norm": jnp.concatenate([r_a[16:17], r_a[24:25]], axis=0), "final_norm": r_a[32:33],
        "b_ws": r_c[:GROUPS * CHUNK], "b_bs": r_c[GROUPS * CHUNK:],
        "f_conv_w": jnp.concatenate([mine(r_b[0:3], fs), mine(r_b[8:11], fs)], axis=0),
        "f_conv_b": jnp.concatenate([r_b[3:4], r_b[11:12]], axis=0),
        "a_in": g_ai, "a_out": g_ao, "b_in": g_bi, "b_out": g_bo,
    }
    names = ["a_norm", "a_in", "a_conv", "a_out", "b_norm", "b_in", "b_vnorm", "b_ws", "b_bs", "b_out", "f_norm", "f_up",
             "f_conv_w", "f_conv_b", "f_down", "final_norm"]
    weights = dict(zip(names, [a_norm, a_in, a_conv, a_out, b_norm, b_in, b_vnorm, b_ws, b_bs, b_out, f_norm, f_up, f_conv_w,
                               f_conv_b, f_down, final_norm]))
    ms = dict(zip(names, [m_a_norm, m_a_in, m_a_conv, m_a_out, m_b_norm, m_b_in, m_b_vnorm, m_b_ws, m_b_bs, m_b_out, m_f_norm,
                          m_f_up, m_f_conv_w, m_f_conv_b, m_f_down, m_final_norm]))
    vs = dict(zip(names, [v_a_norm, v_a_in, v_a_conv, v_a_out, v_b_norm, v_b_in, v_b_vnorm, v_b_ws, v_b_bs, v_b_out, v_f_norm,
                          v_f_up, v_f_conv_w, v_f_conv_b, v_f_down, v_final_norm]))
    result = {}
    for n in names:
        w = weights[n]
        if n in ("f_up", "f_down"):
            g1, g0 = (g_up1, g_up0) if n == "f_up" else (g_dn1, g_dn0)
            first = _adamw_layer(w, g1, ms[n], vs[n], 1, None, name=f"adamw_{n}1")
            result[n] = _adamw_layer(w, g0, ms[n], vs[n], 0, tuple(first), name=f"adamw_{n}0")
            continue
        g2 = grads[n]
        as3d = (lambda a: a.reshape((1,) + g2.shape))
        result[n] = [o.reshape(w.shape) for o in _adamw_layer(as3d(w), g2, as3d(ms[n]), as3d(vs[n]), 0, None, name=f"adamw_{n}")]

    return (loss, grad_x.reshape(x.shape), *[result[n][0] for n in names], *[result[n][1] for n in names],
            *[result[n][2] for n in names], *[result[n][3] for n in names])
```

```python
import functools

import jax
import jax.numpy as jnp
from jax import lax
from jax.experimental import pallas as pl
from jax.experimental.pallas import tpu as pltpu

F32 = jnp.float32
BF16 = jnp.bfloat16
MESH = pl.DeviceIdType.MESH
ANY = pl.BlockSpec(memory_space=pl.ANY)

RMS_EPS = 1e-5
CHUNK = 128
GROUPS = 8
ADAM_LR, ADAM_B1, ADAM_B2, ADAM_EPS, ADAM_WD, ADAM_STEP = 0.001, 0.9, 0.999, 1e-08, 0.01, 10

N_CHIPS = 4
HALO = 8
VMEM_LIMIT = 56 * 1024 * 1024
GELU_C = 0.7978845608028654
GELU_A = 0.044715


def _params(sem=None):
    return pltpu.CompilerParams(dimension_semantics=sem, vmem_limit_bytes=VMEM_LIMIT)


def _tile(dim, pref, quantum=128):
    if dim <= pref:
        return dim
    t = (pref // quantum) * quantum
    while t >= quantum:
        if dim % t == 0:
            return t
        t -= quantum
    return dim


def _mm(a, b, *, name, ta=False, tb=False, res=None, dep=None, out_dtype=F32, tm=1024, tn=1024, tk=2048):
    (K, M) = a.shape if ta else a.shape[::-1]
    parts = b if isinstance(b, tuple) else (b,)
    n_part = parts[0].shape[0] if tb else parts[0].shape[1]
    N = n_part * len(parts)
    assert (parts[0].shape[1] if tb else parts[0].shape[0]) == K and not (tb and len(parts) > 1)
    tm, tn, tk = _tile(M, tm), _tile(n_part, tn), _tile(K, tk)
    nk, nj = K // tk, n_part // tn
    a_spec = pl.BlockSpec((tk, tm), lambda i, j, k: (k, i)) if ta else pl.BlockSpec((tm, tk), lambda i, j, k: (i, k))
    if len(parts) == 1:
        b_specs = [pl.BlockSpec((tn, tk), lambda i, j, k: (j, k)) if tb else pl.BlockSpec((tk, tn), lambda i, j, k: (k, j))]
    else:
        def part_spec(p):
            def index(i, j, k):
                mine = jnp.logical_and(j >= p * nj, j < (p + 1) * nj)
                return (jnp.where(mine, k, 0), jnp.where(mine, j - p * nj, 0))
            return pl.BlockSpec((tk, tn), index)
        b_specs = [part_spec(p) for p in range(len(parts))]
    o_spec = pl.BlockSpec((tm, tn), lambda i, j, k: (i, j))
    dims = (((0 if ta else 1,), (1 if tb else 0,)), ((), ()))
    direct = out_dtype == F32
    n_b = len(parts)

    def body(*refs):
        a_ref, b_refs = refs[0], refs[1:1 + n_b]
        r_ref = refs[1 + n_b] if res is not None else None
        o_ref = refs[1 + n_b + (res is not None) + (dep is not None)]
        acc_ref = o_ref if direct else refs[-1]

        def step(b_ref):
            part = lax.dot_general(a_ref[...], b_ref[...], dims, preferred_element_type=F32)
            if nk == 1:
                if r_ref is not None:
                    part = part + r_ref[...]
                o_ref[...] = part.astype(o_ref.dtype)
                return
            k = pl.program_id(2)

            @pl.when(k == 0)
            def _():
                acc_ref[...] = part

            @pl.when(jnp.logical_and(k > 0, k < nk - 1))
            def _():
                acc_ref[...] += part

            @pl.when(k == nk - 1)
            def _():
                tot = acc_ref[...] + part
                if r_ref is not None:
                    tot = tot + r_ref[...]
                o_ref[...] = tot.astype(o_ref.dtype)

        if n_b == 1:
            step(b_refs[0])
        else:
            for p in range(n_b):
                pl.when(pl.program_id(1) // nj == p)(functools.partial(step, b_refs[p]))

    in_specs = [a_spec] + b_specs + ([o_spec] if res is not None else []) + ([ANY] if dep is not None else [])
    args = (a,) + parts + ((res,) if res is not None else ()) + ((dep,) if dep is not None else ())
    scratch = [] if (direct or nk == 1) else [pltpu.VMEM((tm, tn), F32)]
    return pl.pallas_call(
        body, grid=(M // tm, N // tn, nk), in_specs=in_specs, out_specs=o_spec,
        out_shape=jax.ShapeDtypeStruct((M, N), out_dtype), scratch_shapes=scratch,
        compiler_params=_params(("parallel", "parallel", "arbitrary")), name=name,
    )(*args)


def _row_spec(rb, w):
    return pl.BlockSpec((rb, w), lambda i: (i, 0))


def _prev_spec(rb, w):
    return pl.BlockSpec((HALO, w), lambda i: (jnp.maximum(i * (rb // HALO) - 1, 0), 0))


def _next_spec(rb, w, t):
    return pl.BlockSpec((HALO, w), lambda i: (jnp.minimum((i + 1) * (rb // HALO), t // HALO - 1), 0))


def _full_spec(shape):
    return pl.BlockSpec(shape, lambda i: tuple(0 for _ in shape))


def _shift(e, s):
    return pltpu.roll(e, s % e.shape[0], 0)


def _gelu(x):
    return 0.5 * x * (1.0 + jnp.tanh(GELU_C * (x + GELU_A * x * x * x)))


def _gelu_grad(x):
    th = jnp.tanh(GELU_C * (x + GELU_A * x * x * x))
    return 0.5 * (1.0 + th) + 0.5 * x * (1.0 - th * th) * (GELU_C * (1.0 + 3.0 * GELU_A * x * x))


def _sigmoid(x):
    return 1.0 / (1.0 + jnp.exp(-x))


def _rms_fwd(x, g, *, name, dep=None, rb=256):
    t, d = x.shape
    rb = _tile(t, rb, 8)

    def body(x_ref, g_ref, *rest):
        h_ref = rest[-1]
        xv = x_ref[...]
        r = lax.rsqrt(jnp.mean(xv * xv, axis=-1, keepdims=True) + RMS_EPS)
        h_ref[...] = ((xv * r) * g_ref[...]).astype(BF16)

    return pl.pallas_call(
        body, grid=(t // rb,), in_specs=[_row_spec(rb, d), _full_spec((1, d))] + ([ANY] if dep is not None else []),
        out_specs=_row_spec(rb, d), out_shape=jax.ShapeDtypeStruct((t, d), BF16), compiler_params=_params(("parallel",)), name=name,
    )(x, g, *(() if dep is None else (dep,)))


def _rms_bwd(dh, x, g, dres, *, name, dep=None, rb=256):
    t, d = x.shape
    rb = _tile(t, rb, 8)

    def body(dh_ref, x_ref, g_ref, dres_ref, *rest):
        dx_ref, dxb_ref, dg_ref = rest[-3:]
        xv = x_ref[...]
        r = lax.rsqrt(jnp.mean(xv * xv, axis=-1, keepdims=True) + RMS_EPS)
        xhat = xv * r
        dh_v = dh_ref[...]
        dxhat = dh_v * g_ref[...]
        m = jnp.mean(dxhat * xhat, axis=-1, keepdims=True)
        dx = dres_ref[...] + r * (dxhat - xhat * m)
        dx_ref[...] = dx
        dxb_ref[...] = dx.astype(BF16)

        @pl.when(pl.program_id(0) == 0)
        def _():
            dg_ref[...] = jnp.zeros_like(dg_ref)

        dg_ref[0:1, :] += jnp.sum(dh_v * xhat, axis=0, keepdims=True)

    return pl.pallas_call(
        body, grid=(t // rb,),
        in_specs=[_row_spec(rb, d), _row_spec(rb, d), _full_spec((1, d)), _row_spec(rb, d)] + ([ANY] if dep is not None else []),
        out_specs=[_row_spec(rb, d), _row_spec(rb, d), _full_spec((8, d))],
        out_shape=[jax.ShapeDtypeStruct((t, d), F32), jax.ShapeDtypeStruct((t, d), BF16), jax.ShapeDtypeStruct((8, d), F32)],
        compiler_params=_params(("arbitrary",)), name=name,
    )(dh, x, g, dres, *(() if dep is None else (dep,)))


def _final(x, tgt, g, *, name, rb=256):
    t, d = x.shape
    rb = _tile(t, rb, 8)
    inv_d = 1.0 / d

    def body(x_ref, t_ref, g_ref, l_ref, dx_ref, dxb_ref, dg_ref):
        xv = x_ref[...]
        gv = g_ref[...]
        r = lax.rsqrt(jnp.mean(xv * xv, axis=-1, keepdims=True) + RMS_EPS)
        xhat = xv * r
        e = xhat * gv - t_ref[...]
        dy = e * inv_d
        dxhat = dy * gv
        m = jnp.mean(dxhat * xhat, axis=-1, keepdims=True)
        dx = r * (dxhat - xhat * m)
        dx_ref[...] = dx
        dxb_ref[...] = dx.astype(BF16)

        @pl.when(pl.program_id(0) == 0)
        def _():
            l_ref[...] = jnp.zeros_like(l_ref)
            dg_ref[...] = jnp.zeros_like(dg_ref)

        l_ref[0:1, :] += jnp.sum(e * e, axis=0, keepdims=True) * (0.5 * inv_d)
        dg_ref[0:1, :] += jnp.sum(dy * xhat, axis=0, keepdims=True)

    return pl.pallas_call(
        body, grid=(t // rb,),
        in_specs=[_row_spec(rb, d), _row_spec(rb, d), _full_spec((1, d))],
        out_specs=[_full_spec((8, d)), _row_spec(rb, d), _row_spec(rb, d), _full_spec((8, d))],
        out_shape=[jax.ShapeDtypeStruct((8, d), F32), jax.ShapeDtypeStruct((t, d), F32),
                   jax.ShapeDtypeStruct((t, d), BF16), jax.ShapeDtypeStruct((8, d), F32)],
        compiler_params=_params(("arbitrary",)), name=name,
    )(x, tgt, g)


def _a_mid_fwd(bcx, wconv, *, name, rb=256, cw=512):
    t, d3 = bcx.shape
    d = d3 // 3
    rb, cw = _tile(t, rb, 8), _tile(d, cw)

    def body(cur_ref, prev_ref, w_ref, y_ref):
        first = pl.program_id(0) == 0
        for c0 in range(0, d, cw):
            cs = slice(c0, c0 + cw)
            gc, xs = slice(d + c0, d + c0 + cw), slice(2 * d + c0, 2 * d + c0 + cw)
            p_prev = jnp.where(first, 0.0, prev_ref[:, gc] * prev_ref[:, xs])
            e = jnp.concatenate([p_prev, cur_ref[:, gc] * cur_ref[:, xs]], axis=0)
            w = w_ref[:, cs]
            q = w[0:1] * _shift(e, 2) + w[1:2] * _shift(e, 1) + w[2:3] * e
            y_ref[:, cs] = (cur_ref[:, cs] * q[HALO:]).astype(BF16)

    return pl.pallas_call(
        body, grid=(t // rb,),
        in_specs=[_row_spec(rb, d3), _prev_spec(rb, d3), _full_spec((3, d))], out_specs=_row_spec(rb, d),
        out_shape=jax.ShapeDtypeStruct((t, d), BF16), compiler_params=_params(("parallel",)), name=name,
    )(bcx, bcx, wconv)


def _a_mid_bwd(bcx, dy, wconv, *, name, rb=128, cw=512):
    t, d3 = bcx.shape
    d = d3 // 3
    rb, cw = _tile(t, rb, 8), _tile(d, cw)

    def body(cur_ref, prev_ref, next_ref, dy_ref, dyn_ref, w_ref, o_ref, dw_ref):
        i = pl.program_id(0)
        first, last = i == 0, i == pl.num_programs(0) - 1

        @pl.when(first)
        def _():
            dw_ref[...] = jnp.zeros_like(dw_ref)

        for c0 in range(0, d, cw):
            cs = slice(c0, c0 + cw)
            gc, xs = slice(d + c0, d + c0 + cw), slice(2 * d + c0, 2 * d + c0 + cw)
            zeros = jnp.zeros((HALO, cw), F32)
            gb_c, gc_c, xs_c = cur_ref[:, cs], cur_ref[:, gc], cur_ref[:, xs]
            p_prev = jnp.where(first, 0.0, prev_ref[:, gc] * prev_ref[:, xs])
            e = jnp.concatenate([p_prev, gc_c * xs_c, zeros], axis=0)
            dq_next = jnp.where(last, 0.0, dyn_ref[:, cs] * next_ref[:, cs])
            dy_c = dy_ref[:, cs]
            dq = jnp.concatenate([zeros, dy_c * gb_c, dq_next], axis=0)
            w = w_ref[:, cs]
            e1, e2 = _shift(e, 1), _shift(e, 2)
            q = w[0:1] * e2 + w[1:2] * e1 + w[2:3] * e
            dp = (w[2:3] * dq + w[1:2] * _shift(dq, -1) + w[0:1] * _shift(dq, -2))[HALO:HALO + rb]
            o_ref[:, cs] = (dy_c * q[HALO:HALO + rb]).astype(BF16)
            o_ref[:, gc] = (dp * xs_c).astype(BF16)
            o_ref[:, xs] = (dp * gc_c).astype(BF16)
            dq_c = dq[HALO:HALO + rb]
            dw_ref[0:1, cs] += jnp.sum(dq_c * e2[HALO:HALO + rb], axis=0, keepdims=True)
            dw_ref[1:2, cs] += jnp.sum(dq_c * e1[HALO:HALO + rb], axis=0, keepdims=True)
            dw_ref[2:3, cs] += jnp.sum(dq_c * e[HALO:HALO + rb], axis=0, keepdims=True)

    return pl.pallas_call(
        body, grid=(t // rb,),
        in_specs=[_row_spec(rb, d3), _prev_spec(rb, d3), _next_spec(rb, d3, t), _row_spec(rb, d), _next_spec(rb, d, t),
                  _full_spec((3, d))],
        out_specs=[_row_spec(rb, d3), _full_spec((8, d))],
        out_shape=[jax.ShapeDtypeStruct((t, d3), BF16), jax.ShapeDtypeStruct((8, d), F32)],
        compiler_params=_params(("arbitrary",)), name=name,
    )(bcx, bcx, bcx, dy, dy, wconv)


def _ffn_mid_fwd(up, wconv, bconv, *, name, rb=128, cw=512):
    t, f2 = up.shape
    f = f2 // 2
    rb, cw = _tile(t, rb, 8), _tile(f, cw)

    def body(cur_ref, prev_ref, w_ref, b_ref, act_ref, up_ref, conv_ref):
        first = pl.program_id(0) == 0

        def conv(cols):
            cur = cur_ref[:, cols]
            up_ref[:, cols] = cur.astype(BF16)
            e = jnp.concatenate([jnp.where(first, 0.0, prev_ref[:, cols]), cur], axis=0)
            w = w_ref[:, cols]
            out = (w[0:1] * _shift(e, 2) + w[1:2] * _shift(e, 1) + w[2:3] * e + b_ref[:, cols])[HALO:]
            conv_ref[:, cols] = out.astype(BF16)
            return out

        for c0 in range(0, f, cw):
            g = conv(slice(c0, c0 + cw))
            a = conv(slice(f + c0, f + c0 + cw))
            act_ref[:, c0:c0 + cw] = (g * _sigmoid(g) * a).astype(BF16)

    return pl.pallas_call(
        body, grid=(t // rb,),
        in_specs=[_row_spec(rb, f2), _prev_spec(rb, f2), _full_spec((3, f2)), _full_spec((1, f2))],
        out_specs=[_row_spec(rb, f), _row_spec(rb, f2), _row_spec(rb, f2)],
        out_shape=[jax.ShapeDtypeStruct((t, f), BF16), jax.ShapeDtypeStruct((t, f2), BF16), jax.ShapeDtypeStruct((t, f2), BF16)],
        compiler_params=_params(("parallel",)), name=name,
    )(up, up, wconv, bconv)


def _ffn_bwd_core(dxb, w_down, conv, up, w_up, wconv, *, name, tm=512, cw=512):
    t, d = dxb.shape
    f2 = conv.shape[1]
    f = f2 // 2
    tm, cw = _tile(t, tm, 8), _tile(f, cw)
    n_i, n_c = t // tm, f // cw
    nt = (((1,), (1,)), ((), ()))

    def body(dx_ref, wd_ref, cg_ref, ca_ref, ug_ref, ua_ref, wug_ref, wua_ref, wg_ref, wa_ref,
             dupg_ref, dupa_ref, dh_ref, dwb_ref, carry_g, carry_a):
        i, c = pl.program_id(0), pl.program_id(1)

        @pl.when(jnp.logical_and(i == 0, c == 0))
        def _():
            dwb_ref[...] = jnp.zeros_like(dwb_ref)

        dact = lax.dot_general(dx_ref[...], wd_ref[...], nt, preferred_element_type=F32)
        g, a = cg_ref[...].astype(F32), ca_ref[...].astype(F32)
        sg = _sigmoid(g)

        def back(dc, carry, u_ref, w_ref, out_ref, half):
            w = w_ref[...]
            e = jnp.concatenate([dc, jnp.where(i == 0, 0.0, carry[c])], axis=0)
            carry[c] = dc[0:HALO]
            e1, e2 = _shift(e, -1)[:tm], _shift(e, -2)[:tm]
            dup = (w[2:3] * dc + w[1:2] * e1 + w[0:1] * e2).astype(BF16)
            out_ref[...] = dup
            u = u_ref[...].astype(F32)
            k = c + half * n_c
            dwb_ref[k, 0:1, :] += jnp.sum(e2 * u, axis=0, keepdims=True)
            dwb_ref[k, 1:2, :] += jnp.sum(e1 * u, axis=0, keepdims=True)
            dwb_ref[k, 2:3, :] += jnp.sum(dc * u, axis=0, keepdims=True)
            dwb_ref[k, 3:4, :] += jnp.sum(dc, axis=0, keepdims=True)
            return dup

        dup_g = back(dact * a * (sg * (1.0 + g * (1.0 - sg))), carry_g, ug_ref, wg_ref, dupg_ref, 0)
        dup_a = back(dact * (g * sg), carry_a, ua_ref, wa_ref, dupa_ref, 1)
        part = (lax.dot_general(dup_g, wug_ref[...], nt, preferred_element_type=F32)
                + lax.dot_general(dup_a, wua_ref[...], nt, preferred_element_type=F32))

        @pl.when(c == 0)
        def _():
            dh_ref[...] = part

        @pl.when(c > 0)
        def _():
            dh_ref[...] += part

    def rows(i):
        return n_i - 1 - i

    act_g = pl.BlockSpec((tm, cw), lambda i, c: (rows(i), c))
    act_a = pl.BlockSpec((tm, cw), lambda i, c: (rows(i), c + n_c))
    return pl.pallas_call(
        body, grid=(n_i, n_c),
        in_specs=[pl.BlockSpec((tm, d), lambda i, c: (rows(i), 0)), pl.BlockSpec((cw, d), lambda i, c: (c, 0)),
                  act_g, act_a, act_g, act_a,
                  pl.BlockSpec((d, cw), lambda i, c: (0, c)), pl.BlockSpec((d, cw), lambda i, c: (0, c + n_c)),
                  pl.BlockSpec((3, cw), lambda i, c: (0, c)), pl.BlockSpec((3, cw), lambda i, c: (0, c + n_c))],
        out_specs=[act_g, act_g, pl.BlockSpec((tm, d), lambda i, c: (rows(i), 0)),
                   pl.BlockSpec((2 * n_c, 8, cw), lambda i, c: (0, 0, 0))],
        out_shape=[jax.ShapeDtypeStruct((t, f), BF16), jax.ShapeDtypeStruct((t, f), BF16), jax.ShapeDtypeStruct((t, d), F32),
                   jax.ShapeDtypeStruct((2 * n_c, 8, cw), F32)],
        scratch_shapes=[pltpu.VMEM((n_c, HALO, cw), F32), pltpu.VMEM((n_c, HALO, cw), F32)],
        compiler_params=_params(("arbitrary", "arbitrary")), name=name,
    )(dxb, w_down, conv, conv, up, up, w_up, w_up, wconv, wconv)


def _causal_mask():
    return lax.broadcasted_iota(jnp.int32, (CHUNK, CHUNK), 0) >= lax.broadcasted_iota(jnp.int32, (CHUNK, CHUNK), 1)


def _b_mid_fwd(zp, vnorm, ws, bs, *, name, rb=256):
    t, d2 = zp.shape
    d = d2 // 2
    c = d // GROUPS
    rb = _tile(t, rb, CHUNK)

    def body(zp_ref, gv_ref, ws_ref, bs_ref, ug_ref, vn_ref, gate_ref):
        v = _gelu(zp_ref[:, d:])
        rv = lax.rsqrt(jnp.mean(v * v, axis=-1, keepdims=True) + RMS_EPS)
        vn_ref[...] = ((v * rv) * gv_ref[...]).astype(BF16)
        mask = _causal_mask()
        for h in range(GROUPS):
            hc = slice(h * c, (h + 1) * c)
            wm = jnp.where(mask, ws_ref[h], 0.0).astype(BF16)
            bcol = jnp.broadcast_to(bs_ref[h:h + 1, :], (CHUNK, CHUNK)).T[:, 0:1]
            for n in range(rb // CHUNK):
                rows = slice(n * CHUNK, (n + 1) * CHUNK)
                gate_ref[rows, hc] = jnp.dot(wm, vn_ref[rows, hc], preferred_element_type=F32) + bcol
        ug_ref[...] = (_gelu(zp_ref[:, :d]) * gate_ref[...]).astype(BF16)

    return pl.pallas_call(
        body, grid=(t // rb,),
        in_specs=[_row_spec(rb, d2), _full_spec((1, d)), _full_spec((GROUPS, CHUNK, CHUNK)), _full_spec((GROUPS, CHUNK))],
        out_specs=_row_spec(rb, d), out_shape=jax.ShapeDtypeStruct((t, d), BF16),
        scratch_shapes=[pltpu.VMEM((rb, d), BF16), pltpu.VMEM((rb, d), F32)],
        compiler_params=_params(("parallel",)), name=name,
    )(zp, vnorm, ws, bs)


def _b_mid_bwd(zp, dug, vnorm, ws, bs, *, name, rb=256):
    t, d2 = zp.shape
    d = d2 // 2
    c = d // GROUPS
    rb = _tile(t, rb, CHUNK)

    def body(zp_ref, dug_ref, gv_ref, ws_ref, bs_ref, dzp_ref, dws_ref, dbs_ref, dgv_ref,
             vn_ref, gate_ref, dm_ref, dvn_ref, dbacc_ref):
        i = pl.program_id(0)

        @pl.when(i == 0)
        def _():
            dws_ref[...] = jnp.zeros_like(dws_ref)
            dgv_ref[...] = jnp.zeros_like(dgv_ref)
            dbacc_ref[...] = jnp.zeros_like(dbacc_ref)

        zu, zv = zp_ref[:, :d], zp_ref[:, d:]
        u, v = _gelu(zu), _gelu(zv)
        rv = lax.rsqrt(jnp.mean(v * v, axis=-1, keepdims=True) + RMS_EPS)
        vhat = v * rv
        gv = gv_ref[...]
        vn_ref[...] = (vhat * gv).astype(BF16)
        dug_v = dug_ref[...]
        dm = dug_v * u
        dm_ref[...] = dm.astype(BF16)
        mask = _causal_mask()
        for h in range(GROUPS):
            hc = slice(h * c, (h + 1) * c)
            wm = jnp.where(mask, ws_ref[h], 0.0)
            wm_b, wmt_b = wm.astype(BF16), wm.T.astype(BF16)
            bcol = jnp.broadcast_to(bs_ref[h:h + 1, :], (CHUNK, CHUNK)).T[:, 0:1]
            dws_h = jnp.zeros((CHUNK, CHUNK), F32)
            dbs_h = jnp.zeros((CHUNK, c), F32)
            for n in range(rb // CHUNK):
                rows = slice(n * CHUNK, (n + 1) * CHUNK)
                vn_c, dm_c = vn_ref[rows, hc], dm_ref[rows, hc]
                gate_ref[rows, hc] = jnp.dot(wm_b, vn_c, preferred_element_type=F32) + bcol
                dws_h += lax.dot_general(dm_c, vn_c, (((1,), (1,)), ((), ())), preferred_element_type=F32)
                dvn_ref[rows, hc] = jnp.dot(wmt_b, dm_c, preferred_element_type=F32)
                dbs_h += dm[rows, hc]
            dws_ref[h] += dws_h
            dbacc_ref[h] += dbs_h
        du = dug_v * gate_ref[...]
        dvn = dvn_ref[...]
        dvhat = dvn * gv
        m = jnp.mean(dvhat * vhat, axis=-1, keepdims=True)
        dv = rv * (dvhat - vhat * m)
        dgv_ref[0:1, :] += jnp.sum(dvn * vhat, axis=0, keepdims=True)
        dzp_ref[:, :d] = (du * _gelu_grad(zu)).astype(BF16)
        dzp_ref[:, d:] = (dv * _gelu_grad(zv)).astype(BF16)

        @pl.when(i == pl.num_programs(0) - 1)
        def _():
            ones = jnp.ones((8, c), F32)
            for h in range(GROUPS):
                dws_ref[h] = jnp.where(mask, dws_ref[h], 0.0)
                row = lax.dot_general(ones, dbacc_ref[h], (((1,), (1,)), ((), ())),
                                      precision=lax.Precision.HIGHEST, preferred_element_type=F32)
                dbs_ref[h:h + 1, :] = row[0:1]

    return pl.pallas_call(
        body, grid=(t // rb,),
        in_specs=[_row_spec(rb, d2), _row_spec(rb, d), _full_spec((1, d)), _full_spec((GROUPS, CHUNK, CHUNK)),
                  _full_spec((GROUPS, CHUNK))],
        out_specs=[_row_spec(rb, d2), _full_spec((GROUPS, CHUNK, CHUNK)), _full_spec((GROUPS, CHUNK)), _full_spec((8, d))],
        out_shape=[jax.ShapeDtypeStruct((t, d2), BF16), jax.ShapeDtypeStruct((GROUPS, CHUNK, CHUNK), F32),
                   jax.ShapeDtypeStruct((GROUPS, CHUNK), F32), jax.ShapeDtypeStruct((8, d), F32)],
        scratch_shapes=[pltpu.VMEM((rb, d), BF16), pltpu.VMEM((rb, d), F32), pltpu.VMEM((rb, d), BF16),
                        pltpu.VMEM((rb, d), F32), pltpu.VMEM((GROUPS, CHUNK, c), F32)],
        compiler_params=_params(("arbitrary",)), name=name,
    )(zp, dug, vnorm, ws, bs)


def _cast_into_full(w, layer, kind, place, *, name, dep=None, rb=256):
    _, r, c = w.shape
    rb = _tile(r, rb, 16)
    nrb = r // rb
    full = (r, c * N_CHIPS) if kind == "col" else (r * N_CHIPS, c)

    def body(place_ref, w_ref, *rest):
        rest[-1][...] = w_ref[...].astype(BF16)

    def o_index(i, place):
        return (i, place[0]) if kind == "col" else (i + place[0] * nrb, 0)

    in_specs = [pl.BlockSpec((None, rb, c), lambda i, place: (layer, i, 0))] + ([ANY] if dep is not None else [])
    return pl.pallas_call(
        body,
        grid_spec=pltpu.PrefetchScalarGridSpec(num_scalar_prefetch=1, grid=(nrb,), in_specs=in_specs,
                                               out_specs=pl.BlockSpec((rb, c), o_index)),
        out_shape=jax.ShapeDtypeStruct(full, BF16), compiler_params=_params(("parallel",)), name=name,
    )(place, w, *(() if dep is None else (dep,)))


def _adamw_layer(w, g, m, v, layer, prev, *, name, rb=128):
    _, r, c = w.shape
    rb = _tile(r, rb, 8)
    c1 = 1.0 - ADAM_B1 ** ADAM_STEP
    c2 = 1.0 - ADAM_B2 ** ADAM_STEP

    def body(w_ref, g_ref, m_ref, v_ref, *rest):
        go_ref, d_ref, nm_ref, nv_ref = rest[-4:]
        gv = g_ref[...]
        nm = ADAM_B1 * m_ref[...] + (1.0 - ADAM_B1) * gv
        nv = ADAM_B2 * v_ref[...] + (1.0 - ADAM_B2) * (gv * gv)
        go_ref[...] = gv
        nm_ref[...] = nm
        nv_ref[...] = nv
        d_ref[...] = -ADAM_LR * ((nm / c1) / (jnp.sqrt(nv / c2) + ADAM_EPS) + ADAM_WD * w_ref[...])

    lay = pl.BlockSpec((None, rb, c), lambda i: (layer, i, 0))
    return pl.pallas_call(
        body, grid=(r // rb,), in_specs=[lay, _row_spec(rb, c), lay, lay] + ([ANY] * 4 if prev else []), out_specs=[lay] * 4,
        out_shape=[jax.ShapeDtypeStruct(w.shape, F32)] * 4, input_output_aliases={4 + k: k for k in range(4)} if prev else {},
        compiler_params=_params(("parallel",)), name=name,
    )(w, g, m, v, *(prev or ()))


HBM = pl.BlockSpec(memory_space=pltpu.HBM)
SEM = pl.BlockSpec(memory_space=pltpu.SEMAPHORE)
SIDE_EFFECT = pltpu.SideEffectType.DATAFLOW_SIDE_EFFECTING


def _place():
    x, y, c = lax.axis_index("x"), lax.axis_index("y"), lax.axis_index("c")
    chips = [(1 - x, y), (x, 1 - y), (1 - x, 1 - y)]
    return x, y, c, 2 * x + y, chips


def _half(ref, kind, c):
    r, w = ref.shape
    if kind == "col":
        return ref.at[pl.ds(pl.multiple_of(c * (r // 2), 8), r // 2), :]
    return ref.at[:, pl.ds(pl.multiple_of(c * (w // 2), 128), w // 2)]


def _shard(ref, kind, s):
    r, w = ref.shape
    if kind == "col":
        return ref.at[:, pl.ds(pl.multiple_of(s * (w // N_CHIPS), 128), w // N_CHIPS)]
    return ref.at[pl.ds(pl.multiple_of(s * (r // N_CHIPS), 8), r // N_CHIPS), :]


def _remote(src, dst, send_sem, recv_sem, dev):
    return pltpu.make_async_remote_copy(src_ref=src, dst_ref=dst, send_sem=send_sem, recv_sem=recv_sem,
                                        device_id=dev, device_id_type=MESH)


def _start(name, bufs, plan, sem_shape, dep=None):
    n = len(bufs)
    n_in = n + (dep is not None)

    def body(*refs):
        sends, _ = plan(refs[:n], refs[n_in], refs[n_in + 1])
        for cp in sends:
            cp.start()
        refs[n_in + 2 + n][...] = jnp.zeros((8, 128), F32)

    dma = pltpu.SemaphoreType.DMA
    outs = pl.pallas_call(
        body, name=name,
        out_shape=(dma(sem_shape), dma(sem_shape), *[pltpu.HBM(b.shape, b.dtype) for b in bufs], jax.ShapeDtypeStruct((8, 128), F32)),
        in_specs=(HBM,) * n + ((ANY,) if dep is not None else ()),
        out_specs=(SEM, SEM) + (HBM,) * n + (pl.BlockSpec(memory_space=pltpu.VMEM),),
        input_output_aliases={i: i + 2 for i in range(n)},
        compiler_params=pltpu.CompilerParams(has_side_effects=SIDE_EFFECT),
    )(*[pltpu.with_memory_space_constraint(b, pltpu.HBM) for b in bufs], *(() if dep is None else (dep,)))
    return outs[0], outs[1], list(outs[2:2 + n]), outs[2 + n]


def _wait(name, started, plan, after):
    send, recv, bufs, _ = started
    n = len(bufs)

    def body(*refs):
        sends, recvs = plan(refs[:n], refs[n], refs[n + 1])
        for cp in sends:
            cp.wait_send()
        for cp in recvs:
            cp.wait_recv()

    return list(pl.pallas_call(
        body, name=name, out_shape=tuple(pltpu.HBM(b.shape, b.dtype) for b in bufs),
        in_specs=(HBM,) * n + (SEM, SEM, ANY), out_specs=(HBM,) * n, input_output_aliases={i: i for i in range(n)},
        compiler_params=pltpu.CompilerParams(has_side_effects=SIDE_EFFECT),
    )(*bufs, send, recv, after))


KINDS = ("col", "row")


def _gather_ici_plan(n_small):
    def plan(refs, send, recv):
        x, y, c, s, chips = _place()
        n = len(KINDS) + n_small
        sends, recvs = [], []
        for k, (px, py) in enumerate(chips):
            sp = 2 * px + py
            for a, kind in enumerate(KINDS):
                mine, theirs = _half(_shard(refs[a], kind, s), kind, c), _half(_shard(refs[a], kind, sp), kind, c)
                sends.append(_remote(mine, mine, send.at[k * n + a], recv.at[k * n + a], (px, py, c)))
                recvs.append(_remote(theirs, theirs, send.at[k * n + a], recv.at[k * n + a], (px, py, c)))
            for b in range(n_small):
                ref, sem = refs[len(KINDS) + b], k * n + len(KINDS) + b
                sends.append(_remote(ref.at[s], ref.at[s], send.at[sem], recv.at[sem], (px, py, c)))
                recvs.append(_remote(ref.at[sp], ref.at[sp], send.at[sem], recv.at[sem], (px, py, c)))
        return sends, recvs
    return plan


def _gather_d2d_plan(refs, send, recv):
    x, y, c, _, chips = _place()
    n = len(KINDS)
    sends, recvs = [], []
    for k, (px, py) in enumerate(chips):
        for a, kind in enumerate(KINDS):
            region, sem = _shard(refs[a], kind, 2 * px + py), k * n + a
            sends.append(_remote(_half(region, kind, c), _half(region, kind, c), send.at[sem], recv.at[sem], (x, y, 1 - c)))
            recvs.append(_remote(_half(region, kind, 1 - c), _half(region, kind, 1 - c), send.at[sem], recv.at[sem], (x, y, 1 - c)))
    return sends, recvs


def _swap_plan(refs, send, recv):
    x, y, c, _, _ = _place()
    n = len(KINDS)
    cps = [_remote(_half(refs[a], KINDS[a], 1 - c), refs[n + a], send.at[a], recv.at[a], (x, y, 1 - c)) for a in range(n)]
    return cps, cps


def _exchange_plan(refs, send, recv):
    x, y, c, _, chips = _place()
    n = len(KINDS)
    cps = []
    for k, (px, py) in enumerate(chips):
        for a in range(n):
            cps.append(_remote(_shard(refs[a], KINDS[a], 2 * px + py), refs[n + a].at[k], send.at[k * n + a], recv.at[k * n + a],
                               (px, py, c)))
    return cps, cps


def _share_plan(refs, send, recv):
    x, y, c, _, _ = _place()
    sends = [_remote(_half(refs[a], KINDS[a], c), _half(refs[a], KINDS[a], c), send.at[a], recv.at[a], (x, y, 1 - c))
             for a in range(len(KINDS))]
    recvs = [_remote(_half(refs[a], KINDS[a], 1 - c), _half(refs[a], KINDS[a], 1 - c), send.at[a], recv.at[a], (x, y, 1 - c))
             for a in range(len(KINDS))]
    return sends, recvs


def _spread_plan(refs, send, recv):
    packed, slots = refs
    x, y, c, _, _ = _place()
    sends, recvs = [], []
    for k in range(1, 8):
        px, py, pc = x ^ (k >> 2), y ^ ((k >> 1) & 1), c ^ (k & 1)
        sends.append(_remote(packed, slots.at[4 * x + 2 * y + c], send.at[k - 1], recv.at[k - 1], (px, py, pc)))
        recvs.append(_remote(packed, slots.at[4 * px + 2 * py + pc], send.at[k - 1], recv.at[k - 1], (px, py, pc)))
    return sends, recvs


def _half_index(kind, nblk):
    def index(i, j, place):
        return (i + place[1] * nblk[0], j) if kind == "col" else (i, j + place[1] * nblk[1])
    return index


def _chip_partial(g, other, kind, place, *, name):
    hr, hc = other.shape
    rb, cb = _tile(hr, 512, 16), _tile(hc, 1024)
    nblk = (hr // rb, hc // cb)

    def body(place_ref, g_ref, o_ref, p_ref):
        p_ref[...] = (g_ref[...].astype(F32) + o_ref[...].astype(F32)).astype(BF16)

    plain = pl.BlockSpec((rb, cb), lambda i, j, place: (i, j))
    return pl.pallas_call(
        body,
        grid_spec=pltpu.PrefetchScalarGridSpec(
            num_scalar_prefetch=1, grid=nblk, in_specs=[pl.BlockSpec((rb, cb), _half_index(kind, nblk)), plain], out_specs=plain),
        out_shape=jax.ShapeDtypeStruct((hr, hc), BF16), compiler_params=_params(("parallel", "parallel")), name=name,
    )(place, g, other)


def _reduce_half(g, other, recv, kind, place, *, name):
    _, pr, pc = recv.shape
    rb, cb = _tile(pr, 512, 16), _tile(pc, 1024)
    nblk = (pr // rb, pc // cb)
    full = (pr * 2, pc) if kind == "col" else (pr, pc * 2)

    def g_index(i, j, place):
        s, c = place[0], place[1]
        return (i + c * nblk[0], j + s * nblk[1]) if kind == "col" else (i + s * nblk[0], j + c * nblk[1])

    def o_index(i, j, place):
        return (i, j + place[0] * nblk[1]) if kind == "col" else (i + place[0] * nblk[0], j)

    def body(place_ref, g_ref, o_ref, r_ref, out_ref):
        acc = g_ref[...].astype(F32) + o_ref[...].astype(F32)
        for k in range(3):
            acc = acc + r_ref[k].astype(F32)
        out_ref[...] = acc

    return pl.pallas_call(
        body,
        grid_spec=pltpu.PrefetchScalarGridSpec(
            num_scalar_prefetch=1, grid=nblk,
            in_specs=[pl.BlockSpec((rb, cb), g_index), pl.BlockSpec((rb, cb), o_index),
                      pl.BlockSpec((3, rb, cb), lambda i, j, place: (0, i, j))],
            out_specs=pl.BlockSpec((rb, cb), _half_index(kind, nblk))),
        out_shape=jax.ShapeDtypeStruct(full, F32), compiler_params=_params(("parallel", "parallel")), name=name,
    )(place, g, other, recv)


def _sum_slots(packed, slots, me, *, name, rb=512):
    r, w = packed.shape
    rb = _tile(r, rb, 8)

    def body(me_ref, p_ref, s_ref, o_ref):
        acc = None
        for j in range(8):
            term = jnp.where(me_ref[0] == j, p_ref[...], s_ref[j])
            acc = term if acc is None else acc + term
        o_ref[...] = acc

    return pl.pallas_call(
        body,
        grid_spec=pltpu.PrefetchScalarGridSpec(
            num_scalar_prefetch=1, grid=(r // rb,),
            in_specs=[pl.BlockSpec((rb, w), lambda i, me: (i, 0)), pl.BlockSpec((8, rb, w), lambda i, me: (0, i, 0))],
            out_specs=pl.BlockSpec((rb, w), lambda i, me: (i, 0))),
        out_shape=jax.ShapeDtypeStruct((r, w), F32), compiler_params=_params(("parallel",)), name=name,
    )(me, packed, slots)


def _half_shape(a, kind):
    return (a.shape[0] // 2, a.shape[1]) if kind == "col" else (a.shape[0], a.shape[1] // 2)


def _rs_swap(tag, grads):
    others = [lax.empty(_half_shape(g, k), g.dtype) for g, k in zip(grads, KINDS)]
    return _start(f"rs_{tag}_swap", list(grads) + others, _swap_plan, (len(KINDS),))


def _rs_exchange(tag, swapped, place, after):
    bufs = _wait(f"rs_{tag}_swap_wait", swapped, _swap_plan, after)
    n = len(KINDS)
    grads, others = bufs[:n], bufs[n:]
    parts = [_chip_partial(g, o, k, place, name=f"rs_{tag}_partial_{k}") for g, o, k in zip(grads, others, KINDS)]
    lands = []
    for p, k in zip(parts, KINDS):
        piece = (p.shape[0], p.shape[1] // N_CHIPS) if k == "col" else (p.shape[0] // N_CHIPS, p.shape[1])
        lands.append(lax.empty((3,) + piece, p.dtype))
    return _start(f"rs_{tag}_exchange", parts + lands, _exchange_plan, (3 * n,)), grads, others


def _rs_share(tag, exchanged, place, after):
    started, grads, others = exchanged
    n = len(KINDS)
    recvs = _wait(f"rs_{tag}_exchange_wait", started, _exchange_plan, after)[n:]
    halves = [_reduce_half(g, o, r, k, place, name=f"rs_{tag}_reduce_{k}") for g, o, r, k in zip(grads, others, recvs, KINDS)]
    return _start(f"rs_{tag}_share", halves, _share_plan, (n,))


def _rs_finish(tag, shared, after):
    return _wait(f"rs_{tag}_share_wait", shared, _share_plan, after)


def _spread(tag, parts, dep):
    rows = [p.reshape(-1, 128) for p in parts]
    n = sum(r.shape[0] for r in rows)
    rows.append(jnp.zeros(((-n) % 512, 128), F32))
    packed = jnp.concatenate(rows, axis=0)
    return _start(f"small_{tag}_spread", [packed, lax.empty((8,) + packed.shape, F32)], _spread_plan, (7,), dep=dep)


def _spread_sum(tag, started, parts, me, after):
    packed, slots = _wait(f"small_{tag}_spread_wait", started, _spread_plan, after)
    total = _sum_slots(packed, slots, me, name=f"small_{tag}_sum")
    out, row = [], 0
    for p in parts:
        n = p.size // 128
        out.append(total[row:row + n].reshape(p.shape))
        row += n
    return out


def _ffn_fwd(x, norm, w_up, conv_w, conv_b, w_down, tag):
    h = _rms_fwd(x, norm, name=f"ffn{tag}_norm")
    up = _mm(h, w_up, name=f"ffn{tag}_up")
    act, up_b, conv_out = _ffn_mid_fwd(up, conv_w, conv_b, name=f"ffn{tag}_mid")
    x_out = _mm(act, w_down, res=x, tk=2816, name=f"ffn{tag}_down")
    return x_out, (h, up_b, conv_out, act)


def _ffn_bwd(dxb, w_up, conv_w, w_down, saved, tag, dep):
    h, up_b, conv_out, act = saved
    d_down = _mm(act, dxb, ta=True, tm=1408, out_dtype=BF16, dep=dep, name=f"ffn{tag}_ddown")
    dup_g, dup_a, dh, dwb = _ffn_bwd_core(dxb, w_down, conv_out, up_b, w_up, conv_w, name=f"ffn{tag}_bwd_core")
    d_up = _mm(h, (dup_g, dup_a), ta=True, out_dtype=BF16, tn=1408, tk=1024, name=f"ffn{tag}_dup")
    return d_up, d_down, dh, jnp.transpose(dwb, (1, 0, 2)).reshape(8, -1)


def kernel(x, a_norm, a_in, a_conv, a_out, b_norm, b_in, b_vnorm, b_ws, b_bs, b_out, f_norm, f_up, f_conv_w, f_conv_b, f_down, final_norm, loss_target, m_a_norm, m_a_in, m_a_conv, m_a_out, m_b_norm, m_b_in, m_b_vnorm, m_b_ws, m_b_bs, m_b_out, m_f_norm, m_f_up, m_f_conv_w, m_f_conv_b, m_f_down, m_final_norm, v_a_norm, v_a_in, v_a_conv, v_a_out, v_b_norm, v_b_in, v_b_vnorm, v_b_ws, v_b_bs, v_b_out, v_f_norm, v_f_up, v_f_conv_w, v_f_conv_b, v_f_down, v_final_norm):
    t, d = x.shape[1], x.shape[2]
    f2 = f_up.shape[2] * N_CHIPS
    x0, tgt = x.reshape(t, d), loss_target.reshape(t, d)
    ax, ay, ac = lax.axis_index("x"), lax.axis_index("y"), lax.axis_index("c")
    s = 2 * ax + ay
    place = jnp.stack([s, ac]).astype(jnp.int32)
    me = (4 * ax + 2 * ay + ac).astype(jnp.int32).reshape(1)

    def stacked(a):
        return lax.dynamic_update_index_in_dim(jnp.zeros((N_CHIPS,) + a.shape, F32), a, s, 0)

    def gather_start(tag, w_in, w_out, layer, small, dep):
        fulls = [_cast_into_full(w_in, layer, "col", place, name=f"cast_{tag}_in", dep=dep),
                 _cast_into_full(w_out, layer, "row", place, name=f"cast_{tag}_out", dep=dep)]
        return _start(f"ag_{tag}_ici", fulls + small, _gather_ici_plan(len(small)), (3 * (2 + len(small)),), dep=dep)

    def gather_forward(tag, started, n_small, after):
        bufs = _wait(f"ag_{tag}_ici_wait", started, _gather_ici_plan(n_small), after)
        return _start(f"ag_{tag}_d2d", bufs[:2], _gather_d2d_plan, (3 * 2,)), bufs[2:]

    def gather_finish(tag, forwarded, after):
        return _wait(f"ag_{tag}_d2d_wait", forwarded, _gather_d2d_plan, after)

    small = [stacked(a_conv[0]), stacked(b_norm), stacked(b_vnorm), stacked(f_conv_w.reshape(2 * 3, -1))]
    ag_a = gather_start("a", a_in, a_out, 0, small, None)
    ag_f0 = gather_start("f0", f_up, f_down, 0, [], ag_a[3])
    ag_b = gather_start("b", b_in, b_out, 0, [], ag_f0[3])
    ag_f1 = gather_start("f1", f_up, f_down, 1, [], ag_b[3])

    def unshard(a):
        return jnp.transpose(a, (1, 0, 2)).reshape(a.shape[1], -1)

    ws, bs = b_ws[0], b_bs[0]

    h0 = _rms_fwd(x0, a_norm, dep=ag_f1[3], name="a_norm")
    fw_a, (g_aconv, g_bnorm, g_bvnorm, g_fconv) = gather_forward("a", ag_a, 4, h0)
    w_ai, w_ao = gather_finish("a", fw_a, fw_a[3])
    a_conv_f, b_norm_f, b_vnorm_f = unshard(g_aconv), unshard(g_bnorm), unshard(g_bvnorm)
    f_conv_f = unshard(g_fconv).reshape(2, 3, f2)
    bcx = _mm(h0, w_ai, name="a_in")
    y = _a_mid_fwd(bcx, a_conv_f, name="a_mid")
    x1 = _mm(y, w_ao, res=x0, name="a_out")
    fw_f0, _ = gather_forward("f0", ag_f0, 0, x1)
    w_up0, w_dn0 = gather_finish("f0", fw_f0, fw_f0[3])
    x2, ffn0 = _ffn_fwd(x1, f_norm[0:1], w_up0, f_conv_f[0], f_conv_b[0:1], w_dn0, 0)
    fw_b, _ = gather_forward("b", ag_b, 0, ffn0[1])
    w_bi, w_bo = gather_finish("b", fw_b, ffn0[3])
    h2 = _rms_fwd(x2, b_norm_f, name="b_norm")
    zp = _mm(h2, w_bi, name="b_in")
    fw_f1, _ = gather_forward("f1", ag_f1, 0, zp)
    ug = _b_mid_fwd(zp, b_vnorm_f, ws, bs, name="b_mid")
    x3 = _mm(ug, w_bo, res=x2, name="b_out")
    w_up1, w_dn1 = gather_finish("f1", fw_f1, x3)
    x4, ffn1 = _ffn_fwd(x3, f_norm[1:2], w_up1, f_conv_f[1], f_conv_b[1:2], w_dn1, 1)
    loss_rows, dx4, dx4b, d_final = _final(x4, tgt, final_norm.reshape(1, d), name="final")

    d_up1, d_dn1, dh3, d_fwb1 = _ffn_bwd(dx4b, w_up1, f_conv_f[1], w_dn1, ffn1, 1, None)
    sw_f1 = _rs_swap("f1", [d_up1, d_dn1])
    dx3, dx3b, d_fnorm1 = _rms_bwd(dh3, x3, f_norm[1:2], dx4, dep=sw_f1[3], name="ffn1_norm_bwd")
    ex_f1 = _rs_exchange("f1", sw_f1, place, dx3)

    d_bo = _mm(ug, dx3b, ta=True, out_dtype=BF16, dep=ex_f1[0][3], name="b_dout")
    dug = _mm(dx3b, w_bo, tb=True, name="b_dug")
    dzp, d_ws, d_bs, d_bvnorm = _b_mid_bwd(zp, dug, b_vnorm_f, ws, bs, name="b_mid_bwd")
    d_bi = _mm(h2, dzp, ta=True, out_dtype=BF16, name="b_din")
    sw_b = _rs_swap("b", [d_bi, d_bo])
    dh2 = _mm(dzp, w_bi, tb=True, dep=sw_b[3], name="b_dh")
    dx2, dx2b, d_bnorm = _rms_bwd(dh2, x2, b_norm_f, dx3, name="b_norm_bwd")
    ex_b = _rs_exchange("b", sw_b, place, dx2)

    d_up0, d_dn0, dh1, d_fwb0 = _ffn_bwd(dx2b, w_up0, f_conv_f[0], w_dn0, ffn0, 0, ex_b[0][3])
    sh_f1 = _rs_share("f1", ex_f1, place, d_up0)
    sw_f0 = _rs_swap("f0", [d_up0, d_dn0])
    dx1, dx1b, d_fnorm0 = _rms_bwd(dh1, x1, f_norm[0:1], dx2, dep=sw_f0[3], name="ffn0_norm_bwd")
    g_up1, g_dn1 = _rs_finish("f1", sh_f1, dx1)
    ex_f0 = _rs_exchange("f0", sw_f0, place, dx1)
    early = [jnp.concatenate([d_bnorm, d_bvnorm, d_fnorm0, d_fnorm1, d_final, loss_rows], axis=0),
             jnp.concatenate([d_fwb0, d_fwb1], axis=0), jnp.concatenate([d_ws.reshape(-1, CHUNK), d_bs], axis=0)]
    sp_early = _spread("early", early, ex_f0[0][3])
    sh_b = _rs_share("b", ex_b, place, sp_early[3])

    d_ao = _mm(y, dx1b, ta=True, out_dtype=BF16, dep=sh_b[3], name="a_dout")
    dyy = _mm(dx1b, w_ao, tb=True, name="a_dy")
    dbcx, d_aconv = _a_mid_bwd(bcx, dyy, a_conv_f, name="a_mid_bwd")
    d_ai = _mm(h0, dbcx, ta=True, out_dtype=BF16, name="a_din")
    sw_a = _rs_swap("a", [d_ai, d_ao])
    g_bi, g_bo = _rs_finish("b", sh_b, sw_a[3])
    ex_a = _rs_exchange("a", sw_a, place, g_bi)
    dh0 = _mm(dbcx, w_ai, tb=True, dep=ex_a[0][3], name="a_dh")
    grad_x, _, d_anorm = _rms_bwd(dh0, x0, a_norm, dx1, name="a_norm_bwd")
    late = [jnp.concatenate([d_anorm, d_aconv], axis=0)]
    sp_late = _spread("late", late, ex_a[0][3])
    sh_f0 = _rs_share("f0", ex_f0, place, sp_late[3])
    sh_a = _rs_share("a", ex_a, place, sh_f0[3])
    g_up0, g_dn0 = _rs_finish("f0", sh_f0, sh_a[3])
    g_ai, g_ao = _rs_finish("a", sh_a, g_up0)
    r_a, r_b, r_c = _spread_sum("early", sp_early, early, me, g_ai)
    (r_l,) = _spread_sum("late", sp_late, late, me, r_a)

    loss = jnp.sum(r_a[40])
    cs, fs = d // N_CHIPS, f2 // N_CHIPS

    def mine(a, width):
        return lax.dynamic_slice_in_dim(a, s * width, width, axis=1)

    grads = {
        "a_norm": r_l[0:1], "a_conv": mine(r_l[8:11], cs), "b_norm": mine(r_a[0:1], cs), "b_vnorm": mine(r_a[8:9], cs),
        "f_norm": jnp.concatenate([r_a[16:17], r_a[24:25]], axis=0), "final_norm": r_a[32:33],
        "b_ws": r_c[:GROUPS * CHUNK], "b_bs": r_c[GROUPS * CHUNK:],
        "f_conv_w": jnp.concatenate([mine(r_b[0:3], fs), mine(r_b[8:11], fs)], axis=0),
        "f_conv_b": jnp.concatenate([r_b[3:4], r_b[11:12]], axis=0),
        "a_in": g_ai, "a_out": g_ao, "b_in": g_bi, "b_out": g_bo,
    }
    names = ["a_norm", "a_in", "a_conv", "a_out", "b_norm", "b_in", "b_vnorm", "b_ws", "b_bs", "b_out", "f_norm", "f_up",
             "f_conv_w", "f_conv_b", "f_down", "final_norm"]
    weights = dict(zip(names, [a_norm, a_in, a_conv, a_out, b_norm, b_in, b_vnorm, b_ws, b_bs, b_out, f_norm, f_up, f_conv_w,
                               f_conv_b, f_down, final_norm]))
    ms = dict(zip(names, [m_a_norm, m_a_in, m_a_conv, m_a_out, m_b_norm, m_b_in, m_b_vnorm, m_b_ws, m_b_bs, m_b_out, m_f_norm,
                          m_f_up, m_f_conv_w, m_f_conv_b, m_f_down, m_final_norm]))
    vs = dict(zip(names, [v_a_norm, v_a_in, v_a_conv, v_a_out, v_b_norm, v_b_in, v_b_vnorm, v_b_ws, v_b_bs, v_b_out, v_f_norm,
                          v_f_up, v_f_conv_w, v_f_conv_b, v_f_down, v_final_norm]))
    result = {}
    for n in names:
        w = weights[n]
        if n in ("f_up", "f_down"):
            g1, g0 = (g_up1, g_up0) if n == "f_up" else (g_dn1, g_dn0)
            first = _adamw_layer(w, g1, ms[n], vs[n], 1, None, name=f"adamw_{n}1")
            result[n] = _adamw_layer(w, g0, ms[n], vs[n], 0, tuple(first), name=f"adamw_{n}0")
            continue
        g2 = grads[n]
        as3d = (lambda a: a.reshape((1,) + g2.shape))
        result[n] = [o.reshape(w.shape) for o in _adamw_layer(as3d(w), g2, as3d(ms[n]), as3d(vs[n]), 0, None, name=f"adamw_{n}")]

    return (loss, grad_x.reshape(x.shape), *[result[n][0] for n in names], *[result[n][1] for n in names],
            *[result[n][2] for n in names], *[result[n][3] for n in names])
```

```python
import functools

import jax
import jax.numpy as jnp
from jax import lax
from jax.experimental import pallas as pl
from jax.experimental.pallas import tpu as pltpu

F32 = jnp.float32
BF16 = jnp.bfloat16
MESH = pl.DeviceIdType.MESH
ANY = pl.BlockSpec(memory_space=pl.ANY)

RMS_EPS = 1e-5
CHUNK = 128
GROUPS = 8
ADAM_LR, ADAM_B1, ADAM_B2, ADAM_EPS, ADAM_WD, ADAM_STEP = 0.001, 0.9, 0.999, 1e-08, 0.01, 10

N_CHIPS = 4
HALO = 8
VMEM_LIMIT = 56 * 1024 * 1024
GELU_C = 0.7978845608028654
GELU_A = 0.044715


def _params(sem=None):
    return pltpu.CompilerParams(dimension_semantics=sem, vmem_limit_bytes=VMEM_LIMIT)


def _tile(dim, pref, quantum=128):
    if dim <= pref:
        return dim
    t = (pref // quantum) * quantum
    while t >= quantum:
        if dim % t == 0:
            return t
        t -= quantum
    return dim


def _mm(a, b, *, name, ta=False, tb=False, res=None, dep=None, out_dtype=F32, tm=1024, tn=1024, tk=2048):
    (K, M) = a.shape if ta else a.shape[::-1]
    parts = b if isinstance(b, tuple) else (b,)
    n_part = parts[0].shape[0] if tb else parts[0].shape[1]
    N = n_part * len(parts)
    assert (parts[0].shape[1] if tb else parts[0].shape[0]) == K and not (tb and len(parts) > 1)
    tm, tn, tk = _tile(M, tm), _tile(n_part, tn), _tile(K, tk)
    nk, nj = K // tk, n_part // tn
    a_spec = pl.BlockSpec((tk, tm), lambda i, j, k: (k, i)) if ta else pl.BlockSpec((tm, tk), lambda i, j, k: (i, k))
    if len(parts) == 1:
        b_specs = [pl.BlockSpec((tn, tk), lambda i, j, k: (j, k)) if tb else pl.BlockSpec((tk, tn), lambda i, j, k: (k, j))]
    else:
        def part_spec(p):
            def index(i, j, k):
                mine = jnp.logical_and(j >= p * nj, j < (p + 1) * nj)
                return (jnp.where(mine, k, 0), jnp.where(mine, j - p * nj, 0))
            return pl.BlockSpec((tk, tn), index)
        b_specs = [part_spec(p) for p in range(len(parts))]
    o_spec = pl.BlockSpec((tm, tn), lambda i, j, k: (i, j))
    dims = (((0 if ta else 1,), (1 if tb else 0,)), ((), ()))
    direct = out_dtype == F32
    n_b = len(parts)

    def body(*refs):
        a_ref, b_refs = refs[0], refs[1:1 + n_b]
        r_ref = refs[1 + n_b] if res is not None else None
        o_ref = refs[1 + n_b + (res is not None) + (dep is not None)]
        acc_ref = o_ref if direct else refs[-1]

        def step(b_ref):
            part = lax.dot_general(a_ref[...], b_ref[...], dims, preferred_element_type=F32)
            if nk == 1:
                if r_ref is not None:
                    part = part + r_ref[...]
                o_ref[...] = part.astype(o_ref.dtype)
                return
            k = pl.program_id(2)

            @pl.when(k == 0)
            def _():
                acc_ref[...] = part

            @pl.when(jnp.logical_and(k > 0, k < nk - 1))
            def _():
                acc_ref[...] += part

            @pl.when(k == nk - 1)
            def _():
                tot = acc_ref[...] + part
                if r_ref is not None:
                    tot = tot + r_ref[...]
                o_ref[...] = tot.astype(o_ref.dtype)

        if n_b == 1:
            step(b_refs[0])
        else:
            for p in range(n_b):
                pl.when(pl.program_id(1) // nj == p)(functools.partial(step, b_refs[p]))

    in_specs = [a_spec] + b_specs + ([o_spec] if res is not None else []) + ([ANY] if dep is not None else [])
    args = (a,) + parts + ((res,) if res is not None else ()) + ((dep,) if dep is not None else ())
    scratch = [] if (direct or nk == 1) else [pltpu.VMEM((tm, tn), F32)]
    return pl.pallas_call(
        body, grid=(M // tm, N // tn, nk), in_specs=in_specs, out_specs=o_spec,
        out_shape=jax.ShapeDtypeStruct((M, N), out_dtype), scratch_shapes=scratch,
        compiler_params=_params(("parallel", "parallel", "arbitrary")), name=name,
    )(*args)


def _row_spec(rb, w):
    return pl.BlockSpec((rb, w), lambda i: (i, 0))


def _prev_spec(rb, w):
    return pl.BlockSpec((HALO, w), lambda i: (jnp.maximum(i * (rb // HALO) - 1, 0), 0))


def _next_spec(rb, w, t):
    return pl.BlockSpec((HALO, w), lambda i: (jnp.minimum((i + 1) * (rb // HALO), t // HALO - 1), 0))


def _full_spec(shape):
    return pl.BlockSpec(shape, lambda i: tuple(0 for _ in shape))


def _shift(e, s):
    return pltpu.roll(e, s % e.shape[0], 0)


def _gelu(x):
    return 0.5 * x * (1.0 + jnp.tanh(GELU_C * (x + GELU_A * x * x * x)))


def _gelu_grad(x):
    th = jnp.tanh(GELU_C * (x + GELU_A * x * x * x))
    return 0.5 * (1.0 + th) + 0.5 * x * (1.0 - th * th) * (GELU_C * (1.0 + 3.0 * GELU_A * x * x))


def _sigmoid(x):
    return 1.0 / (1.0 + jnp.exp(-x))


def _rms_fwd(x, g, *, name, dep=None, rb=256):
    t, d = x.shape
    rb = _tile(t, rb, 8)

    def body(x_ref, g_ref, *rest):
        h_ref = rest[-1]
        xv = x_ref[...]
        r = lax.rsqrt(jnp.mean(xv * xv, axis=-1, keepdims=True) + RMS_EPS)
        h_ref[...] = ((xv * r) * g_ref[...]).astype(BF16)

    return pl.pallas_call(
        body, grid=(t // rb,), in_specs=[_row_spec(rb, d), _full_spec((1, d))] + ([ANY] if dep is not None else []),
        out_specs=_row_spec(rb, d), out_shape=jax.ShapeDtypeStruct((t, d), BF16), compiler_params=_params(("parallel",)), name=name,
    )(x, g, *(() if dep is None else (dep,)))


def _rms_bwd(dh, x, g, dres, *, name, dep=None, rb=256):
    t, d = x.shape
    rb = _tile(t, rb, 8)

    def body(dh_ref, x_ref, g_ref, dres_ref, *rest):
        dx_ref, dxb_ref, dg_ref = rest[-3:]
        xv = x_ref[...]
        r = lax.rsqrt(jnp.mean(xv * xv, axis=-1, keepdims=True) + RMS_EPS)
        xhat = xv * r
        dh_v = dh_ref[...]
        dxhat = dh_v * g_ref[...]
        m = jnp.mean(dxhat * xhat, axis=-1, keepdims=True)
        dx = dres_ref[...] + r * (dxhat - xhat * m)
        dx_ref[...] = dx
        dxb_ref[...] = dx.astype(BF16)

        @pl.when(pl.program_id(0) == 0)
        def _():
            dg_ref[...] = jnp.zeros_like(dg_ref)

        dg_ref[0:1, :] += jnp.sum(dh_v * xhat, axis=0, keepdims=True)

    return pl.pallas_call(
        body, grid=(t // rb,),
        in_specs=[_row_spec(rb, d), _row_spec(rb, d), _full_spec((1, d)), _row_spec(rb, d)] + ([ANY] if dep is not None else []),
        out_specs=[_row_spec(rb, d), _row_spec(rb, d), _full_spec((8, d))],
        out_shape=[jax.ShapeDtypeStruct((t, d), F32), jax.ShapeDtypeStruct((t, d), BF16), jax.ShapeDtypeStruct((8, d), F32)],
        compiler_params=_params(("arbitrary",)), name=name,
    )(dh, x, g, dres, *(() if dep is None else (dep,)))


def _final(x, tgt, g, *, name, rb=256):
    t, d = x.shape
    rb = _tile(t, rb, 8)
    inv_d = 1.0 / d

    def body(x_ref, t_ref, g_ref, l_ref, dx_ref, dxb_ref, dg_ref):
        xv = x_ref[...]
        gv = g_ref[...]
        r = lax.rsqrt(jnp.mean(xv * xv, axis=-1, keepdims=True) + RMS_EPS)
        xhat = xv * r
        e = xhat * gv - t_ref[...]
        dy = e * inv_d
        dxhat = dy * gv
        m = jnp.mean(dxhat * xhat, axis=-1, keepdims=True)
        dx = r * (dxhat - xhat * m)
        dx_ref[...] = dx
        dxb_ref[...] = dx.astype(BF16)

        @pl.when(pl.program_id(0) == 0)
        def _():
            l_ref[...] = jnp.zeros_like(l_ref)
            dg_ref[...] = jnp.zeros_like(dg_ref)

        l_ref[0:1, :] += jnp.sum(e * e, axis=0, keepdims=True) * (0.5 * inv_d)
        dg_ref[0:1, :] += jnp.sum(dy * xhat, axis=0, keepdims=True)

    return pl.pallas_call(
        body, grid=(t // rb,),
        in_specs=[_row_spec(rb, d), _row_spec(rb, d), _full_spec((1, d))],
        out_specs=[_full_spec((8, d)), _row_spec(rb, d), _row_spec(rb, d), _full_spec((8, d))],
        out_shape=[jax.ShapeDtypeStruct((8, d), F32), jax.ShapeDtypeStruct((t, d), F32),
                   jax.ShapeDtypeStruct((t, d), BF16), jax.ShapeDtypeStruct((8, d), F32)],
        compiler_params=_params(("arbitrary",)), name=name,
    )(x, tgt, g)


def _a_mid_fwd(bcx, wconv, *, name, rb=256, cw=512):
    t, d3 = bcx.shape
    d = d3 // 3
    rb, cw = _tile(t, rb, 8), _tile(d, cw)

    def body(cur_ref, prev_ref, w_ref, y_ref):
        first = pl.program_id(0) == 0
        for c0 in range(0, d, cw):
            cs = slice(c0, c0 + cw)
            gc, xs = slice(d + c0, d + c0 + cw), slice(2 * d + c0, 2 * d + c0 + cw)
            p_prev = jnp.where(first, 0.0, prev_ref[:, gc] * prev_ref[:, xs])
            e = jnp.concatenate([p_prev, cur_ref[:, gc] * cur_ref[:, xs]], axis=0)
            w = w_ref[:, cs]
            q = w[0:1] * _shift(e, 2) + w[1:2] * _shift(e, 1) + w[2:3] * e
            y_ref[:, cs] = (cur_ref[:, cs] * q[HALO:]).astype(BF16)

    return pl.pallas_call(
        body, grid=(t // rb,),
        in_specs=[_row_spec(rb, d3), _prev_spec(rb, d3), _full_spec((3, d))], out_specs=_row_spec(rb, d),
        out_shape=jax.ShapeDtypeStruct((t, d), BF16), compiler_params=_params(("parallel",)), name=name,
    )(bcx, bcx, wconv)


def _a_mid_bwd(bcx, dy, wconv, *, name, rb=128, cw=512):
    t, d3 = bcx.shape
    d = d3 // 3
    rb, cw = _tile(t, rb, 8), _tile(d, cw)

    def body(cur_ref, prev_ref, next_ref, dy_ref, dyn_ref, w_ref, o_ref, dw_ref):
        i = pl.program_id(0)
        first, last = i == 0, i == pl.num_programs(0) - 1

        @pl.when(first)
        def _():
            dw_ref[...] = jnp.zeros_like(dw_ref)

        for c0 in range(0, d, cw):
            cs = slice(c0, c0 + cw)
            gc, xs = slice(d + c0, d + c0 + cw), slice(2 * d + c0, 2 * d + c0 + cw)
            zeros = jnp.zeros((HALO, cw), F32)
            gb_c, gc_c, xs_c = cur_ref[:, cs], cur_ref[:, gc], cur_ref[:, xs]
            p_prev = jnp.where(first, 0.0, prev_ref[:, gc] * prev_ref[:, xs])
            e = jnp.concatenate([p_prev, gc_c * xs_c, zeros], axis=0)
            dq_next = jnp.where(last, 0.0, dyn_ref[:, cs] * next_ref[:, cs])
            dy_c = dy_ref[:, cs]
            dq = jnp.concatenate([zeros, dy_c * gb_c, dq_next], axis=0)
            w = w_ref[:, cs]
            e1, e2 = _shift(e, 1), _shift(e, 2)
            q = w[0:1] * e2 + w[1:2] * e1 + w[2:3] * e
            dp = (w[2:3] * dq + w[1:2] * _shift(dq, -1) + w[0:1] * _shift(dq, -2))[HALO:HALO + rb]
            o_ref[:, cs] = (dy_c * q[HALO:HALO + rb]).astype(BF16)
            o_ref[:, gc] = (dp * xs_c).astype(BF16)
            o_ref[:, xs] = (dp * gc_c).astype(BF16)
            dq_c = dq[HALO:HALO + rb]
            dw_ref[0:1, cs] += jnp.sum(dq_c * e2[HALO:HALO + rb], axis=0, keepdims=True)
            dw_ref[1:2, cs] += jnp.sum(dq_c * e1[HALO:HALO + rb], axis=0, keepdims=True)
            dw_ref[2:3, cs] += jnp.sum(dq_c * e[HALO:HALO + rb], axis=0, keepdims=True)

    return pl.pallas_call(
        body, grid=(t // rb,),
        in_specs=[_row_spec(rb, d3), _prev_spec(rb, d3), _next_spec(rb, d3, t), _row_spec(rb, d), _next_spec(rb, d, t),
                  _full_spec((3, d))],
        out_specs=[_row_spec(rb, d3), _full_spec((8, d))],
        out_shape=[jax.ShapeDtypeStruct((t, d3), BF16), jax.ShapeDtypeStruct((8, d), F32)],
        compiler_params=_params(("arbitrary",)), name=name,
    )(bcx, bcx, bcx, dy, dy, wconv)


def _ffn_mid_fwd(up, wconv, bconv, *, name, rb=256, cw=512):
    t, f2 = up.shape
    f = f2 // 2
    rb, cw = _tile(t, rb, 8), _tile(f, cw)

    def body(cur_ref, prev_ref, w_ref, b_ref, act_ref):
        first = pl.program_id(0) == 0

        def conv(cols):
            e = jnp.concatenate([jnp.where(first, 0.0, prev_ref[:, cols]), cur_ref[:, cols]], axis=0)
            w = w_ref[:, cols]
            return (w[0:1] * _shift(e, 2) + w[1:2] * _shift(e, 1) + w[2:3] * e + b_ref[:, cols])[HALO:]

        for c0 in range(0, f, cw):
            g = conv(slice(c0, c0 + cw))
            a = conv(slice(f + c0, f + c0 + cw))
            act_ref[:, c0:c0 + cw] = (g * _sigmoid(g) * a).astype(BF16)

    return pl.pallas_call(
        body, grid=(t // rb,),
        in_specs=[_row_spec(rb, f2), _prev_spec(rb, f2), _full_spec((3, f2)), _full_spec((1, f2))],
        out_specs=_row_spec(rb, f), out_shape=jax.ShapeDtypeStruct((t, f), BF16),
        compiler_params=_params(("parallel",)), name=name,
    )(up, up, wconv, bconv)


def _ffn_mid_bwd(up, dact, wconv, bconv, *, name, rb=128, cw=512):
    t, f2 = up.shape
    f = f2 // 2
    rb, cw = _tile(t, rb, 8), _tile(f, cw)

    def body(cur_ref, prev_ref, next_ref, da_ref, dan_ref, w_ref, b_ref, o_ref, dwb_ref):
        i = pl.program_id(0)
        first, last = i == 0, i == pl.num_programs(0) - 1

        @pl.when(first)
        def _():
            dwb_ref[...] = jnp.zeros_like(dwb_ref)

        def ext(cols):
            e = jnp.concatenate([jnp.where(first, 0.0, prev_ref[:, cols]), cur_ref[:, cols], next_ref[:, cols]], axis=0)
            w = w_ref[:, cols]
            e1, e2 = _shift(e, 1), _shift(e, 2)
            return e, e1, e2, w, w[0:1] * e2 + w[1:2] * e1 + w[2:3] * e + b_ref[:, cols]

        def back(dc, e, e1, e2, w, cols):
            o_ref[:, cols] = (w[2:3] * dc + w[1:2] * _shift(dc, -1) + w[0:1] * _shift(dc, -2))[HALO:HALO + rb].astype(BF16)
            dc_c = dc[HALO:HALO + rb]
            dwb_ref[0:1, cols] += jnp.sum(dc_c * e2[HALO:HALO + rb], axis=0, keepdims=True)
            dwb_ref[1:2, cols] += jnp.sum(dc_c * e1[HALO:HALO + rb], axis=0, keepdims=True)
            dwb_ref[2:3, cols] += jnp.sum(dc_c * e[HALO:HALO + rb], axis=0, keepdims=True)
            dwb_ref[3:4, cols] += jnp.sum(dc_c, axis=0, keepdims=True)

        for c0 in range(0, f, cw):
            gcols, acols = slice(c0, c0 + cw), slice(f + c0, f + c0 + cw)
            eg, eg1, eg2, wg, g = ext(gcols)
            ea, ea1, ea2, wa, a = ext(acols)
            da_next = jnp.where(last, 0.0, dan_ref[:, gcols])
            da = jnp.concatenate([jnp.zeros((HALO, cw), F32), da_ref[:, gcols], da_next], axis=0)
            sg = _sigmoid(g)
            dg = da * a * (sg * (1.0 + g * (1.0 - sg)))
            dav = da * (g * sg)
            back(dg, eg, eg1, eg2, wg, gcols)
            back(dav, ea, ea1, ea2, wa, acols)

    return pl.pallas_call(
        body, grid=(t // rb,),
        in_specs=[_row_spec(rb, f2), _prev_spec(rb, f2), _next_spec(rb, f2, t), _row_spec(rb, f), _next_spec(rb, f, t),
                  _full_spec((3, f2)), _full_spec((1, f2))],
        out_specs=[_row_spec(rb, f2), _full_spec((8, f2))],
        out_shape=[jax.ShapeDtypeStruct((t, f2), BF16), jax.ShapeDtypeStruct((8, f2), F32)],
        compiler_params=_params(("arbitrary",)), name=name,
    )(up, up, up, dact, dact, wconv, bconv)


def _ffn_bwd_core(dxb, w_down, conv, up, w_up, wconv, *, name, tm=512, cw=512):
    t, d = dxb.shape
    f2 = conv.shape[1]
    f = f2 // 2
    tm, cw = _tile(t, tm, 8), _tile(f, cw)
    n_i, n_c = t // tm, f // cw
    nt = (((1,), (1,)), ((), ()))

    def body(dx_ref, wd_ref, cg_ref, ca_ref, ug_ref, ua_ref, wug_ref, wua_ref, wg_ref, wa_ref,
             dupg_ref, dupa_ref, dh_ref, dwb_ref, carry_g, carry_a):
        i, c = pl.program_id(0), pl.program_id(1)

        @pl.when(jnp.logical_and(i == 0, c == 0))
        def _():
            dwb_ref[...] = jnp.zeros_like(dwb_ref)

        dact = lax.dot_general(dx_ref[...], wd_ref[...], nt, preferred_element_type=F32)
        g, a = cg_ref[...].astype(F32), ca_ref[...].astype(F32)
        sg = _sigmoid(g)

        def back(dc, carry, u_ref, w_ref, out_ref, half):
            w = w_ref[...]
            e = jnp.concatenate([dc, jnp.where(i == 0, 0.0, carry[c])], axis=0)
            carry[c] = dc[0:HALO]
            e1, e2 = _shift(e, -1)[:tm], _shift(e, -2)[:tm]
            dup = (w[2:3] * dc + w[1:2] * e1 + w[0:1] * e2).astype(BF16)
            out_ref[...] = dup
            u = u_ref[...].astype(F32)
            k = c + half * n_c
            dwb_ref[k, 0:1, :] += jnp.sum(e2 * u, axis=0, keepdims=True)
            dwb_ref[k, 1:2, :] += jnp.sum(e1 * u, axis=0, keepdims=True)
            dwb_ref[k, 2:3, :] += jnp.sum(dc * u, axis=0, keepdims=True)
            dwb_ref[k, 3:4, :] += jnp.sum(dc, axis=0, keepdims=True)
            return dup

        dup_g = back(dact * a * (sg * (1.0 + g * (1.0 - sg))), carry_g, ug_ref, wg_ref, dupg_ref, 0)
        dup_a = back(dact * (g * sg), carry_a, ua_ref, wa_ref, dupa_ref, 1)
        part = (lax.dot_general(dup_g, wug_ref[...], nt, preferred_element_type=F32)
                + lax.dot_general(dup_a, wua_ref[...], nt, preferred_element_type=F32))

        @pl.when(c == 0)
        def _():
            dh_ref[...] = part

        @pl.when(c > 0)
        def _():
            dh_ref[...] += part

    def rows(i):
        return n_i - 1 - i

    act_g = pl.BlockSpec((tm, cw), lambda i, c: (rows(i), c))
    act_a = pl.BlockSpec((tm, cw), lambda i, c: (rows(i), c + n_c))
    return pl.pallas_call(
        body, grid=(n_i, n_c),
        in_specs=[pl.BlockSpec((tm, d), lambda i, c: (rows(i), 0)), pl.BlockSpec((cw, d), lambda i, c: (c, 0)),
                  act_g, act_a, act_g, act_a,
                  pl.BlockSpec((d, cw), lambda i, c: (0, c)), pl.BlockSpec((d, cw), lambda i, c: (0, c + n_c)),
                  pl.BlockSpec((3, cw), lambda i, c: (0, c)), pl.BlockSpec((3, cw), lambda i, c: (0, c + n_c))],
        out_specs=[act_g, act_g, pl.BlockSpec((tm, d), lambda i, c: (rows(i), 0)),
                   pl.BlockSpec((2 * n_c, 8, cw), lambda i, c: (0, 0, 0))],
        out_shape=[jax.ShapeDtypeStruct((t, f), BF16), jax.ShapeDtypeStruct((t, f), BF16), jax.ShapeDtypeStruct((t, d), F32),
                   jax.ShapeDtypeStruct((2 * n_c, 8, cw), F32)],
        scratch_shapes=[pltpu.VMEM((n_c, HALO, cw), F32), pltpu.VMEM((n_c, HALO, cw), F32)],
        compiler_params=_params(("arbitrary", "arbitrary")), name=name,
    )(dxb, w_down, conv, conv, up, up, w_up, w_up, wconv, wconv)


def _causal_mask():
    return lax.broadcasted_iota(jnp.int32, (CHUNK, CHUNK), 0) >= lax.broadcasted_iota(jnp.int32, (CHUNK, CHUNK), 1)


def _b_mid_fwd(zp, vnorm, ws, bs, *, name, rb=256):
    t, d2 = zp.shape
    d = d2 // 2
    c = d // GROUPS
    rb = _tile(t, rb, CHUNK)

    def body(zp_ref, gv_ref, ws_ref, bs_ref, ug_ref, vn_ref, gate_ref):
        v = _gelu(zp_ref[:, d:])
        rv = lax.rsqrt(jnp.mean(v * v, axis=-1, keepdims=True) + RMS_EPS)
        vn_ref[...] = ((v * rv) * gv_ref[...]).astype(BF16)
        mask = _causal_mask()
        for h in range(GROUPS):
            hc = slice(h * c, (h + 1) * c)
            wm = jnp.where(mask, ws_ref[h], 0.0).astype(BF16)
            bcol = jnp.broadcast_to(bs_ref[h:h + 1, :], (CHUNK, CHUNK)).T[:, 0:1]
            for n in range(rb // CHUNK):
                rows = slice(n * CHUNK, (n + 1) * CHUNK)
                gate_ref[rows, hc] = jnp.dot(wm, vn_ref[rows, hc], preferred_element_type=F32) + bcol
        ug_ref[...] = (_gelu(zp_ref[:, :d]) * gate_ref[...]).astype(BF16)

    return pl.pallas_call(
        body, grid=(t // rb,),
        in_specs=[_row_spec(rb, d2), _full_spec((1, d)), _full_spec((GROUPS, CHUNK, CHUNK)), _full_spec((GROUPS, CHUNK))],
        out_specs=_row_spec(rb, d), out_shape=jax.ShapeDtypeStruct((t, d), BF16),
        scratch_shapes=[pltpu.VMEM((rb, d), BF16), pltpu.VMEM((rb, d), F32)],
        compiler_params=_params(("parallel",)), name=name,
    )(zp, vnorm, ws, bs)


def _b_mid_bwd(zp, dug, vnorm, ws, bs, *, name, rb=256):
    t, d2 = zp.shape
    d = d2 // 2
    c = d // GROUPS
    rb = _tile(t, rb, CHUNK)

    def body(zp_ref, dug_ref, gv_ref, ws_ref, bs_ref, dzp_ref, dws_ref, dbs_ref, dgv_ref,
             vn_ref, gate_ref, dm_ref, dvn_ref, dbacc_ref):
        i = pl.program_id(0)

        @pl.when(i == 0)
        def _():
            dws_ref[...] = jnp.zeros_like(dws_ref)
            dgv_ref[...] = jnp.zeros_like(dgv_ref)
            dbacc_ref[...] = jnp.zeros_like(dbacc_ref)

        zu, zv = zp_ref[:, :d], zp_ref[:, d:]
        u, v = _gelu(zu), _gelu(zv)
        rv = lax.rsqrt(jnp.mean(v * v, axis=-1, keepdims=True) + RMS_EPS)
        vhat = v * rv
        gv = gv_ref[...]
        vn_ref[...] = (vhat * gv).astype(BF16)
        dug_v = dug_ref[...]
        dm = dug_v * u
        dm_ref[...] = dm.astype(BF16)
        mask = _causal_mask()
        for h in range(GROUPS):
            hc = slice(h * c, (h + 1) * c)
            wm = jnp.where(mask, ws_ref[h], 0.0)
            wm_b, wmt_b = wm.astype(BF16), wm.T.astype(BF16)
            bcol = jnp.broadcast_to(bs_ref[h:h + 1, :], (CHUNK, CHUNK)).T[:, 0:1]
            dws_h = jnp.zeros((CHUNK, CHUNK), F32)
            dbs_h = jnp.zeros((CHUNK, c), F32)
            for n in range(rb // CHUNK):
                rows = slice(n * CHUNK, (n + 1) * CHUNK)
                vn_c, dm_c = vn_ref[rows, hc], dm_ref[rows, hc]
                gate_ref[rows, hc] = jnp.dot(wm_b, vn_c, preferred_element_type=F32) + bcol
                dws_h += lax.dot_general(dm_c, vn_c, (((1,), (1,)), ((), ())), preferred_element_type=F32)
                dvn_ref[rows, hc] = jnp.dot(wmt_b, dm_c, preferred_element_type=F32)
                dbs_h += dm[rows, hc]
            dws_ref[h] += dws_h
            dbacc_ref[h] += dbs_h
        du = dug_v * gate_ref[...]
        dvn = dvn_ref[...]
        dvhat = dvn * gv
        m = jnp.mean(dvhat * vhat, axis=-1, keepdims=True)
        dv = rv * (dvhat - vhat * m)
        dgv_ref[0:1, :] += jnp.sum(dvn * vhat, axis=0, keepdims=True)
        dzp_ref[:, :d] = (du * _gelu_grad(zu)).astype(BF16)
        dzp_ref[:, d:] = (dv * _gelu_grad(zv)).astype(BF16)

        @pl.when(i == pl.num_programs(0) - 1)
        def _():
            ones = jnp.ones((8, c), F32)
            for h in range(GROUPS):
                dws_ref[h] = jnp.where(mask, dws_ref[h], 0.0)
                row = lax.dot_general(ones, dbacc_ref[h], (((1,), (1,)), ((), ())),
                                      precision=lax.Precision.HIGHEST, preferred_element_type=F32)
                dbs_ref[h:h + 1, :] = row[0:1]

    return pl.pallas_call(
        body, grid=(t // rb,),
        in_specs=[_row_spec(rb, d2), _row_spec(rb, d), _full_spec((1, d)), _full_spec((GROUPS, CHUNK, CHUNK)),
                  _full_spec((GROUPS, CHUNK))],
        out_specs=[_row_spec(rb, d2), _full_spec((GROUPS, CHUNK, CHUNK)), _full_spec((GROUPS, CHUNK)), _full_spec((8, d))],
        out_shape=[jax.ShapeDtypeStruct((t, d2), BF16), jax.ShapeDtypeStruct((GROUPS, CHUNK, CHUNK), F32),
                   jax.ShapeDtypeStruct((GROUPS, CHUNK), F32), jax.ShapeDtypeStruct((8, d), F32)],
        scratch_shapes=[pltpu.VMEM((rb, d), BF16), pltpu.VMEM((rb, d), F32), pltpu.VMEM((rb, d), BF16),
                        pltpu.VMEM((rb, d), F32), pltpu.VMEM((GROUPS, CHUNK, c), F32)],
        compiler_params=_params(("arbitrary",)), name=name,
    )(zp, dug, vnorm, ws, bs)


def _cast_into_full(w, layer, kind, place, *, name, dep=None, rb=256):
    _, r, c = w.shape
    rb = _tile(r, rb, 16)
    nrb = r // rb
    full = (r, c * N_CHIPS) if kind == "col" else (r * N_CHIPS, c)

    def body(place_ref, w_ref, *rest):
        rest[-1][...] = w_ref[...].astype(BF16)

    def o_index(i, place):
        return (i, place[0]) if kind == "col" else (i + place[0] * nrb, 0)

    in_specs = [pl.BlockSpec((None, rb, c), lambda i, place: (layer, i, 0))] + ([ANY] if dep is not None else [])
    return pl.pallas_call(
        body,
        grid_spec=pltpu.PrefetchScalarGridSpec(num_scalar_prefetch=1, grid=(nrb,), in_specs=in_specs,
                                               out_specs=pl.BlockSpec((rb, c), o_index)),
        out_shape=jax.ShapeDtypeStruct(full, BF16), compiler_params=_params(("parallel",)), name=name,
    )(place, w, *(() if dep is None else (dep,)))


def _adamw_layer(w, g, m, v, layer, prev, *, name, rb=128):
    _, r, c = w.shape
    rb = _tile(r, rb, 8)
    c1 = 1.0 - ADAM_B1 ** ADAM_STEP
    c2 = 1.0 - ADAM_B2 ** ADAM_STEP

    def body(w_ref, g_ref, m_ref, v_ref, *rest):
        go_ref, d_ref, nm_ref, nv_ref = rest[-4:]
        gv = g_ref[...]
        nm = ADAM_B1 * m_ref[...] + (1.0 - ADAM_B1) * gv
        nv = ADAM_B2 * v_ref[...] + (1.0 - ADAM_B2) * (gv * gv)
        go_ref[...] = gv
        nm_ref[...] = nm
        nv_ref[...] = nv
        d_ref[...] = -ADAM_LR * ((nm / c1) / (jnp.sqrt(nv / c2) + ADAM_EPS) + ADAM_WD * w_ref[...])

    lay = pl.BlockSpec((None, rb, c), lambda i: (layer, i, 0))
    return pl.pallas_call(
        body, grid=(r // rb,), in_specs=[lay, _row_spec(rb, c), lay, lay] + ([ANY] * 4 if prev else []), out_specs=[lay] * 4,
        out_shape=[jax.ShapeDtypeStruct(w.shape, F32)] * 4, input_output_aliases={4 + k: k for k in range(4)} if prev else {},
        compiler_params=_params(("parallel",)), name=name,
    )(w, g, m, v, *(prev or ()))


HBM = pl.BlockSpec(memory_space=pltpu.HBM)
SEM = pl.BlockSpec(memory_space=pltpu.SEMAPHORE)
SIDE_EFFECT = pltpu.SideEffectType.DATAFLOW_SIDE_EFFECTING


def _place():
    x, y, c = lax.axis_index("x"), lax.axis_index("y"), lax.axis_index("c")
    chips = [(1 - x, y), (x, 1 - y), (1 - x, 1 - y)]
    return x, y, c, 2 * x + y, chips


def _half(ref, kind, c):
    r, w = ref.shape
    if kind == "col":
        return ref.at[pl.ds(pl.multiple_of(c * (r // 2), 8), r // 2), :]
    return ref.at[:, pl.ds(pl.multiple_of(c * (w // 2), 128), w // 2)]


def _shard(ref, kind, s):
    r, w = ref.shape
    if kind == "col":
        return ref.at[:, pl.ds(pl.multiple_of(s * (w // N_CHIPS), 128), w // N_CHIPS)]
    return ref.at[pl.ds(pl.multiple_of(s * (r // N_CHIPS), 8), r // N_CHIPS), :]


def _remote(src, dst, send_sem, recv_sem, dev):
    return pltpu.make_async_remote_copy(src_ref=src, dst_ref=dst, send_sem=send_sem, recv_sem=recv_sem,
                                        device_id=dev, device_id_type=MESH)


def _start(name, bufs, plan, sem_shape, dep=None):
    n = len(bufs)
    n_in = n + (dep is not None)

    def body(*refs):
        sends, _ = plan(refs[:n], refs[n_in], refs[n_in + 1])
        for cp in sends:
            cp.start()
        refs[n_in + 2 + n][...] = jnp.zeros((8, 128), F32)

    dma = pltpu.SemaphoreType.DMA
    outs = pl.pallas_call(
        body, name=name,
        out_shape=(dma(sem_shape), dma(sem_shape), *[pltpu.HBM(b.shape, b.dtype) for b in bufs], jax.ShapeDtypeStruct((8, 128), F32)),
        in_specs=(HBM,) * n + ((ANY,) if dep is not None else ()),
        out_specs=(SEM, SEM) + (HBM,) * n + (pl.BlockSpec(memory_space=pltpu.VMEM),),
        input_output_aliases={i: i + 2 for i in range(n)},
        compiler_params=pltpu.CompilerParams(has_side_effects=SIDE_EFFECT),
    )(*[pltpu.with_memory_space_constraint(b, pltpu.HBM) for b in bufs], *(() if dep is None else (dep,)))
    return outs[0], outs[1], list(outs[2:2 + n]), outs[2 + n]


def _wait(name, started, plan, after):
    send, recv, bufs, _ = started
    n = len(bufs)

    def body(*refs):
        sends, recvs = plan(refs[:n], refs[n], refs[n + 1])
        for cp in sends:
            cp.wait_send()
        for cp in recvs:
            cp.wait_recv()

    return list(pl.pallas_call(
        body, name=name, out_shape=tuple(pltpu.HBM(b.shape, b.dtype) for b in bufs),
        in_specs=(HBM,) * n + (SEM, SEM, ANY), out_specs=(HBM,) * n, input_output_aliases={i: i for i in range(n)},
        compiler_params=pltpu.CompilerParams(has_side_effects=SIDE_EFFECT),
    )(*bufs, send, recv, after))


KINDS = ("col", "row")


def _gather_ici_plan(n_small):
    def plan(refs, send, recv):
        x, y, c, s, chips = _place()
        n = len(KINDS) + n_small
        sends, recvs = [], []
        for k, (px, py) in enumerate(chips):
            sp = 2 * px + py
            for a, kind in enumerate(KINDS):
                mine, theirs = _half(_shard(refs[a], kind, s), kind, c), _half(_shard(refs[a], kind, sp), kind, c)
                sends.append(_remote(mine, mine, send.at[k * n + a], recv.at[k * n + a], (px, py, c)))
                recvs.append(_remote(theirs, theirs, send.at[k * n + a], recv.at[k * n + a], (px, py, c)))
            for b in range(n_small):
                ref, sem = refs[len(KINDS) + b], k * n + len(KINDS) + b
                sends.append(_remote(ref.at[s], ref.at[s], send.at[sem], recv.at[sem], (px, py, c)))
                recvs.append(_remote(ref.at[sp], ref.at[sp], send.at[sem], recv.at[sem], (px, py, c)))
        return sends, recvs
    return plan


def _gather_d2d_plan(refs, send, recv):
    x, y, c, _, chips = _place()
    n = len(KINDS)
    sends, recvs = [], []
    for k, (px, py) in enumerate(chips):
        for a, kind in enumerate(KINDS):
            region, sem = _shard(refs[a], kind, 2 * px + py), k * n + a
            sends.append(_remote(_half(region, kind, c), _half(region, kind, c), send.at[sem], recv.at[sem], (x, y, 1 - c)))
            recvs.append(_remote(_half(region, kind, 1 - c), _half(region, kind, 1 - c), send.at[sem], recv.at[sem], (x, y, 1 - c)))
    return sends, recvs


def _swap_plan(refs, send, recv):
    x, y, c, _, _ = _place()
    n = len(KINDS)
    cps = [_remote(_half(refs[a], KINDS[a], 1 - c), refs[n + a], send.at[a], recv.at[a], (x, y, 1 - c)) for a in range(n)]
    return cps, cps


def _exchange_plan(refs, send, recv):
    x, y, c, _, chips = _place()
    n = len(KINDS)
    cps = []
    for k, (px, py) in enumerate(chips):
        for a in range(n):
            cps.append(_remote(_shard(refs[a], KINDS[a], 2 * px + py), refs[n + a].at[k], send.at[k * n + a], recv.at[k * n + a],
                               (px, py, c)))
    return cps, cps


def _share_plan(refs, send, recv):
    x, y, c, _, _ = _place()
    sends = [_remote(_half(refs[a], KINDS[a], c), _half(refs[a], KINDS[a], c), send.at[a], recv.at[a], (x, y, 1 - c))
             for a in range(len(KINDS))]
    recvs = [_remote(_half(refs[a], KINDS[a], 1 - c), _half(refs[a], KINDS[a], 1 - c), send.at[a], recv.at[a], (x, y, 1 - c))
             for a in range(len(KINDS))]
    return sends, recvs


def _spread_plan(refs, send, recv):
    packed, slots = refs
    x, y, c, _, _ = _place()
    sends, recvs = [], []
    for k in range(1, 8):
        px, py, pc = x ^ (k >> 2), y ^ ((k >> 1) & 1), c ^ (k & 1)
        sends.append(_remote(packed, slots.at[4 * x + 2 * y + c], send.at[k - 1], recv.at[k - 1], (px, py, pc)))
        recvs.append(_remote(packed, slots.at[4 * px + 2 * py + pc], send.at[k - 1], recv.at[k - 1], (px, py, pc)))
    return sends, recvs


def _half_index(kind, nblk):
    def index(i, j, place):
        return (i + place[1] * nblk[0], j) if kind == "col" else (i, j + place[1] * nblk[1])
    return index


def _chip_partial(g, other, kind, place, *, name):
    hr, hc = other.shape
    rb, cb = _tile(hr, 512, 16), _tile(hc, 1024)
    nblk = (hr // rb, hc // cb)

    def body(place_ref, g_ref, o_ref, p_ref):
        p_ref[...] = (g_ref[...].astype(F32) + o_ref[...].astype(F32)).astype(BF16)

    plain = pl.BlockSpec((rb, cb), lambda i, j, place: (i, j))
    return pl.pallas_call(
        body,
        grid_spec=pltpu.PrefetchScalarGridSpec(
            num_scalar_prefetch=1, grid=nblk, in_specs=[pl.BlockSpec((rb, cb), _half_index(kind, nblk)), plain], out_specs=plain),
        out_shape=jax.ShapeDtypeStruct((hr, hc), BF16), compiler_params=_params(("parallel", "parallel")), name=name,
    )(place, g, other)


def _reduce_half(g, other, recv, kind, place, *, name):
    _, pr, pc = recv.shape
    rb, cb = _tile(pr, 512, 16), _tile(pc, 1024)
    nblk = (pr // rb, pc // cb)
    full = (pr * 2, pc) if kind == "col" else (pr, pc * 2)

    def g_index(i, j, place):
        s, c = place[0], place[1]
        return (i + c * nblk[0], j + s * nblk[1]) if kind == "col" else (i + s * nblk[0], j + c * nblk[1])

    def o_index(i, j, place):
        return (i, j + place[0] * nblk[1]) if kind == "col" else (i + place[0] * nblk[0], j)

    def body(place_ref, g_ref, o_ref, r_ref, out_ref):
        acc = g_ref[...].astype(F32) + o_ref[...].astype(F32)
        for k in range(3):
            acc = acc + r_ref[k].astype(F32)
        out_ref[...] = acc

    return pl.pallas_call(
        body,
        grid_spec=pltpu.PrefetchScalarGridSpec(
            num_scalar_prefetch=1, grid=nblk,
            in_specs=[pl.BlockSpec((rb, cb), g_index), pl.BlockSpec((rb, cb), o_index),
                      pl.BlockSpec((3, rb, cb), lambda i, j, place: (0, i, j))],
            out_specs=pl.BlockSpec((rb, cb), _half_index(kind, nblk))),
        out_shape=jax.ShapeDtypeStruct(full, F32), compiler_params=_params(("parallel", "parallel")), name=name,
    )(place, g, other, recv)


def _sum_slots(packed, slots, me, *, name, rb=512):
    r, w = packed.shape
    rb = _tile(r, rb, 8)

    def body(me_ref, p_ref, s_ref, o_ref):
        acc = None
        for j in range(8):
            term = jnp.where(me_ref[0] == j, p_ref[...], s_ref[j])
            acc = term if acc is None else acc + term
        o_ref[...] = acc

    return pl.pallas_call(
        body,
        grid_spec=pltpu.PrefetchScalarGridSpec(
            num_scalar_prefetch=1, grid=(r // rb,),
            in_specs=[pl.BlockSpec((rb, w), lambda i, me: (i, 0)), pl.BlockSpec((8, rb, w), lambda i, me: (0, i, 0))],
            out_specs=pl.BlockSpec((rb, w), lambda i, me: (i, 0))),
        out_shape=jax.ShapeDtypeStruct((r, w), F32), compiler_params=_params(("parallel",)), name=name,
    )(me, packed, slots)


def _half_shape(a, kind):
    return (a.shape[0] // 2, a.shape[1]) if kind == "col" else (a.shape[0], a.shape[1] // 2)


def _rs_swap(tag, grads):
    others = [lax.empty(_half_shape(g, k), g.dtype) for g, k in zip(grads, KINDS)]
    return _start(f"rs_{tag}_swap", list(grads) + others, _swap_plan, (len(KINDS),))


def _rs_exchange(tag, swapped, place, after):
    bufs = _wait(f"rs_{tag}_swap_wait", swapped, _swap_plan, after)
    n = len(KINDS)
    grads, others = bufs[:n], bufs[n:]
    parts = [_chip_partial(g, o, k, place, name=f"rs_{tag}_partial_{k}") for g, o, k in zip(grads, others, KINDS)]
    lands = []
    for p, k in zip(parts, KINDS):
        piece = (p.shape[0], p.shape[1] // N_CHIPS) if k == "col" else (p.shape[0] // N_CHIPS, p.shape[1])
        lands.append(lax.empty((3,) + piece, p.dtype))
    return _start(f"rs_{tag}_exchange", parts + lands, _exchange_plan, (3 * n,)), grads, others


def _rs_share(tag, exchanged, place, after):
    started, grads, others = exchanged
    n = len(KINDS)
    recvs = _wait(f"rs_{tag}_exchange_wait", started, _exchange_plan, after)[n:]
    halves = [_reduce_half(g, o, r, k, place, name=f"rs_{tag}_reduce_{k}") for g, o, r, k in zip(grads, others, recvs, KINDS)]
    return _start(f"rs_{tag}_share", halves, _share_plan, (n,))


def _rs_finish(tag, shared, after):
    return _wait(f"rs_{tag}_share_wait", shared, _share_plan, after)


def _spread(tag, parts, dep):
    rows = [p.reshape(-1, 128) for p in parts]
    n = sum(r.shape[0] for r in rows)
    rows.append(jnp.zeros(((-n) % 512, 128), F32))
    packed = jnp.concatenate(rows, axis=0)
    return _start(f"small_{tag}_spread", [packed, lax.empty((8,) + packed.shape, F32)], _spread_plan, (7,), dep=dep)


def _spread_sum(tag, started, parts, me, after):
    packed, slots = _wait(f"small_{tag}_spread_wait", started, _spread_plan, after)
    total = _sum_slots(packed, slots, me, name=f"small_{tag}_sum")
    out, row = [], 0
    for p in parts:
        n = p.size // 128
        out.append(total[row:row + n].reshape(p.shape))
        row += n
    return out


def _ffn_fwd(x, h, w_up, conv_w, conv_b, w_down, tag):
    up = _mm(h, w_up, name=f"ffn{tag}_up")
    act = _ffn_mid_fwd(up, conv_w, conv_b, name=f"ffn{tag}_mid")
    x_out = _mm(act, w_down, res=x, tk=2816, name=f"ffn{tag}_down")
    return x_out, (up, act)


def kernel(x, a_norm, a_in, a_conv, a_out, b_norm, b_in, b_vnorm, b_ws, b_bs, b_out, f_norm, f_up, f_conv_w, f_conv_b, f_down, final_norm, loss_target, m_a_norm, m_a_in, m_a_conv, m_a_out, m_b_norm, m_b_in, m_b_vnorm, m_b_ws, m_b_bs, m_b_out, m_f_norm, m_f_up, m_f_conv_w, m_f_conv_b, m_f_down, m_final_norm, v_a_norm, v_a_in, v_a_conv, v_a_out, v_b_norm, v_b_in, v_b_vnorm, v_b_ws, v_b_bs, v_b_out, v_f_norm, v_f_up, v_f_conv_w, v_f_conv_b, v_f_down, v_final_norm):
    t, d = x.shape[1], x.shape[2]
    f2 = f_up.shape[2] * N_CHIPS
    x0, tgt = x.reshape(t, d), loss_target.reshape(t, d)
    ax, ay, ac = lax.axis_index("x"), lax.axis_index("y"), lax.axis_index("c")
    s = 2 * ax + ay
    place = jnp.stack([s, ac]).astype(jnp.int32)
    me = (4 * ax + 2 * ay + ac).astype(jnp.int32).reshape(1)

    def stacked(a):
        return lax.dynamic_update_index_in_dim(jnp.zeros((N_CHIPS,) + a.shape, F32), a, s, 0)

    def gather_start(tag, w_in, w_out, layer, small, dep):
        fulls = [_cast_into_full(w_in, layer, "col", place, name=f"cast_{tag}_in", dep=dep),
                 _cast_into_full(w_out, layer, "row", place, name=f"cast_{tag}_out", dep=dep)]
        return _start(f"ag_{tag}_ici", fulls + small, _gather_ici_plan(len(small)), (3 * (2 + len(small)),), dep=dep)

    def gather_forward(tag, started, n_small, after):
        bufs = _wait(f"ag_{tag}_ici_wait", started, _gather_ici_plan(n_small), after)
        return _start(f"ag_{tag}_d2d", bufs[:2], _gather_d2d_plan, (3 * 2,)), bufs[2:]

    def gather_finish(tag, forwarded, after):
        return _wait(f"ag_{tag}_d2d_wait", forwarded, _gather_d2d_plan, after)

    small = [stacked(a_conv[0]), stacked(b_norm), stacked(b_vnorm), stacked(f_conv_w.reshape(2 * 3, -1))]
    ag_a = gather_start("a", a_in, a_out, 0, small, None)
    ag_f0 = gather_start("f0", f_up, f_down, 0, [], ag_a[3])
    ag_b = gather_start("b", b_in, b_out, 0, [], ag_f0[3])
    ag_f1 = gather_start("f1", f_up, f_down, 1, [], ag_b[3])

    def unshard(a):
        return jnp.transpose(a, (1, 0, 2)).reshape(a.shape[1], -1)

    ws, bs = b_ws[0], b_bs[0]

    h0 = _rms_fwd(x0, a_norm, dep=ag_f1[3], name="a_norm")
    fw_a, (g_aconv, g_bnorm, g_bvnorm, g_fconv) = gather_forward("a", ag_a, 4, h0)
    w_ai, w_ao = gather_finish("a", fw_a, fw_a[3])
    a_conv_f, b_norm_f, b_vnorm_f = unshard(g_aconv), unshard(g_bnorm), unshard(g_bvnorm)
    f_conv_f = unshard(g_fconv).reshape(2, 3, f2)
    bcx = _mm(h0, w_ai, name="a_in")
    y = _a_mid_fwd(bcx, a_conv_f, name="a_mid")
    x1 = _mm(y, w_ao, res=x0, tm=512, tn=2048, name="a_out")
    fw_f0, _ = gather_forward("f0", ag_f0, 0, x1)
    h1 = _rms_fwd(x1, f_norm[0:1], dep=fw_f0[3], name="ffn0_norm")
    w_up0, w_dn0 = gather_finish("f0", fw_f0, h1)
    x2, (up0, act0) = _ffn_fwd(x1, h1, w_up0, f_conv_f[0], f_conv_b[0:1], w_dn0, 0)
    fw_b, _ = gather_forward("b", ag_b, 0, up0)
    w_bi, w_bo = gather_finish("b", fw_b, act0)
    h2 = _rms_fwd(x2, b_norm_f, name="b_norm")
    zp = _mm(h2, w_bi, name="b_in")
    fw_f1, _ = gather_forward("f1", ag_f1, 0, zp)
    ug = _b_mid_fwd(zp, b_vnorm_f, ws, bs, name="b_mid")
    x3 = _mm(ug, w_bo, res=x2, tm=512, tn=2048, name="b_out")
    w_up1, w_dn1 = gather_finish("f1", fw_f1, x3)
    h3 = _rms_fwd(x3, f_norm[1:2], name="ffn1_norm")
    x4, (up1, act1) = _ffn_fwd(x3, h3, w_up1, f_conv_f[1], f_conv_b[1:2], w_dn1, 1)
    loss_rows, dx4, dx4b, d_final = _final(x4, tgt, final_norm.reshape(1, d), name="final")

    d_dn1 = _mm(act1, dx4b, ta=True, tm=1408, out_dtype=BF16, name="ffn1_ddown")
    dact1 = _mm(dx4b, w_dn1, tb=True, tn=512, tm=2048, name="ffn1_dact")
    dup1, d_fwb1 = _ffn_mid_bwd(up1, dact1, f_conv_f[1], f_conv_b[1:2], name="ffn1_mid_bwd")
    d_up1 = _mm(h3, dup1, ta=True, out_dtype=BF16, name="ffn1_dup")
    sw_f1 = _rs_swap("f1", [d_up1, d_dn1])
    dh3 = _mm(dup1, w_up1, tb=True, tk=2816, dep=sw_f1[3], name="ffn1_dh")
    dx3, dx3b, d_fnorm1 = _rms_bwd(dh3, x3, f_norm[1:2], dx4, name="ffn1_norm_bwd")
    ex_f1 = _rs_exchange("f1", sw_f1, place, dx3)

    d_bo = _mm(ug, dx3b, ta=True, out_dtype=BF16, dep=ex_f1[0][3], name="b_dout")
    dug = _mm(dx3b, w_bo, tb=True, tm=512, tn=2048, name="b_dug")
    dzp, d_ws, d_bs, d_bvnorm = _b_mid_bwd(zp, dug, b_vnorm_f, ws, bs, name="b_mid_bwd")
    d_bi = _mm(h2, dzp, ta=True, out_dtype=BF16, name="b_din")
    sw_b = _rs_swap("b", [d_bi, d_bo])
    dh2 = _mm(dzp, w_bi, tb=True, dep=sw_b[3], name="b_dh")
    dx2, dx2b, d_bnorm = _rms_bwd(dh2, x2, b_norm_f, dx3, name="b_norm_bwd")
    ex_b = _rs_exchange("b", sw_b, place, dx2)

    d_dn0 = _mm(act0, dx2b, ta=True, tm=1408, out_dtype=BF16, dep=ex_b[0][3], name="ffn0_ddown")
    dact0 = _mm(dx2b, w_dn0, tb=True, tn=512, tm=2048, name="ffn0_dact")
    dup0, d_fwb0 = _ffn_mid_bwd(up0, dact0, f_conv_f[0], f_conv_b[0:1], name="ffn0_mid_bwd")
    sh_f1 = _rs_share("f1", ex_f1, place, dup0)
    d_up0 = _mm(h1, dup0, ta=True, out_dtype=BF16, dep=sh_f1[3], name="ffn0_dup")
    sw_f0 = _rs_swap("f0", [d_up0, d_dn0])
    g_up1, g_dn1 = _rs_finish("f1", sh_f1, sw_f0[3])
    dh1 = _mm(dup0, w_up0, tb=True, tk=2816, dep=sw_f0[3], name="ffn0_dh")
    dx1, dx1b, d_fnorm0 = _rms_bwd(dh1, x1, f_norm[0:1], dx2, name="ffn0_norm_bwd")
    ex_f0 = _rs_exchange("f0", sw_f0, place, dx1)
    early = [jnp.concatenate([d_bnorm, d_bvnorm, d_fnorm0, d_fnorm1, d_final, loss_rows], axis=0),
             jnp.concatenate([d_fwb0, d_fwb1], axis=0), jnp.concatenate([d_ws.reshape(-1, CHUNK), d_bs], axis=0)]
    sp_early = _spread("early", early, ex_f0[0][3])
    sh_b = _rs_share("b", ex_b, place, sp_early[3])

    d_ao = _mm(y, dx1b, ta=True, out_dtype=BF16, dep=sh_b[3], name="a_dout")
    dyy = _mm(dx1b, w_ao, tb=True, tm=512, tn=2048, name="a_dy")
    dbcx, d_aconv = _a_mid_bwd(bcx, dyy, a_conv_f, name="a_mid_bwd")
    d_ai = _mm(h0, dbcx, ta=True, out_dtype=BF16, name="a_din")
    sw_a = _rs_swap("a", [d_ai, d_ao])
    g_bi, g_bo = _rs_finish("b", sh_b, sw_a[3])
    ex_a = _rs_exchange("a", sw_a, place, g_bi)
    dh0 = _mm(dbcx, w_ai, tb=True, dep=ex_a[0][3], name="a_dh")
    grad_x, _, d_anorm = _rms_bwd(dh0, x0, a_norm, dx1, name="a_norm_bwd")
    late = [jnp.concatenate([d_anorm, d_aconv], axis=0)]
    sp_late = _spread("late", late, ex_a[0][3])
    sh_f0 = _rs_share("f0", ex_f0, place, sp_late[3])
    sh_a = _rs_share("a", ex_a, place, sh_f0[3])
    g_up0, g_dn0 = _rs_finish("f0", sh_f0, sh_a[3])
    g_ai, g_ao = _rs_finish("a", sh_a, g_up0)
    r_a, r_b, r_c = _spread_sum("early", sp_early, early, me, g_ai)
    (r_l,) = _spread_sum("late", sp_late, late, me, r_a)

    loss = jnp.sum(r_a[40])
    cs, fs = d // N_CHIPS, f2 // N_CHIPS

    def mine(a, width):
        return lax.dynamic_slice_in_dim(a, s * width, width, axis=1)

    grads = {
        "a_norm": r_l[0:1], "a_conv": mine(r_l[8:11], cs), "b_norm": mine(r_a[0:1], cs), "b_vnorm": mine(r_a[8:9], cs),
        "f_norm": jnp.concatenate([r_a[16:17], r_a[24:25]], axis=0), "final_norm": r_a[32:33],
        "b_ws": r_c[:GROUPS * CHUNK], "b_bs": r_c[GROUPS * CHUNK:],
        "f_conv_w": jnp.concatenate([mine(r_b[0:3], fs), mine(r_b[8:11], fs)], axis=0),
        "f_conv_b": jnp.concatenate([r_b[3:4], r_b[11:12]], axis=0),
        "a_in": g_ai, "a_out": g_ao, "b_in": g_bi, "b_out": g_bo,
    }
    names = ["a_norm", "a_in", "a_conv", "a_out", "b_norm", "b_in", "b_vnorm", "b_ws", "b_bs", "b_out", "f_norm", "f_up",
             "f_conv_w", "f_conv_b", "f_down", "final_norm"]
    weights = dict(zip(names, [a_norm, a_in, a_conv, a_out, b_norm, b_in, b_vnorm, b_ws, b_bs, b_out, f_norm, f_up, f_conv_w,
                               f_conv_b, f_down, final_norm]))
    ms = dict(zip(names, [m_a_norm, m_a_in, m_a_conv, m_a_out, m_b_norm, m_b_in, m_b_vnorm, m_b_ws, m_b_bs, m_b_out, m_f_norm,
                          m_f_up, m_f_conv_w, m_f_conv_b, m_f_down, m_final_norm]))
    vs = dict(zip(names, [v_a_norm, v_a_in, v_a_conv, v_a_out, v_b_norm, v_b_in, v_b_vnorm, v_b_ws, v_b_bs, v_b_out, v_f_norm,
                          v_f_up, v_f_conv_w, v_f_conv_b, v_f_down, v_final_norm]))
    result = {}
    for n in names:
        w = weights[n]
        if n in ("f_up", "f_down"):
            g1, g0 = (g_up1, g_up0) if n == "f_up" else (g_dn1, g_dn0)
            first = _adamw_layer(w, g1, ms[n], vs[n], 1, None, name=f"adamw_{n}1")
            result[n] = _adamw_layer(w, g0, ms[n], vs[n], 0, tuple(first), name=f"adamw_{n}0")
            continue
        g2 = grads[n]
        as3d = (lambda a: a.reshape((1,) + g2.shape))
        result[n] = [o.reshape(w.shape) for o in _adamw_layer(as3d(w), g2, as3d(ms[n]), as3d(vs[n]), 0, None, name=f"adamw_{n}")]

    return (loss, grad_x.reshape(x.shape), *[result[n][0] for n in names], *[result[n][1] for n in names],
            *[result[n][2] for n in names], *[result[n][3] for n in names])
```

```python
import functools

import jax
import jax.numpy as jnp
from jax import lax
from jax.experimental import pallas as pl
from jax.experimental.pallas import tpu as pltpu

F32 = jnp.float32
BF16 = jnp.bfloat16
MESH = pl.DeviceIdType.MESH
ANY = pl.BlockSpec(memory_space=pl.ANY)

RMS_EPS = 1e-5
CHUNK = 128
GROUPS = 8
ADAM_LR, ADAM_B1, ADAM_B2, ADAM_EPS, ADAM_WD, ADAM_STEP = 0.001, 0.9, 0.999, 1e-08, 0.01, 10

N_CHIPS = 4
HALO = 8
VMEM_LIMIT = 56 * 1024 * 1024
GELU_C = 0.7978845608028654
GELU_A = 0.044715


def _params(sem=None):
    return pltpu.CompilerParams(dimension_semantics=sem, vmem_limit_bytes=VMEM_LIMIT)


def _tile(dim, pref, quantum=128):
    if dim <= pref:
        return dim
    t = (pref // quantum) * quantum
    while t >= quantum:
        if dim % t == 0:
            return t
        t -= quantum
    return dim


def _mm(a, b, *, name, ta=False, tb=False, res=None, dep=None, out_dtype=F32, tm=1024, tn=1024, tk=2048):
    (K, M) = a.shape if ta else a.shape[::-1]
    parts = b if isinstance(b, tuple) else (b,)
    n_part = parts[0].shape[0] if tb else parts[0].shape[1]
    N = n_part * len(parts)
    assert (parts[0].shape[1] if tb else parts[0].shape[0]) == K and not (tb and len(parts) > 1)
    tm, tn, tk = _tile(M, tm), _tile(n_part, tn), _tile(K, tk)
    nk, nj = K // tk, n_part // tn
    a_spec = pl.BlockSpec((tk, tm), lambda i, j, k: (k, i)) if ta else pl.BlockSpec((tm, tk), lambda i, j, k: (i, k))
    if len(parts) == 1:
        b_specs = [pl.BlockSpec((tn, tk), lambda i, j, k: (j, k)) if tb else pl.BlockSpec((tk, tn), lambda i, j, k: (k, j))]
    else:
        def part_spec(p):
            def index(i, j, k):
                mine = jnp.logical_and(j >= p * nj, j < (p + 1) * nj)
                return (jnp.where(mine, k, 0), jnp.where(mine, j - p * nj, 0))
            return pl.BlockSpec((tk, tn), index)
        b_specs = [part_spec(p) for p in range(len(parts))]
    o_spec = pl.BlockSpec((tm, tn), lambda i, j, k: (i, j))
    dims = (((0 if ta else 1,), (1 if tb else 0,)), ((), ()))
    direct = out_dtype == F32
    n_b = len(parts)

    def body(*refs):
        a_ref, b_refs = refs[0], refs[1:1 + n_b]
        r_ref = refs[1 + n_b] if res is not None else None
        o_ref = refs[1 + n_b + (res is not None) + (dep is not None)]
        acc_ref = o_ref if direct else refs[-1]

        def step(b_ref):
            part = lax.dot_general(a_ref[...], b_ref[...], dims, preferred_element_type=F32)
            if nk == 1:
                if r_ref is not None:
                    part = part + r_ref[...]
                o_ref[...] = part.astype(o_ref.dtype)
                return
            k = pl.program_id(2)

            @pl.when(k == 0)
            def _():
                acc_ref[...] = part

            @pl.when(jnp.logical_and(k > 0, k < nk - 1))
            def _():
                acc_ref[...] += part

            @pl.when(k == nk - 1)
            def _():
                tot = acc_ref[...] + part
                if r_ref is not None:
                    tot = tot + r_ref[...]
                o_ref[...] = tot.astype(o_ref.dtype)

        if n_b == 1:
            step(b_refs[0])
        else:
            for p in range(n_b):
                pl.when(pl.program_id(1) // nj == p)(functools.partial(step, b_refs[p]))

    in_specs = [a_spec] + b_specs + ([o_spec] if res is not None else []) + ([ANY] if dep is not None else [])
    args = (a,) + parts + ((res,) if res is not None else ()) + ((dep,) if dep is not None else ())
    scratch = [] if (direct or nk == 1) else [pltpu.VMEM((tm, tn), F32)]
    return pl.pallas_call(
        body, grid=(M // tm, N // tn, nk), in_specs=in_specs, out_specs=o_spec,
        out_shape=jax.ShapeDtypeStruct((M, N), out_dtype), scratch_shapes=scratch,
        compiler_params=_params(("parallel", "parallel", "arbitrary")), name=name,
    )(*args)


def _row_spec(rb, w):
    return pl.BlockSpec((rb, w), lambda i: (i, 0))


def _prev_spec(rb, w):
    return pl.BlockSpec((HALO, w), lambda i: (jnp.maximum(i * (rb // HALO) - 1, 0), 0))


def _next_spec(rb, w, t):
    return pl.BlockSpec((HALO, w), lambda i: (jnp.minimum((i + 1) * (rb // HALO), t // HALO - 1), 0))


def _full_spec(shape):
    return pl.BlockSpec(shape, lambda i: tuple(0 for _ in shape))


def _shift(e, s):
    return pltpu.roll(e, s % e.shape[0], 0)


def _gelu(x):
    return 0.5 * x * (1.0 + jnp.tanh(GELU_C * (x + GELU_A * x * x * x)))


def _gelu_grad(x):
    th = jnp.tanh(GELU_C * (x + GELU_A * x * x * x))
    return 0.5 * (1.0 + th) + 0.5 * x * (1.0 - th * th) * (GELU_C * (1.0 + 3.0 * GELU_A * x * x))


def _sigmoid(x):
    return 1.0 / (1.0 + jnp.exp(-x))


def _rms_fwd(x, g, *, name, dep=None, rb=256):
    t, d = x.shape
    rb = _tile(t, rb, 8)

    def body(x_ref, g_ref, *rest):
        h_ref = rest[-1]
        xv = x_ref[...]
        r = lax.rsqrt(jnp.mean(xv * xv, axis=-1, keepdims=True) + RMS_EPS)
        h_ref[...] = ((xv * r) * g_ref[...]).astype(BF16)

    return pl.pallas_call(
        body, grid=(t // rb,), in_specs=[_row_spec(rb, d), _full_spec((1, d))] + ([ANY] if dep is not None else []),
        out_specs=_row_spec(rb, d), out_shape=jax.ShapeDtypeStruct((t, d), BF16), compiler_params=_params(("parallel",)), name=name,
    )(x, g, *(() if dep is None else (dep,)))


def _rms_bwd(dh, x, g, dres, *, name, dep=None, rb=256):
    t, d = x.shape
    rb = _tile(t, rb, 8)

    def body(dh_ref, x_ref, g_ref, dres_ref, *rest):
        dx_ref, dxb_ref, dg_ref = rest[-3:]
        xv = x_ref[...]
        r = lax.rsqrt(jnp.mean(xv * xv, axis=-1, keepdims=True) + RMS_EPS)
        xhat = xv * r
        dh_v = dh_ref[...]
        dxhat = dh_v * g_ref[...]
        m = jnp.mean(dxhat * xhat, axis=-1, keepdims=True)
        dx = dres_ref[...] + r * (dxhat - xhat * m)
        dx_ref[...] = dx
        dxb_ref[...] = dx.astype(BF16)

        @pl.when(pl.program_id(0) == 0)
        def _():
            dg_ref[...] = jnp.zeros_like(dg_ref)

        dg_ref[0:1, :] += jnp.sum(dh_v * xhat, axis=0, keepdims=True)

    return pl.pallas_call(
        body, grid=(t // rb,),
        in_specs=[_row_spec(rb, d), _row_spec(rb, d), _full_spec((1, d)), _row_spec(rb, d)] + ([ANY] if dep is not None else []),
        out_specs=[_row_spec(rb, d), _row_spec(rb, d), _full_spec((8, d))],
        out_shape=[jax.ShapeDtypeStruct((t, d), F32), jax.ShapeDtypeStruct((t, d), BF16), jax.ShapeDtypeStruct((8, d), F32)],
        compiler_params=_params(("arbitrary",)), name=name,
    )(dh, x, g, dres, *(() if dep is None else (dep,)))


def _final(x, tgt, g, *, name, rb=256):
    t, d = x.shape
    rb = _tile(t, rb, 8)
    inv_d = 1.0 / d

    def body(x_ref, t_ref, g_ref, l_ref, dx_ref, dxb_ref, dg_ref):
        xv = x_ref[...]
        gv = g_ref[...]
        r = lax.rsqrt(jnp.mean(xv * xv, axis=-1, keepdims=True) + RMS_EPS)
        xhat = xv * r
        e = xhat * gv - t_ref[...]
        dy = e * inv_d
        dxhat = dy * gv
        m = jnp.mean(dxhat * xhat, axis=-1, keepdims=True)
        dx = r * (dxhat - xhat * m)
        dx_ref[...] = dx
        dxb_ref[...] = dx.astype(BF16)

        @pl.when(pl.program_id(0) == 0)
        def _():
            l_ref[...] = jnp.zeros_like(l_ref)
            dg_ref[...] = jnp.zeros_like(dg_ref)

        l_ref[0:1, :] += jnp.sum(e * e, axis=0, keepdims=True) * (0.5 * inv_d)
        dg_ref[0:1, :] += jnp.sum(dy * xhat, axis=0, keepdims=True)

    return pl.pallas_call(
        body, grid=(t // rb,),
        in_specs=[_row_spec(rb, d), _row_spec(rb, d), _full_spec((1, d))],
        out_specs=[_full_spec((8, d)), _row_spec(rb, d), _row_spec(rb, d), _full_spec((8, d))],
        out_shape=[jax.ShapeDtypeStruct((8, d), F32), jax.ShapeDtypeStruct((t, d), F32),
                   jax.ShapeDtypeStruct((t, d), BF16), jax.ShapeDtypeStruct((8, d), F32)],
        compiler_params=_params(("arbitrary",)), name=name,
    )(x, tgt, g)


def _a_mid_fwd(bcx, wconv, *, name, rb=256, cw=512):
    t, d3 = bcx.shape
    d = d3 // 3
    rb, cw = _tile(t, rb, 8), _tile(d, cw)

    def body(cur_ref, prev_ref, w_ref, y_ref):
        first = pl.program_id(0) == 0
        for c0 in range(0, d, cw):
            cs = slice(c0, c0 + cw)
            gc, xs = slice(d + c0, d + c0 + cw), slice(2 * d + c0, 2 * d + c0 + cw)
            p_prev = jnp.where(first, 0.0, prev_ref[:, gc] * prev_ref[:, xs])
            e = jnp.concatenate([p_prev, cur_ref[:, gc] * cur_ref[:, xs]], axis=0)
            w = w_ref[:, cs]
            q = w[0:1] * _shift(e, 2) + w[1:2] * _shift(e, 1) + w[2:3] * e
            y_ref[:, cs] = (cur_ref[:, cs] * q[HALO:]).astype(BF16)

    return pl.pallas_call(
        body, grid=(t // rb,),
        in_specs=[_row_spec(rb, d3), _prev_spec(rb, d3), _full_spec((3, d))], out_specs=_row_spec(rb, d),
        out_shape=jax.ShapeDtypeStruct((t, d), BF16), compiler_params=_params(("parallel",)), name=name,
    )(bcx, bcx, wconv)


def _a_mid_bwd(bcx, dy, wconv, *, name, rb=128, cw=512):
    t, d3 = bcx.shape
    d = d3 // 3
    rb, cw = _tile(t, rb, 8), _tile(d, cw)

    def body(cur_ref, prev_ref, next_ref, dy_ref, dyn_ref, w_ref, o_ref, dw_ref):
        i = pl.program_id(0)
        first, last = i == 0, i == pl.num_programs(0) - 1

        @pl.when(first)
        def _():
            dw_ref[...] = jnp.zeros_like(dw_ref)

        for c0 in range(0, d, cw):
            cs = slice(c0, c0 + cw)
            gc, xs = slice(d + c0, d + c0 + cw), slice(2 * d + c0, 2 * d + c0 + cw)
            zeros = jnp.zeros((HALO, cw), F32)
            gb_c, gc_c, xs_c = cur_ref[:, cs], cur_ref[:, gc], cur_ref[:, xs]
            p_prev = jnp.where(first, 0.0, prev_ref[:, gc] * prev_ref[:, xs])
            e = jnp.concatenate([p_prev, gc_c * xs_c, zeros], axis=0)
            dq_next = jnp.where(last, 0.0, dyn_ref[:, cs] * next_ref[:, cs])
            dy_c = dy_ref[:, cs]
            dq = jnp.concatenate([zeros, dy_c * gb_c, dq_next], axis=0)
            w = w_ref[:, cs]
            e1, e2 = _shift(e, 1), _shift(e, 2)
            q = w[0:1] * e2 + w[1:2] * e1 + w[2:3] * e
            dp = (w[2:3] * dq + w[1:2] * _shift(dq, -1) + w[0:1] * _shift(dq, -2))[HALO:HALO + rb]
            o_ref[:, cs] = (dy_c * q[HALO:HALO + rb]).astype(BF16)
            o_ref[:, gc] = (dp * xs_c).astype(BF16)
            o_ref[:, xs] = (dp * gc_c).astype(BF16)
            dq_c = dq[HALO:HALO + rb]
            dw_ref[0:1, cs] += jnp.sum(dq_c * e2[HALO:HALO + rb], axis=0, keepdims=True)
            dw_ref[1:2, cs] += jnp.sum(dq_c * e1[HALO:HALO + rb], axis=0, keepdims=True)
            dw_ref[2:3, cs] += jnp.sum(dq_c * e[HALO:HALO + rb], axis=0, keepdims=True)

    return pl.pallas_call(
        body, grid=(t // rb,),
        in_specs=[_row_spec(rb, d3), _prev_spec(rb, d3), _next_spec(rb, d3, t), _row_spec(rb, d), _next_spec(rb, d, t),
                  _full_spec((3, d))],
        out_specs=[_row_spec(rb, d3), _full_spec((8, d))],
        out_shape=[jax.ShapeDtypeStruct((t, d3), BF16), jax.ShapeDtypeStruct((8, d), F32)],
        compiler_params=_params(("arbitrary",)), name=name,
    )(bcx, bcx, bcx, dy, dy, wconv)


def _ffn_mid_fwd(up, wconv, bconv, *, name, rb=128, cw=512):
    t, f2 = up.shape
    f = f2 // 2
    rb, cw = _tile(t, rb, 16), _tile(f, cw)

    def body(cur_ref, prev_ref, w_ref, b_ref, act_ref, conv_ref):
        first = pl.program_id(0) == 0

        def conv(cols):
            e = jnp.concatenate([jnp.where(first, 0.0, prev_ref[:, cols]), cur_ref[:, cols]], axis=0)
            w = w_ref[:, cols]
            out = (w[0:1] * _shift(e, 2) + w[1:2] * _shift(e, 1) + w[2:3] * e + b_ref[:, cols])[HALO:]
            conv_ref[:, cols] = out.astype(BF16)
            return out

        for c0 in range(0, f, cw):
            g = conv(slice(c0, c0 + cw))
            a = conv(slice(f + c0, f + c0 + cw))
            act_ref[:, c0:c0 + cw] = (g * _sigmoid(g) * a).astype(BF16)

    return pl.pallas_call(
        body, grid=(t // rb,),
        in_specs=[_row_spec(rb, f2), _prev_spec(rb, f2), _full_spec((3, f2)), _full_spec((1, f2))],
        out_specs=[_row_spec(rb, f), _row_spec(rb, f2)],
        out_shape=[jax.ShapeDtypeStruct((t, f), BF16), jax.ShapeDtypeStruct((t, f2), BF16)],
        compiler_params=_params(("parallel",)), name=name,
    )(up, up, wconv, bconv)


BF16_HALO = 16


def _ffn_mid_bwd(up, conv, dact, wconv, *, name, rb=128, cw=512):
    t, f2 = up.shape
    f = f2 // 2
    rb, cw = _tile(t, rb, 16), _tile(f, cw)

    def body(up_ref, conv_ref, convn_ref, da_ref, dan_ref, w_ref, o_ref, dwb_ref):
        i = pl.program_id(0)
        last = i == pl.num_programs(0) - 1

        @pl.when(i == 0)
        def _():
            dwb_ref[...] = jnp.zeros_like(dwb_ref)

        def rows(cols):
            return jnp.concatenate([conv_ref[:, cols].astype(F32), convn_ref[:, cols].astype(F32)[0:HALO]], axis=0)

        def back(dc, cols):
            w = w_ref[:, cols]
            dc1, dc2 = _shift(dc, -1)[:rb], _shift(dc, -2)[:rb]
            dc0 = dc[:rb]
            o_ref[:, cols] = (w[2:3] * dc0 + w[1:2] * dc1 + w[0:1] * dc2).astype(BF16)
            u = up_ref[:, cols]
            dwb_ref[0:1, cols] += jnp.sum(dc2 * u, axis=0, keepdims=True)
            dwb_ref[1:2, cols] += jnp.sum(dc1 * u, axis=0, keepdims=True)
            dwb_ref[2:3, cols] += jnp.sum(dc0 * u, axis=0, keepdims=True)
            dwb_ref[3:4, cols] += jnp.sum(dc0, axis=0, keepdims=True)

        for c0 in range(0, f, cw):
            gcols, acols = slice(c0, c0 + cw), slice(f + c0, f + c0 + cw)
            g, a = rows(gcols), rows(acols)
            da = jnp.concatenate([da_ref[:, gcols], jnp.where(last, 0.0, dan_ref[:, gcols])], axis=0)
            sg = _sigmoid(g)
            back(da * a * (sg * (1.0 + g * (1.0 - sg))), gcols)
            back(da * (g * sg), acols)

    return pl.pallas_call(
        body, grid=(t // rb,),
        in_specs=[_row_spec(rb, f2), _row_spec(rb, f2),
                  pl.BlockSpec((BF16_HALO, f2), lambda i: (jnp.minimum((i + 1) * (rb // BF16_HALO), t // BF16_HALO - 1), 0)),
                  _row_spec(rb, f), _next_spec(rb, f, t), _full_spec((3, f2))],
        out_specs=[_row_spec(rb, f2), _full_spec((8, f2))],
        out_shape=[jax.ShapeDtypeStruct((t, f2), BF16), jax.ShapeDtypeStruct((8, f2), F32)],
        compiler_params=_params(("arbitrary",)), name=name,
    )(up, conv, conv, dact, dact, wconv)


def _ffn_bwd_core(dxb, w_down, conv, up, w_up, wconv, *, name, tm=512, cw=512):
    t, d = dxb.shape
    f2 = conv.shape[1]
    f = f2 // 2
    tm, cw = _tile(t, tm, 8), _tile(f, cw)
    n_i, n_c = t // tm, f // cw
    nt = (((1,), (1,)), ((), ()))

    def body(dx_ref, wd_ref, cg_ref, ca_ref, ug_ref, ua_ref, wug_ref, wua_ref, wg_ref, wa_ref,
             dupg_ref, dupa_ref, dh_ref, dwb_ref, carry_g, carry_a):
        i, c = pl.program_id(0), pl.program_id(1)

        @pl.when(jnp.logical_and(i == 0, c == 0))
        def _():
            dwb_ref[...] = jnp.zeros_like(dwb_ref)

        dact = lax.dot_general(dx_ref[...], wd_ref[...], nt, preferred_element_type=F32)
        g, a = cg_ref[...].astype(F32), ca_ref[...].astype(F32)
        sg = _sigmoid(g)

        def back(dc, carry, u_ref, w_ref, out_ref, half):
            w = w_ref[...]
            e = jnp.concatenate([dc, jnp.where(i == 0, 0.0, carry[c])], axis=0)
            carry[c] = dc[0:HALO]
            e1, e2 = _shift(e, -1)[:tm], _shift(e, -2)[:tm]
            dup = (w[2:3] * dc + w[1:2] * e1 + w[0:1] * e2).astype(BF16)
            out_ref[...] = dup
            u = u_ref[...].astype(F32)
            k = c + half * n_c
            dwb_ref[k, 0:1, :] += jnp.sum(e2 * u, axis=0, keepdims=True)
            dwb_ref[k, 1:2, :] += jnp.sum(e1 * u, axis=0, keepdims=True)
            dwb_ref[k, 2:3, :] += jnp.sum(dc * u, axis=0, keepdims=True)
            dwb_ref[k, 3:4, :] += jnp.sum(dc, axis=0, keepdims=True)
            return dup

        dup_g = back(dact * a * (sg * (1.0 + g * (1.0 - sg))), carry_g, ug_ref, wg_ref, dupg_ref, 0)
        dup_a = back(dact * (g * sg), carry_a, ua_ref, wa_ref, dupa_ref, 1)
        part = (lax.dot_general(dup_g, wug_ref[...], nt, preferred_element_type=F32)
                + lax.dot_general(dup_a, wua_ref[...], nt, preferred_element_type=F32))

        @pl.when(c == 0)
        def _():
            dh_ref[...] = part

        @pl.when(c > 0)
        def _():
            dh_ref[...] += part

    def rows(i):
        return n_i - 1 - i

    act_g = pl.BlockSpec((tm, cw), lambda i, c: (rows(i), c))
    act_a = pl.BlockSpec((tm, cw), lambda i, c: (rows(i), c + n_c))
    return pl.pallas_call(
        body, grid=(n_i, n_c),
        in_specs=[pl.BlockSpec((tm, d), lambda i, c: (rows(i), 0)), pl.BlockSpec((cw, d), lambda i, c: (c, 0)),
                  act_g, act_a, act_g, act_a,
                  pl.BlockSpec((d, cw), lambda i, c: (0, c)), pl.BlockSpec((d, cw), lambda i, c: (0, c + n_c)),
                  pl.BlockSpec((3, cw), lambda i, c: (0, c)), pl.BlockSpec((3, cw), lambda i, c: (0, c + n_c))],
        out_specs=[act_g, act_g, pl.BlockSpec((tm, d), lambda i, c: (rows(i), 0)),
                   pl.BlockSpec((2 * n_c, 8, cw), lambda i, c: (0, 0, 0))],
        out_shape=[jax.ShapeDtypeStruct((t, f), BF16), jax.ShapeDtypeStruct((t, f), BF16), jax.ShapeDtypeStruct((t, d), F32),
                   jax.ShapeDtypeStruct((2 * n_c, 8, cw), F32)],
        scratch_shapes=[pltpu.VMEM((n_c, HALO, cw), F32), pltpu.VMEM((n_c, HALO, cw), F32)],
        compiler_params=_params(("arbitrary", "arbitrary")), name=name,
    )(dxb, w_down, conv, conv, up, up, w_up, w_up, wconv, wconv)


def _causal_mask():
    return lax.broadcasted_iota(jnp.int32, (CHUNK, CHUNK), 0) >= lax.broadcasted_iota(jnp.int32, (CHUNK, CHUNK), 1)


def _b_mid_fwd(zp, vnorm, ws, bs, *, name, rb=256):
    t, d2 = zp.shape
    d = d2 // 2
    c = d // GROUPS
    rb = _tile(t, rb, CHUNK)

    def body(zp_ref, gv_ref, ws_ref, bs_ref, ug_ref, vn_ref, gate_ref):
        v = _gelu(zp_ref[:, d:])
        rv = lax.rsqrt(jnp.mean(v * v, axis=-1, keepdims=True) + RMS_EPS)
        vn_ref[...] = ((v * rv) * gv_ref[...]).astype(BF16)
        mask = _causal_mask()
        for h in range(GROUPS):
            hc = slice(h * c, (h + 1) * c)
            wm = jnp.where(mask, ws_ref[h], 0.0).astype(BF16)
            bcol = jnp.broadcast_to(bs_ref[h:h + 1, :], (CHUNK, CHUNK)).T[:, 0:1]
            for n in range(rb // CHUNK):
                rows = slice(n * CHUNK, (n + 1) * CHUNK)
                gate_ref[rows, hc] = jnp.dot(wm, vn_ref[rows, hc], preferred_element_type=F32) + bcol
        ug_ref[...] = (_gelu(zp_ref[:, :d]) * gate_ref[...]).astype(BF16)

    return pl.pallas_call(
        body, grid=(t // rb,),
        in_specs=[_row_spec(rb, d2), _full_spec((1, d)), _full_spec((GROUPS, CHUNK, CHUNK)), _full_spec((GROUPS, CHUNK))],
        out_specs=_row_spec(rb, d), out_shape=jax.ShapeDtypeStruct((t, d), BF16),
        scratch_shapes=[pltpu.VMEM((rb, d), BF16), pltpu.VMEM((rb, d), F32)],
        compiler_params=_params(("parallel",)), name=name,
    )(zp, vnorm, ws, bs)


def _b_mid_bwd(zp, dug, vnorm, ws, bs, *, name, rb=256):
    t, d2 = zp.shape
    d = d2 // 2
    c = d // GROUPS
    rb = _tile(t, rb, CHUNK)

    def body(zp_ref, dug_ref, gv_ref, ws_ref, bs_ref, dzp_ref, dws_ref, dbs_ref, dgv_ref,
             vn_ref, gate_ref, dm_ref, dvn_ref, dbacc_ref):
        i = pl.program_id(0)

        @pl.when(i == 0)
        def _():
            dws_ref[...] = jnp.zeros_like(dws_ref)
            dgv_ref[...] = jnp.zeros_like(dgv_ref)
            dbacc_ref[...] = jnp.zeros_like(dbacc_ref)

        zu, zv = zp_ref[:, :d], zp_ref[:, d:]
        u, v = _gelu(zu), _gelu(zv)
        rv = lax.rsqrt(jnp.mean(v * v, axis=-1, keepdims=True) + RMS_EPS)
        vhat = v * rv
        gv = gv_ref[...]
        vn_ref[...] = (vhat * gv).astype(BF16)
        dug_v = dug_ref[...]
        dm = dug_v * u
        dm_ref[...] = dm.astype(BF16)
        mask = _causal_mask()
        for h in range(GROUPS):
            hc = slice(h * c, (h + 1) * c)
            wm = jnp.where(mask, ws_ref[h], 0.0)
            wm_b, wmt_b = wm.astype(BF16), wm.T.astype(BF16)
            bcol = jnp.broadcast_to(bs_ref[h:h + 1, :], (CHUNK, CHUNK)).T[:, 0:1]
            dws_h = jnp.zeros((CHUNK, CHUNK), F32)
            dbs_h = jnp.zeros((CHUNK, c), F32)
            for n in range(rb // CHUNK):
                rows = slice(n * CHUNK, (n + 1) * CHUNK)
                vn_c, dm_c = vn_ref[rows, hc], dm_ref[rows, hc]
                gate_ref[rows, hc] = jnp.dot(wm_b, vn_c, preferred_element_type=F32) + bcol
                dws_h += lax.dot_general(dm_c, vn_c, (((1,), (1,)), ((), ())), preferred_element_type=F32)
                dvn_ref[rows, hc] = jnp.dot(wmt_b, dm_c, preferred_element_type=F32)
                dbs_h += dm[rows, hc]
            dws_ref[h] += dws_h
            dbacc_ref[h] += dbs_h
        du = dug_v * gate_ref[...]
        dvn = dvn_ref[...]
        dvhat = dvn * gv
        m = jnp.mean(dvhat * vhat, axis=-1, keepdims=True)
        dv = rv * (dvhat - vhat * m)
        dgv_ref[0:1, :] += jnp.sum(dvn * vhat, axis=0, keepdims=True)
        dzp_ref[:, :d] = (du * _gelu_grad(zu)).astype(BF16)
        dzp_ref[:, d:] = (dv * _gelu_grad(zv)).astype(BF16)

        @pl.when(i == pl.num_programs(0) - 1)
        def _():
            ones = jnp.ones((8, c), F32)
            for h in range(GROUPS):
                dws_ref[h] = jnp.where(mask, dws_ref[h], 0.0)
                row = lax.dot_general(ones, dbacc_ref[h], (((1,), (1,)), ((), ())),
                                      precision=lax.Precision.HIGHEST, preferred_element_type=F32)
                dbs_ref[h:h + 1, :] = row[0:1]

    return pl.pallas_call(
        body, grid=(t // rb,),
        in_specs=[_row_spec(rb, d2), _row_spec(rb, d), _full_spec((1, d)), _full_spec((GROUPS, CHUNK, CHUNK)),
                  _full_spec((GROUPS, CHUNK))],
        out_specs=[_row_spec(rb, d2), _full_spec((GROUPS, CHUNK, CHUNK)), _full_spec((GROUPS, CHUNK)), _full_spec((8, d))],
        out_shape=[jax.ShapeDtypeStruct((t, d2), BF16), jax.ShapeDtypeStruct((GROUPS, CHUNK, CHUNK), F32),
                   jax.ShapeDtypeStruct((GROUPS, CHUNK), F32), jax.ShapeDtypeStruct((8, d), F32)],
        scratch_shapes=[pltpu.VMEM((rb, d), BF16), pltpu.VMEM((rb, d), F32), pltpu.VMEM((rb, d), BF16),
                        pltpu.VMEM((rb, d), F32), pltpu.VMEM((GROUPS, CHUNK, c), F32)],
        compiler_params=_params(("arbitrary",)), name=name,
    )(zp, dug, vnorm, ws, bs)


def _cast_into_full(w, layer, kind, place, *, name, dep=None, rb=256):
    _, r, c = w.shape
    rb = _tile(r, rb, 16)
    nrb = r // rb
    full = (r, c * N_CHIPS) if kind == "col" else (r * N_CHIPS, c)

    def body(place_ref, w_ref, *rest):
        rest[-1][...] = w_ref[...].astype(BF16)

    def o_index(i, place):
        return (i, place[0]) if kind == "col" else (i + place[0] * nrb, 0)

    in_specs = [pl.BlockSpec((None, rb, c), lambda i, place: (layer, i, 0))] + ([ANY] if dep is not None else [])
    return pl.pallas_call(
        body,
        grid_spec=pltpu.PrefetchScalarGridSpec(num_scalar_prefetch=1, grid=(nrb,), in_specs=in_specs,
                                               out_specs=pl.BlockSpec((rb, c), o_index)),
        out_shape=jax.ShapeDtypeStruct(full, BF16), compiler_params=_params(("parallel",)), name=name,
    )(place, w, *(() if dep is None else (dep,)))


def _adamw_layer(w, g, m, v, layer, prev, *, name, rb=128):
    _, r, c = w.shape
    rb = _tile(r, rb, 8)
    c1 = 1.0 - ADAM_B1 ** ADAM_STEP
    c2 = 1.0 - ADAM_B2 ** ADAM_STEP

    def body(w_ref, g_ref, m_ref, v_ref, *rest):
        go_ref, d_ref, nm_ref, nv_ref = rest[-4:]
        gv = g_ref[...]
        nm = ADAM_B1 * m_ref[...] + (1.0 - ADAM_B1) * gv
        nv = ADAM_B2 * v_ref[...] + (1.0 - ADAM_B2) * (gv * gv)
        go_ref[...] = gv
        nm_ref[...] = nm
        nv_ref[...] = nv
        d_ref[...] = -ADAM_LR * ((nm / c1) / (jnp.sqrt(nv / c2) + ADAM_EPS) + ADAM_WD * w_ref[...])

    lay = pl.BlockSpec((None, rb, c), lambda i: (layer, i, 0))
    return pl.pallas_call(
        body, grid=(r // rb,), in_specs=[lay, _row_spec(rb, c), lay, lay] + ([ANY] * 4 if prev else []), out_specs=[lay] * 4,
        out_shape=[jax.ShapeDtypeStruct(w.shape, F32)] * 4, input_output_aliases={4 + k: k for k in range(4)} if prev else {},
        compiler_params=_params(("parallel",)), name=name,
    )(w, g, m, v, *(prev or ()))


HBM = pl.BlockSpec(memory_space=pltpu.HBM)
SEM = pl.BlockSpec(memory_space=pltpu.SEMAPHORE)
SIDE_EFFECT = pltpu.SideEffectType.DATAFLOW_SIDE_EFFECTING


def _place():
    x, y, c = lax.axis_index("x"), lax.axis_index("y"), lax.axis_index("c")
    chips = [(1 - x, y), (x, 1 - y), (1 - x, 1 - y)]
    return x, y, c, 2 * x + y, chips


def _half(ref, kind, c):
    r, w = ref.shape
    if kind == "col":
        return ref.at[pl.ds(pl.multiple_of(c * (r // 2), 8), r // 2), :]
    return ref.at[:, pl.ds(pl.multiple_of(c * (w // 2), 128), w // 2)]


def _shard(ref, kind, s):
    r, w = ref.shape
    if kind == "col":
        return ref.at[:, pl.ds(pl.multiple_of(s * (w // N_CHIPS), 128), w // N_CHIPS)]
    return ref.at[pl.ds(pl.multiple_of(s * (r // N_CHIPS), 8), r // N_CHIPS), :]


def _remote(src, dst, send_sem, recv_sem, dev):
    return pltpu.make_async_remote_copy(src_ref=src, dst_ref=dst, send_sem=send_sem, recv_sem=recv_sem,
                                        device_id=dev, device_id_type=MESH)


def _start(name, bufs, plan, sem_shape, dep=None):
    n = len(bufs)
    n_in = n + (dep is not None)

    def body(*refs):
        sends, _ = plan(refs[:n], refs[n_in], refs[n_in + 1])
        for cp in sends:
            cp.start()
        refs[n_in + 2 + n][...] = jnp.zeros((8, 128), F32)

    dma = pltpu.SemaphoreType.DMA
    outs = pl.pallas_call(
        body, name=name,
        out_shape=(dma(sem_shape), dma(sem_shape), *[pltpu.HBM(b.shape, b.dtype) for b in bufs], jax.ShapeDtypeStruct((8, 128), F32)),
        in_specs=(HBM,) * n + ((ANY,) if dep is not None else ()),
        out_specs=(SEM, SEM) + (HBM,) * n + (pl.BlockSpec(memory_space=pltpu.VMEM),),
        input_output_aliases={i: i + 2 for i in range(n)},
        compiler_params=pltpu.CompilerParams(has_side_effects=SIDE_EFFECT),
    )(*[pltpu.with_memory_space_constraint(b, pltpu.HBM) for b in bufs], *(() if dep is None else (dep,)))
    return outs[0], outs[1], list(outs[2:2 + n]), outs[2 + n]


def _wait(name, started, plan, after):
    send, recv, bufs, _ = started
    n = len(bufs)

    def body(*refs):
        sends, recvs = plan(refs[:n], refs[n], refs[n + 1])
        for cp in sends:
            cp.wait_send()
        for cp in recvs:
            cp.wait_recv()

    return list(pl.pallas_call(
        body, name=name, out_shape=tuple(pltpu.HBM(b.shape, b.dtype) for b in bufs),
        in_specs=(HBM,) * n + (SEM, SEM, ANY), out_specs=(HBM,) * n, input_output_aliases={i: i for i in range(n)},
        compiler_params=pltpu.CompilerParams(has_side_effects=SIDE_EFFECT),
    )(*bufs, send, recv, after))


KINDS = ("col", "row")


def _gather_ici_plan(n_small):
    def plan(refs, send, recv):
        x, y, c, s, chips = _place()
        n = len(KINDS) + n_small
        sends, recvs = [], []
        for k, (px, py) in enumerate(chips):
            sp = 2 * px + py
            for a, kind in enumerate(KINDS):
                mine, theirs = _half(_shard(refs[a], kind, s), kind, c), _half(_shard(refs[a], kind, sp), kind, c)
                sends.append(_remote(mine, mine, send.at[k * n + a], recv.at[k * n + a], (px, py, c)))
                recvs.append(_remote(theirs, theirs, send.at[k * n + a], recv.at[k * n + a], (px, py, c)))
            for b in range(n_small):
                ref, sem = refs[len(KINDS) + b], k * n + len(KINDS) + b
                sends.append(_remote(ref.at[s], ref.at[s], send.at[sem], recv.at[sem], (px, py, c)))
                recvs.append(_remote(ref.at[sp], ref.at[sp], send.at[sem], recv.at[sem], (px, py, c)))
        return sends, recvs
    return plan


def _gather_d2d_plan(refs, send, recv):
    x, y, c, _, chips = _place()
    n = len(KINDS)
    sends, recvs = [], []
    for k, (px, py) in enumerate(chips):
        for a, kind in enumerate(KINDS):
            region, sem = _shard(refs[a], kind, 2 * px + py), k * n + a
            sends.append(_remote(_half(region, kind, c), _half(region, kind, c), send.at[sem], recv.at[sem], (x, y, 1 - c)))
            recvs.append(_remote(_half(region, kind, 1 - c), _half(region, kind, 1 - c), send.at[sem], recv.at[sem], (x, y, 1 - c)))
    return sends, recvs


def _swap_plan(refs, send, recv):
    x, y, c, _, _ = _place()
    n = len(KINDS)
    cps = [_remote(_half(refs[a], KINDS[a], 1 - c), refs[n + a], send.at[a], recv.at[a], (x, y, 1 - c)) for a in range(n)]
    return cps, cps


def _exchange_plan(refs, send, recv):
    x, y, c, _, chips = _place()
    n = len(KINDS)
    cps = []
    for k, (px, py) in enumerate(chips):
        for a in range(n):
            cps.append(_remote(_shard(refs[a], KINDS[a], 2 * px + py), refs[n + a].at[k], send.at[k * n + a], recv.at[k * n + a],
                               (px, py, c)))
    return cps, cps


def _share_plan(refs, send, recv):
    x, y, c, _, _ = _place()
    sends = [_remote(_half(refs[a], KINDS[a], c), _half(refs[a], KINDS[a], c), send.at[a], recv.at[a], (x, y, 1 - c))
             for a in range(len(KINDS))]
    recvs = [_remote(_half(refs[a], KINDS[a], 1 - c), _half(refs[a], KINDS[a], 1 - c), send.at[a], recv.at[a], (x, y, 1 - c))
             for a in range(len(KINDS))]
    return sends, recvs


def _spread_plan(refs, send, recv):
    packed, slots = refs
    x, y, c, _, _ = _place()
    sends, recvs = [], []
    for k in range(1, 8):
        px, py, pc = x ^ (k >> 2), y ^ ((k >> 1) & 1), c ^ (k & 1)
        sends.append(_remote(packed, slots.at[4 * x + 2 * y + c], send.at[k - 1], recv.at[k - 1], (px, py, pc)))
        recvs.append(_remote(packed, slots.at[4 * px + 2 * py + pc], send.at[k - 1], recv.at[k - 1], (px, py, pc)))
    return sends, recvs


def _half_index(kind, nblk):
    def index(i, j, place):
        return (i + place[1] * nblk[0], j) if kind == "col" else (i, j + place[1] * nblk[1])
    return index


def _chip_partial(g, other, kind, place, *, name):
    hr, hc = other.shape
    rb, cb = _tile(hr, 512, 16), _tile(hc, 1024)
    nblk = (hr // rb, hc // cb)

    def body(place_ref, g_ref, o_ref, p_ref):
        p_ref[...] = (g_ref[...].astype(F32) + o_ref[...].astype(F32)).astype(BF16)

    plain = pl.BlockSpec((rb, cb), lambda i, j, place: (i, j))
    return pl.pallas_call(
        body,
        grid_spec=pltpu.PrefetchScalarGridSpec(
            num_scalar_prefetch=1, grid=nblk, in_specs=[pl.BlockSpec((rb, cb), _half_index(kind, nblk)), plain], out_specs=plain),
        out_shape=jax.ShapeDtypeStruct((hr, hc), BF16), compiler_params=_params(("parallel", "parallel")), name=name,
    )(place, g, other)


def _reduce_half(g, other, recv, kind, place, *, name):
    _, pr, pc = recv.shape
    rb, cb = _tile(pr, 512, 16), _tile(pc, 1024)
    nblk = (pr // rb, pc // cb)
    full = (pr * 2, pc) if kind == "col" else (pr, pc * 2)

    def g_index(i, j, place):
        s, c = place[0], place[1]
        return (i + c * nblk[0], j + s * nblk[1]) if kind == "col" else (i + s * nblk[0], j + c * nblk[1])

    def o_index(i, j, place):
        return (i, j + place[0] * nblk[1]) if kind == "col" else (i + place[0] * nblk[0], j)

    def body(place_ref, g_ref, o_ref, r_ref, out_ref):
        acc = g_ref[...].astype(F32) + o_ref[...].astype(F32)
        for k in range(3):
            acc = acc + r_ref[k].astype(F32)
        out_ref[...] = acc

    return pl.pallas_call(
        body,
        grid_spec=pltpu.PrefetchScalarGridSpec(
            num_scalar_prefetch=1, grid=nblk,
            in_specs=[pl.BlockSpec((rb, cb), g_index), pl.BlockSpec((rb, cb), o_index),
                      pl.BlockSpec((3, rb, cb), lambda i, j, place: (0, i, j))],
            out_specs=pl.BlockSpec((rb, cb), _half_index(kind, nblk))),
        out_shape=jax.ShapeDtypeStruct(full, F32), compiler_params=_params(("parallel", "parallel")), name=name,
    )(place, g, other, recv)


def _sum_slots(packed, slots, me, *, name, rb=512):
    r, w = packed.shape
    rb = _tile(r, rb, 8)

    def body(me_ref, p_ref, s_ref, o_ref):
        acc = None
        for j in range(8):
            term = jnp.where(me_ref[0] == j, p_ref[...], s_ref[j])
            acc = term if acc is None else acc + term
        o_ref[...] = acc

    return pl.pallas_call(
        body,
        grid_spec=pltpu.PrefetchScalarGridSpec(
            num_scalar_prefetch=1, grid=(r // rb,),
            in_specs=[pl.BlockSpec((rb, w), lambda i, me: (i, 0)), pl.BlockSpec((8, rb, w), lambda i, me: (0, i, 0))],
            out_specs=pl.BlockSpec((rb, w), lambda i, me: (i, 0))),
        out_shape=jax.ShapeDtypeStruct((r, w), F32), compiler_params=_params(("parallel",)), name=name,
    )(me, packed, slots)


def _half_shape(a, kind):
    return (a.shape[0] // 2, a.shape[1]) if kind == "col" else (a.shape[0], a.shape[1] // 2)


def _rs_swap(tag, grads):
    others = [lax.empty(_half_shape(g, k), g.dtype) for g, k in zip(grads, KINDS)]
    return _start(f"rs_{tag}_swap", list(grads) + others, _swap_plan, (len(KINDS),))


def _rs_exchange(tag, swapped, place, after):
    bufs = _wait(f"rs_{tag}_swap_wait", swapped, _swap_plan, after)
    n = len(KINDS)
    grads, others = bufs[:n], bufs[n:]
    parts = [_chip_partial(g, o, k, place, name=f"rs_{tag}_partial_{k}") for g, o, k in zip(grads, others, KINDS)]
    lands = []
    for p, k in zip(parts, KINDS):
        piece = (p.shape[0], p.shape[1] // N_CHIPS) if k == "col" else (p.shape[0] // N_CHIPS, p.shape[1])
        lands.append(lax.empty((3,) + piece, p.dtype))
    return _start(f"rs_{tag}_exchange", parts + lands, _exchange_plan, (3 * n,)), grads, others


def _rs_share(tag, exchanged, place, after):
    started, grads, others = exchanged
    n = len(KINDS)
    recvs = _wait(f"rs_{tag}_exchange_wait", started, _exchange_plan, after)[n:]
    halves = [_reduce_half(g, o, r, k, place, name=f"rs_{tag}_reduce_{k}") for g, o, r, k in zip(grads, others, recvs, KINDS)]
    return _start(f"rs_{tag}_share", halves, _share_plan, (n,))


def _rs_finish(tag, shared, after):
    return _wait(f"rs_{tag}_share_wait", shared, _share_plan, after)


def _spread(tag, parts, dep):
    rows = [p.reshape(-1, 128) for p in parts]
    n = sum(r.shape[0] for r in rows)
    rows.append(jnp.zeros(((-n) % 512, 128), F32))
    packed = jnp.concatenate(rows, axis=0)
    return _start(f"small_{tag}_spread", [packed, lax.empty((8,) + packed.shape, F32)], _spread_plan, (7,), dep=dep)


def _spread_sum(tag, started, parts, me, after):
    packed, slots = _wait(f"small_{tag}_spread_wait", started, _spread_plan, after)
    total = _sum_slots(packed, slots, me, name=f"small_{tag}_sum")
    out, row = [], 0
    for p in parts:
        n = p.size // 128
        out.append(total[row:row + n].reshape(p.shape))
        row += n
    return out


def _ffn_fwd(x, h, w_up, conv_w, conv_b, w_down, tag):
    up = _mm(h, w_up, name=f"ffn{tag}_up")
    act, conv = _ffn_mid_fwd(up, conv_w, conv_b, name=f"ffn{tag}_mid")
    x_out = _mm(act, w_down, res=x, tk=2816, name=f"ffn{tag}_down")
    return x_out, (up, conv, act)


def kernel(x, a_norm, a_in, a_conv, a_out, b_norm, b_in, b_vnorm, b_ws, b_bs, b_out, f_norm, f_up, f_conv_w, f_conv_b, f_down, final_norm, loss_target, m_a_norm, m_a_in, m_a_conv, m_a_out, m_b_norm, m_b_in, m_b_vnorm, m_b_ws, m_b_bs, m_b_out, m_f_norm, m_f_up, m_f_conv_w, m_f_conv_b, m_f_down, m_final_norm, v_a_norm, v_a_in, v_a_conv, v_a_out, v_b_norm, v_b_in, v_b_vnorm, v_b_ws, v_b_bs, v_b_out, v_f_norm, v_f_up, v_f_conv_w, v_f_conv_b, v_f_down, v_final_norm):
    t, d = x.shape[1], x.shape[2]
    f2 = f_up.shape[2] * N_CHIPS
    x0, tgt = x.reshape(t, d), loss_target.reshape(t, d)
    ax, ay, ac = lax.axis_index("x"), lax.axis_index("y"), lax.axis_index("c")
    s = 2 * ax + ay
    place = jnp.stack([s, ac]).astype(jnp.int32)
    me = (4 * ax + 2 * ay + ac).astype(jnp.int32).reshape(1)

    def stacked(a):
        return lax.dynamic_update_index_in_dim(jnp.zeros((N_CHIPS,) + a.shape, F32), a, s, 0)

    def gather_start(tag, w_in, w_out, layer, small, dep):
        fulls = [_cast_into_full(w_in, layer, "col", place, name=f"cast_{tag}_in", dep=dep),
                 _cast_into_full(w_out, layer, "row", place, name=f"cast_{tag}_out", dep=dep)]
        return _start(f"ag_{tag}_ici", fulls + small, _gather_ici_plan(len(small)), (3 * (2 + len(small)),), dep=dep)

    def gather_forward(tag, started, n_small, after):
        bufs = _wait(f"ag_{tag}_ici_wait", started, _gather_ici_plan(n_small), after)
        return _start(f"ag_{tag}_d2d", bufs[:2], _gather_d2d_plan, (3 * 2,)), bufs[2:]

    def gather_finish(tag, forwarded, after):
        return _wait(f"ag_{tag}_d2d_wait", forwarded, _gather_d2d_plan, after)

    small = [stacked(a_conv[0]), stacked(b_norm), stacked(b_vnorm), stacked(f_conv_w.reshape(2 * 3, -1))]
    ag_a = gather_start("a", a_in, a_out, 0, small, None)
    ag_f0 = gather_start("f0", f_up, f_down, 0, [], ag_a[3])
    ag_b = gather_start("b", b_in, b_out, 0, [], ag_f0[3])
    ag_f1 = gather_start("f1", f_up, f_down, 1, [], ag_b[3])

    def unshard(a):
        return jnp.transpose(a, (1, 0, 2)).reshape(a.shape[1], -1)

    ws, bs = b_ws[0], b_bs[0]

    h0 = _rms_fwd(x0, a_norm, dep=ag_f1[3], name="a_norm")
    fw_a, (g_aconv, g_bnorm, g_bvnorm, g_fconv) = gather_forward("a", ag_a, 4, h0)
    w_ai, w_ao = gather_finish("a", fw_a, fw_a[3])
    a_conv_f, b_norm_f, b_vnorm_f = unshard(g_aconv), unshard(g_bnorm), unshard(g_bvnorm)
    f_conv_f = unshard(g_fconv).reshape(2, 3, f2)
    bcx = _mm(h0, w_ai, name="a_in")
    y = _a_mid_fwd(bcx, a_conv_f, name="a_mid")
    x1 = _mm(y, w_ao, res=x0, tm=512, tn=2048, name="a_out")
    fw_f0, _ = gather_forward("f0", ag_f0, 0, x1)
    h1 = _rms_fwd(x1, f_norm[0:1], dep=fw_f0[3], name="ffn0_norm")
    w_up0, w_dn0 = gather_finish("f0", fw_f0, h1)
    x2, (up0, conv0, act0) = _ffn_fwd(x1, h1, w_up0, f_conv_f[0], f_conv_b[0:1], w_dn0, 0)
    fw_b, _ = gather_forward("b", ag_b, 0, up0)
    w_bi, w_bo = gather_finish("b", fw_b, act0)
    h2 = _rms_fwd(x2, b_norm_f, name="b_norm")
    zp = _mm(h2, w_bi, name="b_in")
    fw_f1, _ = gather_forward("f1", ag_f1, 0, zp)
    ug = _b_mid_fwd(zp, b_vnorm_f, ws, bs, name="b_mid")
    x3 = _mm(ug, w_bo, res=x2, tm=512, tn=2048, name="b_out")
    w_up1, w_dn1 = gather_finish("f1", fw_f1, x3)
    h3 = _rms_fwd(x3, f_norm[1:2], name="ffn1_norm")
    x4, (up1, conv1, act1) = _ffn_fwd(x3, h3, w_up1, f_conv_f[1], f_conv_b[1:2], w_dn1, 1)
    loss_rows, dx4, dx4b, d_final = _final(x4, tgt, final_norm.reshape(1, d), name="final")

    d_dn1 = _mm(act1, dx4b, ta=True, tm=1408, out_dtype=BF16, name="ffn1_ddown")
    dact1 = _mm(dx4b, w_dn1, tb=True, tn=512, tm=2048, name="ffn1_dact")
    dup1, d_fwb1 = _ffn_mid_bwd(up1, conv1, dact1, f_conv_f[1], name="ffn1_mid_bwd")
    d_up1 = _mm(h3, dup1, ta=True, out_dtype=BF16, name="ffn1_dup")
    sw_f1 = _rs_swap("f1", [d_up1, d_dn1])
    dh3 = _mm(dup1, w_up1, tb=True, tk=2816, dep=sw_f1[3], name="ffn1_dh")
    dx3, dx3b, d_fnorm1 = _rms_bwd(dh3, x3, f_norm[1:2], dx4, name="ffn1_norm_bwd")
    ex_f1 = _rs_exchange("f1", sw_f1, place, dx3)

    d_bo = _mm(ug, dx3b, ta=True, out_dtype=BF16, dep=ex_f1[0][3], name="b_dout")
    dug = _mm(dx3b, w_bo, tb=True, tm=512, tn=2048, name="b_dug")
    dzp, d_ws, d_bs, d_bvnorm = _b_mid_bwd(zp, dug, b_vnorm_f, ws, bs, name="b_mid_bwd")
    d_bi = _mm(h2, dzp, ta=True, out_dtype=BF16, name="b_din")
    sw_b = _rs_swap("b", [d_bi, d_bo])
    dh2 = _mm(dzp, w_bi, tb=True, dep=sw_b[3], name="b_dh")
    dx2, dx2b, d_bnorm = _rms_bwd(dh2, x2, b_norm_f, dx3, name="b_norm_bwd")
    ex_b = _rs_exchange("b", sw_b, place, dx2)

    d_dn0 = _mm(act0, dx2b, ta=True, tm=1408, out_dtype=BF16, dep=ex_b[0][3], name="ffn0_ddown")
    dact0 = _mm(dx2b, w_dn0, tb=True, tn=512, tm=2048, name="ffn0_dact")
    dup0, d_fwb0 = _ffn_mid_bwd(up0, conv0, dact0, f_conv_f[0], name="ffn0_mid_bwd")
    sh_f1 = _rs_share("f1", ex_f1, place, dup0)
    d_up0 = _mm(h1, dup0, ta=True, out_dtype=BF16, dep=sh_f1[3], name="ffn0_dup")
    sw_f0 = _rs_swap("f0", [d_up0, d_dn0])
    g_up1, g_dn1 = _rs_finish("f1", sh_f1, sw_f0[3])
    dh1 = _mm(dup0, w_up0, tb=True, tk=2816, dep=sw_f0[3], name="ffn0_dh")
    dx1, dx1b, d_fnorm0 = _rms_bwd(dh1, x1, f_norm[0:1], dx2, name="ffn0_norm_bwd")
    ex_f0 = _rs_exchange("f0", sw_f0, place, dx1)
    early = [jnp.concatenate([d_bnorm, d_bvnorm, d_fnorm0, d_fnorm1, d_final, loss_rows], axis=0),
             jnp.concatenate([d_fwb0, d_fwb1], axis=0), jnp.concatenate([d_ws.reshape(-1, CHUNK), d_bs], axis=0)]
    sp_early = _spread("early", early, ex_f0[0][3])
    sh_b = _rs_share("b", ex_b, place, sp_early[3])

    d_ao = _mm(y, dx1b, ta=True, out_dtype=BF16, dep=sh_b[3], name="a_dout")
    dyy = _mm(dx1b, w_ao, tb=True, tm=512, tn=2048, name="a_dy")
    dbcx, d_aconv = _a_mid_bwd(bcx, dyy, a_conv_f, name="a_mid_bwd")
    d_ai = _mm(h0, dbcx, ta=True, out_dtype=BF16, name="a_din")
    sw_a = _rs_swap("a", [d_ai, d_ao])
    g_bi, g_bo = _rs_finish("b", sh_b, sw_a[3])
    ex_a = _rs_exchange("a", sw_a, place, g_bi)
    dh0 = _mm(dbcx, w_ai, tb=True, dep=ex_a[0][3], name="a_dh")
    grad_x, _, d_anorm = _rms_bwd(dh0, x0, a_norm, dx1, name="a_norm_bwd")
    late = [jnp.concatenate([d_anorm, d_aconv], axis=0)]
    sp_late = _spread("late", late, ex_a[0][3])
    sh_f0 = _rs_share("f0", ex_f0, place, sp_late[3])
    sh_a = _rs_share("a", ex_a, place, sh_f0[3])
    g_up0, g_dn0 = _rs_finish("f0", sh_f0, sh_a[3])
    g_ai, g_ao = _rs_finish("a", sh_a, g_up0)
    r_a, r_b, r_c = _spread_sum("early", sp_early, early, me, g_ai)
    (r_l,) = _spread_sum("late", sp_late, late, me, r_a)

    loss = jnp.sum(r_a[40])
    cs, fs = d // N_CHIPS, f2 // N_CHIPS

    def mine(a, width):
        return lax.dynamic_slice_in_dim(a, s * width, width, axis=1)

    grads = {
        "a_norm": r_l[0:1], "a_conv": mine(r_l[8:11], cs), "b_norm": mine(r_a[0:1], cs), "b_vnorm": mine(r_a[8:9], cs),
        "f_norm": jnp.concatenate([r_a[16:17], r_a[24:25]], axis=0), "final_norm": r_a[32:33],
        "b_ws": r_c[:GROUPS * CHUNK], "b_bs": r_c[GROUPS * CHUNK:],
        "f_conv_w": jnp.concatenate([mine(r_b[0:3], fs), mine(r_b[8:11], fs)], axis=0),
        "f_conv_b": jnp.concatenate([r_b[3:4], r_b[11:12]], axis=0),
        "a_in": g_ai, "a_out": g_ao, "b_in": g_bi, "b_out": g_bo,
    }
    names = ["a_norm", "a_in", "a_conv", "a_out", "b_norm", "b_in", "b_vnorm", "b_ws", "b_bs", "b_out", "f_norm", "f_up",
             "f_conv_w", "f_conv_b", "f_down", "final_norm"]
    weights = dict(zip(names, [a_norm, a_in, a_conv, a_out, b_norm, b_in, b_vnorm, b_ws, b_bs, b_out, f_norm, f_up, f_conv_w,
                               f_conv_b, f_down, final_norm]))
    ms = dict(zip(names, [m_a_norm, m_a_in, m_a_conv, m_a_out, m_b_norm, m_b_in, m_b_vnorm, m_b_ws, m_b_bs, m_b_out, m_f_norm,
                          m_f_up, m_f_conv_w, m_f_conv_b, m_f_down, m_final_norm]))
    vs = dict(zip(names, [v_a_norm, v_a_in, v_a_conv, v_a_out, v_b_norm, v_b_in, v_b_vnorm, v_b_ws, v_b_bs, v_b_out, v_f_norm,
                          v_f_up, v_f_conv_w, v_f_conv_b, v_f_down, v_final_norm]))
    result = {}
    for n in names:
        w = weights[n]
        if n in ("f_up", "f_down"):
            g1, g0 = (g_up1, g_up0) if n == "f_up" else (g_dn1, g_dn0)
            first = _adamw_layer(w, g1, ms[n], vs[n], 1, None, name=f"adamw_{n}1")
            result[n] = _adamw_layer(w, g0, ms[n], vs[n], 0, tuple(first), name=f"adamw_{n}0")
            continue
        g2 = grads[n]
        as3d = (lambda a: a.reshape((1,) + g2.shape))
        result[n] = [o.reshape(w.shape) for o in _adamw_layer(as3d(w), g2, as3d(ms[n]), as3d(vs[n]), 0, None, name=f"adamw_{n}")]

    return (loss, grad_x.reshape(x.shape), *[result[n][0] for n in names], *[result[n][1] for n in names],
            *[result[n][2] for n in names], *[result[n][3] for n in names])
```

```python
import functools

import jax
import jax.numpy as jnp
from jax import lax
from jax.experimental import pallas as pl
from jax.experimental.pallas import tpu as pltpu

F32 = jnp.float32
BF16 = jnp.bfloat16
MESH = pl.DeviceIdType.MESH
ANY = pl.BlockSpec(memory_space=pl.ANY)

RMS_EPS = 1e-5
CHUNK = 128
GROUPS = 8
ADAM_LR, ADAM_B1, ADAM_B2, ADAM_EPS, ADAM_WD, ADAM_STEP = 0.001, 0.9, 0.999, 1e-08, 0.01, 10

N_CHIPS = 4
HALO = 8
VMEM_LIMIT = 56 * 1024 * 1024
GELU_C = 0.7978845608028654
GELU_A = 0.044715


def _params(sem=None):
    return pltpu.CompilerParams(dimension_semantics=sem, vmem_limit_bytes=VMEM_LIMIT)


def _tile(dim, pref, quantum=128):
    if dim <= pref:
        return dim
    t = (pref // quantum) * quantum
    while t >= quantum:
        if dim % t == 0:
            return t
        t -= quantum
    return dim


def _mm(a, b, *, name, ta=False, tb=False, res=None, norm=None, dep=None, out_dtype=F32, tm=1024, tn=1024, tk=2048):
    (K, M) = a.shape if ta else a.shape[::-1]
    N = b.shape[0] if tb else b.shape[1]
    assert (b.shape[1] if tb else b.shape[0]) == K
    tm, tn, tk = _tile(M, tm), _tile(N, tn), _tile(K, tk)
    nk = K // tk
    assert norm is None or (tn == N and nk == 1)
    a_spec = pl.BlockSpec((tk, tm), lambda i, j, k: (k, i)) if ta else pl.BlockSpec((tm, tk), lambda i, j, k: (i, k))
    b_spec = pl.BlockSpec((tn, tk), lambda i, j, k: (j, k)) if tb else pl.BlockSpec((tk, tn), lambda i, j, k: (k, j))
    o_spec = pl.BlockSpec((tm, tn), lambda i, j, k: (i, j))
    dims = (((0 if ta else 1,), (1 if tb else 0,)), ((), ()))
    direct = out_dtype == F32
    n_in = 2 + (res is not None) + (norm is not None) + (dep is not None)

    def body(*refs):
        a_ref, b_ref = refs[0], refs[1]
        r_ref = refs[2] if res is not None else None
        g_ref = refs[2 + (res is not None)] if norm is not None else None
        o_ref = refs[n_in]
        acc_ref = o_ref if direct else refs[-1]
        part = lax.dot_general(a_ref[...], b_ref[...], dims, preferred_element_type=F32)
        if nk == 1:
            if r_ref is not None:
                part = part + r_ref[...]
            o_ref[...] = part.astype(o_ref.dtype)
            if g_ref is not None:
                r = lax.rsqrt(jnp.mean(part * part, axis=-1, keepdims=True) + RMS_EPS)
                refs[n_in + 1][...] = ((part * r) * g_ref[...]).astype(BF16)
            return
        k = pl.program_id(2)

        @pl.when(k == 0)
        def _():
            acc_ref[...] = part

        @pl.when(jnp.logical_and(k > 0, k < nk - 1))
        def _():
            acc_ref[...] += part

        @pl.when(k == nk - 1)
        def _():
            tot = acc_ref[...] + part
            if r_ref is not None:
                tot = tot + r_ref[...]
            o_ref[...] = tot.astype(o_ref.dtype)

    in_specs = ([a_spec, b_spec] + ([o_spec] if res is not None else [])
                + ([pl.BlockSpec((1, tn), lambda i, j, k: (0, j))] if norm is not None else []) + ([ANY] if dep is not None else []))
    args = (a, b) + tuple(x for x in (res, norm, dep) if x is not None)
    scratch = [] if (direct or nk == 1) else [pltpu.VMEM((tm, tn), F32)]
    out_shape = jax.ShapeDtypeStruct((M, N), out_dtype)
    return pl.pallas_call(
        body, grid=(M // tm, N // tn, nk), in_specs=in_specs, out_specs=[o_spec, o_spec] if norm is not None else o_spec,
        out_shape=[out_shape, jax.ShapeDtypeStruct((M, N), BF16)] if norm is not None else out_shape, scratch_shapes=scratch,
        compiler_params=_params(("parallel", "parallel", "arbitrary")), name=name,
    )(*args)


def _row_spec(rb, w):
    return pl.BlockSpec((rb, w), lambda i: (i, 0))


def _prev_spec(rb, w):
    return pl.BlockSpec((HALO, w), lambda i: (jnp.maximum(i * (rb // HALO) - 1, 0), 0))


def _next_spec(rb, w, t):
    return pl.BlockSpec((HALO, w), lambda i: (jnp.minimum((i + 1) * (rb // HALO), t // HALO - 1), 0))


def _full_spec(shape):
    return pl.BlockSpec(shape, lambda i: tuple(0 for _ in shape))


def _shift(e, s):
    return pltpu.roll(e, s % e.shape[0], 0)


def _gelu(x):
    return 0.5 * x * (1.0 + jnp.tanh(GELU_C * (x + GELU_A * x * x * x)))


def _gelu_grad(x):
    th = jnp.tanh(GELU_C * (x + GELU_A * x * x * x))
    return 0.5 * (1.0 + th) + 0.5 * x * (1.0 - th * th) * (GELU_C * (1.0 + 3.0 * GELU_A * x * x))


def _sigmoid(x):
    return 1.0 / (1.0 + jnp.exp(-x))


def _rms_fwd(x, g, *, name, dep=None, rb=256):
    t, d = x.shape
    rb = _tile(t, rb, 8)

    def body(x_ref, g_ref, *rest):
        h_ref = rest[-1]
        xv = x_ref[...]
        r = lax.rsqrt(jnp.mean(xv * xv, axis=-1, keepdims=True) + RMS_EPS)
        h_ref[...] = ((xv * r) * g_ref[...]).astype(BF16)

    return pl.pallas_call(
        body, grid=(t // rb,), in_specs=[_row_spec(rb, d), _full_spec((1, d))] + ([ANY] if dep is not None else []),
        out_specs=_row_spec(rb, d), out_shape=jax.ShapeDtypeStruct((t, d), BF16), compiler_params=_params(("parallel",)), name=name,
    )(x, g, *(() if dep is None else (dep,)))


def _rms_bwd(dh, x, g, dres, *, name, dep=None, rb=256):
    t, d = x.shape
    rb = _tile(t, rb, 8)

    def body(dh_ref, x_ref, g_ref, dres_ref, *rest):
        dx_ref, dxb_ref, dg_ref = rest[-3:]
        xv = x_ref[...]
        r = lax.rsqrt(jnp.mean(xv * xv, axis=-1, keepdims=True) + RMS_EPS)
        xhat = xv * r
        dh_v = dh_ref[...]
        dxhat = dh_v * g_ref[...]
        m = jnp.mean(dxhat * xhat, axis=-1, keepdims=True)
        dx = dres_ref[...] + r * (dxhat - xhat * m)
        dx_ref[...] = dx
        dxb_ref[...] = dx.astype(BF16)

        @pl.when(pl.program_id(0) == 0)
        def _():
            dg_ref[...] = jnp.zeros_like(dg_ref)

        dg_ref[0:1, :] += jnp.sum(dh_v * xhat, axis=0, keepdims=True)

    return pl.pallas_call(
        body, grid=(t // rb,),
        in_specs=[_row_spec(rb, d), _row_spec(rb, d), _full_spec((1, d)), _row_spec(rb, d)] + ([ANY] if dep is not None else []),
        out_specs=[_row_spec(rb, d), _row_spec(rb, d), _full_spec((8, d))],
        out_shape=[jax.ShapeDtypeStruct((t, d), F32), jax.ShapeDtypeStruct((t, d), BF16), jax.ShapeDtypeStruct((8, d), F32)],
        compiler_params=_params(("arbitrary",)), name=name,
    )(dh, x, g, dres, *(() if dep is None else (dep,)))


def _final(x, tgt, g, *, name, rb=256):
    t, d = x.shape
    rb = _tile(t, rb, 8)
    inv_d = 1.0 / d

    def body(x_ref, t_ref, g_ref, l_ref, dx_ref, dxb_ref, dg_ref):
        xv = x_ref[...]
        gv = g_ref[...]
        r = lax.rsqrt(jnp.mean(xv * xv, axis=-1, keepdims=True) + RMS_EPS)
        xhat = xv * r
        e = xhat * gv - t_ref[...]
        dy = e * inv_d
        dxhat = dy * gv
        m = jnp.mean(dxhat * xhat, axis=-1, keepdims=True)
        dx = r * (dxhat - xhat * m)
        dx_ref[...] = dx
        dxb_ref[...] = dx.astype(BF16)

        @pl.when(pl.program_id(0) == 0)
        def _():
            l_ref[...] = jnp.zeros_like(l_ref)
            dg_ref[...] = jnp.zeros_like(dg_ref)

        l_ref[0:1, :] += jnp.sum(e * e, axis=0, keepdims=True) * (0.5 * inv_d)
        dg_ref[0:1, :] += jnp.sum(dy * xhat, axis=0, keepdims=True)

    return pl.pallas_call(
        body, grid=(t // rb,),
        in_specs=[_row_spec(rb, d), _row_spec(rb, d), _full_spec((1, d))],
        out_specs=[_full_spec((8, d)), _row_spec(rb, d), _row_spec(rb, d), _full_spec((8, d))],
        out_shape=[jax.ShapeDtypeStruct((8, d), F32), jax.ShapeDtypeStruct((t, d), F32),
                   jax.ShapeDtypeStruct((t, d), BF16), jax.ShapeDtypeStruct((8, d), F32)],
        compiler_params=_params(("arbitrary",)), name=name,
    )(x, tgt, g)


def _a_mid_fwd(bcx, wconv, *, name, rb=256, cw=512):
    t, d3 = bcx.shape
    d = d3 // 3
    rb, cw = _tile(t, rb, 8), _tile(d, cw)

    def body(cur_ref, prev_ref, w_ref, y_ref):
        first = pl.program_id(0) == 0
        for c0 in range(0, d, cw):
            cs = slice(c0, c0 + cw)
            gc, xs = slice(d + c0, d + c0 + cw), slice(2 * d + c0, 2 * d + c0 + cw)
            p_prev = jnp.where(first, 0.0, prev_ref[:, gc] * prev_ref[:, xs])
            e = jnp.concatenate([p_prev, cur_ref[:, gc] * cur_ref[:, xs]], axis=0)
            w = w_ref[:, cs]
            q = w[0:1] * _shift(e, 2) + w[1:2] * _shift(e, 1) + w[2:3] * e
            y_ref[:, cs] = (cur_ref[:, cs] * q[HALO:]).astype(BF16)

    return pl.pallas_call(
        body, grid=(t // rb,),
        in_specs=[_row_spec(rb, d3), _prev_spec(rb, d3), _full_spec((3, d))], out_specs=_row_spec(rb, d),
        out_shape=jax.ShapeDtypeStruct((t, d), BF16), compiler_params=_params(("parallel",)), name=name,
    )(bcx, bcx, wconv)


def _a_mid_bwd(bcx, dy, wconv, *, name, rb=128, cw=512):
    t, d3 = bcx.shape
    d = d3 // 3
    rb, cw = _tile(t, rb, 8), _tile(d, cw)

    def body(cur_ref, prev_ref, next_ref, dy_ref, dyn_ref, w_ref, o_ref, dw_ref):
        i = pl.program_id(0)
        first, last = i == 0, i == pl.num_programs(0) - 1

        @pl.when(first)
        def _():
            dw_ref[...] = jnp.zeros_like(dw_ref)

        for c0 in range(0, d, cw):
            cs = slice(c0, c0 + cw)
            gc, xs = slice(d + c0, d + c0 + cw), slice(2 * d + c0, 2 * d + c0 + cw)
            zeros = jnp.zeros((HALO, cw), F32)
            gb_c, gc_c, xs_c = cur_ref[:, cs], cur_ref[:, gc], cur_ref[:, xs]
            p_prev = jnp.where(first, 0.0, prev_ref[:, gc] * prev_ref[:, xs])
            e = jnp.concatenate([p_prev, gc_c * xs_c, zeros], axis=0)
            dq_next = jnp.where(last, 0.0, dyn_ref[:, cs] * next_ref[:, cs])
            dy_c = dy_ref[:, cs]
            dq = jnp.concatenate([zeros, dy_c * gb_c, dq_next], axis=0)
            w = w_ref[:, cs]
            e1, e2 = _shift(e, 1), _shift(e, 2)
            q = w[0:1] * e2 + w[1:2] * e1 + w[2:3] * e
            dp = (w[2:3] * dq + w[1:2] * _shift(dq, -1) + w[0:1] * _shift(dq, -2))[HALO:HALO + rb]
            o_ref[:, cs] = (dy_c * q[HALO:HALO + rb]).astype(BF16)
            o_ref[:, gc] = (dp * xs_c).astype(BF16)
            o_ref[:, xs] = (dp * gc_c).astype(BF16)
            dq_c = dq[HALO:HALO + rb]
            dw_ref[0:1, cs] += jnp.sum(dq_c * e2[HALO:HALO + rb], axis=0, keepdims=True)
            dw_ref[1:2, cs] += jnp.sum(dq_c * e1[HALO:HALO + rb], axis=0, keepdims=True)
            dw_ref[2:3, cs] += jnp.sum(dq_c * e[HALO:HALO + rb], axis=0, keepdims=True)

    return pl.pallas_call(
        body, grid=(t // rb,),
        in_specs=[_row_spec(rb, d3), _prev_spec(rb, d3), _next_spec(rb, d3, t), _row_spec(rb, d), _next_spec(rb, d, t),
                  _full_spec((3, d))],
        out_specs=[_row_spec(rb, d3), _full_spec((8, d))],
        out_shape=[jax.ShapeDtypeStruct((t, d3), BF16), jax.ShapeDtypeStruct((8, d), F32)],
        compiler_params=_params(("arbitrary",)), name=name,
    )(bcx, bcx, bcx, dy, dy, wconv)


def _ffn_mid_fwd(up, wconv, bconv, *, name, rb=128, cw=512):
    t, f2 = up.shape
    f = f2 // 2
    rb, cw = _tile(t, rb, 16), _tile(f, cw)

    def body(cur_ref, prev_ref, w_ref, b_ref, act_ref, conv_ref):
        first = pl.program_id(0) == 0

        def conv(cols):
            e = jnp.concatenate([jnp.where(first, 0.0, prev_ref[:, cols]), cur_ref[:, cols]], axis=0)
            w = w_ref[:, cols]
            out = (w[0:1] * _shift(e, 2) + w[1:2] * _shift(e, 1) + w[2:3] * e + b_ref[:, cols])[HALO:]
            conv_ref[:, cols] = out.astype(BF16)
            return out

        for c0 in range(0, f, cw):
            g = conv(slice(c0, c0 + cw))
            a = conv(slice(f + c0, f + c0 + cw))
            act_ref[:, c0:c0 + cw] = (g * _sigmoid(g) * a).astype(BF16)

    return pl.pallas_call(
        body, grid=(t // rb,),
        in_specs=[_row_spec(rb, f2), _prev_spec(rb, f2), _full_spec((3, f2)), _full_spec((1, f2))],
        out_specs=[_row_spec(rb, f), _row_spec(rb, f2)],
        out_shape=[jax.ShapeDtypeStruct((t, f), BF16), jax.ShapeDtypeStruct((t, f2), BF16)],
        compiler_params=_params(("parallel",)), name=name,
    )(up, up, wconv, bconv)


BF16_HALO = 16


def _ffn_mid_bwd(up, conv, dact, wconv, *, name, rb=128, cw=512):
    t, f2 = up.shape
    f = f2 // 2
    rb, cw = _tile(t, rb, 16), _tile(f, cw)

    def body(up_ref, conv_ref, convn_ref, da_ref, dan_ref, w_ref, o_ref, dwb_ref):
        i = pl.program_id(0)
        last = i == pl.num_programs(0) - 1

        @pl.when(i == 0)
        def _():
            dwb_ref[...] = jnp.zeros_like(dwb_ref)

        def rows(cols):
            return jnp.concatenate([conv_ref[:, cols].astype(F32), convn_ref[:, cols].astype(F32)[0:HALO]], axis=0)

        def back(dc, cols):
            w = w_ref[:, cols]
            dc1, dc2 = _shift(dc, -1)[:rb], _shift(dc, -2)[:rb]
            dc0 = dc[:rb]
            o_ref[:, cols] = (w[2:3] * dc0 + w[1:2] * dc1 + w[0:1] * dc2).astype(BF16)
            u = up_ref[:, cols]
            dwb_ref[0:1, cols] += jnp.sum(dc2 * u, axis=0, keepdims=True)
            dwb_ref[1:2, cols] += jnp.sum(dc1 * u, axis=0, keepdims=True)
            dwb_ref[2:3, cols] += jnp.sum(dc0 * u, axis=0, keepdims=True)
            dwb_ref[3:4, cols] += jnp.sum(dc0, axis=0, keepdims=True)

        for c0 in range(0, f, cw):
            gcols, acols = slice(c0, c0 + cw), slice(f + c0, f + c0 + cw)
            g, a = rows(gcols), rows(acols)
            da = jnp.concatenate([da_ref[:, gcols], jnp.where(last, 0.0, dan_ref[:, gcols])], axis=0)
            sg = _sigmoid(g)
            back(da * a * (sg * (1.0 + g * (1.0 - sg))), gcols)
            back(da * (g * sg), acols)

    return pl.pallas_call(
        body, grid=(t // rb,),
        in_specs=[_row_spec(rb, f2), _row_spec(rb, f2),
                  pl.BlockSpec((BF16_HALO, f2), lambda i: (jnp.minimum((i + 1) * (rb // BF16_HALO), t // BF16_HALO - 1), 0)),
                  _row_spec(rb, f), _next_spec(rb, f, t), _full_spec((3, f2))],
        out_specs=[_row_spec(rb, f2), _full_spec((8, f2))],
        out_shape=[jax.ShapeDtypeStruct((t, f2), BF16), jax.ShapeDtypeStruct((8, f2), F32)],
        compiler_params=_params(("arbitrary",)), name=name,
    )(up, conv, conv, dact, dact, wconv)


def _causal_mask():
    return lax.broadcasted_iota(jnp.int32, (CHUNK, CHUNK), 0) >= lax.broadcasted_iota(jnp.int32, (CHUNK, CHUNK), 1)


def _b_mid_fwd(zp, vnorm, ws, bs, *, name, rb=256):
    t, d2 = zp.shape
    d = d2 // 2
    c = d // GROUPS
    rb = _tile(t, rb, CHUNK)

    def body(zp_ref, gv_ref, ws_ref, bs_ref, ug_ref, vn_ref, gate_ref):
        v = _gelu(zp_ref[:, d:])
        rv = lax.rsqrt(jnp.mean(v * v, axis=-1, keepdims=True) + RMS_EPS)
        vn_ref[...] = ((v * rv) * gv_ref[...]).astype(BF16)
        mask = _causal_mask()
        for h in range(GROUPS):
            hc = slice(h * c, (h + 1) * c)
            wm = jnp.where(mask, ws_ref[h], 0.0).astype(BF16)
            bcol = jnp.broadcast_to(bs_ref[h:h + 1, :], (CHUNK, CHUNK)).T[:, 0:1]
            for n in range(rb // CHUNK):
                rows = slice(n * CHUNK, (n + 1) * CHUNK)
                gate_ref[rows, hc] = jnp.dot(wm, vn_ref[rows, hc], preferred_element_type=F32) + bcol
        ug_ref[...] = (_gelu(zp_ref[:, :d]) * gate_ref[...]).astype(BF16)

    return pl.pallas_call(
        body, grid=(t // rb,),
        in_specs=[_row_spec(rb, d2), _full_spec((1, d)), _full_spec((GROUPS, CHUNK, CHUNK)), _full_spec((GROUPS, CHUNK))],
        out_specs=_row_spec(rb, d), out_shape=jax.ShapeDtypeStruct((t, d), BF16),
        scratch_shapes=[pltpu.VMEM((rb, d), BF16), pltpu.VMEM((rb, d), F32)],
        compiler_params=_params(("parallel",)), name=name,
    )(zp, vnorm, ws, bs)


def _b_mid_bwd(zp, dug, vnorm, ws, bs, *, name, rb=256):
    t, d2 = zp.shape
    d = d2 // 2
    c = d // GROUPS
    rb = _tile(t, rb, CHUNK)

    def body(zp_ref, dug_ref, gv_ref, ws_ref, bs_ref, dzp_ref, dws_ref, dbs_ref, dgv_ref,
             vn_ref, gate_ref, dm_ref, dvn_ref, dbacc_ref):
        i = pl.program_id(0)

        @pl.when(i == 0)
        def _():
            dws_ref[...] = jnp.zeros_like(dws_ref)
            dgv_ref[...] = jnp.zeros_like(dgv_ref)
            dbacc_ref[...] = jnp.zeros_like(dbacc_ref)

        zu, zv = zp_ref[:, :d], zp_ref[:, d:]
        u, v = _gelu(zu), _gelu(zv)
        rv = lax.rsqrt(jnp.mean(v * v, axis=-1, keepdims=True) + RMS_EPS)
        vhat = v * rv
        gv = gv_ref[...]
        vn_ref[...] = (vhat * gv).astype(BF16)
        dug_v = dug_ref[...]
        dm = dug_v * u
        dm_ref[...] = dm.astype(BF16)
        mask = _causal_mask()
        for h in range(GROUPS):
            hc = slice(h * c, (h + 1) * c)
            wm = jnp.where(mask, ws_ref[h], 0.0)
            wm_b, wmt_b = wm.astype(BF16), wm.T.astype(BF16)
            bcol = jnp.broadcast_to(bs_ref[h:h + 1, :], (CHUNK, CHUNK)).T[:, 0:1]
            dws_h = jnp.zeros((CHUNK, CHUNK), F32)
            dbs_h = jnp.zeros((CHUNK, c), F32)
            for n in range(rb // CHUNK):
                rows = slice(n * CHUNK, (n + 1) * CHUNK)
                vn_c, dm_c = vn_ref[rows, hc], dm_ref[rows, hc]
                gate_ref[rows, hc] = jnp.dot(wm_b, vn_c, preferred_element_type=F32) + bcol
                dws_h += lax.dot_general(dm_c, vn_c, (((1,), (1,)), ((), ())), preferred_element_type=F32)
                dvn_ref[rows, hc] = jnp.dot(wmt_b, dm_c, preferred_element_type=F32)
                dbs_h += dm[rows, hc]
            dws_ref[h] += dws_h
            dbacc_ref[h] += dbs_h
        du = dug_v * gate_ref[...]
        dvn = dvn_ref[...]
        dvhat = dvn * gv
        m = jnp.mean(dvhat * vhat, axis=-1, keepdims=True)
        dv = rv * (dvhat - vhat * m)
        dgv_ref[0:1, :] += jnp.sum(dvn * vhat, axis=0, keepdims=True)
        dzp_ref[:, :d] = (du * _gelu_grad(zu)).astype(BF16)
        dzp_ref[:, d:] = (dv * _gelu_grad(zv)).astype(BF16)

        @pl.when(i == pl.num_programs(0) - 1)
        def _():
            ones = jnp.ones((8, c), F32)
            for h in range(GROUPS):
                dws_ref[h] = jnp.where(mask, dws_ref[h], 0.0)
                row = lax.dot_general(ones, dbacc_ref[h], (((1,), (1,)), ((), ())),
                                      precision=lax.Precision.HIGHEST, preferred_element_type=F32)
                dbs_ref[h:h + 1, :] = row[0:1]

    return pl.pallas_call(
        body, grid=(t // rb,),
        in_specs=[_row_spec(rb, d2), _row_spec(rb, d), _full_spec((1, d)), _full_spec((GROUPS, CHUNK, CHUNK)),
                  _full_spec((GROUPS, CHUNK))],
        out_specs=[_row_spec(rb, d2), _full_spec((GROUPS, CHUNK, CHUNK)), _full_spec((GROUPS, CHUNK)), _full_spec((8, d))],
        out_shape=[jax.ShapeDtypeStruct((t, d2), BF16), jax.ShapeDtypeStruct((GROUPS, CHUNK, CHUNK), F32),
                   jax.ShapeDtypeStruct((GROUPS, CHUNK), F32), jax.ShapeDtypeStruct((8, d), F32)],
        scratch_shapes=[pltpu.VMEM((rb, d), BF16), pltpu.VMEM((rb, d), F32), pltpu.VMEM((rb, d), BF16),
                        pltpu.VMEM((rb, d), F32), pltpu.VMEM((GROUPS, CHUNK, c), F32)],
        compiler_params=_params(("arbitrary",)), name=name,
    )(zp, dug, vnorm, ws, bs)


def _cast_into_full(w, layer, kind, place, *, name, dep=None, rb=256):
    _, r, c = w.shape
    rb = _tile(r, rb, 16)
    nrb = r // rb
    full = (r, c * N_CHIPS) if kind == "col" else (r * N_CHIPS, c)

    def body(place_ref, w_ref, *rest):
        rest[-1][...] = w_ref[...].astype(BF16)

    def o_index(i, place):
        return (i, place[0]) if kind == "col" else (i + place[0] * nrb, 0)

    in_specs = [pl.BlockSpec((None, rb, c), lambda i, place: (layer, i, 0))] + ([ANY] if dep is not None else [])
    return pl.pallas_call(
        body,
        grid_spec=pltpu.PrefetchScalarGridSpec(num_scalar_prefetch=1, grid=(nrb,), in_specs=in_specs,
                                               out_specs=pl.BlockSpec((rb, c), o_index)),
        out_shape=jax.ShapeDtypeStruct(full, BF16), compiler_params=_params(("parallel",)), name=name,
    )(place, w, *(() if dep is None else (dep,)))


def _adamw_layer(w, g, m, v, layer, prev, *, name, rb=128):
    _, r, c = w.shape
    rb = _tile(r, rb, 8)
    c1 = 1.0 - ADAM_B1 ** ADAM_STEP
    c2 = 1.0 - ADAM_B2 ** ADAM_STEP

    def body(w_ref, g_ref, m_ref, v_ref, *rest):
        go_ref, d_ref, nm_ref, nv_ref = rest[-4:]
        gv = g_ref[...]
        nm = ADAM_B1 * m_ref[...] + (1.0 - ADAM_B1) * gv
        nv = ADAM_B2 * v_ref[...] + (1.0 - ADAM_B2) * (gv * gv)
        go_ref[...] = gv
        nm_ref[...] = nm
        nv_ref[...] = nv
        d_ref[...] = -ADAM_LR * ((nm / c1) / (jnp.sqrt(nv / c2) + ADAM_EPS) + ADAM_WD * w_ref[...])

    lay = pl.BlockSpec((None, rb, c), lambda i: (layer, i, 0))
    return pl.pallas_call(
        body, grid=(r // rb,), in_specs=[lay, _row_spec(rb, c), lay, lay] + ([ANY] * 4 if prev else []), out_specs=[lay] * 4,
        out_shape=[jax.ShapeDtypeStruct(w.shape, F32)] * 4, input_output_aliases={4 + k: k for k in range(4)} if prev else {},
        compiler_params=_params(("parallel",)), name=name,
    )(w, g, m, v, *(prev or ()))


HBM = pl.BlockSpec(memory_space=pltpu.HBM)
SEM = pl.BlockSpec(memory_space=pltpu.SEMAPHORE)
SIDE_EFFECT = pltpu.SideEffectType.DATAFLOW_SIDE_EFFECTING


def _place():
    x, y, c = lax.axis_index("x"), lax.axis_index("y"), lax.axis_index("c")
    chips = [(1 - x, y), (x, 1 - y), (1 - x, 1 - y)]
    return x, y, c, 2 * x + y, chips


def _half(ref, kind, c):
    r, w = ref.shape
    if kind == "col":
        return ref.at[pl.ds(pl.multiple_of(c * (r // 2), 8), r // 2), :]
    return ref.at[:, pl.ds(pl.multiple_of(c * (w // 2), 128), w // 2)]


def _shard(ref, kind, s):
    r, w = ref.shape
    if kind == "col":
        return ref.at[:, pl.ds(pl.multiple_of(s * (w // N_CHIPS), 128), w // N_CHIPS)]
    return ref.at[pl.ds(pl.multiple_of(s * (r // N_CHIPS), 8), r // N_CHIPS), :]


def _remote(src, dst, send_sem, recv_sem, dev):
    return pltpu.make_async_remote_copy(src_ref=src, dst_ref=dst, send_sem=send_sem, recv_sem=recv_sem,
                                        device_id=dev, device_id_type=MESH)


def _start(name, bufs, plan, sem_shape, dep=None):
    n = len(bufs)
    n_in = n + (dep is not None)

    def body(*refs):
        sends, _ = plan(refs[:n], refs[n_in], refs[n_in + 1])
        for cp in sends:
            cp.start()
        refs[n_in + 2 + n][...] = jnp.zeros((8, 128), F32)

    dma = pltpu.SemaphoreType.DMA
    outs = pl.pallas_call(
        body, name=name,
        out_shape=(dma(sem_shape), dma(sem_shape), *[pltpu.HBM(b.shape, b.dtype) for b in bufs], jax.ShapeDtypeStruct((8, 128), F32)),
        in_specs=(HBM,) * n + ((ANY,) if dep is not None else ()),
        out_specs=(SEM, SEM) + (HBM,) * n + (pl.BlockSpec(memory_space=pltpu.VMEM),),
        input_output_aliases={i: i + 2 for i in range(n)},
        compiler_params=pltpu.CompilerParams(has_side_effects=SIDE_EFFECT),
    )(*[pltpu.with_memory_space_constraint(b, pltpu.HBM) for b in bufs], *(() if dep is None else (dep,)))
    return outs[0], outs[1], list(outs[2:2 + n]), outs[2 + n]


def _wait(name, started, plan, after):
    send, recv, bufs, _ = started
    n = len(bufs)

    def body(*refs):
        sends, recvs = plan(refs[:n], refs[n], refs[n + 1])
        for cp in sends:
            cp.wait_send()
        for cp in recvs:
            cp.wait_recv()

    return list(pl.pallas_call(
        body, name=name, out_shape=tuple(pltpu.HBM(b.shape, b.dtype) for b in bufs),
        in_specs=(HBM,) * n + (SEM, SEM, ANY), out_specs=(HBM,) * n, input_output_aliases={i: i for i in range(n)},
        compiler_params=pltpu.CompilerParams(has_side_effects=SIDE_EFFECT),
    )(*bufs, send, recv, after))


KINDS = ("col", "row")


def _gather_ici_plan(n_small):
    def plan(refs, send, recv):
        x, y, c, s, chips = _place()
        n = len(KINDS) + n_small
        sends, recvs = [], []
        for k, (px, py) in enumerate(chips):
            sp = 2 * px + py
            for a, kind in enumerate(KINDS):
                mine, theirs = _half(_shard(refs[a], kind, s), kind, c), _half(_shard(refs[a], kind, sp), kind, c)
                sends.append(_remote(mine, mine, send.at[k * n + a], recv.at[k * n + a], (px, py, c)))
                recvs.append(_remote(theirs, theirs, send.at[k * n + a], recv.at[k * n + a], (px, py, c)))
            for b in range(n_small):
                ref, sem = refs[len(KINDS) + b], k * n + len(KINDS) + b
                sends.append(_remote(ref.at[s], ref.at[s], send.at[sem], recv.at[sem], (px, py, c)))
                recvs.append(_remote(ref.at[sp], ref.at[sp], send.at[sem], recv.at[sem], (px, py, c)))
        return sends, recvs
    return plan


def _gather_d2d_plan(refs, send, recv):
    x, y, c, _, chips = _place()
    n = len(KINDS)
    sends, recvs = [], []
    for k, (px, py) in enumerate(chips):
        for a, kind in enumerate(KINDS):
            region, sem = _shard(refs[a], kind, 2 * px + py), k * n + a
            sends.append(_remote(_half(region, kind, c), _half(region, kind, c), send.at[sem], recv.at[sem], (x, y, 1 - c)))
            recvs.append(_remote(_half(region, kind, 1 - c), _half(region, kind, 1 - c), send.at[sem], recv.at[sem], (x, y, 1 - c)))
    return sends, recvs


def _swap_plan(refs, send, recv):
    x, y, c, _, _ = _place()
    n = len(KINDS)
    cps = [_remote(_half(refs[a], KINDS[a], 1 - c), refs[n + a], send.at[a], recv.at[a], (x, y, 1 - c)) for a in range(n)]
    return cps, cps


def _exchange_plan(refs, send, recv):
    x, y, c, _, chips = _place()
    n = len(KINDS)
    cps = []
    for k, (px, py) in enumerate(chips):
        for a in range(n):
            cps.append(_remote(_shard(refs[a], KINDS[a], 2 * px + py), refs[n + a].at[k], send.at[k * n + a], recv.at[k * n + a],
                               (px, py, c)))
    return cps, cps


def _share_plan(refs, send, recv):
    x, y, c, _, _ = _place()
    sends = [_remote(_half(refs[a], KINDS[a], c), _half(refs[a], KINDS[a], c), send.at[a], recv.at[a], (x, y, 1 - c))
             for a in range(len(KINDS))]
    recvs = [_remote(_half(refs[a], KINDS[a], 1 - c), _half(refs[a], KINDS[a], 1 - c), send.at[a], recv.at[a], (x, y, 1 - c))
             for a in range(len(KINDS))]
    return sends, recvs


def _spread_plan(refs, send, recv):
    packed, slots = refs
    x, y, c, _, _ = _place()
    sends, recvs = [], []
    for k in range(1, 8):
        px, py, pc = x ^ (k >> 2), y ^ ((k >> 1) & 1), c ^ (k & 1)
        sends.append(_remote(packed, slots.at[4 * x + 2 * y + c], send.at[k - 1], recv.at[k - 1], (px, py, pc)))
        recvs.append(_remote(packed, slots.at[4 * px + 2 * py + pc], send.at[k - 1], recv.at[k - 1], (px, py, pc)))
    return sends, recvs


def _half_index(kind, nblk):
    def index(i, j, place):
        return (i + place[1] * nblk[0], j) if kind == "col" else (i, j + place[1] * nblk[1])
    return index


def _chip_partial(g, other, kind, place, *, name):
    hr, hc = other.shape
    rb, cb = _tile(hr, 512, 16), _tile(hc, 1024)
    nblk = (hr // rb, hc // cb)

    def body(place_ref, g_ref, o_ref, p_ref):
        p_ref[...] = (g_ref[...].astype(F32) + o_ref[...].astype(F32)).astype(BF16)

    plain = pl.BlockSpec((rb, cb), lambda i, j, place: (i, j))
    return pl.pallas_call(
        body,
        grid_spec=pltpu.PrefetchScalarGridSpec(
            num_scalar_prefetch=1, grid=nblk, in_specs=[pl.BlockSpec((rb, cb), _half_index(kind, nblk)), plain], out_specs=plain),
        out_shape=jax.ShapeDtypeStruct((hr, hc), BF16), compiler_params=_params(("parallel", "parallel")), name=name,
    )(place, g, other)


def _reduce_half(g, other, recv, kind, place, *, name):
    _, pr, pc = recv.shape
    rb, cb = _tile(pr, 512, 16), _tile(pc, 1024)
    nblk = (pr // rb, pc // cb)
    full = (pr * 2, pc) if kind == "col" else (pr, pc * 2)

    def g_index(i, j, place):
        s, c = place[0], place[1]
        return (i + c * nblk[0], j + s * nblk[1]) if kind == "col" else (i + s * nblk[0], j + c * nblk[1])

    def o_index(i, j, place):
        return (i, j + place[0] * nblk[1]) if kind == "col" else (i + place[0] * nblk[0], j)

    def body(place_ref, g_ref, o_ref, r_ref, out_ref):
        acc = g_ref[...].astype(F32) + o_ref[...].astype(F32)
        for k in range(3):
            acc = acc + r_ref[k].astype(F32)
        out_ref[...] = acc

    return pl.pallas_call(
        body,
        grid_spec=pltpu.PrefetchScalarGridSpec(
            num_scalar_prefetch=1, grid=nblk,
            in_specs=[pl.BlockSpec((rb, cb), g_index), pl.BlockSpec((rb, cb), o_index),
                      pl.BlockSpec((3, rb, cb), lambda i, j, place: (0, i, j))],
            out_specs=pl.BlockSpec((rb, cb), _half_index(kind, nblk))),
        out_shape=jax.ShapeDtypeStruct(full, F32), compiler_params=_params(("parallel", "parallel")), name=name,
    )(place, g, other, recv)


def _sum_slots(packed, slots, me, *, name, rb=512):
    r, w = packed.shape
    rb = _tile(r, rb, 8)

    def body(me_ref, p_ref, s_ref, o_ref):
        acc = None
        for j in range(8):
            term = jnp.where(me_ref[0] == j, p_ref[...], s_ref[j])
            acc = term if acc is None else acc + term
        o_ref[...] = acc

    return pl.pallas_call(
        body,
        grid_spec=pltpu.PrefetchScalarGridSpec(
            num_scalar_prefetch=1, grid=(r // rb,),
            in_specs=[pl.BlockSpec((rb, w), lambda i, me: (i, 0)), pl.BlockSpec((8, rb, w), lambda i, me: (0, i, 0))],
            out_specs=pl.BlockSpec((rb, w), lambda i, me: (i, 0))),
        out_shape=jax.ShapeDtypeStruct((r, w), F32), compiler_params=_params(("parallel",)), name=name,
    )(me, packed, slots)


def _half_shape(a, kind):
    return (a.shape[0] // 2, a.shape[1]) if kind == "col" else (a.shape[0], a.shape[1] // 2)


def _rs_swap(tag, grads):
    others = [lax.empty(_half_shape(g, k), g.dtype) for g, k in zip(grads, KINDS)]
    return _start(f"rs_{tag}_swap", list(grads) + others, _swap_plan, (len(KINDS),))


def _rs_exchange(tag, swapped, place, after):
    bufs = _wait(f"rs_{tag}_swap_wait", swapped, _swap_plan, after)
    n = len(KINDS)
    grads, others = bufs[:n], bufs[n:]
    parts = [_chip_partial(g, o, k, place, name=f"rs_{tag}_partial_{k}") for g, o, k in zip(grads, others, KINDS)]
    lands = []
    for p, k in zip(parts, KINDS):
        piece = (p.shape[0], p.shape[1] // N_CHIPS) if k == "col" else (p.shape[0] // N_CHIPS, p.shape[1])
        lands.append(lax.empty((3,) + piece, p.dtype))
    return _start(f"rs_{tag}_exchange", parts + lands, _exchange_plan, (3 * n,)), grads, others


def _rs_share(tag, exchanged, place, after):
    started, grads, others = exchanged
    n = len(KINDS)
    recvs = _wait(f"rs_{tag}_exchange_wait", started, _exchange_plan, after)[n:]
    halves = [_reduce_half(g, o, r, k, place, name=f"rs_{tag}_reduce_{k}") for g, o, r, k in zip(grads, others, recvs, KINDS)]
    return _start(f"rs_{tag}_share", halves, _share_plan, (n,))


def _rs_finish(tag, shared, after):
    return _wait(f"rs_{tag}_share_wait", shared, _share_plan, after)


def _spread(tag, parts, dep):
    rows = [p.reshape(-1, 128) for p in parts]
    n = sum(r.shape[0] for r in rows)
    rows.append(jnp.zeros(((-n) % 512, 128), F32))
    packed = jnp.concatenate(rows, axis=0)
    return _start(f"small_{tag}_spread", [packed, lax.empty((8,) + packed.shape, F32)], _spread_plan, (7,), dep=dep)


def _spread_sum(tag, started, parts, me, after):
    packed, slots = _wait(f"small_{tag}_spread_wait", started, _spread_plan, after)
    total = _sum_slots(packed, slots, me, name=f"small_{tag}_sum")
    out, row = [], 0
    for p in parts:
        n = p.size // 128
        out.append(total[row:row + n].reshape(p.shape))
        row += n
    return out


def _ffn_fwd(x, h, w_up, conv_w, conv_b, w_down, tag):
    up = _mm(h, w_up, name=f"ffn{tag}_up")
    act, conv = _ffn_mid_fwd(up, conv_w, conv_b, name=f"ffn{tag}_mid")
    x_out = _mm(act, w_down, res=x, tk=2816, name=f"ffn{tag}_down")
    return x_out, (up, conv, act)


def kernel(x, a_norm, a_in, a_conv, a_out, b_norm, b_in, b_vnorm, b_ws, b_bs, b_out, f_norm, f_up, f_conv_w, f_conv_b, f_down, final_norm, loss_target, m_a_norm, m_a_in, m_a_conv, m_a_out, m_b_norm, m_b_in, m_b_vnorm, m_b_ws, m_b_bs, m_b_out, m_f_norm, m_f_up, m_f_conv_w, m_f_conv_b, m_f_down, m_final_norm, v_a_norm, v_a_in, v_a_conv, v_a_out, v_b_norm, v_b_in, v_b_vnorm, v_b_ws, v_b_bs, v_b_out, v_f_norm, v_f_up, v_f_conv_w, v_f_conv_b, v_f_down, v_final_norm):
    t, d = x.shape[1], x.shape[2]
    f2 = f_up.shape[2] * N_CHIPS
    x0, tgt = x.reshape(t, d), loss_target.reshape(t, d)
    ax, ay, ac = lax.axis_index("x"), lax.axis_index("y"), lax.axis_index("c")
    s = 2 * ax + ay
    place = jnp.stack([s, ac]).astype(jnp.int32)
    me = (4 * ax + 2 * ay + ac).astype(jnp.int32).reshape(1)

    def stacked(a):
        return lax.dynamic_update_index_in_dim(jnp.zeros((N_CHIPS,) + a.shape, F32), a, s, 0)

    def gather_start(tag, w_in, w_out, layer, small, dep):
        fulls = [_cast_into_full(w_in, layer, "col", place, name=f"cast_{tag}_in", dep=dep),
                 _cast_into_full(w_out, layer, "row", place, name=f"cast_{tag}_out", dep=dep)]
        return _start(f"ag_{tag}_ici", fulls + small, _gather_ici_plan(len(small)), (3 * (2 + len(small)),), dep=dep)

    def gather_forward(tag, started, n_small, after):
        bufs = _wait(f"ag_{tag}_ici_wait", started, _gather_ici_plan(n_small), after)
        return _start(f"ag_{tag}_d2d", bufs[:2], _gather_d2d_plan, (3 * 2,)), bufs[2:]

    def gather_finish(tag, forwarded, after):
        return _wait(f"ag_{tag}_d2d_wait", forwarded, _gather_d2d_plan, after)

    small = [stacked(a_conv[0]), stacked(b_norm), stacked(b_vnorm), stacked(f_conv_w.reshape(2 * 3, -1))]
    ag_a = gather_start("a", a_in, a_out, 0, small, None)
    ag_f0 = gather_start("f0", f_up, f_down, 0, [], ag_a[3])
    ag_b = gather_start("b", b_in, b_out, 0, [], ag_f0[3])
    ag_f1 = gather_start("f1", f_up, f_down, 1, [], ag_b[3])

    def unshard(a):
        return jnp.transpose(a, (1, 0, 2)).reshape(a.shape[1], -1)

    ws, bs = b_ws[0], b_bs[0]

    h0 = _rms_fwd(x0, a_norm, dep=ag_f1[3], name="a_norm")
    fw_a, (g_aconv, g_bnorm, g_bvnorm, g_fconv) = gather_forward("a", ag_a, 4, h0)
    w_ai, w_ao = gather_finish("a", fw_a, fw_a[3])
    a_conv_f, b_norm_f, b_vnorm_f = unshard(g_aconv), unshard(g_bnorm), unshard(g_bvnorm)
    f_conv_f = unshard(g_fconv).reshape(2, 3, f2)
    bcx = _mm(h0, w_ai, name="a_in")
    y = _a_mid_fwd(bcx, a_conv_f, name="a_mid")
    x1 = _mm(y, w_ao, res=x0, tm=512, tn=2048, name="a_out")
    fw_f0, _ = gather_forward("f0", ag_f0, 0, x1)
    h1 = _rms_fwd(x1, f_norm[0:1], dep=fw_f0[3], name="ffn0_norm")
    w_up0, w_dn0 = gather_finish("f0", fw_f0, h1)
    x2, (up0, conv0, act0) = _ffn_fwd(x1, h1, w_up0, f_conv_f[0], f_conv_b[0:1], w_dn0, 0)
    fw_b, _ = gather_forward("b", ag_b, 0, up0)
    w_bi, w_bo = gather_finish("b", fw_b, act0)
    h2 = _rms_fwd(x2, b_norm_f, name="b_norm")
    zp = _mm(h2, w_bi, name="b_in")
    fw_f1, _ = gather_forward("f1", ag_f1, 0, zp)
    ug = _b_mid_fwd(zp, b_vnorm_f, ws, bs, name="b_mid")
    x3, h3 = _mm(ug, w_bo, res=x2, norm=f_norm[1:2], tm=512, tn=2048, name="b_out")
    w_up1, w_dn1 = gather_finish("f1", fw_f1, x3)
    x4, (up1, conv1, act1) = _ffn_fwd(x3, h3, w_up1, f_conv_f[1], f_conv_b[1:2], w_dn1, 1)
    loss_rows, dx4, dx4b, d_final = _final(x4, tgt, final_norm.reshape(1, d), name="final")

    d_dn1 = _mm(act1, dx4b, ta=True, tm=1408, out_dtype=BF16, name="ffn1_ddown")
    dact1 = _mm(dx4b, w_dn1, tb=True, tn=512, tm=2048, name="ffn1_dact")
    dup1, d_fwb1 = _ffn_mid_bwd(up1, conv1, dact1, f_conv_f[1], name="ffn1_mid_bwd")
    d_up1 = _mm(h3, dup1, ta=True, out_dtype=BF16, name="ffn1_dup")
    sw_f1 = _rs_swap("f1", [d_up1, d_dn1])
    dh3 = _mm(dup1, w_up1, tb=True, tk=2816, dep=sw_f1[3], name="ffn1_dh")
    dx3, dx3b, d_fnorm1 = _rms_bwd(dh3, x3, f_norm[1:2], dx4, name="ffn1_norm_bwd")
    ex_f1 = _rs_exchange("f1", sw_f1, place, dx3)

    d_bo = _mm(ug, dx3b, ta=True, out_dtype=BF16, dep=ex_f1[0][3], name="b_dout")
    dug = _mm(dx3b, w_bo, tb=True, tm=512, tn=2048, name="b_dug")
    dzp, d_ws, d_bs, d_bvnorm = _b_mid_bwd(zp, dug, b_vnorm_f, ws, bs, name="b_mid_bwd")
    d_bi = _mm(h2, dzp, ta=True, out_dtype=BF16, name="b_din")
    sw_b = _rs_swap("b", [d_bi, d_bo])
    dh2 = _mm(dzp, w_bi, tb=True, dep=sw_b[3], name="b_dh")
    dx2, dx2b, d_bnorm = _rms_bwd(dh2, x2, b_norm_f, dx3, name="b_norm_bwd")
    ex_b = _rs_exchange("b", sw_b, place, dx2)

    d_dn0 = _mm(act0, dx2b, ta=True, tm=1408, tn=2048, tk=1024, out_dtype=BF16, dep=ex_b[0][3], name="ffn0_ddown")
    dact0 = _mm(dx2b, w_dn0, tb=True, tn=512, tm=2048, name="ffn0_dact")
    dup0, d_fwb0 = _ffn_mid_bwd(up0, conv0, dact0, f_conv_f[0], name="ffn0_mid_bwd")
    sh_f1 = _rs_share("f1", ex_f1, place, dup0)
    d_up0 = _mm(h1, dup0, ta=True, out_dtype=BF16, tk=1024, dep=sh_f1[3], name="ffn0_dup")
    sw_f0 = _rs_swap("f0", [d_up0, d_dn0])
    g_up1, g_dn1 = _rs_finish("f1", sh_f1, sw_f0[3])
    dh1 = _mm(dup0, w_up0, tb=True, tk=1408, dep=sw_f0[3], name="ffn0_dh")
    dx1, dx1b, d_fnorm0 = _rms_bwd(dh1, x1, f_norm[0:1], dx2, name="ffn0_norm_bwd")
    ex_f0 = _rs_exchange("f0", sw_f0, place, dx1)
    early = [jnp.concatenate([d_bnorm, d_bvnorm, d_fnorm0, d_fnorm1, d_final, loss_rows], axis=0),
             jnp.concatenate([d_fwb0, d_fwb1], axis=0), jnp.concatenate([d_ws.reshape(-1, CHUNK), d_bs], axis=0)]
    sp_early = _spread("early", early, ex_f0[0][3])
    sh_b = _rs_share("b", ex_b, place, sp_early[3])

    d_ao = _mm(y, dx1b, ta=True, out_dtype=BF16, dep=sh_b[3], name="a_dout")
    dyy = _mm(dx1b, w_ao, tb=True, tm=512, tn=2048, name="a_dy")
    dbcx, d_aconv = _a_mid_bwd(bcx, dyy, a_conv_f, name="a_mid_bwd")
    d_ai = _mm(h0, dbcx, ta=True, out_dtype=BF16, name="a_din")
    sw_a = _rs_swap("a", [d_ai, d_ao])
    g_bi, g_bo = _rs_finish("b", sh_b, sw_a[3])
    ex_a = _rs_exchange("a", sw_a, place, g_bi)
    dh0 = _mm(dbcx, w_ai, tb=True, dep=ex_a[0][3], name="a_dh")
    grad_x, _, d_anorm = _rms_bwd(dh0, x0, a_norm, dx1, name="a_norm_bwd")
    late = [jnp.concatenate([d_anorm, d_aconv], axis=0)]
    sp_late = _spread("late", late, ex_a[0][3])
    sh_f0 = _rs_share("f0", ex_f0, place, sp_late[3])
    sh_a = _rs_share("a", ex_a, place, sh_f0[3])
    g_up0, g_dn0 = _rs_finish("f0", sh_f0, sh_a[3])
    g_ai, g_ao = _rs_finish("a", sh_a, g_up0)
    r_a, r_b, r_c = _spread_sum("early", sp_early, early, me, g_ai)
    (r_l,) = _spread_sum("late", sp_late, late, me, r_a)

    loss = jnp.sum(r_a[40])
    cs, fs = d // N_CHIPS, f2 // N_CHIPS

    def mine(a, width):
        return lax.dynamic_slice_in_dim(a, s * width, width, axis=1)

    grads = {
        "a_norm": r_l[0:1], "a_conv": mine(r_l[8:11], cs), "b_norm": mine(r_a[0:1], cs), "b_vnorm": mine(r_a[8:9], cs),
        "f_norm": jnp.concatenate([r_a[16:17], r_a[24:25]], axis=0), "final_norm": r_a[32:33],
        "b_ws": r_c[:GROUPS * CHUNK], "b_bs": r_c[GROUPS * CHUNK:],
        "f_conv_w": jnp.concatenate([mine(r_b[0:3], fs), mine(r_b[8:11], fs)], axis=0),
        "f_conv_b": jnp.concatenate([r_b[3:4], r_b[11:12]], axis=0),
        "a_in": g_ai, "a_out": g_ao, "b_in": g_bi, "b_out": g_bo,
    }
    names = ["a_norm", "a_in", "a_conv", "a_out", "b_norm", "b_in", "b_vnorm", "b_ws", "b_bs", "b_out", "f_norm", "f_up",
             "f_conv_w", "f_conv_b", "f_down", "final_norm"]
    weights = dict(zip(names, [a_norm, a_in, a_conv, a_out, b_norm, b_in, b_vnorm, b_ws, b_bs, b_out, f_norm, f_up, f_conv_w,
                               f_conv_b, f_down, final_norm]))
    ms = dict(zip(names, [m_a_norm, m_a_in, m_a_conv, m_a_out, m_b_norm, m_b_in, m_b_vnorm, m_b_ws, m_b_bs, m_b_out, m_f_norm,
                          m_f_up, m_f_conv_w, m_f_conv_b, m_f_down, m_final_norm]))
    vs = dict(zip(names, [v_a_norm, v_a_in, v_a_conv, v_a_out, v_b_norm, v_b_in, v_b_vnorm, v_b_ws, v_b_bs, v_b_out, v_f_norm,
                          v_f_up, v_f_conv_w, v_f_conv_b, v_f_down, v_final_norm]))
    result = {}
    for n in names:
        w = weights[n]
        if n in ("f_up", "f_down"):
            g1, g0 = (g_up1, g_up0) if n == "f_up" else (g_dn1, g_dn0)
            first = _adamw_layer(w, g1, ms[n], vs[n], 1, None, name=f"adamw_{n}1")
            result[n] = _adamw_layer(w, g0, ms[n], vs[n], 0, tuple(first), name=f"adamw_{n}0")
            continue
        g2 = grads[n]
        as3d = (lambda a: a.reshape((1,) + g2.shape))
        result[n] = [o.reshape(w.shape) for o in _adamw_layer(as3d(w), g2, as3d(ms[n]), as3d(vs[n]), 0, None, name=f"adamw_{n}")]

    return (loss, grad_x.reshape(x.shape), *[result[n][0] for n in names], *[result[n][1] for n in names],
            *[result[n][2] for n in names], *[result[n][3] for n in names])
```

```python
import functools

import jax
import jax.numpy as jnp
from jax import lax
from jax.experimental import pallas as pl
from jax.experimental.pallas import tpu as pltpu

F32 = jnp.float32
BF16 = jnp.bfloat16
MESH = pl.DeviceIdType.MESH
ANY = pl.BlockSpec(memory_space=pl.ANY)

RMS_EPS = 1e-5
CHUNK = 128
GROUPS = 8
ADAM_LR, ADAM_B1, ADAM_B2, ADAM_EPS, ADAM_WD, ADAM_STEP = 0.001, 0.9, 0.999, 1e-08, 0.01, 10

N_CHIPS = 4
HALO = 8
VMEM_LIMIT = 56 * 1024 * 1024
GELU_C = 0.7978845608028654
GELU_A = 0.044715


def _params(sem=None):
    return pltpu.CompilerParams(dimension_semantics=sem, vmem_limit_bytes=VMEM_LIMIT)


def _tile(dim, pref, quantum=128):
    if dim <= pref:
        return dim
    t = (pref // quantum) * quantum
    while t >= quantum:
        if dim % t == 0:
            return t
        t -= quantum
    return dim


def _mm(a, b, *, name, ta=False, tb=False, res=None, norm=None, dep=None, out_dtype=F32, tm=1024, tn=1024, tk=2048):
    (K, M) = a.shape if ta else a.shape[::-1]
    N = b.shape[0] if tb else b.shape[1]
    assert (b.shape[1] if tb else b.shape[0]) == K
    tm, tn, tk = _tile(M, tm), _tile(N, tn), _tile(K, tk)
    nk = K // tk
    assert norm is None or (tn == N and nk == 1)
    a_spec = pl.BlockSpec((tk, tm), lambda i, j, k: (k, i)) if ta else pl.BlockSpec((tm, tk), lambda i, j, k: (i, k))
    b_spec = pl.BlockSpec((tn, tk), lambda i, j, k: (j, k)) if tb else pl.BlockSpec((tk, tn), lambda i, j, k: (k, j))
    o_spec = pl.BlockSpec((tm, tn), lambda i, j, k: (i, j))
    dims = (((0 if ta else 1,), (1 if tb else 0,)), ((), ()))
    direct = out_dtype == F32
    n_in = 2 + (res is not None) + (norm is not None) + (dep is not None)

    def body(*refs):
        a_ref, b_ref = refs[0], refs[1]
        r_ref = refs[2] if res is not None else None
        g_ref = refs[2 + (res is not None)] if norm is not None else None
        o_ref = refs[n_in]
        acc_ref = o_ref if direct else refs[-1]
        part = lax.dot_general(a_ref[...], b_ref[...], dims, preferred_element_type=F32)
        if nk == 1:
            if r_ref is not None:
                part = part + r_ref[...]
            o_ref[...] = part.astype(o_ref.dtype)
            if g_ref is not None:
                r = lax.rsqrt(jnp.mean(part * part, axis=-1, keepdims=True) + RMS_EPS)
                refs[n_in + 1][...] = ((part * r) * g_ref[...]).astype(BF16)
            return
        k = pl.program_id(2)

        @pl.when(k == 0)
        def _():
            acc_ref[...] = part

        @pl.when(jnp.logical_and(k > 0, k < nk - 1))
        def _():
            acc_ref[...] += part

        @pl.when(k == nk - 1)
        def _():
            tot = acc_ref[...] + part
            if r_ref is not None:
                tot = tot + r_ref[...]
            o_ref[...] = tot.astype(o_ref.dtype)

    in_specs = ([a_spec, b_spec] + ([o_spec] if res is not None else [])
                + ([pl.BlockSpec((1, tn), lambda i, j, k: (0, j))] if norm is not None else []) + ([ANY] if dep is not None else []))
    args = (a, b) + tuple(x for x in (res, norm, dep) if x is not None)
    scratch = [] if (direct or nk == 1) else [pltpu.VMEM((tm, tn), F32)]
    out_shape = jax.ShapeDtypeStruct((M, N), out_dtype)
    return pl.pallas_call(
        body, grid=(M // tm, N // tn, nk), in_specs=in_specs, out_specs=[o_spec, o_spec] if norm is not None else o_spec,
        out_shape=[out_shape, jax.ShapeDtypeStruct((M, N), BF16)] if norm is not None else out_shape, scratch_shapes=scratch,
        compiler_params=_params(("parallel", "parallel", "arbitrary")), name=name,
    )(*args)


def _row_spec(rb, w):
    return pl.BlockSpec((rb, w), lambda i: (i, 0))


def _prev_spec(rb, w):
    return pl.BlockSpec((HALO, w), lambda i: (jnp.maximum(i * (rb // HALO) - 1, 0), 0))


def _next_spec(rb, w, t):
    return pl.BlockSpec((HALO, w), lambda i: (jnp.minimum((i + 1) * (rb // HALO), t // HALO - 1), 0))


def _full_spec(shape):
    return pl.BlockSpec(shape, lambda i: tuple(0 for _ in shape))


def _shift(e, s):
    return pltpu.roll(e, s % e.shape[0], 0)


def _gelu(x):
    return 0.5 * x * (1.0 + jnp.tanh(GELU_C * (x + GELU_A * x * x * x)))


def _gelu_grad(x):
    th = jnp.tanh(GELU_C * (x + GELU_A * x * x * x))
    return 0.5 * (1.0 + th) + 0.5 * x * (1.0 - th * th) * (GELU_C * (1.0 + 3.0 * GELU_A * x * x))


def _sigmoid(x):
    return 1.0 / (1.0 + jnp.exp(-x))


def _rms_fwd(x, g, *, name, dep=None, rb=256):
    t, d = x.shape
    rb = _tile(t, rb, 8)

    def body(x_ref, g_ref, *rest):
        h_ref = rest[-1]
        xv = x_ref[...]
        r = lax.rsqrt(jnp.mean(xv * xv, axis=-1, keepdims=True) + RMS_EPS)
        h_ref[...] = ((xv * r) * g_ref[...]).astype(BF16)

    return pl.pallas_call(
        body, grid=(t // rb,), in_specs=[_row_spec(rb, d), _full_spec((1, d))] + ([ANY] if dep is not None else []),
        out_specs=_row_spec(rb, d), out_shape=jax.ShapeDtypeStruct((t, d), BF16), compiler_params=_params(("parallel",)), name=name,
    )(x, g, *(() if dep is None else (dep,)))


def _rms_bwd(dh, x, g, dres, *, name, dep=None, rb=256):
    t, d = x.shape
    rb = _tile(t, rb, 8)

    def body(dh_ref, x_ref, g_ref, dres_ref, *rest):
        dx_ref, dxb_ref, dg_ref = rest[-3:]
        xv = x_ref[...]
        r = lax.rsqrt(jnp.mean(xv * xv, axis=-1, keepdims=True) + RMS_EPS)
        xhat = xv * r
        dh_v = dh_ref[...]
        dxhat = dh_v * g_ref[...]
        m = jnp.mean(dxhat * xhat, axis=-1, keepdims=True)
        dx = dres_ref[...] + r * (dxhat - xhat * m)
        dx_ref[...] = dx
        dxb_ref[...] = dx.astype(BF16)

        @pl.when(pl.program_id(0) == 0)
        def _():
            dg_ref[...] = jnp.zeros_like(dg_ref)

        dg_ref[0:1, :] += jnp.sum(dh_v * xhat, axis=0, keepdims=True)

    return pl.pallas_call(
        body, grid=(t // rb,),
        in_specs=[_row_spec(rb, d), _row_spec(rb, d), _full_spec((1, d)), _row_spec(rb, d)] + ([ANY] if dep is not None else []),
        out_specs=[_row_spec(rb, d), _row_spec(rb, d), _full_spec((8, d))],
        out_shape=[jax.ShapeDtypeStruct((t, d), F32), jax.ShapeDtypeStruct((t, d), BF16), jax.ShapeDtypeStruct((8, d), F32)],
        compiler_params=_params(("arbitrary",)), name=name,
    )(dh, x, g, dres, *(() if dep is None else (dep,)))


def _final(x, tgt, g, *, name, rb=256):
    t, d = x.shape
    rb = _tile(t, rb, 8)
    inv_d = 1.0 / d

    def body(x_ref, t_ref, g_ref, l_ref, dx_ref, dxb_ref, dg_ref):
        xv = x_ref[...]
        gv = g_ref[...]
        r = lax.rsqrt(jnp.mean(xv * xv, axis=-1, keepdims=True) + RMS_EPS)
        xhat = xv * r
        e = xhat * gv - t_ref[...]
        dy = e * inv_d
        dxhat = dy * gv
        m = jnp.mean(dxhat * xhat, axis=-1, keepdims=True)
        dx = r * (dxhat - xhat * m)
        dx_ref[...] = dx
        dxb_ref[...] = dx.astype(BF16)

        @pl.when(pl.program_id(0) == 0)
        def _():
            l_ref[...] = jnp.zeros_like(l_ref)
            dg_ref[...] = jnp.zeros_like(dg_ref)

        l_ref[0:1, :] += jnp.sum(e * e, axis=0, keepdims=True) * (0.5 * inv_d)
        dg_ref[0:1, :] += jnp.sum(dy * xhat, axis=0, keepdims=True)

    return pl.pallas_call(
        body, grid=(t // rb,),
        in_specs=[_row_spec(rb, d), _row_spec(rb, d), _full_spec((1, d))],
        out_specs=[_full_spec((8, d)), _row_spec(rb, d), _row_spec(rb, d), _full_spec((8, d))],
        out_shape=[jax.ShapeDtypeStruct((8, d), F32), jax.ShapeDtypeStruct((t, d), F32),
                   jax.ShapeDtypeStruct((t, d), BF16), jax.ShapeDtypeStruct((8, d), F32)],
        compiler_params=_params(("arbitrary",)), name=name,
    )(x, tgt, g)


def _a_mid_fwd(bcx, wconv, *, name, rb=256, cw=512):
    t, d3 = bcx.shape
    d = d3 // 3
    rb, cw = _tile(t, rb, 8), _tile(d, cw)

    def body(cur_ref, prev_ref, w_ref, y_ref):
        first = pl.program_id(0) == 0
        for c0 in range(0, d, cw):
            cs = slice(c0, c0 + cw)
            gc, xs = slice(d + c0, d + c0 + cw), slice(2 * d + c0, 2 * d + c0 + cw)
            p_prev = jnp.where(first, 0.0, prev_ref[:, gc] * prev_ref[:, xs])
            e = jnp.concatenate([p_prev, cur_ref[:, gc] * cur_ref[:, xs]], axis=0)
            w = w_ref[:, cs]
            q = w[0:1] * _shift(e, 2) + w[1:2] * _shift(e, 1) + w[2:3] * e
            y_ref[:, cs] = (cur_ref[:, cs] * q[HALO:]).astype(BF16)

    return pl.pallas_call(
        body, grid=(t // rb,),
        in_specs=[_row_spec(rb, d3), _prev_spec(rb, d3), _full_spec((3, d))], out_specs=_row_spec(rb, d),
        out_shape=jax.ShapeDtypeStruct((t, d), BF16), compiler_params=_params(("parallel",)), name=name,
    )(bcx, bcx, wconv)


def _a_mid_bwd(bcx, dy, wconv, *, name, rb=128, cw=512):
    t, d3 = bcx.shape
    d = d3 // 3
    rb, cw = _tile(t, rb, 8), _tile(d, cw)

    def body(cur_ref, prev_ref, next_ref, dy_ref, dyn_ref, w_ref, o_ref, dw_ref):
        i = pl.program_id(0)
        first, last = i == 0, i == pl.num_programs(0) - 1

        @pl.when(first)
        def _():
            dw_ref[...] = jnp.zeros_like(dw_ref)

        for c0 in range(0, d, cw):
            cs = slice(c0, c0 + cw)
            gc, xs = slice(d + c0, d + c0 + cw), slice(2 * d + c0, 2 * d + c0 + cw)
            zeros = jnp.zeros((HALO, cw), F32)
            gb_c, gc_c, xs_c = cur_ref[:, cs], cur_ref[:, gc], cur_ref[:, xs]
            p_prev = jnp.where(first, 0.0, prev_ref[:, gc] * prev_ref[:, xs])
            e = jnp.concatenate([p_prev, gc_c * xs_c, zeros], axis=0)
            dq_next = jnp.where(last, 0.0, dyn_ref[:, cs] * next_ref[:, cs])
            dy_c = dy_ref[:, cs]
            dq = jnp.concatenate([zeros, dy_c * gb_c, dq_next], axis=0)
            w = w_ref[:, cs]
            e1, e2 = _shift(e, 1), _shift(e, 2)
            q = w[0:1] * e2 + w[1:2] * e1 + w[2:3] * e
            dp = (w[2:3] * dq + w[1:2] * _shift(dq, -1) + w[0:1] * _shift(dq, -2))[HALO:HALO + rb]
            o_ref[:, cs] = (dy_c * q[HALO:HALO + rb]).astype(BF16)
            o_ref[:, gc] = (dp * xs_c).astype(BF16)
            o_ref[:, xs] = (dp * gc_c).astype(BF16)
            dq_c = dq[HALO:HALO + rb]
            dw_ref[0:1, cs] += jnp.sum(dq_c * e2[HALO:HALO + rb], axis=0, keepdims=True)
            dw_ref[1:2, cs] += jnp.sum(dq_c * e1[HALO:HALO + rb], axis=0, keepdims=True)
            dw_ref[2:3, cs] += jnp.sum(dq_c * e[HALO:HALO + rb], axis=0, keepdims=True)

    return pl.pallas_call(
        body, grid=(t // rb,),
        in_specs=[_row_spec(rb, d3), _prev_spec(rb, d3), _next_spec(rb, d3, t), _row_spec(rb, d), _next_spec(rb, d, t),
                  _full_spec((3, d))],
        out_specs=[_row_spec(rb, d3), _full_spec((8, d))],
        out_shape=[jax.ShapeDtypeStruct((t, d3), BF16), jax.ShapeDtypeStruct((8, d), F32)],
        compiler_params=_params(("arbitrary",)), name=name,
    )(bcx, bcx, bcx, dy, dy, wconv)


def _ffn_mid_fwd(up, wconv, bconv, *, name, rb=128, cw=512):
    t, f2 = up.shape
    f = f2 // 2
    rb, cw = _tile(t, rb, 16), _tile(f, cw)

    def body(cur_ref, prev_ref, w_ref, b_ref, act_ref, conv_ref):
        first = pl.program_id(0) == 0

        def conv(cols):
            e = jnp.concatenate([jnp.where(first, 0.0, prev_ref[:, cols]), cur_ref[:, cols]], axis=0)
            w = w_ref[:, cols]
            out = (w[0:1] * _shift(e, 2) + w[1:2] * _shift(e, 1) + w[2:3] * e + b_ref[:, cols])[HALO:]
            conv_ref[:, cols] = out.astype(BF16)
            return out

        for c0 in range(0, f, cw):
            g = conv(slice(c0, c0 + cw))
            a = conv(slice(f + c0, f + c0 + cw))
            act_ref[:, c0:c0 + cw] = (g * _sigmoid(g) * a).astype(BF16)

    return pl.pallas_call(
        body, grid=(t // rb,),
        in_specs=[_row_spec(rb, f2), _prev_spec(rb, f2), _full_spec((3, f2)), _full_spec((1, f2))],
        out_specs=[_row_spec(rb, f), _row_spec(rb, f2)],
        out_shape=[jax.ShapeDtypeStruct((t, f), BF16), jax.ShapeDtypeStruct((t, f2), BF16)],
        compiler_params=_params(("parallel",)), name=name,
    )(up, up, wconv, bconv)


BF16_HALO = 16


def _ffn_mid_bwd(up, conv, dact, wconv, *, name, rb=128, cw=512):
    t, f2 = up.shape
    f = f2 // 2
    rb, cw = _tile(t, rb, 16), _tile(f, cw)

    def body(up_ref, conv_ref, convn_ref, da_ref, dan_ref, w_ref, o_ref, dwb_ref):
        i = pl.program_id(0)
        last = i == pl.num_programs(0) - 1

        @pl.when(i == 0)
        def _():
            dwb_ref[...] = jnp.zeros_like(dwb_ref)

        def rows(cols):
            return jnp.concatenate([conv_ref[:, cols].astype(F32), convn_ref[:, cols].astype(F32)[0:HALO]], axis=0)

        def back(dc, cols):
            w = w_ref[:, cols]
            dc1, dc2 = _shift(dc, -1)[:rb], _shift(dc, -2)[:rb]
            dc0 = dc[:rb]
            o_ref[:, cols] = (w[2:3] * dc0 + w[1:2] * dc1 + w[0:1] * dc2).astype(BF16)
            u = up_ref[:, cols]
            dwb_ref[0:1, cols] += jnp.sum(dc2 * u, axis=0, keepdims=True)
            dwb_ref[1:2, cols] += jnp.sum(dc1 * u, axis=0, keepdims=True)
            dwb_ref[2:3, cols] += jnp.sum(dc0 * u, axis=0, keepdims=True)
            dwb_ref[3:4, cols] += jnp.sum(dc0, axis=0, keepdims=True)

        for c0 in range(0, f, cw):
            gcols, acols = slice(c0, c0 + cw), slice(f + c0, f + c0 + cw)
            g, a = rows(gcols), rows(acols)
            da = jnp.concatenate([da_ref[:, gcols], jnp.where(last, 0.0, dan_ref[:, gcols])], axis=0)
            sg = _sigmoid(g)
            back(da * a * (sg * (1.0 + g * (1.0 - sg))), gcols)
            back(da * (g * sg), acols)

    return pl.pallas_call(
        body, grid=(t // rb,),
        in_specs=[_row_spec(rb, f2), _row_spec(rb, f2),
                  pl.BlockSpec((BF16_HALO, f2), lambda i: (jnp.minimum((i + 1) * (rb // BF16_HALO), t // BF16_HALO - 1), 0)),
                  _row_spec(rb, f), _next_spec(rb, f, t), _full_spec((3, f2))],
        out_specs=[_row_spec(rb, f2), _full_spec((8, f2))],
        out_shape=[jax.ShapeDtypeStruct((t, f2), BF16), jax.ShapeDtypeStruct((8, f2), F32)],
        compiler_params=_params(("arbitrary",)), name=name,
    )(up, conv, conv, dact, dact, wconv)


def _causal_mask():
    return lax.broadcasted_iota(jnp.int32, (CHUNK, CHUNK), 0) >= lax.broadcasted_iota(jnp.int32, (CHUNK, CHUNK), 1)


def _b_mid_fwd(zp, vnorm, ws, bs, *, name, rb=256):
    t, d2 = zp.shape
    d = d2 // 2
    c = d // GROUPS
    rb = _tile(t, rb, CHUNK)

    def body(zp_ref, gv_ref, ws_ref, bs_ref, ug_ref, vn_ref, gate_ref):
        v = _gelu(zp_ref[:, d:])
        rv = lax.rsqrt(jnp.mean(v * v, axis=-1, keepdims=True) + RMS_EPS)
        vn_ref[...] = ((v * rv) * gv_ref[...]).astype(BF16)
        mask = _causal_mask()
        for h in range(GROUPS):
            hc = slice(h * c, (h + 1) * c)
            wm = jnp.where(mask, ws_ref[h], 0.0).astype(BF16)
            bcol = jnp.broadcast_to(bs_ref[h:h + 1, :], (CHUNK, CHUNK)).T[:, 0:1]
            for n in range(rb // CHUNK):
                rows = slice(n * CHUNK, (n + 1) * CHUNK)
                gate_ref[rows, hc] = jnp.dot(wm, vn_ref[rows, hc], preferred_element_type=F32) + bcol
        ug_ref[...] = (_gelu(zp_ref[:, :d]) * gate_ref[...]).astype(BF16)

    return pl.pallas_call(
        body, grid=(t // rb,),
        in_specs=[_row_spec(rb, d2), _full_spec((1, d)), _full_spec((GROUPS, CHUNK, CHUNK)), _full_spec((GROUPS, CHUNK))],
        out_specs=_row_spec(rb, d), out_shape=jax.ShapeDtypeStruct((t, d), BF16),
        scratch_shapes=[pltpu.VMEM((rb, d), BF16), pltpu.VMEM((rb, d), F32)],
        compiler_params=_params(("parallel",)), name=name,
    )(zp, vnorm, ws, bs)


def _b_mid_bwd(zp, dug, vnorm, ws, bs, *, name, rb=256):
    t, d2 = zp.shape
    d = d2 // 2
    c = d // GROUPS
    rb = _tile(t, rb, CHUNK)

    def body(zp_ref, dug_ref, gv_ref, ws_ref, bs_ref, dzp_ref, dws_ref, dbs_ref, dgv_ref,
             vn_ref, gate_ref, dm_ref, dvn_ref, dbacc_ref):
        i = pl.program_id(0)

        @pl.when(i == 0)
        def _():
            dws_ref[...] = jnp.zeros_like(dws_ref)
            dgv_ref[...] = jnp.zeros_like(dgv_ref)
            dbacc_ref[...] = jnp.zeros_like(dbacc_ref)

        zu, zv = zp_ref[:, :d], zp_ref[:, d:]
        u, v = _gelu(zu), _gelu(zv)
        rv = lax.rsqrt(jnp.mean(v * v, axis=-1, keepdims=True) + RMS_EPS)
        vhat = v * rv
        gv = gv_ref[...]
        vn_ref[...] = (vhat * gv).astype(BF16)
        dug_v = dug_ref[...]
        dm = dug_v * u
        dm_ref[...] = dm.astype(BF16)
        mask = _causal_mask()
        for h in range(GROUPS):
            hc = slice(h * c, (h + 1) * c)
            wm = jnp.where(mask, ws_ref[h], 0.0)
            wm_b, wmt_b = wm.astype(BF16), wm.T.astype(BF16)
            bcol = jnp.broadcast_to(bs_ref[h:h + 1, :], (CHUNK, CHUNK)).T[:, 0:1]
            dws_h = jnp.zeros((CHUNK, CHUNK), F32)
            dbs_h = jnp.zeros((CHUNK, c), F32)
            for n in range(rb // CHUNK):
                rows = slice(n * CHUNK, (n + 1) * CHUNK)
                vn_c, dm_c = vn_ref[rows, hc], dm_ref[rows, hc]
                gate_ref[rows, hc] = jnp.dot(wm_b, vn_c, preferred_element_type=F32) + bcol
                dws_h += lax.dot_general(dm_c, vn_c, (((1,), (1,)), ((), ())), preferred_element_type=F32)
                dvn_ref[rows, hc] = jnp.dot(wmt_b, dm_c, preferred_element_type=F32)
                dbs_h += dm[rows, hc]
            dws_ref[h] += dws_h
            dbacc_ref[h] += dbs_h
        du = dug_v * gate_ref[...]
        dvn = dvn_ref[...]
        dvhat = dvn * gv
        m = jnp.mean(dvhat * vhat, axis=-1, keepdims=True)
        dv = rv * (dvhat - vhat * m)
        dgv_ref[0:1, :] += jnp.sum(dvn * vhat, axis=0, keepdims=True)
        dzp_ref[:, :d] = (du * _gelu_grad(zu)).astype(BF16)
        dzp_ref[:, d:] = (dv * _gelu_grad(zv)).astype(BF16)

        @pl.when(i == pl.num_programs(0) - 1)
        def _():
            ones = jnp.ones((8, c), F32)
            for h in range(GROUPS):
                dws_ref[h] = jnp.where(mask, dws_ref[h], 0.0)
                row = lax.dot_general(ones, dbacc_ref[h], (((1,), (1,)), ((), ())),
                                      precision=lax.Precision.HIGHEST, preferred_element_type=F32)
                dbs_ref[h:h + 1, :] = row[0:1]

    return pl.pallas_call(
        body, grid=(t // rb,),
        in_specs=[_row_spec(rb, d2), _row_spec(rb, d), _full_spec((1, d)), _full_spec((GROUPS, CHUNK, CHUNK)),
                  _full_spec((GROUPS, CHUNK))],
        out_specs=[_row_spec(rb, d2), _full_spec((GROUPS, CHUNK, CHUNK)), _full_spec((GROUPS, CHUNK)), _full_spec((8, d))],
        out_shape=[jax.ShapeDtypeStruct((t, d2), BF16), jax.ShapeDtypeStruct((GROUPS, CHUNK, CHUNK), F32),
                   jax.ShapeDtypeStruct((GROUPS, CHUNK), F32), jax.ShapeDtypeStruct((8, d), F32)],
        scratch_shapes=[pltpu.VMEM((rb, d), BF16), pltpu.VMEM((rb, d), F32), pltpu.VMEM((rb, d), BF16),
                        pltpu.VMEM((rb, d), F32), pltpu.VMEM((GROUPS, CHUNK, c), F32)],
        compiler_params=_params(("arbitrary",)), name=name,
    )(zp, dug, vnorm, ws, bs)


def _cast_into_full(w, layer, kind, place, *, name, dep=None, rb=256):
    _, r, c = w.shape
    rb = _tile(r, rb, 16)
    nrb = r // rb
    full = (r, c * N_CHIPS) if kind == "col" else (r * N_CHIPS, c)

    def body(place_ref, w_ref, *rest):
        rest[-1][...] = w_ref[...].astype(BF16)

    def o_index(i, place):
        return (i, place[0]) if kind == "col" else (i + place[0] * nrb, 0)

    in_specs = [pl.BlockSpec((None, rb, c), lambda i, place: (layer, i, 0))] + ([ANY] if dep is not None else [])
    return pl.pallas_call(
        body,
        grid_spec=pltpu.PrefetchScalarGridSpec(num_scalar_prefetch=1, grid=(nrb,), in_specs=in_specs,
                                               out_specs=pl.BlockSpec((rb, c), o_index)),
        out_shape=jax.ShapeDtypeStruct(full, BF16), compiler_params=_params(("parallel",)), name=name,
    )(place, w, *(() if dep is None else (dep,)))


def _adamw_layer(w, g, m, v, layer, prev, *, name, rb=128):
    _, r, c = w.shape
    rb = _tile(r, rb, 8)
    c1 = 1.0 - ADAM_B1 ** ADAM_STEP
    c2 = 1.0 - ADAM_B2 ** ADAM_STEP

    def body(w_ref, g_ref, m_ref, v_ref, *rest):
        go_ref, d_ref, nm_ref, nv_ref = rest[-4:]
        gv = g_ref[...]
        nm = ADAM_B1 * m_ref[...] + (1.0 - ADAM_B1) * gv
        nv = ADAM_B2 * v_ref[...] + (1.0 - ADAM_B2) * (gv * gv)
        go_ref[...] = gv
        nm_ref[...] = nm
        nv_ref[...] = nv
        d_ref[...] = -ADAM_LR * ((nm / c1) / (jnp.sqrt(nv / c2) + ADAM_EPS) + ADAM_WD * w_ref[...])

    lay = pl.BlockSpec((None, rb, c), lambda i: (layer, i, 0))
    return pl.pallas_call(
        body, grid=(r // rb,), in_specs=[lay, _row_spec(rb, c), lay, lay] + ([ANY] * 4 if prev else []), out_specs=[lay] * 4,
        out_shape=[jax.ShapeDtypeStruct(w.shape, F32)] * 4, input_output_aliases={4 + k: k for k in range(4)} if prev else {},
        compiler_params=_params(("parallel",)), name=name,
    )(w, g, m, v, *(prev or ()))


HBM = pl.BlockSpec(memory_space=pltpu.HBM)
SEM = pl.BlockSpec(memory_space=pltpu.SEMAPHORE)
SIDE_EFFECT = pltpu.SideEffectType.DATAFLOW_SIDE_EFFECTING


def _place():
    x, y, c = lax.axis_index("x"), lax.axis_index("y"), lax.axis_index("c")
    chips = [(1 - x, y), (x, 1 - y), (1 - x, 1 - y)]
    return x, y, c, 2 * x + y, chips


def _half(ref, kind, c):
    r, w = ref.shape
    if kind == "col":
        return ref.at[pl.ds(pl.multiple_of(c * (r // 2), 8), r // 2), :]
    return ref.at[:, pl.ds(pl.multiple_of(c * (w // 2), 128), w // 2)]


def _shard(ref, kind, s):
    r, w = ref.shape
    if kind == "col":
        return ref.at[:, pl.ds(pl.multiple_of(s * (w // N_CHIPS), 128), w // N_CHIPS)]
    return ref.at[pl.ds(pl.multiple_of(s * (r // N_CHIPS), 8), r // N_CHIPS), :]


def _remote(src, dst, send_sem, recv_sem, dev):
    return pltpu.make_async_remote_copy(src_ref=src, dst_ref=dst, send_sem=send_sem, recv_sem=recv_sem,
                                        device_id=dev, device_id_type=MESH)


def _start(name, bufs, plan, sem_shape, dep=None):
    n = len(bufs)
    n_in = n + (dep is not None)

    def body(*refs):
        sends, _ = plan(refs[:n], refs[n_in], refs[n_in + 1])
        for cp in sends:
            cp.start()
        refs[n_in + 2 + n][...] = jnp.zeros((8, 128), F32)

    dma = pltpu.SemaphoreType.DMA
    outs = pl.pallas_call(
        body, name=name,
        out_shape=(dma(sem_shape), dma(sem_shape), *[pltpu.HBM(b.shape, b.dtype) for b in bufs], jax.ShapeDtypeStruct((8, 128), F32)),
        in_specs=(HBM,) * n + ((ANY,) if dep is not None else ()),
        out_specs=(SEM, SEM) + (HBM,) * n + (pl.BlockSpec(memory_space=pltpu.VMEM),),
        input_output_aliases={i: i + 2 for i in range(n)},
        compiler_params=pltpu.CompilerParams(has_side_effects=SIDE_EFFECT),
    )(*[pltpu.with_memory_space_constraint(b, pltpu.HBM) for b in bufs], *(() if dep is None else (dep,)))
    return outs[0], outs[1], list(outs[2:2 + n]), outs[2 + n]


def _wait(name, started, plan, after):
    send, recv, bufs, _ = started
    n = len(bufs)

    def body(*refs):
        sends, recvs = plan(refs[:n], refs[n], refs[n + 1])
        for cp in sends:
            cp.wait_send()
        for cp in recvs:
            cp.wait_recv()

    return list(pl.pallas_call(
        body, name=name, out_shape=tuple(pltpu.HBM(b.shape, b.dtype) for b in bufs),
        in_specs=(HBM,) * n + (SEM, SEM, ANY), out_specs=(HBM,) * n, input_output_aliases={i: i for i in range(n)},
        compiler_params=pltpu.CompilerParams(has_side_effects=SIDE_EFFECT),
    )(*bufs, send, recv, after))


KINDS = ("col", "row")


def _gather_ici_plan(n_small):
    def plan(refs, send, recv):
        x, y, c, s, chips = _place()
        n = len(KINDS) + n_small
        sends, recvs = [], []
        for k, (px, py) in enumerate(chips):
            sp = 2 * px + py
            for a, kind in enumerate(KINDS):
                mine, theirs = _half(_shard(refs[a], kind, s), kind, c), _half(_shard(refs[a], kind, sp), kind, c)
                sends.append(_remote(mine, mine, send.at[k * n + a], recv.at[k * n + a], (px, py, c)))
                recvs.append(_remote(theirs, theirs, send.at[k * n + a], recv.at[k * n + a], (px, py, c)))
            for b in range(n_small):
                ref, sem = refs[len(KINDS) + b], k * n + len(KINDS) + b
                sends.append(_remote(ref.at[s], ref.at[s], send.at[sem], recv.at[sem], (px, py, c)))
                recvs.append(_remote(ref.at[sp], ref.at[sp], send.at[sem], recv.at[sem], (px, py, c)))
        return sends, recvs
    return plan


def _gather_d2d_plan(refs, send, recv):
    x, y, c, _, chips = _place()
    n = len(KINDS)
    sends, recvs = [], []
    for k, (px, py) in enumerate(chips):
        for a, kind in enumerate(KINDS):
            region, sem = _shard(refs[a], kind, 2 * px + py), k * n + a
            sends.append(_remote(_half(region, kind, c), _half(region, kind, c), send.at[sem], recv.at[sem], (x, y, 1 - c)))
            recvs.append(_remote(_half(region, kind, 1 - c), _half(region, kind, 1 - c), send.at[sem], recv.at[sem], (x, y, 1 - c)))
    return sends, recvs


def _swap_plan(refs, send, recv):
    x, y, c, _, _ = _place()
    n = len(KINDS)
    cps = [_remote(_half(refs[a], KINDS[a], 1 - c), refs[n + a], send.at[a], recv.at[a], (x, y, 1 - c)) for a in range(n)]
    return cps, cps


def _exchange_plan(refs, send, recv):
    x, y, c, _, chips = _place()
    n = len(KINDS)
    cps = []
    for k, (px, py) in enumerate(chips):
        for a in range(n):
            cps.append(_remote(_shard(refs[a], KINDS[a], 2 * px + py), refs[n + a].at[k], send.at[k * n + a], recv.at[k * n + a],
                               (px, py, c)))
    return cps, cps


def _share_plan(refs, send, recv):
    x, y, c, _, _ = _place()
    sends = [_remote(_half(refs[a], KINDS[a], c), _half(refs[a], KINDS[a], c), send.at[a], recv.at[a], (x, y, 1 - c))
             for a in range(len(KINDS))]
    recvs = [_remote(_half(refs[a], KINDS[a], 1 - c), _half(refs[a], KINDS[a], 1 - c), send.at[a], recv.at[a], (x, y, 1 - c))
             for a in range(len(KINDS))]
    return sends, recvs


def _spread_plan(refs, send, recv):
    packed, slots = refs
    x, y, c, _, _ = _place()
    sends, recvs = [], []
    for k in range(1, 8):
        px, py, pc = x ^ (k >> 2), y ^ ((k >> 1) & 1), c ^ (k & 1)
        sends.append(_remote(packed, slots.at[4 * x + 2 * y + c], send.at[k - 1], recv.at[k - 1], (px, py, pc)))
        recvs.append(_remote(packed, slots.at[4 * px + 2 * py + pc], send.at[k - 1], recv.at[k - 1], (px, py, pc)))
    return sends, recvs


def _half_index(kind, nblk):
    def index(i, j, place):
        return (i + place[1] * nblk[0], j) if kind == "col" else (i, j + place[1] * nblk[1])
    return index


def _chip_partial(g, other, kind, place, *, name):
    hr, hc = other.shape
    rb, cb = _tile(hr, 512, 16), _tile(hc, 1024)
    nblk = (hr // rb, hc // cb)

    def body(place_ref, g_ref, o_ref, p_ref):
        p_ref[...] = (g_ref[...].astype(F32) + o_ref[...].astype(F32)).astype(BF16)

    plain = pl.BlockSpec((rb, cb), lambda i, j, place: (i, j))
    return pl.pallas_call(
        body,
        grid_spec=pltpu.PrefetchScalarGridSpec(
            num_scalar_prefetch=1, grid=nblk, in_specs=[pl.BlockSpec((rb, cb), _half_index(kind, nblk)), plain], out_specs=plain),
        out_shape=jax.ShapeDtypeStruct((hr, hc), BF16), compiler_params=_params(("parallel", "parallel")), name=name,
    )(place, g, other)


def _reduce_half(g, other, recv, kind, place, *, name):
    _, pr, pc = recv.shape
    rb, cb = _tile(pr, 512, 16), _tile(pc, 1024)
    nblk = (pr // rb, pc // cb)
    full = (pr * 2, pc) if kind == "col" else (pr, pc * 2)

    def g_index(i, j, place):
        s, c = place[0], place[1]
        return (i + c * nblk[0], j + s * nblk[1]) if kind == "col" else (i + s * nblk[0], j + c * nblk[1])

    def o_index(i, j, place):
        return (i, j + place[0] * nblk[1]) if kind == "col" else (i + place[0] * nblk[0], j)

    def body(place_ref, g_ref, o_ref, r_ref, out_ref):
        acc = g_ref[...].astype(F32) + o_ref[...].astype(F32)
        for k in range(3):
            acc = acc + r_ref[k].astype(F32)
        out_ref[...] = acc

    return pl.pallas_call(
        body,
        grid_spec=pltpu.PrefetchScalarGridSpec(
            num_scalar_prefetch=1, grid=nblk,
            in_specs=[pl.BlockSpec((rb, cb), g_index), pl.BlockSpec((rb, cb), o_index),
                      pl.BlockSpec((3, rb, cb), lambda i, j, place: (0, i, j))],
            out_specs=pl.BlockSpec((rb, cb), _half_index(kind, nblk))),
        out_shape=jax.ShapeDtypeStruct(full, F32), compiler_params=_params(("parallel", "parallel")), name=name,
    )(place, g, other, recv)


def _sum_slots(packed, slots, me, *, name, rb=512):
    r, w = packed.shape
    rb = _tile(r, rb, 8)

    def body(me_ref, p_ref, s_ref, o_ref):
        acc = None
        for j in range(8):
            term = jnp.where(me_ref[0] == j, p_ref[...], s_ref[j])
            acc = term if acc is None else acc + term
        o_ref[...] = acc

    return pl.pallas_call(
        body,
        grid_spec=pltpu.PrefetchScalarGridSpec(
            num_scalar_prefetch=1, grid=(r // rb,),
            in_specs=[pl.BlockSpec((rb, w), lambda i, me: (i, 0)), pl.BlockSpec((8, rb, w), lambda i, me: (0, i, 0))],
            out_specs=pl.BlockSpec((rb, w), lambda i, me: (i, 0))),
        out_shape=jax.ShapeDtypeStruct((r, w), F32), compiler_params=_params(("parallel",)), name=name,
    )(me, packed, slots)


def _half_shape(a, kind):
    return (a.shape[0] // 2, a.shape[1]) if kind == "col" else (a.shape[0], a.shape[1] // 2)


def _rs_swap(tag, grads):
    others = [lax.empty(_half_shape(g, k), g.dtype) for g, k in zip(grads, KINDS)]
    return _start(f"rs_{tag}_swap", list(grads) + others, _swap_plan, (len(KINDS),))


def _rs_exchange(tag, swapped, place, after):
    bufs = _wait(f"rs_{tag}_swap_wait", swapped, _swap_plan, after)
    n = len(KINDS)
    grads, others = bufs[:n], bufs[n:]
    parts = [_chip_partial(g, o, k, place, name=f"rs_{tag}_partial_{k}") for g, o, k in zip(grads, others, KINDS)]
    lands = []
    for p, k in zip(parts, KINDS):
        piece = (p.shape[0], p.shape[1] // N_CHIPS) if k == "col" else (p.shape[0] // N_CHIPS, p.shape[1])
        lands.append(lax.empty((3,) + piece, p.dtype))
    return _start(f"rs_{tag}_exchange", parts + lands, _exchange_plan, (3 * n,)), grads, others


def _rs_share(tag, exchanged, place, after):
    started, grads, others = exchanged
    n = len(KINDS)
    recvs = _wait(f"rs_{tag}_exchange_wait", started, _exchange_plan, after)[n:]
    halves = [_reduce_half(g, o, r, k, place, name=f"rs_{tag}_reduce_{k}") for g, o, r, k in zip(grads, others, recvs, KINDS)]
    return _start(f"rs_{tag}_share", halves, _share_plan, (n,))


def _rs_finish(tag, shared, after):
    return _wait(f"rs_{tag}_share_wait", shared, _share_plan, after)


def _spread(tag, parts, dep):
    rows = [p.reshape(-1, 128) for p in parts]
    n = sum(r.shape[0] for r in rows)
    rows.append(jnp.zeros(((-n) % 512, 128), F32))
    packed = jnp.concatenate(rows, axis=0)
    return _start(f"small_{tag}_spread", [packed, lax.empty((8,) + packed.shape, F32)], _spread_plan, (7,), dep=dep)


def _spread_sum(tag, started, parts, me, after):
    packed, slots = _wait(f"small_{tag}_spread_wait", started, _spread_plan, after)
    total = _sum_slots(packed, slots, me, name=f"small_{tag}_sum")
    out, row = [], 0
    for p in parts:
        n = p.size // 128
        out.append(total[row:row + n].reshape(p.shape))
        row += n
    return out


def _ffn_fwd(x, h, w_up, conv_w, conv_b, w_down, tag):
    up = _mm(h, w_up, name=f"ffn{tag}_up")
    act, conv = _ffn_mid_fwd(up, conv_w, conv_b, name=f"ffn{tag}_mid")
    x_out = _mm(act, w_down, res=x, tk=2816, name=f"ffn{tag}_down")
    return x_out, (up, conv, act)


def kernel(x, a_norm, a_in, a_conv, a_out, b_norm, b_in, b_vnorm, b_ws, b_bs, b_out, f_norm, f_up, f_conv_w, f_conv_b, f_down, final_norm, loss_target, m_a_norm, m_a_in, m_a_conv, m_a_out, m_b_norm, m_b_in, m_b_vnorm, m_b_ws, m_b_bs, m_b_out, m_f_norm, m_f_up, m_f_conv_w, m_f_conv_b, m_f_down, m_final_norm, v_a_norm, v_a_in, v_a_conv, v_a_out, v_b_norm, v_b_in, v_b_vnorm, v_b_ws, v_b_bs, v_b_out, v_f_norm, v_f_up, v_f_conv_w, v_f_conv_b, v_f_down, v_final_norm):
    t, d = x.shape[1], x.shape[2]
    f2 = f_up.shape[2] * N_CHIPS
    x0, tgt = x.reshape(t, d), loss_target.reshape(t, d)
    ax, ay, ac = lax.axis_index("x"), lax.axis_index("y"), lax.axis_index("c")
    s = 2 * ax + ay
    place = jnp.stack([s, ac]).astype(jnp.int32)
    me = (4 * ax + 2 * ay + ac).astype(jnp.int32).reshape(1)

    def stacked(a):
        return lax.dynamic_update_index_in_dim(jnp.zeros((N_CHIPS,) + a.shape, F32), a, s, 0)

    def gather_start(tag, w_in, w_out, layer, small, dep):
        fulls = [_cast_into_full(w_in, layer, "col", place, name=f"cast_{tag}_in", dep=dep),
                 _cast_into_full(w_out, layer, "row", place, name=f"cast_{tag}_out", dep=dep)]
        return _start(f"ag_{tag}_ici", fulls + small, _gather_ici_plan(len(small)), (3 * (2 + len(small)),), dep=dep)

    def gather_forward(tag, started, n_small, after):
        bufs = _wait(f"ag_{tag}_ici_wait", started, _gather_ici_plan(n_small), after)
        return _start(f"ag_{tag}_d2d", bufs[:2], _gather_d2d_plan, (3 * 2,)), bufs[2:]

    def gather_finish(tag, forwarded, after):
        return _wait(f"ag_{tag}_d2d_wait", forwarded, _gather_d2d_plan, after)

    small = [stacked(a_conv[0]), stacked(b_norm), stacked(b_vnorm), stacked(f_conv_w.reshape(2 * 3, -1))]
    ag_a = gather_start("a", a_in, a_out, 0, small, None)
    ag_f0 = gather_start("f0", f_up, f_down, 0, [], ag_a[3])
    ag_b = gather_start("b", b_in, b_out, 0, [], ag_f0[3])
    ag_f1 = gather_start("f1", f_up, f_down, 1, [], ag_b[3])

    def unshard(a):
        return jnp.transpose(a, (1, 0, 2)).reshape(a.shape[1], -1)

    ws, bs = b_ws[0], b_bs[0]

    h0 = _rms_fwd(x0, a_norm, dep=ag_f1[3], name="a_norm")
    fw_a, (g_aconv, g_bnorm, g_bvnorm, g_fconv) = gather_forward("a", ag_a, 4, h0)
    w_ai, w_ao = gather_finish("a", fw_a, fw_a[3])
    a_conv_f, b_norm_f, b_vnorm_f = unshard(g_aconv), unshard(g_bnorm), unshard(g_bvnorm)
    f_conv_f = unshard(g_fconv).reshape(2, 3, f2)
    bcx = _mm(h0, w_ai, name="a_in")
    y = _a_mid_fwd(bcx, a_conv_f, name="a_mid")
    x1 = _mm(y, w_ao, res=x0, tm=512, tn=2048, name="a_out")
    fw_f0, _ = gather_forward("f0", ag_f0, 0, x1)
    h1 = _rms_fwd(x1, f_norm[0:1], dep=fw_f0[3], name="ffn0_norm")
    w_up0, w_dn0 = gather_finish("f0", fw_f0, h1)
    x2, (up0, conv0, act0) = _ffn_fwd(x1, h1, w_up0, f_conv_f[0], f_conv_b[0:1], w_dn0, 0)
    fw_b, _ = gather_forward("b", ag_b, 0, up0)
    w_bi, w_bo = gather_finish("b", fw_b, act0)
    h2 = _rms_fwd(x2, b_norm_f, name="b_norm")
    zp = _mm(h2, w_bi, name="b_in")
    fw_f1, _ = gather_forward("f1", ag_f1, 0, zp)
    ug = _b_mid_fwd(zp, b_vnorm_f, ws, bs, name="b_mid")
    x3, h3 = _mm(ug, w_bo, res=x2, norm=f_norm[1:2], tm=512, tn=2048, name="b_out")
    w_up1, w_dn1 = gather_finish("f1", fw_f1, x3)
    x4, (up1, conv1, act1) = _ffn_fwd(x3, h3, w_up1, f_conv_f[1], f_conv_b[1:2], w_dn1, 1)
    loss_rows, dx4, dx4b, d_final = _final(x4, tgt, final_norm.reshape(1, d), name="final")

    d_dn1 = _mm(act1, dx4b, ta=True, tm=1408, out_dtype=BF16, name="ffn1_ddown")
    dact1 = _mm(dx4b, w_dn1, tb=True, tn=512, tm=2048, name="ffn1_dact")
    dup1, d_fwb1 = _ffn_mid_bwd(up1, conv1, dact1, f_conv_f[1], name="ffn1_mid_bwd")
    d_up1 = _mm(h3, dup1, ta=True, out_dtype=BF16, name="ffn1_dup")
    sw_f1 = _rs_swap("f1", [d_up1, d_dn1])
    dh3 = _mm(dup1, w_up1, tb=True, tk=2816, dep=sw_f1[3], name="ffn1_dh")
    dx3, dx3b, d_fnorm1 = _rms_bwd(dh3, x3, f_norm[1:2], dx4, name="ffn1_norm_bwd")
    ex_f1 = _rs_exchange("f1", sw_f1, place, dx3)

    d_bo = _mm(ug, dx3b, ta=True, out_dtype=BF16, tk=4096, dep=ex_f1[0][3], name="b_dout")
    dug = _mm(dx3b, w_bo, tb=True, tm=512, tn=2048, name="b_dug")
    dzp, d_ws, d_bs, d_bvnorm = _b_mid_bwd(zp, dug, b_vnorm_f, ws, bs, name="b_mid_bwd")
    d_bi = _mm(h2, dzp, ta=True, out_dtype=BF16, tk=4096, name="b_din")
    sw_b = _rs_swap("b", [d_bi, d_bo])
    dh2 = _mm(dzp, w_bi, tb=True, tk=4096, dep=sw_b[3], name="b_dh")
    dx2, dx2b, d_bnorm = _rms_bwd(dh2, x2, b_norm_f, dx3, name="b_norm_bwd")
    ex_b = _rs_exchange("b", sw_b, place, dx2)

    d_dn0 = _mm(act0, dx2b, ta=True, tm=1408, out_dtype=BF16, dep=ex_b[0][3], name="ffn0_ddown")
    dact0 = _mm(dx2b, w_dn0, tb=True, tn=512, tm=2048, name="ffn0_dact")
    dup0, d_fwb0 = _ffn_mid_bwd(up0, conv0, dact0, f_conv_f[0], name="ffn0_mid_bwd")
    sh_f1 = _rs_share("f1", ex_f1, place, dup0)
    d_up0 = _mm(h1, dup0, ta=True, out_dtype=BF16, tk=4096, dep=sh_f1[3], name="ffn0_dup")
    sw_f0 = _rs_swap("f0", [d_up0, d_dn0])
    g_up1, g_dn1 = _rs_finish("f1", sh_f1, sw_f0[3])
    dh1 = _mm(dup0, w_up0, tb=True, tk=2816, dep=sw_f0[3], name="ffn0_dh")
    dx1, dx1b, d_fnorm0 = _rms_bwd(dh1, x1, f_norm[0:1], dx2, name="ffn0_norm_bwd")
    ex_f0 = _rs_exchange("f0", sw_f0, place, dx1)
    early = [jnp.concatenate([d_bnorm, d_bvnorm, d_fnorm0, d_fnorm1, d_final, loss_rows], axis=0),
             jnp.concatenate([d_fwb0, d_fwb1], axis=0), jnp.concatenate([d_ws.reshape(-1, CHUNK), d_bs], axis=0)]
    sp_early = _spread("early", early, ex_f0[0][3])
    sh_b = _rs_share("b", ex_b, place, sp_early[3])

    d_ao = _mm(y, dx1b, ta=True, out_dtype=BF16, tk=4096, dep=sh_b[3], name="a_dout")
    dyy = _mm(dx1b, w_ao, tb=True, tm=512, tn=2048, name="a_dy")
    dbcx, d_aconv = _a_mid_bwd(bcx, dyy, a_conv_f, name="a_mid_bwd")
    d_ai = _mm(h0, dbcx, ta=True, out_dtype=BF16, tk=4096, name="a_din")
    sw_a = _rs_swap("a", [d_ai, d_ao])
    g_bi, g_bo = _rs_finish("b", sh_b, sw_a[3])
    ex_a = _rs_exchange("a", sw_a, place, g_bi)
    dh0 = _mm(dbcx, w_ai, tb=True, tk=3072, dep=ex_a[0][3], name="a_dh")
    grad_x, _, d_anorm = _rms_bwd(dh0, x0, a_norm, dx1, name="a_norm_bwd")
    late = [jnp.concatenate([d_anorm, d_aconv], axis=0)]
    sp_late = _spread("late", late, ex_a[0][3])
    sh_f0 = _rs_share("f0", ex_f0, place, sp_late[3])
    sh_a = _rs_share("a", ex_a, place, sh_f0[3])
    g_up0, g_dn0 = _rs_finish("f0", sh_f0, sh_a[3])
    g_ai, g_ao = _rs_finish("a", sh_a, g_up0)
    r_a, r_b, r_c = _spread_sum("early", sp_early, early, me, g_ai)
    (r_l,) = _spread_sum("late", sp_late, late, me, r_a)

    loss = jnp.sum(r_a[40])
    cs, fs = d // N_CHIPS, f2 // N_CHIPS

    def mine(a, width):
        return lax.dynamic_slice_in_dim(a, s * width, width, axis=1)

    grads = {
        "a_norm": r_l[0:1], "a_conv": mine(r_l[8:11], cs), "b_norm": mine(r_a[0:1], cs), "b_vnorm": mine(r_a[8:9], cs),
        "f_norm": jnp.concatenate([r_a[16:17], r_a[24:25]], axis=0), "final_norm": r_a[32:33],
        "b_ws": r_c[:GROUPS * CHUNK], "b_bs": r_c[GROUPS * CHUNK:],
        "f_conv_w": jnp.concatenate([mine(r_b[0:3], fs), mine(r_b[8:11], fs)], axis=0),
        "f_conv_b": jnp.concatenate([r_b[3:4], r_b[11:12]], axis=0),
        "a_in": g_ai, "a_out": g_ao, "b_in": g_bi, "b_out": g_bo,
    }
    names = ["a_norm", "a_in", "a_conv", "a_out", "b_norm", "b_in", "b_vnorm", "b_ws", "b_bs", "b_out", "f_norm", "f_up",
             "f_conv_w", "f_conv_b", "f_down", "final_norm"]
    weights = dict(zip(names, [a_norm, a_in, a_conv, a_out, b_norm, b_in, b_vnorm, b_ws, b_bs, b_out, f_norm, f_up, f_conv_w,
                               f_conv_b, f_down, final_norm]))
    ms = dict(zip(names, [m_a_norm, m_a_in, m_a_conv, m_a_out, m_b_norm, m_b_in, m_b_vnorm, m_b_ws, m_b_bs, m_b_out, m_f_norm,
                          m_f_up, m_f_conv_w, m_f_conv_b, m_f_down, m_final_norm]))
    vs = dict(zip(names, [v_a_norm, v_a_in, v_a_conv, v_a_out, v_b_norm, v_b_in, v_b_vnorm, v_b_ws, v_b_bs, v_b_out, v_f_norm,
                          v_f_up, v_f_conv_w, v_f_conv_b, v_f_down, v_final_norm]))
    result = {}
    for n in names:
        w = weights[n]
        if n in ("f_up", "f_down"):
            g1, g0 = (g_up1, g_up0) if n == "f_up" else (g_dn1, g_dn0)
            first = _adamw_layer(w, g1, ms[n], vs[n], 1, None, name=f"adamw_{n}1")
            result[n] = _adamw_layer(w, g0, ms[n], vs[n], 0, tuple(first), name=f"adamw_{n}0")
            continue
        g2 = grads[n]
        as3d = (lambda a: a.reshape((1,) + g2.shape))
        result[n] = [o.reshape(w.shape) for o in _adamw_layer(as3d(w), g2, as3d(ms[n]), as3d(vs[n]), 0, None, name=f"adamw_{n}")]

    return (loss, grad_x.reshape(x.shape), *[result[n][0] for n in names], *[result[n][1] for n in names],
            *[result[n][2] for n in names], *[result[n][3] for n in names])
```

```python
import functools

import jax
import jax.numpy as jnp
from jax import lax
from jax.experimental import pallas as pl
from jax.experimental.pallas import tpu as pltpu

F32 = jnp.float32
BF16 = jnp.bfloat16
MESH = pl.DeviceIdType.MESH
ANY = pl.BlockSpec(memory_space=pl.ANY)

RMS_EPS = 1e-5
CHUNK = 128
GROUPS = 8
ADAM_LR, ADAM_B1, ADAM_B2, ADAM_EPS, ADAM_WD, ADAM_STEP = 0.001, 0.9, 0.999, 1e-08, 0.01, 10

N_CHIPS = 4
HALO = 8
VMEM_LIMIT = 56 * 1024 * 1024
GELU_C = 0.7978845608028654
GELU_A = 0.044715


def _params(sem=None):
    return pltpu.CompilerParams(dimension_semantics=sem, vmem_limit_bytes=VMEM_LIMIT)


def _tile(dim, pref, quantum=128):
    if dim <= pref:
        return dim
    t = (pref // quantum) * quantum
    while t >= quantum:
        if dim % t == 0:
            return t
        t -= quantum
    return dim


def _mm(a, b, *, name, ta=False, tb=False, res=None, norm=None, dep=None, out_dtype=F32, tm=1024, tn=1024, tk=2048):
    (K, M) = a.shape if ta else a.shape[::-1]
    N = b.shape[0] if tb else b.shape[1]
    assert (b.shape[1] if tb else b.shape[0]) == K
    tm, tn, tk = _tile(M, tm), _tile(N, tn), _tile(K, tk)
    nk = K // tk
    assert norm is None or (tn == N and nk == 1)
    a_spec = pl.BlockSpec((tk, tm), lambda i, j, k: (k, i)) if ta else pl.BlockSpec((tm, tk), lambda i, j, k: (i, k))
    b_spec = pl.BlockSpec((tn, tk), lambda i, j, k: (j, k)) if tb else pl.BlockSpec((tk, tn), lambda i, j, k: (k, j))
    o_spec = pl.BlockSpec((tm, tn), lambda i, j, k: (i, j))
    dims = (((0 if ta else 1,), (1 if tb else 0,)), ((), ()))
    direct = out_dtype == F32
    n_in = 2 + (res is not None) + (norm is not None) + (dep is not None)

    def body(*refs):
        a_ref, b_ref = refs[0], refs[1]
        r_ref = refs[2] if res is not None else None
        g_ref = refs[2 + (res is not None)] if norm is not None else None
        o_ref = refs[n_in]
        acc_ref = o_ref if direct else refs[-1]
        part = lax.dot_general(a_ref[...], b_ref[...], dims, preferred_element_type=F32)
        if nk == 1:
            if r_ref is not None:
                part = part + r_ref[...]
            o_ref[...] = part.astype(o_ref.dtype)
            if g_ref is not None:
                r = lax.rsqrt(jnp.mean(part * part, axis=-1, keepdims=True) + RMS_EPS)
                refs[n_in + 1][...] = ((part * r) * g_ref[...]).astype(BF16)
            return
        k = pl.program_id(2)

        @pl.when(k == 0)
        def _():
            acc_ref[...] = part

        @pl.when(jnp.logical_and(k > 0, k < nk - 1))
        def _():
            acc_ref[...] += part

        @pl.when(k == nk - 1)
        def _():
            tot = acc_ref[...] + part
            if r_ref is not None:
                tot = tot + r_ref[...]
            o_ref[...] = tot.astype(o_ref.dtype)

    in_specs = ([a_spec, b_spec] + ([o_spec] if res is not None else [])
                + ([pl.BlockSpec((1, tn), lambda i, j, k: (0, j))] if norm is not None else []) + ([ANY] if dep is not None else []))
    args = (a, b) + tuple(x for x in (res, norm, dep) if x is not None)
    scratch = [] if (direct or nk == 1) else [pltpu.VMEM((tm, tn), F32)]
    out_shape = jax.ShapeDtypeStruct((M, N), out_dtype)
    return pl.pallas_call(
        body, grid=(M // tm, N // tn, nk), in_specs=in_specs, out_specs=[o_spec, o_spec] if norm is not None else o_spec,
        out_shape=[out_shape, jax.ShapeDtypeStruct((M, N), BF16)] if norm is not None else out_shape, scratch_shapes=scratch,
        compiler_params=_params(("parallel", "parallel", "arbitrary")), name=name,
    )(*args)


def _row_spec(rb, w):
    return pl.BlockSpec((rb, w), lambda i: (i, 0))


def _prev_spec(rb, w):
    return pl.BlockSpec((HALO, w), lambda i: (jnp.maximum(i * (rb // HALO) - 1, 0), 0))


def _next_spec(rb, w, t):
    return pl.BlockSpec((HALO, w), lambda i: (jnp.minimum((i + 1) * (rb // HALO), t // HALO - 1), 0))


def _full_spec(shape):
    return pl.BlockSpec(shape, lambda i: tuple(0 for _ in shape))


def _shift(e, s):
    return pltpu.roll(e, s % e.shape[0], 0)


def _gelu(x):
    return 0.5 * x * (1.0 + jnp.tanh(GELU_C * (x + GELU_A * x * x * x)))


def _gelu_grad(x):
    th = jnp.tanh(GELU_C * (x + GELU_A * x * x * x))
    return 0.5 * (1.0 + th) + 0.5 * x * (1.0 - th * th) * (GELU_C * (1.0 + 3.0 * GELU_A * x * x))


def _sigmoid(x):
    return 1.0 / (1.0 + jnp.exp(-x))


def _rms_fwd(x, g, *, name, dep=None, rb=256):
    t, d = x.shape
    rb = _tile(t, rb, 8)

    def body(x_ref, g_ref, *rest):
        h_ref = rest[-1]
        xv = x_ref[...]
        r = lax.rsqrt(jnp.mean(xv * xv, axis=-1, keepdims=True) + RMS_EPS)
        h_ref[...] = ((xv * r) * g_ref[...]).astype(BF16)

    return pl.pallas_call(
        body, grid=(t // rb,), in_specs=[_row_spec(rb, d), _full_spec((1, d))] + ([ANY] if dep is not None else []),
        out_specs=_row_spec(rb, d), out_shape=jax.ShapeDtypeStruct((t, d), BF16), compiler_params=_params(("parallel",)), name=name,
    )(x, g, *(() if dep is None else (dep,)))


def _rms_bwd(dh, x, g, dres, *, name, dep=None, rb=256):
    t, d = x.shape
    rb = _tile(t, rb, 8)

    def body(dh_ref, x_ref, g_ref, dres_ref, *rest):
        dx_ref, dxb_ref, dg_ref = rest[-3:]
        xv = x_ref[...]
        r = lax.rsqrt(jnp.mean(xv * xv, axis=-1, keepdims=True) + RMS_EPS)
        xhat = xv * r
        dh_v = dh_ref[...]
        dxhat = dh_v * g_ref[...]
        m = jnp.mean(dxhat * xhat, axis=-1, keepdims=True)
        dx = dres_ref[...] + r * (dxhat - xhat * m)
        dx_ref[...] = dx
        dxb_ref[...] = dx.astype(BF16)

        @pl.when(pl.program_id(0) == 0)
        def _():
            dg_ref[...] = jnp.zeros_like(dg_ref)

        dg_ref[0:1, :] += jnp.sum(dh_v * xhat, axis=0, keepdims=True)

    return pl.pallas_call(
        body, grid=(t // rb,),
        in_specs=[_row_spec(rb, d), _row_spec(rb, d), _full_spec((1, d)), _row_spec(rb, d)] + ([ANY] if dep is not None else []),
        out_specs=[_row_spec(rb, d), _row_spec(rb, d), _full_spec((8, d))],
        out_shape=[jax.ShapeDtypeStruct((t, d), F32), jax.ShapeDtypeStruct((t, d), BF16), jax.ShapeDtypeStruct((8, d), F32)],
        compiler_params=_params(("arbitrary",)), name=name,
    )(dh, x, g, dres, *(() if dep is None else (dep,)))


def _final(x, tgt, g, *, name, rb=256):
    t, d = x.shape
    rb = _tile(t, rb, 8)
    inv_d = 1.0 / d

    def body(x_ref, t_ref, g_ref, l_ref, dx_ref, dxb_ref, dg_ref):
        xv = x_ref[...]
        gv = g_ref[...]
        r = lax.rsqrt(jnp.mean(xv * xv, axis=-1, keepdims=True) + RMS_EPS)
        xhat = xv * r
        e = xhat * gv - t_ref[...]
        dy = e * inv_d
        dxhat = dy * gv
        m = jnp.mean(dxhat * xhat, axis=-1, keepdims=True)
        dx = r * (dxhat - xhat * m)
        dx_ref[...] = dx
        dxb_ref[...] = dx.astype(BF16)

        @pl.when(pl.program_id(0) == 0)
        def _():
            l_ref[...] = jnp.zeros_like(l_ref)
            dg_ref[...] = jnp.zeros_like(dg_ref)

        l_ref[0:1, :] += jnp.sum(e * e, axis=0, keepdims=True) * (0.5 * inv_d)
        dg_ref[0:1, :] += jnp.sum(dy * xhat, axis=0, keepdims=True)

    return pl.pallas_call(
        body, grid=(t // rb,),
        in_specs=[_row_spec(rb, d), _row_spec(rb, d), _full_spec((1, d))],
        out_specs=[_full_spec((8, d)), _row_spec(rb, d), _row_spec(rb, d), _full_spec((8, d))],
        out_shape=[jax.ShapeDtypeStruct((8, d), F32), jax.ShapeDtypeStruct((t, d), F32),
                   jax.ShapeDtypeStruct((t, d), BF16), jax.ShapeDtypeStruct((8, d), F32)],
        compiler_params=_params(("arbitrary",)), name=name,
    )(x, tgt, g)


def _a_mid_fwd(bcx, wconv, *, name, rb=256, cw=512):
    t, d3 = bcx.shape
    d = d3 // 3
    rb, cw = _tile(t, rb, 8), _tile(d, cw)

    def body(cur_ref, prev_ref, w_ref, y_ref):
        first = pl.program_id(0) == 0
        for c0 in range(0, d, cw):
            cs = slice(c0, c0 + cw)
            gc, xs = slice(d + c0, d + c0 + cw), slice(2 * d + c0, 2 * d + c0 + cw)
            p_prev = jnp.where(first, 0.0, prev_ref[:, gc] * prev_ref[:, xs])
            e = jnp.concatenate([p_prev, cur_ref[:, gc] * cur_ref[:, xs]], axis=0)
            w = w_ref[:, cs]
            q = w[0:1] * _shift(e, 2) + w[1:2] * _shift(e, 1) + w[2:3] * e
            y_ref[:, cs] = (cur_ref[:, cs] * q[HALO:]).astype(BF16)

    return pl.pallas_call(
        body, grid=(t // rb,),
        in_specs=[_row_spec(rb, d3), _prev_spec(rb, d3), _full_spec((3, d))], out_specs=_row_spec(rb, d),
        out_shape=jax.ShapeDtypeStruct((t, d), BF16), compiler_params=_params(("parallel",)), name=name,
    )(bcx, bcx, wconv)


def _a_mid_bwd(bcx, dy, wconv, *, name, rb=128, cw=512):
    t, d3 = bcx.shape
    d = d3 // 3
    rb, cw = _tile(t, rb, 8), _tile(d, cw)

    def body(cur_ref, prev_ref, next_ref, dy_ref, dyn_ref, w_ref, o_ref, dw_ref):
        i = pl.program_id(0)
        first, last = i == 0, i == pl.num_programs(0) - 1

        @pl.when(first)
        def _():
            dw_ref[...] = jnp.zeros_like(dw_ref)

        for c0 in range(0, d, cw):
            cs = slice(c0, c0 + cw)
            gc, xs = slice(d + c0, d + c0 + cw), slice(2 * d + c0, 2 * d + c0 + cw)
            zeros = jnp.zeros((HALO, cw), F32)
            gb_c, gc_c, xs_c = cur_ref[:, cs], cur_ref[:, gc], cur_ref[:, xs]
            p_prev = jnp.where(first, 0.0, prev_ref[:, gc] * prev_ref[:, xs])
            e = jnp.concatenate([p_prev, gc_c * xs_c, zeros], axis=0)
            dq_next = jnp.where(last, 0.0, dyn_ref[:, cs] * next_ref[:, cs])
            dy_c = dy_ref[:, cs]
            dq = jnp.concatenate([zeros, dy_c * gb_c, dq_next], axis=0)
            w = w_ref[:, cs]
            e1, e2 = _shift(e, 1), _shift(e, 2)
            q = w[0:1] * e2 + w[1:2] * e1 + w[2:3] * e
            dp = (w[2:3] * dq + w[1:2] * _shift(dq, -1) + w[0:1] * _shift(dq, -2))[HALO:HALO + rb]
            o_ref[:, cs] = (dy_c * q[HALO:HALO + rb]).astype(BF16)
            o_ref[:, gc] = (dp * xs_c).astype(BF16)
            o_ref[:, xs] = (dp * gc_c).astype(BF16)
            dq_c = dq[HALO:HALO + rb]
            dw_ref[0:1, cs] += jnp.sum(dq_c * e2[HALO:HALO + rb], axis=0, keepdims=True)
            dw_ref[1:2, cs] += jnp.sum(dq_c * e1[HALO:HALO + rb], axis=0, keepdims=True)
            dw_ref[2:3, cs] += jnp.sum(dq_c * e[HALO:HALO + rb], axis=0, keepdims=True)

    return pl.pallas_call(
        body, grid=(t // rb,),
        in_specs=[_row_spec(rb, d3), _prev_spec(rb, d3), _next_spec(rb, d3, t), _row_spec(rb, d), _next_spec(rb, d, t),
                  _full_spec((3, d))],
        out_specs=[_row_spec(rb, d3), _full_spec((8, d))],
        out_shape=[jax.ShapeDtypeStruct((t, d3), BF16), jax.ShapeDtypeStruct((8, d), F32)],
        compiler_params=_params(("arbitrary",)), name=name,
    )(bcx, bcx, bcx, dy, dy, wconv)


def _ffn_mid_fwd(up, wconv, bconv, *, name, rb=128, cw=512):
    t, f2 = up.shape
    f = f2 // 2
    rb, cw = _tile(t, rb, 16), _tile(f, cw)

    def body(cur_ref, prev_ref, w_ref, b_ref, act_ref, conv_ref):
        first = pl.program_id(0) == 0

        def conv(cols):
            e = jnp.concatenate([jnp.where(first, 0.0, prev_ref[:, cols]), cur_ref[:, cols]], axis=0)
            w = w_ref[:, cols]
            out = (w[0:1] * _shift(e, 2) + w[1:2] * _shift(e, 1) + w[2:3] * e + b_ref[:, cols])[HALO:]
            conv_ref[:, cols] = out.astype(BF16)
            return out

        for c0 in range(0, f, cw):
            g = conv(slice(c0, c0 + cw))
            a = conv(slice(f + c0, f + c0 + cw))
            act_ref[:, c0:c0 + cw] = (g * _sigmoid(g) * a).astype(BF16)

    return pl.pallas_call(
        body, grid=(t // rb,),
        in_specs=[_row_spec(rb, f2), _prev_spec(rb, f2), _full_spec((3, f2)), _full_spec((1, f2))],
        out_specs=[_row_spec(rb, f), _row_spec(rb, f2)],
        out_shape=[jax.ShapeDtypeStruct((t, f), BF16), jax.ShapeDtypeStruct((t, f2), BF16)],
        compiler_params=_params(("parallel",)), name=name,
    )(up, up, wconv, bconv)


BF16_HALO = 16


def _ffn_mid_bwd(up, conv, dact, wconv, *, name, rb=128, cw=512):
    t, f2 = up.shape
    f = f2 // 2
    rb, cw = _tile(t, rb, 16), _tile(f, cw)

    def body(up_ref, conv_ref, convn_ref, da_ref, dan_ref, w_ref, o_ref, dwb_ref):
        i = pl.program_id(0)
        last = i == pl.num_programs(0) - 1

        @pl.when(i == 0)
        def _():
            dwb_ref[...] = jnp.zeros_like(dwb_ref)

        def rows(cols):
            return jnp.concatenate([conv_ref[:, cols].astype(F32), convn_ref[:, cols].astype(F32)[0:HALO]], axis=0)

        def back(dc, cols):
            w = w_ref[:, cols]
            dc1, dc2 = _shift(dc, -1)[:rb], _shift(dc, -2)[:rb]
            dc0 = dc[:rb]
            o_ref[:, cols] = (w[2:3] * dc0 + w[1:2] * dc1 + w[0:1] * dc2).astype(BF16)
            u = up_ref[:, cols]
            dwb_ref[0:1, cols] += jnp.sum(dc2 * u, axis=0, keepdims=True)
            dwb_ref[1:2, cols] += jnp.sum(dc1 * u, axis=0, keepdims=True)
            dwb_ref[2:3, cols] += jnp.sum(dc0 * u, axis=0, keepdims=True)
            dwb_ref[3:4, cols] += jnp.sum(dc0, axis=0, keepdims=True)

        for c0 in range(0, f, cw):
            gcols, acols = slice(c0, c0 + cw), slice(f + c0, f + c0 + cw)
            g, a = rows(gcols), rows(acols)
            da = jnp.concatenate([da_ref[:, gcols], jnp.where(last, 0.0, dan_ref[:, gcols])], axis=0)
            sg = _sigmoid(g)
            back(da * a * (sg * (1.0 + g * (1.0 - sg))), gcols)
            back(da * (g * sg), acols)

    return pl.pallas_call(
        body, grid=(t // rb,),
        in_specs=[_row_spec(rb, f2), _row_spec(rb, f2),
                  pl.BlockSpec((BF16_HALO, f2), lambda i: (jnp.minimum((i + 1) * (rb // BF16_HALO), t // BF16_HALO - 1), 0)),
                  _row_spec(rb, f), _next_spec(rb, f, t), _full_spec((3, f2))],
        out_specs=[_row_spec(rb, f2), _full_spec((8, f2))],
        out_shape=[jax.ShapeDtypeStruct((t, f2), BF16), jax.ShapeDtypeStruct((8, f2), F32)],
        compiler_params=_params(("arbitrary",)), name=name,
    )(up, conv, conv, dact, dact, wconv)


def _causal_mask():
    return lax.broadcasted_iota(jnp.int32, (CHUNK, CHUNK), 0) >= lax.broadcasted_iota(jnp.int32, (CHUNK, CHUNK), 1)


def _b_mid_fwd(zp, vnorm, ws, bs, *, name, rb=256):
    t, d2 = zp.shape
    d = d2 // 2
    c = d // GROUPS
    rb = _tile(t, rb, CHUNK)

    def body(zp_ref, gv_ref, ws_ref, bs_ref, ug_ref, vn_ref, gate_ref):
        v = _gelu(zp_ref[:, d:])
        rv = lax.rsqrt(jnp.mean(v * v, axis=-1, keepdims=True) + RMS_EPS)
        vn_ref[...] = ((v * rv) * gv_ref[...]).astype(BF16)
        mask = _causal_mask()
        for h in range(GROUPS):
            hc = slice(h * c, (h + 1) * c)
            wm = jnp.where(mask, ws_ref[h], 0.0).astype(BF16)
            bcol = jnp.broadcast_to(bs_ref[h:h + 1, :], (CHUNK, CHUNK)).T[:, 0:1]
            for n in range(rb // CHUNK):
                rows = slice(n * CHUNK, (n + 1) * CHUNK)
                gate_ref[rows, hc] = jnp.dot(wm, vn_ref[rows, hc], preferred_element_type=F32) + bcol
        ug_ref[...] = (_gelu(zp_ref[:, :d]) * gate_ref[...]).astype(BF16)

    return pl.pallas_call(
        body, grid=(t // rb,),
        in_specs=[_row_spec(rb, d2), _full_spec((1, d)), _full_spec((GROUPS, CHUNK, CHUNK)), _full_spec((GROUPS, CHUNK))],
        out_specs=_row_spec(rb, d), out_shape=jax.ShapeDtypeStruct((t, d), BF16),
        scratch_shapes=[pltpu.VMEM((rb, d), BF16), pltpu.VMEM((rb, d), F32)],
        compiler_params=_params(("parallel",)), name=name,
    )(zp, vnorm, ws, bs)


def _b_mid_bwd(zp, dug, vnorm, ws, bs, *, name, rb=256):
    t, d2 = zp.shape
    d = d2 // 2
    c = d // GROUPS
    rb = _tile(t, rb, CHUNK)

    def body(zp_ref, dug_ref, gv_ref, ws_ref, bs_ref, dzp_ref, dws_ref, dbs_ref, dgv_ref,
             vn_ref, gate_ref, dm_ref, dvn_ref, dbacc_ref):
        i = pl.program_id(0)

        @pl.when(i == 0)
        def _():
            dws_ref[...] = jnp.zeros_like(dws_ref)
            dgv_ref[...] = jnp.zeros_like(dgv_ref)
            dbacc_ref[...] = jnp.zeros_like(dbacc_ref)

        zu, zv = zp_ref[:, :d], zp_ref[:, d:]
        u, v = _gelu(zu), _gelu(zv)
        rv = lax.rsqrt(jnp.mean(v * v, axis=-1, keepdims=True) + RMS_EPS)
        vhat = v * rv
        gv = gv_ref[...]
        vn_ref[...] = (vhat * gv).astype(BF16)
        dug_v = dug_ref[...]
        dm = dug_v * u
        dm_ref[...] = dm.astype(BF16)
        mask = _causal_mask()
        for h in range(GROUPS):
            hc = slice(h * c, (h + 1) * c)
            wm = jnp.where(mask, ws_ref[h], 0.0)
            wm_b, wmt_b = wm.astype(BF16), wm.T.astype(BF16)
            bcol = jnp.broadcast_to(bs_ref[h:h + 1, :], (CHUNK, CHUNK)).T[:, 0:1]
            dws_h = jnp.zeros((CHUNK, CHUNK), F32)
            dbs_h = jnp.zeros((CHUNK, c), F32)
            for n in range(rb // CHUNK):
                rows = slice(n * CHUNK, (n + 1) * CHUNK)
                vn_c, dm_c = vn_ref[rows, hc], dm_ref[rows, hc]
                gate_ref[rows, hc] = jnp.dot(wm_b, vn_c, preferred_element_type=F32) + bcol
                dws_h += lax.dot_general(dm_c, vn_c, (((1,), (1,)), ((), ())), preferred_element_type=F32)
                dvn_ref[rows, hc] = jnp.dot(wmt_b, dm_c, preferred_element_type=F32)
                dbs_h += dm[rows, hc]
            dws_ref[h] += dws_h
            dbacc_ref[h] += dbs_h
        du = dug_v * gate_ref[...]
        dvn = dvn_ref[...]
        dvhat = dvn * gv
        m = jnp.mean(dvhat * vhat, axis=-1, keepdims=True)
        dv = rv * (dvhat - vhat * m)
        dgv_ref[0:1, :] += jnp.sum(dvn * vhat, axis=0, keepdims=True)
        dzp_ref[:, :d] = (du * _gelu_grad(zu)).astype(BF16)
        dzp_ref[:, d:] = (dv * _gelu_grad(zv)).astype(BF16)

        @pl.when(i == pl.num_programs(0) - 1)
        def _():
            ones = jnp.ones((8, c), F32)
            for h in range(GROUPS):
                dws_ref[h] = jnp.where(mask, dws_ref[h], 0.0)
                row = lax.dot_general(ones, dbacc_ref[h], (((1,), (1,)), ((), ())),
                                      precision=lax.Precision.HIGHEST, preferred_element_type=F32)
                dbs_ref[h:h + 1, :] = row[0:1]

    return pl.pallas_call(
        body, grid=(t // rb,),
        in_specs=[_row_spec(rb, d2), _row_spec(rb, d), _full_spec((1, d)), _full_spec((GROUPS, CHUNK, CHUNK)),
                  _full_spec((GROUPS, CHUNK))],
        out_specs=[_row_spec(rb, d2), _full_spec((GROUPS, CHUNK, CHUNK)), _full_spec((GROUPS, CHUNK)), _full_spec((8, d))],
        out_shape=[jax.ShapeDtypeStruct((t, d2), BF16), jax.ShapeDtypeStruct((GROUPS, CHUNK, CHUNK), F32),
                   jax.ShapeDtypeStruct((GROUPS, CHUNK), F32), jax.ShapeDtypeStruct((8, d), F32)],
        scratch_shapes=[pltpu.VMEM((rb, d), BF16), pltpu.VMEM((rb, d), F32), pltpu.VMEM((rb, d), BF16),
                        pltpu.VMEM((rb, d), F32), pltpu.VMEM((GROUPS, CHUNK, c), F32)],
        compiler_params=_params(("arbitrary",)), name=name,
    )(zp, dug, vnorm, ws, bs)


def _cast_into_full(w, layer, kind, place, *, name, dep=None, rb=256):
    _, r, c = w.shape
    rb = _tile(r, rb, 16)
    nrb = r // rb
    full = (r, c * N_CHIPS) if kind == "col" else (r * N_CHIPS, c)

    def body(place_ref, w_ref, *rest):
        rest[-1][...] = w_ref[...].astype(BF16)

    def o_index(i, place):
        return (i, place[0]) if kind == "col" else (i + place[0] * nrb, 0)

    in_specs = [pl.BlockSpec((None, rb, c), lambda i, place: (layer, i, 0))] + ([ANY] if dep is not None else [])
    return pl.pallas_call(
        body,
        grid_spec=pltpu.PrefetchScalarGridSpec(num_scalar_prefetch=1, grid=(nrb,), in_specs=in_specs,
                                               out_specs=pl.BlockSpec((rb, c), o_index)),
        out_shape=jax.ShapeDtypeStruct(full, BF16), compiler_params=_params(("parallel",)), name=name,
    )(place, w, *(() if dep is None else (dep,)))


def _adamw_layer(w, g, m, v, layer, prev, *, name, rb=128):
    _, r, c = w.shape
    rb = _tile(r, rb, 8)
    c1 = 1.0 - ADAM_B1 ** ADAM_STEP
    c2 = 1.0 - ADAM_B2 ** ADAM_STEP

    def body(w_ref, g_ref, m_ref, v_ref, *rest):
        go_ref, d_ref, nm_ref, nv_ref = rest[-4:]
        gv = g_ref[...]
        nm = ADAM_B1 * m_ref[...] + (1.0 - ADAM_B1) * gv
        nv = ADAM_B2 * v_ref[...] + (1.0 - ADAM_B2) * (gv * gv)
        go_ref[...] = gv
        nm_ref[...] = nm
        nv_ref[...] = nv
        d_ref[...] = -ADAM_LR * ((nm / c1) / (jnp.sqrt(nv / c2) + ADAM_EPS) + ADAM_WD * w_ref[...])

    lay = pl.BlockSpec((None, rb, c), lambda i: (layer, i, 0))
    return pl.pallas_call(
        body, grid=(r // rb,), in_specs=[lay, _row_spec(rb, c), lay, lay] + ([ANY] * 4 if prev else []), out_specs=[lay] * 4,
        out_shape=[jax.ShapeDtypeStruct(w.shape, F32)] * 4, input_output_aliases={4 + k: k for k in range(4)} if prev else {},
        compiler_params=_params(("parallel",)), name=name,
    )(w, g, m, v, *(prev or ()))


HBM = pl.BlockSpec(memory_space=pltpu.HBM)
SEM = pl.BlockSpec(memory_space=pltpu.SEMAPHORE)
SIDE_EFFECT = pltpu.SideEffectType.DATAFLOW_SIDE_EFFECTING


def _place():
    x, y, c = lax.axis_index("x"), lax.axis_index("y"), lax.axis_index("c")
    chips = [(1 - x, y), (x, 1 - y), (1 - x, 1 - y)]
    return x, y, c, 2 * x + y, chips


def _half(ref, kind, c):
    r, w = ref.shape
    if kind == "col":
        return ref.at[pl.ds(pl.multiple_of(c * (r // 2), 8), r // 2), :]
    return ref.at[:, pl.ds(pl.multiple_of(c * (w // 2), 128), w // 2)]


def _shard(ref, kind, s):
    r, w = ref.shape
    if kind == "col":
        return ref.at[:, pl.ds(pl.multiple_of(s * (w // N_CHIPS), 128), w // N_CHIPS)]
    return ref.at[pl.ds(pl.multiple_of(s * (r // N_CHIPS), 8), r // N_CHIPS), :]


def _remote(src, dst, send_sem, recv_sem, dev):
    return pltpu.make_async_remote_copy(src_ref=src, dst_ref=dst, send_sem=send_sem, recv_sem=recv_sem,
                                        device_id=dev, device_id_type=MESH)


def _start(name, bufs, plan, sem_shape, dep=None):
    n = len(bufs)
    n_in = n + (dep is not None)

    def body(*refs):
        sends, _ = plan(refs[:n], refs[n_in], refs[n_in + 1])
        for cp in sends:
            cp.start()
        refs[n_in + 2 + n][...] = jnp.zeros((8, 128), F32)

    dma = pltpu.SemaphoreType.DMA
    outs = pl.pallas_call(
        body, name=name,
        out_shape=(dma(sem_shape), dma(sem_shape), *[pltpu.HBM(b.shape, b.dtype) for b in bufs], jax.ShapeDtypeStruct((8, 128), F32)),
        in_specs=(HBM,) * n + ((ANY,) if dep is not None else ()),
        out_specs=(SEM, SEM) + (HBM,) * n + (pl.BlockSpec(memory_space=pltpu.VMEM),),
        input_output_aliases={i: i + 2 for i in range(n)},
        compiler_params=pltpu.CompilerParams(has_side_effects=SIDE_EFFECT),
    )(*[pltpu.with_memory_space_constraint(b, pltpu.HBM) for b in bufs], *(() if dep is None else (dep,)))
    return outs[0], outs[1], list(outs[2:2 + n]), outs[2 + n]


def _wait(name, started, plan, after):
    send, recv, bufs, _ = started
    n = len(bufs)

    def body(*refs):
        sends, recvs = plan(refs[:n], refs[n], refs[n + 1])
        for cp in sends:
            cp.wait_send()
        for cp in recvs:
            cp.wait_recv()

    return list(pl.pallas_call(
        body, name=name, out_shape=tuple(pltpu.HBM(b.shape, b.dtype) for b in bufs),
        in_specs=(HBM,) * n + (SEM, SEM, ANY), out_specs=(HBM,) * n, input_output_aliases={i: i for i in range(n)},
        compiler_params=pltpu.CompilerParams(has_side_effects=SIDE_EFFECT),
    )(*bufs, send, recv, after))


KINDS = ("col", "row")


def _gather_ici_plan(n_small):
    def plan(refs, send, recv):
        x, y, c, s, chips = _place()
        n = len(KINDS) + n_small
        sends, recvs = [], []
        for k, (px, py) in enumerate(chips):
            sp = 2 * px + py
            for a, kind in enumerate(KINDS):
                mine, theirs = _half(_shard(refs[a], kind, s), kind, c), _half(_shard(refs[a], kind, sp), kind, c)
                sends.append(_remote(mine, mine, send.at[k * n + a], recv.at[k * n + a], (px, py, c)))
                recvs.append(_remote(theirs, theirs, send.at[k * n + a], recv.at[k * n + a], (px, py, c)))
            for b in range(n_small):
                ref, sem = refs[len(KINDS) + b], k * n + len(KINDS) + b
                sends.append(_remote(ref.at[s], ref.at[s], send.at[sem], recv.at[sem], (px, py, c)))
                recvs.append(_remote(ref.at[sp], ref.at[sp], send.at[sem], recv.at[sem], (px, py, c)))
        return sends, recvs
    return plan


def _gather_d2d_plan(refs, send, recv):
    x, y, c, _, chips = _place()
    n = len(KINDS)
    sends, recvs = [], []
    for k, (px, py) in enumerate(chips):
        for a, kind in enumerate(KINDS):
            region, sem = _shard(refs[a], kind, 2 * px + py), k * n + a
            sends.append(_remote(_half(region, kind, c), _half(region, kind, c), send.at[sem], recv.at[sem], (x, y, 1 - c)))
            recvs.append(_remote(_half(region, kind, 1 - c), _half(region, kind, 1 - c), send.at[sem], recv.at[sem], (x, y, 1 - c)))
    return sends, recvs


def _swap_plan(refs, send, recv):
    x, y, c, _, _ = _place()
    n = len(KINDS)
    cps = [_remote(_half(refs[a], KINDS[a], 1 - c), refs[n + a], send.at[a], recv.at[a], (x, y, 1 - c)) for a in range(n)]
    return cps, cps


def _exchange_plan(refs, send, recv):
    x, y, c, _, chips = _place()
    n = len(KINDS)
    cps = []
    for k, (px, py) in enumerate(chips):
        for a in range(n):
            cps.append(_remote(_shard(refs[a], KINDS[a], 2 * px + py), refs[n + a].at[k], send.at[k * n + a], recv.at[k * n + a],
                               (px, py, c)))
    return cps, cps


def _share_plan(refs, send, recv):
    x, y, c, _, _ = _place()
    sends = [_remote(_half(refs[a], KINDS[a], c), _half(refs[a], KINDS[a], c), send.at[a], recv.at[a], (x, y, 1 - c))
             for a in range(len(KINDS))]
    recvs = [_remote(_half(refs[a], KINDS[a], 1 - c), _half(refs[a], KINDS[a], 1 - c), send.at[a], recv.at[a], (x, y, 1 - c))
             for a in range(len(KINDS))]
    return sends, recvs


def _spread_plan(refs, send, recv):
    packed, slots = refs
    x, y, c, _, _ = _place()
    sends, recvs = [], []
    for k in range(1, 8):
        px, py, pc = x ^ (k >> 2), y ^ ((k >> 1) & 1), c ^ (k & 1)
        sends.append(_remote(packed, slots.at[4 * x + 2 * y + c], send.at[k - 1], recv.at[k - 1], (px, py, pc)))
        recvs.append(_remote(packed, slots.at[4 * px + 2 * py + pc], send.at[k - 1], recv.at[k - 1], (px, py, pc)))
    return sends, recvs


def _half_index(kind, nblk):
    def index(i, j, place):
        return (i + place[1] * nblk[0], j) if kind == "col" else (i, j + place[1] * nblk[1])
    return index


def _chip_partial(g, other, kind, place, *, name):
    hr, hc = other.shape
    rb, cb = _tile(hr, 512, 16), _tile(hc, 1024)
    nblk = (hr // rb, hc // cb)

    def body(place_ref, g_ref, o_ref, p_ref):
        p_ref[...] = (g_ref[...].astype(F32) + o_ref[...].astype(F32)).astype(BF16)

    plain = pl.BlockSpec((rb, cb), lambda i, j, place: (i, j))
    return pl.pallas_call(
        body,
        grid_spec=pltpu.PrefetchScalarGridSpec(
            num_scalar_prefetch=1, grid=nblk, in_specs=[pl.BlockSpec((rb, cb), _half_index(kind, nblk)), plain], out_specs=plain),
        out_shape=jax.ShapeDtypeStruct((hr, hc), BF16), compiler_params=_params(("parallel", "parallel")), name=name,
    )(place, g, other)


def _reduce_half(g, other, recv, kind, place, *, name):
    _, pr, pc = recv.shape
    rb, cb = _tile(pr, 512, 16), _tile(pc, 1024)
    nblk = (pr // rb, pc // cb)
    full = (pr * 2, pc) if kind == "col" else (pr, pc * 2)

    def g_index(i, j, place):
        s, c = place[0], place[1]
        return (i + c * nblk[0], j + s * nblk[1]) if kind == "col" else (i + s * nblk[0], j + c * nblk[1])

    def o_index(i, j, place):
        return (i, j + place[0] * nblk[1]) if kind == "col" else (i + place[0] * nblk[0], j)

    def body(place_ref, g_ref, o_ref, r_ref, out_ref):
        acc = g_ref[...].astype(F32) + o_ref[...].astype(F32)
        for k in range(3):
            acc = acc + r_ref[k].astype(F32)
        out_ref[...] = acc

    return pl.pallas_call(
        body,
        grid_spec=pltpu.PrefetchScalarGridSpec(
            num_scalar_prefetch=1, grid=nblk,
            in_specs=[pl.BlockSpec((rb, cb), g_index), pl.BlockSpec((rb, cb), o_index),
                      pl.BlockSpec((3, rb, cb), lambda i, j, place: (0, i, j))],
            out_specs=pl.BlockSpec((rb, cb), _half_index(kind, nblk))),
        out_shape=jax.ShapeDtypeStruct(full, F32), compiler_params=_params(("parallel", "parallel")), name=name,
    )(place, g, other, recv)


def _sum_slots(packed, slots, me, *, name, rb=512):
    r, w = packed.shape
    rb = _tile(r, rb, 8)

    def body(me_ref, p_ref, s_ref, o_ref):
        acc = None
        for j in range(8):
            term = jnp.where(me_ref[0] == j, p_ref[...], s_ref[j])
            acc = term if acc is None else acc + term
        o_ref[...] = acc

    return pl.pallas_call(
        body,
        grid_spec=pltpu.PrefetchScalarGridSpec(
            num_scalar_prefetch=1, grid=(r // rb,),
            in_specs=[pl.BlockSpec((rb, w), lambda i, me: (i, 0)), pl.BlockSpec((8, rb, w), lambda i, me: (0, i, 0))],
            out_specs=pl.BlockSpec((rb, w), lambda i, me: (i, 0))),
        out_shape=jax.ShapeDtypeStruct((r, w), F32), compiler_params=_params(("parallel",)), name=name,
    )(me, packed, slots)


def _half_shape(a, kind):
    return (a.shape[0] // 2, a.shape[1]) if kind == "col" else (a.shape[0], a.shape[1] // 2)


def _rs_swap(tag, grads):
    others = [lax.empty(_half_shape(g, k), g.dtype) for g, k in zip(grads, KINDS)]
    return _start(f"rs_{tag}_swap", list(grads) + others, _swap_plan, (len(KINDS),))


def _rs_exchange(tag, swapped, place, after):
    bufs = _wait(f"rs_{tag}_swap_wait", swapped, _swap_plan, after)
    n = len(KINDS)
    grads, others = bufs[:n], bufs[n:]
    parts = [_chip_partial(g, o, k, place, name=f"rs_{tag}_partial_{k}") for g, o, k in zip(grads, others, KINDS)]
    lands = []
    for p, k in zip(parts, KINDS):
        piece = (p.shape[0], p.shape[1] // N_CHIPS) if k == "col" else (p.shape[0] // N_CHIPS, p.shape[1])
        lands.append(lax.empty((3,) + piece, p.dtype))
    return _start(f"rs_{tag}_exchange", parts + lands, _exchange_plan, (3 * n,)), grads, others


def _rs_share(tag, exchanged, place, after):
    started, grads, others = exchanged
    n = len(KINDS)
    recvs = _wait(f"rs_{tag}_exchange_wait", started, _exchange_plan, after)[n:]
    halves = [_reduce_half(g, o, r, k, place, name=f"rs_{tag}_reduce_{k}") for g, o, r, k in zip(grads, others, recvs, KINDS)]
    return _start(f"rs_{tag}_share", halves, _share_plan, (n,))


def _rs_finish(tag, shared, after):
    return _wait(f"rs_{tag}_share_wait", shared, _share_plan, after)


def _spread(tag, parts, dep):
    rows = [p.reshape(-1, 128) for p in parts]
    n = sum(r.shape[0] for r in rows)
    rows.append(jnp.zeros(((-n) % 512, 128), F32))
    packed = jnp.concatenate(rows, axis=0)
    return _start(f"small_{tag}_spread", [packed, lax.empty((8,) + packed.shape, F32)], _spread_plan, (7,), dep=dep)


def _spread_sum(tag, started, parts, me, after):
    packed, slots = _wait(f"small_{tag}_spread_wait", started, _spread_plan, after)
    total = _sum_slots(packed, slots, me, name=f"small_{tag}_sum")
    out, row = [], 0
    for p in parts:
        n = p.size // 128
        out.append(total[row:row + n].reshape(p.shape))
        row += n
    return out


def _ffn_fwd(x, h, w_up, conv_w, conv_b, w_down, tag, tm_up=1024):
    up = _mm(h, w_up, tm=tm_up, name=f"ffn{tag}_up")
    act, conv = _ffn_mid_fwd(up, conv_w, conv_b, name=f"ffn{tag}_mid")
    x_out = _mm(act, w_down, res=x, tk=2816, name=f"ffn{tag}_down")
    return x_out, (up, conv, act)


def kernel(x, a_norm, a_in, a_conv, a_out, b_norm, b_in, b_vnorm, b_ws, b_bs, b_out, f_norm, f_up, f_conv_w, f_conv_b, f_down, final_norm, loss_target, m_a_norm, m_a_in, m_a_conv, m_a_out, m_b_norm, m_b_in, m_b_vnorm, m_b_ws, m_b_bs, m_b_out, m_f_norm, m_f_up, m_f_conv_w, m_f_conv_b, m_f_down, m_final_norm, v_a_norm, v_a_in, v_a_conv, v_a_out, v_b_norm, v_b_in, v_b_vnorm, v_b_ws, v_b_bs, v_b_out, v_f_norm, v_f_up, v_f_conv_w, v_f_conv_b, v_f_down, v_final_norm):
    t, d = x.shape[1], x.shape[2]
    f2 = f_up.shape[2] * N_CHIPS
    x0, tgt = x.reshape(t, d), loss_target.reshape(t, d)
    ax, ay, ac = lax.axis_index("x"), lax.axis_index("y"), lax.axis_index("c")
    s = 2 * ax + ay
    place = jnp.stack([s, ac]).astype(jnp.int32)
    me = (4 * ax + 2 * ay + ac).astype(jnp.int32).reshape(1)

    def stacked(a):
        return lax.dynamic_update_index_in_dim(jnp.zeros((N_CHIPS,) + a.shape, F32), a, s, 0)

    def cast_pair(tag, w_in, w_out, layer, dep):
        return [_cast_into_full(w_in, layer, "col", place, name=f"cast_{tag}_in", dep=dep),
                _cast_into_full(w_out, layer, "row", place, name=f"cast_{tag}_out", dep=dep)]

    def gather_start(tag, fulls, small, dep):
        return _start(f"ag_{tag}_ici", fulls + small, _gather_ici_plan(len(small)), (3 * (2 + len(small)),), dep=dep)

    def gather_forward(tag, started, n_small, after):
        bufs = _wait(f"ag_{tag}_ici_wait", started, _gather_ici_plan(n_small), after)
        return _start(f"ag_{tag}_d2d", bufs[:2], _gather_d2d_plan, (3 * 2,)), bufs[2:]

    def gather_finish(tag, forwarded, after):
        return _wait(f"ag_{tag}_d2d_wait", forwarded, _gather_d2d_plan, after)

    small = [stacked(a_conv[0]), stacked(b_norm), stacked(b_vnorm), stacked(f_conv_w.reshape(2 * 3, -1))]
    ag_a = gather_start("a", cast_pair("a", a_in, a_out, 0, None), small, None)
    full_f0 = cast_pair("f0", f_up, f_down, 0, ag_a[3])
    full_b = cast_pair("b", b_in, b_out, 0, full_f0[1])
    full_f1 = cast_pair("f1", f_up, f_down, 1, full_b[1])

    def unshard(a):
        return jnp.transpose(a, (1, 0, 2)).reshape(a.shape[1], -1)

    ws, bs = b_ws[0], b_bs[0]

    h0 = _rms_fwd(x0, a_norm, dep=full_f1[1], name="a_norm")
    fw_a, (g_aconv, g_bnorm, g_bvnorm, g_fconv) = gather_forward("a", ag_a, 4, h0)
    ag_f0 = gather_start("f0", full_f0, [], fw_a[3])
    ag_b = gather_start("b", full_b, [], ag_f0[3])
    ag_f1 = gather_start("f1", full_f1, [], ag_b[3])
    w_ai, w_ao = gather_finish("a", fw_a, ag_f1[3])
    a_conv_f, b_norm_f, b_vnorm_f = unshard(g_aconv), unshard(g_bnorm), unshard(g_bvnorm)
    f_conv_f = unshard(g_fconv).reshape(2, 3, f2)
    bcx = _mm(h0, w_ai, tm=2048, name="a_in")
    y = _a_mid_fwd(bcx, a_conv_f, name="a_mid")
    x1 = _mm(y, w_ao, res=x0, tm=512, tn=2048, name="a_out")
    fw_f0, _ = gather_forward("f0", ag_f0, 0, x1)
    h1 = _rms_fwd(x1, f_norm[0:1], dep=fw_f0[3], name="ffn0_norm")
    w_up0, w_dn0 = gather_finish("f0", fw_f0, h1)
    x2, (up0, conv0, act0) = _ffn_fwd(x1, h1, w_up0, f_conv_f[0], f_conv_b[0:1], w_dn0, 0, tm_up=2048)
    fw_b, _ = gather_forward("b", ag_b, 0, up0)
    w_bi, w_bo = gather_finish("b", fw_b, act0)
    h2 = _rms_fwd(x2, b_norm_f, name="b_norm")
    zp = _mm(h2, w_bi, name="b_in")
    fw_f1, _ = gather_forward("f1", ag_f1, 0, zp)
    ug = _b_mid_fwd(zp, b_vnorm_f, ws, bs, name="b_mid")
    x3, h3 = _mm(ug, w_bo, res=x2, norm=f_norm[1:2], tm=512, tn=2048, name="b_out")
    w_up1, w_dn1 = gather_finish("f1", fw_f1, x3)
    x4, (up1, conv1, act1) = _ffn_fwd(x3, h3, w_up1, f_conv_f[1], f_conv_b[1:2], w_dn1, 1)
    loss_rows, dx4, dx4b, d_final = _final(x4, tgt, final_norm.reshape(1, d), name="final")

    d_dn1 = _mm(act1, dx4b, ta=True, tm=1408, out_dtype=BF16, name="ffn1_ddown")
    dact1 = _mm(dx4b, w_dn1, tb=True, tn=512, tm=2048, name="ffn1_dact")
    dup1, d_fwb1 = _ffn_mid_bwd(up1, conv1, dact1, f_conv_f[1], name="ffn1_mid_bwd")
    d_up1 = _mm(h3, dup1, ta=True, out_dtype=BF16, tk=4096, name="ffn1_dup")
    sw_f1 = _rs_swap("f1", [d_up1, d_dn1])
    dh3 = _mm(dup1, w_up1, tb=True, tk=2816, dep=sw_f1[3], name="ffn1_dh")
    dx3, dx3b, d_fnorm1 = _rms_bwd(dh3, x3, f_norm[1:2], dx4, name="ffn1_norm_bwd")
    ex_f1 = _rs_exchange("f1", sw_f1, place, dx3)

    d_bo = _mm(ug, dx3b, ta=True, out_dtype=BF16, tk=4096, dep=ex_f1[0][3], name="b_dout")
    dug = _mm(dx3b, w_bo, tb=True, tm=512, tn=2048, name="b_dug")
    dzp, d_ws, d_bs, d_bvnorm = _b_mid_bwd(zp, dug, b_vnorm_f, ws, bs, name="b_mid_bwd")
    d_bi = _mm(h2, dzp, ta=True, out_dtype=BF16, tk=4096, name="b_din")
    sw_b = _rs_swap("b", [d_bi, d_bo])
    dh2 = _mm(dzp, w_bi, tb=True, tk=4096, dep=sw_b[3], name="b_dh")
    dx2, dx2b, d_bnorm = _rms_bwd(dh2, x2, b_norm_f, dx3, name="b_norm_bwd")
    ex_b = _rs_exchange("b", sw_b, place, dx2)

    d_dn0 = _mm(act0, dx2b, ta=True, tm=1408, out_dtype=BF16, dep=ex_b[0][3], name="ffn0_ddown")
    dact0 = _mm(dx2b, w_dn0, tb=True, tn=512, tm=2048, name="ffn0_dact")
    dup0, d_fwb0 = _ffn_mid_bwd(up0, conv0, dact0, f_conv_f[0], name="ffn0_mid_bwd")
    sh_f1 = _rs_share("f1", ex_f1, place, dup0)
    d_up0 = _mm(h1, dup0, ta=True, out_dtype=BF16, tk=4096, dep=sh_f1[3], name="ffn0_dup")
    sw_f0 = _rs_swap("f0", [d_up0, d_dn0])
    g_up1, g_dn1 = _rs_finish("f1", sh_f1, sw_f0[3])
    dh1 = _mm(dup0, w_up0, tb=True, tk=2816, dep=sw_f0[3], name="ffn0_dh")
    dx1, dx1b, d_fnorm0 = _rms_bwd(dh1, x1, f_norm[0:1], dx2, name="ffn0_norm_bwd")
    ex_f0 = _rs_exchange("f0", sw_f0, place, dx1)
    early = [jnp.concatenate([d_bnorm, d_bvnorm, d_fnorm0, d_fnorm1, d_final, loss_rows], axis=0),
             jnp.concatenate([d_fwb0, d_fwb1], axis=0), jnp.concatenate([d_ws.reshape(-1, CHUNK), d_bs], axis=0)]
    sp_early = _spread("early", early, ex_f0[0][3])
    sh_b = _rs_share("b", ex_b, place, sp_early[3])

    d_ao = _mm(y, dx1b, ta=True, out_dtype=BF16, tk=4096, dep=sh_b[3], name="a_dout")
    dyy = _mm(dx1b, w_ao, tb=True, tm=512, tn=2048, name="a_dy")
    dbcx, d_aconv = _a_mid_bwd(bcx, dyy, a_conv_f, name="a_mid_bwd")
    d_ai = _mm(h0, dbcx, ta=True, out_dtype=BF16, tk=4096, name="a_din")
    sw_a = _rs_swap("a", [d_ai, d_ao])
    g_bi, g_bo = _rs_finish("b", sh_b, sw_a[3])
    ex_a = _rs_exchange("a", sw_a, place, g_bi)
    dh0 = _mm(dbcx, w_ai, tb=True, tk=3072, dep=ex_a[0][3], name="a_dh")
    grad_x, _, d_anorm = _rms_bwd(dh0, x0, a_norm, dx1, name="a_norm_bwd")
    late = [jnp.concatenate([d_anorm, d_aconv], axis=0)]
    sp_late = _spread("late", late, ex_a[0][3])
    sh_f0 = _rs_share("f0", ex_f0, place, sp_late[3])
    sh_a = _rs_share("a", ex_a, place, sh_f0[3])
    g_up0, g_dn0 = _rs_finish("f0", sh_f0, sh_a[3])
    g_ai, g_ao = _rs_finish("a", sh_a, g_up0)
    r_a, r_b, r_c = _spread_sum("early", sp_early, early, me, g_ai)
    (r_l,) = _spread_sum("late", sp_late, late, me, r_a)

    loss = jnp.sum(r_a[40])
    cs, fs = d // N_CHIPS, f2 // N_CHIPS

    def mine(a, width):
        return lax.dynamic_slice_in_dim(a, s * width, width, axis=1)

    grads = {
        "a_norm": r_l[0:1], "a_conv": mine(r_l[8:11], cs), "b_norm": mine(r_a[0:1], cs), "b_vnorm": mine(r_a[8:9], cs),
        "f_norm": jnp.concatenate([r_a[16:17], r_a[24:25]], axis=0), "final_norm": r_a[32:33],
        "b_ws": r_c[:GROUPS * CHUNK], "b_bs": r_c[GROUPS * CHUNK:],
        "f_conv_w": jnp.concatenate([mine(r_b[0:3], fs), mine(r_b[8:11], fs)], axis=0),
        "f_conv_b": jnp.concatenate([r_b[3:4], r_b[11:12]], axis=0),
        "a_in": g_ai, "a_out": g_ao, "b_in": g_bi, "b_out": g_bo,
    }
    names = ["a_norm", "a_in", "a_conv", "a_out", "b_norm", "b_in", "b_vnorm", "b_ws", "b_bs", "b_out", "f_norm", "f_up",
             "f_conv_w", "f_conv_b", "f_down", "final_norm"]
    weights = dict(zip(names, [a_norm, a_in, a_conv, a_out, b_norm, b_in, b_vnorm, b_ws, b_bs, b_out, f_norm, f_up, f_conv_w,
                               f_conv_b, f_down, final_norm]))
    ms = dict(zip(names, [m_a_norm, m_a_in, m_a_conv, m_a_out, m_b_norm, m_b_in, m_b_vnorm, m_b_ws, m_b_bs, m_b_out, m_f_norm,
                          m_f_up, m_f_conv_w, m_f_conv_b, m_f_down, m_final_norm]))
    vs = dict(zip(names, [v_a_norm, v_a_in, v_a_conv, v_a_out, v_b_norm, v_b_in, v_b_vnorm, v_b_ws, v_b_bs, v_b_out, v_f_norm,
                          v_f_up, v_f_conv_w, v_f_conv_b, v_f_down, v_final_norm]))
    result = {}
    for n in names:
        w = weights[n]
        if n in ("f_up", "f_down"):
            g1, g0 = (g_up1, g_up0) if n == "f_up" else (g_dn1, g_dn0)
            first = _adamw_layer(w, g1, ms[n], vs[n], 1, None, name=f"adamw_{n}1")
            result[n] = _adamw_layer(w, g0, ms[n], vs[n], 0, tuple(first), name=f"adamw_{n}0")
            continue
        g2 = grads[n]
        as3d = (lambda a: a.reshape((1,) + g2.shape))
        result[n] = [o.reshape(w.shape) for o in _adamw_layer(as3d(w), g2, as3d(ms[n]), as3d(vs[n]), 0, None, name=f"adamw_{n}")]

    return (loss, grad_x.reshape(x.shape), *[result[n][0] for n in names], *[result[n][1] for n in names],
            *[result[n][2] for n in names], *[result[n][3] for n in names])
```

```python
import functools

import jax
import jax.numpy as jnp
from jax import lax
from jax.experimental import pallas as pl
from jax.experimental.pallas import tpu as pltpu

F32 = jnp.float32
BF16 = jnp.bfloat16
MESH = pl.DeviceIdType.MESH
ANY = pl.BlockSpec(memory_space=pl.ANY)

RMS_EPS = 1e-5
CHUNK = 128
GROUPS = 8
ADAM_LR, ADAM_B1, ADAM_B2, ADAM_EPS, ADAM_WD, ADAM_STEP = 0.001, 0.9, 0.999, 1e-08, 0.01, 10

N_CHIPS = 4
HALO = 8
VMEM_LIMIT = 56 * 1024 * 1024
GELU_C = 0.7978845608028654
GELU_A = 0.044715


def _params(sem=None):
    return pltpu.CompilerParams(dimension_semantics=sem, vmem_limit_bytes=VMEM_LIMIT)


def _tile(dim, pref, quantum=128):
    if dim <= pref:
        return dim
    t = (pref // quantum) * quantum
    while t >= quantum:
        if dim % t == 0:
            return t
        t -= quantum
    return dim


def _mm(a, b, *, name, ta=False, tb=False, res=None, norm=None, dep=None, out_dtype=F32, tm=1024, tn=1024, tk=2048):
    (K, M) = a.shape if ta else a.shape[::-1]
    N = b.shape[0] if tb else b.shape[1]
    assert (b.shape[1] if tb else b.shape[0]) == K
    tm, tn, tk = _tile(M, tm), _tile(N, tn), _tile(K, tk)
    nk = K // tk
    assert norm is None or (tn == N and nk == 1)
    a_spec = pl.BlockSpec((tk, tm), lambda i, j, k: (k, i)) if ta else pl.BlockSpec((tm, tk), lambda i, j, k: (i, k))
    b_spec = pl.BlockSpec((tn, tk), lambda i, j, k: (j, k)) if tb else pl.BlockSpec((tk, tn), lambda i, j, k: (k, j))
    o_spec = pl.BlockSpec((tm, tn), lambda i, j, k: (i, j))
    dims = (((0 if ta else 1,), (1 if tb else 0,)), ((), ()))
    direct = out_dtype == F32
    n_in = 2 + (res is not None) + (norm is not None) + (dep is not None)

    def body(*refs):
        a_ref, b_ref = refs[0], refs[1]
        r_ref = refs[2] if res is not None else None
        g_ref = refs[2 + (res is not None)] if norm is not None else None
        o_ref = refs[n_in]
        acc_ref = o_ref if direct else refs[-1]
        part = lax.dot_general(a_ref[...], b_ref[...], dims, preferred_element_type=F32)
        if nk == 1:
            if r_ref is not None:
                part = part + r_ref[...]
            o_ref[...] = part.astype(o_ref.dtype)
            if g_ref is not None:
                r = lax.rsqrt(jnp.mean(part * part, axis=-1, keepdims=True) + RMS_EPS)
                refs[n_in + 1][...] = ((part * r) * g_ref[...]).astype(BF16)
            return
        k = pl.program_id(2)

        @pl.when(k == 0)
        def _():
            acc_ref[...] = part

        @pl.when(jnp.logical_and(k > 0, k < nk - 1))
        def _():
            acc_ref[...] += part

        @pl.when(k == nk - 1)
        def _():
            tot = acc_ref[...] + part
            if r_ref is not None:
                tot = tot + r_ref[...]
            o_ref[...] = tot.astype(o_ref.dtype)

    in_specs = ([a_spec, b_spec] + ([o_spec] if res is not None else [])
                + ([pl.BlockSpec((1, tn), lambda i, j, k: (0, j))] if norm is not None else []) + ([ANY] if dep is not None else []))
    args = (a, b) + tuple(x for x in (res, norm, dep) if x is not None)
    scratch = [] if (direct or nk == 1) else [pltpu.VMEM((tm, tn), F32)]
    out_shape = jax.ShapeDtypeStruct((M, N), out_dtype)
    return pl.pallas_call(
        body, grid=(M // tm, N // tn, nk), in_specs=in_specs, out_specs=[o_spec, o_spec] if norm is not None else o_spec,
        out_shape=[out_shape, jax.ShapeDtypeStruct((M, N), BF16)] if norm is not None else out_shape, scratch_shapes=scratch,
        compiler_params=_params(("parallel", "parallel", "arbitrary")), name=name,
    )(*args)


def _row_spec(rb, w):
    return pl.BlockSpec((rb, w), lambda i: (i, 0))


def _prev_spec(rb, w):
    return pl.BlockSpec((HALO, w), lambda i: (jnp.maximum(i * (rb // HALO) - 1, 0), 0))


def _next_spec(rb, w, t):
    return pl.BlockSpec((HALO, w), lambda i: (jnp.minimum((i + 1) * (rb // HALO), t // HALO - 1), 0))


def _full_spec(shape):
    return pl.BlockSpec(shape, lambda i: tuple(0 for _ in shape))


def _shift(e, s):
    return pltpu.roll(e, s % e.shape[0], 0)


def _gelu(x):
    return 0.5 * x * (1.0 + jnp.tanh(GELU_C * (x + GELU_A * x * x * x)))


def _gelu_grad(x):
    th = jnp.tanh(GELU_C * (x + GELU_A * x * x * x))
    return 0.5 * (1.0 + th) + 0.5 * x * (1.0 - th * th) * (GELU_C * (1.0 + 3.0 * GELU_A * x * x))


def _sigmoid(x):
    return 1.0 / (1.0 + jnp.exp(-x))


def _rms_fwd(x, g, *, name, dep=None, rb=256):
    t, d = x.shape
    rb = _tile(t, rb, 8)

    def body(x_ref, g_ref, *rest):
        h_ref = rest[-1]
        xv = x_ref[...]
        r = lax.rsqrt(jnp.mean(xv * xv, axis=-1, keepdims=True) + RMS_EPS)
        h_ref[...] = ((xv * r) * g_ref[...]).astype(BF16)

    return pl.pallas_call(
        body, grid=(t // rb,), in_specs=[_row_spec(rb, d), _full_spec((1, d))] + ([ANY] if dep is not None else []),
        out_specs=_row_spec(rb, d), out_shape=jax.ShapeDtypeStruct((t, d), BF16), compiler_params=_params(("parallel",)), name=name,
    )(x, g, *(() if dep is None else (dep,)))


def _rms_bwd(dh, x, g, dres, *, name, dep=None, rb=256):
    t, d = x.shape
    rb = _tile(t, rb, 8)

    def body(dh_ref, x_ref, g_ref, dres_ref, *rest):
        dx_ref, dxb_ref, dg_ref = rest[-3:]
        xv = x_ref[...]
        r = lax.rsqrt(jnp.mean(xv * xv, axis=-1, keepdims=True) + RMS_EPS)
        xhat = xv * r
        dh_v = dh_ref[...].astype(F32)
        dxhat = dh_v * g_ref[...]
        m = jnp.mean(dxhat * xhat, axis=-1, keepdims=True)
        dx = dres_ref[...] + r * (dxhat - xhat * m)
        dx_ref[...] = dx
        dxb_ref[...] = dx.astype(BF16)

        @pl.when(pl.program_id(0) == 0)
        def _():
            dg_ref[...] = jnp.zeros_like(dg_ref)

        dg_ref[0:1, :] += jnp.sum(dh_v * xhat, axis=0, keepdims=True)

    return pl.pallas_call(
        body, grid=(t // rb,),
        in_specs=[_row_spec(rb, d), _row_spec(rb, d), _full_spec((1, d)), _row_spec(rb, d)] + ([ANY] if dep is not None else []),
        out_specs=[_row_spec(rb, d), _row_spec(rb, d), _full_spec((8, d))],
        out_shape=[jax.ShapeDtypeStruct((t, d), F32), jax.ShapeDtypeStruct((t, d), BF16), jax.ShapeDtypeStruct((8, d), F32)],
        compiler_params=_params(("arbitrary",)), name=name,
    )(dh, x, g, dres, *(() if dep is None else (dep,)))


def _final(x, tgt, g, *, name, rb=256):
    t, d = x.shape
    rb = _tile(t, rb, 8)
    inv_d = 1.0 / d

    def body(x_ref, t_ref, g_ref, l_ref, dx_ref, dxb_ref, dg_ref):
        xv = x_ref[...]
        gv = g_ref[...]
        r = lax.rsqrt(jnp.mean(xv * xv, axis=-1, keepdims=True) + RMS_EPS)
        xhat = xv * r
        e = xhat * gv - t_ref[...]
        dy = e * inv_d
        dxhat = dy * gv
        m = jnp.mean(dxhat * xhat, axis=-1, keepdims=True)
        dx = r * (dxhat - xhat * m)
        dx_ref[...] = dx
        dxb_ref[...] = dx.astype(BF16)

        @pl.when(pl.program_id(0) == 0)
        def _():
            l_ref[...] = jnp.zeros_like(l_ref)
            dg_ref[...] = jnp.zeros_like(dg_ref)

        l_ref[0:1, :] += jnp.sum(e * e, axis=0, keepdims=True) * (0.5 * inv_d)
        dg_ref[0:1, :] += jnp.sum(dy * xhat, axis=0, keepdims=True)

    return pl.pallas_call(
        body, grid=(t // rb,),
        in_specs=[_row_spec(rb, d), _row_spec(rb, d), _full_spec((1, d))],
        out_specs=[_full_spec((8, d)), _row_spec(rb, d), _row_spec(rb, d), _full_spec((8, d))],
        out_shape=[jax.ShapeDtypeStruct((8, d), F32), jax.ShapeDtypeStruct((t, d), F32),
                   jax.ShapeDtypeStruct((t, d), BF16), jax.ShapeDtypeStruct((8, d), F32)],
        compiler_params=_params(("arbitrary",)), name=name,
    )(x, tgt, g)


def _a_mid_fwd(bcx, wconv, *, name, rb=256, cw=512):
    t, d3 = bcx.shape
    d = d3 // 3
    rb, cw = _tile(t, rb, 8), _tile(d, cw)

    def body(cur_ref, prev_ref, w_ref, y_ref):
        first = pl.program_id(0) == 0
        for c0 in range(0, d, cw):
            cs = slice(c0, c0 + cw)
            gc, xs = slice(d + c0, d + c0 + cw), slice(2 * d + c0, 2 * d + c0 + cw)
            p_prev = jnp.where(first, 0.0, prev_ref[:, gc] * prev_ref[:, xs])
            e = jnp.concatenate([p_prev, cur_ref[:, gc] * cur_ref[:, xs]], axis=0)
            w = w_ref[:, cs]
            q = w[0:1] * _shift(e, 2) + w[1:2] * _shift(e, 1) + w[2:3] * e
            y_ref[:, cs] = (cur_ref[:, cs] * q[HALO:]).astype(BF16)

    return pl.pallas_call(
        body, grid=(t // rb,),
        in_specs=[_row_spec(rb, d3), _prev_spec(rb, d3), _full_spec((3, d))], out_specs=_row_spec(rb, d),
        out_shape=jax.ShapeDtypeStruct((t, d), BF16), compiler_params=_params(("parallel",)), name=name,
    )(bcx, bcx, wconv)


def _a_mid_bwd(bcx, dy, wconv, *, name, rb=128, cw=512):
    t, d3 = bcx.shape
    d = d3 // 3
    rb, cw = _tile(t, rb, 8), _tile(d, cw)

    def body(cur_ref, prev_ref, next_ref, dy_ref, dyn_ref, w_ref, o_ref, dw_ref):
        i = pl.program_id(0)
        first, last = i == 0, i == pl.num_programs(0) - 1

        @pl.when(first)
        def _():
            dw_ref[...] = jnp.zeros_like(dw_ref)

        for c0 in range(0, d, cw):
            cs = slice(c0, c0 + cw)
            gc, xs = slice(d + c0, d + c0 + cw), slice(2 * d + c0, 2 * d + c0 + cw)
            zeros = jnp.zeros((HALO, cw), F32)
            gb_c, gc_c, xs_c = cur_ref[:, cs], cur_ref[:, gc], cur_ref[:, xs]
            p_prev = jnp.where(first, 0.0, prev_ref[:, gc] * prev_ref[:, xs])
            e = jnp.concatenate([p_prev, gc_c * xs_c, zeros], axis=0)
            dq_next = jnp.where(last, 0.0, dyn_ref[:, cs] * next_ref[:, cs])
            dy_c = dy_ref[:, cs]
            dq = jnp.concatenate([zeros, dy_c * gb_c, dq_next], axis=0)
            w = w_ref[:, cs]
            e1, e2 = _shift(e, 1), _shift(e, 2)
            q = w[0:1] * e2 + w[1:2] * e1 + w[2:3] * e
            dp = (w[2:3] * dq + w[1:2] * _shift(dq, -1) + w[0:1] * _shift(dq, -2))[HALO:HALO + rb]
            o_ref[:, cs] = (dy_c * q[HALO:HALO + rb]).astype(BF16)
            o_ref[:, gc] = (dp * xs_c).astype(BF16)
            o_ref[:, xs] = (dp * gc_c).astype(BF16)
            dq_c = dq[HALO:HALO + rb]
            dw_ref[0:1, cs] += jnp.sum(dq_c * e2[HALO:HALO + rb], axis=0, keepdims=True)
            dw_ref[1:2, cs] += jnp.sum(dq_c * e1[HALO:HALO + rb], axis=0, keepdims=True)
            dw_ref[2:3, cs] += jnp.sum(dq_c * e[HALO:HALO + rb], axis=0, keepdims=True)

    return pl.pallas_call(
        body, grid=(t // rb,),
        in_specs=[_row_spec(rb, d3), _prev_spec(rb, d3), _next_spec(rb, d3, t), _row_spec(rb, d), _next_spec(rb, d, t),
                  _full_spec((3, d))],
        out_specs=[_row_spec(rb, d3), _full_spec((8, d))],
        out_shape=[jax.ShapeDtypeStruct((t, d3), BF16), jax.ShapeDtypeStruct((8, d), F32)],
        compiler_params=_params(("arbitrary",)), name=name,
    )(bcx, bcx, bcx, dy, dy, wconv)


def _ffn_mid_fwd(up, wconv, bconv, *, name, rb=128, cw=512):
    t, f2 = up.shape
    f = f2 // 2
    rb, cw = _tile(t, rb, 16), _tile(f, cw)

    def body(cur_ref, prev_ref, w_ref, b_ref, act_ref, conv_ref):
        first = pl.program_id(0) == 0

        def conv(cols):
            prev = prev_ref[:, cols].astype(F32)[BF16_HALO - HALO:]
            e = jnp.concatenate([jnp.where(first, 0.0, prev), cur_ref[:, cols].astype(F32)], axis=0)
            w = w_ref[:, cols]
            out = (w[0:1] * _shift(e, 2) + w[1:2] * _shift(e, 1) + w[2:3] * e + b_ref[:, cols])[HALO:]
            conv_ref[:, cols] = out.astype(BF16)
            return out

        for c0 in range(0, f, cw):
            g = conv(slice(c0, c0 + cw))
            a = conv(slice(f + c0, f + c0 + cw))
            act_ref[:, c0:c0 + cw] = (g * _sigmoid(g) * a).astype(BF16)

    return pl.pallas_call(
        body, grid=(t // rb,),
        in_specs=[_row_spec(rb, f2), pl.BlockSpec((BF16_HALO, f2), lambda i: (jnp.maximum(i * (rb // BF16_HALO) - 1, 0), 0)),
                  _full_spec((3, f2)), _full_spec((1, f2))],
        out_specs=[_row_spec(rb, f), _row_spec(rb, f2)],
        out_shape=[jax.ShapeDtypeStruct((t, f), BF16), jax.ShapeDtypeStruct((t, f2), BF16)],
        compiler_params=_params(("parallel",)), name=name,
    )(up, up, wconv, bconv)


BF16_HALO = 16


def _ffn_mid_bwd(up, conv, dact, wconv, *, name, rb=128, cw=512):
    t, f2 = up.shape
    f = f2 // 2
    rb, cw = _tile(t, rb, 16), _tile(f, cw)

    def body(up_ref, conv_ref, convn_ref, da_ref, dan_ref, w_ref, o_ref, dwb_ref):
        i = pl.program_id(0)
        last = i == pl.num_programs(0) - 1

        @pl.when(i == 0)
        def _():
            dwb_ref[...] = jnp.zeros_like(dwb_ref)

        def rows(cols):
            return jnp.concatenate([conv_ref[:, cols].astype(F32), convn_ref[:, cols].astype(F32)[0:HALO]], axis=0)

        def back(dc, cols):
            w = w_ref[:, cols]
            dc1, dc2 = _shift(dc, -1)[:rb], _shift(dc, -2)[:rb]
            dc0 = dc[:rb]
            o_ref[:, cols] = (w[2:3] * dc0 + w[1:2] * dc1 + w[0:1] * dc2).astype(BF16)
            u = up_ref[:, cols].astype(F32)
            dwb_ref[0:1, cols] += jnp.sum(dc2 * u, axis=0, keepdims=True)
            dwb_ref[1:2, cols] += jnp.sum(dc1 * u, axis=0, keepdims=True)
            dwb_ref[2:3, cols] += jnp.sum(dc0 * u, axis=0, keepdims=True)
            dwb_ref[3:4, cols] += jnp.sum(dc0, axis=0, keepdims=True)

        for c0 in range(0, f, cw):
            gcols, acols = slice(c0, c0 + cw), slice(f + c0, f + c0 + cw)
            g, a = rows(gcols), rows(acols)
            da = jnp.concatenate([da_ref[:, gcols], jnp.where(last, 0.0, dan_ref[:, gcols])], axis=0)
            sg = _sigmoid(g)
            back(da * a * (sg * (1.0 + g * (1.0 - sg))), gcols)
            back(da * (g * sg), acols)

    return pl.pallas_call(
        body, grid=(t // rb,),
        in_specs=[_row_spec(rb, f2), _row_spec(rb, f2),
                  pl.BlockSpec((BF16_HALO, f2), lambda i: (jnp.minimum((i + 1) * (rb // BF16_HALO), t // BF16_HALO - 1), 0)),
                  _row_spec(rb, f), _next_spec(rb, f, t), _full_spec((3, f2))],
        out_specs=[_row_spec(rb, f2), _full_spec((8, f2))],
        out_shape=[jax.ShapeDtypeStruct((t, f2), BF16), jax.ShapeDtypeStruct((8, f2), F32)],
        compiler_params=_params(("arbitrary",)), name=name,
    )(up, conv, conv, dact, dact, wconv)


def _causal_mask():
    return lax.broadcasted_iota(jnp.int32, (CHUNK, CHUNK), 0) >= lax.broadcasted_iota(jnp.int32, (CHUNK, CHUNK), 1)


def _b_mid_fwd(zp, vnorm, ws, bs, *, name, rb=256):
    t, d2 = zp.shape
    d = d2 // 2
    c = d // GROUPS
    rb = _tile(t, rb, CHUNK)

    def body(zp_ref, gv_ref, ws_ref, bs_ref, ug_ref, vn_ref, gate_ref):
        v = _gelu(zp_ref[:, d:])
        rv = lax.rsqrt(jnp.mean(v * v, axis=-1, keepdims=True) + RMS_EPS)
        vn_ref[...] = ((v * rv) * gv_ref[...]).astype(BF16)
        mask = _causal_mask()
        for h in range(GROUPS):
            hc = slice(h * c, (h + 1) * c)
            wm = jnp.where(mask, ws_ref[h], 0.0).astype(BF16)
            bcol = jnp.broadcast_to(bs_ref[h:h + 1, :], (CHUNK, CHUNK)).T[:, 0:1]
            for n in range(rb // CHUNK):
                rows = slice(n * CHUNK, (n + 1) * CHUNK)
                gate_ref[rows, hc] = jnp.dot(wm, vn_ref[rows, hc], preferred_element_type=F32) + bcol
        ug_ref[...] = (_gelu(zp_ref[:, :d]) * gate_ref[...]).astype(BF16)

    return pl.pallas_call(
        body, grid=(t // rb,),
        in_specs=[_row_spec(rb, d2), _full_spec((1, d)), _full_spec((GROUPS, CHUNK, CHUNK)), _full_spec((GROUPS, CHUNK))],
        out_specs=_row_spec(rb, d), out_shape=jax.ShapeDtypeStruct((t, d), BF16),
        scratch_shapes=[pltpu.VMEM((rb, d), BF16), pltpu.VMEM((rb, d), F32)],
        compiler_params=_params(("parallel",)), name=name,
    )(zp, vnorm, ws, bs)


def _b_mid_bwd(zp, dug, vnorm, ws, bs, *, name, rb=256):
    t, d2 = zp.shape
    d = d2 // 2
    c = d // GROUPS
    rb = _tile(t, rb, CHUNK)

    def body(zp_ref, dug_ref, gv_ref, ws_ref, bs_ref, dzp_ref, dws_ref, dbs_ref, dgv_ref,
             vn_ref, gate_ref, dm_ref, dvn_ref, dbacc_ref):
        i = pl.program_id(0)

        @pl.when(i == 0)
        def _():
            dws_ref[...] = jnp.zeros_like(dws_ref)
            dgv_ref[...] = jnp.zeros_like(dgv_ref)
            dbacc_ref[...] = jnp.zeros_like(dbacc_ref)

        zu, zv = zp_ref[:, :d], zp_ref[:, d:]
        u, v = _gelu(zu), _gelu(zv)
        rv = lax.rsqrt(jnp.mean(v * v, axis=-1, keepdims=True) + RMS_EPS)
        vhat = v * rv
        gv = gv_ref[...]
        vn_ref[...] = (vhat * gv).astype(BF16)
        dug_v = dug_ref[...]
        dm = dug_v * u
        dm_ref[...] = dm.astype(BF16)
        mask = _causal_mask()
        for h in range(GROUPS):
            hc = slice(h * c, (h + 1) * c)
            wm = jnp.where(mask, ws_ref[h], 0.0)
            wm_b, wmt_b = wm.astype(BF16), wm.T.astype(BF16)
            bcol = jnp.broadcast_to(bs_ref[h:h + 1, :], (CHUNK, CHUNK)).T[:, 0:1]
            dws_h = jnp.zeros((CHUNK, CHUNK), F32)
            dbs_h = jnp.zeros((CHUNK, c), F32)
            for n in range(rb // CHUNK):
                rows = slice(n * CHUNK, (n + 1) * CHUNK)
                vn_c, dm_c = vn_ref[rows, hc], dm_ref[rows, hc]
                gate_ref[rows, hc] = jnp.dot(wm_b, vn_c, preferred_element_type=F32) + bcol
                dws_h += lax.dot_general(dm_c, vn_c, (((1,), (1,)), ((), ())), preferred_element_type=F32)
                dvn_ref[rows, hc] = jnp.dot(wmt_b, dm_c, preferred_element_type=F32)
                dbs_h += dm[rows, hc]
            dws_ref[h] += dws_h
            dbacc_ref[h] += dbs_h
        du = dug_v * gate_ref[...]
        dvn = dvn_ref[...]
        dvhat = dvn * gv
        m = jnp.mean(dvhat * vhat, axis=-1, keepdims=True)
        dv = rv * (dvhat - vhat * m)
        dgv_ref[0:1, :] += jnp.sum(dvn * vhat, axis=0, keepdims=True)
        dzp_ref[:, :d] = (du * _gelu_grad(zu)).astype(BF16)
        dzp_ref[:, d:] = (dv * _gelu_grad(zv)).astype(BF16)

        @pl.when(i == pl.num_programs(0) - 1)
        def _():
            ones = jnp.ones((8, c), F32)
            for h in range(GROUPS):
                dws_ref[h] = jnp.where(mask, dws_ref[h], 0.0)
                row = lax.dot_general(ones, dbacc_ref[h], (((1,), (1,)), ((), ())),
                                      precision=lax.Precision.HIGHEST, preferred_element_type=F32)
                dbs_ref[h:h + 1, :] = row[0:1]

    return pl.pallas_call(
        body, grid=(t // rb,),
        in_specs=[_row_spec(rb, d2), _row_spec(rb, d), _full_spec((1, d)), _full_spec((GROUPS, CHUNK, CHUNK)),
                  _full_spec((GROUPS, CHUNK))],
        out_specs=[_row_spec(rb, d2), _full_spec((GROUPS, CHUNK, CHUNK)), _full_spec((GROUPS, CHUNK)), _full_spec((8, d))],
        out_shape=[jax.ShapeDtypeStruct((t, d2), BF16), jax.ShapeDtypeStruct((GROUPS, CHUNK, CHUNK), F32),
                   jax.ShapeDtypeStruct((GROUPS, CHUNK), F32), jax.ShapeDtypeStruct((8, d), F32)],
        scratch_shapes=[pltpu.VMEM((rb, d), BF16), pltpu.VMEM((rb, d), F32), pltpu.VMEM((rb, d), BF16),
                        pltpu.VMEM((rb, d), F32), pltpu.VMEM((GROUPS, CHUNK, c), F32)],
        compiler_params=_params(("arbitrary",)), name=name,
    )(zp, dug, vnorm, ws, bs)


def _cast_into_full(w, layer, kind, place, *, name, dep=None, rb=256):
    _, r, c = w.shape
    rb = _tile(r, rb, 16)
    nrb = r // rb
    full = (r, c * N_CHIPS) if kind == "col" else (r * N_CHIPS, c)

    def body(place_ref, w_ref, *rest):
        rest[-1][...] = w_ref[...].astype(BF16)

    def o_index(i, place):
        return (i, place[0]) if kind == "col" else (i + place[0] * nrb, 0)

    in_specs = [pl.BlockSpec((None, rb, c), lambda i, place: (layer, i, 0))] + ([ANY] if dep is not None else [])
    return pl.pallas_call(
        body,
        grid_spec=pltpu.PrefetchScalarGridSpec(num_scalar_prefetch=1, grid=(nrb,), in_specs=in_specs,
                                               out_specs=pl.BlockSpec((rb, c), o_index)),
        out_shape=jax.ShapeDtypeStruct(full, BF16), compiler_params=_params(("parallel",)), name=name,
    )(place, w, *(() if dep is None else (dep,)))


def _adamw_layer(w, g, m, v, layer, prev, *, name, rb=128):
    _, r, c = w.shape
    rb = _tile(r, rb, 8)
    c1 = 1.0 - ADAM_B1 ** ADAM_STEP
    c2 = 1.0 - ADAM_B2 ** ADAM_STEP

    def body(w_ref, g_ref, m_ref, v_ref, *rest):
        go_ref, d_ref, nm_ref, nv_ref = rest[-4:]
        gv = g_ref[...]
        nm = ADAM_B1 * m_ref[...] + (1.0 - ADAM_B1) * gv
        nv = ADAM_B2 * v_ref[...] + (1.0 - ADAM_B2) * (gv * gv)
        go_ref[...] = gv
        nm_ref[...] = nm
        nv_ref[...] = nv
        d_ref[...] = -ADAM_LR * ((nm / c1) / (jnp.sqrt(nv / c2) + ADAM_EPS) + ADAM_WD * w_ref[...])

    lay = pl.BlockSpec((None, rb, c), lambda i: (layer, i, 0))
    return pl.pallas_call(
        body, grid=(r // rb,), in_specs=[lay, _row_spec(rb, c), lay, lay] + ([ANY] * 4 if prev else []), out_specs=[lay] * 4,
        out_shape=[jax.ShapeDtypeStruct(w.shape, F32)] * 4, input_output_aliases={4 + k: k for k in range(4)} if prev else {},
        compiler_params=_params(("parallel",)), name=name,
    )(w, g, m, v, *(prev or ()))


HBM = pl.BlockSpec(memory_space=pltpu.HBM)
SEM = pl.BlockSpec(memory_space=pltpu.SEMAPHORE)
SIDE_EFFECT = pltpu.SideEffectType.DATAFLOW_SIDE_EFFECTING


def _place():
    x, y, c = lax.axis_index("x"), lax.axis_index("y"), lax.axis_index("c")
    chips = [(1 - x, y), (x, 1 - y), (1 - x, 1 - y)]
    return x, y, c, 2 * x + y, chips


def _half(ref, kind, c):
    r, w = ref.shape
    if kind == "col":
        return ref.at[pl.ds(pl.multiple_of(c * (r // 2), 8), r // 2), :]
    return ref.at[:, pl.ds(pl.multiple_of(c * (w // 2), 128), w // 2)]


def _shard(ref, kind, s):
    r, w = ref.shape
    if kind == "col":
        return ref.at[:, pl.ds(pl.multiple_of(s * (w // N_CHIPS), 128), w // N_CHIPS)]
    return ref.at[pl.ds(pl.multiple_of(s * (r // N_CHIPS), 8), r // N_CHIPS), :]


def _remote(src, dst, send_sem, recv_sem, dev):
    return pltpu.make_async_remote_copy(src_ref=src, dst_ref=dst, send_sem=send_sem, recv_sem=recv_sem,
                                        device_id=dev, device_id_type=MESH)


def _start(name, bufs, plan, sem_shape, dep=None):
    n = len(bufs)
    n_in = n + (dep is not None)

    def body(*refs):
        sends, _ = plan(refs[:n], refs[n_in], refs[n_in + 1])
        for cp in sends:
            cp.start()
        refs[n_in + 2 + n][...] = jnp.zeros((8, 128), F32)

    dma = pltpu.SemaphoreType.DMA
    outs = pl.pallas_call(
        body, name=name,
        out_shape=(dma(sem_shape), dma(sem_shape), *[pltpu.HBM(b.shape, b.dtype) for b in bufs], jax.ShapeDtypeStruct((8, 128), F32)),
        in_specs=(HBM,) * n + ((ANY,) if dep is not None else ()),
        out_specs=(SEM, SEM) + (HBM,) * n + (pl.BlockSpec(memory_space=pltpu.VMEM),),
        input_output_aliases={i: i + 2 for i in range(n)},
        compiler_params=pltpu.CompilerParams(has_side_effects=SIDE_EFFECT),
    )(*[pltpu.with_memory_space_constraint(b, pltpu.HBM) for b in bufs], *(() if dep is None else (dep,)))
    return outs[0], outs[1], list(outs[2:2 + n]), outs[2 + n]


def _wait(name, started, plan, after):
    send, recv, bufs, _ = started
    n = len(bufs)

    def body(*refs):
        sends, recvs = plan(refs[:n], refs[n], refs[n + 1])
        for cp in sends:
            cp.wait_send()
        for cp in recvs:
            cp.wait_recv()

    return list(pl.pallas_call(
        body, name=name, out_shape=tuple(pltpu.HBM(b.shape, b.dtype) for b in bufs),
        in_specs=(HBM,) * n + (SEM, SEM, ANY), out_specs=(HBM,) * n, input_output_aliases={i: i for i in range(n)},
        compiler_params=pltpu.CompilerParams(has_side_effects=SIDE_EFFECT),
    )(*bufs, send, recv, after))


KINDS = ("col", "row")


def _gather_ici_plan(n_small):
    def plan(refs, send, recv):
        x, y, c, s, chips = _place()
        n = len(KINDS) + n_small
        sends, recvs = [], []
        for k, (px, py) in enumerate(chips):
            sp = 2 * px + py
            for a, kind in enumerate(KINDS):
                mine, theirs = _half(_shard(refs[a], kind, s), kind, c), _half(_shard(refs[a], kind, sp), kind, c)
                sends.append(_remote(mine, mine, send.at[k * n + a], recv.at[k * n + a], (px, py, c)))
                recvs.append(_remote(theirs, theirs, send.at[k * n + a], recv.at[k * n + a], (px, py, c)))
            for b in range(n_small):
                ref, sem = refs[len(KINDS) + b], k * n + len(KINDS) + b
                sends.append(_remote(ref.at[s], ref.at[s], send.at[sem], recv.at[sem], (px, py, c)))
                recvs.append(_remote(ref.at[sp], ref.at[sp], send.at[sem], recv.at[sem], (px, py, c)))
        return sends, recvs
    return plan


def _gather_d2d_plan(refs, send, recv):
    x, y, c, _, chips = _place()
    n = len(KINDS)
    sends, recvs = [], []
    for k, (px, py) in enumerate(chips):
        for a, kind in enumerate(KINDS):
            region, sem = _shard(refs[a], kind, 2 * px + py), k * n + a
            sends.append(_remote(_half(region, kind, c), _half(region, kind, c), send.at[sem], recv.at[sem], (x, y, 1 - c)))
            recvs.append(_remote(_half(region, kind, 1 - c), _half(region, kind, 1 - c), send.at[sem], recv.at[sem], (x, y, 1 - c)))
    return sends, recvs


def _swap_plan(refs, send, recv):
    x, y, c, _, _ = _place()
    n = len(KINDS)
    cps = [_remote(_half(refs[a], KINDS[a], 1 - c), refs[n + a], send.at[a], recv.at[a], (x, y, 1 - c)) for a in range(n)]
    return cps, cps


def _exchange_plan(refs, send, recv):
    x, y, c, _, chips = _place()
    n = len(KINDS)
    cps = []
    for k, (px, py) in enumerate(chips):
        for a in range(n):
            cps.append(_remote(_shard(refs[a], KINDS[a], 2 * px + py), refs[n + a].at[k], send.at[k * n + a], recv.at[k * n + a],
                               (px, py, c)))
    return cps, cps


def _share_plan(refs, send, recv):
    x, y, c, _, _ = _place()
    sends = [_remote(_half(refs[a], KINDS[a], c), _half(refs[a], KINDS[a], c), send.at[a], recv.at[a], (x, y, 1 - c))
             for a in range(len(KINDS))]
    recvs = [_remote(_half(refs[a], KINDS[a], 1 - c), _half(refs[a], KINDS[a], 1 - c), send.at[a], recv.at[a], (x, y, 1 - c))
             for a in range(len(KINDS))]
    return sends, recvs


def _spread_plan(refs, send, recv):
    packed, slots = refs
    x, y, c, _, _ = _place()
    sends, recvs = [], []
    for k in range(1, 8):
        px, py, pc = x ^ (k >> 2), y ^ ((k >> 1) & 1), c ^ (k & 1)
        sends.append(_remote(packed, slots.at[4 * x + 2 * y + c], send.at[k - 1], recv.at[k - 1], (px, py, pc)))
        recvs.append(_remote(packed, slots.at[4 * px + 2 * py + pc], send.at[k - 1], recv.at[k - 1], (px, py, pc)))
    return sends, recvs


def _half_index(kind, nblk):
    def index(i, j, place):
        return (i + place[1] * nblk[0], j) if kind == "col" else (i, j + place[1] * nblk[1])
    return index


def _chip_partial(g, other, kind, place, *, name):
    hr, hc = other.shape
    rb, cb = _tile(hr, 512, 16), _tile(hc, 1024)
    nblk = (hr // rb, hc // cb)

    def body(place_ref, g_ref, o_ref, p_ref):
        p_ref[...] = (g_ref[...].astype(F32) + o_ref[...].astype(F32)).astype(BF16)

    plain = pl.BlockSpec((rb, cb), lambda i, j, place: (i, j))
    return pl.pallas_call(
        body,
        grid_spec=pltpu.PrefetchScalarGridSpec(
            num_scalar_prefetch=1, grid=nblk, in_specs=[pl.BlockSpec((rb, cb), _half_index(kind, nblk)), plain], out_specs=plain),
        out_shape=jax.ShapeDtypeStruct((hr, hc), BF16), compiler_params=_params(("parallel", "parallel")), name=name,
    )(place, g, other)


def _reduce_half(g, other, recv, kind, place, *, name):
    _, pr, pc = recv.shape
    rb, cb = _tile(pr, 512, 16), _tile(pc, 1024)
    nblk = (pr // rb, pc // cb)
    full = (pr * 2, pc) if kind == "col" else (pr, pc * 2)

    def g_index(i, j, place):
        s, c = place[0], place[1]
        return (i + c * nblk[0], j + s * nblk[1]) if kind == "col" else (i + s * nblk[0], j + c * nblk[1])

    def o_index(i, j, place):
        return (i, j + place[0] * nblk[1]) if kind == "col" else (i + place[0] * nblk[0], j)

    def body(place_ref, g_ref, o_ref, r_ref, out_ref):
        acc = g_ref[...].astype(F32) + o_ref[...].astype(F32)
        for k in range(3):
            acc = acc + r_ref[k].astype(F32)
        out_ref[...] = acc

    return pl.pallas_call(
        body,
        grid_spec=pltpu.PrefetchScalarGridSpec(
            num_scalar_prefetch=1, grid=nblk,
            in_specs=[pl.BlockSpec((rb, cb), g_index), pl.BlockSpec((rb, cb), o_index),
                      pl.BlockSpec((3, rb, cb), lambda i, j, place: (0, i, j))],
            out_specs=pl.BlockSpec((rb, cb), _half_index(kind, nblk))),
        out_shape=jax.ShapeDtypeStruct(full, F32), compiler_params=_params(("parallel", "parallel")), name=name,
    )(place, g, other, recv)


def _sum_slots(packed, slots, me, *, name, rb=512):
    r, w = packed.shape
    rb = _tile(r, rb, 8)

    def body(me_ref, p_ref, s_ref, o_ref):
        acc = None
        for j in range(8):
            term = jnp.where(me_ref[0] == j, p_ref[...], s_ref[j])
            acc = term if acc is None else acc + term
        o_ref[...] = acc

    return pl.pallas_call(
        body,
        grid_spec=pltpu.PrefetchScalarGridSpec(
            num_scalar_prefetch=1, grid=(r // rb,),
            in_specs=[pl.BlockSpec((rb, w), lambda i, me: (i, 0)), pl.BlockSpec((8, rb, w), lambda i, me: (0, i, 0))],
            out_specs=pl.BlockSpec((rb, w), lambda i, me: (i, 0))),
        out_shape=jax.ShapeDtypeStruct((r, w), F32), compiler_params=_params(("parallel",)), name=name,
    )(me, packed, slots)


def _half_shape(a, kind):
    return (a.shape[0] // 2, a.shape[1]) if kind == "col" else (a.shape[0], a.shape[1] // 2)


def _rs_swap(tag, grads):
    others = [lax.empty(_half_shape(g, k), g.dtype) for g, k in zip(grads, KINDS)]
    return _start(f"rs_{tag}_swap", list(grads) + others, _swap_plan, (len(KINDS),))


def _rs_exchange(tag, swapped, place, after):
    bufs = _wait(f"rs_{tag}_swap_wait", swapped, _swap_plan, after)
    n = len(KINDS)
    grads, others = bufs[:n], bufs[n:]
    parts = [_chip_partial(g, o, k, place, name=f"rs_{tag}_partial_{k}") for g, o, k in zip(grads, others, KINDS)]
    lands = []
    for p, k in zip(parts, KINDS):
        piece = (p.shape[0], p.shape[1] // N_CHIPS) if k == "col" else (p.shape[0] // N_CHIPS, p.shape[1])
        lands.append(lax.empty((3,) + piece, p.dtype))
    return _start(f"rs_{tag}_exchange", parts + lands, _exchange_plan, (3 * n,)), grads, others


def _rs_share(tag, exchanged, place, after):
    started, grads, others = exchanged
    n = len(KINDS)
    recvs = _wait(f"rs_{tag}_exchange_wait", started, _exchange_plan, after)[n:]
    halves = [_reduce_half(g, o, r, k, place, name=f"rs_{tag}_reduce_{k}") for g, o, r, k in zip(grads, others, recvs, KINDS)]
    return _start(f"rs_{tag}_share", halves, _share_plan, (n,))


def _rs_finish(tag, shared, after):
    return _wait(f"rs_{tag}_share_wait", shared, _share_plan, after)


def _spread(tag, parts, dep):
    rows = [p.reshape(-1, 128) for p in parts]
    n = sum(r.shape[0] for r in rows)
    rows.append(jnp.zeros(((-n) % 512, 128), F32))
    packed = jnp.concatenate(rows, axis=0)
    return _start(f"small_{tag}_spread", [packed, lax.empty((8,) + packed.shape, F32)], _spread_plan, (7,), dep=dep)


def _spread_sum(tag, started, parts, me, after):
    packed, slots = _wait(f"small_{tag}_spread_wait", started, _spread_plan, after)
    total = _sum_slots(packed, slots, me, name=f"small_{tag}_sum")
    out, row = [], 0
    for p in parts:
        n = p.size // 128
        out.append(total[row:row + n].reshape(p.shape))
        row += n
    return out


def _ffn_fwd(x, h, w_up, conv_w, conv_b, w_down, tag, tm_up=1024):
    up = _mm(h, w_up, tm=tm_up, out_dtype=BF16, name=f"ffn{tag}_up")
    act, conv = _ffn_mid_fwd(up, conv_w, conv_b, name=f"ffn{tag}_mid")
    x_out = _mm(act, w_down, res=x, tk=2816, name=f"ffn{tag}_down")
    return x_out, (up, conv, act)


def kernel(x, a_norm, a_in, a_conv, a_out, b_norm, b_in, b_vnorm, b_ws, b_bs, b_out, f_norm, f_up, f_conv_w, f_conv_b, f_down, final_norm, loss_target, m_a_norm, m_a_in, m_a_conv, m_a_out, m_b_norm, m_b_in, m_b_vnorm, m_b_ws, m_b_bs, m_b_out, m_f_norm, m_f_up, m_f_conv_w, m_f_conv_b, m_f_down, m_final_norm, v_a_norm, v_a_in, v_a_conv, v_a_out, v_b_norm, v_b_in, v_b_vnorm, v_b_ws, v_b_bs, v_b_out, v_f_norm, v_f_up, v_f_conv_w, v_f_conv_b, v_f_down, v_final_norm):
    t, d = x.shape[1], x.shape[2]
    f2 = f_up.shape[2] * N_CHIPS
    x0, tgt = x.reshape(t, d), loss_target.reshape(t, d)
    ax, ay, ac = lax.axis_index("x"), lax.axis_index("y"), lax.axis_index("c")
    s = 2 * ax + ay
    place = jnp.stack([s, ac]).astype(jnp.int32)
    me = (4 * ax + 2 * ay + ac).astype(jnp.int32).reshape(1)

    def stacked(a):
        return lax.dynamic_update_index_in_dim(jnp.zeros((N_CHIPS,) + a.shape, F32), a, s, 0)

    def cast_pair(tag, w_in, w_out, layer, dep):
        return [_cast_into_full(w_in, layer, "col", place, name=f"cast_{tag}_in", dep=dep),
                _cast_into_full(w_out, layer, "row", place, name=f"cast_{tag}_out", dep=dep)]

    def gather_start(tag, fulls, small, dep):
        return _start(f"ag_{tag}_ici", fulls + small, _gather_ici_plan(len(small)), (3 * (2 + len(small)),), dep=dep)

    def gather_forward(tag, started, n_small, after):
        bufs = _wait(f"ag_{tag}_ici_wait", started, _gather_ici_plan(n_small), after)
        return _start(f"ag_{tag}_d2d", bufs[:2], _gather_d2d_plan, (3 * 2,)), bufs[2:]

    def gather_finish(tag, forwarded, after):
        return _wait(f"ag_{tag}_d2d_wait", forwarded, _gather_d2d_plan, after)

    small = [stacked(a_conv[0]), stacked(b_norm), stacked(b_vnorm), stacked(f_conv_w.reshape(2 * 3, -1))]
    ag_a = gather_start("a", cast_pair("a", a_in, a_out, 0, None), small, None)
    full_f0 = cast_pair("f0", f_up, f_down, 0, ag_a[3])
    full_b = cast_pair("b", b_in, b_out, 0, full_f0[1])
    full_f1 = cast_pair("f1", f_up, f_down, 1, full_b[1])

    def unshard(a):
        return jnp.transpose(a, (1, 0, 2)).reshape(a.shape[1], -1)

    ws, bs = b_ws[0], b_bs[0]

    h0 = _rms_fwd(x0, a_norm, dep=full_f1[1], name="a_norm")
    fw_a, (g_aconv, g_bnorm, g_bvnorm, g_fconv) = gather_forward("a", ag_a, 4, h0)
    ag_f0 = gather_start("f0", full_f0, [], fw_a[3])
    ag_b = gather_start("b", full_b, [], ag_f0[3])
    ag_f1 = gather_start("f1", full_f1, [], ag_b[3])
    w_ai, w_ao = gather_finish("a", fw_a, ag_f1[3])
    a_conv_f, b_norm_f, b_vnorm_f = unshard(g_aconv), unshard(g_bnorm), unshard(g_bvnorm)
    f_conv_f = unshard(g_fconv).reshape(2, 3, f2)
    bcx = _mm(h0, w_ai, tm=2048, name="a_in")
    y = _a_mid_fwd(bcx, a_conv_f, name="a_mid")
    x1 = _mm(y, w_ao, res=x0, tm=512, tn=2048, name="a_out")
    fw_f0, _ = gather_forward("f0", ag_f0, 0, x1)
    h1 = _rms_fwd(x1, f_norm[0:1], dep=fw_f0[3], name="ffn0_norm")
    w_up0, w_dn0 = gather_finish("f0", fw_f0, h1)
    x2, (up0, conv0, act0) = _ffn_fwd(x1, h1, w_up0, f_conv_f[0], f_conv_b[0:1], w_dn0, 0, tm_up=2048)
    fw_b, _ = gather_forward("b", ag_b, 0, up0)
    w_bi, w_bo = gather_finish("b", fw_b, act0)
    h2 = _rms_fwd(x2, b_norm_f, name="b_norm")
    zp = _mm(h2, w_bi, tm=2048, name="b_in")
    fw_f1, _ = gather_forward("f1", ag_f1, 0, zp)
    ug = _b_mid_fwd(zp, b_vnorm_f, ws, bs, name="b_mid")
    x3, h3 = _mm(ug, w_bo, res=x2, norm=f_norm[1:2], tm=512, tn=2048, name="b_out")
    w_up1, w_dn1 = gather_finish("f1", fw_f1, x3)
    x4, (up1, conv1, act1) = _ffn_fwd(x3, h3, w_up1, f_conv_f[1], f_conv_b[1:2], w_dn1, 1, tm_up=2048)
    loss_rows, dx4, dx4b, d_final = _final(x4, tgt, final_norm.reshape(1, d), name="final")

    d_dn1 = _mm(act1, dx4b, ta=True, tm=1408, out_dtype=BF16, name="ffn1_ddown")
    dact1 = _mm(dx4b, w_dn1, tb=True, tn=512, tm=2048, name="ffn1_dact")
    dup1, d_fwb1 = _ffn_mid_bwd(up1, conv1, dact1, f_conv_f[1], name="ffn1_mid_bwd")
    d_up1 = _mm(h3, dup1, ta=True, out_dtype=BF16, tk=4096, name="ffn1_dup")
    sw_f1 = _rs_swap("f1", [d_up1, d_dn1])
    dh3 = _mm(dup1, w_up1, tb=True, tk=2816, out_dtype=BF16, dep=sw_f1[3], name="ffn1_dh")
    dx3, dx3b, d_fnorm1 = _rms_bwd(dh3, x3, f_norm[1:2], dx4, name="ffn1_norm_bwd")
    ex_f1 = _rs_exchange("f1", sw_f1, place, dx3)

    d_bo = _mm(ug, dx3b, ta=True, out_dtype=BF16, tk=4096, dep=ex_f1[0][3], name="b_dout")
    dug = _mm(dx3b, w_bo, tb=True, tm=512, tn=2048, name="b_dug")
    dzp, d_ws, d_bs, d_bvnorm = _b_mid_bwd(zp, dug, b_vnorm_f, ws, bs, name="b_mid_bwd")
    d_bi = _mm(h2, dzp, ta=True, out_dtype=BF16, tk=4096, name="b_din")
    sw_b = _rs_swap("b", [d_bi, d_bo])
    dh2 = _mm(dzp, w_bi, tb=True, tk=4096, out_dtype=BF16, dep=sw_b[3], name="b_dh")
    dx2, dx2b, d_bnorm = _rms_bwd(dh2, x2, b_norm_f, dx3, name="b_norm_bwd")
    ex_b = _rs_exchange("b", sw_b, place, dx2)

    d_dn0 = _mm(act0, dx2b, ta=True, tm=1408, out_dtype=BF16, dep=ex_b[0][3], name="ffn0_ddown")
    dact0 = _mm(dx2b, w_dn0, tb=True, tn=512, tm=2048, name="ffn0_dact")
    dup0, d_fwb0 = _ffn_mid_bwd(up0, conv0, dact0, f_conv_f[0], name="ffn0_mid_bwd")
    sh_f1 = _rs_share("f1", ex_f1, place, dup0)
    d_up0 = _mm(h1, dup0, ta=True, out_dtype=BF16, tk=4096, dep=sh_f1[3], name="ffn0_dup")
    sw_f0 = _rs_swap("f0", [d_up0, d_dn0])
    g_up1, g_dn1 = _rs_finish("f1", sh_f1, sw_f0[3])
    dh1 = _mm(dup0, w_up0, tb=True, tk=2816, out_dtype=BF16, dep=sw_f0[3], name="ffn0_dh")
    dx1, dx1b, d_fnorm0 = _rms_bwd(dh1, x1, f_norm[0:1], dx2, name="ffn0_norm_bwd")
    ex_f0 = _rs_exchange("f0", sw_f0, place, dx1)
    early = [jnp.concatenate([d_bnorm, d_bvnorm, d_fnorm0, d_fnorm1, d_final, loss_rows], axis=0),
             jnp.concatenate([d_fwb0, d_fwb1], axis=0), jnp.concatenate([d_ws.reshape(-1, CHUNK), d_bs], axis=0)]
    sp_early = _spread("early", early, ex_f0[0][3])
    sh_b = _rs_share("b", ex_b, place, sp_early[3])

    d_ao = _mm(y, dx1b, ta=True, out_dtype=BF16, tk=4096, dep=sh_b[3], name="a_dout")
    dyy = _mm(dx1b, w_ao, tb=True, tm=512, tn=2048, name="a_dy")
    dbcx, d_aconv = _a_mid_bwd(bcx, dyy, a_conv_f, name="a_mid_bwd")
    d_ai = _mm(h0, dbcx, ta=True, out_dtype=BF16, tk=4096, name="a_din")
    sw_a = _rs_swap("a", [d_ai, d_ao])
    g_bi, g_bo = _rs_finish("b", sh_b, sw_a[3])
    ex_a = _rs_exchange("a", sw_a, place, g_bi)
    dh0 = _mm(dbcx, w_ai, tb=True, tk=3072, out_dtype=BF16, dep=ex_a[0][3], name="a_dh")
    grad_x, _, d_anorm = _rms_bwd(dh0, x0, a_norm, dx1, name="a_norm_bwd")
    late = [jnp.concatenate([d_anorm, d_aconv], axis=0)]
    sp_late = _spread("late", late, ex_a[0][3])
    sh_f0 = _rs_share("f0", ex_f0, place, sp_late[3])
    sh_a = _rs_share("a", ex_a, place, sh_f0[3])
    g_up0, g_dn0 = _rs_finish("f0", sh_f0, sh_a[3])
    g_ai, g_ao = _rs_finish("a", sh_a, g_up0)
    r_a, r_b, r_c = _spread_sum("early", sp_early, early, me, g_ai)
    (r_l,) = _spread_sum("late", sp_late, late, me, r_a)

    loss = jnp.sum(r_a[40])
    cs, fs = d // N_CHIPS, f2 // N_CHIPS

    def mine(a, width):
        return lax.dynamic_slice_in_dim(a, s * width, width, axis=1)

    grads = {
        "a_norm": r_l[0:1], "a_conv": mine(r_l[8:11], cs), "b_norm": mine(r_a[0:1], cs), "b_vnorm": mine(r_a[8:9], cs),
        "f_norm": jnp.concatenate([r_a[16:17], r_a[24:25]], axis=0), "final_norm": r_a[32:33],
        "b_ws": r_c[:GROUPS * CHUNK], "b_bs": r_c[GROUPS * CHUNK:],
        "f_conv_w": jnp.concatenate([mine(r_b[0:3], fs), mine(r_b[8:11], fs)], axis=0),
        "f_conv_b": jnp.concatenate([r_b[3:4], r_b[11:12]], axis=0),
        "a_in": g_ai, "a_out": g_ao, "b_in": g_bi, "b_out": g_bo,
    }
    names = ["a_norm", "a_in", "a_conv", "a_out", "b_norm", "b_in", "b_vnorm", "b_ws", "b_bs", "b_out", "f_norm", "f_up",
             "f_conv_w", "f_conv_b", "f_down", "final_norm"]
    weights = dict(zip(names, [a_norm, a_in, a_conv, a_out, b_norm, b_in, b_vnorm, b_ws, b_bs, b_out, f_norm, f_up, f_conv_w,
                               f_conv_b, f_down, final_norm]))
    ms = dict(zip(names, [m_a_norm, m_a_in, m_a_conv, m_a_out, m_b_norm, m_b_in, m_b_vnorm, m_b_ws, m_b_bs, m_b_out, m_f_norm,
                          m_f_up, m_f_conv_w, m_f_conv_b, m_f_down, m_final_norm]))
    vs = dict(zip(names, [v_a_norm, v_a_in, v_a_conv, v_a_out, v_b_norm, v_b_in, v_b_vnorm, v_b_ws, v_b_bs, v_b_out, v_f_norm,
                          v_f_up, v_f_conv_w, v_f_conv_b, v_f_down, v_final_norm]))
    result = {}
    for n in names:
        w = weights[n]
        if n in ("f_up", "f_down"):
            g1, g0 = (g_up1, g_up0) if n == "f_up" else (g_dn1, g_dn0)
            first = _adamw_layer(w, g1, ms[n], vs[n], 1, None, name=f"adamw_{n}1")
            result[n] = _adamw_layer(w, g0, ms[n], vs[n], 0, tuple(first), name=f"adamw_{n}0")
            continue
        g2 = grads[n]
        as3d = (lambda a: a.reshape((1,) + g2.shape))
        result[n] = [o.reshape(w.shape) for o in _adamw_layer(as3d(w), g2, as3d(ms[n]), as3d(vs[n]), 0, None, name=f"adamw_{n}")]

    return (loss, grad_x.reshape(x.shape), *[result[n][0] for n in names], *[result[n][1] for n in names],
            *[result[n][2] for n in names], *[result[n][3] for n in names])
```

```python
import functools

import jax
import jax.numpy as jnp
from jax import lax
from jax.experimental import pallas as pl
from jax.experimental.pallas import tpu as pltpu

F32 = jnp.float32
BF16 = jnp.bfloat16
MESH = pl.DeviceIdType.MESH
ANY = pl.BlockSpec(memory_space=pl.ANY)

RMS_EPS = 1e-5
CHUNK = 128
GROUPS = 8
ADAM_LR, ADAM_B1, ADAM_B2, ADAM_EPS, ADAM_WD, ADAM_STEP = 0.001, 0.9, 0.999, 1e-08, 0.01, 10

N_CHIPS = 4
HALO = 8
BF16_HALO = 16
VMEM_LIMIT = 56 * 1024 * 1024
GELU_C = 0.7978845608028654
GELU_A = 0.044715


def _params(sem=None):
    return pltpu.CompilerParams(dimension_semantics=sem, vmem_limit_bytes=VMEM_LIMIT)


def _tile(dim, pref, quantum=128):
    if dim <= pref:
        return dim
    t = (pref // quantum) * quantum
    while t >= quantum:
        if dim % t == 0:
            return t
        t -= quantum
    return dim


def _mm(a, b, *, name, ta=False, tb=False, res=None, norm=None, dep=None, out_dtype=F32, tm=1024, tn=1024, tk=2048):
    (K, M) = a.shape if ta else a.shape[::-1]
    N = b.shape[0] if tb else b.shape[1]
    assert (b.shape[1] if tb else b.shape[0]) == K
    tm, tn, tk = _tile(M, tm), _tile(N, tn), _tile(K, tk)
    nk = K // tk
    assert norm is None or (tn == N and nk == 1)
    a_spec = pl.BlockSpec((tk, tm), lambda i, j, k: (k, i)) if ta else pl.BlockSpec((tm, tk), lambda i, j, k: (i, k))
    b_spec = pl.BlockSpec((tn, tk), lambda i, j, k: (j, k)) if tb else pl.BlockSpec((tk, tn), lambda i, j, k: (k, j))
    o_spec = pl.BlockSpec((tm, tn), lambda i, j, k: (i, j))
    dims = (((0 if ta else 1,), (1 if tb else 0,)), ((), ()))
    direct = out_dtype == F32
    n_in = 2 + (res is not None) + (norm is not None) + (dep is not None)

    def body(*refs):
        a_ref, b_ref = refs[0], refs[1]
        r_ref = refs[2] if res is not None else None
        g_ref = refs[2 + (res is not None)] if norm is not None else None
        o_ref = refs[n_in]
        acc_ref = o_ref if direct else refs[-1]
        part = lax.dot_general(a_ref[...], b_ref[...], dims, preferred_element_type=F32)
        if nk == 1:
            if r_ref is not None:
                part = part + r_ref[...]
            o_ref[...] = part.astype(o_ref.dtype)
            if g_ref is not None:
                r = lax.rsqrt(jnp.mean(part * part, axis=-1, keepdims=True) + RMS_EPS)
                refs[n_in + 1][...] = ((part * r) * g_ref[...]).astype(BF16)
            return
        k = pl.program_id(2)

        @pl.when(k == 0)
        def _():
            acc_ref[...] = part

        @pl.when(jnp.logical_and(k > 0, k < nk - 1))
        def _():
            acc_ref[...] += part

        @pl.when(k == nk - 1)
        def _():
            tot = acc_ref[...] + part
            if r_ref is not None:
                tot = tot + r_ref[...]
            o_ref[...] = tot.astype(o_ref.dtype)

    in_specs = ([a_spec, b_spec] + ([o_spec] if res is not None else [])
                + ([pl.BlockSpec((1, tn), lambda i, j, k: (0, j))] if norm is not None else []) + ([ANY] if dep is not None else []))
    args = (a, b) + tuple(x for x in (res, norm, dep) if x is not None)
    scratch = [] if (direct or nk == 1) else [pltpu.VMEM((tm, tn), F32)]
    out_shape = jax.ShapeDtypeStruct((M, N), out_dtype)
    return pl.pallas_call(
        body, grid=(M // tm, N // tn, nk), in_specs=in_specs, out_specs=[o_spec, o_spec] if norm is not None else o_spec,
        out_shape=[out_shape, jax.ShapeDtypeStruct((M, N), BF16)] if norm is not None else out_shape, scratch_shapes=scratch,
        compiler_params=_params(("parallel", "parallel", "arbitrary")), name=name,
    )(*args)


def _row_spec(rb, w):
    return pl.BlockSpec((rb, w), lambda i: (i, 0))


def _prev_spec(rb, w):
    return pl.BlockSpec((HALO, w), lambda i: (jnp.maximum(i * (rb // HALO) - 1, 0), 0))


def _next_spec(rb, w, t):
    return pl.BlockSpec((HALO, w), lambda i: (jnp.minimum((i + 1) * (rb // HALO), t // HALO - 1), 0))


def _prev_spec16(rb, w):
    return pl.BlockSpec((BF16_HALO, w), lambda i: (jnp.maximum(i * (rb // BF16_HALO) - 1, 0), 0))


def _next_spec16(rb, w, t):
    return pl.BlockSpec((BF16_HALO, w), lambda i: (jnp.minimum((i + 1) * (rb // BF16_HALO), t // BF16_HALO - 1), 0))


def _full_spec(shape):
    return pl.BlockSpec(shape, lambda i: tuple(0 for _ in shape))


def _shift(e, s):
    return pltpu.roll(e, s % e.shape[0], 0)


def _gelu(x):
    return 0.5 * x * (1.0 + jnp.tanh(GELU_C * (x + GELU_A * x * x * x)))


def _gelu_grad(x):
    th = jnp.tanh(GELU_C * (x + GELU_A * x * x * x))
    return 0.5 * (1.0 + th) + 0.5 * x * (1.0 - th * th) * (GELU_C * (1.0 + 3.0 * GELU_A * x * x))


def _sigmoid(x):
    return 1.0 / (1.0 + jnp.exp(-x))


def _rms_fwd(x, g, *, name, dep=None, rb=256):
    t, d = x.shape
    rb = _tile(t, rb, 8)

    def body(x_ref, g_ref, *rest):
        h_ref = rest[-1]
        xv = x_ref[...]
        r = lax.rsqrt(jnp.mean(xv * xv, axis=-1, keepdims=True) + RMS_EPS)
        h_ref[...] = ((xv * r) * g_ref[...]).astype(BF16)

    return pl.pallas_call(
        body, grid=(t // rb,), in_specs=[_row_spec(rb, d), _full_spec((1, d))] + ([ANY] if dep is not None else []),
        out_specs=_row_spec(rb, d), out_shape=jax.ShapeDtypeStruct((t, d), BF16), compiler_params=_params(("parallel",)), name=name,
    )(x, g, *(() if dep is None else (dep,)))


def _rms_bwd(dh, x, g, dres, *, name, dep=None, rb=256):
    t, d = x.shape
    rb = _tile(t, rb, 8)

    def body(dh_ref, x_ref, g_ref, dres_ref, *rest):
        dx_ref, dxb_ref, dg_ref = rest[-3:]
        xv = x_ref[...]
        r = lax.rsqrt(jnp.mean(xv * xv, axis=-1, keepdims=True) + RMS_EPS)
        xhat = xv * r
        dh_v = dh_ref[...].astype(F32)
        dxhat = dh_v * g_ref[...]
        m = jnp.mean(dxhat * xhat, axis=-1, keepdims=True)
        dx = dres_ref[...] + r * (dxhat - xhat * m)
        dx_ref[...] = dx
        dxb_ref[...] = dx.astype(BF16)

        @pl.when(pl.program_id(0) == 0)
        def _():
            dg_ref[...] = jnp.zeros_like(dg_ref)

        dg_ref[0:1, :] += jnp.sum(dh_v * xhat, axis=0, keepdims=True)

    return pl.pallas_call(
        body, grid=(t // rb,),
        in_specs=[_row_spec(rb, d), _row_spec(rb, d), _full_spec((1, d)), _row_spec(rb, d)] + ([ANY] if dep is not None else []),
        out_specs=[_row_spec(rb, d), _row_spec(rb, d), _full_spec((8, d))],
        out_shape=[jax.ShapeDtypeStruct((t, d), F32), jax.ShapeDtypeStruct((t, d), BF16), jax.ShapeDtypeStruct((8, d), F32)],
        compiler_params=_params(("arbitrary",)), name=name,
    )(dh, x, g, dres, *(() if dep is None else (dep,)))


def _final(x, tgt, g, *, name, rb=256):
    t, d = x.shape
    rb = _tile(t, rb, 8)
    inv_d = 1.0 / d

    def body(x_ref, t_ref, g_ref, l_ref, dx_ref, dxb_ref, dg_ref):
        xv = x_ref[...]
        gv = g_ref[...]
        r = lax.rsqrt(jnp.mean(xv * xv, axis=-1, keepdims=True) + RMS_EPS)
        xhat = xv * r
        e = xhat * gv - t_ref[...]
        dy = e * inv_d
        dxhat = dy * gv
        m = jnp.mean(dxhat * xhat, axis=-1, keepdims=True)
        dx = r * (dxhat - xhat * m)
        dx_ref[...] = dx
        dxb_ref[...] = dx.astype(BF16)

        @pl.when(pl.program_id(0) == 0)
        def _():
            l_ref[...] = jnp.zeros_like(l_ref)
            dg_ref[...] = jnp.zeros_like(dg_ref)

        l_ref[0:1, :] += jnp.sum(e * e, axis=0, keepdims=True) * (0.5 * inv_d)
        dg_ref[0:1, :] += jnp.sum(dy * xhat, axis=0, keepdims=True)

    return pl.pallas_call(
        body, grid=(t // rb,),
        in_specs=[_row_spec(rb, d), _row_spec(rb, d), _full_spec((1, d))],
        out_specs=[_full_spec((8, d)), _row_spec(rb, d), _row_spec(rb, d), _full_spec((8, d))],
        out_shape=[jax.ShapeDtypeStruct((8, d), F32), jax.ShapeDtypeStruct((t, d), F32),
                   jax.ShapeDtypeStruct((t, d), BF16), jax.ShapeDtypeStruct((8, d), F32)],
        compiler_params=_params(("arbitrary",)), name=name,
    )(x, tgt, g)


def _a_mid_fwd(bcx, wconv, *, name, rb=256, cw=512):
    t, d3 = bcx.shape
    d = d3 // 3
    rb, cw = _tile(t, rb, 16), _tile(d, cw)

    def body(cur_ref, prev_ref, w_ref, y_ref):
        first = pl.program_id(0) == 0

        def f32(ref, cols):
            return ref[:, cols].astype(F32)

        for c0 in range(0, d, cw):
            cs = slice(c0, c0 + cw)
            gc, xs = slice(d + c0, d + c0 + cw), slice(2 * d + c0, 2 * d + c0 + cw)
            p_prev = jnp.where(first, 0.0, (f32(prev_ref, gc) * f32(prev_ref, xs))[BF16_HALO - HALO:])
            e = jnp.concatenate([p_prev, f32(cur_ref, gc) * f32(cur_ref, xs)], axis=0)
            w = w_ref[:, cs]
            q = w[0:1] * _shift(e, 2) + w[1:2] * _shift(e, 1) + w[2:3] * e
            y_ref[:, cs] = (f32(cur_ref, cs) * q[HALO:]).astype(BF16)

    return pl.pallas_call(
        body, grid=(t // rb,),
        in_specs=[_row_spec(rb, d3), _prev_spec16(rb, d3), _full_spec((3, d))], out_specs=_row_spec(rb, d),
        out_shape=jax.ShapeDtypeStruct((t, d), BF16), compiler_params=_params(("parallel",)), name=name,
    )(bcx, bcx, wconv)


def _a_mid_bwd(bcx, dy, wconv, *, name, rb=128, cw=512):
    t, d3 = bcx.shape
    d = d3 // 3
    rb, cw = _tile(t, rb, 16), _tile(d, cw)

    def body(cur_ref, prev_ref, next_ref, dy_ref, dyn_ref, w_ref, o_ref, dw_ref):
        i = pl.program_id(0)
        first, last = i == 0, i == pl.num_programs(0) - 1

        @pl.when(first)
        def _():
            dw_ref[...] = jnp.zeros_like(dw_ref)

        def f32(ref, cols):
            return ref[:, cols].astype(F32)

        for c0 in range(0, d, cw):
            cs = slice(c0, c0 + cw)
            gc, xs = slice(d + c0, d + c0 + cw), slice(2 * d + c0, 2 * d + c0 + cw)
            zeros = jnp.zeros((HALO, cw), F32)
            gb_c, gc_c, xs_c = f32(cur_ref, cs), f32(cur_ref, gc), f32(cur_ref, xs)
            p_prev = jnp.where(first, 0.0, (f32(prev_ref, gc) * f32(prev_ref, xs))[BF16_HALO - HALO:])
            e = jnp.concatenate([p_prev, gc_c * xs_c, zeros], axis=0)
            dq_next = jnp.where(last, 0.0, dyn_ref[:, cs] * f32(next_ref, cs)[:HALO])
            dy_c = dy_ref[:, cs]
            dq = jnp.concatenate([zeros, dy_c * gb_c, dq_next], axis=0)
            w = w_ref[:, cs]
            e1, e2 = _shift(e, 1), _shift(e, 2)
            q = w[0:1] * e2 + w[1:2] * e1 + w[2:3] * e
            dp = (w[2:3] * dq + w[1:2] * _shift(dq, -1) + w[0:1] * _shift(dq, -2))[HALO:HALO + rb]
            o_ref[:, cs] = (dy_c * q[HALO:HALO + rb]).astype(BF16)
            o_ref[:, gc] = (dp * xs_c).astype(BF16)
            o_ref[:, xs] = (dp * gc_c).astype(BF16)
            dq_c = dq[HALO:HALO + rb]
            dw_ref[0:1, cs] += jnp.sum(dq_c * e2[HALO:HALO + rb], axis=0, keepdims=True)
            dw_ref[1:2, cs] += jnp.sum(dq_c * e1[HALO:HALO + rb], axis=0, keepdims=True)
            dw_ref[2:3, cs] += jnp.sum(dq_c * e[HALO:HALO + rb], axis=0, keepdims=True)

    return pl.pallas_call(
        body, grid=(t // rb,),
        in_specs=[_row_spec(rb, d3), _prev_spec16(rb, d3), _next_spec16(rb, d3, t), _row_spec(rb, d), _next_spec(rb, d, t),
                  _full_spec((3, d))],
        out_specs=[_row_spec(rb, d3), _full_spec((8, d))],
        out_shape=[jax.ShapeDtypeStruct((t, d3), BF16), jax.ShapeDtypeStruct((8, d), F32)],
        compiler_params=_params(("arbitrary",)), name=name,
    )(bcx, bcx, bcx, dy, dy, wconv)


def _ffn_mid_fwd(up, wconv, bconv, *, name, rb=128, cw=512):
    t, f2 = up.shape
    f = f2 // 2
    rb, cw = _tile(t, rb, 16), _tile(f, cw)

    def body(cur_ref, prev_ref, w_ref, b_ref, act_ref, conv_ref):
        first = pl.program_id(0) == 0

        def conv(cols):
            prev = prev_ref[:, cols].astype(F32)[BF16_HALO - HALO:]
            e = jnp.concatenate([jnp.where(first, 0.0, prev), cur_ref[:, cols].astype(F32)], axis=0)
            w = w_ref[:, cols]
            out = (w[0:1] * _shift(e, 2) + w[1:2] * _shift(e, 1) + w[2:3] * e + b_ref[:, cols])[HALO:]
            conv_ref[:, cols] = out.astype(BF16)
            return out

        for c0 in range(0, f, cw):
            g = conv(slice(c0, c0 + cw))
            a = conv(slice(f + c0, f + c0 + cw))
            act_ref[:, c0:c0 + cw] = (g * _sigmoid(g) * a).astype(BF16)

    return pl.pallas_call(
        body, grid=(t // rb,),
        in_specs=[_row_spec(rb, f2), _prev_spec16(rb, f2), _full_spec((3, f2)), _full_spec((1, f2))],
        out_specs=[_row_spec(rb, f), _row_spec(rb, f2)],
        out_shape=[jax.ShapeDtypeStruct((t, f), BF16), jax.ShapeDtypeStruct((t, f2), BF16)],
        compiler_params=_params(("parallel",)), name=name,
    )(up, up, wconv, bconv)


def _ffn_mid_bwd(up, conv, dact, wconv, *, name, rb=128, cw=512):
    t, f2 = up.shape
    f = f2 // 2
    rb, cw = _tile(t, rb, 16), _tile(f, cw)

    def body(up_ref, conv_ref, convn_ref, da_ref, dan_ref, w_ref, o_ref, dwb_ref):
        i = pl.program_id(0)
        last = i == pl.num_programs(0) - 1

        @pl.when(i == 0)
        def _():
            dwb_ref[...] = jnp.zeros_like(dwb_ref)

        def rows(cols):
            return jnp.concatenate([conv_ref[:, cols].astype(F32), convn_ref[:, cols].astype(F32)[0:HALO]], axis=0)

        def back(dc, cols):
            w = w_ref[:, cols]
            dc1, dc2 = _shift(dc, -1)[:rb], _shift(dc, -2)[:rb]
            dc0 = dc[:rb]
            o_ref[:, cols] = (w[2:3] * dc0 + w[1:2] * dc1 + w[0:1] * dc2).astype(BF16)
            u = up_ref[:, cols].astype(F32)
            dwb_ref[0:1, cols] += jnp.sum(dc2 * u, axis=0, keepdims=True)
            dwb_ref[1:2, cols] += jnp.sum(dc1 * u, axis=0, keepdims=True)
            dwb_ref[2:3, cols] += jnp.sum(dc0 * u, axis=0, keepdims=True)
            dwb_ref[3:4, cols] += jnp.sum(dc0, axis=0, keepdims=True)

        for c0 in range(0, f, cw):
            gcols, acols = slice(c0, c0 + cw), slice(f + c0, f + c0 + cw)
            g, a = rows(gcols), rows(acols)
            da = jnp.concatenate([da_ref[:, gcols], jnp.where(last, 0.0, dan_ref[:, gcols])], axis=0)
            sg = _sigmoid(g)
            back(da * a * (sg * (1.0 + g * (1.0 - sg))), gcols)
            back(da * (g * sg), acols)

    return pl.pallas_call(
        body, grid=(t // rb,),
        in_specs=[_row_spec(rb, f2), _row_spec(rb, f2), _next_spec16(rb, f2, t), _row_spec(rb, f), _next_spec(rb, f, t),
                  _full_spec((3, f2))],
        out_specs=[_row_spec(rb, f2), _full_spec((8, f2))],
        out_shape=[jax.ShapeDtypeStruct((t, f2), BF16), jax.ShapeDtypeStruct((8, f2), F32)],
        compiler_params=_params(("arbitrary",)), name=name,
    )(up, conv, conv, dact, dact, wconv)


def _causal_mask():
    return lax.broadcasted_iota(jnp.int32, (CHUNK, CHUNK), 0) >= lax.broadcasted_iota(jnp.int32, (CHUNK, CHUNK), 1)


def _b_mid_fwd(zp, vnorm, ws, bs, *, name, rb=256):
    t, d2 = zp.shape
    d = d2 // 2
    c = d // GROUPS
    rb = _tile(t, rb, CHUNK)

    def body(zp_ref, gv_ref, ws_ref, bs_ref, ug_ref, vn_ref, gate_ref):
        v = _gelu(zp_ref[:, d:].astype(F32))
        rv = lax.rsqrt(jnp.mean(v * v, axis=-1, keepdims=True) + RMS_EPS)
        vn_ref[...] = ((v * rv) * gv_ref[...]).astype(BF16)
        mask = _causal_mask()
        for h in range(GROUPS):
            hc = slice(h * c, (h + 1) * c)
            wm = jnp.where(mask, ws_ref[h], 0.0).astype(BF16)
            bcol = jnp.broadcast_to(bs_ref[h:h + 1, :], (CHUNK, CHUNK)).T[:, 0:1]
            for n in range(rb // CHUNK):
                rows = slice(n * CHUNK, (n + 1) * CHUNK)
                gate_ref[rows, hc] = jnp.dot(wm, vn_ref[rows, hc], preferred_element_type=F32) + bcol
        ug_ref[...] = (_gelu(zp_ref[:, :d].astype(F32)) * gate_ref[...]).astype(BF16)

    return pl.pallas_call(
        body, grid=(t // rb,),
        in_specs=[_row_spec(rb, d2), _full_spec((1, d)), _full_spec((GROUPS, CHUNK, CHUNK)), _full_spec((GROUPS, CHUNK))],
        out_specs=_row_spec(rb, d), out_shape=jax.ShapeDtypeStruct((t, d), BF16),
        scratch_shapes=[pltpu.VMEM((rb, d), BF16), pltpu.VMEM((rb, d), F32)],
        compiler_params=_params(("parallel",)), name=name,
    )(zp, vnorm, ws, bs)


def _b_mid_bwd(zp, dug, vnorm, ws, bs, *, name, rb=256):
    t, d2 = zp.shape
    d = d2 // 2
    c = d // GROUPS
    rb = _tile(t, rb, CHUNK)

    def body(zp_ref, dug_ref, gv_ref, ws_ref, bs_ref, dzp_ref, dws_ref, dbs_ref, dgv_ref,
             vn_ref, gate_ref, dm_ref, dvn_ref, dbacc_ref):
        i = pl.program_id(0)

        @pl.when(i == 0)
        def _():
            dws_ref[...] = jnp.zeros_like(dws_ref)
            dgv_ref[...] = jnp.zeros_like(dgv_ref)
            dbacc_ref[...] = jnp.zeros_like(dbacc_ref)

        zu, zv = zp_ref[:, :d].astype(F32), zp_ref[:, d:].astype(F32)
        u, v = _gelu(zu), _gelu(zv)
        rv = lax.rsqrt(jnp.mean(v * v, axis=-1, keepdims=True) + RMS_EPS)
        vhat = v * rv
        gv = gv_ref[...]
        vn_ref[...] = (vhat * gv).astype(BF16)
        dug_v = dug_ref[...].astype(F32)
        dm = dug_v * u
        dm_ref[...] = dm.astype(BF16)
        mask = _causal_mask()
        for h in range(GROUPS):
            hc = slice(h * c, (h + 1) * c)
            wm = jnp.where(mask, ws_ref[h], 0.0)
            wm_b, wmt_b = wm.astype(BF16), wm.T.astype(BF16)
            bcol = jnp.broadcast_to(bs_ref[h:h + 1, :], (CHUNK, CHUNK)).T[:, 0:1]
            dws_h = jnp.zeros((CHUNK, CHUNK), F32)
            dbs_h = jnp.zeros((CHUNK, c), F32)
            for n in range(rb // CHUNK):
                rows = slice(n * CHUNK, (n + 1) * CHUNK)
                vn_c, dm_c = vn_ref[rows, hc], dm_ref[rows, hc]
                gate_ref[rows, hc] = jnp.dot(wm_b, vn_c, preferred_element_type=F32) + bcol
                dws_h += lax.dot_general(dm_c, vn_c, (((1,), (1,)), ((), ())), preferred_element_type=F32)
                dvn_ref[rows, hc] = jnp.dot(wmt_b, dm_c, preferred_element_type=F32)
                dbs_h += dm[rows, hc]
            dws_ref[h] += dws_h
            dbacc_ref[h] += dbs_h
        du = dug_v * gate_ref[...]
        dvn = dvn_ref[...]
        dvhat = dvn * gv
        m = jnp.mean(dvhat * vhat, axis=-1, keepdims=True)
        dv = rv * (dvhat - vhat * m)
        dgv_ref[0:1, :] += jnp.sum(dvn * vhat, axis=0, keepdims=True)
        dzp_ref[:, :d] = (du * _gelu_grad(zu)).astype(BF16)
        dzp_ref[:, d:] = (dv * _gelu_grad(zv)).astype(BF16)

        @pl.when(i == pl.num_programs(0) - 1)
        def _():
            ones = jnp.ones((8, c), F32)
            for h in range(GROUPS):
                dws_ref[h] = jnp.where(mask, dws_ref[h], 0.0)
                row = lax.dot_general(ones, dbacc_ref[h], (((1,), (1,)), ((), ())),
                                      precision=lax.Precision.HIGHEST, preferred_element_type=F32)
                dbs_ref[h:h + 1, :] = row[0:1]

    return pl.pallas_call(
        body, grid=(t // rb,),
        in_specs=[_row_spec(rb, d2), _row_spec(rb, d), _full_spec((1, d)), _full_spec((GROUPS, CHUNK, CHUNK)),
                  _full_spec((GROUPS, CHUNK))],
        out_specs=[_row_spec(rb, d2), _full_spec((GROUPS, CHUNK, CHUNK)), _full_spec((GROUPS, CHUNK)), _full_spec((8, d))],
        out_shape=[jax.ShapeDtypeStruct((t, d2), BF16), jax.ShapeDtypeStruct((GROUPS, CHUNK, CHUNK), F32),
                   jax.ShapeDtypeStruct((GROUPS, CHUNK), F32), jax.ShapeDtypeStruct((8, d), F32)],
        scratch_shapes=[pltpu.VMEM((rb, d), BF16), pltpu.VMEM((rb, d), F32), pltpu.VMEM((rb, d), BF16),
                        pltpu.VMEM((rb, d), F32), pltpu.VMEM((GROUPS, CHUNK, c), F32)],
        compiler_params=_params(("arbitrary",)), name=name,
    )(zp, dug, vnorm, ws, bs)


def _cast_into_full(w, layer, kind, place, *, name, dep=None, rb=256):
    _, r, c = w.shape
    rb = _tile(r, rb, 16)
    nrb = r // rb
    full = (r, c * N_CHIPS) if kind == "col" else (r * N_CHIPS, c)

    def body(place_ref, w_ref, *rest):
        rest[-1][...] = w_ref[...].astype(BF16)

    def o_index(i, place):
        return (i, place[0]) if kind == "col" else (i + place[0] * nrb, 0)

    in_specs = [pl.BlockSpec((None, rb, c), lambda i, place: (layer, i, 0))] + ([ANY] if dep is not None else [])
    return pl.pallas_call(
        body,
        grid_spec=pltpu.PrefetchScalarGridSpec(num_scalar_prefetch=1, grid=(nrb,), in_specs=in_specs,
                                               out_specs=pl.BlockSpec((rb, c), o_index)),
        out_shape=jax.ShapeDtypeStruct(full, BF16), compiler_params=_params(("parallel",)), name=name,
    )(place, w, *(() if dep is None else (dep,)))


def _adamw_layer(w, g, m, v, layer, prev, *, name, rb=128):
    _, r, c = w.shape
    rb = _tile(r, rb, 8)
    c1 = 1.0 - ADAM_B1 ** ADAM_STEP
    c2 = 1.0 - ADAM_B2 ** ADAM_STEP

    def body(w_ref, g_ref, m_ref, v_ref, *rest):
        go_ref, d_ref, nm_ref, nv_ref = rest[-4:]
        gv = g_ref[...]
        nm = ADAM_B1 * m_ref[...] + (1.0 - ADAM_B1) * gv
        nv = ADAM_B2 * v_ref[...] + (1.0 - ADAM_B2) * (gv * gv)
        go_ref[...] = gv
        nm_ref[...] = nm
        nv_ref[...] = nv
        d_ref[...] = -ADAM_LR * ((nm / c1) / (jnp.sqrt(nv / c2) + ADAM_EPS) + ADAM_WD * w_ref[...])

    lay = pl.BlockSpec((None, rb, c), lambda i: (layer, i, 0))
    return pl.pallas_call(
        body, grid=(r // rb,), in_specs=[lay, _row_spec(rb, c), lay, lay] + ([ANY] * 4 if prev else []), out_specs=[lay] * 4,
        out_shape=[jax.ShapeDtypeStruct(w.shape, F32)] * 4, input_output_aliases={4 + k: k for k in range(4)} if prev else {},
        compiler_params=_params(("parallel",)), name=name,
    )(w, g, m, v, *(prev or ()))


HBM = pl.BlockSpec(memory_space=pltpu.HBM)
SEM = pl.BlockSpec(memory_space=pltpu.SEMAPHORE)
SIDE_EFFECT = pltpu.SideEffectType.DATAFLOW_SIDE_EFFECTING


def _place():
    x, y, c = lax.axis_index("x"), lax.axis_index("y"), lax.axis_index("c")
    chips = [(1 - x, y), (x, 1 - y), (1 - x, 1 - y)]
    return x, y, c, 2 * x + y, chips


def _half(ref, kind, c):
    r, w = ref.shape
    if kind == "col":
        return ref.at[pl.ds(pl.multiple_of(c * (r // 2), 8), r // 2), :]
    return ref.at[:, pl.ds(pl.multiple_of(c * (w // 2), 128), w // 2)]


def _shard(ref, kind, s):
    r, w = ref.shape
    if kind == "col":
        return ref.at[:, pl.ds(pl.multiple_of(s * (w // N_CHIPS), 128), w // N_CHIPS)]
    return ref.at[pl.ds(pl.multiple_of(s * (r // N_CHIPS), 8), r // N_CHIPS), :]


def _remote(src, dst, send_sem, recv_sem, dev):
    return pltpu.make_async_remote_copy(src_ref=src, dst_ref=dst, send_sem=send_sem, recv_sem=recv_sem,
                                        device_id=dev, device_id_type=MESH)


def _start(name, bufs, plan, sem_shape, dep=None):
    n = len(bufs)
    n_in = n + (dep is not None)

    def body(*refs):
        sends, _ = plan(refs[:n], refs[n_in], refs[n_in + 1])
        for cp in sends:
            cp.start()
        refs[n_in + 2 + n][...] = jnp.zeros((8, 128), F32)

    dma = pltpu.SemaphoreType.DMA
    outs = pl.pallas_call(
        body, name=name,
        out_shape=(dma(sem_shape), dma(sem_shape), *[pltpu.HBM(b.shape, b.dtype) for b in bufs], jax.ShapeDtypeStruct((8, 128), F32)),
        in_specs=(HBM,) * n + ((ANY,) if dep is not None else ()),
        out_specs=(SEM, SEM) + (HBM,) * n + (pl.BlockSpec(memory_space=pltpu.VMEM),),
        input_output_aliases={i: i + 2 for i in range(n)},
        compiler_params=pltpu.CompilerParams(has_side_effects=SIDE_EFFECT),
    )(*[pltpu.with_memory_space_constraint(b, pltpu.HBM) for b in bufs], *(() if dep is None else (dep,)))
    return outs[0], outs[1], list(outs[2:2 + n]), outs[2 + n]


def _wait(name, started, plan, after):
    send, recv, bufs, _ = started
    n = len(bufs)

    def body(*refs):
        sends, recvs = plan(refs[:n], refs[n], refs[n + 1])
        for cp in sends:
            cp.wait_send()
        for cp in recvs:
            cp.wait_recv()

    return list(pl.pallas_call(
        body, name=name, out_shape=tuple(pltpu.HBM(b.shape, b.dtype) for b in bufs),
        in_specs=(HBM,) * n + (SEM, SEM, ANY), out_specs=(HBM,) * n, input_output_aliases={i: i for i in range(n)},
        compiler_params=pltpu.CompilerParams(has_side_effects=SIDE_EFFECT),
    )(*bufs, send, recv, after))


KINDS = ("col", "row")


def _gather_ici_plan(n_small):
    def plan(refs, send, recv):
        x, y, c, s, chips = _place()
        n = len(KINDS) + n_small
        sends, recvs = [], []
        for k, (px, py) in enumerate(chips):
            sp = 2 * px + py
            for a, kind in enumerate(KINDS):
                mine, theirs = _half(_shard(refs[a], kind, s), kind, c), _half(_shard(refs[a], kind, sp), kind, c)
                sends.append(_remote(mine, mine, send.at[k * n + a], recv.at[k * n + a], (px, py, c)))
                recvs.append(_remote(theirs, theirs, send.at[k * n + a], recv.at[k * n + a], (px, py, c)))
            for b in range(n_small):
                ref, sem = refs[len(KINDS) + b], k * n + len(KINDS) + b
                sends.append(_remote(ref.at[s], ref.at[s], send.at[sem], recv.at[sem], (px, py, c)))
                recvs.append(_remote(ref.at[sp], ref.at[sp], send.at[sem], recv.at[sem], (px, py, c)))
        return sends, recvs
    return plan


def _gather_d2d_plan(refs, send, recv):
    x, y, c, _, chips = _place()
    n = len(KINDS)
    sends, recvs = [], []
    for k, (px, py) in enumerate(chips):
        for a, kind in enumerate(KINDS):
            region, sem = _shard(refs[a], kind, 2 * px + py), k * n + a
            sends.append(_remote(_half(region, kind, c), _half(region, kind, c), send.at[sem], recv.at[sem], (x, y, 1 - c)))
            recvs.append(_remote(_half(region, kind, 1 - c), _half(region, kind, 1 - c), send.at[sem], recv.at[sem], (x, y, 1 - c)))
    return sends, recvs


def _swap_plan(refs, send, recv):
    x, y, c, _, _ = _place()
    n = len(KINDS)
    cps = [_remote(_half(refs[a], KINDS[a], 1 - c), refs[n + a], send.at[a], recv.at[a], (x, y, 1 - c)) for a in range(n)]
    return cps, cps


def _exchange_plan(refs, send, recv):
    x, y, c, _, chips = _place()
    n = len(KINDS)
    cps = []
    for k, (px, py) in enumerate(chips):
        for a in range(n):
            cps.append(_remote(_shard(refs[a], KINDS[a], 2 * px + py), refs[n + a].at[k], send.at[k * n + a], recv.at[k * n + a],
                               (px, py, c)))
    return cps, cps


def _share_plan(refs, send, recv):
    x, y, c, _, _ = _place()
    sends = [_remote(_half(refs[a], KINDS[a], c), _half(refs[a], KINDS[a], c), send.at[a], recv.at[a], (x, y, 1 - c))
             for a in range(len(KINDS))]
    recvs = [_remote(_half(refs[a], KINDS[a], 1 - c), _half(refs[a], KINDS[a], 1 - c), send.at[a], recv.at[a], (x, y, 1 - c))
             for a in range(len(KINDS))]
    return sends, recvs


def _spread_plan(refs, send, recv):
    packed, slots = refs
    x, y, c, _, _ = _place()
    sends, recvs = [], []
    for k in range(1, 8):
        px, py, pc = x ^ (k >> 2), y ^ ((k >> 1) & 1), c ^ (k & 1)
        sends.append(_remote(packed, slots.at[4 * x + 2 * y + c], send.at[k - 1], recv.at[k - 1], (px, py, pc)))
        recvs.append(_remote(packed, slots.at[4 * px + 2 * py + pc], send.at[k - 1], recv.at[k - 1], (px, py, pc)))
    return sends, recvs


def _half_index(kind, nblk):
    def index(i, j, place):
        return (i + place[1] * nblk[0], j) if kind == "col" else (i, j + place[1] * nblk[1])
    return index


def _chip_partial(g, other, kind, place, *, name):
    hr, hc = other.shape
    rb, cb = _tile(hr, 512, 16), _tile(hc, 1024)
    nblk = (hr // rb, hc // cb)

    def body(place_ref, g_ref, o_ref, p_ref):
        p_ref[...] = (g_ref[...].astype(F32) + o_ref[...].astype(F32)).astype(BF16)

    plain = pl.BlockSpec((rb, cb), lambda i, j, place: (i, j))
    return pl.pallas_call(
        body,
        grid_spec=pltpu.PrefetchScalarGridSpec(
            num_scalar_prefetch=1, grid=nblk, in_specs=[pl.BlockSpec((rb, cb), _half_index(kind, nblk)), plain], out_specs=plain),
        out_shape=jax.ShapeDtypeStruct((hr, hc), BF16), compiler_params=_params(("parallel", "parallel")), name=name,
    )(place, g, other)


def _reduce_half(g, other, recv, kind, place, *, name):
    _, pr, pc = recv.shape
    rb, cb = _tile(pr, 512, 16), _tile(pc, 1024)
    nblk = (pr // rb, pc // cb)
    full = (pr * 2, pc) if kind == "col" else (pr, pc * 2)

    def g_index(i, j, place):
        s, c = place[0], place[1]
        return (i + c * nblk[0], j + s * nblk[1]) if kind == "col" else (i + s * nblk[0], j + c * nblk[1])

    def o_index(i, j, place):
        return (i, j + place[0] * nblk[1]) if kind == "col" else (i + place[0] * nblk[0], j)

    def body(place_ref, g_ref, o_ref, r_ref, out_ref):
        acc = g_ref[...].astype(F32) + o_ref[...].astype(F32)
        for k in range(3):
            acc = acc + r_ref[k].astype(F32)
        out_ref[...] = acc

    return pl.pallas_call(
        body,
        grid_spec=pltpu.PrefetchScalarGridSpec(
            num_scalar_prefetch=1, grid=nblk,
            in_specs=[pl.BlockSpec((rb, cb), g_index), pl.BlockSpec((rb, cb), o_index),
                      pl.BlockSpec((3, rb, cb), lambda i, j, place: (0, i, j))],
            out_specs=pl.BlockSpec((rb, cb), _half_index(kind, nblk))),
        out_shape=jax.ShapeDtypeStruct(full, F32), compiler_params=_params(("parallel", "parallel")), name=name,
    )(place, g, other, recv)


def _sum_slots(packed, slots, me, *, name, rb=512):
    r, w = packed.shape
    rb = _tile(r, rb, 8)

    def body(me_ref, p_ref, s_ref, o_ref):
        acc = None
        for j in range(8):
            term = jnp.where(me_ref[0] == j, p_ref[...], s_ref[j])
            acc = term if acc is None else acc + term
        o_ref[...] = acc

    return pl.pallas_call(
        body,
        grid_spec=pltpu.PrefetchScalarGridSpec(
            num_scalar_prefetch=1, grid=(r // rb,),
            in_specs=[pl.BlockSpec((rb, w), lambda i, me: (i, 0)), pl.BlockSpec((8, rb, w), lambda i, me: (0, i, 0))],
            out_specs=pl.BlockSpec((rb, w), lambda i, me: (i, 0))),
        out_shape=jax.ShapeDtypeStruct((r, w), F32), compiler_params=_params(("parallel",)), name=name,
    )(me, packed, slots)


def _half_shape(a, kind):
    return (a.shape[0] // 2, a.shape[1]) if kind == "col" else (a.shape[0], a.shape[1] // 2)


def _rs_swap(tag, grads):
    others = [lax.empty(_half_shape(g, k), g.dtype) for g, k in zip(grads, KINDS)]
    return _start(f"rs_{tag}_swap", list(grads) + others, _swap_plan, (len(KINDS),))


def _rs_exchange(tag, swapped, place, after):
    bufs = _wait(f"rs_{tag}_swap_wait", swapped, _swap_plan, after)
    n = len(KINDS)
    grads, others = bufs[:n], bufs[n:]
    parts = [_chip_partial(g, o, k, place, name=f"rs_{tag}_partial_{k}") for g, o, k in zip(grads, others, KINDS)]
    lands = []
    for p, k in zip(parts, KINDS):
        piece = (p.shape[0], p.shape[1] // N_CHIPS) if k == "col" else (p.shape[0] // N_CHIPS, p.shape[1])
        lands.append(lax.empty((3,) + piece, p.dtype))
    return _start(f"rs_{tag}_exchange", parts + lands, _exchange_plan, (3 * n,)), grads, others


def _rs_share(tag, exchanged, place, after):
    started, grads, others = exchanged
    n = len(KINDS)
    recvs = _wait(f"rs_{tag}_exchange_wait", started, _exchange_plan, after)[n:]
    halves = [_reduce_half(g, o, r, k, place, name=f"rs_{tag}_reduce_{k}") for g, o, r, k in zip(grads, others, recvs, KINDS)]
    return _start(f"rs_{tag}_share", halves, _share_plan, (n,))


def _rs_finish(tag, shared, after):
    return _wait(f"rs_{tag}_share_wait", shared, _share_plan, after)


def _spread(tag, parts, dep):
    rows = [p.reshape(-1, 128) for p in parts]
    n = sum(r.shape[0] for r in rows)
    rows.append(jnp.zeros(((-n) % 512, 128), F32))
    packed = jnp.concatenate(rows, axis=0)
    return _start(f"small_{tag}_spread", [packed, lax.empty((8,) + packed.shape, F32)], _spread_plan, (7,), dep=dep)


def _spread_sum(tag, started, parts, me, after):
    packed, slots = _wait(f"small_{tag}_spread_wait", started, _spread_plan, after)
    total = _sum_slots(packed, slots, me, name=f"small_{tag}_sum")
    out, row = [], 0
    for p in parts:
        n = p.size // 128
        out.append(total[row:row + n].reshape(p.shape))
        row += n
    return out


def _ffn_fwd(x, h, w_up, conv_w, conv_b, w_down, tag, tm_up=1024):
    up = _mm(h, w_up, tm=tm_up, out_dtype=BF16, name=f"ffn{tag}_up")
    act, conv = _ffn_mid_fwd(up, conv_w, conv_b, name=f"ffn{tag}_mid")
    x_out = _mm(act, w_down, res=x, tk=2816, name=f"ffn{tag}_down")
    return x_out, (up, conv, act)


def kernel(x, a_norm, a_in, a_conv, a_out, b_norm, b_in, b_vnorm, b_ws, b_bs, b_out, f_norm, f_up, f_conv_w, f_conv_b, f_down, final_norm, loss_target, m_a_norm, m_a_in, m_a_conv, m_a_out, m_b_norm, m_b_in, m_b_vnorm, m_b_ws, m_b_bs, m_b_out, m_f_norm, m_f_up, m_f_conv_w, m_f_conv_b, m_f_down, m_final_norm, v_a_norm, v_a_in, v_a_conv, v_a_out, v_b_norm, v_b_in, v_b_vnorm, v_b_ws, v_b_bs, v_b_out, v_f_norm, v_f_up, v_f_conv_w, v_f_conv_b, v_f_down, v_final_norm):
    t, d = x.shape[1], x.shape[2]
    f2 = f_up.shape[2] * N_CHIPS
    x0, tgt = x.reshape(t, d), loss_target.reshape(t, d)
    ax, ay, ac = lax.axis_index("x"), lax.axis_index("y"), lax.axis_index("c")
    s = 2 * ax + ay
    place = jnp.stack([s, ac]).astype(jnp.int32)
    me = (4 * ax + 2 * ay + ac).astype(jnp.int32).reshape(1)

    def stacked(a):
        return lax.dynamic_update_index_in_dim(jnp.zeros((N_CHIPS,) + a.shape, F32), a, s, 0)

    def cast_pair(tag, w_in, w_out, layer, dep):
        return [_cast_into_full(w_in, layer, "col", place, name=f"cast_{tag}_in", dep=dep),
                _cast_into_full(w_out, layer, "row", place, name=f"cast_{tag}_out", dep=dep)]

    def gather_start(tag, fulls, small, dep):
        return _start(f"ag_{tag}_ici", fulls + small, _gather_ici_plan(len(small)), (3 * (2 + len(small)),), dep=dep)

    def gather_forward(tag, started, n_small, after):
        bufs = _wait(f"ag_{tag}_ici_wait", started, _gather_ici_plan(n_small), after)
        return _start(f"ag_{tag}_d2d", bufs[:2], _gather_d2d_plan, (3 * 2,)), bufs[2:]

    def gather_finish(tag, forwarded, after):
        return _wait(f"ag_{tag}_d2d_wait", forwarded, _gather_d2d_plan, after)

    small = [stacked(a_conv[0]), stacked(b_norm), stacked(b_vnorm), stacked(f_conv_w.reshape(2 * 3, -1))]
    ag_a = gather_start("a", cast_pair("a", a_in, a_out, 0, None), small, None)
    full_f0 = cast_pair("f0", f_up, f_down, 0, ag_a[3])
    full_b = cast_pair("b", b_in, b_out, 0, full_f0[1])
    full_f1 = cast_pair("f1", f_up, f_down, 1, full_b[1])

    def unshard(a):
        return jnp.transpose(a, (1, 0, 2)).reshape(a.shape[1], -1)

    ws, bs = b_ws[0], b_bs[0]

    h0 = _rms_fwd(x0, a_norm, dep=full_f1[1], name="a_norm")
    fw_a, (g_aconv, g_bnorm, g_bvnorm, g_fconv) = gather_forward("a", ag_a, 4, h0)
    ag_f0 = gather_start("f0", full_f0, [], fw_a[3])
    ag_b = gather_start("b", full_b, [], ag_f0[3])
    ag_f1 = gather_start("f1", full_f1, [], ag_b[3])
    w_ai, w_ao = gather_finish("a", fw_a, ag_f1[3])
    a_conv_f, b_norm_f, b_vnorm_f = unshard(g_aconv), unshard(g_bnorm), unshard(g_bvnorm)
    f_conv_f = unshard(g_fconv).reshape(2, 3, f2)
    bcx = _mm(h0, w_ai, tm=2048, out_dtype=BF16, name="a_in")
    y = _a_mid_fwd(bcx, a_conv_f, name="a_mid")
    x1 = _mm(y, w_ao, res=x0, tm=512, tn=2048, name="a_out")
    fw_f0, _ = gather_forward("f0", ag_f0, 0, x1)
    h1 = _rms_fwd(x1, f_norm[0:1], dep=fw_f0[3], name="ffn0_norm")
    w_up0, w_dn0 = gather_finish("f0", fw_f0, h1)
    x2, (up0, conv0, act0) = _ffn_fwd(x1, h1, w_up0, f_conv_f[0], f_conv_b[0:1], w_dn0, 0, tm_up=2048)
    fw_b, _ = gather_forward("b", ag_b, 0, up0)
    w_bi, w_bo = gather_finish("b", fw_b, act0)
    h2 = _rms_fwd(x2, b_norm_f, name="b_norm")
    zp = _mm(h2, w_bi, tm=2048, out_dtype=BF16, name="b_in")
    fw_f1, _ = gather_forward("f1", ag_f1, 0, zp)
    ug = _b_mid_fwd(zp, b_vnorm_f, ws, bs, name="b_mid")
    x3, h3 = _mm(ug, w_bo, res=x2, norm=f_norm[1:2], tm=512, tn=2048, name="b_out")
    w_up1, w_dn1 = gather_finish("f1", fw_f1, x3)
    x4, (up1, conv1, act1) = _ffn_fwd(x3, h3, w_up1, f_conv_f[1], f_conv_b[1:2], w_dn1, 1, tm_up=2048)
    loss_rows, dx4, dx4b, d_final = _final(x4, tgt, final_norm.reshape(1, d), name="final")

    d_dn1 = _mm(act1, dx4b, ta=True, tm=1408, out_dtype=BF16, name="ffn1_ddown")
    dact1 = _mm(dx4b, w_dn1, tb=True, tn=512, tm=2048, name="ffn1_dact")
    dup1, d_fwb1 = _ffn_mid_bwd(up1, conv1, dact1, f_conv_f[1], name="ffn1_mid_bwd")
    d_up1 = _mm(h3, dup1, ta=True, out_dtype=BF16, tk=4096, name="ffn1_dup")
    sw_f1 = _rs_swap("f1", [d_up1, d_dn1])
    dh3 = _mm(dup1, w_up1, tb=True, tk=2816, out_dtype=BF16, dep=sw_f1[3], name="ffn1_dh")
    dx3, dx3b, d_fnorm1 = _rms_bwd(dh3, x3, f_norm[1:2], dx4, name="ffn1_norm_bwd")
    ex_f1 = _rs_exchange("f1", sw_f1, place, dx3)

    d_bo = _mm(ug, dx3b, ta=True, out_dtype=BF16, tk=4096, dep=ex_f1[0][3], name="b_dout")
    dug = _mm(dx3b, w_bo, tb=True, tm=512, tn=2048, out_dtype=BF16, name="b_dug")
    dzp, d_ws, d_bs, d_bvnorm = _b_mid_bwd(zp, dug, b_vnorm_f, ws, bs, name="b_mid_bwd")
    d_bi = _mm(h2, dzp, ta=True, out_dtype=BF16, tk=4096, name="b_din")
    sw_b = _rs_swap("b", [d_bi, d_bo])
    dh2 = _mm(dzp, w_bi, tb=True, tk=4096, out_dtype=BF16, dep=sw_b[3], name="b_dh")
    dx2, dx2b, d_bnorm = _rms_bwd(dh2, x2, b_norm_f, dx3, name="b_norm_bwd")
    ex_b = _rs_exchange("b", sw_b, place, dx2)

    d_dn0 = _mm(act0, dx2b, ta=True, tm=1408, out_dtype=BF16, dep=ex_b[0][3], name="ffn0_ddown")
    dact0 = _mm(dx2b, w_dn0, tb=True, tn=512, tm=2048, name="ffn0_dact")
    dup0, d_fwb0 = _ffn_mid_bwd(up0, conv0, dact0, f_conv_f[0], name="ffn0_mid_bwd")
    sh_f1 = _rs_share("f1", ex_f1, place, dup0)
    d_up0 = _mm(h1, dup0, ta=True, out_dtype=BF16, tk=4096, dep=sh_f1[3], name="ffn0_dup")
    sw_f0 = _rs_swap("f0", [d_up0, d_dn0])
    g_up1, g_dn1 = _rs_finish("f1", sh_f1, sw_f0[3])
    dh1 = _mm(dup0, w_up0, tb=True, tk=2816, out_dtype=BF16, dep=sw_f0[3], name="ffn0_dh")
    dx1, dx1b, d_fnorm0 = _rms_bwd(dh1, x1, f_norm[0:1], dx2, name="ffn0_norm_bwd")
    ex_f0 = _rs_exchange("f0", sw_f0, place, dx1)
    early = [jnp.concatenate([d_bnorm, d_bvnorm, d_fnorm0, d_fnorm1, d_final, loss_rows], axis=0),
             jnp.concatenate([d_fwb0, d_fwb1], axis=0), jnp.concatenate([d_ws.reshape(-1, CHUNK), d_bs], axis=0)]
    sp_early = _spread("early", early, ex_f0[0][3])
    sh_b = _rs_share("b", ex_b, place, sp_early[3])

    d_ao = _mm(y, dx1b, ta=True, out_dtype=BF16, tk=4096, dep=sh_b[3], name="a_dout")
    dyy = _mm(dx1b, w_ao, tb=True, tm=512, tn=2048, name="a_dy")
    dbcx, d_aconv = _a_mid_bwd(bcx, dyy, a_conv_f, name="a_mid_bwd")
    d_ai = _mm(h0, dbcx, ta=True, out_dtype=BF16, tk=4096, name="a_din")
    sw_a = _rs_swap("a", [d_ai, d_ao])
    g_bi, g_bo = _rs_finish("b", sh_b, sw_a[3])
    ex_a = _rs_exchange("a", sw_a, place, g_bi)
    dh0 = _mm(dbcx, w_ai, tb=True, tk=3072, out_dtype=BF16, dep=ex_a[0][3], name="a_dh")
    grad_x, _, d_anorm = _rms_bwd(dh0, x0, a_norm, dx1, name="a_norm_bwd")
    late = [jnp.concatenate([d_anorm, d_aconv], axis=0)]
    sp_late = _spread("late", late, ex_a[0][3])
    sh_f0 = _rs_share("f0", ex_f0, place, sp_late[3])
    sh_a = _rs_share("a", ex_a, place, sh_f0[3])
    g_up0, g_dn0 = _rs_finish("f0", sh_f0, sh_a[3])
    g_ai, g_ao = _rs_finish("a", sh_a, g_up0)
    r_a, r_b, r_c = _spread_sum("early", sp_early, early, me, g_ai)
    (r_l,) = _spread_sum("late", sp_late, late, me, r_a)

    loss = jnp.sum(r_a[40])
    cs, fs = d // N_CHIPS, f2 // N_CHIPS

    def mine(a, width):
        return lax.dynamic_slice_in_dim(a, s * width, width, axis=1)

    grads = {
        "a_norm": r_l[0:1], "a_conv": mine(r_l[8:11], cs), "b_norm": mine(r_a[0:1], cs), "b_vnorm": mine(r_a[8:9], cs),
        "f_norm": jnp.concatenate([r_a[16:17], r_a[24:25]], axis=0), "final_norm": r_a[32:33],
        "b_ws": r_c[:GROUPS * CHUNK], "b_bs": r_c[GROUPS * CHUNK:],
        "f_conv_w": jnp.concatenate([mine(r_b[0:3], fs), mine(r_b[8:11], fs)], axis=0),
        "f_conv_b": jnp.concatenate([r_b[3:4], r_b[11:12]], axis=0),
        "a_in": g_ai, "a_out": g_ao, "b_in": g_bi, "b_out": g_bo,
    }
    names = ["a_norm", "a_in", "a_conv", "a_out", "b_norm", "b_in", "b_vnorm", "b_ws", "b_bs", "b_out", "f_norm", "f_up",
             "f_conv_w", "f_conv_b", "f_down", "final_norm"]
    weights = dict(zip(names, [a_norm, a_in, a_conv, a_out, b_norm, b_in, b_vnorm, b_ws, b_bs, b_out, f_norm, f_up, f_conv_w,
                               f_conv_b, f_down, final_norm]))
    ms = dict(zip(names, [m_a_norm, m_a_in, m_a_conv, m_a_out, m_b_norm, m_b_in, m_b_vnorm, m_b_ws, m_b_bs, m_b_out, m_f_norm,
                          m_f_up, m_f_conv_w, m_f_conv_b, m_f_down, m_final_norm]))
    vs = dict(zip(names, [v_a_norm, v_a_in, v_a_conv, v_a_out, v_b_norm, v_b_in, v_b_vnorm, v_b_ws, v_b_bs, v_b_out, v_f_norm,
                          v_f_up, v_f_conv_w, v_f_conv_b, v_f_down, v_final_norm]))
    result = {}
    for n in names:
        w = weights[n]
        if n in ("f_up", "f_down"):
            g1, g0 = (g_up1, g_up0) if n == "f_up" else (g_dn1, g_dn0)
            first = _adamw_layer(w, g1, ms[n], vs[n], 1, None, name=f"adamw_{n}1")
            result[n] = _adamw_layer(w, g0, ms[n], vs[n], 0, tuple(first), name=f"adamw_{n}0")
            continue
        g2 = grads[n]
        as3d = (lambda a: a.reshape((1,) + g2.shape))
        result[n] = [o.reshape(w.shape) for o in _adamw_layer(as3d(w), g2, as3d(ms[n]), as3d(vs[n]), 0, None, name=f"adamw_{n}")]

    return (loss, grad_x.reshape(x.shape), *[result[n][0] for n in names], *[result[n][1] for n in names],
            *[result[n][2] for n in names], *[result[n][3] for n in names])
```

```python
import functools

import jax
import jax.numpy as jnp
from jax import lax
from jax.experimental import pallas as pl
from jax.experimental.pallas import tpu as pltpu

F32 = jnp.float32
BF16 = jnp.bfloat16
MESH = pl.DeviceIdType.MESH
ANY = pl.BlockSpec(memory_space=pl.ANY)

RMS_EPS = 1e-5
CHUNK = 128
GROUPS = 8
ADAM_LR, ADAM_B1, ADAM_B2, ADAM_EPS, ADAM_WD, ADAM_STEP = 0.001, 0.9, 0.999, 1e-08, 0.01, 10

N_CHIPS = 4
HALO = 8
BF16_HALO = 16
VMEM_LIMIT = 56 * 1024 * 1024
GELU_C = 0.7978845608028654
GELU_A = 0.044715


def _params(sem=None):
    return pltpu.CompilerParams(dimension_semantics=sem, vmem_limit_bytes=VMEM_LIMIT)


def _tile(dim, pref, quantum=128):
    if dim <= pref:
        return dim
    t = (pref // quantum) * quantum
    while t >= quantum:
        if dim % t == 0:
            return t
        t -= quantum
    return dim


def _mm(a, b, *, name, ta=False, tb=False, res=None, norm=None, dep=None, out_dtype=F32, tm=1024, tn=1024, tk=2048):
    (K, M) = a.shape if ta else a.shape[::-1]
    N = b.shape[0] if tb else b.shape[1]
    assert (b.shape[1] if tb else b.shape[0]) == K
    tm, tn, tk = _tile(M, tm), _tile(N, tn), _tile(K, tk)
    nk = K // tk
    assert norm is None or (tn == N and nk == 1)
    a_spec = pl.BlockSpec((tk, tm), lambda i, j, k: (k, i)) if ta else pl.BlockSpec((tm, tk), lambda i, j, k: (i, k))
    b_spec = pl.BlockSpec((tn, tk), lambda i, j, k: (j, k)) if tb else pl.BlockSpec((tk, tn), lambda i, j, k: (k, j))
    o_spec = pl.BlockSpec((tm, tn), lambda i, j, k: (i, j))
    dims = (((0 if ta else 1,), (1 if tb else 0,)), ((), ()))
    direct = out_dtype == F32
    n_in = 2 + (res is not None) + (norm is not None) + (dep is not None)

    def body(*refs):
        a_ref, b_ref = refs[0], refs[1]
        r_ref = refs[2] if res is not None else None
        g_ref = refs[2 + (res is not None)] if norm is not None else None
        o_ref = refs[n_in]
        acc_ref = o_ref if direct else refs[-1]
        part = lax.dot_general(a_ref[...], b_ref[...], dims, preferred_element_type=F32)
        if nk == 1:
            if r_ref is not None:
                part = part + r_ref[...]
            o_ref[...] = part.astype(o_ref.dtype)
            if g_ref is not None:
                r = lax.rsqrt(jnp.mean(part * part, axis=-1, keepdims=True) + RMS_EPS)
                refs[n_in + 1][...] = ((part * r) * g_ref[...]).astype(BF16)
            return
        k = pl.program_id(2)

        @pl.when(k == 0)
        def _():
            acc_ref[...] = part

        @pl.when(jnp.logical_and(k > 0, k < nk - 1))
        def _():
            acc_ref[...] += part

        @pl.when(k == nk - 1)
        def _():
            tot = acc_ref[...] + part
            if r_ref is not None:
                tot = tot + r_ref[...]
            o_ref[...] = tot.astype(o_ref.dtype)

    in_specs = ([a_spec, b_spec] + ([o_spec] if res is not None else [])
                + ([pl.BlockSpec((1, tn), lambda i, j, k: (0, j))] if norm is not None else []) + ([ANY] if dep is not None else []))
    args = (a, b) + tuple(x for x in (res, norm, dep) if x is not None)
    scratch = [] if (direct or nk == 1) else [pltpu.VMEM((tm, tn), F32)]
    out_shape = jax.ShapeDtypeStruct((M, N), out_dtype)
    return pl.pallas_call(
        body, grid=(M // tm, N // tn, nk), in_specs=in_specs, out_specs=[o_spec, o_spec] if norm is not None else o_spec,
        out_shape=[out_shape, jax.ShapeDtypeStruct((M, N), BF16)] if norm is not None else out_shape, scratch_shapes=scratch,
        compiler_params=_params(("parallel", "parallel", "arbitrary")), name=name,
    )(*args)


def _row_spec(rb, w):
    return pl.BlockSpec((rb, w), lambda i: (i, 0))


def _prev_spec(rb, w):
    return pl.BlockSpec((HALO, w), lambda i: (jnp.maximum(i * (rb // HALO) - 1, 0), 0))


def _next_spec(rb, w, t):
    return pl.BlockSpec((HALO, w), lambda i: (jnp.minimum((i + 1) * (rb // HALO), t // HALO - 1), 0))


def _prev_spec16(rb, w):
    return pl.BlockSpec((BF16_HALO, w), lambda i: (jnp.maximum(i * (rb // BF16_HALO) - 1, 0), 0))


def _next_spec16(rb, w, t):
    return pl.BlockSpec((BF16_HALO, w), lambda i: (jnp.minimum((i + 1) * (rb // BF16_HALO), t // BF16_HALO - 1), 0))


def _full_spec(shape):
    return pl.BlockSpec(shape, lambda i: tuple(0 for _ in shape))


def _shift(e, s):
    return pltpu.roll(e, s % e.shape[0], 0)


def _gelu(x):
    return 0.5 * x * (1.0 + jnp.tanh(GELU_C * (x + GELU_A * x * x * x)))


def _gelu_grad(x):
    th = jnp.tanh(GELU_C * (x + GELU_A * x * x * x))
    return 0.5 * (1.0 + th) + 0.5 * x * (1.0 - th * th) * (GELU_C * (1.0 + 3.0 * GELU_A * x * x))


def _sigmoid(x):
    return 1.0 / (1.0 + jnp.exp(-x))


def _rms_fwd(x, g, *, name, dep=None, rb=256):
    t, d = x.shape
    rb = _tile(t, rb, 8)

    def body(x_ref, g_ref, *rest):
        h_ref = rest[-1]
        xv = x_ref[...]
        r = lax.rsqrt(jnp.mean(xv * xv, axis=-1, keepdims=True) + RMS_EPS)
        h_ref[...] = ((xv * r) * g_ref[...]).astype(BF16)

    return pl.pallas_call(
        body, grid=(t // rb,), in_specs=[_row_spec(rb, d), _full_spec((1, d))] + ([ANY] if dep is not None else []),
        out_specs=_row_spec(rb, d), out_shape=jax.ShapeDtypeStruct((t, d), BF16), compiler_params=_params(("parallel",)), name=name,
    )(x, g, *(() if dep is None else (dep,)))


def _rms_bwd(dh, x, g, dres, *, name, dep=None, rb=256):
    t, d = x.shape
    rb = _tile(t, rb, 8)

    def body(dh_ref, x_ref, g_ref, dres_ref, *rest):
        dx_ref, dxb_ref, dg_ref = rest[-3:]
        xv = x_ref[...]
        r = lax.rsqrt(jnp.mean(xv * xv, axis=-1, keepdims=True) + RMS_EPS)
        xhat = xv * r
        dh_v = dh_ref[...].astype(F32)
        dxhat = dh_v * g_ref[...]
        m = jnp.mean(dxhat * xhat, axis=-1, keepdims=True)
        dx = dres_ref[...] + r * (dxhat - xhat * m)
        dx_ref[...] = dx
        dxb_ref[...] = dx.astype(BF16)

        @pl.when(pl.program_id(0) == 0)
        def _():
            dg_ref[...] = jnp.zeros_like(dg_ref)

        dg_ref[0:1, :] += jnp.sum(dh_v * xhat, axis=0, keepdims=True)

    return pl.pallas_call(
        body, grid=(t // rb,),
        in_specs=[_row_spec(rb, d), _row_spec(rb, d), _full_spec((1, d)), _row_spec(rb, d)] + ([ANY] if dep is not None else []),
        out_specs=[_row_spec(rb, d), _row_spec(rb, d), _full_spec((8, d))],
        out_shape=[jax.ShapeDtypeStruct((t, d), F32), jax.ShapeDtypeStruct((t, d), BF16), jax.ShapeDtypeStruct((8, d), F32)],
        compiler_params=_params(("arbitrary",)), name=name,
    )(dh, x, g, dres, *(() if dep is None else (dep,)))


def _final(x, tgt, g, *, name, rb=256):
    t, d = x.shape
    rb = _tile(t, rb, 8)
    inv_d = 1.0 / d

    def body(x_ref, t_ref, g_ref, l_ref, dx_ref, dxb_ref, dg_ref):
        xv = x_ref[...]
        gv = g_ref[...]
        r = lax.rsqrt(jnp.mean(xv * xv, axis=-1, keepdims=True) + RMS_EPS)
        xhat = xv * r
        e = xhat * gv - t_ref[...]
        dy = e * inv_d
        dxhat = dy * gv
        m = jnp.mean(dxhat * xhat, axis=-1, keepdims=True)
        dx = r * (dxhat - xhat * m)
        dx_ref[...] = dx
        dxb_ref[...] = dx.astype(BF16)

        @pl.when(pl.program_id(0) == 0)
        def _():
            l_ref[...] = jnp.zeros_like(l_ref)
            dg_ref[...] = jnp.zeros_like(dg_ref)

        l_ref[0:1, :] += jnp.sum(e * e, axis=0, keepdims=True) * (0.5 * inv_d)
        dg_ref[0:1, :] += jnp.sum(dy * xhat, axis=0, keepdims=True)

    return pl.pallas_call(
        body, grid=(t // rb,),
        in_specs=[_row_spec(rb, d), _row_spec(rb, d), _full_spec((1, d))],
        out_specs=[_full_spec((8, d)), _row_spec(rb, d), _row_spec(rb, d), _full_spec((8, d))],
        out_shape=[jax.ShapeDtypeStruct((8, d), F32), jax.ShapeDtypeStruct((t, d), F32),
                   jax.ShapeDtypeStruct((t, d), BF16), jax.ShapeDtypeStruct((8, d), F32)],
        compiler_params=_params(("arbitrary",)), name=name,
    )(x, tgt, g)


def _a_mid_fwd(bcx, wconv, *, name, rb=256, cw=512):
    t, d3 = bcx.shape
    d = d3 // 3
    rb, cw = _tile(t, rb, 16), _tile(d, cw)

    def body(cur_ref, prev_ref, w_ref, y_ref):
        first = pl.program_id(0) == 0

        def f32(ref, cols):
            return ref[:, cols].astype(F32)

        for c0 in range(0, d, cw):
            cs = slice(c0, c0 + cw)
            gc, xs = slice(d + c0, d + c0 + cw), slice(2 * d + c0, 2 * d + c0 + cw)
            p_prev = jnp.where(first, 0.0, (f32(prev_ref, gc) * f32(prev_ref, xs))[BF16_HALO - HALO:])
            e = jnp.concatenate([p_prev, f32(cur_ref, gc) * f32(cur_ref, xs)], axis=0)
            w = w_ref[:, cs]
            q = w[0:1] * _shift(e, 2) + w[1:2] * _shift(e, 1) + w[2:3] * e
            y_ref[:, cs] = (f32(cur_ref, cs) * q[HALO:]).astype(BF16)

    return pl.pallas_call(
        body, grid=(t // rb,),
        in_specs=[_row_spec(rb, d3), _prev_spec16(rb, d3), _full_spec((3, d))], out_specs=_row_spec(rb, d),
        out_shape=jax.ShapeDtypeStruct((t, d), BF16), compiler_params=_params(("parallel",)), name=name,
    )(bcx, bcx, wconv)


def _a_mid_bwd(bcx, dy, wconv, *, name, rb=128, cw=512):
    t, d3 = bcx.shape
    d = d3 // 3
    rb, cw = _tile(t, rb, 16), _tile(d, cw)

    def body(cur_ref, prev_ref, next_ref, dy_ref, dyn_ref, w_ref, o_ref, dw_ref):
        i = pl.program_id(0)
        first, last = i == 0, i == pl.num_programs(0) - 1

        @pl.when(first)
        def _():
            dw_ref[...] = jnp.zeros_like(dw_ref)

        def f32(ref, cols):
            return ref[:, cols].astype(F32)

        for c0 in range(0, d, cw):
            cs = slice(c0, c0 + cw)
            gc, xs = slice(d + c0, d + c0 + cw), slice(2 * d + c0, 2 * d + c0 + cw)
            zeros = jnp.zeros((HALO, cw), F32)
            gb_c, gc_c, xs_c = f32(cur_ref, cs), f32(cur_ref, gc), f32(cur_ref, xs)
            p_prev = jnp.where(first, 0.0, (f32(prev_ref, gc) * f32(prev_ref, xs))[BF16_HALO - HALO:])
            e = jnp.concatenate([p_prev, gc_c * xs_c, zeros], axis=0)
            dq_next = jnp.where(last, 0.0, dyn_ref[:, cs] * f32(next_ref, cs)[:HALO])
            dy_c = dy_ref[:, cs]
            dq = jnp.concatenate([zeros, dy_c * gb_c, dq_next], axis=0)
            w = w_ref[:, cs]
            e1, e2 = _shift(e, 1), _shift(e, 2)
            q = w[0:1] * e2 + w[1:2] * e1 + w[2:3] * e
            dp = (w[2:3] * dq + w[1:2] * _shift(dq, -1) + w[0:1] * _shift(dq, -2))[HALO:HALO + rb]
            o_ref[:, cs] = (dy_c * q[HALO:HALO + rb]).astype(BF16)
            o_ref[:, gc] = (dp * xs_c).astype(BF16)
            o_ref[:, xs] = (dp * gc_c).astype(BF16)
            dq_c = dq[HALO:HALO + rb]
            dw_ref[0:1, cs] += jnp.sum(dq_c * e2[HALO:HALO + rb], axis=0, keepdims=True)
            dw_ref[1:2, cs] += jnp.sum(dq_c * e1[HALO:HALO + rb], axis=0, keepdims=True)
            dw_ref[2:3, cs] += jnp.sum(dq_c * e[HALO:HALO + rb], axis=0, keepdims=True)

    return pl.pallas_call(
        body, grid=(t // rb,),
        in_specs=[_row_spec(rb, d3), _prev_spec16(rb, d3), _next_spec16(rb, d3, t), _row_spec(rb, d), _next_spec(rb, d, t),
                  _full_spec((3, d))],
        out_specs=[_row_spec(rb, d3), _full_spec((8, d))],
        out_shape=[jax.ShapeDtypeStruct((t, d3), BF16), jax.ShapeDtypeStruct((8, d), F32)],
        compiler_params=_params(("arbitrary",)), name=name,
    )(bcx, bcx, bcx, dy, dy, wconv)


def _ffn_mid_fwd(up, wconv, bconv, *, name, rb=128, cw=512):
    t, f2 = up.shape
    f = f2 // 2
    rb, cw = _tile(t, rb, 16), _tile(f, cw)

    def body(cur_ref, prev_ref, w_ref, b_ref, act_ref, conv_ref):
        first = pl.program_id(0) == 0

        def conv(cols):
            prev = prev_ref[:, cols].astype(F32)[BF16_HALO - HALO:]
            e = jnp.concatenate([jnp.where(first, 0.0, prev), cur_ref[:, cols].astype(F32)], axis=0)
            w = w_ref[:, cols]
            out = (w[0:1] * _shift(e, 2) + w[1:2] * _shift(e, 1) + w[2:3] * e + b_ref[:, cols])[HALO:]
            conv_ref[:, cols] = out.astype(BF16)
            return out

        for c0 in range(0, f, cw):
            g = conv(slice(c0, c0 + cw))
            a = conv(slice(f + c0, f + c0 + cw))
            act_ref[:, c0:c0 + cw] = (g * _sigmoid(g) * a).astype(BF16)

    return pl.pallas_call(
        body, grid=(t // rb,),
        in_specs=[_row_spec(rb, f2), _prev_spec16(rb, f2), _full_spec((3, f2)), _full_spec((1, f2))],
        out_specs=[_row_spec(rb, f), _row_spec(rb, f2)],
        out_shape=[jax.ShapeDtypeStruct((t, f), BF16), jax.ShapeDtypeStruct((t, f2), BF16)],
        compiler_params=_params(("parallel",)), name=name,
    )(up, up, wconv, bconv)


def _ffn_mid_bwd(up, conv, dact, wconv, *, name, rb=128, cw=512):
    t, f2 = up.shape
    f = f2 // 2
    rb, cw = _tile(t, rb, 16), _tile(f, cw)

    def body(up_ref, conv_ref, convn_ref, da_ref, dan_ref, w_ref, o_ref, dwb_ref):
        i = pl.program_id(0)
        last = i == pl.num_programs(0) - 1

        @pl.when(i == 0)
        def _():
            dwb_ref[...] = jnp.zeros_like(dwb_ref)

        def rows(cols):
            return jnp.concatenate([conv_ref[:, cols].astype(F32), convn_ref[:, cols].astype(F32)[0:HALO]], axis=0)

        def back(dc, cols):
            w = w_ref[:, cols]
            dc1, dc2 = _shift(dc, -1)[:rb], _shift(dc, -2)[:rb]
            dc0 = dc[:rb]
            o_ref[:, cols] = (w[2:3] * dc0 + w[1:2] * dc1 + w[0:1] * dc2).astype(BF16)
            u = up_ref[:, cols].astype(F32)
            ones = jnp.ones((8, rb), BF16)
            for k, prod in enumerate((dc2 * u, dc1 * u, dc0 * u, dc0)):
                dwb_ref[k:k + 1, cols] += jnp.dot(ones, prod.astype(BF16), preferred_element_type=F32)[0:1]

        for c0 in range(0, f, cw):
            gcols, acols = slice(c0, c0 + cw), slice(f + c0, f + c0 + cw)
            g, a = rows(gcols), rows(acols)
            da = jnp.concatenate([da_ref[:, gcols], jnp.where(last, 0.0, dan_ref[:, gcols])], axis=0)
            sg = _sigmoid(g)
            back(da * a * (sg * (1.0 + g * (1.0 - sg))), gcols)
            back(da * (g * sg), acols)

    return pl.pallas_call(
        body, grid=(t // rb,),
        in_specs=[_row_spec(rb, f2), _row_spec(rb, f2), _next_spec16(rb, f2, t), _row_spec(rb, f), _next_spec(rb, f, t),
                  _full_spec((3, f2))],
        out_specs=[_row_spec(rb, f2), _full_spec((8, f2))],
        out_shape=[jax.ShapeDtypeStruct((t, f2), BF16), jax.ShapeDtypeStruct((8, f2), F32)],
        compiler_params=_params(("arbitrary",)), name=name,
    )(up, conv, conv, dact, dact, wconv)


def _causal_mask():
    return lax.broadcasted_iota(jnp.int32, (CHUNK, CHUNK), 0) >= lax.broadcasted_iota(jnp.int32, (CHUNK, CHUNK), 1)


def _b_mid_fwd(zp, vnorm, ws, bs, *, name, rb=256):
    t, d2 = zp.shape
    d = d2 // 2
    c = d // GROUPS
    rb = _tile(t, rb, CHUNK)

    def body(zp_ref, gv_ref, ws_ref, bs_ref, ug_ref, vn_ref, gate_ref):
        v = _gelu(zp_ref[:, d:].astype(F32))
        rv = lax.rsqrt(jnp.mean(v * v, axis=-1, keepdims=True) + RMS_EPS)
        vn_ref[...] = ((v * rv) * gv_ref[...]).astype(BF16)
        mask = _causal_mask()
        for h in range(GROUPS):
            hc = slice(h * c, (h + 1) * c)
            wm = jnp.where(mask, ws_ref[h], 0.0).astype(BF16)
            bcol = jnp.broadcast_to(bs_ref[h:h + 1, :], (CHUNK, CHUNK)).T[:, 0:1]
            for n in range(rb // CHUNK):
                rows = slice(n * CHUNK, (n + 1) * CHUNK)
                gate_ref[rows, hc] = jnp.dot(wm, vn_ref[rows, hc], preferred_element_type=F32) + bcol
        ug_ref[...] = (_gelu(zp_ref[:, :d].astype(F32)) * gate_ref[...]).astype(BF16)

    return pl.pallas_call(
        body, grid=(t // rb,),
        in_specs=[_row_spec(rb, d2), _full_spec((1, d)), _full_spec((GROUPS, CHUNK, CHUNK)), _full_spec((GROUPS, CHUNK))],
        out_specs=_row_spec(rb, d), out_shape=jax.ShapeDtypeStruct((t, d), BF16),
        scratch_shapes=[pltpu.VMEM((rb, d), BF16), pltpu.VMEM((rb, d), F32)],
        compiler_params=_params(("parallel",)), name=name,
    )(zp, vnorm, ws, bs)


def _b_mid_bwd(zp, dug, vnorm, ws, bs, *, name, rb=256):
    t, d2 = zp.shape
    d = d2 // 2
    c = d // GROUPS
    rb = _tile(t, rb, CHUNK)

    def body(zp_ref, dug_ref, gv_ref, ws_ref, bs_ref, dzp_ref, dws_ref, dbs_ref, dgv_ref,
             vn_ref, gate_ref, dm_ref, dvn_ref, dbacc_ref):
        i = pl.program_id(0)

        @pl.when(i == 0)
        def _():
            dws_ref[...] = jnp.zeros_like(dws_ref)
            dgv_ref[...] = jnp.zeros_like(dgv_ref)
            dbacc_ref[...] = jnp.zeros_like(dbacc_ref)

        zu, zv = zp_ref[:, :d].astype(F32), zp_ref[:, d:].astype(F32)
        u, v = _gelu(zu), _gelu(zv)
        rv = lax.rsqrt(jnp.mean(v * v, axis=-1, keepdims=True) + RMS_EPS)
        vhat = v * rv
        gv = gv_ref[...]
        vn_ref[...] = (vhat * gv).astype(BF16)
        dug_v = dug_ref[...].astype(F32)
        dm = dug_v * u
        dm_ref[...] = dm.astype(BF16)
        mask = _causal_mask()
        for h in range(GROUPS):
            hc = slice(h * c, (h + 1) * c)
            wm = jnp.where(mask, ws_ref[h], 0.0)
            wm_b, wmt_b = wm.astype(BF16), wm.T.astype(BF16)
            bcol = jnp.broadcast_to(bs_ref[h:h + 1, :], (CHUNK, CHUNK)).T[:, 0:1]
            dws_h = jnp.zeros((CHUNK, CHUNK), F32)
            dbs_h = jnp.zeros((CHUNK, c), F32)
            for n in range(rb // CHUNK):
                rows = slice(n * CHUNK, (n + 1) * CHUNK)
                vn_c, dm_c = vn_ref[rows, hc], dm_ref[rows, hc]
                gate_ref[rows, hc] = jnp.dot(wm_b, vn_c, preferred_element_type=F32) + bcol
                dws_h += lax.dot_general(dm_c, vn_c, (((1,), (1,)), ((), ())), preferred_element_type=F32)
                dvn_ref[rows, hc] = jnp.dot(wmt_b, dm_c, preferred_element_type=F32)
                dbs_h += dm[rows, hc]
            dws_ref[h] += dws_h
            dbacc_ref[h] += dbs_h
        du = dug_v * gate_ref[...]
        dvn = dvn_ref[...]
        dvhat = dvn * gv
        m = jnp.mean(dvhat * vhat, axis=-1, keepdims=True)
        dv = rv * (dvhat - vhat * m)
        dgv_ref[0:1, :] += jnp.sum(dvn * vhat, axis=0, keepdims=True)
        dzp_ref[:, :d] = (du * _gelu_grad(zu)).astype(BF16)
        dzp_ref[:, d:] = (dv * _gelu_grad(zv)).astype(BF16)

        @pl.when(i == pl.num_programs(0) - 1)
        def _():
            ones = jnp.ones((8, c), F32)
            for h in range(GROUPS):
                dws_ref[h] = jnp.where(mask, dws_ref[h], 0.0)
                row = lax.dot_general(ones, dbacc_ref[h], (((1,), (1,)), ((), ())),
                                      precision=lax.Precision.HIGHEST, preferred_element_type=F32)
                dbs_ref[h:h + 1, :] = row[0:1]

    return pl.pallas_call(
        body, grid=(t // rb,),
        in_specs=[_row_spec(rb, d2), _row_spec(rb, d), _full_spec((1, d)), _full_spec((GROUPS, CHUNK, CHUNK)),
                  _full_spec((GROUPS, CHUNK))],
        out_specs=[_row_spec(rb, d2), _full_spec((GROUPS, CHUNK, CHUNK)), _full_spec((GROUPS, CHUNK)), _full_spec((8, d))],
        out_shape=[jax.ShapeDtypeStruct((t, d2), BF16), jax.ShapeDtypeStruct((GROUPS, CHUNK, CHUNK), F32),
                   jax.ShapeDtypeStruct((GROUPS, CHUNK), F32), jax.ShapeDtypeStruct((8, d), F32)],
        scratch_shapes=[pltpu.VMEM((rb, d), BF16), pltpu.VMEM((rb, d), F32), pltpu.VMEM((rb, d), BF16),
                        pltpu.VMEM((rb, d), F32), pltpu.VMEM((GROUPS, CHUNK, c), F32)],
        compiler_params=_params(("arbitrary",)), name=name,
    )(zp, dug, vnorm, ws, bs)


def _cast_into_full(w, layer, kind, place, *, name, dep=None, rb=256):
    _, r, c = w.shape
    rb = _tile(r, rb, 16)
    nrb = r // rb
    full = (r, c * N_CHIPS) if kind == "col" else (r * N_CHIPS, c)

    def body(place_ref, w_ref, *rest):
        rest[-1][...] = w_ref[...].astype(BF16)

    def o_index(i, place):
        return (i, place[0]) if kind == "col" else (i + place[0] * nrb, 0)

    in_specs = [pl.BlockSpec((None, rb, c), lambda i, place: (layer, i, 0))] + ([ANY] if dep is not None else [])
    return pl.pallas_call(
        body,
        grid_spec=pltpu.PrefetchScalarGridSpec(num_scalar_prefetch=1, grid=(nrb,), in_specs=in_specs,
                                               out_specs=pl.BlockSpec((rb, c), o_index)),
        out_shape=jax.ShapeDtypeStruct(full, BF16), compiler_params=_params(("parallel",)), name=name,
    )(place, w, *(() if dep is None else (dep,)))


def _adamw_layer(w, g, m, v, layer, prev, *, name, rb=128):
    _, r, c = w.shape
    rb = _tile(r, rb, 8)
    c1 = 1.0 - ADAM_B1 ** ADAM_STEP
    c2 = 1.0 - ADAM_B2 ** ADAM_STEP

    def body(w_ref, g_ref, m_ref, v_ref, *rest):
        go_ref, d_ref, nm_ref, nv_ref = rest[-4:]
        gv = g_ref[...]
        nm = ADAM_B1 * m_ref[...] + (1.0 - ADAM_B1) * gv
        nv = ADAM_B2 * v_ref[...] + (1.0 - ADAM_B2) * (gv * gv)
        go_ref[...] = gv
        nm_ref[...] = nm
        nv_ref[...] = nv
        d_ref[...] = -ADAM_LR * ((nm / c1) / (jnp.sqrt(nv / c2) + ADAM_EPS) + ADAM_WD * w_ref[...])

    lay = pl.BlockSpec((None, rb, c), lambda i: (layer, i, 0))
    return pl.pallas_call(
        body, grid=(r // rb,), in_specs=[lay, _row_spec(rb, c), lay, lay] + ([ANY] * 4 if prev else []), out_specs=[lay] * 4,
        out_shape=[jax.ShapeDtypeStruct(w.shape, F32)] * 4, input_output_aliases={4 + k: k for k in range(4)} if prev else {},
        compiler_params=_params(("parallel",)), name=name,
    )(w, g, m, v, *(prev or ()))


HBM = pl.BlockSpec(memory_space=pltpu.HBM)
SEM = pl.BlockSpec(memory_space=pltpu.SEMAPHORE)
SIDE_EFFECT = pltpu.SideEffectType.DATAFLOW_SIDE_EFFECTING


def _place():
    x, y, c = lax.axis_index("x"), lax.axis_index("y"), lax.axis_index("c")
    chips = [(1 - x, y), (x, 1 - y), (1 - x, 1 - y)]
    return x, y, c, 2 * x + y, chips


def _half(ref, kind, c):
    r, w = ref.shape
    if kind == "col":
        return ref.at[pl.ds(pl.multiple_of(c * (r // 2), 8), r // 2), :]
    return ref.at[:, pl.ds(pl.multiple_of(c * (w // 2), 128), w // 2)]


def _shard(ref, kind, s):
    r, w = ref.shape
    if kind == "col":
        return ref.at[:, pl.ds(pl.multiple_of(s * (w // N_CHIPS), 128), w // N_CHIPS)]
    return ref.at[pl.ds(pl.multiple_of(s * (r // N_CHIPS), 8), r // N_CHIPS), :]


def _remote(src, dst, send_sem, recv_sem, dev):
    return pltpu.make_async_remote_copy(src_ref=src, dst_ref=dst, send_sem=send_sem, recv_sem=recv_sem,
                                        device_id=dev, device_id_type=MESH)


def _start(name, bufs, plan, sem_shape, dep=None):
    n = len(bufs)
    n_in = n + (dep is not None)

    def body(*refs):
        sends, _ = plan(refs[:n], refs[n_in], refs[n_in + 1])
        for cp in sends:
            cp.start()
        refs[n_in + 2 + n][...] = jnp.zeros((8, 128), F32)

    dma = pltpu.SemaphoreType.DMA
    outs = pl.pallas_call(
        body, name=name,
        out_shape=(dma(sem_shape), dma(sem_shape), *[pltpu.HBM(b.shape, b.dtype) for b in bufs], jax.ShapeDtypeStruct((8, 128), F32)),
        in_specs=(HBM,) * n + ((ANY,) if dep is not None else ()),
        out_specs=(SEM, SEM) + (HBM,) * n + (pl.BlockSpec(memory_space=pltpu.VMEM),),
        input_output_aliases={i: i + 2 for i in range(n)},
        compiler_params=pltpu.CompilerParams(has_side_effects=SIDE_EFFECT),
    )(*[pltpu.with_memory_space_constraint(b, pltpu.HBM) for b in bufs], *(() if dep is None else (dep,)))
    return outs[0], outs[1], list(outs[2:2 + n]), outs[2 + n]


def _wait(name, started, plan, after):
    send, recv, bufs, _ = started
    n = len(bufs)

    def body(*refs):
        sends, recvs = plan(refs[:n], refs[n], refs[n + 1])
        for cp in sends:
            cp.wait_send()
        for cp in recvs:
            cp.wait_recv()

    return list(pl.pallas_call(
        body, name=name, out_shape=tuple(pltpu.HBM(b.shape, b.dtype) for b in bufs),
        in_specs=(HBM,) * n + (SEM, SEM, ANY), out_specs=(HBM,) * n, input_output_aliases={i: i for i in range(n)},
        compiler_params=pltpu.CompilerParams(has_side_effects=SIDE_EFFECT),
    )(*bufs, send, recv, after))


KINDS = ("col", "row")


def _gather_ici_plan(n_small):
    def plan(refs, send, recv):
        x, y, c, s, chips = _place()
        n = len(KINDS) + n_small
        sends, recvs = [], []
        for k, (px, py) in enumerate(chips):
            sp = 2 * px + py
            for a, kind in enumerate(KINDS):
                mine, theirs = _half(_shard(refs[a], kind, s), kind, c), _half(_shard(refs[a], kind, sp), kind, c)
                sends.append(_remote(mine, mine, send.at[k * n + a], recv.at[k * n + a], (px, py, c)))
                recvs.append(_remote(theirs, theirs, send.at[k * n + a], recv.at[k * n + a], (px, py, c)))
            for b in range(n_small):
                ref, sem = refs[len(KINDS) + b], k * n + len(KINDS) + b
                sends.append(_remote(ref.at[s], ref.at[s], send.at[sem], recv.at[sem], (px, py, c)))
                recvs.append(_remote(ref.at[sp], ref.at[sp], send.at[sem], recv.at[sem], (px, py, c)))
        return sends, recvs
    return plan


def _gather_d2d_plan(refs, send, recv):
    x, y, c, _, chips = _place()
    n = len(KINDS)
    sends, recvs = [], []
    for k, (px, py) in enumerate(chips):
        for a, kind in enumerate(KINDS):
            region, sem = _shard(refs[a], kind, 2 * px + py), k * n + a
            sends.append(_remote(_half(region, kind, c), _half(region, kind, c), send.at[sem], recv.at[sem], (x, y, 1 - c)))
            recvs.append(_remote(_half(region, kind, 1 - c), _half(region, kind, 1 - c), send.at[sem], recv.at[sem], (x, y, 1 - c)))
    return sends, recvs


def _swap_plan(refs, send, recv):
    x, y, c, _, _ = _place()
    n = len(KINDS)
    cps = [_remote(_half(refs[a], KINDS[a], 1 - c), refs[n + a], send.at[a], recv.at[a], (x, y, 1 - c)) for a in range(n)]
    return cps, cps


def _exchange_plan(refs, send, recv):
    x, y, c, _, chips = _place()
    n = len(KINDS)
    cps = []
    for k, (px, py) in enumerate(chips):
        for a in range(n):
            cps.append(_remote(_shard(refs[a], KINDS[a], 2 * px + py), refs[n + a].at[k], send.at[k * n + a], recv.at[k * n + a],
                               (px, py, c)))
    return cps, cps


def _share_plan(refs, send, recv):
    x, y, c, _, _ = _place()
    sends = [_remote(_half(refs[a], KINDS[a], c), _half(refs[a], KINDS[a], c), send.at[a], recv.at[a], (x, y, 1 - c))
             for a in range(len(KINDS))]
    recvs = [_remote(_half(refs[a], KINDS[a], 1 - c), _half(refs[a], KINDS[a], 1 - c), send.at[a], recv.at[a], (x, y, 1 - c))
             for a in range(len(KINDS))]
    return sends, recvs


def _spread_plan(refs, send, recv):
    packed, slots = refs
    x, y, c, _, _ = _place()
    sends, recvs = [], []
    for k in range(1, 8):
        px, py, pc = x ^ (k >> 2), y ^ ((k >> 1) & 1), c ^ (k & 1)
        sends.append(_remote(packed, slots.at[4 * x + 2 * y + c], send.at[k - 1], recv.at[k - 1], (px, py, pc)))
        recvs.append(_remote(packed, slots.at[4 * px + 2 * py + pc], send.at[k - 1], recv.at[k - 1], (px, py, pc)))
    return sends, recvs


def _half_index(kind, nblk):
    def index(i, j, place):
        return (i + place[1] * nblk[0], j) if kind == "col" else (i, j + place[1] * nblk[1])
    return index


def _chip_partial(g, other, kind, place, *, name):
    hr, hc = other.shape
    rb, cb = _tile(hr, 512, 16), _tile(hc, 1024)
    nblk = (hr // rb, hc // cb)

    def body(place_ref, g_ref, o_ref, p_ref):
        p_ref[...] = (g_ref[...].astype(F32) + o_ref[...].astype(F32)).astype(BF16)

    plain = pl.BlockSpec((rb, cb), lambda i, j, place: (i, j))
    return pl.pallas_call(
        body,
        grid_spec=pltpu.PrefetchScalarGridSpec(
            num_scalar_prefetch=1, grid=nblk, in_specs=[pl.BlockSpec((rb, cb), _half_index(kind, nblk)), plain], out_specs=plain),
        out_shape=jax.ShapeDtypeStruct((hr, hc), BF16), compiler_params=_params(("parallel", "parallel")), name=name,
    )(place, g, other)


def _reduce_half(g, other, recv, kind, place, *, name):
    _, pr, pc = recv.shape
    rb, cb = _tile(pr, 512, 16), _tile(pc, 1024)
    nblk = (pr // rb, pc // cb)
    full = (pr * 2, pc) if kind == "col" else (pr, pc * 2)

    def g_index(i, j, place):
        s, c = place[0], place[1]
        return (i + c * nblk[0], j + s * nblk[1]) if kind == "col" else (i + s * nblk[0], j + c * nblk[1])

    def o_index(i, j, place):
        return (i, j + place[0] * nblk[1]) if kind == "col" else (i + place[0] * nblk[0], j)

    def body(place_ref, g_ref, o_ref, r_ref, out_ref):
        acc = g_ref[...].astype(F32) + o_ref[...].astype(F32)
        for k in range(3):
            acc = acc + r_ref[k].astype(F32)
        out_ref[...] = acc

    return pl.pallas_call(
        body,
        grid_spec=pltpu.PrefetchScalarGridSpec(
            num_scalar_prefetch=1, grid=nblk,
            in_specs=[pl.BlockSpec((rb, cb), g_index), pl.BlockSpec((rb, cb), o_index),
                      pl.BlockSpec((3, rb, cb), lambda i, j, place: (0, i, j))],
            out_specs=pl.BlockSpec((rb, cb), _half_index(kind, nblk))),
        out_shape=jax.ShapeDtypeStruct(full, F32), compiler_params=_params(("parallel", "parallel")), name=name,
    )(place, g, other, recv)


def _sum_slots(packed, slots, me, *, name, rb=512):
    r, w = packed.shape
    rb = _tile(r, rb, 8)

    def body(me_ref, p_ref, s_ref, o_ref):
        acc = None
        for j in range(8):
            term = jnp.where(me_ref[0] == j, p_ref[...], s_ref[j])
            acc = term if acc is None else acc + term
        o_ref[...] = acc

    return pl.pallas_call(
        body,
        grid_spec=pltpu.PrefetchScalarGridSpec(
            num_scalar_prefetch=1, grid=(r // rb,),
            in_specs=[pl.BlockSpec((rb, w), lambda i, me: (i, 0)), pl.BlockSpec((8, rb, w), lambda i, me: (0, i, 0))],
            out_specs=pl.BlockSpec((rb, w), lambda i, me: (i, 0))),
        out_shape=jax.ShapeDtypeStruct((r, w), F32), compiler_params=_params(("parallel",)), name=name,
    )(me, packed, slots)


def _half_shape(a, kind):
    return (a.shape[0] // 2, a.shape[1]) if kind == "col" else (a.shape[0], a.shape[1] // 2)


def _rs_swap(tag, grads):
    others = [lax.empty(_half_shape(g, k), g.dtype) for g, k in zip(grads, KINDS)]
    return _start(f"rs_{tag}_swap", list(grads) + others, _swap_plan, (len(KINDS),))


def _rs_exchange(tag, swapped, place, after):
    bufs = _wait(f"rs_{tag}_swap_wait", swapped, _swap_plan, after)
    n = len(KINDS)
    grads, others = bufs[:n], bufs[n:]
    parts = [_chip_partial(g, o, k, place, name=f"rs_{tag}_partial_{k}") for g, o, k in zip(grads, others, KINDS)]
    lands = []
    for p, k in zip(parts, KINDS):
        piece = (p.shape[0], p.shape[1] // N_CHIPS) if k == "col" else (p.shape[0] // N_CHIPS, p.shape[1])
        lands.append(lax.empty((3,) + piece, p.dtype))
    return _start(f"rs_{tag}_exchange", parts + lands, _exchange_plan, (3 * n,)), grads, others


def _rs_share(tag, exchanged, place, after):
    started, grads, others = exchanged
    n = len(KINDS)
    recvs = _wait(f"rs_{tag}_exchange_wait", started, _exchange_plan, after)[n:]
    halves = [_reduce_half(g, o, r, k, place, name=f"rs_{tag}_reduce_{k}") for g, o, r, k in zip(grads, others, recvs, KINDS)]
    return _start(f"rs_{tag}_share", halves, _share_plan, (n,))


def _rs_finish(tag, shared, after):
    return _wait(f"rs_{tag}_share_wait", shared, _share_plan, after)


def _spread(tag, parts, dep):
    rows = [p.reshape(-1, 128) for p in parts]
    n = sum(r.shape[0] for r in rows)
    rows.append(jnp.zeros(((-n) % 512, 128), F32))
    packed = jnp.concatenate(rows, axis=0)
    return _start(f"small_{tag}_spread", [packed, lax.empty((8,) + packed.shape, F32)], _spread_plan, (7,), dep=dep)


def _spread_sum(tag, started, parts, me, after):
    packed, slots = _wait(f"small_{tag}_spread_wait", started, _spread_plan, after)
    total = _sum_slots(packed, slots, me, name=f"small_{tag}_sum")
    out, row = [], 0
    for p in parts:
        n = p.size // 128
        out.append(total[row:row + n].reshape(p.shape))
        row += n
    return out


def _ffn_fwd(x, h, w_up, conv_w, conv_b, w_down, tag, tm_up=1024):
    up = _mm(h, w_up, tm=tm_up, out_dtype=BF16, name=f"ffn{tag}_up")
    act, conv = _ffn_mid_fwd(up, conv_w, conv_b, name=f"ffn{tag}_mid")
    x_out = _mm(act, w_down, res=x, tk=2816, name=f"ffn{tag}_down")
    return x_out, (up, conv, act)


def kernel(x, a_norm, a_in, a_conv, a_out, b_norm, b_in, b_vnorm, b_ws, b_bs, b_out, f_norm, f_up, f_conv_w, f_conv_b, f_down, final_norm, loss_target, m_a_norm, m_a_in, m_a_conv, m_a_out, m_b_norm, m_b_in, m_b_vnorm, m_b_ws, m_b_bs, m_b_out, m_f_norm, m_f_up, m_f_conv_w, m_f_conv_b, m_f_down, m_final_norm, v_a_norm, v_a_in, v_a_conv, v_a_out, v_b_norm, v_b_in, v_b_vnorm, v_b_ws, v_b_bs, v_b_out, v_f_norm, v_f_up, v_f_conv_w, v_f_conv_b, v_f_down, v_final_norm):
    t, d = x.shape[1], x.shape[2]
    f2 = f_up.shape[2] * N_CHIPS
    x0, tgt = x.reshape(t, d), loss_target.reshape(t, d)
    ax, ay, ac = lax.axis_index("x"), lax.axis_index("y"), lax.axis_index("c")
    s = 2 * ax + ay
    place = jnp.stack([s, ac]).astype(jnp.int32)
    me = (4 * ax + 2 * ay + ac).astype(jnp.int32).reshape(1)

    def stacked(a):
        return lax.dynamic_update_index_in_dim(jnp.zeros((N_CHIPS,) + a.shape, F32), a, s, 0)

    def cast_pair(tag, w_in, w_out, layer, dep):
        return [_cast_into_full(w_in, layer, "col", place, name=f"cast_{tag}_in", dep=dep),
                _cast_into_full(w_out, layer, "row", place, name=f"cast_{tag}_out", dep=dep)]

    def gather_start(tag, fulls, small, dep):
        return _start(f"ag_{tag}_ici", fulls + small, _gather_ici_plan(len(small)), (3 * (2 + len(small)),), dep=dep)

    def gather_forward(tag, started, n_small, after):
        bufs = _wait(f"ag_{tag}_ici_wait", started, _gather_ici_plan(n_small), after)
        return _start(f"ag_{tag}_d2d", bufs[:2], _gather_d2d_plan, (3 * 2,)), bufs[2:]

    def gather_finish(tag, forwarded, after):
        return _wait(f"ag_{tag}_d2d_wait", forwarded, _gather_d2d_plan, after)

    small = [stacked(a_conv[0]), stacked(b_norm), stacked(b_vnorm), stacked(f_conv_w.reshape(2 * 3, -1))]
    ag_a = gather_start("a", cast_pair("a", a_in, a_out, 0, None), small, None)
    full_f0 = cast_pair("f0", f_up, f_down, 0, ag_a[3])
    full_b = cast_pair("b", b_in, b_out, 0, full_f0[1])
    full_f1 = cast_pair("f1", f_up, f_down, 1, full_b[1])

    def unshard(a):
        return jnp.transpose(a, (1, 0, 2)).reshape(a.shape[1], -1)

    ws, bs = b_ws[0], b_bs[0]

    h0 = _rms_fwd(x0, a_norm, dep=full_f1[1], name="a_norm")
    fw_a, (g_aconv, g_bnorm, g_bvnorm, g_fconv) = gather_forward("a", ag_a, 4, h0)
    ag_f0 = gather_start("f0", full_f0, [], fw_a[3])
    ag_b = gather_start("b", full_b, [], ag_f0[3])
    ag_f1 = gather_start("f1", full_f1, [], ag_b[3])
    w_ai, w_ao = gather_finish("a", fw_a, ag_f1[3])
    a_conv_f, b_norm_f, b_vnorm_f = unshard(g_aconv), unshard(g_bnorm), unshard(g_bvnorm)
    f_conv_f = unshard(g_fconv).reshape(2, 3, f2)
    bcx = _mm(h0, w_ai, tm=2048, out_dtype=BF16, name="a_in")
    y = _a_mid_fwd(bcx, a_conv_f, name="a_mid")
    x1 = _mm(y, w_ao, res=x0, tm=512, tn=2048, name="a_out")
    fw_f0, _ = gather_forward("f0", ag_f0, 0, x1)
    h1 = _rms_fwd(x1, f_norm[0:1], dep=fw_f0[3], name="ffn0_norm")
    w_up0, w_dn0 = gather_finish("f0", fw_f0, h1)
    x2, (up0, conv0, act0) = _ffn_fwd(x1, h1, w_up0, f_conv_f[0], f_conv_b[0:1], w_dn0, 0, tm_up=2048)
    fw_b, _ = gather_forward("b", ag_b, 0, up0)
    w_bi, w_bo = gather_finish("b", fw_b, act0)
    h2 = _rms_fwd(x2, b_norm_f, name="b_norm")
    zp = _mm(h2, w_bi, tm=2048, out_dtype=BF16, name="b_in")
    fw_f1, _ = gather_forward("f1", ag_f1, 0, zp)
    ug = _b_mid_fwd(zp, b_vnorm_f, ws, bs, name="b_mid")
    x3, h3 = _mm(ug, w_bo, res=x2, norm=f_norm[1:2], tm=512, tn=2048, name="b_out")
    w_up1, w_dn1 = gather_finish("f1", fw_f1, x3)
    x4, (up1, conv1, act1) = _ffn_fwd(x3, h3, w_up1, f_conv_f[1], f_conv_b[1:2], w_dn1, 1, tm_up=2048)
    loss_rows, dx4, dx4b, d_final = _final(x4, tgt, final_norm.reshape(1, d), name="final")

    d_dn1 = _mm(act1, dx4b, ta=True, tm=1408, out_dtype=BF16, name="ffn1_ddown")
    dact1 = _mm(dx4b, w_dn1, tb=True, tn=512, tm=2048, name="ffn1_dact")
    dup1, d_fwb1 = _ffn_mid_bwd(up1, conv1, dact1, f_conv_f[1], name="ffn1_mid_bwd")
    d_up1 = _mm(h3, dup1, ta=True, out_dtype=BF16, tk=4096, name="ffn1_dup")
    sw_f1 = _rs_swap("f1", [d_up1, d_dn1])
    dh3 = _mm(dup1, w_up1, tb=True, tk=2816, out_dtype=BF16, dep=sw_f1[3], name="ffn1_dh")
    dx3, dx3b, d_fnorm1 = _rms_bwd(dh3, x3, f_norm[1:2], dx4, name="ffn1_norm_bwd")
    ex_f1 = _rs_exchange("f1", sw_f1, place, dx3)

    d_bo = _mm(ug, dx3b, ta=True, out_dtype=BF16, tk=4096, dep=ex_f1[0][3], name="b_dout")
    dug = _mm(dx3b, w_bo, tb=True, tm=512, tn=2048, out_dtype=BF16, name="b_dug")
    dzp, d_ws, d_bs, d_bvnorm = _b_mid_bwd(zp, dug, b_vnorm_f, ws, bs, name="b_mid_bwd")
    d_bi = _mm(h2, dzp, ta=True, out_dtype=BF16, tk=4096, name="b_din")
    sw_b = _rs_swap("b", [d_bi, d_bo])
    dh2 = _mm(dzp, w_bi, tb=True, tk=4096, out_dtype=BF16, dep=sw_b[3], name="b_dh")
    dx2, dx2b, d_bnorm = _rms_bwd(dh2, x2, b_norm_f, dx3, name="b_norm_bwd")
    ex_b = _rs_exchange("b", sw_b, place, dx2)

    d_dn0 = _mm(act0, dx2b, ta=True, tm=1408, out_dtype=BF16, dep=ex_b[0][3], name="ffn0_ddown")
    dact0 = _mm(dx2b, w_dn0, tb=True, tn=512, tm=2048, name="ffn0_dact")
    dup0, d_fwb0 = _ffn_mid_bwd(up0, conv0, dact0, f_conv_f[0], name="ffn0_mid_bwd")
    sh_f1 = _rs_share("f1", ex_f1, place, dup0)
    d_up0 = _mm(h1, dup0, ta=True, out_dtype=BF16, tk=4096, dep=sh_f1[3], name="ffn0_dup")
    sw_f0 = _rs_swap("f0", [d_up0, d_dn0])
    g_up1, g_dn1 = _rs_finish("f1", sh_f1, sw_f0[3])
    dh1 = _mm(dup0, w_up0, tb=True, tk=2816, out_dtype=BF16, dep=sw_f0[3], name="ffn0_dh")
    dx1, dx1b, d_fnorm0 = _rms_bwd(dh1, x1, f_norm[0:1], dx2, name="ffn0_norm_bwd")
    ex_f0 = _rs_exchange("f0", sw_f0, place, dx1)
    early = [jnp.concatenate([d_bnorm, d_bvnorm, d_fnorm0, d_fnorm1, d_final, loss_rows], axis=0),
             jnp.concatenate([d_fwb0, d_fwb1], axis=0), jnp.concatenate([d_ws.reshape(-1, CHUNK), d_bs], axis=0)]
    sp_early = _spread("early", early, ex_f0[0][3])
    sh_b = _rs_share("b", ex_b, place, sp_early[3])

    d_ao = _mm(y, dx1b, ta=True, out_dtype=BF16, tk=4096, dep=sh_b[3], name="a_dout")
    dyy = _mm(dx1b, w_ao, tb=True, tm=512, tn=2048, name="a_dy")
    dbcx, d_aconv = _a_mid_bwd(bcx, dyy, a_conv_f, name="a_mid_bwd")
    d_ai = _mm(h0, dbcx, ta=True, out_dtype=BF16, tk=4096, name="a_din")
    sw_a = _rs_swap("a", [d_ai, d_ao])
    g_bi, g_bo = _rs_finish("b", sh_b, sw_a[3])
    ex_a = _rs_exchange("a", sw_a, place, g_bi)
    dh0 = _mm(dbcx, w_ai, tb=True, tk=3072, out_dtype=BF16, dep=ex_a[0][3], name="a_dh")
    grad_x, _, d_anorm = _rms_bwd(dh0, x0, a_norm, dx1, name="a_norm_bwd")
    late = [jnp.concatenate([d_anorm, d_aconv], axis=0)]
    sp_late = _spread("late", late, ex_a[0][3])
    sh_f0 = _rs_share("f0", ex_f0, place, sp_late[3])
    sh_a = _rs_share("a", ex_a, place, sh_f0[3])
    g_up0, g_dn0 = _rs_finish("f0", sh_f0, sh_a[3])
    g_ai, g_ao = _rs_finish("a", sh_a, g_up0)
    r_a, r_b, r_c = _spread_sum("early", sp_early, early, me, g_ai)
    (r_l,) = _spread_sum("late", sp_late, late, me, r_a)

    loss = jnp.sum(r_a[40])
    cs, fs = d // N_CHIPS, f2 // N_CHIPS

    def mine(a, width):
        return lax.dynamic_slice_in_dim(a, s * width, width, axis=1)

    grads = {
        "a_norm": r_l[0:1], "a_conv": mine(r_l[8:11], cs), "b_norm": mine(r_a[0:1], cs), "b_vnorm": mine(r_a[8:9], cs),
        "f_norm": jnp.concatenate([r_a[16:17], r_a[24:25]], axis=0), "final_norm": r_a[32:33],
        "b_ws": r_c[:GROUPS * CHUNK], "b_bs": r_c[GROUPS * CHUNK:],
        "f_conv_w": jnp.concatenate([mine(r_b[0:3], fs), mine(r_b[8:11], fs)], axis=0),
        "f_conv_b": jnp.concatenate([r_b[3:4], r_b[11:12]], axis=0),
        "a_in": g_ai, "a_out": g_ao, "b_in": g_bi, "b_out": g_bo,
    }
    names = ["a_norm", "a_in", "a_conv", "a_out", "b_norm", "b_in", "b_vnorm", "b_ws", "b_bs", "b_out", "f_norm", "f_up",
             "f_conv_w", "f_conv_b", "f_down", "final_norm"]
    weights = dict(zip(names, [a_norm, a_in, a_conv, a_out, b_norm, b_in, b_vnorm, b_ws, b_bs, b_out, f_norm, f_up, f_conv_w,
                               f_conv_b, f_down, final_norm]))
    ms = dict(zip(names, [m_a_norm, m_a_in, m_a_conv, m_a_out, m_b_norm, m_b_in, m_b_vnorm, m_b_ws, m_b_bs, m_b_out, m_f_norm,
                          m_f_up, m_f_conv_w, m_f_conv_b, m_f_down, m_final_norm]))
    vs = dict(zip(names, [v_a_norm, v_a_in, v_a_conv, v_a_out, v_b_norm, v_b_in, v_b_vnorm, v_b_ws, v_b_bs, v_b_out, v_f_norm,
                          v_f_up, v_f_conv_w, v_f_conv_b, v_f_down, v_final_norm]))
    result = {}
    for n in names:
        w = weights[n]
        if n in ("f_up", "f_down"):
            g1, g0 = (g_up1, g_up0) if n == "f_up" else (g_dn1, g_dn0)
            first = _adamw_layer(w, g1, ms[n], vs[n], 1, None, name=f"adamw_{n}1")
            result[n] = _adamw_layer(w, g0, ms[n], vs[n], 0, tuple(first), name=f"adamw_{n}0")
            continue
        g2 = grads[n]
        as3d = (lambda a: a.reshape((1,) + g2.shape))
        result[n] = [o.reshape(w.shape) for o in _adamw_layer(as3d(w), g2, as3d(ms[n]), as3d(vs[n]), 0, None, name=f"adamw_{n}")]

    return (loss, grad_x.reshape(x.shape), *[result[n][0] for n in names], *[result[n][1] for n in names],
            *[result[n][2] for n in names], *[result[n][3] for n in names])
```

```python
import jax
import jax.numpy as jnp
from jax import lax
from jax.experimental import pallas as pl
from jax.experimental.pallas import tpu as pltpu

F32 = jnp.float32
BF16 = jnp.bfloat16
MESH = pl.DeviceIdType.MESH
ANY = pl.BlockSpec(memory_space=pl.ANY)

RMS_EPS = 1e-5
CHUNK = 128
GROUPS = 8
ADAM_LR, ADAM_B1, ADAM_B2, ADAM_EPS, ADAM_WD, ADAM_STEP = 0.001, 0.9, 0.999, 1e-08, 0.01, 10

N_CHIPS = 4
HALO = 8
BF16_HALO = 16
VMEM_LIMIT = 56 * 1024 * 1024
GELU_C = 0.7978845608028654
GELU_A = 0.044715


def _params(sem=None):
    return pltpu.CompilerParams(dimension_semantics=sem, vmem_limit_bytes=VMEM_LIMIT)


def _tile(dim, pref, quantum=128):
    if dim <= pref:
        return dim
    t = (pref // quantum) * quantum
    while t >= quantum:
        if dim % t == 0:
            return t
        t -= quantum
    return dim


def _mm(a, b, *, name, ta=False, tb=False, res=None, norm=None, dep=None, out_dtype=F32, tm=1024, tn=1024, tk=2048):
    (K, M) = a.shape if ta else a.shape[::-1]
    N = b.shape[0] if tb else b.shape[1]
    assert (b.shape[1] if tb else b.shape[0]) == K
    tm, tn, tk = _tile(M, tm), _tile(N, tn), _tile(K, tk)
    nk = K // tk
    assert norm is None or (tn == N and nk == 1)
    a_spec = pl.BlockSpec((tk, tm), lambda i, j, k: (k, i)) if ta else pl.BlockSpec((tm, tk), lambda i, j, k: (i, k))
    b_spec = pl.BlockSpec((tn, tk), lambda i, j, k: (j, k)) if tb else pl.BlockSpec((tk, tn), lambda i, j, k: (k, j))
    o_spec = pl.BlockSpec((tm, tn), lambda i, j, k: (i, j))
    dims = (((0 if ta else 1,), (1 if tb else 0,)), ((), ()))
    direct = out_dtype == F32
    n_in = 2 + (res is not None) + (norm is not None) + (dep is not None)

    def body(*refs):
        a_ref, b_ref = refs[0], refs[1]
        r_ref = refs[2] if res is not None else None
        g_ref = refs[2 + (res is not None)] if norm is not None else None
        o_ref = refs[n_in]
        acc_ref = o_ref if direct else refs[-1]
        part = lax.dot_general(a_ref[...], b_ref[...], dims, preferred_element_type=F32)
        if nk == 1:
            if r_ref is not None:
                part = part + r_ref[...]
            o_ref[...] = part.astype(o_ref.dtype)
            if g_ref is not None:
                r = lax.rsqrt(jnp.mean(part * part, axis=-1, keepdims=True) + RMS_EPS)
                refs[n_in + 1][...] = ((part * r) * g_ref[...]).astype(BF16)
            return
        k = pl.program_id(2)

        @pl.when(k == 0)
        def _():
            acc_ref[...] = part

        @pl.when(jnp.logical_and(k > 0, k < nk - 1))
        def _():
            acc_ref[...] += part

        @pl.when(k == nk - 1)
        def _():
            tot = acc_ref[...] + part
            if r_ref is not None:
                tot = tot + r_ref[...]
            o_ref[...] = tot.astype(o_ref.dtype)

    in_specs = ([a_spec, b_spec] + ([o_spec] if res is not None else [])
                + ([pl.BlockSpec((1, tn), lambda i, j, k: (0, j))] if norm is not None else []) + ([ANY] if dep is not None else []))
    args = (a, b) + tuple(x for x in (res, norm, dep) if x is not None)
    scratch = [] if (direct or nk == 1) else [pltpu.VMEM((tm, tn), F32)]
    out_shape = jax.ShapeDtypeStruct((M, N), out_dtype)
    return pl.pallas_call(
        body, grid=(M // tm, N // tn, nk), in_specs=in_specs, out_specs=[o_spec, o_spec] if norm is not None else o_spec,
        out_shape=[out_shape, jax.ShapeDtypeStruct((M, N), BF16)] if norm is not None else out_shape, scratch_shapes=scratch,
        compiler_params=_params(("parallel", "parallel", "arbitrary")), name=name,
    )(*args)


def _row_spec(rb, w):
    return pl.BlockSpec((rb, w), lambda i: (i, 0))


def _next_spec(rb, w, t):
    return pl.BlockSpec((HALO, w), lambda i: (jnp.minimum((i + 1) * (rb // HALO), t // HALO - 1), 0))


def _prev_spec16(rb, w):
    return pl.BlockSpec((BF16_HALO, w), lambda i: (jnp.maximum(i * (rb // BF16_HALO) - 1, 0), 0))


def _next_spec16(rb, w, t):
    return pl.BlockSpec((BF16_HALO, w), lambda i: (jnp.minimum((i + 1) * (rb // BF16_HALO), t // BF16_HALO - 1), 0))


def _full_spec(shape):
    return pl.BlockSpec(shape, lambda i: tuple(0 for _ in shape))


def _shift(e, s):
    return pltpu.roll(e, s % e.shape[0], 0)


def _gelu(x):
    return 0.5 * x * (1.0 + jnp.tanh(GELU_C * (x + GELU_A * x * x * x)))


def _gelu_grad(x):
    th = jnp.tanh(GELU_C * (x + GELU_A * x * x * x))
    return 0.5 * (1.0 + th) + 0.5 * x * (1.0 - th * th) * (GELU_C * (1.0 + 3.0 * GELU_A * x * x))


def _sigmoid(x):
    return 1.0 / (1.0 + jnp.exp(-x))


def _rms_fwd(x, g, *, name, dep=None, rb=256):
    t, d = x.shape
    rb = _tile(t, rb, 8)

    def body(x_ref, g_ref, *rest):
        h_ref = rest[-1]
        xv = x_ref[...]
        r = lax.rsqrt(jnp.mean(xv * xv, axis=-1, keepdims=True) + RMS_EPS)
        h_ref[...] = ((xv * r) * g_ref[...]).astype(BF16)

    return pl.pallas_call(
        body, grid=(t // rb,), in_specs=[_row_spec(rb, d), _full_spec((1, d))] + ([ANY] if dep is not None else []),
        out_specs=_row_spec(rb, d), out_shape=jax.ShapeDtypeStruct((t, d), BF16), compiler_params=_params(("parallel",)), name=name,
    )(x, g, *(() if dep is None else (dep,)))


def _rms_bwd(dh, x, g, dres, *, name, rb=256):
    t, d = x.shape
    rb = _tile(t, rb, 16)

    def body(dh_ref, x_ref, g_ref, dres_ref, dx_ref, dxb_ref, dg_ref):
        xv = x_ref[...]
        r = lax.rsqrt(jnp.mean(xv * xv, axis=-1, keepdims=True) + RMS_EPS)
        xhat = xv * r
        dh_v = dh_ref[...].astype(F32)
        dxhat = dh_v * g_ref[...]
        m = jnp.mean(dxhat * xhat, axis=-1, keepdims=True)
        dx = dres_ref[...] + r * (dxhat - xhat * m)
        dx_ref[...] = dx
        dxb_ref[...] = dx.astype(BF16)

        @pl.when(pl.program_id(0) == 0)
        def _():
            dg_ref[...] = jnp.zeros_like(dg_ref)

        dg_ref[0:1, :] += jnp.sum(dh_v * xhat, axis=0, keepdims=True)

    return pl.pallas_call(
        body, grid=(t // rb,),
        in_specs=[_row_spec(rb, d), _row_spec(rb, d), _full_spec((1, d)), _row_spec(rb, d)],
        out_specs=[_row_spec(rb, d), _row_spec(rb, d), _full_spec((8, d))],
        out_shape=[jax.ShapeDtypeStruct((t, d), F32), jax.ShapeDtypeStruct((t, d), BF16), jax.ShapeDtypeStruct((8, d), F32)],
        compiler_params=_params(("arbitrary",)), name=name,
    )(dh, x, g, dres)


def _final(x, tgt, g, *, name, rb=256):
    t, d = x.shape
    rb = _tile(t, rb, 8)
    inv_d = 1.0 / d

    def body(x_ref, t_ref, g_ref, l_ref, dx_ref, dxb_ref, dg_ref):
        xv = x_ref[...]
        gv = g_ref[...]
        r = lax.rsqrt(jnp.mean(xv * xv, axis=-1, keepdims=True) + RMS_EPS)
        xhat = xv * r
        e = xhat * gv - t_ref[...]
        dy = e * inv_d
        dxhat = dy * gv
        m = jnp.mean(dxhat * xhat, axis=-1, keepdims=True)
        dx = r * (dxhat - xhat * m)
        dx_ref[...] = dx
        dxb_ref[...] = dx.astype(BF16)

        @pl.when(pl.program_id(0) == 0)
        def _():
            l_ref[...] = jnp.zeros_like(l_ref)
            dg_ref[...] = jnp.zeros_like(dg_ref)

        l_ref[0:1, :] += jnp.sum(e * e, axis=0, keepdims=True) * (0.5 * inv_d)
        dg_ref[0:1, :] += jnp.sum(dy * xhat, axis=0, keepdims=True)

    return pl.pallas_call(
        body, grid=(t // rb,),
        in_specs=[_row_spec(rb, d), _row_spec(rb, d), _full_spec((1, d))],
        out_specs=[_full_spec((8, d)), _row_spec(rb, d), _row_spec(rb, d), _full_spec((8, d))],
        out_shape=[jax.ShapeDtypeStruct((8, d), F32), jax.ShapeDtypeStruct((t, d), F32),
                   jax.ShapeDtypeStruct((t, d), BF16), jax.ShapeDtypeStruct((8, d), F32)],
        compiler_params=_params(("arbitrary",)), name=name,
    )(x, tgt, g)


def _a_mid_fwd(bcx, wconv, *, name, rb=256, cw=512):
    t, d3 = bcx.shape
    d = d3 // 3
    rb, cw = _tile(t, rb, 16), _tile(d, cw)

    def body(cur_ref, prev_ref, w_ref, y_ref):
        first = pl.program_id(0) == 0

        def f32(ref, cols):
            return ref[:, cols].astype(F32)

        for c0 in range(0, d, cw):
            cs = slice(c0, c0 + cw)
            gc, xs = slice(d + c0, d + c0 + cw), slice(2 * d + c0, 2 * d + c0 + cw)
            p_prev = jnp.where(first, 0.0, (f32(prev_ref, gc) * f32(prev_ref, xs))[BF16_HALO - HALO:])
            e = jnp.concatenate([p_prev, f32(cur_ref, gc) * f32(cur_ref, xs)], axis=0)
            w = w_ref[:, cs]
            q = w[0:1] * _shift(e, 2) + w[1:2] * _shift(e, 1) + w[2:3] * e
            y_ref[:, cs] = (f32(cur_ref, cs) * q[HALO:]).astype(BF16)

    return pl.pallas_call(
        body, grid=(t // rb,),
        in_specs=[_row_spec(rb, d3), _prev_spec16(rb, d3), _full_spec((3, d))], out_specs=_row_spec(rb, d),
        out_shape=jax.ShapeDtypeStruct((t, d), BF16), compiler_params=_params(("parallel",)), name=name,
    )(bcx, bcx, wconv)


def _a_mid_bwd(bcx, dy, wconv, *, name, rb=256, cw=512):
    t, d3 = bcx.shape
    d = d3 // 3
    rb, cw = _tile(t, rb, 16), _tile(d, cw)

    def body(cur_ref, prev_ref, next_ref, dy_ref, dyn_ref, w_ref, o_ref, dw_ref):
        i = pl.program_id(0)
        first, last = i == 0, i == pl.num_programs(0) - 1

        @pl.when(first)
        def _():
            dw_ref[...] = jnp.zeros_like(dw_ref)

        def f32(ref, cols):
            return ref[:, cols].astype(F32)

        for c0 in range(0, d, cw):
            cs = slice(c0, c0 + cw)
            gc, xs = slice(d + c0, d + c0 + cw), slice(2 * d + c0, 2 * d + c0 + cw)
            zeros = jnp.zeros((HALO, cw), F32)
            gb_c, gc_c, xs_c = f32(cur_ref, cs), f32(cur_ref, gc), f32(cur_ref, xs)
            p_prev = jnp.where(first, 0.0, (f32(prev_ref, gc) * f32(prev_ref, xs))[BF16_HALO - HALO:])
            e = jnp.concatenate([p_prev, gc_c * xs_c, zeros], axis=0)
            dq_next = jnp.where(last, 0.0, dyn_ref[:, cs] * f32(next_ref, cs)[:HALO])
            dy_c = dy_ref[:, cs]
            dq = jnp.concatenate([zeros, dy_c * gb_c, dq_next], axis=0)
            w = w_ref[:, cs]
            e1, e2 = _shift(e, 1), _shift(e, 2)
            q = w[0:1] * e2 + w[1:2] * e1 + w[2:3] * e
            dp = (w[2:3] * dq + w[1:2] * _shift(dq, -1) + w[0:1] * _shift(dq, -2))[HALO:HALO + rb]
            o_ref[:, cs] = (dy_c * q[HALO:HALO + rb]).astype(BF16)
            o_ref[:, gc] = (dp * xs_c).astype(BF16)
            o_ref[:, xs] = (dp * gc_c).astype(BF16)
            dq_c = dq[HALO:HALO + rb]
            dw_ref[0:1, cs] += jnp.sum(dq_c * e2[HALO:HALO + rb], axis=0, keepdims=True)
            dw_ref[1:2, cs] += jnp.sum(dq_c * e1[HALO:HALO + rb], axis=0, keepdims=True)
            dw_ref[2:3, cs] += jnp.sum(dq_c * e[HALO:HALO + rb], axis=0, keepdims=True)

    return pl.pallas_call(
        body, grid=(t // rb,),
        in_specs=[_row_spec(rb, d3), _prev_spec16(rb, d3), _next_spec16(rb, d3, t), _row_spec(rb, d), _next_spec(rb, d, t),
                  _full_spec((3, d))],
        out_specs=[_row_spec(rb, d3), _full_spec((8, d))],
        out_shape=[jax.ShapeDtypeStruct((t, d3), BF16), jax.ShapeDtypeStruct((8, d), F32)],
        compiler_params=_params(("arbitrary",)), name=name,
    )(bcx, bcx, bcx, dy, dy, wconv)


def _ffn_mid_fwd(up, wconv, bconv, *, name, rb=128, cw=512):
    t, f2 = up.shape
    f = f2 // 2
    rb, cw = _tile(t, rb, 16), _tile(f, cw)

    def body(cur_ref, prev_ref, w_ref, b_ref, act_ref, conv_ref):
        first = pl.program_id(0) == 0

        def conv(cols):
            prev = prev_ref[:, cols].astype(F32)[BF16_HALO - HALO:]
            e = jnp.concatenate([jnp.where(first, 0.0, prev), cur_ref[:, cols].astype(F32)], axis=0)
            w = w_ref[:, cols]
            out = (w[0:1] * _shift(e, 2) + w[1:2] * _shift(e, 1) + w[2:3] * e + b_ref[:, cols])[HALO:]
            conv_ref[:, cols] = out.astype(BF16)
            return out

        for c0 in range(0, f, cw):
            g = conv(slice(c0, c0 + cw))
            a = conv(slice(f + c0, f + c0 + cw))
            act_ref[:, c0:c0 + cw] = (g * _sigmoid(g) * a).astype(BF16)

    return pl.pallas_call(
        body, grid=(t // rb,),
        in_specs=[_row_spec(rb, f2), _prev_spec16(rb, f2), _full_spec((3, f2)), _full_spec((1, f2))],
        out_specs=[_row_spec(rb, f), _row_spec(rb, f2)],
        out_shape=[jax.ShapeDtypeStruct((t, f), BF16), jax.ShapeDtypeStruct((t, f2), BF16)],
        compiler_params=_params(("parallel",)), name=name,
    )(up, up, wconv, bconv)


def _ffn_mid_bwd(up, conv, dact, wconv, *, name, rb=128, cw=512):
    t, f2 = up.shape
    f = f2 // 2
    rb, cw = _tile(t, rb, 16), _tile(f, cw)

    def body(up_ref, conv_ref, convn_ref, da_ref, dan_ref, w_ref, o_ref, dwb_ref):
        i = pl.program_id(0)
        last = i == pl.num_programs(0) - 1

        @pl.when(i == 0)
        def _():
            dwb_ref[...] = jnp.zeros_like(dwb_ref)

        def rows(cols):
            return jnp.concatenate([conv_ref[:, cols].astype(F32), convn_ref[:, cols].astype(F32)[0:HALO]], axis=0)

        def back(dc, cols):
            w = w_ref[:, cols]
            dc1, dc2 = _shift(dc, -1)[:rb], _shift(dc, -2)[:rb]
            dc0 = dc[:rb]
            o_ref[:, cols] = (w[2:3] * dc0 + w[1:2] * dc1 + w[0:1] * dc2).astype(BF16)
            u = up_ref[:, cols].astype(F32)
            ones = jnp.ones((8, rb), BF16)
            for k, prod in enumerate((dc2 * u, dc1 * u, dc0 * u, dc0)):
                dwb_ref[k:k + 1, cols] += jnp.dot(ones, prod.astype(BF16), preferred_element_type=F32)[0:1]

        for c0 in range(0, f, cw):
            gcols, acols = slice(c0, c0 + cw), slice(f + c0, f + c0 + cw)
            g, a = rows(gcols), rows(acols)
            da = jnp.concatenate([da_ref[:, gcols], jnp.where(last, 0.0, dan_ref[:, gcols])], axis=0)
            sg = _sigmoid(g)
            back(da * a * (sg * (1.0 + g * (1.0 - sg))), gcols)
            back(da * (g * sg), acols)

    return pl.pallas_call(
        body, grid=(t // rb,),
        in_specs=[_row_spec(rb, f2), _row_spec(rb, f2), _next_spec16(rb, f2, t), _row_spec(rb, f), _next_spec(rb, f, t),
                  _full_spec((3, f2))],
        out_specs=[_row_spec(rb, f2), _full_spec((8, f2))],
        out_shape=[jax.ShapeDtypeStruct((t, f2), BF16), jax.ShapeDtypeStruct((8, f2), F32)],
        compiler_params=_params(("arbitrary",)), name=name,
    )(up, conv, conv, dact, dact, wconv)


def _causal_mask():
    return lax.broadcasted_iota(jnp.int32, (CHUNK, CHUNK), 0) >= lax.broadcasted_iota(jnp.int32, (CHUNK, CHUNK), 1)


def _b_mid_fwd(zp, vnorm, ws, bs, *, name, rb=512):
    t, d2 = zp.shape
    d = d2 // 2
    c = d // GROUPS
    rb = _tile(t, rb, CHUNK)

    def body(zp_ref, gv_ref, ws_ref, bs_ref, ug_ref, vn_ref, gate_ref):
        v = _gelu(zp_ref[:, d:].astype(F32))
        rv = lax.rsqrt(jnp.mean(v * v, axis=-1, keepdims=True) + RMS_EPS)
        vn_ref[...] = ((v * rv) * gv_ref[...]).astype(BF16)
        mask = _causal_mask()
        for h in range(GROUPS):
            hc = slice(h * c, (h + 1) * c)
            wm = jnp.where(mask, ws_ref[h], 0.0).astype(BF16)
            bcol = jnp.broadcast_to(bs_ref[h:h + 1, :], (CHUNK, CHUNK)).T[:, 0:1]
            for n in range(rb // CHUNK):
                rows = slice(n * CHUNK, (n + 1) * CHUNK)
                gate_ref[rows, hc] = jnp.dot(wm, vn_ref[rows, hc], preferred_element_type=F32) + bcol
        ug_ref[...] = (_gelu(zp_ref[:, :d].astype(F32)) * gate_ref[...]).astype(BF16)

    return pl.pallas_call(
        body, grid=(t // rb,),
        in_specs=[_row_spec(rb, d2), _full_spec((1, d)), _full_spec((GROUPS, CHUNK, CHUNK)), _full_spec((GROUPS, CHUNK))],
        out_specs=_row_spec(rb, d), out_shape=jax.ShapeDtypeStruct((t, d), BF16),
        scratch_shapes=[pltpu.VMEM((rb, d), BF16), pltpu.VMEM((rb, d), F32)],
        compiler_params=_params(("parallel",)), name=name,
    )(zp, vnorm, ws, bs)


def _b_mid_bwd(zp, dug, vnorm, ws, bs, *, name, rb=512):
    t, d2 = zp.shape
    d = d2 // 2
    c = d // GROUPS
    rb = _tile(t, rb, CHUNK)

    def body(zp_ref, dug_ref, gv_ref, ws_ref, bs_ref, dzp_ref, dws_ref, dbs_ref, dgv_ref,
             vn_ref, gate_ref, dm_ref, dvn_ref, dbacc_ref):
        i = pl.program_id(0)

        @pl.when(i == 0)
        def _():
            dws_ref[...] = jnp.zeros_like(dws_ref)
            dgv_ref[...] = jnp.zeros_like(dgv_ref)
            dbacc_ref[...] = jnp.zeros_like(dbacc_ref)

        zu, zv = zp_ref[:, :d].astype(F32), zp_ref[:, d:].astype(F32)
        u, v = _gelu(zu), _gelu(zv)
        rv = lax.rsqrt(jnp.mean(v * v, axis=-1, keepdims=True) + RMS_EPS)
        vhat = v * rv
        gv = gv_ref[...]
        vn_ref[...] = (vhat * gv).astype(BF16)
        dug_v = dug_ref[...].astype(F32)
        dm = dug_v * u
        dm_ref[...] = dm.astype(BF16)
        mask = _causal_mask()
        for h in range(GROUPS):
            hc = slice(h * c, (h + 1) * c)
            wm = jnp.where(mask, ws_ref[h], 0.0)
            wm_b, wmt_b = wm.astype(BF16), wm.T.astype(BF16)
            bcol = jnp.broadcast_to(bs_ref[h:h + 1, :], (CHUNK, CHUNK)).T[:, 0:1]
            dws_h = jnp.zeros((CHUNK, CHUNK), F32)
            dbs_h = jnp.zeros((CHUNK, c), F32)
            for n in range(rb // CHUNK):
                rows = slice(n * CHUNK, (n + 1) * CHUNK)
                vn_c, dm_c = vn_ref[rows, hc], dm_ref[rows, hc]
                gate_ref[rows, hc] = jnp.dot(wm_b, vn_c, preferred_element_type=F32) + bcol
                dws_h += lax.dot_general(dm_c, vn_c, (((1,), (1,)), ((), ())), preferred_element_type=F32)
                dvn_ref[rows, hc] = jnp.dot(wmt_b, dm_c, preferred_element_type=F32)
                dbs_h += dm[rows, hc]
            dws_ref[h] += dws_h
            dbacc_ref[h] += dbs_h
        du = dug_v * gate_ref[...]
        dvn = dvn_ref[...]
        dvhat = dvn * gv
        m = jnp.mean(dvhat * vhat, axis=-1, keepdims=True)
        dv = rv * (dvhat - vhat * m)
        dgv_ref[0:1, :] += jnp.sum(dvn * vhat, axis=0, keepdims=True)
        dzp_ref[:, :d] = (du * _gelu_grad(zu)).astype(BF16)
        dzp_ref[:, d:] = (dv * _gelu_grad(zv)).astype(BF16)

        @pl.when(i == pl.num_programs(0) - 1)
        def _():
            ones = jnp.ones((8, c), F32)
            for h in range(GROUPS):
                dws_ref[h] = jnp.where(mask, dws_ref[h], 0.0)
                row = lax.dot_general(ones, dbacc_ref[h], (((1,), (1,)), ((), ())),
                                      precision=lax.Precision.HIGHEST, preferred_element_type=F32)
                dbs_ref[h:h + 1, :] = row[0:1]

    return pl.pallas_call(
        body, grid=(t // rb,),
        in_specs=[_row_spec(rb, d2), _row_spec(rb, d), _full_spec((1, d)), _full_spec((GROUPS, CHUNK, CHUNK)),
                  _full_spec((GROUPS, CHUNK))],
        out_specs=[_row_spec(rb, d2), _full_spec((GROUPS, CHUNK, CHUNK)), _full_spec((GROUPS, CHUNK)), _full_spec((8, d))],
        out_shape=[jax.ShapeDtypeStruct((t, d2), BF16), jax.ShapeDtypeStruct((GROUPS, CHUNK, CHUNK), F32),
                   jax.ShapeDtypeStruct((GROUPS, CHUNK), F32), jax.ShapeDtypeStruct((8, d), F32)],
        scratch_shapes=[pltpu.VMEM((rb, d), BF16), pltpu.VMEM((rb, d), F32), pltpu.VMEM((rb, d), BF16),
                        pltpu.VMEM((rb, d), F32), pltpu.VMEM((GROUPS, CHUNK, c), F32)],
        compiler_params=_params(("arbitrary",)), name=name,
    )(zp, dug, vnorm, ws, bs)


def _cast_into_full(w, layer, kind, place, *, name, dep=None, rb=256):
    _, r, c = w.shape
    rb = _tile(r, rb, 16)
    nrb = r // rb
    full = (r, c * N_CHIPS) if kind == "col" else (r * N_CHIPS, c)

    def body(place_ref, w_ref, *rest):
        rest[-1][...] = w_ref[...].astype(BF16)

    def o_index(i, place):
        return (i, place[0]) if kind == "col" else (i + place[0] * nrb, 0)

    in_specs = [pl.BlockSpec((None, rb, c), lambda i, place: (layer, i, 0))] + ([ANY] if dep is not None else [])
    return pl.pallas_call(
        body,
        grid_spec=pltpu.PrefetchScalarGridSpec(num_scalar_prefetch=1, grid=(nrb,), in_specs=in_specs,
                                               out_specs=pl.BlockSpec((rb, c), o_index)),
        out_shape=jax.ShapeDtypeStruct(full, BF16), compiler_params=_params(("parallel",)), name=name,
    )(place, w, *(() if dep is None else (dep,)))


def _adamw_layer(w, g, m, v, layer, prev, *, name, rb=128):
    _, r, c = w.shape
    rb = _tile(r, rb, 8)
    c1 = 1.0 - ADAM_B1 ** ADAM_STEP
    c2 = 1.0 - ADAM_B2 ** ADAM_STEP

    def body(w_ref, g_ref, m_ref, v_ref, *rest):
        go_ref, d_ref, nm_ref, nv_ref = rest[-4:]
        gv = g_ref[...]
        nm = ADAM_B1 * m_ref[...] + (1.0 - ADAM_B1) * gv
        nv = ADAM_B2 * v_ref[...] + (1.0 - ADAM_B2) * (gv * gv)
        go_ref[...] = gv
        nm_ref[...] = nm
        nv_ref[...] = nv
        d_ref[...] = -ADAM_LR * ((nm / c1) / (jnp.sqrt(nv / c2) + ADAM_EPS) + ADAM_WD * w_ref[...])

    lay = pl.BlockSpec((None, rb, c), lambda i: (layer, i, 0))
    return pl.pallas_call(
        body, grid=(r // rb,), in_specs=[lay, _row_spec(rb, c), lay, lay] + ([ANY] * 4 if prev else []), out_specs=[lay] * 4,
        out_shape=[jax.ShapeDtypeStruct(w.shape, F32)] * 4, input_output_aliases={4 + k: k for k in range(4)} if prev else {},
        compiler_params=_params(("parallel",)), name=name,
    )(w, g, m, v, *(prev or ()))


HBM = pl.BlockSpec(memory_space=pltpu.HBM)
SEM = pl.BlockSpec(memory_space=pltpu.SEMAPHORE)
SIDE_EFFECT = pltpu.SideEffectType.DATAFLOW_SIDE_EFFECTING


def _place():
    x, y, c = lax.axis_index("x"), lax.axis_index("y"), lax.axis_index("c")
    chips = [(1 - x, y), (x, 1 - y), (1 - x, 1 - y)]
    return x, y, c, 2 * x + y, chips


def _half(ref, kind, c):
    r, w = ref.shape
    if kind == "col":
        return ref.at[pl.ds(pl.multiple_of(c * (r // 2), 8), r // 2), :]
    return ref.at[:, pl.ds(pl.multiple_of(c * (w // 2), 128), w // 2)]


def _shard(ref, kind, s):
    r, w = ref.shape
    if kind == "col":
        return ref.at[:, pl.ds(pl.multiple_of(s * (w // N_CHIPS), 128), w // N_CHIPS)]
    return ref.at[pl.ds(pl.multiple_of(s * (r // N_CHIPS), 8), r // N_CHIPS), :]


def _remote(src, dst, send_sem, recv_sem, dev):
    return pltpu.make_async_remote_copy(src_ref=src, dst_ref=dst, send_sem=send_sem, recv_sem=recv_sem,
                                        device_id=dev, device_id_type=MESH)


def _start(name, bufs, plan, sem_shape, dep=None):
    n = len(bufs)
    n_in = n + (dep is not None)

    def body(*refs):
        sends, _ = plan(refs[:n], refs[n_in], refs[n_in + 1])
        for cp in sends:
            cp.start()
        refs[n_in + 2 + n][...] = jnp.zeros((8, 128), F32)

    dma = pltpu.SemaphoreType.DMA
    outs = pl.pallas_call(
        body, name=name,
        out_shape=(dma(sem_shape), dma(sem_shape), *[pltpu.HBM(b.shape, b.dtype) for b in bufs], jax.ShapeDtypeStruct((8, 128), F32)),
        in_specs=(HBM,) * n + ((ANY,) if dep is not None else ()),
        out_specs=(SEM, SEM) + (HBM,) * n + (pl.BlockSpec(memory_space=pltpu.VMEM),),
        input_output_aliases={i: i + 2 for i in range(n)},
        compiler_params=pltpu.CompilerParams(has_side_effects=SIDE_EFFECT),
    )(*[pltpu.with_memory_space_constraint(b, pltpu.HBM) for b in bufs], *(() if dep is None else (dep,)))
    return outs[0], outs[1], list(outs[2:2 + n]), outs[2 + n]


def _wait(name, started, plan, after):
    send, recv, bufs, _ = started
    n = len(bufs)

    def body(*refs):
        sends, recvs = plan(refs[:n], refs[n], refs[n + 1])
        for cp in sends:
            cp.wait_send()
        for cp in recvs:
            cp.wait_recv()

    return list(pl.pallas_call(
        body, name=name, out_shape=tuple(pltpu.HBM(b.shape, b.dtype) for b in bufs),
        in_specs=(HBM,) * n + (SEM, SEM, ANY), out_specs=(HBM,) * n, input_output_aliases={i: i for i in range(n)},
        compiler_params=pltpu.CompilerParams(has_side_effects=SIDE_EFFECT),
    )(*bufs, send, recv, after))


KINDS = ("col", "row")


def _gather_ici_plan(n_small):
    def plan(refs, send, recv):
        x, y, c, s, chips = _place()
        n = len(KINDS) + n_small
        sends, recvs = [], []
        for k, (px, py) in enumerate(chips):
            sp = 2 * px + py
            for a, kind in enumerate(KINDS):
                mine, theirs = _half(_shard(refs[a], kind, s), kind, c), _half(_shard(refs[a], kind, sp), kind, c)
                sends.append(_remote(mine, mine, send.at[k * n + a], recv.at[k * n + a], (px, py, c)))
                recvs.append(_remote(theirs, theirs, send.at[k * n + a], recv.at[k * n + a], (px, py, c)))
            for b in range(n_small):
                ref, sem = refs[len(KINDS) + b], k * n + len(KINDS) + b
                sends.append(_remote(ref.at[s], ref.at[s], send.at[sem], recv.at[sem], (px, py, c)))
                recvs.append(_remote(ref.at[sp], ref.at[sp], send.at[sem], recv.at[sem], (px, py, c)))
        return sends, recvs
    return plan


def _gather_d2d_plan(refs, send, recv):
    x, y, c, _, chips = _place()
    n = len(KINDS)
    sends, recvs = [], []
    for k, (px, py) in enumerate(chips):
        for a, kind in enumerate(KINDS):
            region, sem = _shard(refs[a], kind, 2 * px + py), k * n + a
            sends.append(_remote(_half(region, kind, c), _half(region, kind, c), send.at[sem], recv.at[sem], (x, y, 1 - c)))
            recvs.append(_remote(_half(region, kind, 1 - c), _half(region, kind, 1 - c), send.at[sem], recv.at[sem], (x, y, 1 - c)))
    return sends, recvs


def _swap_plan(refs, send, recv):
    x, y, c, _, _ = _place()
    n = len(KINDS)
    cps = [_remote(_half(refs[a], KINDS[a], 1 - c), refs[n + a], send.at[a], recv.at[a], (x, y, 1 - c)) for a in range(n)]
    return cps, cps


def _exchange_plan(refs, send, recv):
    x, y, c, _, chips = _place()
    n = len(KINDS)
    cps = []
    for k, (px, py) in enumerate(chips):
        for a in range(n):
            cps.append(_remote(_shard(refs[a], KINDS[a], 2 * px + py), refs[n + a].at[k], send.at[k * n + a], recv.at[k * n + a],
                               (px, py, c)))
    return cps, cps


def _share_plan(refs, send, recv):
    x, y, c, _, _ = _place()
    sends = [_remote(_half(refs[a], KINDS[a], c), _half(refs[a], KINDS[a], c), send.at[a], recv.at[a], (x, y, 1 - c))
             for a in range(len(KINDS))]
    recvs = [_remote(_half(refs[a], KINDS[a], 1 - c), _half(refs[a], KINDS[a], 1 - c), send.at[a], recv.at[a], (x, y, 1 - c))
             for a in range(len(KINDS))]
    return sends, recvs


def _spread_plan(refs, send, recv):
    packed, slots = refs
    x, y, c, _, _ = _place()
    sends, recvs = [], []
    for k in range(1, 8):
        px, py, pc = x ^ (k >> 2), y ^ ((k >> 1) & 1), c ^ (k & 1)
        sends.append(_remote(packed, slots.at[4 * x + 2 * y + c], send.at[k - 1], recv.at[k - 1], (px, py, pc)))
        recvs.append(_remote(packed, slots.at[4 * px + 2 * py + pc], send.at[k - 1], recv.at[k - 1], (px, py, pc)))
    return sends, recvs


def _half_index(kind, nblk):
    def index(i, j, place):
        return (i + place[1] * nblk[0], j) if kind == "col" else (i, j + place[1] * nblk[1])
    return index


def _chip_partial(g, other, kind, place, *, name):
    hr, hc = other.shape
    rb, cb = _tile(hr, 512, 16), _tile(hc, 1024)
    nblk = (hr // rb, hc // cb)

    def body(place_ref, g_ref, o_ref, p_ref):
        p_ref[...] = (g_ref[...].astype(F32) + o_ref[...].astype(F32)).astype(BF16)

    plain = pl.BlockSpec((rb, cb), lambda i, j, place: (i, j))
    return pl.pallas_call(
        body,
        grid_spec=pltpu.PrefetchScalarGridSpec(
            num_scalar_prefetch=1, grid=nblk, in_specs=[pl.BlockSpec((rb, cb), _half_index(kind, nblk)), plain], out_specs=plain),
        out_shape=jax.ShapeDtypeStruct((hr, hc), BF16), compiler_params=_params(("parallel", "parallel")), name=name,
    )(place, g, other)


def _reduce_half(g, other, recv, kind, place, *, name):
    _, pr, pc = recv.shape
    rb, cb = _tile(pr, 512, 16), _tile(pc, 1024)
    nblk = (pr // rb, pc // cb)
    full = (pr * 2, pc) if kind == "col" else (pr, pc * 2)

    def g_index(i, j, place):
        s, c = place[0], place[1]
        return (i + c * nblk[0], j + s * nblk[1]) if kind == "col" else (i + s * nblk[0], j + c * nblk[1])

    def o_index(i, j, place):
        return (i, j + place[0] * nblk[1]) if kind == "col" else (i + place[0] * nblk[0], j)

    def body(place_ref, g_ref, o_ref, r_ref, out_ref):
        acc = g_ref[...].astype(F32) + o_ref[...].astype(F32)
        for k in range(3):
            acc = acc + r_ref[k].astype(F32)
        out_ref[...] = acc

    return pl.pallas_call(
        body,
        grid_spec=pltpu.PrefetchScalarGridSpec(
            num_scalar_prefetch=1, grid=nblk,
            in_specs=[pl.BlockSpec((rb, cb), g_index), pl.BlockSpec((rb, cb), o_index),
                      pl.BlockSpec((3, rb, cb), lambda i, j, place: (0, i, j))],
            out_specs=pl.BlockSpec((rb, cb), _half_index(kind, nblk))),
        out_shape=jax.ShapeDtypeStruct(full, F32), compiler_params=_params(("parallel", "parallel")), name=name,
    )(place, g, other, recv)


def _sum_slots(packed, slots, me, *, name, rb=512):
    r, w = packed.shape
    rb = _tile(r, rb, 8)

    def body(me_ref, p_ref, s_ref, o_ref):
        acc = None
        for j in range(8):
            term = jnp.where(me_ref[0] == j, p_ref[...], s_ref[j])
            acc = term if acc is None else acc + term
        o_ref[...] = acc

    return pl.pallas_call(
        body,
        grid_spec=pltpu.PrefetchScalarGridSpec(
            num_scalar_prefetch=1, grid=(r // rb,),
            in_specs=[pl.BlockSpec((rb, w), lambda i, me: (i, 0)), pl.BlockSpec((8, rb, w), lambda i, me: (0, i, 0))],
            out_specs=pl.BlockSpec((rb, w), lambda i, me: (i, 0))),
        out_shape=jax.ShapeDtypeStruct((r, w), F32), compiler_params=_params(("parallel",)), name=name,
    )(me, packed, slots)


def _half_shape(a, kind):
    return (a.shape[0] // 2, a.shape[1]) if kind == "col" else (a.shape[0], a.shape[1] // 2)


def _rs_swap(tag, grads):
    others = [lax.empty(_half_shape(g, k), g.dtype) for g, k in zip(grads, KINDS)]
    return _start(f"rs_{tag}_swap", list(grads) + others, _swap_plan, (len(KINDS),))


def _rs_exchange(tag, swapped, place, after):
    bufs = _wait(f"rs_{tag}_swap_wait", swapped, _swap_plan, after)
    n = len(KINDS)
    grads, others = bufs[:n], bufs[n:]
    parts = [_chip_partial(g, o, k, place, name=f"rs_{tag}_partial_{k}") for g, o, k in zip(grads, others, KINDS)]
    lands = []
    for p, k in zip(parts, KINDS):
        piece = (p.shape[0], p.shape[1] // N_CHIPS) if k == "col" else (p.shape[0] // N_CHIPS, p.shape[1])
        lands.append(lax.empty((3,) + piece, p.dtype))
    return _start(f"rs_{tag}_exchange", parts + lands, _exchange_plan, (3 * n,)), grads, others


def _rs_share(tag, exchanged, place, after):
    started, grads, others = exchanged
    n = len(KINDS)
    recvs = _wait(f"rs_{tag}_exchange_wait", started, _exchange_plan, after)[n:]
    halves = [_reduce_half(g, o, r, k, place, name=f"rs_{tag}_reduce_{k}") for g, o, r, k in zip(grads, others, recvs, KINDS)]
    return _start(f"rs_{tag}_share", halves, _share_plan, (n,))


def _rs_finish(tag, shared, after):
    return _wait(f"rs_{tag}_share_wait", shared, _share_plan, after)


def _spread(tag, parts, dep):
    rows = [p.reshape(-1, 128) for p in parts]
    n = sum(r.shape[0] for r in rows)
    rows.append(jnp.zeros(((-n) % 512, 128), F32))
    packed = jnp.concatenate(rows, axis=0)
    return _start(f"small_{tag}_spread", [packed, lax.empty((8,) + packed.shape, F32)], _spread_plan, (7,), dep=dep)


def _spread_sum(tag, started, parts, me, after):
    packed, slots = _wait(f"small_{tag}_spread_wait", started, _spread_plan, after)
    total = _sum_slots(packed, slots, me, name=f"small_{tag}_sum")
    out, row = [], 0
    for p in parts:
        n = p.size // 128
        out.append(total[row:row + n].reshape(p.shape))
        row += n
    return out


def _ffn_fwd(x, h, w_up, conv_w, conv_b, w_down, tag, tm_up=1024):
    up = _mm(h, w_up, tm=tm_up, out_dtype=BF16, name=f"ffn{tag}_up")
    act, conv = _ffn_mid_fwd(up, conv_w, conv_b, name=f"ffn{tag}_mid")
    x_out = _mm(act, w_down, res=x, tk=2816, name=f"ffn{tag}_down")
    return x_out, (up, conv, act)


def kernel(x, a_norm, a_in, a_conv, a_out, b_norm, b_in, b_vnorm, b_ws, b_bs, b_out, f_norm, f_up, f_conv_w, f_conv_b, f_down, final_norm, loss_target, m_a_norm, m_a_in, m_a_conv, m_a_out, m_b_norm, m_b_in, m_b_vnorm, m_b_ws, m_b_bs, m_b_out, m_f_norm, m_f_up, m_f_conv_w, m_f_conv_b, m_f_down, m_final_norm, v_a_norm, v_a_in, v_a_conv, v_a_out, v_b_norm, v_b_in, v_b_vnorm, v_b_ws, v_b_bs, v_b_out, v_f_norm, v_f_up, v_f_conv_w, v_f_conv_b, v_f_down, v_final_norm):
    t, d = x.shape[1], x.shape[2]
    f2 = f_up.shape[2] * N_CHIPS
    x0, tgt = x.reshape(t, d), loss_target.reshape(t, d)
    ax, ay, ac = lax.axis_index("x"), lax.axis_index("y"), lax.axis_index("c")
    s = 2 * ax + ay
    place = jnp.stack([s, ac]).astype(jnp.int32)
    me = (4 * ax + 2 * ay + ac).astype(jnp.int32).reshape(1)

    def stacked(a):
        return lax.dynamic_update_index_in_dim(jnp.zeros((N_CHIPS,) + a.shape, F32), a, s, 0)

    def cast_pair(tag, w_in, w_out, layer, dep):
        return [_cast_into_full(w_in, layer, "col", place, name=f"cast_{tag}_in", dep=dep),
                _cast_into_full(w_out, layer, "row", place, name=f"cast_{tag}_out", dep=dep)]

    def gather_start(tag, fulls, small, dep):
        return _start(f"ag_{tag}_ici", fulls + small, _gather_ici_plan(len(small)), (3 * (2 + len(small)),), dep=dep)

    def gather_forward(tag, started, n_small, after):
        bufs = _wait(f"ag_{tag}_ici_wait", started, _gather_ici_plan(n_small), after)
        return _start(f"ag_{tag}_d2d", bufs[:2], _gather_d2d_plan, (3 * 2,)), bufs[2:]

    def gather_finish(tag, forwarded, after):
        return _wait(f"ag_{tag}_d2d_wait", forwarded, _gather_d2d_plan, after)

    small = [stacked(a_conv[0]), stacked(b_norm), stacked(b_vnorm), stacked(f_conv_w.reshape(2 * 3, -1))]
    ag_a = gather_start("a", cast_pair("a", a_in, a_out, 0, None), small, None)
    full_f0 = cast_pair("f0", f_up, f_down, 0, ag_a[3])
    full_b = cast_pair("b", b_in, b_out, 0, full_f0[1])
    full_f1 = cast_pair("f1", f_up, f_down, 1, full_b[1])

    def unshard(a):
        return jnp.transpose(a, (1, 0, 2)).reshape(a.shape[1], -1)

    ws, bs = b_ws[0], b_bs[0]

    h0 = _rms_fwd(x0, a_norm, dep=full_f1[1], name="a_norm")
    fw_a, (g_aconv, g_bnorm, g_bvnorm, g_fconv) = gather_forward("a", ag_a, 4, h0)
    ag_f0 = gather_start("f0", full_f0, [], fw_a[3])
    ag_b = gather_start("b", full_b, [], ag_f0[3])
    ag_f1 = gather_start("f1", full_f1, [], ag_b[3])
    w_ai, w_ao = gather_finish("a", fw_a, ag_f1[3])
    a_conv_f, b_norm_f, b_vnorm_f = unshard(g_aconv), unshard(g_bnorm), unshard(g_bvnorm)
    f_conv_f = unshard(g_fconv).reshape(2, 3, f2)
    bcx = _mm(h0, w_ai, tm=2048, out_dtype=BF16, name="a_in")
    y = _a_mid_fwd(bcx, a_conv_f, name="a_mid")
    x1 = _mm(y, w_ao, res=x0, tm=512, tn=2048, name="a_out")
    fw_f0, _ = gather_forward("f0", ag_f0, 0, x1)
    h1 = _rms_fwd(x1, f_norm[0:1], dep=fw_f0[3], name="ffn0_norm")
    w_up0, w_dn0 = gather_finish("f0", fw_f0, h1)
    x2, (up0, conv0, act0) = _ffn_fwd(x1, h1, w_up0, f_conv_f[0], f_conv_b[0:1], w_dn0, 0, tm_up=2048)
    fw_b, _ = gather_forward("b", ag_b, 0, up0)
    w_bi, w_bo = gather_finish("b", fw_b, act0)
    h2 = _rms_fwd(x2, b_norm_f, name="b_norm")
    zp = _mm(h2, w_bi, tm=2048, out_dtype=BF16, name="b_in")
    fw_f1, _ = gather_forward("f1", ag_f1, 0, zp)
    ug = _b_mid_fwd(zp, b_vnorm_f, ws, bs, name="b_mid")
    x3, h3 = _mm(ug, w_bo, res=x2, norm=f_norm[1:2], tm=512, tn=2048, name="b_out")
    w_up1, w_dn1 = gather_finish("f1", fw_f1, x3)
    x4, (up1, conv1, act1) = _ffn_fwd(x3, h3, w_up1, f_conv_f[1], f_conv_b[1:2], w_dn1, 1, tm_up=2048)
    loss_rows, dx4, dx4b, d_final = _final(x4, tgt, final_norm.reshape(1, d), name="final")

    d_dn1 = _mm(act1, dx4b, ta=True, tm=1408, out_dtype=BF16, name="ffn1_ddown")
    dact1 = _mm(dx4b, w_dn1, tb=True, tn=512, tm=2048, name="ffn1_dact")
    dup1, d_fwb1 = _ffn_mid_bwd(up1, conv1, dact1, f_conv_f[1], name="ffn1_mid_bwd")
    d_up1 = _mm(h3, dup1, ta=True, out_dtype=BF16, tk=4096, name="ffn1_dup")
    sw_f1 = _rs_swap("f1", [d_up1, d_dn1])
    dh3 = _mm(dup1, w_up1, tb=True, tk=2816, out_dtype=BF16, dep=sw_f1[3], name="ffn1_dh")
    dx3, dx3b, d_fnorm1 = _rms_bwd(dh3, x3, f_norm[1:2], dx4, name="ffn1_norm_bwd")
    ex_f1 = _rs_exchange("f1", sw_f1, place, dx3)

    d_bo = _mm(ug, dx3b, ta=True, out_dtype=BF16, tk=4096, dep=ex_f1[0][3], name="b_dout")
    dug = _mm(dx3b, w_bo, tb=True, tm=512, tn=2048, out_dtype=BF16, name="b_dug")
    dzp, d_ws, d_bs, d_bvnorm = _b_mid_bwd(zp, dug, b_vnorm_f, ws, bs, name="b_mid_bwd")
    d_bi = _mm(h2, dzp, ta=True, out_dtype=BF16, tk=4096, name="b_din")
    sw_b = _rs_swap("b", [d_bi, d_bo])
    dh2 = _mm(dzp, w_bi, tb=True, tk=4096, out_dtype=BF16, dep=sw_b[3], name="b_dh")
    dx2, dx2b, d_bnorm = _rms_bwd(dh2, x2, b_norm_f, dx3, name="b_norm_bwd")
    ex_b = _rs_exchange("b", sw_b, place, dx2)

    d_dn0 = _mm(act0, dx2b, ta=True, tm=1408, out_dtype=BF16, dep=ex_b[0][3], name="ffn0_ddown")
    dact0 = _mm(dx2b, w_dn0, tb=True, tn=512, tm=2048, name="ffn0_dact")
    dup0, d_fwb0 = _ffn_mid_bwd(up0, conv0, dact0, f_conv_f[0], name="ffn0_mid_bwd")
    sh_f1 = _rs_share("f1", ex_f1, place, dup0)
    d_up0 = _mm(h1, dup0, ta=True, out_dtype=BF16, tk=4096, dep=sh_f1[3], name="ffn0_dup")
    sw_f0 = _rs_swap("f0", [d_up0, d_dn0])
    g_up1, g_dn1 = _rs_finish("f1", sh_f1, sw_f0[3])
    dh1 = _mm(dup0, w_up0, tb=True, tk=2816, out_dtype=BF16, dep=sw_f0[3], name="ffn0_dh")
    dx1, dx1b, d_fnorm0 = _rms_bwd(dh1, x1, f_norm[0:1], dx2, name="ffn0_norm_bwd")
    ex_f0 = _rs_exchange("f0", sw_f0, place, dx1)
    early = [jnp.concatenate([d_bnorm, d_bvnorm, d_fnorm0, d_fnorm1, d_final, loss_rows], axis=0),
             jnp.concatenate([d_fwb0, d_fwb1], axis=0), jnp.concatenate([d_ws.reshape(-1, CHUNK), d_bs], axis=0)]
    sp_early = _spread("early", early, ex_f0[0][3])
    sh_b = _rs_share("b", ex_b, place, sp_early[3])

    d_ao = _mm(y, dx1b, ta=True, out_dtype=BF16, tk=4096, dep=sh_b[3], name="a_dout")
    dyy = _mm(dx1b, w_ao, tb=True, tm=512, tn=2048, name="a_dy")
    dbcx, d_aconv = _a_mid_bwd(bcx, dyy, a_conv_f, name="a_mid_bwd")
    d_ai = _mm(h0, dbcx, ta=True, out_dtype=BF16, tk=4096, name="a_din")
    sw_a = _rs_swap("a", [d_ai, d_ao])
    g_bi, g_bo = _rs_finish("b", sh_b, sw_a[3])
    early_adamw = {"b_in": _adamw_layer(b_in, g_bi, m_b_in, v_b_in, 0, None, name="adamw_b_in"),
                   "b_out": _adamw_layer(b_out, g_bo, m_b_out, v_b_out, 0, None, name="adamw_b_out")}
    ex_a = _rs_exchange("a", sw_a, place, early_adamw["b_in"][1][0, :8, :128] + early_adamw["b_out"][1][0, :8, :128])
    dh0 = _mm(dbcx, w_ai, tb=True, tk=3072, out_dtype=BF16, dep=ex_a[0][3], name="a_dh")
    grad_x, _, d_anorm = _rms_bwd(dh0, x0, a_norm, dx1, name="a_norm_bwd")
    late = [jnp.concatenate([d_anorm, d_aconv], axis=0)]
    sp_late = _spread("late", late, ex_a[0][3])
    sh_f0 = _rs_share("f0", ex_f0, place, sp_late[3])
    sh_a = _rs_share("a", ex_a, place, sh_f0[3])
    g_up0, g_dn0 = _rs_finish("f0", sh_f0, sh_a[3])
    g_ai, g_ao = _rs_finish("a", sh_a, g_up0)
    r_a, r_b, r_c = _spread_sum("early", sp_early, early, me, g_ai)
    (r_l,) = _spread_sum("late", sp_late, late, me, r_a)

    loss = jnp.sum(r_a[40])
    cs, fs = d // N_CHIPS, f2 // N_CHIPS

    def mine(a, width):
        return lax.dynamic_slice_in_dim(a, s * width, width, axis=1)

    grads = {
        "a_norm": r_l[0:1], "a_conv": mine(r_l[8:11], cs), "b_norm": mine(r_a[0:1], cs), "b_vnorm": mine(r_a[8:9], cs),
        "f_norm": jnp.concatenate([r_a[16:17], r_a[24:25]], axis=0), "final_norm": r_a[32:33],
        "b_ws": r_c[:GROUPS * CHUNK], "b_bs": r_c[GROUPS * CHUNK:],
        "f_conv_w": jnp.concatenate([mine(r_b[0:3], fs), mine(r_b[8:11], fs)], axis=0),
        "f_conv_b": jnp.concatenate([r_b[3:4], r_b[11:12]], axis=0),
        "a_in": g_ai, "a_out": g_ao, "b_in": g_bi, "b_out": g_bo,
    }
    names = ["a_norm", "a_in", "a_conv", "a_out", "b_norm", "b_in", "b_vnorm", "b_ws", "b_bs", "b_out", "f_norm", "f_up",
             "f_conv_w", "f_conv_b", "f_down", "final_norm"]
    weights = dict(zip(names, [a_norm, a_in, a_conv, a_out, b_norm, b_in, b_vnorm, b_ws, b_bs, b_out, f_norm, f_up, f_conv_w,
                               f_conv_b, f_down, final_norm]))
    ms = dict(zip(names, [m_a_norm, m_a_in, m_a_conv, m_a_out, m_b_norm, m_b_in, m_b_vnorm, m_b_ws, m_b_bs, m_b_out, m_f_norm,
                          m_f_up, m_f_conv_w, m_f_conv_b, m_f_down, m_final_norm]))
    vs = dict(zip(names, [v_a_norm, v_a_in, v_a_conv, v_a_out, v_b_norm, v_b_in, v_b_vnorm, v_b_ws, v_b_bs, v_b_out, v_f_norm,
                          v_f_up, v_f_conv_w, v_f_conv_b, v_f_down, v_final_norm]))
    result = {}
    for n in names:
        w = weights[n]
        if n in ("f_up", "f_down"):
            g1, g0 = (g_up1, g_up0) if n == "f_up" else (g_dn1, g_dn0)
            first = _adamw_layer(w, g1, ms[n], vs[n], 1, None, name=f"adamw_{n}1")
            result[n] = _adamw_layer(w, g0, ms[n], vs[n], 0, tuple(first), name=f"adamw_{n}0")
            continue
        if n in early_adamw:
            result[n] = early_adamw[n]
            continue
        g2 = grads[n]
        as3d = (lambda a: a.reshape((1,) + g2.shape))
        result[n] = [o.reshape(w.shape) for o in _adamw_layer(as3d(w), g2, as3d(ms[n]), as3d(vs[n]), 0, None, name=f"adamw_{n}")]

    return (loss, grad_x.reshape(x.shape), *[result[n][0] for n in names], *[result[n][1] for n in names],
            *[result[n][2] for n in names], *[result[n][3] for n in names])
```

```python
import jax
import jax.numpy as jnp
from jax import lax
from jax.experimental import pallas as pl
from jax.experimental.pallas import tpu as pltpu

F32 = jnp.float32
BF16 = jnp.bfloat16
MESH = pl.DeviceIdType.MESH
ANY = pl.BlockSpec(memory_space=pl.ANY)

RMS_EPS = 1e-5
CHUNK = 128
GROUPS = 8
ADAM_LR, ADAM_B1, ADAM_B2, ADAM_EPS, ADAM_WD, ADAM_STEP = 0.001, 0.9, 0.999, 1e-08, 0.01, 10

N_CHIPS = 4
HALO = 8
BF16_HALO = 16
VMEM_LIMIT = 56 * 1024 * 1024
GELU_C = 0.7978845608028654
GELU_A = 0.044715


def _params(sem=None):
    return pltpu.CompilerParams(dimension_semantics=sem, vmem_limit_bytes=VMEM_LIMIT)


def _tile(dim, pref, quantum=128):
    if dim <= pref:
        return dim
    t = (pref // quantum) * quantum
    while t >= quantum:
        if dim % t == 0:
            return t
        t -= quantum
    return dim


def _mm(a, b, *, name, ta=False, tb=False, res=None, norm=None, dep=None, out_dtype=F32, tm=1024, tn=1024, tk=2048):
    (K, M) = a.shape if ta else a.shape[::-1]
    N = b.shape[0] if tb else b.shape[1]
    assert (b.shape[1] if tb else b.shape[0]) == K
    tm, tn, tk = _tile(M, tm), _tile(N, tn), _tile(K, tk)
    nk = K // tk
    assert norm is None or (tn == N and nk == 1)
    a_spec = pl.BlockSpec((tk, tm), lambda i, j, k: (k, i)) if ta else pl.BlockSpec((tm, tk), lambda i, j, k: (i, k))
    b_spec = pl.BlockSpec((tn, tk), lambda i, j, k: (j, k)) if tb else pl.BlockSpec((tk, tn), lambda i, j, k: (k, j))
    o_spec = pl.BlockSpec((tm, tn), lambda i, j, k: (i, j))
    dims = (((0 if ta else 1,), (1 if tb else 0,)), ((), ()))
    direct = out_dtype == F32
    n_in = 2 + (res is not None) + (norm is not None) + (dep is not None)

    def body(*refs):
        a_ref, b_ref = refs[0], refs[1]
        r_ref = refs[2] if res is not None else None
        g_ref = refs[2 + (res is not None)] if norm is not None else None
        o_ref = refs[n_in]
        acc_ref = o_ref if direct else refs[-1]
        part = lax.dot_general(a_ref[...], b_ref[...], dims, preferred_element_type=F32)
        if nk == 1:
            if r_ref is not None:
                part = part + r_ref[...]
            o_ref[...] = part.astype(o_ref.dtype)
            if g_ref is not None:
                r = lax.rsqrt(jnp.mean(part * part, axis=-1, keepdims=True) + RMS_EPS)
                refs[n_in + 1][...] = ((part * r) * g_ref[...]).astype(BF16)
            return
        k = pl.program_id(2)

        @pl.when(k == 0)
        def _():
            acc_ref[...] = part

        @pl.when(jnp.logical_and(k > 0, k < nk - 1))
        def _():
            acc_ref[...] += part

        @pl.when(k == nk - 1)
        def _():
            tot = acc_ref[...] + part
            if r_ref is not None:
                tot = tot + r_ref[...]
            o_ref[...] = tot.astype(o_ref.dtype)

    in_specs = ([a_spec, b_spec] + ([o_spec] if res is not None else [])
                + ([pl.BlockSpec((1, tn), lambda i, j, k: (0, j))] if norm is not None else []) + ([ANY] if dep is not None else []))
    args = (a, b) + tuple(x for x in (res, norm, dep) if x is not None)
    scratch = [] if (direct or nk == 1) else [pltpu.VMEM((tm, tn), F32)]
    out_shape = jax.ShapeDtypeStruct((M, N), out_dtype)
    return pl.pallas_call(
        body, grid=(M // tm, N // tn, nk), in_specs=in_specs, out_specs=[o_spec, o_spec] if norm is not None else o_spec,
        out_shape=[out_shape, jax.ShapeDtypeStruct((M, N), BF16)] if norm is not None else out_shape, scratch_shapes=scratch,
        compiler_params=_params(("parallel", "parallel", "arbitrary")), name=name,
    )(*args)


def _row_spec(rb, w):
    return pl.BlockSpec((rb, w), lambda i: (i, 0))


def _next_spec(rb, w, t):
    return pl.BlockSpec((HALO, w), lambda i: (jnp.minimum((i + 1) * (rb // HALO), t // HALO - 1), 0))


def _prev_spec16(rb, w):
    return pl.BlockSpec((BF16_HALO, w), lambda i: (jnp.maximum(i * (rb // BF16_HALO) - 1, 0), 0))


def _next_spec16(rb, w, t):
    return pl.BlockSpec((BF16_HALO, w), lambda i: (jnp.minimum((i + 1) * (rb // BF16_HALO), t // BF16_HALO - 1), 0))


def _full_spec(shape):
    return pl.BlockSpec(shape, lambda i: tuple(0 for _ in shape))


def _shift(e, s):
    return pltpu.roll(e, s % e.shape[0], 0)


def _gelu(x):
    return 0.5 * x * (1.0 + jnp.tanh(GELU_C * (x + GELU_A * x * x * x)))


def _gelu_and_grad(x):
    x2 = x * x
    th = jnp.tanh(GELU_C * (x + GELU_A * x2 * x))
    half = 0.5 * (1.0 + th)
    return x * half, half + 0.5 * x * (1.0 - th * th) * (GELU_C * (1.0 + 3.0 * GELU_A * x2))


def _sigmoid(x):
    return 1.0 / (1.0 + jnp.exp(-x))


def _rms_fwd(x, g, *, name, dep=None, rb=256):
    t, d = x.shape
    rb = _tile(t, rb, 8)

    def body(x_ref, g_ref, *rest):
        h_ref = rest[-1]
        xv = x_ref[...]
        r = lax.rsqrt(jnp.mean(xv * xv, axis=-1, keepdims=True) + RMS_EPS)
        h_ref[...] = ((xv * r) * g_ref[...]).astype(BF16)

    return pl.pallas_call(
        body, grid=(t // rb,), in_specs=[_row_spec(rb, d), _full_spec((1, d))] + ([ANY] if dep is not None else []),
        out_specs=_row_spec(rb, d), out_shape=jax.ShapeDtypeStruct((t, d), BF16), compiler_params=_params(("parallel",)), name=name,
    )(x, g, *(() if dep is None else (dep,)))


def _rms_bwd(dh, x, g, dres, *, name, rb=256):
    t, d = x.shape
    rb = _tile(t, rb, 16)

    def body(dh_ref, x_ref, g_ref, dres_ref, dx_ref, dxb_ref, dg_ref):
        xv = x_ref[...]
        r = lax.rsqrt(jnp.mean(xv * xv, axis=-1, keepdims=True) + RMS_EPS)
        xhat = xv * r
        dh_v = dh_ref[...].astype(F32)
        dxhat = dh_v * g_ref[...]
        m = jnp.mean(dxhat * xhat, axis=-1, keepdims=True)
        dx = dres_ref[...] + r * (dxhat - xhat * m)
        dx_ref[...] = dx
        dxb_ref[...] = dx.astype(BF16)

        @pl.when(pl.program_id(0) == 0)
        def _():
            dg_ref[...] = jnp.zeros_like(dg_ref)

        dg_ref[0:1, :] += jnp.sum(dh_v * xhat, axis=0, keepdims=True)

    return pl.pallas_call(
        body, grid=(t // rb,),
        in_specs=[_row_spec(rb, d), _row_spec(rb, d), _full_spec((1, d)), _row_spec(rb, d)],
        out_specs=[_row_spec(rb, d), _row_spec(rb, d), _full_spec((8, d))],
        out_shape=[jax.ShapeDtypeStruct((t, d), F32), jax.ShapeDtypeStruct((t, d), BF16), jax.ShapeDtypeStruct((8, d), F32)],
        compiler_params=_params(("arbitrary",)), name=name,
    )(dh, x, g, dres)


def _final(x, tgt, g, *, name, rb=256):
    t, d = x.shape
    rb = _tile(t, rb, 8)
    inv_d = 1.0 / d

    def body(x_ref, t_ref, g_ref, l_ref, dx_ref, dxb_ref, dg_ref):
        xv = x_ref[...]
        gv = g_ref[...]
        r = lax.rsqrt(jnp.mean(xv * xv, axis=-1, keepdims=True) + RMS_EPS)
        xhat = xv * r
        e = xhat * gv - t_ref[...]
        dy = e * inv_d
        dxhat = dy * gv
        m = jnp.mean(dxhat * xhat, axis=-1, keepdims=True)
        dx = r * (dxhat - xhat * m)
        dx_ref[...] = dx
        dxb_ref[...] = dx.astype(BF16)

        @pl.when(pl.program_id(0) == 0)
        def _():
            l_ref[...] = jnp.zeros_like(l_ref)
            dg_ref[...] = jnp.zeros_like(dg_ref)

        l_ref[0:1, :] += jnp.sum(e * e, axis=0, keepdims=True) * (0.5 * inv_d)
        dg_ref[0:1, :] += jnp.sum(dy * xhat, axis=0, keepdims=True)

    return pl.pallas_call(
        body, grid=(t // rb,),
        in_specs=[_row_spec(rb, d), _row_spec(rb, d), _full_spec((1, d))],
        out_specs=[_full_spec((8, d)), _row_spec(rb, d), _row_spec(rb, d), _full_spec((8, d))],
        out_shape=[jax.ShapeDtypeStruct((8, d), F32), jax.ShapeDtypeStruct((t, d), F32),
                   jax.ShapeDtypeStruct((t, d), BF16), jax.ShapeDtypeStruct((8, d), F32)],
        compiler_params=_params(("arbitrary",)), name=name,
    )(x, tgt, g)


def _a_mid_fwd(bcx, wconv, *, name, rb=256, cw=512):
    t, d3 = bcx.shape
    d = d3 // 3
    rb, cw = _tile(t, rb, 16), _tile(d, cw)

    def body(cur_ref, prev_ref, w_ref, y_ref):
        first = pl.program_id(0) == 0

        def f32(ref, cols):
            return ref[:, cols].astype(F32)

        for c0 in range(0, d, cw):
            cs = slice(c0, c0 + cw)
            gc, xs = slice(d + c0, d + c0 + cw), slice(2 * d + c0, 2 * d + c0 + cw)
            p_prev = jnp.where(first, 0.0, (f32(prev_ref, gc) * f32(prev_ref, xs))[BF16_HALO - HALO:])
            e = jnp.concatenate([p_prev, f32(cur_ref, gc) * f32(cur_ref, xs)], axis=0)
            w = w_ref[:, cs]
            q = w[0:1] * _shift(e, 2) + w[1:2] * _shift(e, 1) + w[2:3] * e
            y_ref[:, cs] = (f32(cur_ref, cs) * q[HALO:]).astype(BF16)

    return pl.pallas_call(
        body, grid=(t // rb,),
        in_specs=[_row_spec(rb, d3), _prev_spec16(rb, d3), _full_spec((3, d))], out_specs=_row_spec(rb, d),
        out_shape=jax.ShapeDtypeStruct((t, d), BF16), compiler_params=_params(("parallel",)), name=name,
    )(bcx, bcx, wconv)


def _a_mid_bwd(bcx, dy, wconv, *, name, rb=256, cw=512):
    t, d3 = bcx.shape
    d = d3 // 3
    rb, cw = _tile(t, rb, 16), _tile(d, cw)

    def body(cur_ref, prev_ref, next_ref, dy_ref, dyn_ref, w_ref, o_ref, dw_ref):
        i = pl.program_id(0)
        first, last = i == 0, i == pl.num_programs(0) - 1

        @pl.when(first)
        def _():
            dw_ref[...] = jnp.zeros_like(dw_ref)

        def f32(ref, cols):
            return ref[:, cols].astype(F32)

        for c0 in range(0, d, cw):
            cs = slice(c0, c0 + cw)
            gc, xs = slice(d + c0, d + c0 + cw), slice(2 * d + c0, 2 * d + c0 + cw)
            zeros = jnp.zeros((HALO, cw), F32)
            gb_c, gc_c, xs_c = f32(cur_ref, cs), f32(cur_ref, gc), f32(cur_ref, xs)
            p_prev = jnp.where(first, 0.0, (f32(prev_ref, gc) * f32(prev_ref, xs))[BF16_HALO - HALO:])
            e = jnp.concatenate([p_prev, gc_c * xs_c, zeros], axis=0)
            dq_next = jnp.where(last, 0.0, dyn_ref[:, cs] * f32(next_ref, cs)[:HALO])
            dy_c = dy_ref[:, cs]
            dq = jnp.concatenate([zeros, dy_c * gb_c, dq_next], axis=0)
            w = w_ref[:, cs]
            e1, e2 = _shift(e, 1), _shift(e, 2)
            q = w[0:1] * e2 + w[1:2] * e1 + w[2:3] * e
            dp = (w[2:3] * dq + w[1:2] * _shift(dq, -1) + w[0:1] * _shift(dq, -2))[HALO:HALO + rb]
            o_ref[:, cs] = (dy_c * q[HALO:HALO + rb]).astype(BF16)
            o_ref[:, gc] = (dp * xs_c).astype(BF16)
            o_ref[:, xs] = (dp * gc_c).astype(BF16)
            dq_c = dq[HALO:HALO + rb]
            dw_ref[0:1, cs] += jnp.sum(dq_c * e2[HALO:HALO + rb], axis=0, keepdims=True)
            dw_ref[1:2, cs] += jnp.sum(dq_c * e1[HALO:HALO + rb], axis=0, keepdims=True)
            dw_ref[2:3, cs] += jnp.sum(dq_c * e[HALO:HALO + rb], axis=0, keepdims=True)

    return pl.pallas_call(
        body, grid=(t // rb,),
        in_specs=[_row_spec(rb, d3), _prev_spec16(rb, d3), _next_spec16(rb, d3, t), _row_spec(rb, d), _next_spec(rb, d, t),
                  _full_spec((3, d))],
        out_specs=[_row_spec(rb, d3), _full_spec((8, d))],
        out_shape=[jax.ShapeDtypeStruct((t, d3), BF16), jax.ShapeDtypeStruct((8, d), F32)],
        compiler_params=_params(("arbitrary",)), name=name,
    )(bcx, bcx, bcx, dy, dy, wconv)


def _ffn_mid_fwd(up, wconv, bconv, *, name, rb=128, cw=512):
    t, f2 = up.shape
    f = f2 // 2
    rb, cw = _tile(t, rb, 16), _tile(f, cw)

    def body(cur_ref, prev_ref, w_ref, b_ref, act_ref, conv_ref):
        first = pl.program_id(0) == 0

        def conv(cols):
            prev = prev_ref[:, cols].astype(F32)[BF16_HALO - HALO:]
            e = jnp.concatenate([jnp.where(first, 0.0, prev), cur_ref[:, cols].astype(F32)], axis=0)
            w = w_ref[:, cols]
            out = (w[0:1] * _shift(e, 2) + w[1:2] * _shift(e, 1) + w[2:3] * e + b_ref[:, cols])[HALO:]
            conv_ref[:, cols] = out.astype(BF16)
            return out

        for c0 in range(0, f, cw):
            g = conv(slice(c0, c0 + cw))
            a = conv(slice(f + c0, f + c0 + cw))
            act_ref[:, c0:c0 + cw] = (g * _sigmoid(g) * a).astype(BF16)

    return pl.pallas_call(
        body, grid=(t // rb,),
        in_specs=[_row_spec(rb, f2), _prev_spec16(rb, f2), _full_spec((3, f2)), _full_spec((1, f2))],
        out_specs=[_row_spec(rb, f), _row_spec(rb, f2)],
        out_shape=[jax.ShapeDtypeStruct((t, f), BF16), jax.ShapeDtypeStruct((t, f2), BF16)],
        compiler_params=_params(("parallel",)), name=name,
    )(up, up, wconv, bconv)


def _ffn_mid_bwd(up, conv, dact, wconv, *, name, rb=128, cw=512):
    t, f2 = up.shape
    f = f2 // 2
    rb, cw = _tile(t, rb, 16), _tile(f, cw)

    def body(up_ref, conv_ref, convn_ref, da_ref, dan_ref, w_ref, o_ref, dwb_ref):
        i = pl.program_id(0)
        last = i == pl.num_programs(0) - 1

        @pl.when(i == 0)
        def _():
            dwb_ref[...] = jnp.zeros_like(dwb_ref)

        def rows(cols):
            return jnp.concatenate([conv_ref[:, cols].astype(F32), convn_ref[:, cols].astype(F32)[0:HALO]], axis=0)

        def back(dc, cols):
            w = w_ref[:, cols]
            dc1, dc2 = _shift(dc, -1)[:rb], _shift(dc, -2)[:rb]
            dc0 = dc[:rb]
            o_ref[:, cols] = (w[2:3] * dc0 + w[1:2] * dc1 + w[0:1] * dc2).astype(BF16)
            u = up_ref[:, cols].astype(F32)
            ones = jnp.ones((8, rb), BF16)
            for k, prod in enumerate((dc2 * u, dc1 * u, dc0 * u, dc0)):
                dwb_ref[k:k + 1, cols] += jnp.dot(ones, prod.astype(BF16), preferred_element_type=F32)[0:1]

        for c0 in range(0, f, cw):
            gcols, acols = slice(c0, c0 + cw), slice(f + c0, f + c0 + cw)
            g, a = rows(gcols), rows(acols)
            da = jnp.concatenate([da_ref[:, gcols], jnp.where(last, 0.0, dan_ref[:, gcols])], axis=0)
            sg = _sigmoid(g)
            back(da * a * (sg * (1.0 + g * (1.0 - sg))), gcols)
            back(da * (g * sg), acols)

    return pl.pallas_call(
        body, grid=(t // rb,),
        in_specs=[_row_spec(rb, f2), _row_spec(rb, f2), _next_spec16(rb, f2, t), _row_spec(rb, f), _next_spec(rb, f, t),
                  _full_spec((3, f2))],
        out_specs=[_row_spec(rb, f2), _full_spec((8, f2))],
        out_shape=[jax.ShapeDtypeStruct((t, f2), BF16), jax.ShapeDtypeStruct((8, f2), F32)],
        compiler_params=_params(("arbitrary",)), name=name,
    )(up, conv, conv, dact, dact, wconv)


def _causal_mask():
    return lax.broadcasted_iota(jnp.int32, (CHUNK, CHUNK), 0) >= lax.broadcasted_iota(jnp.int32, (CHUNK, CHUNK), 1)


def _b_mid_fwd(zp, vnorm, ws, bs, *, name, rb=512):
    t, d2 = zp.shape
    d = d2 // 2
    c = d // GROUPS
    rb = _tile(t, rb, CHUNK)

    def body(zp_ref, gv_ref, ws_ref, bs_ref, ug_ref, vn_ref, gate_ref):
        v = _gelu(zp_ref[:, d:].astype(F32))
        rv = lax.rsqrt(jnp.mean(v * v, axis=-1, keepdims=True) + RMS_EPS)
        vn_ref[...] = ((v * rv) * gv_ref[...]).astype(BF16)
        mask = _causal_mask()
        for h in range(GROUPS):
            hc = slice(h * c, (h + 1) * c)
            wm = jnp.where(mask, ws_ref[h], 0.0).astype(BF16)
            bcol = jnp.broadcast_to(bs_ref[h:h + 1, :], (CHUNK, CHUNK)).T[:, 0:1]
            for n in range(rb // CHUNK):
                rows = slice(n * CHUNK, (n + 1) * CHUNK)
                gate_ref[rows, hc] = jnp.dot(wm, vn_ref[rows, hc], preferred_element_type=F32) + bcol
        ug_ref[...] = (_gelu(zp_ref[:, :d].astype(F32)) * gate_ref[...]).astype(BF16)

    return pl.pallas_call(
        body, grid=(t // rb,),
        in_specs=[_row_spec(rb, d2), _full_spec((1, d)), _full_spec((GROUPS, CHUNK, CHUNK)), _full_spec((GROUPS, CHUNK))],
        out_specs=_row_spec(rb, d), out_shape=jax.ShapeDtypeStruct((t, d), BF16),
        scratch_shapes=[pltpu.VMEM((rb, d), BF16), pltpu.VMEM((rb, d), F32)],
        compiler_params=_params(("parallel",)), name=name,
    )(zp, vnorm, ws, bs)


def _b_mid_bwd(zp, dug, vnorm, ws, bs, *, name, rb=512):
    t, d2 = zp.shape
    d = d2 // 2
    c = d // GROUPS
    rb = _tile(t, rb, CHUNK)

    def body(zp_ref, dug_ref, gv_ref, ws_ref, bs_ref, dzp_ref, dws_ref, dbs_ref, dgv_ref,
             vn_ref, gate_ref, dm_ref, dvn_ref, dbacc_ref):
        i = pl.program_id(0)

        @pl.when(i == 0)
        def _():
            dws_ref[...] = jnp.zeros_like(dws_ref)
            dgv_ref[...] = jnp.zeros_like(dgv_ref)
            dbacc_ref[...] = jnp.zeros_like(dbacc_ref)

        zu, zv = zp_ref[:, :d].astype(F32), zp_ref[:, d:].astype(F32)
        (u, u_grad), (v, v_grad) = _gelu_and_grad(zu), _gelu_and_grad(zv)
        rv = lax.rsqrt(jnp.mean(v * v, axis=-1, keepdims=True) + RMS_EPS)
        vhat = v * rv
        gv = gv_ref[...]
        vn_ref[...] = (vhat * gv).astype(BF16)
        dug_v = dug_ref[...].astype(F32)
        dm = dug_v * u
        dm_ref[...] = dm.astype(BF16)
        mask = _causal_mask()
        for h in range(GROUPS):
            hc = slice(h * c, (h + 1) * c)
            wm = jnp.where(mask, ws_ref[h], 0.0)
            wm_b, wmt_b = wm.astype(BF16), wm.T.astype(BF16)
            bcol = jnp.broadcast_to(bs_ref[h:h + 1, :], (CHUNK, CHUNK)).T[:, 0:1]
            dws_h = jnp.zeros((CHUNK, CHUNK), F32)
            dbs_h = jnp.zeros((CHUNK, c), F32)
            for n in range(rb // CHUNK):
                rows = slice(n * CHUNK, (n + 1) * CHUNK)
                vn_c, dm_c = vn_ref[rows, hc], dm_ref[rows, hc]
                gate_ref[rows, hc] = jnp.dot(wm_b, vn_c, preferred_element_type=F32) + bcol
                dws_h += lax.dot_general(dm_c, vn_c, (((1,), (1,)), ((), ())), preferred_element_type=F32)
                dvn_ref[rows, hc] = jnp.dot(wmt_b, dm_c, preferred_element_type=F32)
                dbs_h += dm[rows, hc]
            dws_ref[h] += dws_h
            dbacc_ref[h] += dbs_h
        du = dug_v * gate_ref[...]
        dvn = dvn_ref[...]
        dvhat = dvn * gv
        m = jnp.mean(dvhat * vhat, axis=-1, keepdims=True)
        dv = rv * (dvhat - vhat * m)
        dgv_ref[0:1, :] += jnp.sum(dvn * vhat, axis=0, keepdims=True)
        dzp_ref[:, :d] = (du * u_grad).astype(BF16)
        dzp_ref[:, d:] = (dv * v_grad).astype(BF16)

        @pl.when(i == pl.num_programs(0) - 1)
        def _():
            ones = jnp.ones((8, c), F32)
            for h in range(GROUPS):
                dws_ref[h] = jnp.where(mask, dws_ref[h], 0.0)
                row = lax.dot_general(ones, dbacc_ref[h], (((1,), (1,)), ((), ())),
                                      precision=lax.Precision.HIGHEST, preferred_element_type=F32)
                dbs_ref[h:h + 1, :] = row[0:1]

    return pl.pallas_call(
        body, grid=(t // rb,),
        in_specs=[_row_spec(rb, d2), _row_spec(rb, d), _full_spec((1, d)), _full_spec((GROUPS, CHUNK, CHUNK)),
                  _full_spec((GROUPS, CHUNK))],
        out_specs=[_row_spec(rb, d2), _full_spec((GROUPS, CHUNK, CHUNK)), _full_spec((GROUPS, CHUNK)), _full_spec((8, d))],
        out_shape=[jax.ShapeDtypeStruct((t, d2), BF16), jax.ShapeDtypeStruct((GROUPS, CHUNK, CHUNK), F32),
                   jax.ShapeDtypeStruct((GROUPS, CHUNK), F32), jax.ShapeDtypeStruct((8, d), F32)],
        scratch_shapes=[pltpu.VMEM((rb, d), BF16), pltpu.VMEM((rb, d), F32), pltpu.VMEM((rb, d), BF16),
                        pltpu.VMEM((rb, d), F32), pltpu.VMEM((GROUPS, CHUNK, c), F32)],
        compiler_params=_params(("arbitrary",)), name=name,
    )(zp, dug, vnorm, ws, bs)


def _cast_into_full(w, layer, kind, place, *, name, dep=None, rb=256):
    _, r, c = w.shape
    rb = _tile(r, rb, 16)
    nrb = r // rb
    full = (r, c * N_CHIPS) if kind == "col" else (r * N_CHIPS, c)

    def body(place_ref, w_ref, *rest):
        rest[-1][...] = w_ref[...].astype(BF16)

    def o_index(i, place):
        return (i, place[0]) if kind == "col" else (i + place[0] * nrb, 0)

    in_specs = [pl.BlockSpec((None, rb, c), lambda i, place: (layer, i, 0))] + ([ANY] if dep is not None else [])
    return pl.pallas_call(
        body,
        grid_spec=pltpu.PrefetchScalarGridSpec(num_scalar_prefetch=1, grid=(nrb,), in_specs=in_specs,
                                               out_specs=pl.BlockSpec((rb, c), o_index)),
        out_shape=jax.ShapeDtypeStruct(full, BF16), compiler_params=_params(("parallel",)), name=name,
    )(place, w, *(() if dep is None else (dep,)))


def _adamw_layer(w, g, m, v, layer, prev, *, name, rb=128):
    _, r, c = w.shape
    rb = _tile(r, rb, 8)
    c1 = 1.0 - ADAM_B1 ** ADAM_STEP
    c2 = 1.0 - ADAM_B2 ** ADAM_STEP

    def body(w_ref, g_ref, m_ref, v_ref, *rest):
        go_ref, d_ref, nm_ref, nv_ref = rest[-4:]
        gv = g_ref[...]
        nm = ADAM_B1 * m_ref[...] + (1.0 - ADAM_B1) * gv
        nv = ADAM_B2 * v_ref[...] + (1.0 - ADAM_B2) * (gv * gv)
        go_ref[...] = gv
        nm_ref[...] = nm
        nv_ref[...] = nv
        d_ref[...] = -ADAM_LR * ((nm / c1) / (jnp.sqrt(nv / c2) + ADAM_EPS) + ADAM_WD * w_ref[...])

    lay = pl.BlockSpec((None, rb, c), lambda i: (layer, i, 0))
    return pl.pallas_call(
        body, grid=(r // rb,), in_specs=[lay, _row_spec(rb, c), lay, lay] + ([ANY] * 4 if prev else []), out_specs=[lay] * 4,
        out_shape=[jax.ShapeDtypeStruct(w.shape, F32)] * 4, input_output_aliases={4 + k: k for k in range(4)} if prev else {},
        compiler_params=_params(("parallel",)), name=name,
    )(w, g, m, v, *(prev or ()))


HBM = pl.BlockSpec(memory_space=pltpu.HBM)
SEM = pl.BlockSpec(memory_space=pltpu.SEMAPHORE)
SIDE_EFFECT = pltpu.SideEffectType.DATAFLOW_SIDE_EFFECTING


def _place():
    x, y, c = lax.axis_index("x"), lax.axis_index("y"), lax.axis_index("c")
    chips = [(1 - x, y), (x, 1 - y), (1 - x, 1 - y)]
    return x, y, c, 2 * x + y, chips


def _half(ref, kind, c):
    r, w = ref.shape
    if kind == "col":
        return ref.at[pl.ds(pl.multiple_of(c * (r // 2), 8), r // 2), :]
    return ref.at[:, pl.ds(pl.multiple_of(c * (w // 2), 128), w // 2)]


def _shard(ref, kind, s):
    r, w = ref.shape
    if kind == "col":
        return ref.at[:, pl.ds(pl.multiple_of(s * (w // N_CHIPS), 128), w // N_CHIPS)]
    return ref.at[pl.ds(pl.multiple_of(s * (r // N_CHIPS), 8), r // N_CHIPS), :]


def _remote(src, dst, send_sem, recv_sem, dev):
    return pltpu.make_async_remote_copy(src_ref=src, dst_ref=dst, send_sem=send_sem, recv_sem=recv_sem,
                                        device_id=dev, device_id_type=MESH)


def _start(name, bufs, plan, sem_shape, dep=None):
    n = len(bufs)
    n_in = n + (dep is not None)

    def body(*refs):
        sends, _ = plan(refs[:n], refs[n_in], refs[n_in + 1])
        for cp in sends:
            cp.start()
        refs[n_in + 2 + n][...] = jnp.zeros((8, 128), F32)

    dma = pltpu.SemaphoreType.DMA
    outs = pl.pallas_call(
        body, name=name,
        out_shape=(dma(sem_shape), dma(sem_shape), *[pltpu.HBM(b.shape, b.dtype) for b in bufs], jax.ShapeDtypeStruct((8, 128), F32)),
        in_specs=(HBM,) * n + ((ANY,) if dep is not None else ()),
        out_specs=(SEM, SEM) + (HBM,) * n + (pl.BlockSpec(memory_space=pltpu.VMEM),),
        input_output_aliases={i: i + 2 for i in range(n)},
        compiler_params=pltpu.CompilerParams(has_side_effects=SIDE_EFFECT),
    )(*[pltpu.with_memory_space_constraint(b, pltpu.HBM) for b in bufs], *(() if dep is None else (dep,)))
    return outs[0], outs[1], list(outs[2:2 + n]), outs[2 + n]


def _wait(name, started, plan, after):
    send, recv, bufs, _ = started
    n = len(bufs)

    def body(*refs):
        sends, recvs = plan(refs[:n], refs[n], refs[n + 1])
        for cp in sends:
            cp.wait_send()
        for cp in recvs:
            cp.wait_recv()

    return list(pl.pallas_call(
        body, name=name, out_shape=tuple(pltpu.HBM(b.shape, b.dtype) for b in bufs),
        in_specs=(HBM,) * n + (SEM, SEM, ANY), out_specs=(HBM,) * n, input_output_aliases={i: i for i in range(n)},
        compiler_params=pltpu.CompilerParams(has_side_effects=SIDE_EFFECT),
    )(*bufs, send, recv, after))


KINDS = ("col", "row")


def _gather_ici_plan(n_small):
    def plan(refs, send, recv):
        x, y, c, s, chips = _place()
        n = len(KINDS) + n_small
        sends, recvs = [], []
        for k, (px, py) in enumerate(chips):
            sp = 2 * px + py
            for a, kind in enumerate(KINDS):
                mine, theirs = _half(_shard(refs[a], kind, s), kind, c), _half(_shard(refs[a], kind, sp), kind, c)
                sends.append(_remote(mine, mine, send.at[k * n + a], recv.at[k * n + a], (px, py, c)))
                recvs.append(_remote(theirs, theirs, send.at[k * n + a], recv.at[k * n + a], (px, py, c)))
            for b in range(n_small):
                ref, sem = refs[len(KINDS) + b], k * n + len(KINDS) + b
                sends.append(_remote(ref.at[s], ref.at[s], send.at[sem], recv.at[sem], (px, py, c)))
                recvs.append(_remote(ref.at[sp], ref.at[sp], send.at[sem], recv.at[sem], (px, py, c)))
        return sends, recvs
    return plan


def _gather_d2d_plan(refs, send, recv):
    x, y, c, _, chips = _place()
    n = len(KINDS)
    sends, recvs = [], []
    for k, (px, py) in enumerate(chips):
        for a, kind in enumerate(KINDS):
            region, sem = _shard(refs[a], kind, 2 * px + py), k * n + a
            sends.append(_remote(_half(region, kind, c), _half(region, kind, c), send.at[sem], recv.at[sem], (x, y, 1 - c)))
            recvs.append(_remote(_half(region, kind, 1 - c), _half(region, kind, 1 - c), send.at[sem], recv.at[sem], (x, y, 1 - c)))
    return sends, recvs


def _swap_plan(refs, send, recv):
    x, y, c, _, _ = _place()
    n = len(KINDS)
    cps = [_remote(_half(refs[a], KINDS[a], 1 - c), refs[n + a], send.at[a], recv.at[a], (x, y, 1 - c)) for a in range(n)]
    return cps, cps


def _exchange_plan(refs, send, recv):
    x, y, c, _, chips = _place()
    n = len(KINDS)
    cps = []
    for k, (px, py) in enumerate(chips):
        for a in range(n):
            cps.append(_remote(_shard(refs[a], KINDS[a], 2 * px + py), refs[n + a].at[k], send.at[k * n + a], recv.at[k * n + a],
                               (px, py, c)))
    return cps, cps


def _share_plan(refs, send, recv):
    x, y, c, _, _ = _place()
    sends = [_remote(_half(refs[a], KINDS[a], c), _half(refs[a], KINDS[a], c), send.at[a], recv.at[a], (x, y, 1 - c))
             for a in range(len(KINDS))]
    recvs = [_remote(_half(refs[a], KINDS[a], 1 - c), _half(refs[a], KINDS[a], 1 - c), send.at[a], recv.at[a], (x, y, 1 - c))
             for a in range(len(KINDS))]
    return sends, recvs


def _spread_plan(refs, send, recv):
    packed, slots = refs
    x, y, c, _, _ = _place()
    sends, recvs = [], []
    for k in range(1, 8):
        px, py, pc = x ^ (k >> 2), y ^ ((k >> 1) & 1), c ^ (k & 1)
        sends.append(_remote(packed, slots.at[4 * x + 2 * y + c], send.at[k - 1], recv.at[k - 1], (px, py, pc)))
        recvs.append(_remote(packed, slots.at[4 * px + 2 * py + pc], send.at[k - 1], recv.at[k - 1], (px, py, pc)))
    return sends, recvs


def _half_index(kind, nblk):
    def index(i, j, place):
        return (i + place[1] * nblk[0], j) if kind == "col" else (i, j + place[1] * nblk[1])
    return index


def _chip_partial(g, other, kind, place, *, name):
    hr, hc = other.shape
    col = kind == "col"
    rb = _tile(hr if col else hr // N_CHIPS, 512, 16)
    cb = _tile(hc // N_CHIPS if col else hc, 1024)
    nblk = (hr // rb, hc // cb)
    own = (nblk[1] if col else nblk[0]) // N_CHIPS

    def block(i, j, place):
        s = place[0]
        return (i, j + jnp.where(j >= s * own, own, 0)) if col else (i + jnp.where(i >= s * own, own, 0), j)

    def body(place_ref, g_ref, o_ref, p_ref):
        p_ref[...] = (g_ref[...].astype(F32) + o_ref[...].astype(F32)).astype(BF16)

    plain = pl.BlockSpec((rb, cb), block)
    in_half = pl.BlockSpec((rb, cb), lambda i, j, place: _half_index(kind, nblk)(*block(i, j, place), place))
    grid = (nblk[0], nblk[1] - own) if col else (nblk[0] - own, nblk[1])
    return pl.pallas_call(
        body,
        grid_spec=pltpu.PrefetchScalarGridSpec(num_scalar_prefetch=1, grid=grid, in_specs=[in_half, plain], out_specs=plain),
        out_shape=jax.ShapeDtypeStruct((hr, hc), BF16), compiler_params=_params(("parallel", "parallel")), name=name,
    )(place, g, other)


def _reduce_half(g, other, recv, kind, place, *, name):
    _, pr, pc = recv.shape
    rb, cb = _tile(pr, 512, 16), _tile(pc, 1024)
    nblk = (pr // rb, pc // cb)
    full = (pr * 2, pc) if kind == "col" else (pr, pc * 2)

    def g_index(i, j, place):
        s, c = place[0], place[1]
        return (i + c * nblk[0], j + s * nblk[1]) if kind == "col" else (i + s * nblk[0], j + c * nblk[1])

    def o_index(i, j, place):
        return (i, j + place[0] * nblk[1]) if kind == "col" else (i + place[0] * nblk[0], j)

    def body(place_ref, g_ref, o_ref, r_ref, out_ref):
        acc = g_ref[...].astype(F32) + o_ref[...].astype(F32)
        for k in range(3):
            acc = acc + r_ref[k].astype(F32)
        out_ref[...] = acc

    return pl.pallas_call(
        body,
        grid_spec=pltpu.PrefetchScalarGridSpec(
            num_scalar_prefetch=1, grid=nblk,
            in_specs=[pl.BlockSpec((rb, cb), g_index), pl.BlockSpec((rb, cb), o_index),
                      pl.BlockSpec((3, rb, cb), lambda i, j, place: (0, i, j))],
            out_specs=pl.BlockSpec((rb, cb), _half_index(kind, nblk))),
        out_shape=jax.ShapeDtypeStruct(full, F32), compiler_params=_params(("parallel", "parallel")), name=name,
    )(place, g, other, recv)


def _sum_slots(packed, slots, me, *, name, rb=512):
    r, w = packed.shape
    rb = _tile(r, rb, 8)

    def body(me_ref, p_ref, s_ref, o_ref):
        acc = None
        for j in range(8):
            term = jnp.where(me_ref[0] == j, p_ref[...], s_ref[j])
            acc = term if acc is None else acc + term
        o_ref[...] = acc

    return pl.pallas_call(
        body,
        grid_spec=pltpu.PrefetchScalarGridSpec(
            num_scalar_prefetch=1, grid=(r // rb,),
            in_specs=[pl.BlockSpec((rb, w), lambda i, me: (i, 0)), pl.BlockSpec((8, rb, w), lambda i, me: (0, i, 0))],
            out_specs=pl.BlockSpec((rb, w), lambda i, me: (i, 0))),
        out_shape=jax.ShapeDtypeStruct((r, w), F32), compiler_params=_params(("parallel",)), name=name,
    )(me, packed, slots)


def _half_shape(a, kind):
    return (a.shape[0] // 2, a.shape[1]) if kind == "col" else (a.shape[0], a.shape[1] // 2)


def _rs_swap(tag, grads):
    others = [lax.empty(_half_shape(g, k), g.dtype) for g, k in zip(grads, KINDS)]
    return _start(f"rs_{tag}_swap", list(grads) + others, _swap_plan, (len(KINDS),))


def _rs_exchange(tag, swapped, place, after):
    bufs = _wait(f"rs_{tag}_swap_wait", swapped, _swap_plan, after)
    n = len(KINDS)
    grads, others = bufs[:n], bufs[n:]
    parts = [_chip_partial(g, o, k, place, name=f"rs_{tag}_partial_{k}") for g, o, k in zip(grads, others, KINDS)]
    lands = []
    for p, k in zip(parts, KINDS):
        piece = (p.shape[0], p.shape[1] // N_CHIPS) if k == "col" else (p.shape[0] // N_CHIPS, p.shape[1])
        lands.append(lax.empty((3,) + piece, p.dtype))
    return _start(f"rs_{tag}_exchange", parts + lands, _exchange_plan, (3 * n,)), grads, others


def _rs_share(tag, exchanged, place, after):
    started, grads, others = exchanged
    n = len(KINDS)
    recvs = _wait(f"rs_{tag}_exchange_wait", started, _exchange_plan, after)[n:]
    halves = [_reduce_half(g, o, r, k, place, name=f"rs_{tag}_reduce_{k}") for g, o, r, k in zip(grads, others, recvs, KINDS)]
    return _start(f"rs_{tag}_share", halves, _share_plan, (n,))


def _rs_finish(tag, shared, after):
    return _wait(f"rs_{tag}_share_wait", shared, _share_plan, after)


def _spread(tag, parts, dep):
    rows = [p.reshape(-1, 128) for p in parts]
    n = sum(r.shape[0] for r in rows)
    rows.append(jnp.zeros(((-n) % 512, 128), F32))
    packed = jnp.concatenate(rows, axis=0)
    return _start(f"small_{tag}_spread", [packed, lax.empty((8,) + packed.shape, F32)], _spread_plan, (7,), dep=dep)


def _spread_sum(tag, started, parts, me, after):
    packed, slots = _wait(f"small_{tag}_spread_wait", started, _spread_plan, after)
    total = _sum_slots(packed, slots, me, name=f"small_{tag}_sum")
    out, row = [], 0
    for p in parts:
        n = p.size // 128
        out.append(total[row:row + n].reshape(p.shape))
        row += n
    return out


def _ffn_fwd(x, h, w_up, conv_w, conv_b, w_down, tag, tm_up=1024):
    up = _mm(h, w_up, tm=tm_up, out_dtype=BF16, name=f"ffn{tag}_up")
    act, conv = _ffn_mid_fwd(up, conv_w, conv_b, name=f"ffn{tag}_mid")
    x_out = _mm(act, w_down, res=x, tk=2816, name=f"ffn{tag}_down")
    return x_out, (up, conv, act)


def kernel(x, a_norm, a_in, a_conv, a_out, b_norm, b_in, b_vnorm, b_ws, b_bs, b_out, f_norm, f_up, f_conv_w, f_conv_b, f_down, final_norm, loss_target, m_a_norm, m_a_in, m_a_conv, m_a_out, m_b_norm, m_b_in, m_b_vnorm, m_b_ws, m_b_bs, m_b_out, m_f_norm, m_f_up, m_f_conv_w, m_f_conv_b, m_f_down, m_final_norm, v_a_norm, v_a_in, v_a_conv, v_a_out, v_b_norm, v_b_in, v_b_vnorm, v_b_ws, v_b_bs, v_b_out, v_f_norm, v_f_up, v_f_conv_w, v_f_conv_b, v_f_down, v_final_norm):
    t, d = x.shape[1], x.shape[2]
    f2 = f_up.shape[2] * N_CHIPS
    x0, tgt = x.reshape(t, d), loss_target.reshape(t, d)
    ax, ay, ac = lax.axis_index("x"), lax.axis_index("y"), lax.axis_index("c")
    s = 2 * ax + ay
    place = jnp.stack([s, ac]).astype(jnp.int32)
    me = (4 * ax + 2 * ay + ac).astype(jnp.int32).reshape(1)

    def stacked(a):
        return lax.dynamic_update_index_in_dim(jnp.zeros((N_CHIPS,) + a.shape, F32), a, s, 0)

    def cast_pair(tag, w_in, w_out, layer, dep):
        return [_cast_into_full(w_in, layer, "col", place, name=f"cast_{tag}_in", dep=dep),
                _cast_into_full(w_out, layer, "row", place, name=f"cast_{tag}_out", dep=dep)]

    def gather_start(tag, fulls, small, dep):
        return _start(f"ag_{tag}_ici", fulls + small, _gather_ici_plan(len(small)), (3 * (2 + len(small)),), dep=dep)

    def gather_forward(tag, started, n_small, after):
        bufs = _wait(f"ag_{tag}_ici_wait", started, _gather_ici_plan(n_small), after)
        return _start(f"ag_{tag}_d2d", bufs[:2], _gather_d2d_plan, (3 * 2,)), bufs[2:]

    def gather_finish(tag, forwarded, after):
        return _wait(f"ag_{tag}_d2d_wait", forwarded, _gather_d2d_plan, after)

    small = [stacked(a_conv[0]), stacked(b_norm), stacked(b_vnorm), stacked(f_conv_w.reshape(2 * 3, -1))]
    ag_a = gather_start("a", cast_pair("a", a_in, a_out, 0, None), small, None)
    full_f0 = cast_pair("f0", f_up, f_down, 0, ag_a[3])
    full_b = cast_pair("b", b_in, b_out, 0, full_f0[1])
    full_f1 = cast_pair("f1", f_up, f_down, 1, full_b[1])

    def unshard(a):
        return jnp.transpose(a, (1, 0, 2)).reshape(a.shape[1], -1)

    ws, bs = b_ws[0], b_bs[0]

    h0 = _rms_fwd(x0, a_norm, dep=full_f1[1], name="a_norm")
    fw_a, (g_aconv, g_bnorm, g_bvnorm, g_fconv) = gather_forward("a", ag_a, 4, h0)
    ag_f0 = gather_start("f0", full_f0, [], fw_a[3])
    ag_b = gather_start("b", full_b, [], ag_f0[3])
    ag_f1 = gather_start("f1", full_f1, [], ag_b[3])
    w_ai, w_ao = gather_finish("a", fw_a, ag_f1[3])
    a_conv_f, b_norm_f, b_vnorm_f = unshard(g_aconv), unshard(g_bnorm), unshard(g_bvnorm)
    f_conv_f = unshard(g_fconv).reshape(2, 3, f2)
    bcx = _mm(h0, w_ai, tm=2048, out_dtype=BF16, name="a_in")
    y = _a_mid_fwd(bcx, a_conv_f, name="a_mid")
    x1 = _mm(y, w_ao, res=x0, tm=512, tn=2048, name="a_out")
    fw_f0, _ = gather_forward("f0", ag_f0, 0, x1)
    h1 = _rms_fwd(x1, f_norm[0:1], dep=fw_f0[3], name="ffn0_norm")
    w_up0, w_dn0 = gather_finish("f0", fw_f0, h1)
    x2, (up0, conv0, act0) = _ffn_fwd(x1, h1, w_up0, f_conv_f[0], f_conv_b[0:1], w_dn0, 0, tm_up=2048)
    fw_b, _ = gather_forward("b", ag_b, 0, up0)
    w_bi, w_bo = gather_finish("b", fw_b, act0)
    h2 = _rms_fwd(x2, b_norm_f, name="b_norm")
    zp = _mm(h2, w_bi, tm=2048, out_dtype=BF16, name="b_in")
    fw_f1, _ = gather_forward("f1", ag_f1, 0, zp)
    ug = _b_mid_fwd(zp, b_vnorm_f, ws, bs, name="b_mid")
    x3, h3 = _mm(ug, w_bo, res=x2, norm=f_norm[1:2], tm=512, tn=2048, name="b_out")
    w_up1, w_dn1 = gather_finish("f1", fw_f1, x3)
    x4, (up1, conv1, act1) = _ffn_fwd(x3, h3, w_up1, f_conv_f[1], f_conv_b[1:2], w_dn1, 1, tm_up=2048)
    loss_rows, dx4, dx4b, d_final = _final(x4, tgt, final_norm.reshape(1, d), name="final")

    d_dn1 = _mm(act1, dx4b, ta=True, tm=1408, out_dtype=BF16, name="ffn1_ddown")
    dact1 = _mm(dx4b, w_dn1, tb=True, tn=512, tm=2048, name="ffn1_dact")
    dup1, d_fwb1 = _ffn_mid_bwd(up1, conv1, dact1, f_conv_f[1], name="ffn1_mid_bwd")
    d_up1 = _mm(h3, dup1, ta=True, out_dtype=BF16, tk=4096, name="ffn1_dup")
    sw_f1 = _rs_swap("f1", [d_up1, d_dn1])
    dh3 = _mm(dup1, w_up1, tb=True, tk=2816, out_dtype=BF16, dep=sw_f1[3], name="ffn1_dh")
    dx3, dx3b, d_fnorm1 = _rms_bwd(dh3, x3, f_norm[1:2], dx4, name="ffn1_norm_bwd")
    ex_f1 = _rs_exchange("f1", sw_f1, place, dx3)

    d_bo = _mm(ug, dx3b, ta=True, out_dtype=BF16, tk=4096, dep=ex_f1[0][3], name="b_dout")
    dug = _mm(dx3b, w_bo, tb=True, tm=512, tn=2048, out_dtype=BF16, name="b_dug")
    dzp, d_ws, d_bs, d_bvnorm = _b_mid_bwd(zp, dug, b_vnorm_f, ws, bs, name="b_mid_bwd")
    d_bi = _mm(h2, dzp, ta=True, out_dtype=BF16, tk=4096, name="b_din")
    sw_b = _rs_swap("b", [d_bi, d_bo])
    dh2 = _mm(dzp, w_bi, tb=True, tk=4096, out_dtype=BF16, dep=sw_b[3], name="b_dh")
    dx2, dx2b, d_bnorm = _rms_bwd(dh2, x2, b_norm_f, dx3, name="b_norm_bwd")
    ex_b = _rs_exchange("b", sw_b, place, dx2)

    d_dn0 = _mm(act0, dx2b, ta=True, tm=1408, out_dtype=BF16, dep=ex_b[0][3], name="ffn0_ddown")
    dact0 = _mm(dx2b, w_dn0, tb=True, tn=512, tm=2048, name="ffn0_dact")
    dup0, d_fwb0 = _ffn_mid_bwd(up0, conv0, dact0, f_conv_f[0], name="ffn0_mid_bwd")
    sh_f1 = _rs_share("f1", ex_f1, place, dup0)
    d_up0 = _mm(h1, dup0, ta=True, out_dtype=BF16, tk=4096, dep=sh_f1[3], name="ffn0_dup")
    sw_f0 = _rs_swap("f0", [d_up0, d_dn0])
    g_up1, g_dn1 = _rs_finish("f1", sh_f1, sw_f0[3])
    dh1 = _mm(dup0, w_up0, tb=True, tk=2816, out_dtype=BF16, dep=sw_f0[3], name="ffn0_dh")
    dx1, dx1b, d_fnorm0 = _rms_bwd(dh1, x1, f_norm[0:1], dx2, name="ffn0_norm_bwd")
    ex_f0 = _rs_exchange("f0", sw_f0, place, dx1)
    early = [jnp.concatenate([d_bnorm, d_bvnorm, d_fnorm0, d_fnorm1, d_final, loss_rows], axis=0),
             jnp.concatenate([d_fwb0, d_fwb1], axis=0), jnp.concatenate([d_ws.reshape(-1, CHUNK), d_bs], axis=0)]
    sp_early = _spread("early", early, ex_f0[0][3])
    sh_b = _rs_share("b", ex_b, place, sp_early[3])

    d_ao = _mm(y, dx1b, ta=True, out_dtype=BF16, tk=4096, dep=sh_b[3], name="a_dout")
    dyy = _mm(dx1b, w_ao, tb=True, tm=512, tn=2048, name="a_dy")
    dbcx, d_aconv = _a_mid_bwd(bcx, dyy, a_conv_f, name="a_mid_bwd")
    d_ai = _mm(h0, dbcx, ta=True, out_dtype=BF16, tk=4096, name="a_din")
    sw_a = _rs_swap("a", [d_ai, d_ao])
    g_bi, g_bo = _rs_finish("b", sh_b, sw_a[3])
    early_adamw = {"b_in": _adamw_layer(b_in, g_bi, m_b_in, v_b_in, 0, None, name="adamw_b_in"),
                   "b_out": _adamw_layer(b_out, g_bo, m_b_out, v_b_out, 0, None, name="adamw_b_out")}
    ex_a = _rs_exchange("a", sw_a, place, early_adamw["b_in"][1][0, :8, :128] + early_adamw["b_out"][1][0, :8, :128])
    dh0 = _mm(dbcx, w_ai, tb=True, tk=3072, out_dtype=BF16, dep=ex_a[0][3], name="a_dh")
    grad_x, _, d_anorm = _rms_bwd(dh0, x0, a_norm, dx1, name="a_norm_bwd")
    late = [jnp.concatenate([d_anorm, d_aconv], axis=0)]
    sp_late = _spread("late", late, ex_a[0][3])
    sh_f0 = _rs_share("f0", ex_f0, place, sp_late[3])
    sh_a = _rs_share("a", ex_a, place, sh_f0[3])
    g_up0, g_dn0 = _rs_finish("f0", sh_f0, sh_a[3])
    g_ai, g_ao = _rs_finish("a", sh_a, g_up0)
    r_a, r_b, r_c = _spread_sum("early", sp_early, early, me, g_ai)
    (r_l,) = _spread_sum("late", sp_late, late, me, r_a)

    loss = jnp.sum(r_a[40])
    cs, fs = d // N_CHIPS, f2 // N_CHIPS

    def mine(a, width):
        return lax.dynamic_slice_in_dim(a, s * width, width, axis=1)

    grads = {
        "a_norm": r_l[0:1], "a_conv": mine(r_l[8:11], cs), "b_norm": mine(r_a[0:1], cs), "b_vnorm": mine(r_a[8:9], cs),
        "f_norm": jnp.concatenate([r_a[16:17], r_a[24:25]], axis=0), "final_norm": r_a[32:33],
        "b_ws": r_c[:GROUPS * CHUNK], "b_bs": r_c[GROUPS * CHUNK:],
        "f_conv_w": jnp.concatenate([mine(r_b[0:3], fs), mine(r_b[8:11], fs)], axis=0),
        "f_conv_b": jnp.concatenate([r_b[3:4], r_b[11:12]], axis=0),
        "a_in": g_ai, "a_out": g_ao, "b_in": g_bi, "b_out": g_bo,
    }
    names = ["a_norm", "a_in", "a_conv", "a_out", "b_norm", "b_in", "b_vnorm", "b_ws", "b_bs", "b_out", "f_norm", "f_up",
             "f_conv_w", "f_conv_b", "f_down", "final_norm"]
    weights = dict(zip(names, [a_norm, a_in, a_conv, a_out, b_norm, b_in, b_vnorm, b_ws, b_bs, b_out, f_norm, f_up, f_conv_w,
                               f_conv_b, f_down, final_norm]))
    ms = dict(zip(names, [m_a_norm, m_a_in, m_a_conv, m_a_out, m_b_norm, m_b_in, m_b_vnorm, m_b_ws, m_b_bs, m_b_out, m_f_norm,
                          m_f_up, m_f_conv_w, m_f_conv_b, m_f_down, m_final_norm]))
    vs = dict(zip(names, [v_a_norm, v_a_in, v_a_conv, v_a_out, v_b_norm, v_b_in, v_b_vnorm, v_b_ws, v_b_bs, v_b_out, v_f_norm,
                          v_f_up, v_f_conv_w, v_f_conv_b, v_f_down, v_final_norm]))
    result = {}
    for n in names:
        w = weights[n]
        if n in ("f_up", "f_down"):
            g1, g0 = (g_up1, g_up0) if n == "f_up" else (g_dn1, g_dn0)
            first = _adamw_layer(w, g1, ms[n], vs[n], 1, None, name=f"adamw_{n}1")
            result[n] = _adamw_layer(w, g0, ms[n], vs[n], 0, tuple(first), name=f"adamw_{n}0")
            continue
        if n in early_adamw:
            result[n] = early_adamw[n]
            continue
        g2 = grads[n]
        as3d = (lambda a: a.reshape((1,) + g2.shape))
        result[n] = [o.reshape(w.shape) for o in _adamw_layer(as3d(w), g2, as3d(ms[n]), as3d(vs[n]), 0, None, name=f"adamw_{n}")]

    return (loss, grad_x.reshape(x.shape), *[result[n][0] for n in names], *[result[n][1] for n in names],
            *[result[n][2] for n in names], *[result[n][3] for n in names])
```

```python
import jax
import jax.numpy as jnp
from jax import lax
from jax.experimental import pallas as pl
from jax.experimental.pallas import tpu as pltpu

F32 = jnp.float32
BF16 = jnp.bfloat16
MESH = pl.DeviceIdType.MESH
ANY = pl.BlockSpec(memory_space=pl.ANY)

RMS_EPS = 1e-5
CHUNK = 128
GROUPS = 8
ADAM_LR, ADAM_B1, ADAM_B2, ADAM_EPS, ADAM_WD, ADAM_STEP = 0.001, 0.9, 0.999, 1e-08, 0.01, 10

N_CHIPS = 4
HALO = 8
BF16_HALO = 16
VMEM_LIMIT = 56 * 1024 * 1024
GELU_C = 0.7978845608028654
GELU_A = 0.044715


def _params(sem=None):
    return pltpu.CompilerParams(dimension_semantics=sem, vmem_limit_bytes=VMEM_LIMIT)


def _tile(dim, pref, quantum=128):
    if dim <= pref:
        return dim
    t = (pref // quantum) * quantum
    while t >= quantum:
        if dim % t == 0:
            return t
        t -= quantum
    return dim


def _mm(a, b, *, name, ta=False, tb=False, res=None, norm=None, dep=None, out_dtype=F32, tm=1024, tn=1024, tk=2048):
    (K, M) = a.shape if ta else a.shape[::-1]
    N = b.shape[0] if tb else b.shape[1]
    assert (b.shape[1] if tb else b.shape[0]) == K
    tm, tn, tk = _tile(M, tm), _tile(N, tn), _tile(K, tk)
    nk = K // tk
    assert norm is None or (tn == N and nk == 1)
    a_spec = pl.BlockSpec((tk, tm), lambda i, j, k: (k, i)) if ta else pl.BlockSpec((tm, tk), lambda i, j, k: (i, k))
    b_spec = pl.BlockSpec((tn, tk), lambda i, j, k: (j, k)) if tb else pl.BlockSpec((tk, tn), lambda i, j, k: (k, j))
    o_spec = pl.BlockSpec((tm, tn), lambda i, j, k: (i, j))
    dims = (((0 if ta else 1,), (1 if tb else 0,)), ((), ()))
    direct = out_dtype == F32
    n_in = 2 + (res is not None) + (norm is not None) + (dep is not None)

    def body(*refs):
        a_ref, b_ref = refs[0], refs[1]
        r_ref = refs[2] if res is not None else None
        g_ref = refs[2 + (res is not None)] if norm is not None else None
        o_ref = refs[n_in]
        acc_ref = o_ref if direct else refs[-1]
        part = lax.dot_general(a_ref[...], b_ref[...], dims, preferred_element_type=F32)
        if nk == 1:
            if r_ref is not None:
                part = part + r_ref[...]
            o_ref[...] = part.astype(o_ref.dtype)
            if g_ref is not None:
                r = lax.rsqrt(jnp.mean(part * part, axis=-1, keepdims=True) + RMS_EPS)
                refs[n_in + 1][...] = ((part * r) * g_ref[...]).astype(BF16)
            return
        k = pl.program_id(2)

        @pl.when(k == 0)
        def _():
            acc_ref[...] = part

        @pl.when(jnp.logical_and(k > 0, k < nk - 1))
        def _():
            acc_ref[...] += part

        @pl.when(k == nk - 1)
        def _():
            tot = acc_ref[...] + part
            if r_ref is not None:
                tot = tot + r_ref[...]
            o_ref[...] = tot.astype(o_ref.dtype)

    in_specs = ([a_spec, b_spec] + ([o_spec] if res is not None else [])
                + ([pl.BlockSpec((1, tn), lambda i, j, k: (0, j))] if norm is not None else []) + ([ANY] if dep is not None else []))
    args = (a, b) + tuple(x for x in (res, norm, dep) if x is not None)
    scratch = [] if (direct or nk == 1) else [pltpu.VMEM((tm, tn), F32)]
    out_shape = jax.ShapeDtypeStruct((M, N), out_dtype)
    return pl.pallas_call(
        body, grid=(M // tm, N // tn, nk), in_specs=in_specs, out_specs=[o_spec, o_spec] if norm is not None else o_spec,
        out_shape=[out_shape, jax.ShapeDtypeStruct((M, N), BF16)] if norm is not None else out_shape, scratch_shapes=scratch,
        compiler_params=_params(("parallel", "parallel", "arbitrary")), name=name,
    )(*args)


def _row_spec(rb, w):
    return pl.BlockSpec((rb, w), lambda i: (i, 0))


def _next_spec(rb, w, t):
    return pl.BlockSpec((HALO, w), lambda i: (jnp.minimum((i + 1) * (rb // HALO), t // HALO - 1), 0))


def _prev_spec16(rb, w):
    return pl.BlockSpec((BF16_HALO, w), lambda i: (jnp.maximum(i * (rb // BF16_HALO) - 1, 0), 0))


def _next_spec16(rb, w, t):
    return pl.BlockSpec((BF16_HALO, w), lambda i: (jnp.minimum((i + 1) * (rb // BF16_HALO), t // BF16_HALO - 1), 0))


def _full_spec(shape):
    return pl.BlockSpec(shape, lambda i: tuple(0 for _ in shape))


def _shift(e, s):
    return pltpu.roll(e, s % e.shape[0], 0)


def _gelu(x):
    return 0.5 * x * (1.0 + jnp.tanh(GELU_C * (x + GELU_A * x * x * x)))


def _gelu_and_grad(x):
    x2 = x * x
    th = jnp.tanh(GELU_C * (x + GELU_A * x2 * x))
    half = 0.5 * (1.0 + th)
    return x * half, half + 0.5 * x * (1.0 - th * th) * (GELU_C * (1.0 + 3.0 * GELU_A * x2))


def _sigmoid(x):
    return 1.0 / (1.0 + jnp.exp(-x))


def _rms_fwd(x, g, *, name, dep=None, rb=256):
    t, d = x.shape
    rb = _tile(t, rb, 8)

    def body(x_ref, g_ref, *rest):
        h_ref = rest[-1]
        xv = x_ref[...]
        r = lax.rsqrt(jnp.mean(xv * xv, axis=-1, keepdims=True) + RMS_EPS)
        h_ref[...] = ((xv * r) * g_ref[...]).astype(BF16)

    return pl.pallas_call(
        body, grid=(t // rb,), in_specs=[_row_spec(rb, d), _full_spec((1, d))] + ([ANY] if dep is not None else []),
        out_specs=_row_spec(rb, d), out_shape=jax.ShapeDtypeStruct((t, d), BF16), compiler_params=_params(("parallel",)), name=name,
    )(x, g, *(() if dep is None else (dep,)))


def _rms_bwd(dh, x, g, dres, *, name, rb=256):
    t, d = x.shape
    rb = _tile(t, rb, 16)

    def body(dh_ref, x_ref, g_ref, dres_ref, dx_ref, dxb_ref, dg_ref):
        xv = x_ref[...]
        r = lax.rsqrt(jnp.mean(xv * xv, axis=-1, keepdims=True) + RMS_EPS)
        xhat = xv * r
        dh_v = dh_ref[...].astype(F32)
        dxhat = dh_v * g_ref[...]
        m = jnp.mean(dxhat * xhat, axis=-1, keepdims=True)
        dx = dres_ref[...] + r * (dxhat - xhat * m)
        dx_ref[...] = dx
        dxb_ref[...] = dx.astype(BF16)

        @pl.when(pl.program_id(0) == 0)
        def _():
            dg_ref[...] = jnp.zeros_like(dg_ref)

        dg_ref[0:1, :] += jnp.sum(dh_v * xhat, axis=0, keepdims=True)

    return pl.pallas_call(
        body, grid=(t // rb,),
        in_specs=[_row_spec(rb, d), _row_spec(rb, d), _full_spec((1, d)), _row_spec(rb, d)],
        out_specs=[_row_spec(rb, d), _row_spec(rb, d), _full_spec((8, d))],
        out_shape=[jax.ShapeDtypeStruct((t, d), F32), jax.ShapeDtypeStruct((t, d), BF16), jax.ShapeDtypeStruct((8, d), F32)],
        compiler_params=_params(("arbitrary",)), name=name,
    )(dh, x, g, dres)


def _final(x, tgt, g, *, name, rb=256):
    t, d = x.shape
    rb = _tile(t, rb, 8)
    inv_d = 1.0 / d

    def body(x_ref, t_ref, g_ref, l_ref, dx_ref, dxb_ref, dg_ref):
        xv = x_ref[...]
        gv = g_ref[...]
        r = lax.rsqrt(jnp.mean(xv * xv, axis=-1, keepdims=True) + RMS_EPS)
        xhat = xv * r
        e = xhat * gv - t_ref[...]
        dy = e * inv_d
        dxhat = dy * gv
        m = jnp.mean(dxhat * xhat, axis=-1, keepdims=True)
        dx = r * (dxhat - xhat * m)
        dx_ref[...] = dx
        dxb_ref[...] = dx.astype(BF16)

        @pl.when(pl.program_id(0) == 0)
        def _():
            l_ref[...] = jnp.zeros_like(l_ref)
            dg_ref[...] = jnp.zeros_like(dg_ref)

        l_ref[0:1, :] += jnp.sum(e * e, axis=0, keepdims=True) * (0.5 * inv_d)
        dg_ref[0:1, :] += jnp.sum(dy * xhat, axis=0, keepdims=True)

    return pl.pallas_call(
        body, grid=(t // rb,),
        in_specs=[_row_spec(rb, d), _row_spec(rb, d), _full_spec((1, d))],
        out_specs=[_full_spec((8, d)), _row_spec(rb, d), _row_spec(rb, d), _full_spec((8, d))],
        out_shape=[jax.ShapeDtypeStruct((8, d), F32), jax.ShapeDtypeStruct((t, d), F32),
                   jax.ShapeDtypeStruct((t, d), BF16), jax.ShapeDtypeStruct((8, d), F32)],
        compiler_params=_params(("arbitrary",)), name=name,
    )(x, tgt, g)


def _a_mid_fwd(bcx, wconv, *, name, rb=256, cw=512):
    t, d3 = bcx.shape
    d = d3 // 3
    rb, cw = _tile(t, rb, 16), _tile(d, cw)

    def body(cur_ref, prev_ref, w_ref, y_ref):
        first = pl.program_id(0) == 0

        def f32(ref, cols):
            return ref[:, cols].astype(F32)

        for c0 in range(0, d, cw):
            cs = slice(c0, c0 + cw)
            gc, xs = slice(d + c0, d + c0 + cw), slice(2 * d + c0, 2 * d + c0 + cw)
            p_prev = jnp.where(first, 0.0, (f32(prev_ref, gc) * f32(prev_ref, xs))[BF16_HALO - HALO:])
            e = jnp.concatenate([p_prev, f32(cur_ref, gc) * f32(cur_ref, xs)], axis=0)
            w = w_ref[:, cs]
            q = w[0:1] * _shift(e, 2) + w[1:2] * _shift(e, 1) + w[2:3] * e
            y_ref[:, cs] = (f32(cur_ref, cs) * q[HALO:]).astype(BF16)

    return pl.pallas_call(
        body, grid=(t // rb,),
        in_specs=[_row_spec(rb, d3), _prev_spec16(rb, d3), _full_spec((3, d))], out_specs=_row_spec(rb, d),
        out_shape=jax.ShapeDtypeStruct((t, d), BF16), compiler_params=_params(("parallel",)), name=name,
    )(bcx, bcx, wconv)


def _a_mid_bwd(bcx, dy, wconv, *, name, rb=256, cw=512):
    t, d3 = bcx.shape
    d = d3 // 3
    rb, cw = _tile(t, rb, 16), _tile(d, cw)

    def body(cur_ref, prev_ref, next_ref, dy_ref, dyn_ref, w_ref, o_ref, dw_ref):
        i = pl.program_id(0)
        first, last = i == 0, i == pl.num_programs(0) - 1

        @pl.when(first)
        def _():
            dw_ref[...] = jnp.zeros_like(dw_ref)

        def f32(ref, cols):
            return ref[:, cols].astype(F32)

        for c0 in range(0, d, cw):
            cs = slice(c0, c0 + cw)
            gc, xs = slice(d + c0, d + c0 + cw), slice(2 * d + c0, 2 * d + c0 + cw)
            zeros = jnp.zeros((HALO, cw), F32)
            gb_c, gc_c, xs_c = f32(cur_ref, cs), f32(cur_ref, gc), f32(cur_ref, xs)
            p_prev = jnp.where(first, 0.0, (f32(prev_ref, gc) * f32(prev_ref, xs))[BF16_HALO - HALO:])
            e = jnp.concatenate([p_prev, gc_c * xs_c, zeros], axis=0)
            dq_next = jnp.where(last, 0.0, dyn_ref[:, cs] * f32(next_ref, cs)[:HALO])
            dy_c = dy_ref[:, cs]
            dq = jnp.concatenate([zeros, dy_c * gb_c, dq_next], axis=0)
            w = w_ref[:, cs]
            e1, e2 = _shift(e, 1), _shift(e, 2)
            q = w[0:1] * e2 + w[1:2] * e1 + w[2:3] * e
            dp = (w[2:3] * dq + w[1:2] * _shift(dq, -1) + w[0:1] * _shift(dq, -2))[HALO:HALO + rb]
            o_ref[:, cs] = (dy_c * q[HALO:HALO + rb]).astype(BF16)
            o_ref[:, gc] = (dp * xs_c).astype(BF16)
            o_ref[:, xs] = (dp * gc_c).astype(BF16)
            dq_c = dq[HALO:HALO + rb]
            dw_ref[0:1, cs] += jnp.sum(dq_c * e2[HALO:HALO + rb], axis=0, keepdims=True)
            dw_ref[1:2, cs] += jnp.sum(dq_c * e1[HALO:HALO + rb], axis=0, keepdims=True)
            dw_ref[2:3, cs] += jnp.sum(dq_c * e[HALO:HALO + rb], axis=0, keepdims=True)

    return pl.pallas_call(
        body, grid=(t // rb,),
        in_specs=[_row_spec(rb, d3), _prev_spec16(rb, d3), _next_spec16(rb, d3, t), _row_spec(rb, d), _next_spec(rb, d, t),
                  _full_spec((3, d))],
        out_specs=[_row_spec(rb, d3), _full_spec((8, d))],
        out_shape=[jax.ShapeDtypeStruct((t, d3), BF16), jax.ShapeDtypeStruct((8, d), F32)],
        compiler_params=_params(("arbitrary",)), name=name,
    )(bcx, bcx, bcx, dy, dy, wconv)


def _ffn_mid_fwd(up, wconv, bconv, *, name, rb=128, cw=512):
    t, f2 = up.shape
    f = f2 // 2
    rb, cw = _tile(t, rb, 16), _tile(f, cw)

    def body(cur_ref, prev_ref, w_ref, b_ref, act_ref, conv_ref):
        first = pl.program_id(0) == 0

        def conv(cols):
            prev = prev_ref[:, cols].astype(F32)[BF16_HALO - HALO:]
            e = jnp.concatenate([jnp.where(first, 0.0, prev), cur_ref[:, cols].astype(F32)], axis=0)
            w = w_ref[:, cols]
            out = (w[0:1] * _shift(e, 2) + w[1:2] * _shift(e, 1) + w[2:3] * e + b_ref[:, cols])[HALO:]
            conv_ref[:, cols] = out.astype(BF16)
            return out

        for c0 in range(0, f, cw):
            g = conv(slice(c0, c0 + cw))
            a = conv(slice(f + c0, f + c0 + cw))
            act_ref[:, c0:c0 + cw] = (g * _sigmoid(g) * a).astype(BF16)

    return pl.pallas_call(
        body, grid=(t // rb,),
        in_specs=[_row_spec(rb, f2), _prev_spec16(rb, f2), _full_spec((3, f2)), _full_spec((1, f2))],
        out_specs=[_row_spec(rb, f), _row_spec(rb, f2)],
        out_shape=[jax.ShapeDtypeStruct((t, f), BF16), jax.ShapeDtypeStruct((t, f2), BF16)],
        compiler_params=_params(("parallel",)), name=name,
    )(up, up, wconv, bconv)


def _ffn_mid_bwd(up, conv, dact, wconv, *, name, rb=128, cw=512):
    t, f2 = up.shape
    f = f2 // 2
    rb, cw = _tile(t, rb, 16), _tile(f, cw)

    def body(up_ref, conv_ref, convn_ref, da_ref, dan_ref, w_ref, o_ref, dwb_ref):
        i = pl.program_id(0)
        last = i == pl.num_programs(0) - 1

        @pl.when(i == 0)
        def _():
            dwb_ref[...] = jnp.zeros_like(dwb_ref)

        def rows(cols):
            return jnp.concatenate([conv_ref[:, cols].astype(F32), convn_ref[:, cols].astype(F32)[0:HALO]], axis=0)

        def back(dc, cols):
            w = w_ref[:, cols]
            dc1, dc2 = _shift(dc, -1)[:rb], _shift(dc, -2)[:rb]
            dc0 = dc[:rb]
            o_ref[:, cols] = (w[2:3] * dc0 + w[1:2] * dc1 + w[0:1] * dc2).astype(BF16)
            u = up_ref[:, cols].astype(F32)
            ones = jnp.ones((8, rb), BF16)
            for k, prod in enumerate((dc2 * u, dc1 * u, dc0 * u, dc0)):
                dwb_ref[k:k + 1, cols] += jnp.dot(ones, prod.astype(BF16), preferred_element_type=F32)[0:1]

        for c0 in range(0, f, cw):
            gcols, acols = slice(c0, c0 + cw), slice(f + c0, f + c0 + cw)
            g, a = rows(gcols), rows(acols)
            da = jnp.concatenate([da_ref[:, gcols], jnp.where(last, 0.0, dan_ref[:, gcols])], axis=0)
            sg = _sigmoid(g)
            back(da * a * (sg * (1.0 + g * (1.0 - sg))), gcols)
            back(da * (g * sg), acols)

    return pl.pallas_call(
        body, grid=(t // rb,),
        in_specs=[_row_spec(rb, f2), _row_spec(rb, f2), _next_spec16(rb, f2, t), _row_spec(rb, f), _next_spec(rb, f, t),
                  _full_spec((3, f2))],
        out_specs=[_row_spec(rb, f2), _full_spec((8, f2))],
        out_shape=[jax.ShapeDtypeStruct((t, f2), BF16), jax.ShapeDtypeStruct((8, f2), F32)],
        compiler_params=_params(("arbitrary",)), name=name,
    )(up, conv, conv, dact, dact, wconv)


def _causal_mask():
    return lax.broadcasted_iota(jnp.int32, (CHUNK, CHUNK), 0) >= lax.broadcasted_iota(jnp.int32, (CHUNK, CHUNK), 1)


def _b_mid_fwd(zp, vnorm, ws, bs, *, name, rb=512):
    t, d2 = zp.shape
    d = d2 // 2
    c = d // GROUPS
    rb = _tile(t, rb, CHUNK)

    def body(zp_ref, gv_ref, ws_ref, bs_ref, ug_ref, vn_ref, gate_ref):
        v = _gelu(zp_ref[:, d:].astype(F32))
        rv = lax.rsqrt(jnp.mean(v * v, axis=-1, keepdims=True) + RMS_EPS)
        vn_ref[...] = ((v * rv) * gv_ref[...]).astype(BF16)
        mask = _causal_mask()
        for h in range(GROUPS):
            hc = slice(h * c, (h + 1) * c)
            wm = jnp.where(mask, ws_ref[h], 0.0).astype(BF16)
            bcol = jnp.broadcast_to(bs_ref[h:h + 1, :], (CHUNK, CHUNK)).T[:, 0:1]
            for n in range(rb // CHUNK):
                rows = slice(n * CHUNK, (n + 1) * CHUNK)
                gate_ref[rows, hc] = jnp.dot(wm, vn_ref[rows, hc], preferred_element_type=F32) + bcol
        ug_ref[...] = (_gelu(zp_ref[:, :d].astype(F32)) * gate_ref[...]).astype(BF16)

    return pl.pallas_call(
        body, grid=(t // rb,),
        in_specs=[_row_spec(rb, d2), _full_spec((1, d)), _full_spec((GROUPS, CHUNK, CHUNK)), _full_spec((GROUPS, CHUNK))],
        out_specs=_row_spec(rb, d), out_shape=jax.ShapeDtypeStruct((t, d), BF16),
        scratch_shapes=[pltpu.VMEM((rb, d), BF16), pltpu.VMEM((rb, d), F32)],
        compiler_params=_params(("parallel",)), name=name,
    )(zp, vnorm, ws, bs)


def _b_mid_bwd(zp, dug, vnorm, ws, bs, *, name, rb=512):
    t, d2 = zp.shape
    d = d2 // 2
    c = d // GROUPS
    rb = _tile(t, rb, CHUNK)

    def body(zp_ref, dug_ref, gv_ref, ws_ref, bs_ref, dzp_ref, dws_ref, dbs_ref, dgv_ref,
             vn_ref, gate_ref, dm_ref, dvn_ref, dbacc_ref):
        i = pl.program_id(0)

        @pl.when(i == 0)
        def _():
            dws_ref[...] = jnp.zeros_like(dws_ref)
            dgv_ref[...] = jnp.zeros_like(dgv_ref)
            dbacc_ref[...] = jnp.zeros_like(dbacc_ref)

        zu, zv = zp_ref[:, :d].astype(F32), zp_ref[:, d:].astype(F32)
        (u, u_grad), (v, v_grad) = _gelu_and_grad(zu), _gelu_and_grad(zv)
        rv = lax.rsqrt(jnp.mean(v * v, axis=-1, keepdims=True) + RMS_EPS)
        vhat = v * rv
        gv = gv_ref[...]
        vn_ref[...] = (vhat * gv).astype(BF16)
        dug_v = dug_ref[...].astype(F32)
        dm = dug_v * u
        dm_ref[...] = dm.astype(BF16)
        mask = _causal_mask()
        for h in range(GROUPS):
            hc = slice(h * c, (h + 1) * c)
            wm = jnp.where(mask, ws_ref[h], 0.0)
            wm_b, wmt_b = wm.astype(BF16), wm.T.astype(BF16)
            bcol = jnp.broadcast_to(bs_ref[h:h + 1, :], (CHUNK, CHUNK)).T[:, 0:1]
            dws_h = jnp.zeros((CHUNK, CHUNK), F32)
            dbs_h = jnp.zeros((CHUNK, c), F32)
            for n in range(rb // CHUNK):
                rows = slice(n * CHUNK, (n + 1) * CHUNK)
                vn_c, dm_c = vn_ref[rows, hc], dm_ref[rows, hc]
                gate_ref[rows, hc] = jnp.dot(wm_b, vn_c, preferred_element_type=F32) + bcol
                dws_h += lax.dot_general(dm_c, vn_c, (((1,), (1,)), ((), ())), preferred_element_type=F32)
                dvn_ref[rows, hc] = jnp.dot(wmt_b, dm_c, preferred_element_type=F32)
                dbs_h += dm[rows, hc]
            dws_ref[h] += dws_h
            dbacc_ref[h] += dbs_h
        du = dug_v * gate_ref[...]
        dvn = dvn_ref[...]
        dvhat = dvn * gv
        m = jnp.mean(dvhat * vhat, axis=-1, keepdims=True)
        dv = rv * (dvhat - vhat * m)
        dgv_ref[0:1, :] += jnp.sum(dvn * vhat, axis=0, keepdims=True)
        dzp_ref[:, :d] = (du * u_grad).astype(BF16)
        dzp_ref[:, d:] = (dv * v_grad).astype(BF16)

        @pl.when(i == pl.num_programs(0) - 1)
        def _():
            ones = jnp.ones((8, c), F32)
            for h in range(GROUPS):
                dws_ref[h] = jnp.where(mask, dws_ref[h], 0.0)
                row = lax.dot_general(ones, dbacc_ref[h], (((1,), (1,)), ((), ())),
                                      precision=lax.Precision.HIGHEST, preferred_element_type=F32)
                dbs_ref[h:h + 1, :] = row[0:1]

    return pl.pallas_call(
        body, grid=(t // rb,),
        in_specs=[_row_spec(rb, d2), _row_spec(rb, d), _full_spec((1, d)), _full_spec((GROUPS, CHUNK, CHUNK)),
                  _full_spec((GROUPS, CHUNK))],
        out_specs=[_row_spec(rb, d2), _full_spec((GROUPS, CHUNK, CHUNK)), _full_spec((GROUPS, CHUNK)), _full_spec((8, d))],
        out_shape=[jax.ShapeDtypeStruct((t, d2), BF16), jax.ShapeDtypeStruct((GROUPS, CHUNK, CHUNK), F32),
                   jax.ShapeDtypeStruct((GROUPS, CHUNK), F32), jax.ShapeDtypeStruct((8, d), F32)],
        scratch_shapes=[pltpu.VMEM((rb, d), BF16), pltpu.VMEM((rb, d), F32), pltpu.VMEM((rb, d), BF16),
                        pltpu.VMEM((rb, d), F32), pltpu.VMEM((GROUPS, CHUNK, c), F32)],
        compiler_params=_params(("arbitrary",)), name=name,
    )(zp, dug, vnorm, ws, bs)


def _cast_into_full(w, layer, kind, place, *, name, dep=None, rb=256):
    _, r, c = w.shape
    rb = _tile(r, rb, 16)
    nrb = r // rb
    full = (r, c * N_CHIPS) if kind == "col" else (r * N_CHIPS, c)

    def body(place_ref, w_ref, *rest):
        rest[-1][...] = w_ref[...].astype(BF16)

    def o_index(i, place):
        return (i, place[0]) if kind == "col" else (i + place[0] * nrb, 0)

    in_specs = [pl.BlockSpec((None, rb, c), lambda i, place: (layer, i, 0))] + ([ANY] if dep is not None else [])
    return pl.pallas_call(
        body,
        grid_spec=pltpu.PrefetchScalarGridSpec(num_scalar_prefetch=1, grid=(nrb,), in_specs=in_specs,
                                               out_specs=pl.BlockSpec((rb, c), o_index)),
        out_shape=jax.ShapeDtypeStruct(full, BF16), compiler_params=_params(("parallel",)), name=name,
    )(place, w, *(() if dep is None else (dep,)))


def _adamw_layer(w, g, m, v, layer, prev, *, name, rb=128):
    _, r, c = w.shape
    rb = _tile(r, rb, 8)
    c1 = 1.0 - ADAM_B1 ** ADAM_STEP
    c2 = 1.0 - ADAM_B2 ** ADAM_STEP

    def body(w_ref, g_ref, m_ref, v_ref, *rest):
        go_ref, d_ref, nm_ref, nv_ref = rest[-4:]
        gv = g_ref[...]
        nm = ADAM_B1 * m_ref[...] + (1.0 - ADAM_B1) * gv
        nv = ADAM_B2 * v_ref[...] + (1.0 - ADAM_B2) * (gv * gv)
        go_ref[...] = gv
        nm_ref[...] = nm
        nv_ref[...] = nv
        d_ref[...] = -ADAM_LR * ((nm / c1) / (jnp.sqrt(nv / c2) + ADAM_EPS) + ADAM_WD * w_ref[...])

    lay = pl.BlockSpec((None, rb, c), lambda i: (layer, i, 0))
    return pl.pallas_call(
        body, grid=(r // rb,), in_specs=[lay, _row_spec(rb, c), lay, lay] + ([ANY] * 4 if prev else []), out_specs=[lay] * 4,
        out_shape=[jax.ShapeDtypeStruct(w.shape, F32)] * 4, input_output_aliases={4 + k: k for k in range(4)} if prev else {},
        compiler_params=_params(("parallel",)), name=name,
    )(w, g, m, v, *(prev or ()))


HBM = pl.BlockSpec(memory_space=pltpu.HBM)
SEM = pl.BlockSpec(memory_space=pltpu.SEMAPHORE)
SIDE_EFFECT = pltpu.SideEffectType.DATAFLOW_SIDE_EFFECTING


def _place():
    x, y, c = lax.axis_index("x"), lax.axis_index("y"), lax.axis_index("c")
    chips = [(1 - x, y), (x, 1 - y), (1 - x, 1 - y)]
    return x, y, c, 2 * x + y, chips


def _half(ref, kind, c):
    r, w = ref.shape
    if kind == "col":
        return ref.at[pl.ds(pl.multiple_of(c * (r // 2), 8), r // 2), :]
    return ref.at[:, pl.ds(pl.multiple_of(c * (w // 2), 128), w // 2)]


def _shard(ref, kind, s):
    r, w = ref.shape
    if kind == "col":
        return ref.at[:, pl.ds(pl.multiple_of(s * (w // N_CHIPS), 128), w // N_CHIPS)]
    return ref.at[pl.ds(pl.multiple_of(s * (r // N_CHIPS), 8), r // N_CHIPS), :]


def _remote(src, dst, send_sem, recv_sem, dev):
    return pltpu.make_async_remote_copy(src_ref=src, dst_ref=dst, send_sem=send_sem, recv_sem=recv_sem,
                                        device_id=dev, device_id_type=MESH)


def _start(name, bufs, plan, sem_shape, dep=None):
    n = len(bufs)
    n_in = n + (dep is not None)

    def body(*refs):
        sends, _ = plan(refs[:n], refs[n_in], refs[n_in + 1])
        for cp in sends:
            cp.start()
        refs[n_in + 2 + n][...] = jnp.zeros((8, 128), F32)

    dma = pltpu.SemaphoreType.DMA
    outs = pl.pallas_call(
        body, name=name,
        out_shape=(dma(sem_shape), dma(sem_shape), *[pltpu.HBM(b.shape, b.dtype) for b in bufs], jax.ShapeDtypeStruct((8, 128), F32)),
        in_specs=(HBM,) * n + ((ANY,) if dep is not None else ()),
        out_specs=(SEM, SEM) + (HBM,) * n + (pl.BlockSpec(memory_space=pltpu.VMEM),),
        input_output_aliases={i: i + 2 for i in range(n)},
        compiler_params=pltpu.CompilerParams(has_side_effects=SIDE_EFFECT),
    )(*[pltpu.with_memory_space_constraint(b, pltpu.HBM) for b in bufs], *(() if dep is None else (dep,)))
    return outs[0], outs[1], list(outs[2:2 + n]), outs[2 + n]


def _wait(name, started, plan, after):
    send, recv, bufs, _ = started
    n = len(bufs)

    def body(*refs):
        sends, recvs = plan(refs[:n], refs[n], refs[n + 1])
        for cp in sends:
            cp.wait_send()
        for cp in recvs:
            cp.wait_recv()

    return list(pl.pallas_call(
        body, name=name, out_shape=tuple(pltpu.HBM(b.shape, b.dtype) for b in bufs),
        in_specs=(HBM,) * n + (SEM, SEM, ANY), out_specs=(HBM,) * n, input_output_aliases={i: i for i in range(n)},
        compiler_params=pltpu.CompilerParams(has_side_effects=SIDE_EFFECT),
    )(*bufs, send, recv, after))


KINDS = ("col", "row")


def _gather_ici_plan(n_small):
    def plan(refs, send, recv):
        x, y, c, s, chips = _place()
        n = len(KINDS) + n_small
        sends, recvs = [], []
        for k, (px, py) in enumerate(chips):
            sp = 2 * px + py
            for a, kind in enumerate(KINDS):
                mine, theirs = _half(_shard(refs[a], kind, s), kind, c), _half(_shard(refs[a], kind, sp), kind, c)
                sends.append(_remote(mine, mine, send.at[k * n + a], recv.at[k * n + a], (px, py, c)))
                recvs.append(_remote(theirs, theirs, send.at[k * n + a], recv.at[k * n + a], (px, py, c)))
            for b in range(n_small):
                ref, sem = refs[len(KINDS) + b], k * n + len(KINDS) + b
                sends.append(_remote(ref.at[s], ref.at[s], send.at[sem], recv.at[sem], (px, py, c)))
                recvs.append(_remote(ref.at[sp], ref.at[sp], send.at[sem], recv.at[sem], (px, py, c)))
        return sends, recvs
    return plan


def _gather_d2d_plan(refs, send, recv):
    x, y, c, _, chips = _place()
    n = len(KINDS)
    sends, recvs = [], []
    for k, (px, py) in enumerate(chips):
        for a, kind in enumerate(KINDS):
            region, sem = _shard(refs[a], kind, 2 * px + py), k * n + a
            sends.append(_remote(_half(region, kind, c), _half(region, kind, c), send.at[sem], recv.at[sem], (x, y, 1 - c)))
            recvs.append(_remote(_half(region, kind, 1 - c), _half(region, kind, 1 - c), send.at[sem], recv.at[sem], (x, y, 1 - c)))
    return sends, recvs


def _swap_plan(refs, send, recv):
    x, y, c, _, _ = _place()
    n = len(KINDS)
    cps = [_remote(_half(refs[a], KINDS[a], 1 - c), refs[n + a], send.at[a], recv.at[a], (x, y, 1 - c)) for a in range(n)]
    return cps, cps


def _exchange_plan(refs, send, recv):
    x, y, c, _, chips = _place()
    n = len(KINDS)
    cps = []
    for k, (px, py) in enumerate(chips):
        for a in range(n):
            cps.append(_remote(_shard(refs[a], KINDS[a], 2 * px + py), refs[n + a].at[k], send.at[k * n + a], recv.at[k * n + a],
                               (px, py, c)))
    return cps, cps


def _share_plan(refs, send, recv):
    x, y, c, _, _ = _place()
    sends = [_remote(_half(refs[a], KINDS[a], c), _half(refs[a], KINDS[a], c), send.at[a], recv.at[a], (x, y, 1 - c))
             for a in range(len(KINDS))]
    recvs = [_remote(_half(refs[a], KINDS[a], 1 - c), _half(refs[a], KINDS[a], 1 - c), send.at[a], recv.at[a], (x, y, 1 - c))
             for a in range(len(KINDS))]
    return sends, recvs


def _spread_plan(refs, send, recv):
    packed, slots = refs
    x, y, c, _, _ = _place()
    sends, recvs = [], []
    for k in range(1, 8):
        px, py, pc = x ^ (k >> 2), y ^ ((k >> 1) & 1), c ^ (k & 1)
        sends.append(_remote(packed, slots.at[4 * x + 2 * y + c], send.at[k - 1], recv.at[k - 1], (px, py, pc)))
        recvs.append(_remote(packed, slots.at[4 * px + 2 * py + pc], send.at[k - 1], recv.at[k - 1], (px, py, pc)))
    return sends, recvs


def _half_index(kind, nblk):
    def index(i, j, place):
        return (i + place[1] * nblk[0], j) if kind == "col" else (i, j + place[1] * nblk[1])
    return index


def _chip_partial(g, other, kind, place, *, name):
    hr, hc = other.shape
    col = kind == "col"
    rb = _tile(hr if col else hr // N_CHIPS, 512, 16)
    cb = _tile(hc // N_CHIPS if col else hc, 1408)
    nblk = (hr // rb, hc // cb)
    own = (nblk[1] if col else nblk[0]) // N_CHIPS

    def block(i, j, place):
        s = place[0]
        return (i, j + jnp.where(j >= s * own, own, 0)) if col else (i + jnp.where(i >= s * own, own, 0), j)

    def body(place_ref, g_ref, o_ref, p_ref):
        p_ref[...] = (g_ref[...].astype(F32) + o_ref[...].astype(F32)).astype(BF16)

    plain = pl.BlockSpec((rb, cb), block)
    in_half = pl.BlockSpec((rb, cb), lambda i, j, place: _half_index(kind, nblk)(*block(i, j, place), place))
    grid = (nblk[0], nblk[1] - own) if col else (nblk[0] - own, nblk[1])
    return pl.pallas_call(
        body,
        grid_spec=pltpu.PrefetchScalarGridSpec(num_scalar_prefetch=1, grid=grid, in_specs=[in_half, plain], out_specs=plain),
        out_shape=jax.ShapeDtypeStruct((hr, hc), BF16), compiler_params=_params(("parallel", "parallel")), name=name,
    )(place, g, other)


def _reduce_half(g, other, recv, kind, place, *, name):
    _, pr, pc = recv.shape
    rb, cb = _tile(pr, 512, 16), _tile(pc, 1024)
    nblk = (pr // rb, pc // cb)
    full = (pr * 2, pc) if kind == "col" else (pr, pc * 2)

    def g_index(i, j, place):
        s, c = place[0], place[1]
        return (i + c * nblk[0], j + s * nblk[1]) if kind == "col" else (i + s * nblk[0], j + c * nblk[1])

    def o_index(i, j, place):
        return (i, j + place[0] * nblk[1]) if kind == "col" else (i + place[0] * nblk[0], j)

    def body(place_ref, g_ref, o_ref, r_ref, out_ref):
        acc = g_ref[...].astype(F32) + o_ref[...].astype(F32)
        for k in range(3):
            acc = acc + r_ref[k].astype(F32)
        out_ref[...] = acc

    return pl.pallas_call(
        body,
        grid_spec=pltpu.PrefetchScalarGridSpec(
            num_scalar_prefetch=1, grid=nblk,
            in_specs=[pl.BlockSpec((rb, cb), g_index), pl.BlockSpec((rb, cb), o_index),
                      pl.BlockSpec((3, rb, cb), lambda i, j, place: (0, i, j))],
            out_specs=pl.BlockSpec((rb, cb), _half_index(kind, nblk))),
        out_shape=jax.ShapeDtypeStruct(full, F32), compiler_params=_params(("parallel", "parallel")), name=name,
    )(place, g, other, recv)


def _sum_slots(packed, slots, me, *, name, rb=512):
    r, w = packed.shape
    rb = _tile(r, rb, 8)

    def body(me_ref, p_ref, s_ref, o_ref):
        acc = None
        for j in range(8):
            term = jnp.where(me_ref[0] == j, p_ref[...], s_ref[j])
            acc = term if acc is None else acc + term
        o_ref[...] = acc

    return pl.pallas_call(
        body,
        grid_spec=pltpu.PrefetchScalarGridSpec(
            num_scalar_prefetch=1, grid=(r // rb,),
            in_specs=[pl.BlockSpec((rb, w), lambda i, me: (i, 0)), pl.BlockSpec((8, rb, w), lambda i, me: (0, i, 0))],
            out_specs=pl.BlockSpec((rb, w), lambda i, me: (i, 0))),
        out_shape=jax.ShapeDtypeStruct((r, w), F32), compiler_params=_params(("parallel",)), name=name,
    )(me, packed, slots)


def _half_shape(a, kind):
    return (a.shape[0] // 2, a.shape[1]) if kind == "col" else (a.shape[0], a.shape[1] // 2)


def _rs_swap(tag, grads):
    others = [lax.empty(_half_shape(g, k), g.dtype) for g, k in zip(grads, KINDS)]
    return _start(f"rs_{tag}_swap", list(grads) + others, _swap_plan, (len(KINDS),))


def _rs_exchange(tag, swapped, place, after):
    bufs = _wait(f"rs_{tag}_swap_wait", swapped, _swap_plan, after)
    n = len(KINDS)
    grads, others = bufs[:n], bufs[n:]
    parts = [_chip_partial(g, o, k, place, name=f"rs_{tag}_partial_{k}") for g, o, k in zip(grads, others, KINDS)]
    lands = []
    for p, k in zip(parts, KINDS):
        piece = (p.shape[0], p.shape[1] // N_CHIPS) if k == "col" else (p.shape[0] // N_CHIPS, p.shape[1])
        lands.append(lax.empty((3,) + piece, p.dtype))
    return _start(f"rs_{tag}_exchange", parts + lands, _exchange_plan, (3 * n,)), grads, others


def _rs_share(tag, exchanged, place, after):
    started, grads, others = exchanged
    n = len(KINDS)
    recvs = _wait(f"rs_{tag}_exchange_wait", started, _exchange_plan, after)[n:]
    halves = [_reduce_half(g, o, r, k, place, name=f"rs_{tag}_reduce_{k}") for g, o, r, k in zip(grads, others, recvs, KINDS)]
    return _start(f"rs_{tag}_share", halves, _share_plan, (n,))


def _rs_finish(tag, shared, after):
    return _wait(f"rs_{tag}_share_wait", shared, _share_plan, after)


def _spread(tag, parts, dep):
    rows = [p.reshape(-1, 128) for p in parts]
    n = sum(r.shape[0] for r in rows)
    rows.append(jnp.zeros(((-n) % 512, 128), F32))
    packed = jnp.concatenate(rows, axis=0)
    return _start(f"small_{tag}_spread", [packed, lax.empty((8,) + packed.shape, F32)], _spread_plan, (7,), dep=dep)


def _spread_sum(tag, started, parts, me, after):
    packed, slots = _wait(f"small_{tag}_spread_wait", started, _spread_plan, after)
    total = _sum_slots(packed, slots, me, name=f"small_{tag}_sum")
    out, row = [], 0
    for p in parts:
        n = p.size // 128
        out.append(total[row:row + n].reshape(p.shape))
        row += n
    return out


def _ffn_fwd(x, h, w_up, conv_w, conv_b, w_down, tag, tm_up=1024):
    up = _mm(h, w_up, tm=tm_up, out_dtype=BF16, name=f"ffn{tag}_up")
    act, conv = _ffn_mid_fwd(up, conv_w, conv_b, name=f"ffn{tag}_mid")
    x_out = _mm(act, w_down, res=x, tk=2816, name=f"ffn{tag}_down")
    return x_out, (up, conv, act)


def kernel(x, a_norm, a_in, a_conv, a_out, b_norm, b_in, b_vnorm, b_ws, b_bs, b_out, f_norm, f_up, f_conv_w, f_conv_b, f_down, final_norm, loss_target, m_a_norm, m_a_in, m_a_conv, m_a_out, m_b_norm, m_b_in, m_b_vnorm, m_b_ws, m_b_bs, m_b_out, m_f_norm, m_f_up, m_f_conv_w, m_f_conv_b, m_f_down, m_final_norm, v_a_norm, v_a_in, v_a_conv, v_a_out, v_b_norm, v_b_in, v_b_vnorm, v_b_ws, v_b_bs, v_b_out, v_f_norm, v_f_up, v_f_conv_w, v_f_conv_b, v_f_down, v_final_norm):
    t, d = x.shape[1], x.shape[2]
    f2 = f_up.shape[2] * N_CHIPS
    x0, tgt = x.reshape(t, d), loss_target.reshape(t, d)
    ax, ay, ac = lax.axis_index("x"), lax.axis_index("y"), lax.axis_index("c")
    s = 2 * ax + ay
    place = jnp.stack([s, ac]).astype(jnp.int32)
    me = (4 * ax + 2 * ay + ac).astype(jnp.int32).reshape(1)

    def stacked(a):
        return lax.dynamic_update_index_in_dim(jnp.zeros((N_CHIPS,) + a.shape, F32), a, s, 0)

    def cast_pair(tag, w_in, w_out, layer, dep):
        return [_cast_into_full(w_in, layer, "col", place, name=f"cast_{tag}_in", dep=dep),
                _cast_into_full(w_out, layer, "row", place, name=f"cast_{tag}_out", dep=dep)]

    def gather_start(tag, fulls, small, dep):
        return _start(f"ag_{tag}_ici", fulls + small, _gather_ici_plan(len(small)), (3 * (2 + len(small)),), dep=dep)

    def gather_forward(tag, started, n_small, after):
        bufs = _wait(f"ag_{tag}_ici_wait", started, _gather_ici_plan(n_small), after)
        return _start(f"ag_{tag}_d2d", bufs[:2], _gather_d2d_plan, (3 * 2,)), bufs[2:]

    def gather_finish(tag, forwarded, after):
        return _wait(f"ag_{tag}_d2d_wait", forwarded, _gather_d2d_plan, after)

    small = [stacked(a_conv[0]), stacked(b_norm), stacked(b_vnorm), stacked(f_conv_w.reshape(2 * 3, -1))]
    ag_a = gather_start("a", cast_pair("a", a_in, a_out, 0, None), small, None)
    full_f0 = cast_pair("f0", f_up, f_down, 0, ag_a[3])
    full_b = cast_pair("b", b_in, b_out, 0, full_f0[1])
    full_f1 = cast_pair("f1", f_up, f_down, 1, full_b[1])

    def unshard(a):
        return jnp.transpose(a, (1, 0, 2)).reshape(a.shape[1], -1)

    ws, bs = b_ws[0], b_bs[0]

    h0 = _rms_fwd(x0, a_norm, dep=full_f1[1], name="a_norm")
    fw_a, (g_aconv, g_bnorm, g_bvnorm, g_fconv) = gather_forward("a", ag_a, 4, h0)
    ag_f0 = gather_start("f0", full_f0, [], fw_a[3])
    ag_b = gather_start("b", full_b, [], ag_f0[3])
    ag_f1 = gather_start("f1", full_f1, [], ag_b[3])
    w_ai, w_ao = gather_finish("a", fw_a, ag_f1[3])
    a_conv_f, b_norm_f, b_vnorm_f = unshard(g_aconv), unshard(g_bnorm), unshard(g_bvnorm)
    f_conv_f = unshard(g_fconv).reshape(2, 3, f2)
    bcx = _mm(h0, w_ai, tm=2048, out_dtype=BF16, name="a_in")
    y = _a_mid_fwd(bcx, a_conv_f, name="a_mid")
    x1 = _mm(y, w_ao, res=x0, tm=512, tn=2048, name="a_out")
    fw_f0, _ = gather_forward("f0", ag_f0, 0, x1)
    h1 = _rms_fwd(x1, f_norm[0:1], dep=fw_f0[3], name="ffn0_norm")
    w_up0, w_dn0 = gather_finish("f0", fw_f0, h1)
    x2, (up0, conv0, act0) = _ffn_fwd(x1, h1, w_up0, f_conv_f[0], f_conv_b[0:1], w_dn0, 0, tm_up=2048)
    fw_b, _ = gather_forward("b", ag_b, 0, up0)
    w_bi, w_bo = gather_finish("b", fw_b, act0)
    h2 = _rms_fwd(x2, b_norm_f, name="b_norm")
    zp = _mm(h2, w_bi, tm=2048, out_dtype=BF16, name="b_in")
    fw_f1, _ = gather_forward("f1", ag_f1, 0, zp)
    ug = _b_mid_fwd(zp, b_vnorm_f, ws, bs, name="b_mid")
    x3, h3 = _mm(ug, w_bo, res=x2, norm=f_norm[1:2], tm=512, tn=2048, name="b_out")
    w_up1, w_dn1 = gather_finish("f1", fw_f1, x3)
    x4, (up1, conv1, act1) = _ffn_fwd(x3, h3, w_up1, f_conv_f[1], f_conv_b[1:2], w_dn1, 1, tm_up=2048)
    loss_rows, dx4, dx4b, d_final = _final(x4, tgt, final_norm.reshape(1, d), name="final")

    d_dn1 = _mm(act1, dx4b, ta=True, tm=1408, out_dtype=BF16, name="ffn1_ddown")
    dact1 = _mm(dx4b, w_dn1, tb=True, tn=512, tm=2048, name="ffn1_dact")
    dup1, d_fwb1 = _ffn_mid_bwd(up1, conv1, dact1, f_conv_f[1], name="ffn1_mid_bwd")
    d_up1 = _mm(h3, dup1, ta=True, out_dtype=BF16, tk=4096, name="ffn1_dup")
    sw_f1 = _rs_swap("f1", [d_up1, d_dn1])
    dh3 = _mm(dup1, w_up1, tb=True, tk=2816, out_dtype=BF16, dep=sw_f1[3], name="ffn1_dh")
    dx3, dx3b, d_fnorm1 = _rms_bwd(dh3, x3, f_norm[1:2], dx4, name="ffn1_norm_bwd")
    ex_f1 = _rs_exchange("f1", sw_f1, place, dx3)

    d_bo = _mm(ug, dx3b, ta=True, out_dtype=BF16, tk=4096, dep=ex_f1[0][3], name="b_dout")
    dug = _mm(dx3b, w_bo, tb=True, tm=512, tn=2048, out_dtype=BF16, name="b_dug")
    dzp, d_ws, d_bs, d_bvnorm = _b_mid_bwd(zp, dug, b_vnorm_f, ws, bs, name="b_mid_bwd")
    d_bi = _mm(h2, dzp, ta=True, out_dtype=BF16, tk=4096, name="b_din")
    sw_b = _rs_swap("b", [d_bi, d_bo])
    dh2 = _mm(dzp, w_bi, tb=True, tk=4096, out_dtype=BF16, dep=sw_b[3], name="b_dh")
    dx2, dx2b, d_bnorm = _rms_bwd(dh2, x2, b_norm_f, dx3, name="b_norm_bwd")
    ex_b = _rs_exchange("b", sw_b, place, dx2)

    d_dn0 = _mm(act0, dx2b, ta=True, tm=1408, out_dtype=BF16, dep=ex_b[0][3], name="ffn0_ddown")
    dact0 = _mm(dx2b, w_dn0, tb=True, tn=512, tm=2048, name="ffn0_dact")
    dup0, d_fwb0 = _ffn_mid_bwd(up0, conv0, dact0, f_conv_f[0], name="ffn0_mid_bwd")
    sh_f1 = _rs_share("f1", ex_f1, place, dup0)
    d_up0 = _mm(h1, dup0, ta=True, out_dtype=BF16, tk=4096, dep=sh_f1[3], name="ffn0_dup")
    sw_f0 = _rs_swap("f0", [d_up0, d_dn0])
    g_up1, g_dn1 = _rs_finish("f1", sh_f1, sw_f0[3])
    dh1 = _mm(dup0, w_up0, tb=True, tk=2816, out_dtype=BF16, dep=sw_f0[3], name="ffn0_dh")
    dx1, dx1b, d_fnorm0 = _rms_bwd(dh1, x1, f_norm[0:1], dx2, name="ffn0_norm_bwd")
    ex_f0 = _rs_exchange("f0", sw_f0, place, dx1)
    early = [jnp.concatenate([d_bnorm, d_bvnorm, d_fnorm0, d_fnorm1, d_final, loss_rows], axis=0),
             jnp.concatenate([d_fwb0, d_fwb1], axis=0), jnp.concatenate([d_ws.reshape(-1, CHUNK), d_bs], axis=0)]
    sp_early = _spread("early", early, ex_f0[0][3])
    sh_b = _rs_share("b", ex_b, place, sp_early[3])

    d_ao = _mm(y, dx1b, ta=True, out_dtype=BF16, tk=4096, dep=sh_b[3], name="a_dout")
    dyy = _mm(dx1b, w_ao, tb=True, tm=512, tn=2048, name="a_dy")
    dbcx, d_aconv = _a_mid_bwd(bcx, dyy, a_conv_f, name="a_mid_bwd")
    d_ai = _mm(h0, dbcx, ta=True, out_dtype=BF16, tk=4096, name="a_din")
    sw_a = _rs_swap("a", [d_ai, d_ao])
    g_bi, g_bo = _rs_finish("b", sh_b, sw_a[3])
    early_adamw = {"b_in": _adamw_layer(b_in, g_bi, m_b_in, v_b_in, 0, None, name="adamw_b_in"),
                   "b_out": _adamw_layer(b_out, g_bo, m_b_out, v_b_out, 0, None, name="adamw_b_out")}
    ex_a = _rs_exchange("a", sw_a, place, early_adamw["b_in"][1][0, :8, :128] + early_adamw["b_out"][1][0, :8, :128])
    dh0 = _mm(dbcx, w_ai, tb=True, tk=3072, out_dtype=BF16, dep=ex_a[0][3], name="a_dh")
    grad_x, _, d_anorm = _rms_bwd(dh0, x0, a_norm, dx1, name="a_norm_bwd")
    late = [jnp.concatenate([d_anorm, d_aconv], axis=0)]
    sp_late = _spread("late", late, ex_a[0][3])
    sh_f0 = _rs_share("f0", ex_f0, place, sp_late[3])
    sh_a = _rs_share("a", ex_a, place, sh_f0[3])
    g_up0, g_dn0 = _rs_finish("f0", sh_f0, sh_a[3])
    g_ai, g_ao = _rs_finish("a", sh_a, g_up0)
    r_a, r_b, r_c = _spread_sum("early", sp_early, early, me, g_ai)
    (r_l,) = _spread_sum("late", sp_late, late, me, r_a)

    loss = jnp.sum(r_a[40])
    cs, fs = d // N_CHIPS, f2 // N_CHIPS

    def mine(a, width):
        return lax.dynamic_slice_in_dim(a, s * width, width, axis=1)

    grads = {
        "a_norm": r_l[0:1], "a_conv": mine(r_l[8:11], cs), "b_norm": mine(r_a[0:1], cs), "b_vnorm": mine(r_a[8:9], cs),
        "f_norm": jnp.concatenate([r_a[16:17], r_a[24:25]], axis=0), "final_norm": r_a[32:33],
        "b_ws": r_c[:GROUPS * CHUNK], "b_bs": r_c[GROUPS * CHUNK:],
        "f_conv_w": jnp.concatenate([mine(r_b[0:3], fs), mine(r_b[8:11], fs)], axis=0),
        "f_conv_b": jnp.concatenate([r_b[3:4], r_b[11:12]], axis=0),
        "a_in": g_ai, "a_out": g_ao, "b_in": g_bi, "b_out": g_bo,
    }
    names = ["a_norm", "a_in", "a_conv", "a_out", "b_norm", "b_in", "b_vnorm", "b_ws", "b_bs", "b_out", "f_norm", "f_up",
             "f_conv_w", "f_conv_b", "f_down", "final_norm"]
    weights = dict(zip(names, [a_norm, a_in, a_conv, a_out, b_norm, b_in, b_vnorm, b_ws, b_bs, b_out, f_norm, f_up, f_conv_w,
                               f_conv_b, f_down, final_norm]))
    ms = dict(zip(names, [m_a_norm, m_a_in, m_a_conv, m_a_out, m_b_norm, m_b_in, m_b_vnorm, m_b_ws, m_b_bs, m_b_out, m_f_norm,
                          m_f_up, m_f_conv_w, m_f_conv_b, m_f_down, m_final_norm]))
    vs = dict(zip(names, [v_a_norm, v_a_in, v_a_conv, v_a_out, v_b_norm, v_b_in, v_b_vnorm, v_b_ws, v_b_bs, v_b_out, v_f_norm,
                          v_f_up, v_f_conv_w, v_f_conv_b, v_f_down, v_final_norm]))
    result = {}
    for n in names:
        w = weights[n]
        if n in ("f_up", "f_down"):
            g1, g0 = (g_up1, g_up0) if n == "f_up" else (g_dn1, g_dn0)
            first = _adamw_layer(w, g1, ms[n], vs[n], 1, None, name=f"adamw_{n}1")
            result[n] = _adamw_layer(w, g0, ms[n], vs[n], 0, tuple(first), name=f"adamw_{n}0")
            continue
        if n in early_adamw:
            result[n] = early_adamw[n]
            continue
        g2 = grads[n]
        as3d = (lambda a: a.reshape((1,) + g2.shape))
        result[n] = [o.reshape(w.shape) for o in _adamw_layer(as3d(w), g2, as3d(ms[n]), as3d(vs[n]), 0, None, name=f"adamw_{n}")]

    return (loss, grad_x.reshape(x.shape), *[result[n][0] for n in names], *[result[n][1] for n in names],
            *[result[n][2] for n in names], *[result[n][3] for n in names])
```

```python
import jax
import jax.numpy as jnp
from jax import lax
from jax.experimental import pallas as pl
from jax.experimental.pallas import tpu as pltpu

F32 = jnp.float32
BF16 = jnp.bfloat16
MESH = pl.DeviceIdType.MESH
ANY = pl.BlockSpec(memory_space=pl.ANY)

RMS_EPS = 1e-5
CHUNK = 128
GROUPS = 8
ADAM_LR, ADAM_B1, ADAM_B2, ADAM_EPS, ADAM_WD, ADAM_STEP = 0.001, 0.9, 0.999, 1e-08, 0.01, 10

N_CHIPS = 4
HALO = 8
BF16_HALO = 16
VMEM_LIMIT = 56 * 1024 * 1024
GELU_C = 0.7978845608028654
GELU_A = 0.044715


def _params(sem=None):
    return pltpu.CompilerParams(dimension_semantics=sem, vmem_limit_bytes=VMEM_LIMIT)


def _tile(dim, pref, quantum=128):
    if dim <= pref:
        return dim
    t = (pref // quantum) * quantum
    while t >= quantum:
        if dim % t == 0:
            return t
        t -= quantum
    return dim


def _mm(a, b, *, name, ta=False, tb=False, res=None, norm=None, dep=None, out_dtype=F32, tm=1024, tn=1024, tk=2048):
    (K, M) = a.shape if ta else a.shape[::-1]
    N = b.shape[0] if tb else b.shape[1]
    assert (b.shape[1] if tb else b.shape[0]) == K
    tm, tn, tk = _tile(M, tm), _tile(N, tn), _tile(K, tk)
    nk = K // tk
    assert norm is None or (tn == N and nk == 1)
    a_spec = pl.BlockSpec((tk, tm), lambda i, j, k: (k, i)) if ta else pl.BlockSpec((tm, tk), lambda i, j, k: (i, k))
    b_spec = pl.BlockSpec((tn, tk), lambda i, j, k: (j, k)) if tb else pl.BlockSpec((tk, tn), lambda i, j, k: (k, j))
    o_spec = pl.BlockSpec((tm, tn), lambda i, j, k: (i, j))
    dims = (((0 if ta else 1,), (1 if tb else 0,)), ((), ()))
    direct = out_dtype == F32
    n_in = 2 + (res is not None) + (norm is not None) + (dep is not None)

    def body(*refs):
        a_ref, b_ref = refs[0], refs[1]
        r_ref = refs[2] if res is not None else None
        g_ref = refs[2 + (res is not None)] if norm is not None else None
        o_ref = refs[n_in]
        acc_ref = o_ref if direct else refs[-1]
        part = lax.dot_general(a_ref[...], b_ref[...], dims, preferred_element_type=F32)
        if nk == 1:
            if r_ref is not None:
                part = part + r_ref[...]
            o_ref[...] = part.astype(o_ref.dtype)
            if g_ref is not None:
                r = lax.rsqrt(jnp.mean(part * part, axis=-1, keepdims=True) + RMS_EPS)
                refs[n_in + 1][...] = ((part * r) * g_ref[...]).astype(BF16)
            return
        k = pl.program_id(2)

        @pl.when(k == 0)
        def _():
            acc_ref[...] = part

        @pl.when(jnp.logical_and(k > 0, k < nk - 1))
        def _():
            acc_ref[...] += part

        @pl.when(k == nk - 1)
        def _():
            tot = acc_ref[...] + part
            if r_ref is not None:
                tot = tot + r_ref[...]
            o_ref[...] = tot.astype(o_ref.dtype)

    in_specs = ([a_spec, b_spec] + ([o_spec] if res is not None else [])
                + ([pl.BlockSpec((1, tn), lambda i, j, k: (0, j))] if norm is not None else []) + ([ANY] if dep is not None else []))
    args = (a, b) + tuple(x for x in (res, norm, dep) if x is not None)
    scratch = [] if (direct or nk == 1) else [pltpu.VMEM((tm, tn), F32)]
    out_shape = jax.ShapeDtypeStruct((M, N), out_dtype)
    return pl.pallas_call(
        body, grid=(M // tm, N // tn, nk), in_specs=in_specs, out_specs=[o_spec, o_spec] if norm is not None else o_spec,
        out_shape=[out_shape, jax.ShapeDtypeStruct((M, N), BF16)] if norm is not None else out_shape, scratch_shapes=scratch,
        compiler_params=_params(("parallel", "parallel", "arbitrary")), name=name,
    )(*args)


def _row_spec(rb, w):
    return pl.BlockSpec((rb, w), lambda i: (i, 0))


def _next_spec(rb, w, t):
    return pl.BlockSpec((HALO, w), lambda i: (jnp.minimum((i + 1) * (rb // HALO), t // HALO - 1), 0))


def _prev_spec16(rb, w):
    return pl.BlockSpec((BF16_HALO, w), lambda i: (jnp.maximum(i * (rb // BF16_HALO) - 1, 0), 0))


def _next_spec16(rb, w, t):
    return pl.BlockSpec((BF16_HALO, w), lambda i: (jnp.minimum((i + 1) * (rb // BF16_HALO), t // BF16_HALO - 1), 0))


def _full_spec(shape):
    return pl.BlockSpec(shape, lambda i: tuple(0 for _ in shape))


def _shift(e, s):
    return pltpu.roll(e, s % e.shape[0], 0)


def _gelu(x):
    return 0.5 * x * (1.0 + jnp.tanh(GELU_C * (x + GELU_A * x * x * x)))


def _gelu_and_grad(x):
    x2 = x * x
    th = jnp.tanh(GELU_C * (x + GELU_A * x2 * x))
    half = 0.5 * (1.0 + th)
    return x * half, half + 0.5 * x * (1.0 - th * th) * (GELU_C * (1.0 + 3.0 * GELU_A * x2))


def _sigmoid(x):
    return 1.0 / (1.0 + jnp.exp(-x))


def _rms_fwd(x, g, *, name, dep=None, rb=512):
    t, d = x.shape
    rb = _tile(t, rb, 8)

    def body(x_ref, g_ref, *rest):
        h_ref = rest[-1]
        xv = x_ref[...]
        r = lax.rsqrt(jnp.mean(xv * xv, axis=-1, keepdims=True) + RMS_EPS)
        h_ref[...] = ((xv * r) * g_ref[...]).astype(BF16)

    return pl.pallas_call(
        body, grid=(t // rb,), in_specs=[_row_spec(rb, d), _full_spec((1, d))] + ([ANY] if dep is not None else []),
        out_specs=_row_spec(rb, d), out_shape=jax.ShapeDtypeStruct((t, d), BF16), compiler_params=_params(("parallel",)), name=name,
    )(x, g, *(() if dep is None else (dep,)))


def _rms_bwd(dh, x, g, dres, *, name, rb=512):
    t, d = x.shape
    rb = _tile(t, rb, 16)

    def body(dh_ref, x_ref, g_ref, dres_ref, dx_ref, dxb_ref, dg_ref):
        xv = x_ref[...]
        r = lax.rsqrt(jnp.mean(xv * xv, axis=-1, keepdims=True) + RMS_EPS)
        xhat = xv * r
        dh_v = dh_ref[...].astype(F32)
        dxhat = dh_v * g_ref[...]
        m = jnp.mean(dxhat * xhat, axis=-1, keepdims=True)
        dx = dres_ref[...] + r * (dxhat - xhat * m)
        dx_ref[...] = dx
        dxb_ref[...] = dx.astype(BF16)

        @pl.when(pl.program_id(0) == 0)
        def _():
            dg_ref[...] = jnp.zeros_like(dg_ref)

        dg_ref[0:1, :] += jnp.sum(dh_v * xhat, axis=0, keepdims=True)

    return pl.pallas_call(
        body, grid=(t // rb,),
        in_specs=[_row_spec(rb, d), _row_spec(rb, d), _full_spec((1, d)), _row_spec(rb, d)],
        out_specs=[_row_spec(rb, d), _row_spec(rb, d), _full_spec((8, d))],
        out_shape=[jax.ShapeDtypeStruct((t, d), F32), jax.ShapeDtypeStruct((t, d), BF16), jax.ShapeDtypeStruct((8, d), F32)],
        compiler_params=_params(("arbitrary",)), name=name,
    )(dh, x, g, dres)


def _final(x, tgt, g, *, name, rb=512):
    t, d = x.shape
    rb = _tile(t, rb, 8)
    inv_d = 1.0 / d

    def body(x_ref, t_ref, g_ref, l_ref, dx_ref, dxb_ref, dg_ref):
        xv = x_ref[...]
        gv = g_ref[...]
        r = lax.rsqrt(jnp.mean(xv * xv, axis=-1, keepdims=True) + RMS_EPS)
        xhat = xv * r
        e = xhat * gv - t_ref[...]
        dy = e * inv_d
        dxhat = dy * gv
        m = jnp.mean(dxhat * xhat, axis=-1, keepdims=True)
        dx = r * (dxhat - xhat * m)
        dx_ref[...] = dx
        dxb_ref[...] = dx.astype(BF16)

        @pl.when(pl.program_id(0) == 0)
        def _():
            l_ref[...] = jnp.zeros_like(l_ref)
            dg_ref[...] = jnp.zeros_like(dg_ref)

        l_ref[0:1, :] += jnp.sum(e * e, axis=0, keepdims=True) * (0.5 * inv_d)
        dg_ref[0:1, :] += jnp.sum(dy * xhat, axis=0, keepdims=True)

    return pl.pallas_call(
        body, grid=(t // rb,),
        in_specs=[_row_spec(rb, d), _row_spec(rb, d), _full_spec((1, d))],
        out_specs=[_full_spec((8, d)), _row_spec(rb, d), _row_spec(rb, d), _full_spec((8, d))],
        out_shape=[jax.ShapeDtypeStruct((8, d), F32), jax.ShapeDtypeStruct((t, d), F32),
                   jax.ShapeDtypeStruct((t, d), BF16), jax.ShapeDtypeStruct((8, d), F32)],
        compiler_params=_params(("arbitrary",)), name=name,
    )(x, tgt, g)


def _a_mid_fwd(bcx, wconv, *, name, rb=256, cw=512):
    t, d3 = bcx.shape
    d = d3 // 3
    rb, cw = _tile(t, rb, 16), _tile(d, cw)

    def body(cur_ref, prev_ref, w_ref, y_ref):
        first = pl.program_id(0) == 0

        def f32(ref, cols):
            return ref[:, cols].astype(F32)

        for c0 in range(0, d, cw):
            cs = slice(c0, c0 + cw)
            gc, xs = slice(d + c0, d + c0 + cw), slice(2 * d + c0, 2 * d + c0 + cw)
            p_prev = jnp.where(first, 0.0, (f32(prev_ref, gc) * f32(prev_ref, xs))[BF16_HALO - HALO:])
            e = jnp.concatenate([p_prev, f32(cur_ref, gc) * f32(cur_ref, xs)], axis=0)
            w = w_ref[:, cs]
            q = w[0:1] * _shift(e, 2) + w[1:2] * _shift(e, 1) + w[2:3] * e
            y_ref[:, cs] = (f32(cur_ref, cs) * q[HALO:]).astype(BF16)

    return pl.pallas_call(
        body, grid=(t // rb,),
        in_specs=[_row_spec(rb, d3), _prev_spec16(rb, d3), _full_spec((3, d))], out_specs=_row_spec(rb, d),
        out_shape=jax.ShapeDtypeStruct((t, d), BF16), compiler_params=_params(("parallel",)), name=name,
    )(bcx, bcx, wconv)


def _a_mid_bwd(bcx, dy, wconv, *, name, rb=256, cw=512):
    t, d3 = bcx.shape
    d = d3 // 3
    rb, cw = _tile(t, rb, 16), _tile(d, cw)

    def body(cur_ref, prev_ref, next_ref, dy_ref, dyn_ref, w_ref, o_ref, dw_ref):
        i = pl.program_id(0)
        first, last = i == 0, i == pl.num_programs(0) - 1

        @pl.when(first)
        def _():
            dw_ref[...] = jnp.zeros_like(dw_ref)

        def f32(ref, cols):
            return ref[:, cols].astype(F32)

        for c0 in range(0, d, cw):
            cs = slice(c0, c0 + cw)
            gc, xs = slice(d + c0, d + c0 + cw), slice(2 * d + c0, 2 * d + c0 + cw)
            zeros = jnp.zeros((HALO, cw), F32)
            gb_c, gc_c, xs_c = f32(cur_ref, cs), f32(cur_ref, gc), f32(cur_ref, xs)
            p_prev = jnp.where(first, 0.0, (f32(prev_ref, gc) * f32(prev_ref, xs))[BF16_HALO - HALO:])
            e = jnp.concatenate([p_prev, gc_c * xs_c, zeros], axis=0)
            dq_next = jnp.where(last, 0.0, dyn_ref[:, cs] * f32(next_ref, cs)[:HALO])
            dy_c = dy_ref[:, cs]
            dq = jnp.concatenate([zeros, dy_c * gb_c, dq_next], axis=0)
            w = w_ref[:, cs]
            e1, e2 = _shift(e, 1), _shift(e, 2)
            q = w[0:1] * e2 + w[1:2] * e1 + w[2:3] * e
            dp = (w[2:3] * dq + w[1:2] * _shift(dq, -1) + w[0:1] * _shift(dq, -2))[HALO:HALO + rb]
            o_ref[:, cs] = (dy_c * q[HALO:HALO + rb]).astype(BF16)
            o_ref[:, gc] = (dp * xs_c).astype(BF16)
            o_ref[:, xs] = (dp * gc_c).astype(BF16)
            dq_c = dq[HALO:HALO + rb]
            dw_ref[0:1, cs] += jnp.sum(dq_c * e2[HALO:HALO + rb], axis=0, keepdims=True)
            dw_ref[1:2, cs] += jnp.sum(dq_c * e1[HALO:HALO + rb], axis=0, keepdims=True)
            dw_ref[2:3, cs] += jnp.sum(dq_c * e[HALO:HALO + rb], axis=0, keepdims=True)

    return pl.pallas_call(
        body, grid=(t // rb,),
        in_specs=[_row_spec(rb, d3), _prev_spec16(rb, d3), _next_spec16(rb, d3, t), _row_spec(rb, d), _next_spec(rb, d, t),
                  _full_spec((3, d))],
        out_specs=[_row_spec(rb, d3), _full_spec((8, d))],
        out_shape=[jax.ShapeDtypeStruct((t, d3), BF16), jax.ShapeDtypeStruct((8, d), F32)],
        compiler_params=_params(("arbitrary",)), name=name,
    )(bcx, bcx, bcx, dy, dy, wconv)


def _ffn_mid_fwd(up, wconv, bconv, *, name, rb=128, cw=512):
    t, f2 = up.shape
    f = f2 // 2
    rb, cw = _tile(t, rb, 16), _tile(f, cw)

    def body(cur_ref, prev_ref, w_ref, b_ref, act_ref, conv_ref):
        first = pl.program_id(0) == 0

        def conv(cols):
            prev = prev_ref[:, cols].astype(F32)[BF16_HALO - HALO:]
            e = jnp.concatenate([jnp.where(first, 0.0, prev), cur_ref[:, cols].astype(F32)], axis=0)
            w = w_ref[:, cols]
            out = (w[0:1] * _shift(e, 2) + w[1:2] * _shift(e, 1) + w[2:3] * e + b_ref[:, cols])[HALO:]
            conv_ref[:, cols] = out.astype(BF16)
            return out

        for c0 in range(0, f, cw):
            g = conv(slice(c0, c0 + cw))
            a = conv(slice(f + c0, f + c0 + cw))
            act_ref[:, c0:c0 + cw] = (g * _sigmoid(g) * a).astype(BF16)

    return pl.pallas_call(
        body, grid=(t // rb,),
        in_specs=[_row_spec(rb, f2), _prev_spec16(rb, f2), _full_spec((3, f2)), _full_spec((1, f2))],
        out_specs=[_row_spec(rb, f), _row_spec(rb, f2)],
        out_shape=[jax.ShapeDtypeStruct((t, f), BF16), jax.ShapeDtypeStruct((t, f2), BF16)],
        compiler_params=_params(("parallel",)), name=name,
    )(up, up, wconv, bconv)


def _ffn_mid_bwd(up, conv, dact, wconv, *, name, rb=128, cw=512):
    t, f2 = up.shape
    f = f2 // 2
    rb, cw = _tile(t, rb, 16), _tile(f, cw)

    def body(up_ref, conv_ref, convn_ref, da_ref, dan_ref, w_ref, o_ref, dwb_ref):
        i = pl.program_id(0)
        last = i == pl.num_programs(0) - 1

        @pl.when(i == 0)
        def _():
            dwb_ref[...] = jnp.zeros_like(dwb_ref)

        def rows(cols):
            return jnp.concatenate([conv_ref[:, cols].astype(F32), convn_ref[:, cols].astype(F32)[0:HALO]], axis=0)

        def back(dc, cols):
            w = w_ref[:, cols]
            dc1, dc2 = _shift(dc, -1)[:rb], _shift(dc, -2)[:rb]
            dc0 = dc[:rb]
            o_ref[:, cols] = (w[2:3] * dc0 + w[1:2] * dc1 + w[0:1] * dc2).astype(BF16)
            u = up_ref[:, cols].astype(F32)
            ones = jnp.ones((8, rb), BF16)
            for k, prod in enumerate((dc2 * u, dc1 * u, dc0 * u, dc0)):
                dwb_ref[k:k + 1, cols] += jnp.dot(ones, prod.astype(BF16), preferred_element_type=F32)[0:1]

        for c0 in range(0, f, cw):
            gcols, acols = slice(c0, c0 + cw), slice(f + c0, f + c0 + cw)
            g, a = rows(gcols), rows(acols)
            da = jnp.concatenate([da_ref[:, gcols], jnp.where(last, 0.0, dan_ref[:, gcols])], axis=0)
            sg = _sigmoid(g)
            back(da * a * (sg * (1.0 + g * (1.0 - sg))), gcols)
            back(da * (g * sg), acols)

    return pl.pallas_call(
        body, grid=(t // rb,),
        in_specs=[_row_spec(rb, f2), _row_spec(rb, f2), _next_spec16(rb, f2, t), _row_spec(rb, f), _next_spec(rb, f, t),
                  _full_spec((3, f2))],
        out_specs=[_row_spec(rb, f2), _full_spec((8, f2))],
        out_shape=[jax.ShapeDtypeStruct((t, f2), BF16), jax.ShapeDtypeStruct((8, f2), F32)],
        compiler_params=_params(("arbitrary",)), name=name,
    )(up, conv, conv, dact, dact, wconv)


def _causal_mask():
    return lax.broadcasted_iota(jnp.int32, (CHUNK, CHUNK), 0) >= lax.broadcasted_iota(jnp.int32, (CHUNK, CHUNK), 1)


def _b_mid_fwd(zp, vnorm, ws, bs, *, name, rb=512):
    t, d2 = zp.shape
    d = d2 // 2
    c = d // GROUPS
    rb = _tile(t, rb, CHUNK)

    def body(zp_ref, gv_ref, ws_ref, bs_ref, ug_ref, vn_ref, gate_ref):
        v = _gelu(zp_ref[:, d:].astype(F32))
        rv = lax.rsqrt(jnp.mean(v * v, axis=-1, keepdims=True) + RMS_EPS)
        vn_ref[...] = ((v * rv) * gv_ref[...]).astype(BF16)
        mask = _causal_mask()
        for h in range(GROUPS):
            hc = slice(h * c, (h + 1) * c)
            wm = jnp.where(mask, ws_ref[h], 0.0).astype(BF16)
            bcol = jnp.broadcast_to(bs_ref[h:h + 1, :], (CHUNK, CHUNK)).T[:, 0:1]
            for n in range(rb // CHUNK):
                rows = slice(n * CHUNK, (n + 1) * CHUNK)
                gate_ref[rows, hc] = jnp.dot(wm, vn_ref[rows, hc], preferred_element_type=F32) + bcol
        ug_ref[...] = (_gelu(zp_ref[:, :d].astype(F32)) * gate_ref[...]).astype(BF16)

    return pl.pallas_call(
        body, grid=(t // rb,),
        in_specs=[_row_spec(rb, d2), _full_spec((1, d)), _full_spec((GROUPS, CHUNK, CHUNK)), _full_spec((GROUPS, CHUNK))],
        out_specs=_row_spec(rb, d), out_shape=jax.ShapeDtypeStruct((t, d), BF16),
        scratch_shapes=[pltpu.VMEM((rb, d), BF16), pltpu.VMEM((rb, d), F32)],
        compiler_params=_params(("parallel",)), name=name,
    )(zp, vnorm, ws, bs)


def _b_mid_bwd(zp, dug, vnorm, ws, bs, *, name, rb=512):
    t, d2 = zp.shape
    d = d2 // 2
    c = d // GROUPS
    rb = _tile(t, rb, CHUNK)

    def body(zp_ref, dug_ref, gv_ref, ws_ref, bs_ref, dzp_ref, dws_ref, dbs_ref, dgv_ref,
             vn_ref, gate_ref, dm_ref, dvn_ref, dbacc_ref):
        i = pl.program_id(0)

        @pl.when(i == 0)
        def _():
            dws_ref[...] = jnp.zeros_like(dws_ref)
            dgv_ref[...] = jnp.zeros_like(dgv_ref)
            dbacc_ref[...] = jnp.zeros_like(dbacc_ref)

        zu, zv = zp_ref[:, :d].astype(F32), zp_ref[:, d:].astype(F32)
        (u, u_grad), (v, v_grad) = _gelu_and_grad(zu), _gelu_and_grad(zv)
        rv = lax.rsqrt(jnp.mean(v * v, axis=-1, keepdims=True) + RMS_EPS)
        vhat = v * rv
        gv = gv_ref[...]
        vn_ref[...] = (vhat * gv).astype(BF16)
        dug_v = dug_ref[...].astype(F32)
        dm = dug_v * u
        dm_ref[...] = dm.astype(BF16)
        mask = _causal_mask()
        for h in range(GROUPS):
            hc = slice(h * c, (h + 1) * c)
            wm = jnp.where(mask, ws_ref[h], 0.0)
            wm_b, wmt_b = wm.astype(BF16), wm.T.astype(BF16)
            bcol = jnp.broadcast_to(bs_ref[h:h + 1, :], (CHUNK, CHUNK)).T[:, 0:1]
            dws_h = jnp.zeros((CHUNK, CHUNK), F32)
            dbs_h = jnp.zeros((CHUNK, c), F32)
            for n in range(rb // CHUNK):
                rows = slice(n * CHUNK, (n + 1) * CHUNK)
                vn_c, dm_c = vn_ref[rows, hc], dm_ref[rows, hc]
                gate_ref[rows, hc] = jnp.dot(wm_b, vn_c, preferred_element_type=F32) + bcol
                dws_h += lax.dot_general(dm_c, vn_c, (((1,), (1,)), ((), ())), preferred_element_type=F32)
                dvn_ref[rows, hc] = jnp.dot(wmt_b, dm_c, preferred_element_type=F32)
                dbs_h += dm[rows, hc]
            dws_ref[h] += dws_h
            dbacc_ref[h] += dbs_h
        du = dug_v * gate_ref[...]
        dvn = dvn_ref[...]
        dvhat = dvn * gv
        m = jnp.mean(dvhat * vhat, axis=-1, keepdims=True)
        dv = rv * (dvhat - vhat * m)
        dgv_ref[0:1, :] += jnp.sum(dvn * vhat, axis=0, keepdims=True)
        dzp_ref[:, :d] = (du * u_grad).astype(BF16)
        dzp_ref[:, d:] = (dv * v_grad).astype(BF16)

        @pl.when(i == pl.num_programs(0) - 1)
        def _():
            ones = jnp.ones((8, c), F32)
            for h in range(GROUPS):
                dws_ref[h] = jnp.where(mask, dws_ref[h], 0.0)
                row = lax.dot_general(ones, dbacc_ref[h], (((1,), (1,)), ((), ())),
                                      precision=lax.Precision.HIGHEST, preferred_element_type=F32)
                dbs_ref[h:h + 1, :] = row[0:1]

    return pl.pallas_call(
        body, grid=(t // rb,),
        in_specs=[_row_spec(rb, d2), _row_spec(rb, d), _full_spec((1, d)), _full_spec((GROUPS, CHUNK, CHUNK)),
                  _full_spec((GROUPS, CHUNK))],
        out_specs=[_row_spec(rb, d2), _full_spec((GROUPS, CHUNK, CHUNK)), _full_spec((GROUPS, CHUNK)), _full_spec((8, d))],
        out_shape=[jax.ShapeDtypeStruct((t, d2), BF16), jax.ShapeDtypeStruct((GROUPS, CHUNK, CHUNK), F32),
                   jax.ShapeDtypeStruct((GROUPS, CHUNK), F32), jax.ShapeDtypeStruct((8, d), F32)],
        scratch_shapes=[pltpu.VMEM((rb, d), BF16), pltpu.VMEM((rb, d), F32), pltpu.VMEM((rb, d), BF16),
                        pltpu.VMEM((rb, d), F32), pltpu.VMEM((GROUPS, CHUNK, c), F32)],
        compiler_params=_params(("arbitrary",)), name=name,
    )(zp, dug, vnorm, ws, bs)


def _cast_into_full(w, layer, kind, place, *, name, dep=None, rb=256):
    _, r, c = w.shape
    rb = _tile(r, rb, 16)
    nrb = r // rb
    full = (r, c * N_CHIPS) if kind == "col" else (r * N_CHIPS, c)

    def body(place_ref, w_ref, *rest):
        rest[-1][...] = w_ref[...].astype(BF16)

    def o_index(i, place):
        return (i, place[0]) if kind == "col" else (i + place[0] * nrb, 0)

    in_specs = [pl.BlockSpec((None, rb, c), lambda i, place: (layer, i, 0))] + ([ANY] if dep is not None else [])
    return pl.pallas_call(
        body,
        grid_spec=pltpu.PrefetchScalarGridSpec(num_scalar_prefetch=1, grid=(nrb,), in_specs=in_specs,
                                               out_specs=pl.BlockSpec((rb, c), o_index)),
        out_shape=jax.ShapeDtypeStruct(full, BF16), compiler_params=_params(("parallel",)), name=name,
    )(place, w, *(() if dep is None else (dep,)))


def _adamw_layer(w, g, m, v, layer, prev, *, name, rb=128):
    _, r, c = w.shape
    rb = _tile(r, rb, 8)
    c1 = 1.0 - ADAM_B1 ** ADAM_STEP
    c2 = 1.0 - ADAM_B2 ** ADAM_STEP

    def body(w_ref, g_ref, m_ref, v_ref, *rest):
        go_ref, d_ref, nm_ref, nv_ref = rest[-4:]
        gv = g_ref[...]
        nm = ADAM_B1 * m_ref[...] + (1.0 - ADAM_B1) * gv
        nv = ADAM_B2 * v_ref[...] + (1.0 - ADAM_B2) * (gv * gv)
        go_ref[...] = gv
        nm_ref[...] = nm
        nv_ref[...] = nv
        d_ref[...] = -ADAM_LR * ((nm / c1) / (jnp.sqrt(nv / c2) + ADAM_EPS) + ADAM_WD * w_ref[...])

    lay = pl.BlockSpec((None, rb, c), lambda i: (layer, i, 0))
    return pl.pallas_call(
        body, grid=(r // rb,), in_specs=[lay, _row_spec(rb, c), lay, lay] + ([ANY] * 4 if prev else []), out_specs=[lay] * 4,
        out_shape=[jax.ShapeDtypeStruct(w.shape, F32)] * 4, input_output_aliases={4 + k: k for k in range(4)} if prev else {},
        compiler_params=_params(("parallel",)), name=name,
    )(w, g, m, v, *(prev or ()))


HBM = pl.BlockSpec(memory_space=pltpu.HBM)
SEM = pl.BlockSpec(memory_space=pltpu.SEMAPHORE)
SIDE_EFFECT = pltpu.SideEffectType.DATAFLOW_SIDE_EFFECTING


def _place():
    x, y, c = lax.axis_index("x"), lax.axis_index("y"), lax.axis_index("c")
    chips = [(1 - x, y), (x, 1 - y), (1 - x, 1 - y)]
    return x, y, c, 2 * x + y, chips


def _half(ref, kind, c):
    r, w = ref.shape
    if kind == "col":
        return ref.at[pl.ds(pl.multiple_of(c * (r // 2), 8), r // 2), :]
    return ref.at[:, pl.ds(pl.multiple_of(c * (w // 2), 128), w // 2)]


def _shard(ref, kind, s):
    r, w = ref.shape
    if kind == "col":
        return ref.at[:, pl.ds(pl.multiple_of(s * (w // N_CHIPS), 128), w // N_CHIPS)]
    return ref.at[pl.ds(pl.multiple_of(s * (r // N_CHIPS), 8), r // N_CHIPS), :]


def _remote(src, dst, send_sem, recv_sem, dev):
    return pltpu.make_async_remote_copy(src_ref=src, dst_ref=dst, send_sem=send_sem, recv_sem=recv_sem,
                                        device_id=dev, device_id_type=MESH)


def _start(name, bufs, plan, sem_shape, dep=None):
    n = len(bufs)
    n_in = n + (dep is not None)

    def body(*refs):
        sends, _ = plan(refs[:n], refs[n_in], refs[n_in + 1])
        for cp in sends:
            cp.start()
        refs[n_in + 2 + n][...] = jnp.zeros((8, 128), F32)

    dma = pltpu.SemaphoreType.DMA
    outs = pl.pallas_call(
        body, name=name,
        out_shape=(dma(sem_shape), dma(sem_shape), *[pltpu.HBM(b.shape, b.dtype) for b in bufs], jax.ShapeDtypeStruct((8, 128), F32)),
        in_specs=(HBM,) * n + ((ANY,) if dep is not None else ()),
        out_specs=(SEM, SEM) + (HBM,) * n + (pl.BlockSpec(memory_space=pltpu.VMEM),),
        input_output_aliases={i: i + 2 for i in range(n)},
        compiler_params=pltpu.CompilerParams(has_side_effects=SIDE_EFFECT),
    )(*[pltpu.with_memory_space_constraint(b, pltpu.HBM) for b in bufs], *(() if dep is None else (dep,)))
    return outs[0], outs[1], list(outs[2:2 + n]), outs[2 + n]


def _wait(name, started, plan, after):
    send, recv, bufs, _ = started
    n = len(bufs)

    def body(*refs):
        sends, recvs = plan(refs[:n], refs[n], refs[n + 1])
        for cp in sends:
            cp.wait_send()
        for cp in recvs:
            cp.wait_recv()

    return list(pl.pallas_call(
        body, name=name, out_shape=tuple(pltpu.HBM(b.shape, b.dtype) for b in bufs),
        in_specs=(HBM,) * n + (SEM, SEM, ANY), out_specs=(HBM,) * n, input_output_aliases={i: i for i in range(n)},
        compiler_params=pltpu.CompilerParams(has_side_effects=SIDE_EFFECT),
    )(*bufs, send, recv, after))


KINDS = ("col", "row")


def _gather_ici_plan(n_small):
    def plan(refs, send, recv):
        x, y, c, s, chips = _place()
        n = len(KINDS) + n_small
        sends, recvs = [], []
        for k, (px, py) in enumerate(chips):
            sp = 2 * px + py
            for a, kind in enumerate(KINDS):
                mine, theirs = _half(_shard(refs[a], kind, s), kind, c), _half(_shard(refs[a], kind, sp), kind, c)
                sends.append(_remote(mine, mine, send.at[k * n + a], recv.at[k * n + a], (px, py, c)))
                recvs.append(_remote(theirs, theirs, send.at[k * n + a], recv.at[k * n + a], (px, py, c)))
            for b in range(n_small):
                ref, sem = refs[len(KINDS) + b], k * n + len(KINDS) + b
                sends.append(_remote(ref.at[s], ref.at[s], send.at[sem], recv.at[sem], (px, py, c)))
                recvs.append(_remote(ref.at[sp], ref.at[sp], send.at[sem], recv.at[sem], (px, py, c)))
        return sends, recvs
    return plan


def _gather_d2d_plan(refs, send, recv):
    x, y, c, _, chips = _place()
    n = len(KINDS)
    sends, recvs = [], []
    for k, (px, py) in enumerate(chips):
        for a, kind in enumerate(KINDS):
            region, sem = _shard(refs[a], kind, 2 * px + py), k * n + a
            sends.append(_remote(_half(region, kind, c), _half(region, kind, c), send.at[sem], recv.at[sem], (x, y, 1 - c)))
            recvs.append(_remote(_half(region, kind, 1 - c), _half(region, kind, 1 - c), send.at[sem], recv.at[sem], (x, y, 1 - c)))
    return sends, recvs


def _swap_plan(refs, send, recv):
    x, y, c, _, _ = _place()
    n = len(KINDS)
    cps = [_remote(_half(refs[a], KINDS[a], 1 - c), refs[n + a], send.at[a], recv.at[a], (x, y, 1 - c)) for a in range(n)]
    return cps, cps


def _exchange_plan(refs, send, recv):
    x, y, c, _, chips = _place()
    n = len(KINDS)
    cps = []
    for k, (px, py) in enumerate(chips):
        for a in range(n):
            cps.append(_remote(_shard(refs[a], KINDS[a], 2 * px + py), refs[n + a].at[k], send.at[k * n + a], recv.at[k * n + a],
                               (px, py, c)))
    return cps, cps


def _share_plan(refs, send, recv):
    x, y, c, _, _ = _place()
    sends = [_remote(_half(refs[a], KINDS[a], c), _half(refs[a], KINDS[a], c), send.at[a], recv.at[a], (x, y, 1 - c))
             for a in range(len(KINDS))]
    recvs = [_remote(_half(refs[a], KINDS[a], 1 - c), _half(refs[a], KINDS[a], 1 - c), send.at[a], recv.at[a], (x, y, 1 - c))
             for a in range(len(KINDS))]
    return sends, recvs


def _spread_plan(refs, send, recv):
    packed, slots = refs
    x, y, c, _, _ = _place()
    sends, recvs = [], []
    for k in range(1, 8):
        px, py, pc = x ^ (k >> 2), y ^ ((k >> 1) & 1), c ^ (k & 1)
        sends.append(_remote(packed, slots.at[4 * x + 2 * y + c], send.at[k - 1], recv.at[k - 1], (px, py, pc)))
        recvs.append(_remote(packed, slots.at[4 * px + 2 * py + pc], send.at[k - 1], recv.at[k - 1], (px, py, pc)))
    return sends, recvs


def _half_index(kind, nblk):
    def index(i, j, place):
        return (i + place[1] * nblk[0], j) if kind == "col" else (i, j + place[1] * nblk[1])
    return index


def _chip_partial(g, other, kind, place, *, name):
    hr, hc = other.shape
    col = kind == "col"
    rb = _tile(hr if col else hr // N_CHIPS, 512, 16)
    cb = _tile(hc // N_CHIPS if col else hc, 1408)
    nblk = (hr // rb, hc // cb)
    own = (nblk[1] if col else nblk[0]) // N_CHIPS

    def block(i, j, place):
        s = place[0]
        return (i, j + jnp.where(j >= s * own, own, 0)) if col else (i + jnp.where(i >= s * own, own, 0), j)

    def body(place_ref, g_ref, o_ref, p_ref):
        p_ref[...] = (g_ref[...].astype(F32) + o_ref[...].astype(F32)).astype(BF16)

    plain = pl.BlockSpec((rb, cb), block)
    in_half = pl.BlockSpec((rb, cb), lambda i, j, place: _half_index(kind, nblk)(*block(i, j, place), place))
    grid = (nblk[0], nblk[1] - own) if col else (nblk[0] - own, nblk[1])
    return pl.pallas_call(
        body,
        grid_spec=pltpu.PrefetchScalarGridSpec(num_scalar_prefetch=1, grid=grid, in_specs=[in_half, plain], out_specs=plain),
        out_shape=jax.ShapeDtypeStruct((hr, hc), BF16), compiler_params=_params(("parallel", "parallel")), name=name,
    )(place, g, other)


def _reduce_half(g, other, recv, kind, place, *, name):
    _, pr, pc = recv.shape
    rb, cb = _tile(pr, 512, 16), _tile(pc, 1408)
    nblk = (pr // rb, pc // cb)
    full = (pr * 2, pc) if kind == "col" else (pr, pc * 2)

    def g_index(i, j, place):
        s, c = place[0], place[1]
        return (i + c * nblk[0], j + s * nblk[1]) if kind == "col" else (i + s * nblk[0], j + c * nblk[1])

    def o_index(i, j, place):
        return (i, j + place[0] * nblk[1]) if kind == "col" else (i + place[0] * nblk[0], j)

    def body(place_ref, g_ref, o_ref, r_ref, out_ref):
        acc = g_ref[...].astype(F32) + o_ref[...].astype(F32)
        for k in range(3):
            acc = acc + r_ref[k].astype(F32)
        out_ref[...] = acc

    return pl.pallas_call(
        body,
        grid_spec=pltpu.PrefetchScalarGridSpec(
            num_scalar_prefetch=1, grid=nblk,
            in_specs=[pl.BlockSpec((rb, cb), g_index), pl.BlockSpec((rb, cb), o_index),
                      pl.BlockSpec((3, rb, cb), lambda i, j, place: (0, i, j))],
            out_specs=pl.BlockSpec((rb, cb), _half_index(kind, nblk))),
        out_shape=jax.ShapeDtypeStruct(full, F32), compiler_params=_params(("parallel", "parallel")), name=name,
    )(place, g, other, recv)


def _sum_slots(packed, slots, me, *, name, rb=512):
    r, w = packed.shape
    rb = _tile(r, rb, 8)

    def body(me_ref, p_ref, s_ref, o_ref):
        acc = None
        for j in range(8):
            term = jnp.where(me_ref[0] == j, p_ref[...], s_ref[j])
            acc = term if acc is None else acc + term
        o_ref[...] = acc

    return pl.pallas_call(
        body,
        grid_spec=pltpu.PrefetchScalarGridSpec(
            num_scalar_prefetch=1, grid=(r // rb,),
            in_specs=[pl.BlockSpec((rb, w), lambda i, me: (i, 0)), pl.BlockSpec((8, rb, w), lambda i, me: (0, i, 0))],
            out_specs=pl.BlockSpec((rb, w), lambda i, me: (i, 0))),
        out_shape=jax.ShapeDtypeStruct((r, w), F32), compiler_params=_params(("parallel",)), name=name,
    )(me, packed, slots)


def _half_shape(a, kind):
    return (a.shape[0] // 2, a.shape[1]) if kind == "col" else (a.shape[0], a.shape[1] // 2)


def _rs_swap(tag, grads):
    others = [lax.empty(_half_shape(g, k), g.dtype) for g, k in zip(grads, KINDS)]
    return _start(f"rs_{tag}_swap", list(grads) + others, _swap_plan, (len(KINDS),))


def _rs_exchange(tag, swapped, place, after):
    bufs = _wait(f"rs_{tag}_swap_wait", swapped, _swap_plan, after)
    n = len(KINDS)
    grads, others = bufs[:n], bufs[n:]
    parts = [_chip_partial(g, o, k, place, name=f"rs_{tag}_partial_{k}") for g, o, k in zip(grads, others, KINDS)]
    lands = []
    for p, k in zip(parts, KINDS):
        piece = (p.shape[0], p.shape[1] // N_CHIPS) if k == "col" else (p.shape[0] // N_CHIPS, p.shape[1])
        lands.append(lax.empty((3,) + piece, p.dtype))
    return _start(f"rs_{tag}_exchange", parts + lands, _exchange_plan, (3 * n,)), grads, others


def _rs_share(tag, exchanged, place, after):
    started, grads, others = exchanged
    n = len(KINDS)
    recvs = _wait(f"rs_{tag}_exchange_wait", started, _exchange_plan, after)[n:]
    halves = [_reduce_half(g, o, r, k, place, name=f"rs_{tag}_reduce_{k}") for g, o, r, k in zip(grads, others, recvs, KINDS)]
    return _start(f"rs_{tag}_share", halves, _share_plan, (n,))


def _rs_finish(tag, shared, after):
    return _wait(f"rs_{tag}_share_wait", shared, _share_plan, after)


def _spread(tag, parts, dep):
    rows = [p.reshape(-1, 128) for p in parts]
    n = sum(r.shape[0] for r in rows)
    rows.append(jnp.zeros(((-n) % 512, 128), F32))
    packed = jnp.concatenate(rows, axis=0)
    return _start(f"small_{tag}_spread", [packed, lax.empty((8,) + packed.shape, F32)], _spread_plan, (7,), dep=dep)


def _spread_sum(tag, started, parts, me, after):
    packed, slots = _wait(f"small_{tag}_spread_wait", started, _spread_plan, after)
    total = _sum_slots(packed, slots, me, name=f"small_{tag}_sum")
    out, row = [], 0
    for p in parts:
        n = p.size // 128
        out.append(total[row:row + n].reshape(p.shape))
        row += n
    return out


def _ffn_fwd(x, h, w_up, conv_w, conv_b, w_down, tag, tm_up=1024):
    up = _mm(h, w_up, tm=tm_up, out_dtype=BF16, name=f"ffn{tag}_up")
    act, conv = _ffn_mid_fwd(up, conv_w, conv_b, name=f"ffn{tag}_mid")
    x_out = _mm(act, w_down, res=x, tk=2816, name=f"ffn{tag}_down")
    return x_out, (up, conv, act)


def kernel(x, a_norm, a_in, a_conv, a_out, b_norm, b_in, b_vnorm, b_ws, b_bs, b_out, f_norm, f_up, f_conv_w, f_conv_b, f_down, final_norm, loss_target, m_a_norm, m_a_in, m_a_conv, m_a_out, m_b_norm, m_b_in, m_b_vnorm, m_b_ws, m_b_bs, m_b_out, m_f_norm, m_f_up, m_f_conv_w, m_f_conv_b, m_f_down, m_final_norm, v_a_norm, v_a_in, v_a_conv, v_a_out, v_b_norm, v_b_in, v_b_vnorm, v_b_ws, v_b_bs, v_b_out, v_f_norm, v_f_up, v_f_conv_w, v_f_conv_b, v_f_down, v_final_norm):
    t, d = x.shape[1], x.shape[2]
    f2 = f_up.shape[2] * N_CHIPS
    x0, tgt = x.reshape(t, d), loss_target.reshape(t, d)
    ax, ay, ac = lax.axis_index("x"), lax.axis_index("y"), lax.axis_index("c")
    s = 2 * ax + ay
    place = jnp.stack([s, ac]).astype(jnp.int32)
    me = (4 * ax + 2 * ay + ac).astype(jnp.int32).reshape(1)

    def stacked(a):
        return lax.dynamic_update_index_in_dim(jnp.zeros((N_CHIPS,) + a.shape, F32), a, s, 0)

    def cast_pair(tag, w_in, w_out, layer, dep):
        return [_cast_into_full(w_in, layer, "col", place, name=f"cast_{tag}_in", dep=dep),
                _cast_into_full(w_out, layer, "row", place, name=f"cast_{tag}_out", dep=dep)]

    def gather_start(tag, fulls, small, dep):
        return _start(f"ag_{tag}_ici", fulls + small, _gather_ici_plan(len(small)), (3 * (2 + len(small)),), dep=dep)

    def gather_forward(tag, started, n_small, after):
        bufs = _wait(f"ag_{tag}_ici_wait", started, _gather_ici_plan(n_small), after)
        return _start(f"ag_{tag}_d2d", bufs[:2], _gather_d2d_plan, (3 * 2,)), bufs[2:]

    def gather_finish(tag, forwarded, after):
        return _wait(f"ag_{tag}_d2d_wait", forwarded, _gather_d2d_plan, after)

    small = [stacked(a_conv[0]), stacked(b_norm), stacked(b_vnorm), stacked(f_conv_w.reshape(2 * 3, -1))]
    ag_a = gather_start("a", cast_pair("a", a_in, a_out, 0, None), small, None)
    full_f0 = cast_pair("f0", f_up, f_down, 0, ag_a[3])
    full_b = cast_pair("b", b_in, b_out, 0, full_f0[1])
    full_f1 = cast_pair("f1", f_up, f_down, 1, full_b[1])

    def unshard(a):
        return jnp.transpose(a, (1, 0, 2)).reshape(a.shape[1], -1)

    ws, bs = b_ws[0], b_bs[0]

    h0 = _rms_fwd(x0, a_norm, dep=full_f1[1], name="a_norm")
    fw_a, (g_aconv, g_bnorm, g_bvnorm, g_fconv) = gather_forward("a", ag_a, 4, h0)
    ag_f0 = gather_start("f0", full_f0, [], fw_a[3])
    ag_b = gather_start("b", full_b, [], ag_f0[3])
    ag_f1 = gather_start("f1", full_f1, [], ag_b[3])
    w_ai, w_ao = gather_finish("a", fw_a, ag_f1[3])
    a_conv_f, b_norm_f, b_vnorm_f = unshard(g_aconv), unshard(g_bnorm), unshard(g_bvnorm)
    f_conv_f = unshard(g_fconv).reshape(2, 3, f2)
    bcx = _mm(h0, w_ai, tm=2048, out_dtype=BF16, name="a_in")
    y = _a_mid_fwd(bcx, a_conv_f, name="a_mid")
    x1 = _mm(y, w_ao, res=x0, tm=512, tn=2048, name="a_out")
    fw_f0, _ = gather_forward("f0", ag_f0, 0, x1)
    h1 = _rms_fwd(x1, f_norm[0:1], dep=fw_f0[3], name="ffn0_norm")
    w_up0, w_dn0 = gather_finish("f0", fw_f0, h1)
    x2, (up0, conv0, act0) = _ffn_fwd(x1, h1, w_up0, f_conv_f[0], f_conv_b[0:1], w_dn0, 0, tm_up=2048)
    fw_b, _ = gather_forward("b", ag_b, 0, up0)
    w_bi, w_bo = gather_finish("b", fw_b, act0)
    h2 = _rms_fwd(x2, b_norm_f, name="b_norm")
    zp = _mm(h2, w_bi, tm=2048, out_dtype=BF16, name="b_in")
    fw_f1, _ = gather_forward("f1", ag_f1, 0, zp)
    ug = _b_mid_fwd(zp, b_vnorm_f, ws, bs, name="b_mid")
    x3, h3 = _mm(ug, w_bo, res=x2, norm=f_norm[1:2], tm=512, tn=2048, name="b_out")
    w_up1, w_dn1 = gather_finish("f1", fw_f1, x3)
    x4, (up1, conv1, act1) = _ffn_fwd(x3, h3, w_up1, f_conv_f[1], f_conv_b[1:2], w_dn1, 1, tm_up=2048)
    loss_rows, dx4, dx4b, d_final = _final(x4, tgt, final_norm.reshape(1, d), name="final")

    d_dn1 = _mm(act1, dx4b, ta=True, tm=1408, out_dtype=BF16, name="ffn1_ddown")
    dact1 = _mm(dx4b, w_dn1, tb=True, tn=512, tm=2048, name="ffn1_dact")
    dup1, d_fwb1 = _ffn_mid_bwd(up1, conv1, dact1, f_conv_f[1], name="ffn1_mid_bwd")
    d_up1 = _mm(h3, dup1, ta=True, out_dtype=BF16, tk=4096, name="ffn1_dup")
    sw_f1 = _rs_swap("f1", [d_up1, d_dn1])
    dh3 = _mm(dup1, w_up1, tb=True, tk=2816, out_dtype=BF16, dep=sw_f1[3], name="ffn1_dh")
    dx3, dx3b, d_fnorm1 = _rms_bwd(dh3, x3, f_norm[1:2], dx4, name="ffn1_norm_bwd")
    ex_f1 = _rs_exchange("f1", sw_f1, place, dx3)

    d_bo = _mm(ug, dx3b, ta=True, out_dtype=BF16, tk=4096, dep=ex_f1[0][3], name="b_dout")
    dug = _mm(dx3b, w_bo, tb=True, tm=512, tn=2048, out_dtype=BF16, name="b_dug")
    dzp, d_ws, d_bs, d_bvnorm = _b_mid_bwd(zp, dug, b_vnorm_f, ws, bs, name="b_mid_bwd")
    d_bi = _mm(h2, dzp, ta=True, out_dtype=BF16, tk=4096, name="b_din")
    sw_b = _rs_swap("b", [d_bi, d_bo])
    dh2 = _mm(dzp, w_bi, tb=True, tk=4096, out_dtype=BF16, dep=sw_b[3], name="b_dh")
    dx2, dx2b, d_bnorm = _rms_bwd(dh2, x2, b_norm_f, dx3, name="b_norm_bwd")
    ex_b = _rs_exchange("b", sw_b, place, dx2)

    d_dn0 = _mm(act0, dx2b, ta=True, tm=1408, out_dtype=BF16, dep=ex_b[0][3], name="ffn0_ddown")
    dact0 = _mm(dx2b, w_dn0, tb=True, tn=512, tm=2048, name="ffn0_dact")
    dup0, d_fwb0 = _ffn_mid_bwd(up0, conv0, dact0, f_conv_f[0], name="ffn0_mid_bwd")
    sh_f1 = _rs_share("f1", ex_f1, place, dup0)
    d_up0 = _mm(h1, dup0, ta=True, out_dtype=BF16, tk=4096, dep=sh_f1[3], name="ffn0_dup")
    sw_f0 = _rs_swap("f0", [d_up0, d_dn0])
    g_up1, g_dn1 = _rs_finish("f1", sh_f1, sw_f0[3])
    dh1 = _mm(dup0, w_up0, tb=True, tk=2816, out_dtype=BF16, dep=sw_f0[3], name="ffn0_dh")
    dx1, dx1b, d_fnorm0 = _rms_bwd(dh1, x1, f_norm[0:1], dx2, name="ffn0_norm_bwd")
    ex_f0 = _rs_exchange("f0", sw_f0, place, dx1)
    early = [jnp.concatenate([d_bnorm, d_bvnorm, d_fnorm0, d_fnorm1, d_final, loss_rows], axis=0),
             jnp.concatenate([d_fwb0, d_fwb1], axis=0), jnp.concatenate([d_ws.reshape(-1, CHUNK), d_bs], axis=0)]
    sp_early = _spread("early", early, ex_f0[0][3])
    sh_b = _rs_share("b", ex_b, place, sp_early[3])

    d_ao = _mm(y, dx1b, ta=True, out_dtype=BF16, tk=4096, dep=sh_b[3], name="a_dout")
    dyy = _mm(dx1b, w_ao, tb=True, tm=512, tn=2048, name="a_dy")
    dbcx, d_aconv = _a_mid_bwd(bcx, dyy, a_conv_f, name="a_mid_bwd")
    d_ai = _mm(h0, dbcx, ta=True, out_dtype=BF16, tk=4096, name="a_din")
    sw_a = _rs_swap("a", [d_ai, d_ao])
    g_bi, g_bo = _rs_finish("b", sh_b, sw_a[3])
    early_adamw = {"b_in": _adamw_layer(b_in, g_bi, m_b_in, v_b_in, 0, None, name="adamw_b_in"),
                   "b_out": _adamw_layer(b_out, g_bo, m_b_out, v_b_out, 0, None, name="adamw_b_out")}
    ex_a = _rs_exchange("a", sw_a, place, early_adamw["b_in"][1][0, :8, :128] + early_adamw["b_out"][1][0, :8, :128])
    dh0 = _mm(dbcx, w_ai, tb=True, tk=3072, out_dtype=BF16, dep=ex_a[0][3], name="a_dh")
    grad_x, _, d_anorm = _rms_bwd(dh0, x0, a_norm, dx1, name="a_norm_bwd")
    late = [jnp.concatenate([d_anorm, d_aconv], axis=0)]
    sp_late = _spread("late", late, ex_a[0][3])
    sh_f0 = _rs_share("f0", ex_f0, place, sp_late[3])
    sh_a = _rs_share("a", ex_a, place, sh_f0[3])
    g_up0, g_dn0 = _rs_finish("f0", sh_f0, sh_a[3])
    g_ai, g_ao = _rs_finish("a", sh_a, g_up0)
    r_a, r_b, r_c = _spread_sum("early", sp_early, early, me, g_ai)
    (r_l,) = _spread_sum("late", sp_late, late, me, r_a)

    loss = jnp.sum(r_a[40])
    cs, fs = d // N_CHIPS, f2 // N_CHIPS

    def mine(a, width):
        return lax.dynamic_slice_in_dim(a, s * width, width, axis=1)

    grads = {
        "a_norm": r_l[0:1], "a_conv": mine(r_l[8:11], cs), "b_norm": mine(r_a[0:1], cs), "b_vnorm": mine(r_a[8:9], cs),
        "f_norm": jnp.concatenate([r_a[16:17], r_a[24:25]], axis=0), "final_norm": r_a[32:33],
        "b_ws": r_c[:GROUPS * CHUNK], "b_bs": r_c[GROUPS * CHUNK:],
        "f_conv_w": jnp.concatenate([mine(r_b[0:3], fs), mine(r_b[8:11], fs)], axis=0),
        "f_conv_b": jnp.concatenate([r_b[3:4], r_b[11:12]], axis=0),
        "a_in": g_ai, "a_out": g_ao, "b_in": g_bi, "b_out": g_bo,
    }
    names = ["a_norm", "a_in", "a_conv", "a_out", "b_norm", "b_in", "b_vnorm", "b_ws", "b_bs", "b_out", "f_norm", "f_up",
             "f_conv_w", "f_conv_b", "f_down", "final_norm"]
    weights = dict(zip(names, [a_norm, a_in, a_conv, a_out, b_norm, b_in, b_vnorm, b_ws, b_bs, b_out, f_norm, f_up, f_conv_w,
                               f_conv_b, f_down, final_norm]))
    ms = dict(zip(names, [m_a_norm, m_a_in, m_a_conv, m_a_out, m_b_norm, m_b_in, m_b_vnorm, m_b_ws, m_b_bs, m_b_out, m_f_norm,
                          m_f_up, m_f_conv_w, m_f_conv_b, m_f_down, m_final_norm]))
    vs = dict(zip(names, [v_a_norm, v_a_in, v_a_conv, v_a_out, v_b_norm, v_b_in, v_b_vnorm, v_b_ws, v_b_bs, v_b_out, v_f_norm,
                          v_f_up, v_f_conv_w, v_f_conv_b, v_f_down, v_final_norm]))
    result = {}
    for n in names:
        w = weights[n]
        if n in ("f_up", "f_down"):
            g1, g0 = (g_up1, g_up0) if n == "f_up" else (g_dn1, g_dn0)
            first = _adamw_layer(w, g1, ms[n], vs[n], 1, None, name=f"adamw_{n}1")
            result[n] = _adamw_layer(w, g0, ms[n], vs[n], 0, tuple(first), name=f"adamw_{n}0")
            continue
        if n in early_adamw:
            result[n] = early_adamw[n]
            continue
        g2 = grads[n]
        as3d = (lambda a: a.reshape((1,) + g2.shape))
        result[n] = [o.reshape(w.shape) for o in _adamw_layer(as3d(w), g2, as3d(ms[n]), as3d(vs[n]), 0, None, name=f"adamw_{n}")]

    return (loss, grad_x.reshape(x.shape), *[result[n][0] for n in names], *[result[n][1] for n in names],
            *[result[n][2] for n in names], *[result[n][3] for n in names])
```

```python
import jax
import jax.numpy as jnp
from jax import lax
from jax.experimental import pallas as pl
from jax.experimental.pallas import tpu as pltpu

F32 = jnp.float32
BF16 = jnp.bfloat16
MESH = pl.DeviceIdType.MESH
ANY = pl.BlockSpec(memory_space=pl.ANY)

RMS_EPS = 1e-5
CHUNK = 128
GROUPS = 8
ADAM_LR, ADAM_B1, ADAM_B2, ADAM_EPS, ADAM_WD, ADAM_STEP = 0.001, 0.9, 0.999, 1e-08, 0.01, 10

N_CHIPS = 4
HALO = 8
BF16_HALO = 16
VMEM_LIMIT = 56 * 1024 * 1024
GELU_C = 0.7978845608028654
GELU_A = 0.044715


def _params(sem=None):
    return pltpu.CompilerParams(dimension_semantics=sem, vmem_limit_bytes=VMEM_LIMIT)


def _tile(dim, pref, quantum=128):
    if dim <= pref:
        return dim
    t = (pref // quantum) * quantum
    while t >= quantum:
        if dim % t == 0:
            return t
        t -= quantum
    return dim


def _mm(a, b, *, name, ta=False, tb=False, res=None, norm=None, dep=None, out_dtype=F32, tm=1024, tn=1024, tk=2048):
    (K, M) = a.shape if ta else a.shape[::-1]
    N = b.shape[0] if tb else b.shape[1]
    assert (b.shape[1] if tb else b.shape[0]) == K
    tm, tn, tk = _tile(M, tm), _tile(N, tn), _tile(K, tk)
    nk = K // tk
    assert norm is None or (tn == N and nk == 1)
    a_spec = pl.BlockSpec((tk, tm), lambda i, j, k: (k, i)) if ta else pl.BlockSpec((tm, tk), lambda i, j, k: (i, k))
    b_spec = pl.BlockSpec((tn, tk), lambda i, j, k: (j, k)) if tb else pl.BlockSpec((tk, tn), lambda i, j, k: (k, j))
    o_spec = pl.BlockSpec((tm, tn), lambda i, j, k: (i, j))
    dims = (((0 if ta else 1,), (1 if tb else 0,)), ((), ()))
    direct = out_dtype == F32
    n_in = 2 + (res is not None) + (norm is not None) + (dep is not None)

    def body(*refs):
        a_ref, b_ref = refs[0], refs[1]
        r_ref = refs[2] if res is not None else None
        g_ref = refs[2 + (res is not None)] if norm is not None else None
        o_ref = refs[n_in]
        acc_ref = o_ref if direct else refs[-1]
        part = lax.dot_general(a_ref[...], b_ref[...], dims, preferred_element_type=F32)
        if nk == 1:
            if r_ref is not None:
                part = part + r_ref[...]
            o_ref[...] = part.astype(o_ref.dtype)
            if g_ref is not None:
                r = lax.rsqrt(jnp.mean(part * part, axis=-1, keepdims=True) + RMS_EPS)
                refs[n_in + 1][...] = ((part * r) * g_ref[...]).astype(BF16)
            return
        k = pl.program_id(2)

        @pl.when(k == 0)
        def _():
            acc_ref[...] = part

        @pl.when(jnp.logical_and(k > 0, k < nk - 1))
        def _():
            acc_ref[...] += part

        @pl.when(k == nk - 1)
        def _():
            tot = acc_ref[...] + part
            if r_ref is not None:
                tot = tot + r_ref[...]
            o_ref[...] = tot.astype(o_ref.dtype)

    in_specs = ([a_spec, b_spec] + ([o_spec] if res is not None else [])
                + ([pl.BlockSpec((1, tn), lambda i, j, k: (0, j))] if norm is not None else []) + ([ANY] if dep is not None else []))
    args = (a, b) + tuple(x for x in (res, norm, dep) if x is not None)
    scratch = [] if (direct or nk == 1) else [pltpu.VMEM((tm, tn), F32)]
    out_shape = jax.ShapeDtypeStruct((M, N), out_dtype)
    return pl.pallas_call(
        body, grid=(M // tm, N // tn, nk), in_specs=in_specs, out_specs=[o_spec, o_spec] if norm is not None else o_spec,
        out_shape=[out_shape, jax.ShapeDtypeStruct((M, N), BF16)] if norm is not None else out_shape, scratch_shapes=scratch,
        compiler_params=_params(("parallel", "parallel", "arbitrary")), name=name,
    )(*args)


def _row_spec(rb, w):
    return pl.BlockSpec((rb, w), lambda i: (i, 0))


def _next_spec(rb, w, t):
    return pl.BlockSpec((HALO, w), lambda i: (jnp.minimum((i + 1) * (rb // HALO), t // HALO - 1), 0))


def _prev_spec16(rb, w):
    return pl.BlockSpec((BF16_HALO, w), lambda i: (jnp.maximum(i * (rb // BF16_HALO) - 1, 0), 0))


def _next_spec16(rb, w, t):
    return pl.BlockSpec((BF16_HALO, w), lambda i: (jnp.minimum((i + 1) * (rb // BF16_HALO), t // BF16_HALO - 1), 0))


def _full_spec(shape):
    return pl.BlockSpec(shape, lambda i: tuple(0 for _ in shape))


def _shift(e, s):
    return pltpu.roll(e, s % e.shape[0], 0)


def _gelu(x):
    return 0.5 * x * (1.0 + jnp.tanh(GELU_C * (x + GELU_A * x * x * x)))


def _gelu_and_grad(x):
    x2 = x * x
    th = jnp.tanh(GELU_C * (x + GELU_A * x2 * x))
    half = 0.5 * (1.0 + th)
    return x * half, half + 0.5 * x * (1.0 - th * th) * (GELU_C * (1.0 + 3.0 * GELU_A * x2))


def _sigmoid(x):
    return 1.0 / (1.0 + jnp.exp(-x))


def _rms_fwd(x, g, *, name, dep=None, rb=1024):
    t, d = x.shape
    rb = _tile(t, rb, 8)

    def body(x_ref, g_ref, *rest):
        h_ref = rest[-1]
        xv = x_ref[...]
        r = lax.rsqrt(jnp.mean(xv * xv, axis=-1, keepdims=True) + RMS_EPS)
        h_ref[...] = ((xv * r) * g_ref[...]).astype(BF16)

    return pl.pallas_call(
        body, grid=(t // rb,), in_specs=[_row_spec(rb, d), _full_spec((1, d))] + ([ANY] if dep is not None else []),
        out_specs=_row_spec(rb, d), out_shape=jax.ShapeDtypeStruct((t, d), BF16), compiler_params=_params(("parallel",)), name=name,
    )(x, g, *(() if dep is None else (dep,)))


def _rms_bwd(dh, x, g, dres, *, name, rb=512):
    t, d = x.shape
    rb = _tile(t, rb, 16)

    def body(dh_ref, x_ref, g_ref, dres_ref, dx_ref, dxb_ref, dg_ref):
        xv = x_ref[...]
        r = lax.rsqrt(jnp.mean(xv * xv, axis=-1, keepdims=True) + RMS_EPS)
        xhat = xv * r
        dh_v = dh_ref[...].astype(F32)
        dxhat = dh_v * g_ref[...]
        m = jnp.mean(dxhat * xhat, axis=-1, keepdims=True)
        dx = dres_ref[...] + r * (dxhat - xhat * m)
        dx_ref[...] = dx
        dxb_ref[...] = dx.astype(BF16)

        @pl.when(pl.program_id(0) == 0)
        def _():
            dg_ref[...] = jnp.zeros_like(dg_ref)

        dg_ref[0:1, :] += jnp.sum(dh_v * xhat, axis=0, keepdims=True)

    return pl.pallas_call(
        body, grid=(t // rb,),
        in_specs=[_row_spec(rb, d), _row_spec(rb, d), _full_spec((1, d)), _row_spec(rb, d)],
        out_specs=[_row_spec(rb, d), _row_spec(rb, d), _full_spec((8, d))],
        out_shape=[jax.ShapeDtypeStruct((t, d), F32), jax.ShapeDtypeStruct((t, d), BF16), jax.ShapeDtypeStruct((8, d), F32)],
        compiler_params=_params(("arbitrary",)), name=name,
    )(dh, x, g, dres)


def _final(x, tgt, g, *, name, rb=512):
    t, d = x.shape
    rb = _tile(t, rb, 8)
    inv_d = 1.0 / d

    def body(x_ref, t_ref, g_ref, l_ref, dx_ref, dxb_ref, dg_ref):
        xv = x_ref[...]
        gv = g_ref[...]
        r = lax.rsqrt(jnp.mean(xv * xv, axis=-1, keepdims=True) + RMS_EPS)
        xhat = xv * r
        e = xhat * gv - t_ref[...]
        dy = e * inv_d
        dxhat = dy * gv
        m = jnp.mean(dxhat * xhat, axis=-1, keepdims=True)
        dx = r * (dxhat - xhat * m)
        dx_ref[...] = dx
        dxb_ref[...] = dx.astype(BF16)

        @pl.when(pl.program_id(0) == 0)
        def _():
            l_ref[...] = jnp.zeros_like(l_ref)
            dg_ref[...] = jnp.zeros_like(dg_ref)

        l_ref[0:1, :] += jnp.sum(e * e, axis=0, keepdims=True) * (0.5 * inv_d)
        dg_ref[0:1, :] += jnp.sum(dy * xhat, axis=0, keepdims=True)

    return pl.pallas_call(
        body, grid=(t // rb,),
        in_specs=[_row_spec(rb, d), _row_spec(rb, d), _full_spec((1, d))],
        out_specs=[_full_spec((8, d)), _row_spec(rb, d), _row_spec(rb, d), _full_spec((8, d))],
        out_shape=[jax.ShapeDtypeStruct((8, d), F32), jax.ShapeDtypeStruct((t, d), F32),
                   jax.ShapeDtypeStruct((t, d), BF16), jax.ShapeDtypeStruct((8, d), F32)],
        compiler_params=_params(("arbitrary",)), name=name,
    )(x, tgt, g)


def _a_mid_fwd(bcx, wconv, *, name, rb=256, cw=512):
    t, d3 = bcx.shape
    d = d3 // 3
    rb, cw = _tile(t, rb, 16), _tile(d, cw)

    def body(cur_ref, prev_ref, w_ref, y_ref):
        first = pl.program_id(0) == 0

        def f32(ref, cols):
            return ref[:, cols].astype(F32)

        for c0 in range(0, d, cw):
            cs = slice(c0, c0 + cw)
            gc, xs = slice(d + c0, d + c0 + cw), slice(2 * d + c0, 2 * d + c0 + cw)
            p_prev = jnp.where(first, 0.0, (f32(prev_ref, gc) * f32(prev_ref, xs))[BF16_HALO - HALO:])
            e = jnp.concatenate([p_prev, f32(cur_ref, gc) * f32(cur_ref, xs)], axis=0)
            w = w_ref[:, cs]
            q = w[0:1] * _shift(e, 2) + w[1:2] * _shift(e, 1) + w[2:3] * e
            y_ref[:, cs] = (f32(cur_ref, cs) * q[HALO:]).astype(BF16)

    return pl.pallas_call(
        body, grid=(t // rb,),
        in_specs=[_row_spec(rb, d3), _prev_spec16(rb, d3), _full_spec((3, d))], out_specs=_row_spec(rb, d),
        out_shape=jax.ShapeDtypeStruct((t, d), BF16), compiler_params=_params(("parallel",)), name=name,
    )(bcx, bcx, wconv)


def _a_mid_bwd(bcx, dy, wconv, *, name, rb=256, cw=512):
    t, d3 = bcx.shape
    d = d3 // 3
    rb, cw = _tile(t, rb, 16), _tile(d, cw)

    def body(cur_ref, prev_ref, next_ref, dy_ref, dyn_ref, w_ref, o_ref, dw_ref):
        i = pl.program_id(0)
        first, last = i == 0, i == pl.num_programs(0) - 1

        @pl.when(first)
        def _():
            dw_ref[...] = jnp.zeros_like(dw_ref)

        def f32(ref, cols):
            return ref[:, cols].astype(F32)

        for c0 in range(0, d, cw):
            cs = slice(c0, c0 + cw)
            gc, xs = slice(d + c0, d + c0 + cw), slice(2 * d + c0, 2 * d + c0 + cw)
            zeros = jnp.zeros((HALO, cw), F32)
            gb_c, gc_c, xs_c = f32(cur_ref, cs), f32(cur_ref, gc), f32(cur_ref, xs)
            p_prev = jnp.where(first, 0.0, (f32(prev_ref, gc) * f32(prev_ref, xs))[BF16_HALO - HALO:])
            e = jnp.concatenate([p_prev, gc_c * xs_c, zeros], axis=0)
            dq_next = jnp.where(last, 0.0, dyn_ref[:, cs] * f32(next_ref, cs)[:HALO])
            dy_c = dy_ref[:, cs]
            dq = jnp.concatenate([zeros, dy_c * gb_c, dq_next], axis=0)
            w = w_ref[:, cs]
            e1, e2 = _shift(e, 1), _shift(e, 2)
            q = w[0:1] * e2 + w[1:2] * e1 + w[2:3] * e
            dp = (w[2:3] * dq + w[1:2] * _shift(dq, -1) + w[0:1] * _shift(dq, -2))[HALO:HALO + rb]
            o_ref[:, cs] = (dy_c * q[HALO:HALO + rb]).astype(BF16)
            o_ref[:, gc] = (dp * xs_c).astype(BF16)
            o_ref[:, xs] = (dp * gc_c).astype(BF16)
            dq_c = dq[HALO:HALO + rb]
            dw_ref[0:1, cs] += jnp.sum(dq_c * e2[HALO:HALO + rb], axis=0, keepdims=True)
            dw_ref[1:2, cs] += jnp.sum(dq_c * e1[HALO:HALO + rb], axis=0, keepdims=True)
            dw_ref[2:3, cs] += jnp.sum(dq_c * e[HALO:HALO + rb], axis=0, keepdims=True)

    return pl.pallas_call(
        body, grid=(t // rb,),
        in_specs=[_row_spec(rb, d3), _prev_spec16(rb, d3), _next_spec16(rb, d3, t), _row_spec(rb, d), _next_spec(rb, d, t),
                  _full_spec((3, d))],
        out_specs=[_row_spec(rb, d3), _full_spec((8, d))],
        out_shape=[jax.ShapeDtypeStruct((t, d3), BF16), jax.ShapeDtypeStruct((8, d), F32)],
        compiler_params=_params(("arbitrary",)), name=name,
    )(bcx, bcx, bcx, dy, dy, wconv)


def _ffn_mid_fwd(up, wconv, bconv, *, name, rb=128, cw=512):
    t, f2 = up.shape
    f = f2 // 2
    rb, cw = _tile(t, rb, 16), _tile(f, cw)

    def body(cur_ref, prev_ref, w_ref, b_ref, act_ref, conv_ref):
        first = pl.program_id(0) == 0

        def conv(cols):
            prev = prev_ref[:, cols].astype(F32)[BF16_HALO - HALO:]
            e = jnp.concatenate([jnp.where(first, 0.0, prev), cur_ref[:, cols].astype(F32)], axis=0)
            w = w_ref[:, cols]
            out = (w[0:1] * _shift(e, 2) + w[1:2] * _shift(e, 1) + w[2:3] * e + b_ref[:, cols])[HALO:]
            conv_ref[:, cols] = out.astype(BF16)
            return out

        for c0 in range(0, f, cw):
            g = conv(slice(c0, c0 + cw))
            a = conv(slice(f + c0, f + c0 + cw))
            act_ref[:, c0:c0 + cw] = (g * _sigmoid(g) * a).astype(BF16)

    return pl.pallas_call(
        body, grid=(t // rb,),
        in_specs=[_row_spec(rb, f2), _prev_spec16(rb, f2), _full_spec((3, f2)), _full_spec((1, f2))],
        out_specs=[_row_spec(rb, f), _row_spec(rb, f2)],
        out_shape=[jax.ShapeDtypeStruct((t, f), BF16), jax.ShapeDtypeStruct((t, f2), BF16)],
        compiler_params=_params(("parallel",)), name=name,
    )(up, up, wconv, bconv)


def _ffn_mid_bwd(up, conv, dact, wconv, *, name, rb=128, cw=512):
    t, f2 = up.shape
    f = f2 // 2
    rb, cw = _tile(t, rb, 16), _tile(f, cw)

    def body(up_ref, conv_ref, convn_ref, da_ref, dan_ref, w_ref, o_ref, dwb_ref):
        i = pl.program_id(0)
        last = i == pl.num_programs(0) - 1

        @pl.when(i == 0)
        def _():
            dwb_ref[...] = jnp.zeros_like(dwb_ref)

        def rows(cols):
            return jnp.concatenate([conv_ref[:, cols].astype(F32), convn_ref[:, cols].astype(F32)[0:HALO]], axis=0)

        def back(dc, cols):
            w = w_ref[:, cols]
            dc1, dc2 = _shift(dc, -1)[:rb], _shift(dc, -2)[:rb]
            dc0 = dc[:rb]
            o_ref[:, cols] = (w[2:3] * dc0 + w[1:2] * dc1 + w[0:1] * dc2).astype(BF16)
            u = up_ref[:, cols].astype(F32)
            ones = jnp.ones((8, rb), BF16)
            for k, prod in enumerate((dc2 * u, dc1 * u, dc0 * u, dc0)):
                dwb_ref[k:k + 1, cols] += jnp.dot(ones, prod.astype(BF16), preferred_element_type=F32)[0:1]

        for c0 in range(0, f, cw):
            gcols, acols = slice(c0, c0 + cw), slice(f + c0, f + c0 + cw)
            g, a = rows(gcols), rows(acols)
            da = jnp.concatenate([da_ref[:, gcols], jnp.where(last, 0.0, dan_ref[:, gcols])], axis=0)
            sg = _sigmoid(g)
            back(da * a * (sg * (1.0 + g * (1.0 - sg))), gcols)
            back(da * (g * sg), acols)

    return pl.pallas_call(
        body, grid=(t // rb,),
        in_specs=[_row_spec(rb, f2), _row_spec(rb, f2), _next_spec16(rb, f2, t), _row_spec(rb, f), _next_spec(rb, f, t),
                  _full_spec((3, f2))],
        out_specs=[_row_spec(rb, f2), _full_spec((8, f2))],
        out_shape=[jax.ShapeDtypeStruct((t, f2), BF16), jax.ShapeDtypeStruct((8, f2), F32)],
        compiler_params=_params(("arbitrary",)), name=name,
    )(up, conv, conv, dact, dact, wconv)


def _causal_mask():
    return lax.broadcasted_iota(jnp.int32, (CHUNK, CHUNK), 0) >= lax.broadcasted_iota(jnp.int32, (CHUNK, CHUNK), 1)


def _b_mid_fwd(zp, vnorm, ws, bs, *, name, rb=512):
    t, d2 = zp.shape
    d = d2 // 2
    c = d // GROUPS
    rb = _tile(t, rb, CHUNK)

    def body(zp_ref, gv_ref, ws_ref, bs_ref, ug_ref, vn_ref, gate_ref):
        v = _gelu(zp_ref[:, d:].astype(F32))
        rv = lax.rsqrt(jnp.mean(v * v, axis=-1, keepdims=True) + RMS_EPS)
        vn_ref[...] = ((v * rv) * gv_ref[...]).astype(BF16)
        mask = _causal_mask()
        for h in range(GROUPS):
            hc = slice(h * c, (h + 1) * c)
            wm = jnp.where(mask, ws_ref[h], 0.0).astype(BF16)
            bcol = jnp.broadcast_to(bs_ref[h:h + 1, :], (CHUNK, CHUNK)).T[:, 0:1]
            for n in range(rb // CHUNK):
                rows = slice(n * CHUNK, (n + 1) * CHUNK)
                gate_ref[rows, hc] = jnp.dot(wm, vn_ref[rows, hc], preferred_element_type=F32) + bcol
        ug_ref[...] = (_gelu(zp_ref[:, :d].astype(F32)) * gate_ref[...]).astype(BF16)

    return pl.pallas_call(
        body, grid=(t // rb,),
        in_specs=[_row_spec(rb, d2), _full_spec((1, d)), _full_spec((GROUPS, CHUNK, CHUNK)), _full_spec((GROUPS, CHUNK))],
        out_specs=_row_spec(rb, d), out_shape=jax.ShapeDtypeStruct((t, d), BF16),
        scratch_shapes=[pltpu.VMEM((rb, d), BF16), pltpu.VMEM((rb, d), F32)],
        compiler_params=_params(("parallel",)), name=name,
    )(zp, vnorm, ws, bs)


def _b_mid_bwd(zp, dug, vnorm, ws, bs, *, name, rb=512):
    t, d2 = zp.shape
    d = d2 // 2
    c = d // GROUPS
    rb = _tile(t, rb, CHUNK)

    def body(zp_ref, dug_ref, gv_ref, ws_ref, bs_ref, dzp_ref, dws_ref, dbs_ref, dgv_ref,
             vn_ref, gate_ref, dm_ref, dvn_ref, dbacc_ref):
        i = pl.program_id(0)

        @pl.when(i == 0)
        def _():
            dws_ref[...] = jnp.zeros_like(dws_ref)
            dgv_ref[...] = jnp.zeros_like(dgv_ref)
            dbacc_ref[...] = jnp.zeros_like(dbacc_ref)

        zu, zv = zp_ref[:, :d].astype(F32), zp_ref[:, d:].astype(F32)
        (u, u_grad), (v, v_grad) = _gelu_and_grad(zu), _gelu_and_grad(zv)
        rv = lax.rsqrt(jnp.mean(v * v, axis=-1, keepdims=True) + RMS_EPS)
        vhat = v * rv
        gv = gv_ref[...]
        vn_ref[...] = (vhat * gv).astype(BF16)
        dug_v = dug_ref[...].astype(F32)
        dm = dug_v * u
        dm_ref[...] = dm.astype(BF16)
        mask = _causal_mask()
        for h in range(GROUPS):
            hc = slice(h * c, (h + 1) * c)
            wm = jnp.where(mask, ws_ref[h], 0.0)
            wm_b, wmt_b = wm.astype(BF16), wm.T.astype(BF16)
            bcol = jnp.broadcast_to(bs_ref[h:h + 1, :], (CHUNK, CHUNK)).T[:, 0:1]
            dws_h = jnp.zeros((CHUNK, CHUNK), F32)
            dbs_h = jnp.zeros((CHUNK, c), F32)
            for n in range(rb // CHUNK):
                rows = slice(n * CHUNK, (n + 1) * CHUNK)
                vn_c, dm_c = vn_ref[rows, hc], dm_ref[rows, hc]
                gate_ref[rows, hc] = jnp.dot(wm_b, vn_c, preferred_element_type=F32) + bcol
                dws_h += lax.dot_general(dm_c, vn_c, (((1,), (1,)), ((), ())), preferred_element_type=F32)
                dvn_ref[rows, hc] = jnp.dot(wmt_b, dm_c, preferred_element_type=F32)
                dbs_h += dm[rows, hc]
            dws_ref[h] += dws_h
            dbacc_ref[h] += dbs_h
        du = dug_v * gate_ref[...]
        dvn = dvn_ref[...]
        dvhat = dvn * gv
        m = jnp.mean(dvhat * vhat, axis=-1, keepdims=True)
        dv = rv * (dvhat - vhat * m)
        dgv_ref[0:1, :] += jnp.sum(dvn * vhat, axis=0, keepdims=True)
        dzp_ref[:, :d] = (du * u_grad).astype(BF16)
        dzp_ref[:, d:] = (dv * v_grad).astype(BF16)

        @pl.when(i == pl.num_programs(0) - 1)
        def _():
            ones = jnp.ones((8, c), F32)
            for h in range(GROUPS):
                dws_ref[h] = jnp.where(mask, dws_ref[h], 0.0)
                row = lax.dot_general(ones, dbacc_ref[h], (((1,), (1,)), ((), ())),
                                      precision=lax.Precision.HIGHEST, preferred_element_type=F32)
                dbs_ref[h:h + 1, :] = row[0:1]

    return pl.pallas_call(
        body, grid=(t // rb,),
        in_specs=[_row_spec(rb, d2), _row_spec(rb, d), _full_spec((1, d)), _full_spec((GROUPS, CHUNK, CHUNK)),
                  _full_spec((GROUPS, CHUNK))],
        out_specs=[_row_spec(rb, d2), _full_spec((GROUPS, CHUNK, CHUNK)), _full_spec((GROUPS, CHUNK)), _full_spec((8, d))],
        out_shape=[jax.ShapeDtypeStruct((t, d2), BF16), jax.ShapeDtypeStruct((GROUPS, CHUNK, CHUNK), F32),
                   jax.ShapeDtypeStruct((GROUPS, CHUNK), F32), jax.ShapeDtypeStruct((8, d), F32)],
        scratch_shapes=[pltpu.VMEM((rb, d), BF16), pltpu.VMEM((rb, d), F32), pltpu.VMEM((rb, d), BF16),
                        pltpu.VMEM((rb, d), F32), pltpu.VMEM((GROUPS, CHUNK, c), F32)],
        compiler_params=_params(("arbitrary",)), name=name,
    )(zp, dug, vnorm, ws, bs)


def _cast_into_full(w, layer, kind, place, *, name, dep=None, rb=512):
    _, r, c = w.shape
    rb = _tile(r, rb, 16)
    nrb = r // rb
    full = (r, c * N_CHIPS) if kind == "col" else (r * N_CHIPS, c)

    def body(place_ref, w_ref, *rest):
        rest[-1][...] = w_ref[...].astype(BF16)

    def o_index(i, place):
        return (i, place[0]) if kind == "col" else (i + place[0] * nrb, 0)

    in_specs = [pl.BlockSpec((None, rb, c), lambda i, place: (layer, i, 0))] + ([ANY] if dep is not None else [])
    return pl.pallas_call(
        body,
        grid_spec=pltpu.PrefetchScalarGridSpec(num_scalar_prefetch=1, grid=(nrb,), in_specs=in_specs,
                                               out_specs=pl.BlockSpec((rb, c), o_index)),
        out_shape=jax.ShapeDtypeStruct(full, BF16), compiler_params=_params(("parallel",)), name=name,
    )(place, w, *(() if dep is None else (dep,)))


def _adamw_layer(w, g, m, v, layer, prev, *, name, rb=256):
    _, r, c = w.shape
    rb = _tile(r, rb, 8)
    c1 = 1.0 - ADAM_B1 ** ADAM_STEP
    c2 = 1.0 - ADAM_B2 ** ADAM_STEP

    def body(w_ref, g_ref, m_ref, v_ref, *rest):
        go_ref, d_ref, nm_ref, nv_ref = rest[-4:]
        gv = g_ref[...]
        nm = ADAM_B1 * m_ref[...] + (1.0 - ADAM_B1) * gv
        nv = ADAM_B2 * v_ref[...] + (1.0 - ADAM_B2) * (gv * gv)
        go_ref[...] = gv
        nm_ref[...] = nm
        nv_ref[...] = nv
        d_ref[...] = -ADAM_LR * ((nm / c1) / (jnp.sqrt(nv / c2) + ADAM_EPS) + ADAM_WD * w_ref[...])

    lay = pl.BlockSpec((None, rb, c), lambda i: (layer, i, 0))
    return pl.pallas_call(
        body, grid=(r // rb,), in_specs=[lay, _row_spec(rb, c), lay, lay] + ([ANY] * 4 if prev else []), out_specs=[lay] * 4,
        out_shape=[jax.ShapeDtypeStruct(w.shape, F32)] * 4, input_output_aliases={4 + k: k for k in range(4)} if prev else {},
        compiler_params=_params(("parallel",)), name=name,
    )(w, g, m, v, *(prev or ()))


HBM = pl.BlockSpec(memory_space=pltpu.HBM)
SEM = pl.BlockSpec(memory_space=pltpu.SEMAPHORE)
SIDE_EFFECT = pltpu.SideEffectType.DATAFLOW_SIDE_EFFECTING


def _place():
    x, y, c = lax.axis_index("x"), lax.axis_index("y"), lax.axis_index("c")
    chips = [(1 - x, y), (x, 1 - y), (1 - x, 1 - y)]
    return x, y, c, 2 * x + y, chips


def _half(ref, kind, c):
    r, w = ref.shape
    if kind == "col":
        return ref.at[pl.ds(pl.multiple_of(c * (r // 2), 8), r // 2), :]
    return ref.at[:, pl.ds(pl.multiple_of(c * (w // 2), 128), w // 2)]


def _shard(ref, kind, s):
    r, w = ref.shape
    if kind == "col":
        return ref.at[:, pl.ds(pl.multiple_of(s * (w // N_CHIPS), 128), w // N_CHIPS)]
    return ref.at[pl.ds(pl.multiple_of(s * (r // N_CHIPS), 8), r // N_CHIPS), :]


def _remote(src, dst, send_sem, recv_sem, dev):
    return pltpu.make_async_remote_copy(src_ref=src, dst_ref=dst, send_sem=send_sem, recv_sem=recv_sem,
                                        device_id=dev, device_id_type=MESH)


def _start(name, bufs, plan, sem_shape, dep=None):
    n = len(bufs)
    n_in = n + (dep is not None)

    def body(*refs):
        sends, _ = plan(refs[:n], refs[n_in], refs[n_in + 1])
        for cp in sends:
            cp.start()
        refs[n_in + 2 + n][...] = jnp.zeros((8, 128), F32)

    dma = pltpu.SemaphoreType.DMA
    outs = pl.pallas_call(
        body, name=name,
        out_shape=(dma(sem_shape), dma(sem_shape), *[pltpu.HBM(b.shape, b.dtype) for b in bufs], jax.ShapeDtypeStruct((8, 128), F32)),
        in_specs=(HBM,) * n + ((ANY,) if dep is not None else ()),
        out_specs=(SEM, SEM) + (HBM,) * n + (pl.BlockSpec(memory_space=pltpu.VMEM),),
        input_output_aliases={i: i + 2 for i in range(n)},
        compiler_params=pltpu.CompilerParams(has_side_effects=SIDE_EFFECT),
    )(*[pltpu.with_memory_space_constraint(b, pltpu.HBM) for b in bufs], *(() if dep is None else (dep,)))
    return outs[0], outs[1], list(outs[2:2 + n]), outs[2 + n]


def _wait(name, started, plan, after):
    send, recv, bufs, _ = started
    n = len(bufs)

    def body(*refs):
        sends, recvs = plan(refs[:n], refs[n], refs[n + 1])
        for cp in sends:
            cp.wait_send()
        for cp in recvs:
            cp.wait_recv()

    return list(pl.pallas_call(
        body, name=name, out_shape=tuple(pltpu.HBM(b.shape, b.dtype) for b in bufs),
        in_specs=(HBM,) * n + (SEM, SEM, ANY), out_specs=(HBM,) * n, input_output_aliases={i: i for i in range(n)},
        compiler_params=pltpu.CompilerParams(has_side_effects=SIDE_EFFECT),
    )(*bufs, send, recv, after))


KINDS = ("col", "row")


def _gather_ici_plan(n_small):
    def plan(refs, send, recv):
        x, y, c, s, chips = _place()
        n = len(KINDS) + n_small
        sends, recvs = [], []
        for k, (px, py) in enumerate(chips):
            sp = 2 * px + py
            for a, kind in enumerate(KINDS):
                mine, theirs = _half(_shard(refs[a], kind, s), kind, c), _half(_shard(refs[a], kind, sp), kind, c)
                sends.append(_remote(mine, mine, send.at[k * n + a], recv.at[k * n + a], (px, py, c)))
                recvs.append(_remote(theirs, theirs, send.at[k * n + a], recv.at[k * n + a], (px, py, c)))
            for b in range(n_small):
                ref, sem = refs[len(KINDS) + b], k * n + len(KINDS) + b
                sends.append(_remote(ref.at[s], ref.at[s], send.at[sem], recv.at[sem], (px, py, c)))
                recvs.append(_remote(ref.at[sp], ref.at[sp], send.at[sem], recv.at[sem], (px, py, c)))
        return sends, recvs
    return plan


def _gather_d2d_plan(refs, send, recv):
    x, y, c, _, chips = _place()
    n = len(KINDS)
    sends, recvs = [], []
    for k, (px, py) in enumerate(chips):
        for a, kind in enumerate(KINDS):
            region, sem = _shard(refs[a], kind, 2 * px + py), k * n + a
            sends.append(_remote(_half(region, kind, c), _half(region, kind, c), send.at[sem], recv.at[sem], (x, y, 1 - c)))
            recvs.append(_remote(_half(region, kind, 1 - c), _half(region, kind, 1 - c), send.at[sem], recv.at[sem], (x, y, 1 - c)))
    return sends, recvs


def _swap_plan(refs, send, recv):
    x, y, c, _, _ = _place()
    n = len(KINDS)
    cps = [_remote(_half(refs[a], KINDS[a], 1 - c), refs[n + a], send.at[a], recv.at[a], (x, y, 1 - c)) for a in range(n)]
    return cps, cps


def _exchange_plan(refs, send, recv):
    x, y, c, _, chips = _place()
    n = len(KINDS)
    cps = []
    for k, (px, py) in enumerate(chips):
        for a in range(n):
            cps.append(_remote(_shard(refs[a], KINDS[a], 2 * px + py), refs[n + a].at[k], send.at[k * n + a], recv.at[k * n + a],
                               (px, py, c)))
    return cps, cps


def _share_plan(refs, send, recv):
    x, y, c, _, _ = _place()
    sends = [_remote(_half(refs[a], KINDS[a], c), _half(refs[a], KINDS[a], c), send.at[a], recv.at[a], (x, y, 1 - c))
             for a in range(len(KINDS))]
    recvs = [_remote(_half(refs[a], KINDS[a], 1 - c), _half(refs[a], KINDS[a], 1 - c), send.at[a], recv.at[a], (x, y, 1 - c))
             for a in range(len(KINDS))]
    return sends, recvs


def _spread_plan(refs, send, recv):
    packed, slots = refs
    x, y, c, _, _ = _place()
    sends, recvs = [], []
    for k in range(1, 8):
        px, py, pc = x ^ (k >> 2), y ^ ((k >> 1) & 1), c ^ (k & 1)
        sends.append(_remote(packed, slots.at[4 * x + 2 * y + c], send.at[k - 1], recv.at[k - 1], (px, py, pc)))
        recvs.append(_remote(packed, slots.at[4 * px + 2 * py + pc], send.at[k - 1], recv.at[k - 1], (px, py, pc)))
    return sends, recvs


def _half_index(kind, nblk):
    def index(i, j, place):
        return (i + place[1] * nblk[0], j) if kind == "col" else (i, j + place[1] * nblk[1])
    return index


def _chip_partial(g, other, kind, place, *, name):
    hr, hc = other.shape
    col = kind == "col"
    rb = _tile(hr if col else hr // N_CHIPS, 512, 16)
    cb = _tile(hc // N_CHIPS if col else hc, 1408)
    nblk = (hr // rb, hc // cb)
    own = (nblk[1] if col else nblk[0]) // N_CHIPS

    def block(i, j, place):
        s = place[0]
        return (i, j + jnp.where(j >= s * own, own, 0)) if col else (i + jnp.where(i >= s * own, own, 0), j)

    def body(place_ref, g_ref, o_ref, p_ref):
        p_ref[...] = (g_ref[...].astype(F32) + o_ref[...].astype(F32)).astype(BF16)

    plain = pl.BlockSpec((rb, cb), block)
    in_half = pl.BlockSpec((rb, cb), lambda i, j, place: _half_index(kind, nblk)(*block(i, j, place), place))
    grid = (nblk[0], nblk[1] - own) if col else (nblk[0] - own, nblk[1])
    return pl.pallas_call(
        body,
        grid_spec=pltpu.PrefetchScalarGridSpec(num_scalar_prefetch=1, grid=grid, in_specs=[in_half, plain], out_specs=plain),
        out_shape=jax.ShapeDtypeStruct((hr, hc), BF16), compiler_params=_params(("parallel", "parallel")), name=name,
    )(place, g, other)


def _reduce_half(g, other, recv, kind, place, *, name):
    _, pr, pc = recv.shape
    rb, cb = _tile(pr, 512, 16), _tile(pc, 1408)
    nblk = (pr // rb, pc // cb)
    full = (pr * 2, pc) if kind == "col" else (pr, pc * 2)

    def g_index(i, j, place):
        s, c = place[0], place[1]
        return (i + c * nblk[0], j + s * nblk[1]) if kind == "col" else (i + s * nblk[0], j + c * nblk[1])

    def o_index(i, j, place):
        return (i, j + place[0] * nblk[1]) if kind == "col" else (i + place[0] * nblk[0], j)

    def body(place_ref, g_ref, o_ref, r_ref, out_ref):
        acc = g_ref[...].astype(F32) + o_ref[...].astype(F32)
        for k in range(3):
            acc = acc + r_ref[k].astype(F32)
        out_ref[...] = acc

    return pl.pallas_call(
        body,
        grid_spec=pltpu.PrefetchScalarGridSpec(
            num_scalar_prefetch=1, grid=nblk,
            in_specs=[pl.BlockSpec((rb, cb), g_index), pl.BlockSpec((rb, cb), o_index),
                      pl.BlockSpec((3, rb, cb), lambda i, j, place: (0, i, j))],
            out_specs=pl.BlockSpec((rb, cb), _half_index(kind, nblk))),
        out_shape=jax.ShapeDtypeStruct(full, F32), compiler_params=_params(("parallel", "parallel")), name=name,
    )(place, g, other, recv)


def _sum_slots(packed, slots, me, *, name, rb=512):
    r, w = packed.shape
    rb = _tile(r, rb, 8)

    def body(me_ref, p_ref, s_ref, o_ref):
        acc = None
        for j in range(8):
            term = jnp.where(me_ref[0] == j, p_ref[...], s_ref[j])
            acc = term if acc is None else acc + term
        o_ref[...] = acc

    return pl.pallas_call(
        body,
        grid_spec=pltpu.PrefetchScalarGridSpec(
            num_scalar_prefetch=1, grid=(r // rb,),
            in_specs=[pl.BlockSpec((rb, w), lambda i, me: (i, 0)), pl.BlockSpec((8, rb, w), lambda i, me: (0, i, 0))],
            out_specs=pl.BlockSpec((rb, w), lambda i, me: (i, 0))),
        out_shape=jax.ShapeDtypeStruct((r, w), F32), compiler_params=_params(("parallel",)), name=name,
    )(me, packed, slots)


def _half_shape(a, kind):
    return (a.shape[0] // 2, a.shape[1]) if kind == "col" else (a.shape[0], a.shape[1] // 2)


def _rs_swap(tag, grads):
    others = [lax.empty(_half_shape(g, k), g.dtype) for g, k in zip(grads, KINDS)]
    return _start(f"rs_{tag}_swap", list(grads) + others, _swap_plan, (len(KINDS),))


def _rs_exchange(tag, swapped, place, after):
    bufs = _wait(f"rs_{tag}_swap_wait", swapped, _swap_plan, after)
    n = len(KINDS)
    grads, others = bufs[:n], bufs[n:]
    parts = [_chip_partial(g, o, k, place, name=f"rs_{tag}_partial_{k}") for g, o, k in zip(grads, others, KINDS)]
    lands = []
    for p, k in zip(parts, KINDS):
        piece = (p.shape[0], p.shape[1] // N_CHIPS) if k == "col" else (p.shape[0] // N_CHIPS, p.shape[1])
        lands.append(lax.empty((3,) + piece, p.dtype))
    return _start(f"rs_{tag}_exchange", parts + lands, _exchange_plan, (3 * n,)), grads, others


def _rs_share(tag, exchanged, place, after):
    started, grads, others = exchanged
    n = len(KINDS)
    recvs = _wait(f"rs_{tag}_exchange_wait", started, _exchange_plan, after)[n:]
    halves = [_reduce_half(g, o, r, k, place, name=f"rs_{tag}_reduce_{k}") for g, o, r, k in zip(grads, others, recvs, KINDS)]
    return _start(f"rs_{tag}_share", halves, _share_plan, (n,))


def _rs_finish(tag, shared, after):
    return _wait(f"rs_{tag}_share_wait", shared, _share_plan, after)


def _spread(tag, parts, dep):
    rows = [p.reshape(-1, 128) for p in parts]
    n = sum(r.shape[0] for r in rows)
    rows.append(jnp.zeros(((-n) % 512, 128), F32))
    packed = jnp.concatenate(rows, axis=0)
    return _start(f"small_{tag}_spread", [packed, lax.empty((8,) + packed.shape, F32)], _spread_plan, (7,), dep=dep)


def _spread_sum(tag, started, parts, me, after):
    packed, slots = _wait(f"small_{tag}_spread_wait", started, _spread_plan, after)
    total = _sum_slots(packed, slots, me, name=f"small_{tag}_sum")
    out, row = [], 0
    for p in parts:
        n = p.size // 128
        out.append(total[row:row + n].reshape(p.shape))
        row += n
    return out


def _ffn_fwd(x, h, w_up, conv_w, conv_b, w_down, tag, tm_up=1024):
    up = _mm(h, w_up, tm=tm_up, out_dtype=BF16, name=f"ffn{tag}_up")
    act, conv = _ffn_mid_fwd(up, conv_w, conv_b, name=f"ffn{tag}_mid")
    x_out = _mm(act, w_down, res=x, tk=2816, name=f"ffn{tag}_down")
    return x_out, (up, conv, act)


def kernel(x, a_norm, a_in, a_conv, a_out, b_norm, b_in, b_vnorm, b_ws, b_bs, b_out, f_norm, f_up, f_conv_w, f_conv_b, f_down, final_norm, loss_target, m_a_norm, m_a_in, m_a_conv, m_a_out, m_b_norm, m_b_in, m_b_vnorm, m_b_ws, m_b_bs, m_b_out, m_f_norm, m_f_up, m_f_conv_w, m_f_conv_b, m_f_down, m_final_norm, v_a_norm, v_a_in, v_a_conv, v_a_out, v_b_norm, v_b_in, v_b_vnorm, v_b_ws, v_b_bs, v_b_out, v_f_norm, v_f_up, v_f_conv_w, v_f_conv_b, v_f_down, v_final_norm):
    t, d = x.shape[1], x.shape[2]
    f2 = f_up.shape[2] * N_CHIPS
    x0, tgt = x.reshape(t, d), loss_target.reshape(t, d)
    ax, ay, ac = lax.axis_index("x"), lax.axis_index("y"), lax.axis_index("c")
    s = 2 * ax + ay
    place = jnp.stack([s, ac]).astype(jnp.int32)
    me = (4 * ax + 2 * ay + ac).astype(jnp.int32).reshape(1)

    def stacked(a):
        return lax.dynamic_update_index_in_dim(jnp.zeros((N_CHIPS,) + a.shape, F32), a, s, 0)

    def cast_pair(tag, w_in, w_out, layer, dep):
        return [_cast_into_full(w_in, layer, "col", place, name=f"cast_{tag}_in", dep=dep),
                _cast_into_full(w_out, layer, "row", place, name=f"cast_{tag}_out", dep=dep)]

    def gather_start(tag, fulls, small, dep):
        return _start(f"ag_{tag}_ici", fulls + small, _gather_ici_plan(len(small)), (3 * (2 + len(small)),), dep=dep)

    def gather_forward(tag, started, n_small, after):
        bufs = _wait(f"ag_{tag}_ici_wait", started, _gather_ici_plan(n_small), after)
        return _start(f"ag_{tag}_d2d", bufs[:2], _gather_d2d_plan, (3 * 2,)), bufs[2:]

    def gather_finish(tag, forwarded, after):
        return _wait(f"ag_{tag}_d2d_wait", forwarded, _gather_d2d_plan, after)

    small = [stacked(a_conv[0]), stacked(b_norm), stacked(b_vnorm), stacked(f_conv_w.reshape(2 * 3, -1))]
    ag_a = gather_start("a", cast_pair("a", a_in, a_out, 0, None), small, None)
    full_f0 = cast_pair("f0", f_up, f_down, 0, ag_a[3])
    full_b = cast_pair("b", b_in, b_out, 0, full_f0[1])
    full_f1 = cast_pair("f1", f_up, f_down, 1, full_b[1])

    def unshard(a):
        return jnp.transpose(a, (1, 0, 2)).reshape(a.shape[1], -1)

    ws, bs = b_ws[0], b_bs[0]

    h0 = _rms_fwd(x0, a_norm, dep=full_f1[1], name="a_norm")
    fw_a, (g_aconv, g_bnorm, g_bvnorm, g_fconv) = gather_forward("a", ag_a, 4, h0)
    ag_f0 = gather_start("f0", full_f0, [], fw_a[3])
    ag_b = gather_start("b", full_b, [], ag_f0[3])
    ag_f1 = gather_start("f1", full_f1, [], ag_b[3])
    w_ai, w_ao = gather_finish("a", fw_a, ag_f1[3])
    a_conv_f, b_norm_f, b_vnorm_f = unshard(g_aconv), unshard(g_bnorm), unshard(g_bvnorm)
    f_conv_f = unshard(g_fconv).reshape(2, 3, f2)
    bcx = _mm(h0, w_ai, tm=2048, out_dtype=BF16, name="a_in")
    y = _a_mid_fwd(bcx, a_conv_f, name="a_mid")
    x1 = _mm(y, w_ao, res=x0, tm=512, tn=2048, name="a_out")
    fw_f0, _ = gather_forward("f0", ag_f0, 0, x1)
    h1 = _rms_fwd(x1, f_norm[0:1], dep=fw_f0[3], name="ffn0_norm")
    w_up0, w_dn0 = gather_finish("f0", fw_f0, h1)
    x2, (up0, conv0, act0) = _ffn_fwd(x1, h1, w_up0, f_conv_f[0], f_conv_b[0:1], w_dn0, 0, tm_up=2048)
    fw_b, _ = gather_forward("b", ag_b, 0, up0)
    w_bi, w_bo = gather_finish("b", fw_b, act0)
    h2 = _rms_fwd(x2, b_norm_f, name="b_norm")
    zp = _mm(h2, w_bi, tm=2048, out_dtype=BF16, name="b_in")
    fw_f1, _ = gather_forward("f1", ag_f1, 0, zp)
    ug = _b_mid_fwd(zp, b_vnorm_f, ws, bs, name="b_mid")
    x3, h3 = _mm(ug, w_bo, res=x2, norm=f_norm[1:2], tm=512, tn=2048, name="b_out")
    w_up1, w_dn1 = gather_finish("f1", fw_f1, x3)
    x4, (up1, conv1, act1) = _ffn_fwd(x3, h3, w_up1, f_conv_f[1], f_conv_b[1:2], w_dn1, 1, tm_up=2048)
    loss_rows, dx4, dx4b, d_final = _final(x4, tgt, final_norm.reshape(1, d), name="final")

    d_dn1 = _mm(act1, dx4b, ta=True, tm=1408, out_dtype=BF16, name="ffn1_ddown")
    dact1 = _mm(dx4b, w_dn1, tb=True, tn=512, tm=2048, name="ffn1_dact")
    dup1, d_fwb1 = _ffn_mid_bwd(up1, conv1, dact1, f_conv_f[1], name="ffn1_mid_bwd")
    d_up1 = _mm(h3, dup1, ta=True, out_dtype=BF16, tk=4096, name="ffn1_dup")
    sw_f1 = _rs_swap("f1", [d_up1, d_dn1])
    dh3 = _mm(dup1, w_up1, tb=True, tk=2816, out_dtype=BF16, dep=sw_f1[3], name="ffn1_dh")
    dx3, dx3b, d_fnorm1 = _rms_bwd(dh3, x3, f_norm[1:2], dx4, name="ffn1_norm_bwd")
    ex_f1 = _rs_exchange("f1", sw_f1, place, dx3)

    d_bo = _mm(ug, dx3b, ta=True, out_dtype=BF16, tk=4096, dep=ex_f1[0][3], name="b_dout")
    dug = _mm(dx3b, w_bo, tb=True, tm=512, tn=2048, out_dtype=BF16, name="b_dug")
    dzp, d_ws, d_bs, d_bvnorm = _b_mid_bwd(zp, dug, b_vnorm_f, ws, bs, name="b_mid_bwd")
    d_bi = _mm(h2, dzp, ta=True, out_dtype=BF16, tk=4096, name="b_din")
    sw_b = _rs_swap("b", [d_bi, d_bo])
    dh2 = _mm(dzp, w_bi, tb=True, tk=4096, out_dtype=BF16, dep=sw_b[3], name="b_dh")
    dx2, dx2b, d_bnorm = _rms_bwd(dh2, x2, b_norm_f, dx3, name="b_norm_bwd")
    ex_b = _rs_exchange("b", sw_b, place, dx2)

    d_dn0 = _mm(act0, dx2b, ta=True, tm=1408, out_dtype=BF16, dep=ex_b[0][3], name="ffn0_ddown")
    dact0 = _mm(dx2b, w_dn0, tb=True, tn=512, tm=2048, name="ffn0_dact")
    dup0, d_fwb0 = _ffn_mid_bwd(up0, conv0, dact0, f_conv_f[0], name="ffn0_mid_bwd")
    sh_f1 = _rs_share("f1", ex_f1, place, dup0)
    d_up0 = _mm(h1, dup0, ta=True, out_dtype=BF16, tk=4096, dep=sh_f1[3], name="ffn0_dup")
    sw_f0 = _rs_swap("f0", [d_up0, d_dn0])
    g_up1, g_dn1 = _rs_finish("f1", sh_f1, sw_f0[3])
    dh1 = _mm(dup0, w_up0, tb=True, tk=2816, out_dtype=BF16, dep=sw_f0[3], name="ffn0_dh")
    dx1, dx1b, d_fnorm0 = _rms_bwd(dh1, x1, f_norm[0:1], dx2, name="ffn0_norm_bwd")
    ex_f0 = _rs_exchange("f0", sw_f0, place, dx1)
    early = [jnp.concatenate([d_bnorm, d_bvnorm, d_fnorm0, d_fnorm1, d_final, loss_rows], axis=0),
             jnp.concatenate([d_fwb0, d_fwb1], axis=0), jnp.concatenate([d_ws.reshape(-1, CHUNK), d_bs], axis=0)]
    sp_early = _spread("early", early, ex_f0[0][3])
    sh_b = _rs_share("b", ex_b, place, sp_early[3])

    d_ao = _mm(y, dx1b, ta=True, out_dtype=BF16, tk=4096, dep=sh_b[3], name="a_dout")
    dyy = _mm(dx1b, w_ao, tb=True, tm=512, tn=2048, name="a_dy")
    dbcx, d_aconv = _a_mid_bwd(bcx, dyy, a_conv_f, name="a_mid_bwd")
    d_ai = _mm(h0, dbcx, ta=True, out_dtype=BF16, tk=4096, name="a_din")
    sw_a = _rs_swap("a", [d_ai, d_ao])
    g_bi, g_bo = _rs_finish("b", sh_b, sw_a[3])
    early_adamw = {"b_in": _adamw_layer(b_in, g_bi, m_b_in, v_b_in, 0, None, name="adamw_b_in"),
                   "b_out": _adamw_layer(b_out, g_bo, m_b_out, v_b_out, 0, None, name="adamw_b_out")}
    ex_a = _rs_exchange("a", sw_a, place, early_adamw["b_in"][1][0, :8, :128] + early_adamw["b_out"][1][0, :8, :128])
    dh0 = _mm(dbcx, w_ai, tb=True, tk=3072, out_dtype=BF16, dep=ex_a[0][3], name="a_dh")
    grad_x, _, d_anorm = _rms_bwd(dh0, x0, a_norm, dx1, name="a_norm_bwd")
    late = [jnp.concatenate([d_anorm, d_aconv], axis=0)]
    sp_late = _spread("late", late, ex_a[0][3])
    sh_f0 = _rs_share("f0", ex_f0, place, sp_late[3])
    sh_a = _rs_share("a", ex_a, place, sh_f0[3])
    g_up0, g_dn0 = _rs_finish("f0", sh_f0, sh_a[3])
    g_ai, g_ao = _rs_finish("a", sh_a, g_up0)
    r_a, r_b, r_c = _spread_sum("early", sp_early, early, me, g_ai)
    (r_l,) = _spread_sum("late", sp_late, late, me, r_a)

    loss = jnp.sum(r_a[40])
    cs, fs = d // N_CHIPS, f2 // N_CHIPS

    def mine(a, width):
        return lax.dynamic_slice_in_dim(a, s * width, width, axis=1)

    grads = {
        "a_norm": r_l[0:1], "a_conv": mine(r_l[8:11], cs), "b_norm": mine(r_a[0:1], cs), "b_vnorm": mine(r_a[8:9], cs),
        "f_norm": jnp.concatenate([r_a[16:17], r_a[24:25]], axis=0), "final_norm": r_a[32:33],
        "b_ws": r_c[:GROUPS * CHUNK], "b_bs": r_c[GROUPS * CHUNK:],
        "f_conv_w": jnp.concatenate([mine(r_b[0:3], fs), mine(r_b[8:11], fs)], axis=0),
        "f_conv_b": jnp.concatenate([r_b[3:4], r_b[11:12]], axis=0),
        "a_in": g_ai, "a_out": g_ao, "b_in": g_bi, "b_out": g_bo,
    }
    names = ["a_norm", "a_in", "a_conv", "a_out", "b_norm", "b_in", "b_vnorm", "b_ws", "b_bs", "b_out", "f_norm", "f_up",
             "f_conv_w", "f_conv_b", "f_down", "final_norm"]
    weights = dict(zip(names, [a_norm, a_in, a_conv, a_out, b_norm, b_in, b_vnorm, b_ws, b_bs, b_out, f_norm, f_up, f_conv_w,
                               f_conv_b, f_down, final_norm]))
    ms = dict(zip(names, [m_a_norm, m_a_in, m_a_conv, m_a_out, m_b_norm, m_b_in, m_b_vnorm, m_b_ws, m_b_bs, m_b_out, m_f_norm,
                          m_f_up, m_f_conv_w, m_f_conv_b, m_f_down, m_final_norm]))
    vs = dict(zip(names, [v_a_norm, v_a_in, v_a_conv, v_a_out, v_b_norm, v_b_in, v_b_vnorm, v_b_ws, v_b_bs, v_b_out, v_f_norm,
                          v_f_up, v_f_conv_w, v_f_conv_b, v_f_down, v_final_norm]))
    result = {}
    for n in names:
        w = weights[n]
        if n in ("f_up", "f_down"):
            g1, g0 = (g_up1, g_up0) if n == "f_up" else (g_dn1, g_dn0)
            first = _adamw_layer(w, g1, ms[n], vs[n], 1, None, name=f"adamw_{n}1")
            result[n] = _adamw_layer(w, g0, ms[n], vs[n], 0, tuple(first), name=f"adamw_{n}0")
            continue
        if n in early_adamw:
            result[n] = early_adamw[n]
            continue
        g2 = grads[n]
        as3d = (lambda a: a.reshape((1,) + g2.shape))
        result[n] = [o.reshape(w.shape) for o in _adamw_layer(as3d(w), g2, as3d(ms[n]), as3d(vs[n]), 0, None, name=f"adamw_{n}")]

    return (loss, grad_x.reshape(x.shape), *[result[n][0] for n in names], *[result[n][1] for n in names],
            *[result[n][2] for n in names], *[result[n][3] for n in names])
```

```python
import jax
import jax.numpy as jnp
from jax import lax
from jax.experimental import pallas as pl
from jax.experimental.pallas import tpu as pltpu

F32 = jnp.float32
BF16 = jnp.bfloat16
MESH = pl.DeviceIdType.MESH
ANY = pl.BlockSpec(memory_space=pl.ANY)

RMS_EPS = 1e-5
CHUNK = 128
GROUPS = 8
ADAM_LR, ADAM_B1, ADAM_B2, ADAM_EPS, ADAM_WD, ADAM_STEP = 0.001, 0.9, 0.999, 1e-08, 0.01, 10

N_CHIPS = 4
HALO = 8
BF16_HALO = 16
VMEM_LIMIT = 56 * 1024 * 1024
GELU_C = 0.7978845608028654
GELU_A = 0.044715


def _params(sem=None):
    return pltpu.CompilerParams(dimension_semantics=sem, vmem_limit_bytes=VMEM_LIMIT)


def _tile(dim, pref, quantum=128):
    if dim <= pref:
        return dim
    t = (pref // quantum) * quantum
    while t >= quantum:
        if dim % t == 0:
            return t
        t -= quantum
    return dim


def _mm(a, b, *, name, ta=False, tb=False, res=None, norm=None, dep=None, out_dtype=F32, tm=1024, tn=1024, tk=2048):
    (K, M) = a.shape if ta else a.shape[::-1]
    N = b.shape[0] if tb else b.shape[1]
    assert (b.shape[1] if tb else b.shape[0]) == K
    tm, tn, tk = _tile(M, tm), _tile(N, tn), _tile(K, tk)
    nk = K // tk
    assert norm is None or (tn == N and nk == 1)
    a_spec = pl.BlockSpec((tk, tm), lambda i, j, k: (k, i)) if ta else pl.BlockSpec((tm, tk), lambda i, j, k: (i, k))
    b_spec = pl.BlockSpec((tn, tk), lambda i, j, k: (j, k)) if tb else pl.BlockSpec((tk, tn), lambda i, j, k: (k, j))
    o_spec = pl.BlockSpec((tm, tn), lambda i, j, k: (i, j))
    dims = (((0 if ta else 1,), (1 if tb else 0,)), ((), ()))
    direct = out_dtype == F32
    n_in = 2 + (res is not None) + (norm is not None) + (dep is not None)

    def body(*refs):
        a_ref, b_ref = refs[0], refs[1]
        r_ref = refs[2] if res is not None else None
        g_ref = refs[2 + (res is not None)] if norm is not None else None
        o_ref = refs[n_in]
        acc_ref = o_ref if direct else refs[-1]
        part = lax.dot_general(a_ref[...], b_ref[...], dims, preferred_element_type=F32)
        if nk == 1:
            if r_ref is not None:
                part = part + r_ref[...]
            o_ref[...] = part.astype(o_ref.dtype)
            if g_ref is not None:
                r = lax.rsqrt(jnp.mean(part * part, axis=-1, keepdims=True) + RMS_EPS)
                refs[n_in + 1][...] = ((part * r) * g_ref[...]).astype(BF16)
            return
        k = pl.program_id(2)

        @pl.when(k == 0)
        def _():
            acc_ref[...] = part

        @pl.when(jnp.logical_and(k > 0, k < nk - 1))
        def _():
            acc_ref[...] += part

        @pl.when(k == nk - 1)
        def _():
            tot = acc_ref[...] + part
            if r_ref is not None:
                tot = tot + r_ref[...]
            o_ref[...] = tot.astype(o_ref.dtype)

    in_specs = ([a_spec, b_spec] + ([o_spec] if res is not None else [])
                + ([pl.BlockSpec((1, tn), lambda i, j, k: (0, j))] if norm is not None else []) + ([ANY] if dep is not None else []))
    args = (a, b) + tuple(x for x in (res, norm, dep) if x is not None)
    scratch = [] if (direct or nk == 1) else [pltpu.VMEM((tm, tn), F32)]
    out_shape = jax.ShapeDtypeStruct((M, N), out_dtype)
    return pl.pallas_call(
        body, grid=(M // tm, N // tn, nk), in_specs=in_specs, out_specs=[o_spec, o_spec] if norm is not None else o_spec,
        out_shape=[out_shape, jax.ShapeDtypeStruct((M, N), BF16)] if norm is not None else out_shape, scratch_shapes=scratch,
        compiler_params=_params(("parallel", "parallel", "arbitrary")), name=name,
    )(*args)


def _row_spec(rb, w):
    return pl.BlockSpec((rb, w), lambda i: (i, 0))


def _next_spec(rb, w, t):
    return pl.BlockSpec((HALO, w), lambda i: (jnp.minimum((i + 1) * (rb // HALO), t // HALO - 1), 0))


def _prev_spec16(rb, w):
    return pl.BlockSpec((BF16_HALO, w), lambda i: (jnp.maximum(i * (rb // BF16_HALO) - 1, 0), 0))


def _next_spec16(rb, w, t):
    return pl.BlockSpec((BF16_HALO, w), lambda i: (jnp.minimum((i + 1) * (rb // BF16_HALO), t // BF16_HALO - 1), 0))


def _full_spec(shape):
    return pl.BlockSpec(shape, lambda i: tuple(0 for _ in shape))


def _shift(e, s):
    return pltpu.roll(e, s % e.shape[0], 0)


def _gelu(x):
    return 0.5 * x * (1.0 + jnp.tanh(GELU_C * (x + GELU_A * x * x * x)))


def _gelu_and_grad(x):
    x2 = x * x
    th = jnp.tanh(GELU_C * (x + GELU_A * x2 * x))
    half = 0.5 * (1.0 + th)
    return x * half, half + 0.5 * x * (1.0 - th * th) * (GELU_C * (1.0 + 3.0 * GELU_A * x2))


def _sigmoid(x):
    return 1.0 / (1.0 + jnp.exp(-x))


def _rms_fwd(x, g, *, name, dep=None, rb=1024):
    t, d = x.shape
    rb = _tile(t, rb, 8)

    def body(x_ref, g_ref, *rest):
        h_ref = rest[-1]
        xv = x_ref[...]
        r = lax.rsqrt(jnp.mean(xv * xv, axis=-1, keepdims=True) + RMS_EPS)
        h_ref[...] = ((xv * r) * g_ref[...]).astype(BF16)

    return pl.pallas_call(
        body, grid=(t // rb,), in_specs=[_row_spec(rb, d), _full_spec((1, d))] + ([ANY] if dep is not None else []),
        out_specs=_row_spec(rb, d), out_shape=jax.ShapeDtypeStruct((t, d), BF16), compiler_params=_params(("parallel",)), name=name,
    )(x, g, *(() if dep is None else (dep,)))


def _rms_bwd(dh, x, g, dres, *, name, rb=512):
    t, d = x.shape
    rb = _tile(t, rb, 16)

    def body(dh_ref, x_ref, g_ref, dres_ref, dx_ref, dxb_ref, dg_ref):
        xv = x_ref[...]
        r = lax.rsqrt(jnp.mean(xv * xv, axis=-1, keepdims=True) + RMS_EPS)
        xhat = xv * r
        dh_v = dh_ref[...].astype(F32)
        dxhat = dh_v * g_ref[...]
        m = jnp.mean(dxhat * xhat, axis=-1, keepdims=True)
        dx = dres_ref[...] + r * (dxhat - xhat * m)
        dx_ref[...] = dx
        dxb_ref[...] = dx.astype(BF16)

        @pl.when(pl.program_id(0) == 0)
        def _():
            dg_ref[...] = jnp.zeros_like(dg_ref)

        dg_ref[0:1, :] += jnp.sum(dh_v * xhat, axis=0, keepdims=True)

    return pl.pallas_call(
        body, grid=(t // rb,),
        in_specs=[_row_spec(rb, d), _row_spec(rb, d), _full_spec((1, d)), _row_spec(rb, d)],
        out_specs=[_row_spec(rb, d), _row_spec(rb, d), _full_spec((8, d))],
        out_shape=[jax.ShapeDtypeStruct((t, d), F32), jax.ShapeDtypeStruct((t, d), BF16), jax.ShapeDtypeStruct((8, d), F32)],
        compiler_params=_params(("arbitrary",)), name=name,
    )(dh, x, g, dres)


def _final(x, tgt, g, *, name, rb=512):
    t, d = x.shape
    rb = _tile(t, rb, 8)
    inv_d = 1.0 / d

    def body(x_ref, t_ref, g_ref, l_ref, dx_ref, dxb_ref, dg_ref):
        xv = x_ref[...]
        gv = g_ref[...]
        r = lax.rsqrt(jnp.mean(xv * xv, axis=-1, keepdims=True) + RMS_EPS)
        xhat = xv * r
        e = xhat * gv - t_ref[...]
        dy = e * inv_d
        dxhat = dy * gv
        m = jnp.mean(dxhat * xhat, axis=-1, keepdims=True)
        dx = r * (dxhat - xhat * m)
        dx_ref[...] = dx
        dxb_ref[...] = dx.astype(BF16)

        @pl.when(pl.program_id(0) == 0)
        def _():
            l_ref[...] = jnp.zeros_like(l_ref)
            dg_ref[...] = jnp.zeros_like(dg_ref)

        l_ref[0:1, :] += jnp.sum(e * e, axis=0, keepdims=True) * (0.5 * inv_d)
        dg_ref[0:1, :] += jnp.sum(dy * xhat, axis=0, keepdims=True)

    return pl.pallas_call(
        body, grid=(t // rb,),
        in_specs=[_row_spec(rb, d), _row_spec(rb, d), _full_spec((1, d))],
        out_specs=[_full_spec((8, d)), _row_spec(rb, d), _row_spec(rb, d), _full_spec((8, d))],
        out_shape=[jax.ShapeDtypeStruct((8, d), F32), jax.ShapeDtypeStruct((t, d), F32),
                   jax.ShapeDtypeStruct((t, d), BF16), jax.ShapeDtypeStruct((8, d), F32)],
        compiler_params=_params(("arbitrary",)), name=name,
    )(x, tgt, g)


def _a_mid_fwd(bcx, wconv, *, name, rb=256, cw=512):
    t, d3 = bcx.shape
    d = d3 // 3
    rb, cw = _tile(t, rb, 16), _tile(d, cw)

    def body(cur_ref, prev_ref, w_ref, y_ref):
        first = pl.program_id(0) == 0

        def f32(ref, cols):
            return ref[:, cols].astype(F32)

        for c0 in range(0, d, cw):
            cs = slice(c0, c0 + cw)
            gc, xs = slice(d + c0, d + c0 + cw), slice(2 * d + c0, 2 * d + c0 + cw)
            p_prev = jnp.where(first, 0.0, (f32(prev_ref, gc) * f32(prev_ref, xs))[BF16_HALO - HALO:])
            e = jnp.concatenate([p_prev, f32(cur_ref, gc) * f32(cur_ref, xs)], axis=0)
            w = w_ref[:, cs]
            q = w[0:1] * _shift(e, 2) + w[1:2] * _shift(e, 1) + w[2:3] * e
            y_ref[:, cs] = (f32(cur_ref, cs) * q[HALO:]).astype(BF16)

    return pl.pallas_call(
        body, grid=(t // rb,),
        in_specs=[_row_spec(rb, d3), _prev_spec16(rb, d3), _full_spec((3, d))], out_specs=_row_spec(rb, d),
        out_shape=jax.ShapeDtypeStruct((t, d), BF16), compiler_params=_params(("parallel",)), name=name,
    )(bcx, bcx, wconv)


def _a_mid_bwd(bcx, dy, wconv, *, name, rb=256, cw=512):
    t, d3 = bcx.shape
    d = d3 // 3
    rb, cw = _tile(t, rb, 16), _tile(d, cw)

    def body(cur_ref, prev_ref, next_ref, dy_ref, dyn_ref, w_ref, o_ref, dw_ref):
        i = pl.program_id(0)
        first, last = i == 0, i == pl.num_programs(0) - 1

        @pl.when(first)
        def _():
            dw_ref[...] = jnp.zeros_like(dw_ref)

        def f32(ref, cols):
            return ref[:, cols].astype(F32)

        for c0 in range(0, d, cw):
            cs = slice(c0, c0 + cw)
            gc, xs = slice(d + c0, d + c0 + cw), slice(2 * d + c0, 2 * d + c0 + cw)
            zeros = jnp.zeros((HALO, cw), F32)
            gb_c, gc_c, xs_c = f32(cur_ref, cs), f32(cur_ref, gc), f32(cur_ref, xs)
            p_prev = jnp.where(first, 0.0, (f32(prev_ref, gc) * f32(prev_ref, xs))[BF16_HALO - HALO:])
            e = jnp.concatenate([p_prev, gc_c * xs_c, zeros], axis=0)
            dq_next = jnp.where(last, 0.0, dyn_ref[:, cs] * f32(next_ref, cs)[:HALO])
            dy_c = dy_ref[:, cs]
            dq = jnp.concatenate([zeros, dy_c * gb_c, dq_next], axis=0)
            w = w_ref[:, cs]
            e1, e2 = _shift(e, 1), _shift(e, 2)
            q = w[0:1] * e2 + w[1:2] * e1 + w[2:3] * e
            dp = (w[2:3] * dq + w[1:2] * _shift(dq, -1) + w[0:1] * _shift(dq, -2))[HALO:HALO + rb]
            o_ref[:, cs] = (dy_c * q[HALO:HALO + rb]).astype(BF16)
            o_ref[:, gc] = (dp * xs_c).astype(BF16)
            o_ref[:, xs] = (dp * gc_c).astype(BF16)
            dq_c = dq[HALO:HALO + rb]
            dw_ref[0:1, cs] += jnp.sum(dq_c * e2[HALO:HALO + rb], axis=0, keepdims=True)
            dw_ref[1:2, cs] += jnp.sum(dq_c * e1[HALO:HALO + rb], axis=0, keepdims=True)
            dw_ref[2:3, cs] += jnp.sum(dq_c * e[HALO:HALO + rb], axis=0, keepdims=True)

    return pl.pallas_call(
        body, grid=(t // rb,),
        in_specs=[_row_spec(rb, d3), _prev_spec16(rb, d3), _next_spec16(rb, d3, t), _row_spec(rb, d), _next_spec(rb, d, t),
                  _full_spec((3, d))],
        out_specs=[_row_spec(rb, d3), _full_spec((8, d))],
        out_shape=[jax.ShapeDtypeStruct((t, d3), BF16), jax.ShapeDtypeStruct((8, d), F32)],
        compiler_params=_params(("arbitrary",)), name=name,
    )(bcx, bcx, bcx, dy, dy, wconv)


def _ffn_mid_fwd(up, wconv, bconv, *, name, rb=128, cw=256):
    t, f2 = up.shape
    f = f2 // 2
    rb, cw = _tile(t, rb, 16), _tile(f, cw)

    def body(cur_ref, prev_ref, w_ref, b_ref, act_ref, conv_ref):
        first = pl.program_id(0) == 0

        def conv(cols):
            prev = prev_ref[:, cols].astype(F32)[BF16_HALO - HALO:]
            e = jnp.concatenate([jnp.where(first, 0.0, prev), cur_ref[:, cols].astype(F32)], axis=0)
            w = w_ref[:, cols]
            out = (w[0:1] * _shift(e, 2) + w[1:2] * _shift(e, 1) + w[2:3] * e + b_ref[:, cols])[HALO:]
            conv_ref[:, cols] = out.astype(BF16)
            return out

        for c0 in range(0, f, cw):
            g = conv(slice(c0, c0 + cw))
            a = conv(slice(f + c0, f + c0 + cw))
            act_ref[:, c0:c0 + cw] = (g * _sigmoid(g) * a).astype(BF16)

    return pl.pallas_call(
        body, grid=(t // rb,),
        in_specs=[_row_spec(rb, f2), _prev_spec16(rb, f2), _full_spec((3, f2)), _full_spec((1, f2))],
        out_specs=[_row_spec(rb, f), _row_spec(rb, f2)],
        out_shape=[jax.ShapeDtypeStruct((t, f), BF16), jax.ShapeDtypeStruct((t, f2), BF16)],
        compiler_params=_params(("parallel",)), name=name,
    )(up, up, wconv, bconv)


def _ffn_mid_bwd(up, conv, dact, wconv, *, name, rb=128, cw=256):
    t, f2 = up.shape
    f = f2 // 2
    rb, cw = _tile(t, rb, 16), _tile(f, cw)

    def body(up_ref, conv_ref, convn_ref, da_ref, dan_ref, w_ref, o_ref, dwb_ref):
        i = pl.program_id(0)
        last = i == pl.num_programs(0) - 1

        @pl.when(i == 0)
        def _():
            dwb_ref[...] = jnp.zeros_like(dwb_ref)

        def rows(cols):
            return jnp.concatenate([conv_ref[:, cols].astype(F32), convn_ref[:, cols].astype(F32)[0:HALO]], axis=0)

        def back(dc, cols):
            w = w_ref[:, cols]
            dc1, dc2 = _shift(dc, -1)[:rb], _shift(dc, -2)[:rb]
            dc0 = dc[:rb]
            o_ref[:, cols] = (w[2:3] * dc0 + w[1:2] * dc1 + w[0:1] * dc2).astype(BF16)
            u = up_ref[:, cols].astype(F32)
            ones = jnp.ones((8, rb), BF16)
            for k, prod in enumerate((dc2 * u, dc1 * u, dc0 * u, dc0)):
                dwb_ref[k:k + 1, cols] += jnp.dot(ones, prod.astype(BF16), preferred_element_type=F32)[0:1]

        for c0 in range(0, f, cw):
            gcols, acols = slice(c0, c0 + cw), slice(f + c0, f + c0 + cw)
            g, a = rows(gcols), rows(acols)
            da = jnp.concatenate([da_ref[:, gcols], jnp.where(last, 0.0, dan_ref[:, gcols])], axis=0)
            sg = _sigmoid(g)
            back(da * a * (sg * (1.0 + g * (1.0 - sg))), gcols)
            back(da * (g * sg), acols)

    return pl.pallas_call(
        body, grid=(t // rb,),
        in_specs=[_row_spec(rb, f2), _row_spec(rb, f2), _next_spec16(rb, f2, t), _row_spec(rb, f), _next_spec(rb, f, t),
                  _full_spec((3, f2))],
        out_specs=[_row_spec(rb, f2), _full_spec((8, f2))],
        out_shape=[jax.ShapeDtypeStruct((t, f2), BF16), jax.ShapeDtypeStruct((8, f2), F32)],
        compiler_params=_params(("arbitrary",)), name=name,
    )(up, conv, conv, dact, dact, wconv)


def _causal_mask():
    return lax.broadcasted_iota(jnp.int32, (CHUNK, CHUNK), 0) >= lax.broadcasted_iota(jnp.int32, (CHUNK, CHUNK), 1)


def _b_mid_fwd(zp, vnorm, ws, bs, *, name, rb=512):
    t, d2 = zp.shape
    d = d2 // 2
    c = d // GROUPS
    rb = _tile(t, rb, CHUNK)

    def body(zp_ref, gv_ref, ws_ref, bs_ref, ug_ref, vn_ref, gate_ref):
        v = _gelu(zp_ref[:, d:].astype(F32))
        rv = lax.rsqrt(jnp.mean(v * v, axis=-1, keepdims=True) + RMS_EPS)
        vn_ref[...] = ((v * rv) * gv_ref[...]).astype(BF16)
        mask = _causal_mask()
        for h in range(GROUPS):
            hc = slice(h * c, (h + 1) * c)
            wm = jnp.where(mask, ws_ref[h], 0.0).astype(BF16)
            bcol = jnp.broadcast_to(bs_ref[h:h + 1, :], (CHUNK, CHUNK)).T[:, 0:1]
            for n in range(rb // CHUNK):
                rows = slice(n * CHUNK, (n + 1) * CHUNK)
                gate_ref[rows, hc] = jnp.dot(wm, vn_ref[rows, hc], preferred_element_type=F32) + bcol
        ug_ref[...] = (_gelu(zp_ref[:, :d].astype(F32)) * gate_ref[...]).astype(BF16)

    return pl.pallas_call(
        body, grid=(t // rb,),
        in_specs=[_row_spec(rb, d2), _full_spec((1, d)), _full_spec((GROUPS, CHUNK, CHUNK)), _full_spec((GROUPS, CHUNK))],
        out_specs=_row_spec(rb, d), out_shape=jax.ShapeDtypeStruct((t, d), BF16),
        scratch_shapes=[pltpu.VMEM((rb, d), BF16), pltpu.VMEM((rb, d), F32)],
        compiler_params=_params(("parallel",)), name=name,
    )(zp, vnorm, ws, bs)


def _b_mid_bwd(zp, dug, vnorm, ws, bs, *, name, rb=512):
    t, d2 = zp.shape
    d = d2 // 2
    c = d // GROUPS
    rb = _tile(t, rb, CHUNK)

    def body(zp_ref, dug_ref, gv_ref, ws_ref, bs_ref, dzp_ref, dws_ref, dbs_ref, dgv_ref,
             vn_ref, gate_ref, dm_ref, dvn_ref, dbacc_ref):
        i = pl.program_id(0)

        @pl.when(i == 0)
        def _():
            dws_ref[...] = jnp.zeros_like(dws_ref)
            dgv_ref[...] = jnp.zeros_like(dgv_ref)
            dbacc_ref[...] = jnp.zeros_like(dbacc_ref)

        zu, zv = zp_ref[:, :d].astype(F32), zp_ref[:, d:].astype(F32)
        (u, u_grad), (v, v_grad) = _gelu_and_grad(zu), _gelu_and_grad(zv)
        rv = lax.rsqrt(jnp.mean(v * v, axis=-1, keepdims=True) + RMS_EPS)
        vhat = v * rv
        gv = gv_ref[...]
        vn_ref[...] = (vhat * gv).astype(BF16)
        dug_v = dug_ref[...].astype(F32)
        dm = dug_v * u
        dm_ref[...] = dm.astype(BF16)
        mask = _causal_mask()
        for h in range(GROUPS):
            hc = slice(h * c, (h + 1) * c)
            wm = jnp.where(mask, ws_ref[h], 0.0)
            wm_b, wmt_b = wm.astype(BF16), wm.T.astype(BF16)
            bcol = jnp.broadcast_to(bs_ref[h:h + 1, :], (CHUNK, CHUNK)).T[:, 0:1]
            dws_h = jnp.zeros((CHUNK, CHUNK), F32)
            dbs_h = jnp.zeros((CHUNK, c), F32)
            for n in range(rb // CHUNK):
                rows = slice(n * CHUNK, (n + 1) * CHUNK)
                vn_c, dm_c = vn_ref[rows, hc], dm_ref[rows, hc]
                gate_ref[rows, hc] = jnp.dot(wm_b, vn_c, preferred_element_type=F32) + bcol
                dws_h += lax.dot_general(dm_c, vn_c, (((1,), (1,)), ((), ())), preferred_element_type=F32)
                dvn_ref[rows, hc] = jnp.dot(wmt_b, dm_c, preferred_element_type=F32)
                dbs_h += dm[rows, hc]
            dws_ref[h] += dws_h
            dbacc_ref[h] += dbs_h
        du = dug_v * gate_ref[...]
        dvn = dvn_ref[...]
        dvhat = dvn * gv
        m = jnp.mean(dvhat * vhat, axis=-1, keepdims=True)
        dv = rv * (dvhat - vhat * m)
        dgv_ref[0:1, :] += jnp.sum(dvn * vhat, axis=0, keepdims=True)
        dzp_ref[:, :d] = (du * u_grad).astype(BF16)
        dzp_ref[:, d:] = (dv * v_grad).astype(BF16)

        @pl.when(i == pl.num_programs(0) - 1)
        def _():
            ones = jnp.ones((8, c), F32)
            for h in range(GROUPS):
                dws_ref[h] = jnp.where(mask, dws_ref[h], 0.0)
                row = lax.dot_general(ones, dbacc_ref[h], (((1,), (1,)), ((), ())),
                                      precision=lax.Precision.HIGHEST, preferred_element_type=F32)
                dbs_ref[h:h + 1, :] = row[0:1]

    return pl.pallas_call(
        body, grid=(t // rb,),
        in_specs=[_row_spec(rb, d2), _row_spec(rb, d), _full_spec((1, d)), _full_spec((GROUPS, CHUNK, CHUNK)),
                  _full_spec((GROUPS, CHUNK))],
        out_specs=[_row_spec(rb, d2), _full_spec((GROUPS, CHUNK, CHUNK)), _full_spec((GROUPS, CHUNK)), _full_spec((8, d))],
        out_shape=[jax.ShapeDtypeStruct((t, d2), BF16), jax.ShapeDtypeStruct((GROUPS, CHUNK, CHUNK), F32),
                   jax.ShapeDtypeStruct((GROUPS, CHUNK), F32), jax.ShapeDtypeStruct((8, d), F32)],
        scratch_shapes=[pltpu.VMEM((rb, d), BF16), pltpu.VMEM((rb, d), F32), pltpu.VMEM((rb, d), BF16),
                        pltpu.VMEM((rb, d), F32), pltpu.VMEM((GROUPS, CHUNK, c), F32)],
        compiler_params=_params(("arbitrary",)), name=name,
    )(zp, dug, vnorm, ws, bs)


def _cast_into_full(w, layer, kind, place, *, name, dep=None, rb=512):
    _, r, c = w.shape
    rb = _tile(r, rb, 16)
    nrb = r // rb
    full = (r, c * N_CHIPS) if kind == "col" else (r * N_CHIPS, c)

    def body(place_ref, w_ref, *rest):
        rest[-1][...] = w_ref[...].astype(BF16)

    def o_index(i, place):
        return (i, place[0]) if kind == "col" else (i + place[0] * nrb, 0)

    in_specs = [pl.BlockSpec((None, rb, c), lambda i, place: (layer, i, 0))] + ([ANY] if dep is not None else [])
    return pl.pallas_call(
        body,
        grid_spec=pltpu.PrefetchScalarGridSpec(num_scalar_prefetch=1, grid=(nrb,), in_specs=in_specs,
                                               out_specs=pl.BlockSpec((rb, c), o_index)),
        out_shape=jax.ShapeDtypeStruct(full, BF16), compiler_params=_params(("parallel",)), name=name,
    )(place, w, *(() if dep is None else (dep,)))


def _adamw_layer(w, g, m, v, layer, prev, *, name, rb=256):
    _, r, c = w.shape
    rb = _tile(r, rb, 8)
    c1 = 1.0 - ADAM_B1 ** ADAM_STEP
    c2 = 1.0 - ADAM_B2 ** ADAM_STEP

    def body(w_ref, g_ref, m_ref, v_ref, *rest):
        go_ref, d_ref, nm_ref, nv_ref = rest[-4:]
        gv = g_ref[...]
        nm = ADAM_B1 * m_ref[...] + (1.0 - ADAM_B1) * gv
        nv = ADAM_B2 * v_ref[...] + (1.0 - ADAM_B2) * (gv * gv)
        go_ref[...] = gv
        nm_ref[...] = nm
        nv_ref[...] = nv
        d_ref[...] = -ADAM_LR * ((nm / c1) / (jnp.sqrt(nv / c2) + ADAM_EPS) + ADAM_WD * w_ref[...])

    lay = pl.BlockSpec((None, rb, c), lambda i: (layer, i, 0))
    return pl.pallas_call(
        body, grid=(r // rb,), in_specs=[lay, _row_spec(rb, c), lay, lay] + ([ANY] * 4 if prev else []), out_specs=[lay] * 4,
        out_shape=[jax.ShapeDtypeStruct(w.shape, F32)] * 4, input_output_aliases={4 + k: k for k in range(4)} if prev else {},
        compiler_params=_params(("parallel",)), name=name,
    )(w, g, m, v, *(prev or ()))


HBM = pl.BlockSpec(memory_space=pltpu.HBM)
SEM = pl.BlockSpec(memory_space=pltpu.SEMAPHORE)
SIDE_EFFECT = pltpu.SideEffectType.DATAFLOW_SIDE_EFFECTING


def _place():
    x, y, c = lax.axis_index("x"), lax.axis_index("y"), lax.axis_index("c")
    chips = [(1 - x, y), (x, 1 - y), (1 - x, 1 - y)]
    return x, y, c, 2 * x + y, chips


def _half(ref, kind, c):
    r, w = ref.shape
    if kind == "col":
        return ref.at[pl.ds(pl.multiple_of(c * (r // 2), 8), r // 2), :]
    return ref.at[:, pl.ds(pl.multiple_of(c * (w // 2), 128), w // 2)]


def _shard(ref, kind, s):
    r, w = ref.shape
    if kind == "col":
        return ref.at[:, pl.ds(pl.multiple_of(s * (w // N_CHIPS), 128), w // N_CHIPS)]
    return ref.at[pl.ds(pl.multiple_of(s * (r // N_CHIPS), 8), r // N_CHIPS), :]


def _remote(src, dst, send_sem, recv_sem, dev):
    return pltpu.make_async_remote_copy(src_ref=src, dst_ref=dst, send_sem=send_sem, recv_sem=recv_sem,
                                        device_id=dev, device_id_type=MESH)


def _start(name, bufs, plan, sem_shape, dep=None):
    n = len(bufs)
    n_in = n + (dep is not None)

    def body(*refs):
        sends, _ = plan(refs[:n], refs[n_in], refs[n_in + 1])
        for cp in sends:
            cp.start()
        refs[n_in + 2 + n][...] = jnp.zeros((8, 128), F32)

    dma = pltpu.SemaphoreType.DMA
    outs = pl.pallas_call(
        body, name=name,
        out_shape=(dma(sem_shape), dma(sem_shape), *[pltpu.HBM(b.shape, b.dtype) for b in bufs], jax.ShapeDtypeStruct((8, 128), F32)),
        in_specs=(HBM,) * n + ((ANY,) if dep is not None else ()),
        out_specs=(SEM, SEM) + (HBM,) * n + (pl.BlockSpec(memory_space=pltpu.VMEM),),
        input_output_aliases={i: i + 2 for i in range(n)},
        compiler_params=pltpu.CompilerParams(has_side_effects=SIDE_EFFECT),
    )(*[pltpu.with_memory_space_constraint(b, pltpu.HBM) for b in bufs], *(() if dep is None else (dep,)))
    return outs[0], outs[1], list(outs[2:2 + n]), outs[2 + n]


def _wait(name, started, plan, after):
    send, recv, bufs, _ = started
    n = len(bufs)

    def body(*refs):
        sends, recvs = plan(refs[:n], refs[n], refs[n + 1])
        for cp in sends:
            cp.wait_send()
        for cp in recvs:
            cp.wait_recv()

    return list(pl.pallas_call(
        body, name=name, out_shape=tuple(pltpu.HBM(b.shape, b.dtype) for b in bufs),
        in_specs=(HBM,) * n + (SEM, SEM, ANY), out_specs=(HBM,) * n, input_output_aliases={i: i for i in range(n)},
        compiler_params=pltpu.CompilerParams(has_side_effects=SIDE_EFFECT),
    )(*bufs, send, recv, after))


KINDS = ("col", "row")


def _gather_ici_plan(n_small):
    def plan(refs, send, recv):
        x, y, c, s, chips = _place()
        n = len(KINDS) + n_small
        sends, recvs = [], []
        for k, (px, py) in enumerate(chips):
            sp = 2 * px + py
            for a, kind in enumerate(KINDS):
                mine, theirs = _half(_shard(refs[a], kind, s), kind, c), _half(_shard(refs[a], kind, sp), kind, c)
                sends.append(_remote(mine, mine, send.at[k * n + a], recv.at[k * n + a], (px, py, c)))
                recvs.append(_remote(theirs, theirs, send.at[k * n + a], recv.at[k * n + a], (px, py, c)))
            for b in range(n_small):
                ref, sem = refs[len(KINDS) + b], k * n + len(KINDS) + b
                sends.append(_remote(ref.at[s], ref.at[s], send.at[sem], recv.at[sem], (px, py, c)))
                recvs.append(_remote(ref.at[sp], ref.at[sp], send.at[sem], recv.at[sem], (px, py, c)))
        return sends, recvs
    return plan


def _gather_d2d_plan(refs, send, recv):
    x, y, c, _, chips = _place()
    n = len(KINDS)
    sends, recvs = [], []
    for k, (px, py) in enumerate(chips):
        for a, kind in enumerate(KINDS):
            region, sem = _shard(refs[a], kind, 2 * px + py), k * n + a
            sends.append(_remote(_half(region, kind, c), _half(region, kind, c), send.at[sem], recv.at[sem], (x, y, 1 - c)))
            recvs.append(_remote(_half(region, kind, 1 - c), _half(region, kind, 1 - c), send.at[sem], recv.at[sem], (x, y, 1 - c)))
    return sends, recvs


def _swap_plan(refs, send, recv):
    x, y, c, _, _ = _place()
    n = len(KINDS)
    cps = [_remote(_half(refs[a], KINDS[a], 1 - c), refs[n + a], send.at[a], recv.at[a], (x, y, 1 - c)) for a in range(n)]
    return cps, cps


def _exchange_plan(refs, send, recv):
    x, y, c, _, chips = _place()
    n = len(KINDS)
    cps = []
    for k, (px, py) in enumerate(chips):
        for a in range(n):
            cps.append(_remote(_shard(refs[a], KINDS[a], 2 * px + py), refs[n + a].at[k], send.at[k * n + a], recv.at[k * n + a],
                               (px, py, c)))
    return cps, cps


def _share_plan(refs, send, recv):
    x, y, c, _, _ = _place()
    sends = [_remote(_half(refs[a], KINDS[a], c), _half(refs[a], KINDS[a], c), send.at[a], recv.at[a], (x, y, 1 - c))
             for a in range(len(KINDS))]
    recvs = [_remote(_half(refs[a], KINDS[a], 1 - c), _half(refs[a], KINDS[a], 1 - c), send.at[a], recv.at[a], (x, y, 1 - c))
             for a in range(len(KINDS))]
    return sends, recvs


def _spread_plan(refs, send, recv):
    packed, slots = refs
    x, y, c, _, _ = _place()
    sends, recvs = [], []
    for k in range(1, 8):
        px, py, pc = x ^ (k >> 2), y ^ ((k >> 1) & 1), c ^ (k & 1)
        sends.append(_remote(packed, slots.at[4 * x + 2 * y + c], send.at[k - 1], recv.at[k - 1], (px, py, pc)))
        recvs.append(_remote(packed, slots.at[4 * px + 2 * py + pc], send.at[k - 1], recv.at[k - 1], (px, py, pc)))
    return sends, recvs


def _half_index(kind, nblk):
    def index(i, j, place):
        return (i + place[1] * nblk[0], j) if kind == "col" else (i, j + place[1] * nblk[1])
    return index


def _chip_partial(g, other, kind, place, *, name):
    hr, hc = other.shape
    col = kind == "col"
    rb = _tile(hr if col else hr // N_CHIPS, 512, 16)
    cb = _tile(hc // N_CHIPS if col else hc, 1408)
    nblk = (hr // rb, hc // cb)
    own = (nblk[1] if col else nblk[0]) // N_CHIPS

    def block(i, j, place):
        s = place[0]
        return (i, j + jnp.where(j >= s * own, own, 0)) if col else (i + jnp.where(i >= s * own, own, 0), j)

    def body(place_ref, g_ref, o_ref, p_ref):
        p_ref[...] = (g_ref[...].astype(F32) + o_ref[...].astype(F32)).astype(BF16)

    plain = pl.BlockSpec((rb, cb), block)
    in_half = pl.BlockSpec((rb, cb), lambda i, j, place: _half_index(kind, nblk)(*block(i, j, place), place))
    grid = (nblk[0], nblk[1] - own) if col else (nblk[0] - own, nblk[1])
    return pl.pallas_call(
        body,
        grid_spec=pltpu.PrefetchScalarGridSpec(num_scalar_prefetch=1, grid=grid, in_specs=[in_half, plain], out_specs=plain),
        out_shape=jax.ShapeDtypeStruct((hr, hc), BF16), compiler_params=_params(("parallel", "parallel")), name=name,
    )(place, g, other)


def _reduce_half(g, other, recv, kind, place, *, name):
    _, pr, pc = recv.shape
    rb, cb = _tile(pr, 512, 16), _tile(pc, 1408)
    nblk = (pr // rb, pc // cb)
    full = (pr * 2, pc) if kind == "col" else (pr, pc * 2)

    def g_index(i, j, place):
        s, c = place[0], place[1]
        return (i + c * nblk[0], j + s * nblk[1]) if kind == "col" else (i + s * nblk[0], j + c * nblk[1])

    def o_index(i, j, place):
        return (i, j + place[0] * nblk[1]) if kind == "col" else (i + place[0] * nblk[0], j)

    def body(place_ref, g_ref, o_ref, r_ref, out_ref):
        acc = g_ref[...].astype(F32) + o_ref[...].astype(F32)
        for k in range(3):
            acc = acc + r_ref[k].astype(F32)
        out_ref[...] = acc

    return pl.pallas_call(
        body,
        grid_spec=pltpu.PrefetchScalarGridSpec(
            num_scalar_prefetch=1, grid=nblk,
            in_specs=[pl.BlockSpec((rb, cb), g_index), pl.BlockSpec((rb, cb), o_index),
                      pl.BlockSpec((3, rb, cb), lambda i, j, place: (0, i, j))],
            out_specs=pl.BlockSpec((rb, cb), _half_index(kind, nblk))),
        out_shape=jax.ShapeDtypeStruct(full, F32), compiler_params=_params(("parallel", "parallel")), name=name,
    )(place, g, other, recv)


def _sum_slots(packed, slots, me, *, name, rb=512):
    r, w = packed.shape
    rb = _tile(r, rb, 8)

    def body(me_ref, p_ref, s_ref, o_ref):
        acc = None
        for j in range(8):
            term = jnp.where(me_ref[0] == j, p_ref[...], s_ref[j])
            acc = term if acc is None else acc + term
        o_ref[...] = acc

    return pl.pallas_call(
        body,
        grid_spec=pltpu.PrefetchScalarGridSpec(
            num_scalar_prefetch=1, grid=(r // rb,),
            in_specs=[pl.BlockSpec((rb, w), lambda i, me: (i, 0)), pl.BlockSpec((8, rb, w), lambda i, me: (0, i, 0))],
            out_specs=pl.BlockSpec((rb, w), lambda i, me: (i, 0))),
        out_shape=jax.ShapeDtypeStruct((r, w), F32), compiler_params=_params(("parallel",)), name=name,
    )(me, packed, slots)


def _half_shape(a, kind):
    return (a.shape[0] // 2, a.shape[1]) if kind == "col" else (a.shape[0], a.shape[1] // 2)


def _rs_swap(tag, grads):
    others = [lax.empty(_half_shape(g, k), g.dtype) for g, k in zip(grads, KINDS)]
    return _start(f"rs_{tag}_swap", list(grads) + others, _swap_plan, (len(KINDS),))


def _rs_exchange(tag, swapped, place, after):
    bufs = _wait(f"rs_{tag}_swap_wait", swapped, _swap_plan, after)
    n = len(KINDS)
    grads, others = bufs[:n], bufs[n:]
    parts = [_chip_partial(g, o, k, place, name=f"rs_{tag}_partial_{k}") for g, o, k in zip(grads, others, KINDS)]
    lands = []
    for p, k in zip(parts, KINDS):
        piece = (p.shape[0], p.shape[1] // N_CHIPS) if k == "col" else (p.shape[0] // N_CHIPS, p.shape[1])
        lands.append(lax.empty((3,) + piece, p.dtype))
    return _start(f"rs_{tag}_exchange", parts + lands, _exchange_plan, (3 * n,)), grads, others


def _rs_share(tag, exchanged, place, after):
    started, grads, others = exchanged
    n = len(KINDS)
    recvs = _wait(f"rs_{tag}_exchange_wait", started, _exchange_plan, after)[n:]
    halves = [_reduce_half(g, o, r, k, place, name=f"rs_{tag}_reduce_{k}") for g, o, r, k in zip(grads, others, recvs, KINDS)]
    return _start(f"rs_{tag}_share", halves, _share_plan, (n,))


def _rs_finish(tag, shared, after):
    return _wait(f"rs_{tag}_share_wait", shared, _share_plan, after)


def _spread(tag, parts, dep):
    rows = [p.reshape(-1, 128) for p in parts]
    n = sum(r.shape[0] for r in rows)
    rows.append(jnp.zeros(((-n) % 512, 128), F32))
    packed = jnp.concatenate(rows, axis=0)
    return _start(f"small_{tag}_spread", [packed, lax.empty((8,) + packed.shape, F32)], _spread_plan, (7,), dep=dep)


def _spread_sum(tag, started, parts, me, after):
    packed, slots = _wait(f"small_{tag}_spread_wait", started, _spread_plan, after)
    total = _sum_slots(packed, slots, me, name=f"small_{tag}_sum")
    out, row = [], 0
    for p in parts:
        n = p.size // 128
        out.append(total[row:row + n].reshape(p.shape))
        row += n
    return out


def _ffn_fwd(x, h, w_up, conv_w, conv_b, w_down, tag, tm_up=1024):
    up = _mm(h, w_up, tm=tm_up, out_dtype=BF16, name=f"ffn{tag}_up")
    act, conv = _ffn_mid_fwd(up, conv_w, conv_b, name=f"ffn{tag}_mid")
    x_out = _mm(act, w_down, res=x, tk=2816, name=f"ffn{tag}_down")
    return x_out, (up, conv, act)


def kernel(x, a_norm, a_in, a_conv, a_out, b_norm, b_in, b_vnorm, b_ws, b_bs, b_out, f_norm, f_up, f_conv_w, f_conv_b, f_down, final_norm, loss_target, m_a_norm, m_a_in, m_a_conv, m_a_out, m_b_norm, m_b_in, m_b_vnorm, m_b_ws, m_b_bs, m_b_out, m_f_norm, m_f_up, m_f_conv_w, m_f_conv_b, m_f_down, m_final_norm, v_a_norm, v_a_in, v_a_conv, v_a_out, v_b_norm, v_b_in, v_b_vnorm, v_b_ws, v_b_bs, v_b_out, v_f_norm, v_f_up, v_f_conv_w, v_f_conv_b, v_f_down, v_final_norm):
    t, d = x.shape[1], x.shape[2]
    f2 = f_up.shape[2] * N_CHIPS
    x0, tgt = x.reshape(t, d), loss_target.reshape(t, d)
    ax, ay, ac = lax.axis_index("x"), lax.axis_index("y"), lax.axis_index("c")
    s = 2 * ax + ay
    place = jnp.stack([s, ac]).astype(jnp.int32)
    me = (4 * ax + 2 * ay + ac).astype(jnp.int32).reshape(1)

    def stacked(a):
        return lax.dynamic_update_index_in_dim(jnp.zeros((N_CHIPS,) + a.shape, F32), a, s, 0)

    def cast_pair(tag, w_in, w_out, layer, dep):
        return [_cast_into_full(w_in, layer, "col", place, name=f"cast_{tag}_in", dep=dep),
                _cast_into_full(w_out, layer, "row", place, name=f"cast_{tag}_out", dep=dep)]

    def gather_start(tag, fulls, small, dep):
        return _start(f"ag_{tag}_ici", fulls + small, _gather_ici_plan(len(small)), (3 * (2 + len(small)),), dep=dep)

    def gather_forward(tag, started, n_small, after):
        bufs = _wait(f"ag_{tag}_ici_wait", started, _gather_ici_plan(n_small), after)
        return _start(f"ag_{tag}_d2d", bufs[:2], _gather_d2d_plan, (3 * 2,)), bufs[2:]

    def gather_finish(tag, forwarded, after):
        return _wait(f"ag_{tag}_d2d_wait", forwarded, _gather_d2d_plan, after)

    small = [stacked(a_conv[0]), stacked(b_norm), stacked(b_vnorm), stacked(f_conv_w.reshape(2 * 3, -1))]
    ag_a = gather_start("a", cast_pair("a", a_in, a_out, 0, None), small, None)
    full_f0 = cast_pair("f0", f_up, f_down, 0, ag_a[3])
    full_b = cast_pair("b", b_in, b_out, 0, full_f0[1])
    full_f1 = cast_pair("f1", f_up, f_down, 1, full_b[1])

    def unshard(a):
        return jnp.transpose(a, (1, 0, 2)).reshape(a.shape[1], -1)

    ws, bs = b_ws[0], b_bs[0]

    h0 = _rms_fwd(x0, a_norm, dep=full_f1[1], name="a_norm")
    fw_a, (g_aconv, g_bnorm, g_bvnorm, g_fconv) = gather_forward("a", ag_a, 4, h0)
    ag_f0 = gather_start("f0", full_f0, [], fw_a[3])
    ag_b = gather_start("b", full_b, [], ag_f0[3])
    ag_f1 = gather_start("f1", full_f1, [], ag_b[3])
    w_ai, w_ao = gather_finish("a", fw_a, ag_f1[3])
    a_conv_f, b_norm_f, b_vnorm_f = unshard(g_aconv), unshard(g_bnorm), unshard(g_bvnorm)
    f_conv_f = unshard(g_fconv).reshape(2, 3, f2)
    bcx = _mm(h0, w_ai, tm=2048, out_dtype=BF16, name="a_in")
    y = _a_mid_fwd(bcx, a_conv_f, name="a_mid")
    x1 = _mm(y, w_ao, res=x0, tm=512, tn=2048, name="a_out")
    fw_f0, _ = gather_forward("f0", ag_f0, 0, x1)
    h1 = _rms_fwd(x1, f_norm[0:1], dep=fw_f0[3], name="ffn0_norm")
    w_up0, w_dn0 = gather_finish("f0", fw_f0, h1)
    x2, (up0, conv0, act0) = _ffn_fwd(x1, h1, w_up0, f_conv_f[0], f_conv_b[0:1], w_dn0, 0, tm_up=2048)
    fw_b, _ = gather_forward("b", ag_b, 0, up0)
    w_bi, w_bo = gather_finish("b", fw_b, act0)
    h2 = _rms_fwd(x2, b_norm_f, name="b_norm")
    zp = _mm(h2, w_bi, tm=2048, out_dtype=BF16, name="b_in")
    fw_f1, _ = gather_forward("f1", ag_f1, 0, zp)
    ug = _b_mid_fwd(zp, b_vnorm_f, ws, bs, name="b_mid")
    x3, h3 = _mm(ug, w_bo, res=x2, norm=f_norm[1:2], tm=512, tn=2048, name="b_out")
    w_up1, w_dn1 = gather_finish("f1", fw_f1, x3)
    x4, (up1, conv1, act1) = _ffn_fwd(x3, h3, w_up1, f_conv_f[1], f_conv_b[1:2], w_dn1, 1, tm_up=2048)
    loss_rows, dx4, dx4b, d_final = _final(x4, tgt, final_norm.reshape(1, d), name="final")

    d_dn1 = _mm(act1, dx4b, ta=True, tm=1408, out_dtype=BF16, name="ffn1_ddown")
    dact1 = _mm(dx4b, w_dn1, tb=True, tn=512, tm=2048, name="ffn1_dact")
    dup1, d_fwb1 = _ffn_mid_bwd(up1, conv1, dact1, f_conv_f[1], name="ffn1_mid_bwd")
    d_up1 = _mm(h3, dup1, ta=True, out_dtype=BF16, tk=4096, name="ffn1_dup")
    sw_f1 = _rs_swap("f1", [d_up1, d_dn1])
    dh3 = _mm(dup1, w_up1, tb=True, tk=2816, out_dtype=BF16, dep=sw_f1[3], name="ffn1_dh")
    dx3, dx3b, d_fnorm1 = _rms_bwd(dh3, x3, f_norm[1:2], dx4, name="ffn1_norm_bwd")
    ex_f1 = _rs_exchange("f1", sw_f1, place, dx3)

    d_bo = _mm(ug, dx3b, ta=True, out_dtype=BF16, tk=4096, dep=ex_f1[0][3], name="b_dout")
    dug = _mm(dx3b, w_bo, tb=True, tm=512, tn=2048, out_dtype=BF16, name="b_dug")
    dzp, d_ws, d_bs, d_bvnorm = _b_mid_bwd(zp, dug, b_vnorm_f, ws, bs, name="b_mid_bwd")
    d_bi = _mm(h2, dzp, ta=True, out_dtype=BF16, tk=4096, name="b_din")
    sw_b = _rs_swap("b", [d_bi, d_bo])
    dh2 = _mm(dzp, w_bi, tb=True, tk=4096, out_dtype=BF16, dep=sw_b[3], name="b_dh")
    dx2, dx2b, d_bnorm = _rms_bwd(dh2, x2, b_norm_f, dx3, name="b_norm_bwd")
    ex_b = _rs_exchange("b", sw_b, place, dx2)

    d_dn0 = _mm(act0, dx2b, ta=True, tm=1408, out_dtype=BF16, dep=ex_b[0][3], name="ffn0_ddown")
    dact0 = _mm(dx2b, w_dn0, tb=True, tn=512, tm=2048, name="ffn0_dact")
    dup0, d_fwb0 = _ffn_mid_bwd(up0, conv0, dact0, f_conv_f[0], name="ffn0_mid_bwd")
    sh_f1 = _rs_share("f1", ex_f1, place, dup0)
    d_up0 = _mm(h1, dup0, ta=True, out_dtype=BF16, tk=4096, dep=sh_f1[3], name="ffn0_dup")
    sw_f0 = _rs_swap("f0", [d_up0, d_dn0])
    g_up1, g_dn1 = _rs_finish("f1", sh_f1, sw_f0[3])
    dh1 = _mm(dup0, w_up0, tb=True, tk=2816, out_dtype=BF16, dep=sw_f0[3], name="ffn0_dh")
    dx1, dx1b, d_fnorm0 = _rms_bwd(dh1, x1, f_norm[0:1], dx2, name="ffn0_norm_bwd")
    ex_f0 = _rs_exchange("f0", sw_f0, place, dx1)
    early = [jnp.concatenate([d_bnorm, d_bvnorm, d_fnorm0, d_fnorm1, d_final, loss_rows], axis=0),
             jnp.concatenate([d_fwb0, d_fwb1], axis=0), jnp.concatenate([d_ws.reshape(-1, CHUNK), d_bs], axis=0)]
    sp_early = _spread("early", early, ex_f0[0][3])
    sh_b = _rs_share("b", ex_b, place, sp_early[3])

    d_ao = _mm(y, dx1b, ta=True, out_dtype=BF16, tk=4096, dep=sh_b[3], name="a_dout")
    dyy = _mm(dx1b, w_ao, tb=True, tm=512, tn=2048, name="a_dy")
    dbcx, d_aconv = _a_mid_bwd(bcx, dyy, a_conv_f, name="a_mid_bwd")
    d_ai = _mm(h0, dbcx, ta=True, out_dtype=BF16, tk=4096, name="a_din")
    sw_a = _rs_swap("a", [d_ai, d_ao])
    g_bi, g_bo = _rs_finish("b", sh_b, sw_a[3])
    early_adamw = {"b_in": _adamw_layer(b_in, g_bi, m_b_in, v_b_in, 0, None, name="adamw_b_in"),
                   "b_out": _adamw_layer(b_out, g_bo, m_b_out, v_b_out, 0, None, name="adamw_b_out")}
    ex_a = _rs_exchange("a", sw_a, place, early_adamw["b_in"][1][0, :8, :128] + early_adamw["b_out"][1][0, :8, :128])
    dh0 = _mm(dbcx, w_ai, tb=True, tk=3072, out_dtype=BF16, dep=ex_a[0][3], name="a_dh")
    grad_x, _, d_anorm = _rms_bwd(dh0, x0, a_norm, dx1, name="a_norm_bwd")
    late = [jnp.concatenate([d_anorm, d_aconv], axis=0)]
    sp_late = _spread("late", late, ex_a[0][3])
    sh_f0 = _rs_share("f0", ex_f0, place, sp_late[3])
    sh_a = _rs_share("a", ex_a, place, sh_f0[3])
    g_up0, g_dn0 = _rs_finish("f0", sh_f0, sh_a[3])
    g_ai, g_ao = _rs_finish("a", sh_a, g_up0)
    r_a, r_b, r_c = _spread_sum("early", sp_early, early, me, g_ai)
    (r_l,) = _spread_sum("late", sp_late, late, me, r_a)

    loss = jnp.sum(r_a[40])
    cs, fs = d // N_CHIPS, f2 // N_CHIPS

    def mine(a, width):
        return lax.dynamic_slice_in_dim(a, s * width, width, axis=1)

    grads = {
        "a_norm": r_l[0:1], "a_conv": mine(r_l[8:11], cs), "b_norm": mine(r_a[0:1], cs), "b_vnorm": mine(r_a[8:9], cs),
        "f_norm": jnp.concatenate([r_a[16:17], r_a[24:25]], axis=0), "final_norm": r_a[32:33],
        "b_ws": r_c[:GROUPS * CHUNK], "b_bs": r_c[GROUPS * CHUNK:],
        "f_conv_w": jnp.concatenate([mine(r_b[0:3], fs), mine(r_b[8:11], fs)], axis=0),
        "f_conv_b": jnp.concatenate([r_b[3:4], r_b[11:12]], axis=0),
        "a_in": g_ai, "a_out": g_ao, "b_in": g_bi, "b_out": g_bo,
    }
    names = ["a_norm", "a_in", "a_conv", "a_out", "b_norm", "b_in", "b_vnorm", "b_ws", "b_bs", "b_out", "f_norm", "f_up",
             "f_conv_w", "f_conv_b", "f_down", "final_norm"]
    weights = dict(zip(names, [a_norm, a_in, a_conv, a_out, b_norm, b_in, b_vnorm, b_ws, b_bs, b_out, f_norm, f_up, f_conv_w,
                               f_conv_b, f_down, final_norm]))
    ms = dict(zip(names, [m_a_norm, m_a_in, m_a_conv, m_a_out, m_b_norm, m_b_in, m_b_vnorm, m_b_ws, m_b_bs, m_b_out, m_f_norm,
                          m_f_up, m_f_conv_w, m_f_conv_b, m_f_down, m_final_norm]))
    vs = dict(zip(names, [v_a_norm, v_a_in, v_a_conv, v_a_out, v_b_norm, v_b_in, v_b_vnorm, v_b_ws, v_b_bs, v_b_out, v_f_norm,
                          v_f_up, v_f_conv_w, v_f_conv_b, v_f_down, v_final_norm]))
    result = {}
    for n in names:
        w = weights[n]
        if n in ("f_up", "f_down"):
            g1, g0 = (g_up1, g_up0) if n == "f_up" else (g_dn1, g_dn0)
            first = _adamw_layer(w, g1, ms[n], vs[n], 1, None, name=f"adamw_{n}1")
            result[n] = _adamw_layer(w, g0, ms[n], vs[n], 0, tuple(first), name=f"adamw_{n}0")
            continue
        if n in early_adamw:
            result[n] = early_adamw[n]
            continue
        g2 = grads[n]
        as3d = (lambda a: a.reshape((1,) + g2.shape))
        result[n] = [o.reshape(w.shape) for o in _adamw_layer(as3d(w), g2, as3d(ms[n]), as3d(vs[n]), 0, None, name=f"adamw_{n}")]

    return (loss, grad_x.reshape(x.shape), *[result[n][0] for n in names], *[result[n][1] for n in names],
            *[result[n][2] for n in names], *[result[n][3] for n in names])
```
